```python
import math
import jax, jax.numpy as jnp
from jax import lax
import numpy as np

D_MODEL = 1024
BATCH = 2
SEQ = 8192
DEPTH = 2
DEC_BATCH = 128
DEC_SEQ = 4
PAST_LEN = 16384
PAGE_SIZE = 128

N_EVEN = (DEPTH + 1) // 2
N_ODD = DEPTH // 2
A_WIDTH = D_MODEL // 2
A_HEADS = 8
A_HEAD_DIM = A_WIDTH // A_HEADS
CHUNK = 128
B_HEADS = 8
B_KV_HEADS = 2
B_HEAD_DIM = 64
B_GROUP = B_HEADS // B_KV_HEADS
WINDOW = 128
N_BUCKETS = 32
MAX_DISTANCE = WINDOW
Q_WIDTH = B_HEADS * B_HEAD_DIM
KV_WIDTH = B_KV_HEADS * B_HEAD_DIM
IN_WIDTH = 2 * A_WIDTH + Q_WIDTH + 2 * KV_WIDTH
MIX_WIDTH = A_WIDTH + Q_WIDTH
ATTN_SCALE = B_HEAD_DIM ** -0.5
NEG_INF = -1e30
POOL_SIZES = (2, 4, 8, 16)
N_POOL_GROUPS = len(POOL_SIZES)
POOL_GROUP_DIM = D_MODEL // N_POOL_GROUPS
POOL_MAX = 16
N_GROUPS = 4
EXPERTS_PER_GROUP = 8
N_EXPERTS = N_GROUPS * EXPERTS_PER_GROUP
TOP_K = 2
D_EXPERT = D_MODEL // 2
MOE_BLOCK = 128
EPS = 1e-6

kernel_name = "hybrid_gmlp_swa_pool_hmoe_step"


def rmsnorm(x, g):
    xf = x.astype(jnp.float32)
    y = xf * lax.rsqrt(jnp.mean(xf * xf, axis=-1, keepdims=True) + EPS)
    return (y * g.astype(jnp.float32)).astype(x.dtype)


def layernorm(x, g, b):
    xf = x.astype(jnp.float32)
    xc = xf - jnp.mean(xf, axis=-1, keepdims=True)
    y = xc * lax.rsqrt(jnp.mean(xc * xc, axis=-1, keepdims=True) + EPS)
    return (y * g.astype(jnp.float32) + b.astype(jnp.float32)).astype(x.dtype)


def t5_bucket(dist):
    n = jnp.maximum(dist, 0)
    max_exact = N_BUCKETS // 2
    nf = jnp.maximum(n, 1).astype(jnp.float32)
    large = max_exact + (jnp.log(nf / max_exact) / math.log(MAX_DISTANCE / max_exact)
                         * (N_BUCKETS - max_exact)).astype(jnp.int32)
    large = jnp.minimum(large, N_BUCKETS - 1)
    return jnp.where(n < max_exact, n, large)


def rel_bias(dist, table):
    b = table.astype(jnp.float32)[t5_bucket(dist)]
    b = jnp.moveaxis(b, -1, 0)
    return b.reshape((B_KV_HEADS, B_GROUP) + dist.shape)


def sink_attention(q, k, v, bias, valid, sinks):
    s = jnp.einsum('...qhgd,...khd->...hgqk', q, k).astype(jnp.float32) * ATTN_SCALE + bias
    s = jnp.where(valid[..., None, None, :, :], s, NEG_INF)
    sink = sinks.astype(jnp.float32).reshape(B_KV_HEADS, B_GROUP, 1, 1)
    m = jnp.maximum(jnp.max(s, axis=-1, keepdims=True), sink)
    p = jnp.exp(s - m)
    p = p / (jnp.sum(p, axis=-1, keepdims=True) + jnp.exp(sink - m))
    return jnp.einsum('...hgqk,...khd->...qhgd', p.astype(v.dtype), v)


def swa_prompt(q, k, v, table, sinks):
    B, S = q.shape[:2]
    nb = S // WINDOW
    qb = q.reshape(B, nb, WINDOW, B_KV_HEADS, B_GROUP, B_HEAD_DIM)

    def with_prev(t):
        tb = t.reshape(B, nb, WINDOW, B_KV_HEADS, B_HEAD_DIM)
        prev = jnp.concatenate([jnp.zeros_like(tb[:, :1]), tb[:, :-1]], axis=1)
        return jnp.concatenate([prev, tb], axis=2)

    kk, vv = with_prev(k), with_prev(v)
    qi = jnp.arange(WINDOW)[:, None]
    ki = jnp.arange(2 * WINDOW)[None, :]
    dist = qi + WINDOW - ki
    kpos = jnp.arange(nb)[:, None, None] * WINDOW + ki[None] - WINDOW
    valid = (dist >= 0) & (dist < WINDOW) & (kpos >= 0)
    o = sink_attention(qb, kk, vv, rel_bias(dist, table), valid, sinks)
    return o.reshape(B, S, Q_WIDTH)


def swa_sample(q, k, v, k_buf, v_buf, table, sinks):
    Bd, T = q.shape[:2]
    L = k_buf.shape[1]
    kk = jnp.concatenate([k_buf.astype(k.dtype), k], axis=1)
    vv = jnp.concatenate([v_buf.astype(v.dtype), v], axis=1)
    qpos = PAST_LEN + jnp.arange(T)
    kpos = PAST_LEN - L + jnp.arange(L + T)
    dist = qpos[:, None] - kpos[None, :]
    valid = (dist >= 0) & (dist < WINDOW) & (kpos[None, :] >= 0)
    o = sink_attention(q, kk, vv, rel_bias(dist, table), valid, sinks)
    return o.reshape(Bd, T, Q_WIDTH), kk[:, -L:], vv[:, -L:]


def split_projection(z, ln_g, ln_b):
    lead = z.shape[:-1]
    u = jax.nn.gelu(z[..., :A_WIDTH])
    va = layernorm(jax.nn.gelu(z[..., A_WIDTH:2 * A_WIDTH]), ln_g, ln_b)
    o = 2 * A_WIDTH
    q = z[..., o:o + Q_WIDTH].reshape(lead + (B_KV_HEADS, B_GROUP, B_HEAD_DIM))
    o += Q_WIDTH
    k = z[..., o:o + KV_WIDTH].reshape(lead + (B_KV_HEADS, B_HEAD_DIM))
    o += KV_WIDTH
    vb = z[..., o:o + KV_WIDTH].reshape(lead + (B_KV_HEADS, B_HEAD_DIM))
    u = u.reshape(lead + (A_HEADS, A_HEAD_DIM))
    va = va.reshape(lead + (A_HEADS, A_HEAD_DIM))
    return u, va, q, k, vb


def causal_spatial_weights(w_s):
    return w_s * jnp.tril(jnp.ones((CHUNK, CHUNK), w_s.dtype))


def spatial_gate_prompt(u, v, w_s, b_s):
    B, S = u.shape[:2]
    nc = S // CHUNK
    vc = v.reshape(B, nc, CHUNK, A_HEADS, A_HEAD_DIM)
    s = jnp.einsum('hts,bcshd->bcthd', causal_spatial_weights(w_s), vc) \
        + jnp.swapaxes(b_s, 0, 1)[:, :, None]
    return (u * s.reshape(u.shape)).reshape(B, S, A_WIDTH), vc[:, -1]


def spatial_gate_sample(u, v, w_s, b_s):
    Bd, T = u.shape[:2]
    w = causal_spatial_weights(w_s)[:, :T, :T]
    s = jnp.einsum('hts,bshd->bthd', w, v) + jnp.swapaxes(b_s[:, :T], 0, 1)[:, :, None]
    return (u * s).reshape(Bd, T, A_WIDTH), v


def pool_mixer(h_ext, n_ctx, pos, w_pool, scale):
    L = h_ext.shape[1]
    hf = h_ext.astype(jnp.float32)
    cum = jnp.cumsum(hf, axis=1)
    cum_pad = jnp.pad(cum, ((0, 0), (POOL_MAX, 0), (0, 0)))
    outs = []
    for g, w in enumerate(POOL_SIZES):
        c = slice(g * POOL_GROUP_DIM, (g + 1) * POOL_GROUP_DIM)
        win_sum = cum[:, n_ctx:, c] - cum_pad[:, POOL_MAX - w + n_ctx:POOL_MAX - w + L, c]
        count = jnp.minimum(pos + 1, w).astype(jnp.float32)[:, None]
        d = win_sum / count - hf[:, n_ctx:, c]
        outs.append(jnp.einsum('btc,ce->bte', d.astype(h_ext.dtype), w_pool[g]))
    return jnp.concatenate(outs, axis=-1) * scale


def routed_experts(h, eidx, gates, w_gate, w_up, w_down):
    T = h.shape[0]
    n_slots = T * TOP_K
    flat_e = eidx.reshape(n_slots).astype(jnp.int32)
    order = jnp.argsort(flat_e)
    sorted_e = flat_e[order]
    tok = (order // TOP_K).astype(jnp.int32)
    counts = jnp.bincount(flat_e, length=N_EXPERTS).astype(jnp.int32)
    padded = (counts + MOE_BLOCK - 1) // MOE_BLOCK * MOE_BLOCK
    pad_end = jnp.cumsum(padded)
    pad_start = pad_end - padded
    start = jnp.cumsum(counts) - counts
    dest = pad_start[sorted_e] + jnp.arange(n_slots, dtype=jnp.int32) - start[sorted_e]
    n_blocks = -(-n_slots // MOE_BLOCK) + N_EXPERTS
    rows = n_blocks * MOE_BLOCK
    row_tok = jnp.full((rows,), T, jnp.int32).at[dest].set(tok)
    block_e = jnp.minimum(
        jnp.searchsorted(pad_end, jnp.arange(n_blocks, dtype=jnp.int32) * MOE_BLOCK, side='right'),
        N_EXPERTS - 1)
    h_pad = jnp.concatenate([h, jnp.zeros((1, h.shape[1]), h.dtype)], axis=0)
    xb = h_pad[row_tok].reshape(n_blocks, MOE_BLOCK, h.shape[1])

    def expert_block(args):
        xblk, e = args
        return (jax.nn.silu(xblk @ w_gate[e]) * (xblk @ w_up[e])) @ w_down[e]

    yb = lax.map(expert_block, (xb, block_e)).reshape(rows, h.shape[1])
    w_sorted = gates.reshape(n_slots)[order].astype(h.dtype)
    return jnp.zeros_like(h).at[tok].add(yb[dest] * w_sorted[:, None])


def hier_moe(h, wg, bg, we, be, w_gate, w_up, w_down):
    ht = h.reshape(-1, D_MODEL)
    glog = (ht @ wg).astype(jnp.float32) + bg.astype(jnp.float32)
    gsel = jnp.argmax(glog, axis=-1)
    g1 = jnp.take_along_axis(jax.nn.softmax(glog, axis=-1), gsel[:, None], axis=-1)
    elog = jnp.einsum('td,gde->tge', ht, we).astype(jnp.float32) + be.astype(jnp.float32)
    elog = jnp.take_along_axis(elog, gsel[:, None, None], axis=1)[:, 0]
    top_v, top_i = lax.top_k(elog, TOP_K)
    gates = g1 * jax.nn.softmax(top_v, axis=-1)
    eidx = gsel[:, None].astype(jnp.int32) * EXPERTS_PER_GROUP + top_i.astype(jnp.int32)
    return routed_experts(ht, eidx, gates, w_gate, w_up, w_down).reshape(h.shape)


def setup_inputs(seed: int = 0) -> dict:
    key = jax.random.key(seed)
    ks = jax.random.split(key, 32)
    f32 = jnp.float32

    def nrm(k, shape, scale):
        return scale * jax.random.normal(k, shape, f32)

    win_buf = min(WINDOW, PAST_LEN)
    pool_buf = min(POOL_MAX - 1, PAST_LEN)
    return {
        "x_prompt": nrm(ks[0], (BATCH, SEQ, D_MODEL), 1.0),
        "x_sample": nrm(ks[1], (DEC_BATCH, DEC_SEQ, D_MODEL), 1.0),
        "cache_k_win": nrm(ks[2], (N_EVEN, DEC_BATCH, win_buf, B_KV_HEADS, B_HEAD_DIM), 1.0),
        "cache_v_win": nrm(ks[3], (N_EVEN, DEC_BATCH, win_buf, B_KV_HEADS, B_HEAD_DIM), 1.0),
        "state_pool": nrm(ks[4], (N_ODD, DEC_BATCH, pool_buf, D_MODEL), 1.0),
        "norm_mix": 1.0 + nrm(ks[5], (DEPTH, D_MODEL), 0.02),
        "norm_ffn": 1.0 + nrm(ks[6], (DEPTH, D_MODEL), 0.02),
        "norm_final": 1.0 + nrm(ks[7], (D_MODEL,), 0.02),
        "w_in": nrm(ks[8], (N_EVEN, D_MODEL, IN_WIDTH), D_MODEL ** -0.5),
        "a_ln_g": 1.0 + nrm(ks[9], (N_EVEN, A_WIDTH), 0.02),
        "a_ln_b": nrm(ks[10], (N_EVEN, A_WIDTH), 0.02),
        "a_w_s": nrm(ks[11], (N_EVEN, A_HEADS, CHUNK, CHUNK), CHUNK ** -0.5),
        "a_b_s": 1.0 + nrm(ks[12], (N_EVEN, A_HEADS, CHUNK), 0.1),
        "b_sinks": nrm(ks[13], (N_EVEN, B_HEADS), 0.5),
        "rel_bias_table": nrm(ks[14], (N_BUCKETS, B_HEADS), 0.5),
        "w_out": nrm(ks[15], (N_EVEN, MIX_WIDTH, D_MODEL), MIX_WIDTH ** -0.5),
        "c_w_pool": nrm(ks[16], (N_ODD, N_POOL_GROUPS, POOL_GROUP_DIM, POOL_GROUP_DIM), POOL_GROUP_DIM ** -0.5),
        "c_scale": 0.5 + nrm(ks[17], (N_ODD, D_MODEL), 0.1),
        "router_group_w": nrm(ks[18], (DEPTH, D_MODEL, N_GROUPS), D_MODEL ** -0.5),
        "router_group_b": nrm(ks[19], (DEPTH, N_GROUPS), 0.01),
        "router_expert_w": nrm(ks[20], (DEPTH, N_GROUPS, D_MODEL, EXPERTS_PER_GROUP), D_MODEL ** -0.5),
        "router_expert_b": nrm(ks[21], (DEPTH, N_GROUPS, EXPERTS_PER_GROUP), 0.01),
        "w_gate": nrm(ks[22], (DEPTH, N_EXPERTS, D_MODEL, D_EXPERT), D_MODEL ** -0.5),
        "w_up": nrm(ks[23], (DEPTH, N_EXPERTS, D_MODEL, D_EXPERT), D_MODEL ** -0.5),
        "w_down": nrm(ks[24], (DEPTH, N_EXPERTS, D_EXPERT, D_MODEL), D_EXPERT ** -0.5),
    }


def reference(x_prompt, x_sample, cache_k_win, cache_v_win, state_pool,
              norm_mix, norm_ffn, norm_final, w_in, a_ln_g, a_ln_b, a_w_s, a_b_s,
              b_sinks, rel_bias_table, w_out, c_w_pool, c_scale,
              router_group_w, router_group_b, router_expert_w, router_expert_b,
              w_gate, w_up, w_down):
    xp, xs = x_prompt, x_sample
    S = xp.shape[1]
    T = xs.shape[1]
    win_k_p, win_v_p, win_k_s, win_v_s = [], [], [], []
    chunk_v_p, chunk_v_s, pool_p, pool_s = [], [], [], []
    for layer in range(DEPTH):
        hp = rmsnorm(xp, norm_mix[layer])
        hs = rmsnorm(xs, norm_mix[layer])
        if layer % 2 == 0:
            i = layer // 2
            up, vap, qp, kp, vbp = split_projection(hp @ w_in[i], a_ln_g[i], a_ln_b[i])
            us, vas, qs, ksm, vbs = split_projection(hs @ w_in[i], a_ln_g[i], a_ln_b[i])
            ap, cvp = spatial_gate_prompt(up, vap, a_w_s[i], a_b_s[i])
            asm, cvs = spatial_gate_sample(us, vas, a_w_s[i], a_b_s[i])
            bp = swa_prompt(qp, kp, vbp, rel_bias_table, b_sinks[i])
            bs, nks, nvs = swa_sample(qs, ksm, vbs, cache_k_win[i], cache_v_win[i],
                                      rel_bias_table, b_sinks[i])
            xp = xp + jnp.concatenate([ap, bp], axis=-1) @ w_out[i]
            xs = xs + jnp.concatenate([asm, bs], axis=-1) @ w_out[i]
            win_k_p.append(kp[:, -WINDOW:])
            win_v_p.append(vbp[:, -WINDOW:])
            win_k_s.append(nks)
            win_v_s.append(nvs)
            chunk_v_p.append(cvp)
            chunk_v_s.append(cvs)
        else:
            j = layer // 2
            xp = xp + pool_mixer(hp, 0, jnp.arange(S), c_w_pool[j], c_scale[j])
            n_ctx = state_pool.shape[2]
            hs_ext = jnp.concatenate([state_pool[j].astype(hs.dtype), hs], axis=1)
            xs = xs + pool_mixer(hs_ext, n_ctx, PAST_LEN + jnp.arange(T), c_w_pool[j], c_scale[j])
            pool_p.append(hp[:, -(POOL_MAX - 1):])
            pool_s.append(hs_ext[:, -n_ctx:])
        moe_args = (router_group_w[layer], router_group_b[layer], router_expert_w[layer],
                    router_expert_b[layer], w_gate[layer], w_up[layer], w_down[layer])
        xp = xp + hier_moe(rmsnorm(xp, norm_ffn[layer]), *moe_args)
        xs = xs + hier_moe(rmsnorm(xs, norm_ffn[layer]), *moe_args)
    y_prompt = rmsnorm(xp, norm_final)
    y_sample = rmsnorm(xs, norm_final)
    return (y_prompt, y_sample,
            jnp.stack(win_k_p), jnp.stack(win_v_p), jnp.stack(win_k_s), jnp.stack(win_v_s),
            jnp.stack(chunk_v_p), jnp.stack(chunk_v_s), jnp.stack(pool_p), jnp.stack(pool_s))
```

```python
import functools
import math

import numpy as np
import jax
import jax.numpy as jnp
from jax import lax
from jax.experimental import pallas as pl
from jax.experimental.pallas import tpu as pltpu

F32 = jnp.float32
BF16 = jnp.bfloat16

D_MODEL = 1024
BATCH = 2
SEQ = 8192
DEC_BATCH = 128
DEC_SEQ = 4
A_WIDTH = 512
A_HEADS = 8
A_HEAD_DIM = 64
CHUNK = 128
B_HEADS = 8
B_KV_HEADS = 2
B_HEAD_DIM = 64
B_GROUP = 4
WINDOW = 128
N_BUCKETS = 32
MAX_DISTANCE = WINDOW
Q_WIDTH = 512
KV_WIDTH = 128
IN_WIDTH = 2 * A_WIDTH + Q_WIDTH + 2 * KV_WIDTH
ATTN_SCALE = B_HEAD_DIM ** -0.5
NEG_INF = -1e30
POOL_SIZES = (2, 4, 8, 16)
POOL_GROUP_DIM = 256
POOL_MAX = 16
N_GROUPS = 4
EXPERTS_PER_GROUP = 8
N_EXPERTS = 32
TOP_K = 2
D_EXPERT = 512
EPS = 1e-6

LANES = 128
ROW_TILE = D_MODEL // LANES
T_PROMPT = BATCH * SEQ
T_SAMPLE = DEC_BATCH * DEC_SEQ
T_ALL = T_PROMPT + T_SAMPLE
TM = 512
N_PROMPT_BLOCKS = T_PROMPT // TM
N_ROW_BLOCKS = T_ALL // TM
STEPS_PER_BATCH = SEQ // TM
SUB = TM // WINDOW
N_SLOTS = T_ALL * TOP_K
MOE_BLK = 256
N_MOE_BLOCKS = N_SLOTS // MOE_BLK + N_EXPERTS
N_SORT_ROWS = N_MOE_BLOCKS * MOE_BLK
SAMPLE_GROUP = 8
N_SAMPLE_GROUPS = DEC_BATCH // SAMPLE_GROUP
VMEM_LIMIT = 56 * 1024 * 1024

STACK_HEADS = ((0, 2, 5, 7), (1, 3, 4, 6))


def _t5_bucket_np(dist):
    n = np.maximum(dist, 0)
    max_exact = N_BUCKETS // 2
    nf = np.maximum(n, 1).astype(np.float32)
    large = max_exact + (np.log(nf / np.float32(max_exact)) / np.float32(math.log(MAX_DISTANCE / max_exact))
                         * np.float32(N_BUCKETS - max_exact)).astype(np.int32)
    large = np.minimum(large, N_BUCKETS - 1)
    return np.where(n < max_exact, n, large).astype(np.int32)


def _bucket_tables():
    qi = np.arange(WINDOW)[:, None]
    ki = np.arange(2 * WINDOW)[None, :]
    dist = qi + WINDOW - ki
    valid = (dist >= 0) & (dist < WINDOW)
    bp = np.where(valid, _t5_bucket_np(dist), -1)
    bp_first = np.where(ki >= WINDOW, bp, -1)
    bkt_p = np.stack([bp_first, bp]).astype(np.int32)

    t = np.repeat(np.arange(DEC_SEQ), SAMPLE_GROUP)[:, None]
    b = np.tile(np.arange(SAMPLE_GROUP), DEC_SEQ)[:, None]
    cb = np.repeat(np.arange(SAMPLE_GROUP), WINDOW)[None, :]
    cj = np.tile(np.arange(WINDOW), SAMPLE_GROUP)[None, :]
    dist_c = t + WINDOW - cj
    valid_c = (cb == b) & (dist_c >= 0) & (dist_c < WINDOW)
    bkt_sc = np.where(valid_c, _t5_bucket_np(dist_c), -1).astype(np.int32)
    nt = np.repeat(np.arange(DEC_SEQ), SAMPLE_GROUP)[None, :]
    nb = np.tile(np.arange(SAMPLE_GROUP), DEC_SEQ)[None, :]
    dist_n = t - nt
    valid_n = (nb == b) & (dist_n >= 0)
    bkt_sn = np.where(valid_n, _t5_bucket_np(dist_n), -1).astype(np.int32)
    bkt_sn = np.concatenate([bkt_sn, np.full((32, LANES - 32), -1, np.int32)], axis=1)
    return bkt_p, bkt_sc, bkt_sn


_BKT_P, _BKT_SC, _BKT_SN = _bucket_tables()


def _cparams(semantics):
    return pltpu.CompilerParams(dimension_semantics=semantics, vmem_limit_bytes=VMEM_LIMIT)


def _rms(x, g):
    return x * lax.rsqrt(jnp.mean(x * x, axis=-1, keepdims=True) + EPS) * g


def _layernorm(x, g, b):
    xc = x - jnp.mean(x, axis=-1, keepdims=True)
    return xc * lax.rsqrt(jnp.mean(xc * xc, axis=-1, keepdims=True) + EPS) * g + b


def _dot(a, b):
    return jnp.dot(a, b, preferred_element_type=F32)


def _dot_nt(a, b):
    return lax.dot_general(a, b, (((1,), (1,)), ((), ())), preferred_element_type=F32)


def _project(x, nm, win, lng, lnb):
    h = _rms(x, nm)
    z = _dot(h.astype(BF16), win)
    u = jax.nn.gelu(z[:, :A_WIDTH])
    va = _layernorm(jax.nn.gelu(z[:, A_WIDTH:2 * A_WIDTH]), lng, lnb)
    q = z[:, 2 * A_WIDTH:2 * A_WIDTH + Q_WIDTH] * ATTN_SCALE
    k = z[:, 2 * A_WIDTH + Q_WIDTH:2 * A_WIDTH + Q_WIDTH + KV_WIDTH]
    v = z[:, 2 * A_WIDTH + Q_WIDTH + KV_WIDTH:]
    return u, va, q, k, v


def _route(x1, nf, wr, br):
    h = _rms(x1, nf)
    logits = jnp.dot(h, wr, preferred_element_type=F32, precision=lax.Precision.HIGHEST) + br
    rows = logits.shape[0]
    lane = lax.broadcasted_iota(jnp.int32, (rows, LANES), 1)
    lanef = lane.astype(F32)
    big = jnp.float32(1e9)
    is_g = lane < N_GROUPS
    gl = jnp.where(is_g, logits, -jnp.inf)
    gmax = jnp.max(gl, axis=1, keepdims=True)
    gsel = jnp.min(jnp.where(gl == gmax, lanef, big), axis=1, keepdims=True)
    gsum = jnp.sum(jnp.where(is_g, jnp.exp(logits - gmax), 0.0), axis=1, keepdims=True)
    g1 = 1.0 / gsum
    lo = N_GROUPS + EXPERTS_PER_GROUP * gsel
    emask = (lanef >= lo) & (lanef < lo + EXPERTS_PER_GROUP)
    el = jnp.where(emask, logits, -jnp.inf)
    v1 = jnp.max(el, axis=1, keepdims=True)
    i1 = jnp.min(jnp.where(el == v1, lanef, big), axis=1, keepdims=True)
    el2 = jnp.where(lanef == i1, -jnp.inf, el)
    v2 = jnp.max(el2, axis=1, keepdims=True)
    i2 = jnp.min(jnp.where(el2 == v2, lanef, big), axis=1, keepdims=True)
    e2 = jnp.exp(v2 - v1)
    den = 1.0 + e2
    w1 = g1 / den
    w2 = g1 * e2 / den
    ids = jnp.where(lane == 0, i1 - N_GROUPS, jnp.where(lane == 1, i2 - N_GROUPS, 0.0)).astype(jnp.int32)
    gates = jnp.where(lane == 0, w1, jnp.where(lane == 1, w2, 0.0))
    return h, ids, gates


def _softmax_pv(s, sink, v_bf16):
    m = jnp.maximum(jnp.max(s, axis=-1, keepdims=True), sink)
    p = jnp.exp(s - m)
    den = jnp.sum(p, axis=-1, keepdims=True) + jnp.exp(sink - m)
    return p, m, den


def _prep_kernel(tab_ref, bp_ref, bsc_ref, bsn_ref, ws_ref, op_ref, osc_ref, osn_ref, ows_ref):
    def fill(bkt, write):
        for h in range(B_HEADS):
            acc = jnp.full(bkt.shape, NEG_INF, F32)
            for b in range(N_BUCKETS):
                acc = jnp.where(bkt == b, tab_ref[b, h], acc)
            write(h, acc)

    for var in range(2):
        def wr_p(h, acc, var=var):
            op_ref[var, h] = acc
        fill(bp_ref[var], wr_p)

    def wr_sc(h, acc):
        osc_ref[h] = acc
    fill(bsc_ref[...], wr_sc)

    def wr_sn(h, acc):
        osn_ref[h] = acc
    fill(bsn_ref[...], wr_sn)

    r = lax.broadcasted_iota(jnp.int32, (CHUNK, CHUNK), 0)
    c = lax.broadcasted_iota(jnp.int32, (CHUNK, CHUNK), 1)
    for h in range(A_HEADS):
        ows_ref[h] = jnp.where(r >= c, ws_ref[h], 0.0).astype(BF16)


def _prep(rel_bias_table, w_s):
    vm = pl.BlockSpec(memory_space=pltpu.VMEM)
    return pl.pallas_call(
        _prep_kernel,
        in_specs=[pl.BlockSpec(memory_space=pltpu.SMEM), vm, vm, vm, vm],
        out_specs=[vm, vm, vm, vm],
        out_shape=[
            jax.ShapeDtypeStruct((2, B_HEADS, WINDOW, 2 * WINDOW), F32),
            jax.ShapeDtypeStruct((B_HEADS, 32, SAMPLE_GROUP * WINDOW), F32),
            jax.ShapeDtypeStruct((B_HEADS, 32, LANES), F32),
            jax.ShapeDtypeStruct((A_HEADS, CHUNK, CHUNK), BF16),
        ],
        name="prep_tables",
    )(rel_bias_table, jnp.asarray(_BKT_P), jnp.asarray(_BKT_SC), jnp.asarray(_BKT_SN), w_s)


def _gate_pairs(va_rows, wsp_ref, lane_lo):
    outs = []
    for p in range(A_HEADS // 2):
        vp = va_rows[:, p * LANES:(p + 1) * LANES]
        rhs = jnp.concatenate([jnp.where(lane_lo, vp, 0.0), jnp.where(lane_lo, 0.0, vp)], axis=0).astype(BF16)
        outs.append(_dot(wsp_ref[p], rhs))
    return jnp.concatenate(outs, axis=1)


def _prompt_steps(body, first_row_out):
    def kern(*refs):
        i = pl.program_id(0)

        @pl.when(i < N_PROMPT_BLOCKS)
        def _():
            body(*refs)

        @pl.when(i >= N_PROMPT_BLOCKS)
        def _():
            for r in refs[first_row_out:first_row_out + 4]:
                r[...] = jnp.zeros(r.shape, r.dtype)

    return kern


def _mix0_prompt_kernel(x_ref, nm_ref, win_ref, lng_ref, lnb_ref, wsp_ref, bs_ref, bias_ref, sink_ref,
                        wout_ref, nf_ref, wr_ref, br_ref,
                        x1_ref, h_ref, ri_ref, rg_ref, kl_ref, vl_ref, val_ref,
                        kprev, vprev, mix_scr):
    x = x_ref[...]
    u, va, q, k, v = _project(x, nm_ref[...], win_ref[...], lng_ref[...], lnb_ref[...])
    lane_lo = lax.broadcasted_iota(jnp.int32, (WINDOW, LANES), 1) < B_HEAD_DIM
    first = pl.program_id(0) % STEPS_PER_BATCH == 0

    @pl.when(first)
    def _():
        kprev[...] = jnp.zeros_like(kprev)
        vprev[...] = jnp.zeros_like(vprev)

    for j in range(SUB):
        rows = slice(j * WINDOW, (j + 1) * WINDOW)
        s_gate = _gate_pairs(va[rows], wsp_ref, lane_lo)
        mix_scr[rows, :A_WIDTH] = u[rows] * (s_gate + bs_ref[...])

        if j == 0:
            kp, vp = kprev[...], vprev[...]
        else:
            prows = slice((j - 1) * WINDOW, j * WINDOW)
            kp, vp = k[prows], v[prows]
        kk = jnp.concatenate([kp, k[rows]], axis=0)
        vv = jnp.concatenate([vp, v[rows]], axis=0)
        kops = (kk.astype(BF16), pltpu.roll(kk, B_HEAD_DIM, 1).astype(BF16))
        vops = (vv.astype(BF16), pltpu.roll(vv, B_HEAD_DIM, 1).astype(BF16))
        qt = [q[rows, p * LANES:(p + 1) * LANES] for p in range(4)]
        q_even = [jnp.where(lane_lo, t, 0.0) for t in qt]
        q_odd = [jnp.where(lane_lo, 0.0, t) for t in qt]
        stacks = (jnp.concatenate([q_even[0], q_even[1], q_odd[2], q_odd[3]], axis=0),
                  jnp.concatenate([q_odd[0], q_odd[1], q_even[2], q_even[3]], axis=0))
        o = []
        for st in range(2):
            s = _dot_nt(stacks[st].astype(BF16), kops[st])
            if j == 0:
                bias = jnp.where(first, bias_ref[0, st], bias_ref[1, st])
            else:
                bias = bias_ref[1, st]
            s = s + bias
            sink = sink_ref[st]
            m = jnp.maximum(jnp.max(s, axis=-1, keepdims=True), sink)
            p = jnp.exp(s - m)
            den = jnp.sum(p, axis=-1, keepdims=True) + jnp.exp(sink - m)
            o.append(_dot(p.astype(BF16), vops[st]) / den)
        oa, ob = o
        sl = [slice(i * WINDOW, (i + 1) * WINDOW) for i in range(4)]
        tiles = (jnp.where(lane_lo, oa[sl[0]], ob[sl[0]]), jnp.where(lane_lo, oa[sl[1]], ob[sl[1]]),
                 jnp.where(lane_lo, ob[sl[2]], oa[sl[2]]), jnp.where(lane_lo, ob[sl[3]], oa[sl[3]]))
        for p in range(4):
            mix_scr[rows, A_WIDTH + p * LANES:A_WIDTH + (p + 1) * LANES] = tiles[p]

    last = slice(TM - WINDOW, TM)
    kprev[...] = k[last]
    vprev[...] = v[last]
    kl_ref[...] = k[last]
    vl_ref[...] = v[last]
    val_ref[...] = va[last]

    x1 = x + _dot(mix_scr[...].astype(BF16), wout_ref[...])
    x1_ref[...] = x1
    h, ids, gates = _route(x1, nf_ref[...], wr_ref[...], br_ref[...])
    h_ref[...] = h.reshape(h_ref.shape)
    ri_ref[...] = ids
    rg_ref[...] = gates


def _const_spec(shape):
    nd = len(shape)
    return pl.BlockSpec(shape, lambda i, _n=nd: (0,) * _n)


def _mix0_prompt(x_all, nm, win, lng, lnb, wsp, bs_full, bias_p, sink_p, wout, nf, wr, br):
    row_spec = pl.BlockSpec((TM, D_MODEL), lambda i: (i, 0))
    row3_spec = pl.BlockSpec((TM, ROW_TILE, LANES), lambda i: (i, 0, 0))
    lane_spec = pl.BlockSpec((TM, LANES), lambda i: (i, 0))
    last_kv = pl.BlockSpec((None, WINDOW, KV_WIDTH), lambda i: (jnp.minimum(i // STEPS_PER_BATCH, BATCH - 1), 0, 0))
    last_va = pl.BlockSpec((None, WINDOW, A_WIDTH), lambda i: (jnp.minimum(i // STEPS_PER_BATCH, BATCH - 1), 0, 0))
    return pl.pallas_call(
        _prompt_steps(_mix0_prompt_kernel, 13),
        grid=(N_ROW_BLOCKS,),
        in_specs=[row_spec, _const_spec((1, D_MODEL)), _const_spec((D_MODEL, IN_WIDTH)),
                  _const_spec((1, A_WIDTH)), _const_spec((1, A_WIDTH)),
                  _const_spec((A_HEADS // 2, CHUNK, 2 * CHUNK)), _const_spec((CHUNK, A_WIDTH)),
                  _const_spec((2, 2, 4 * WINDOW, 2 * WINDOW)), _const_spec((2, 4 * WINDOW, 1)),
                  _const_spec((A_WIDTH + Q_WIDTH, D_MODEL)), _const_spec((1, D_MODEL)),
                  _const_spec((D_MODEL, LANES)), _const_spec((1, LANES))],
        out_specs=[row_spec, row3_spec, lane_spec, lane_spec, last_kv, last_kv, last_va],
        out_shape=[jax.ShapeDtypeStruct((T_ALL, D_MODEL), F32), jax.ShapeDtypeStruct((T_ALL, ROW_TILE, LANES), F32),
                   jax.ShapeDtypeStruct((T_ALL, LANES), jnp.int32), jax.ShapeDtypeStruct((T_ALL, LANES), F32),
                   jax.ShapeDtypeStruct((BATCH, WINDOW, KV_WIDTH), F32),
                   jax.ShapeDtypeStruct((BATCH, WINDOW, KV_WIDTH), F32),
                   jax.ShapeDtypeStruct((BATCH, WINDOW, A_WIDTH), F32)],
        scratch_shapes=[pltpu.VMEM((WINDOW, KV_WIDTH), F32), pltpu.VMEM((WINDOW, KV_WIDTH), F32),
                        pltpu.VMEM((TM, D_MODEL), F32)],
        compiler_params=_cparams(("arbitrary",)),
        name="mix0_prompt",
    )(x_all, nm, win, lng, lnb, wsp, bs_full, bias_p, sink_p, wout, nf, wr, br)


def _mix0_sample_kernel(x_ref, nm_ref, win_ref, lng_ref, lnb_ref, wcoef_ref, bcoef_ref,
                        ck_ref, cv_ref, bsc_ref, bsn_ref, sink_ref,
                        wout_ref, nf_ref, wr_ref, br_ref,
                        x1_in, h_in, ri_in, rg_in,
                        x1_ref, h_ref, ri_ref, rg_ref, kn_ref, vn_ref, va_ref,
                        q_scr, k_scr, v_scr, mix_scr):
    del x1_in, h_in, ri_in, rg_in
    g = pl.program_id(0)

    @pl.when(g == 0)
    def _():
        u, va, q, k, v = _project(x_ref[...], nm_ref[...], win_ref[...], lng_ref[...], lnb_ref[...])
        q_scr[...] = q
        k_scr[...] = k
        v_scr[...] = v
        kn_ref[...] = k
        vn_ref[...] = v
        va_ref[...] = va
        idx = 0
        for t in range(DEC_SEQ):
            acc = jnp.zeros((DEC_BATCH, A_WIDTH), F32) + bcoef_ref[t:t + 1, :]
            for s in range(t + 1):
                acc = acc + wcoef_ref[idx:idx + 1, :] * va[s * DEC_BATCH:(s + 1) * DEC_BATCH]
                idx += 1
            mix_scr[t * DEC_BATCH:(t + 1) * DEC_BATCH, :A_WIDTH] = u[t * DEC_BATCH:(t + 1) * DEC_BATCH] * acc

    b0 = pl.multiple_of(g * SAMPLE_GROUP, SAMPLE_GROUP)
    lane_lo = lax.broadcasted_iota(jnp.int32, (DEC_SEQ * SAMPLE_GROUP, LANES), 1) < B_HEAD_DIM

    def grab(ref, width):
        return jnp.concatenate([ref[pl.ds(t * DEC_BATCH + b0, SAMPLE_GROUP), :] for t in range(DEC_SEQ)], axis=0)

    qg = grab(q_scr, Q_WIDTH)
    kn = grab(k_scr, KV_WIDTH)
    vn = grab(v_scr, KV_WIDTH)
    kc = ck_ref[...]
    vc = cv_ref[...]
    kc_ops = (kc.astype(BF16), pltpu.roll(kc, B_HEAD_DIM, 1).astype(BF16))
    vc_ops = (vc.astype(BF16), pltpu.roll(vc, B_HEAD_DIM, 1).astype(BF16))
    kn_ops = (kn.astype(BF16), pltpu.roll(kn, B_HEAD_DIM, 1).astype(BF16))
    vn_ops = (vn.astype(BF16), pltpu.roll(vn, B_HEAD_DIM, 1).astype(BF16))
    qt = [qg[:, p * LANES:(p + 1) * LANES] for p in range(4)]
    q_even = [jnp.where(lane_lo, t, 0.0) for t in qt]
    q_odd = [jnp.where(lane_lo, 0.0, t) for t in qt]
    stacks = (jnp.concatenate([q_even[0], q_even[1], q_odd[2], q_odd[3]], axis=0),
              jnp.concatenate([q_odd[0], q_odd[1], q_even[2], q_even[3]], axis=0))
    o = []
    for st in range(2):
        qs = stacks[st].astype(BF16)
        sc = _dot_nt(qs, kc_ops[st]) + bsc_ref[st]
        sn = _dot_nt(qs, kn_ops[st]) + bsn_ref[st][:, :DEC_SEQ * SAMPLE_GROUP]
        sink = sink_ref[st]
        m = jnp.maximum(jnp.maximum(jnp.max(sc, axis=-1, keepdims=True), jnp.max(sn, axis=-1, keepdims=True)), sink)
        pc = jnp.exp(sc - m)
        pn = jnp.exp(sn - m)
        den = jnp.sum(pc, axis=-1, keepdims=True) + jnp.sum(pn, axis=-1, keepdims=True) + jnp.exp(sink - m)
        o.append((_dot(pc.astype(BF16), vc_ops[st]) + _dot(pn.astype(BF16), vn_ops[st])) / den)
    oa, ob = o
    n = DEC_SEQ * SAMPLE_GROUP
    sl = [slice(i * n, (i + 1) * n) for i in range(4)]
    tiles = (jnp.where(lane_lo, oa[sl[0]], ob[sl[0]]), jnp.where(lane_lo, oa[sl[1]], ob[sl[1]]),
             jnp.where(lane_lo, ob[sl[2]], oa[sl[2]]), jnp.where(lane_lo, ob[sl[3]], oa[sl[3]]))
    for p in range(4):
        for t in range(DEC_SEQ):
            mix_scr[pl.ds(t * DEC_BATCH + b0, SAMPLE_GROUP), A_WIDTH + p * LANES:A_WIDTH + (p + 1) * LANES] = (
                tiles[p][t * SAMPLE_GROUP:(t + 1) * SAMPLE_GROUP])

    @pl.when(g == N_SAMPLE_GROUPS - 1)
    def _():
        x1 = x_ref[...] + _dot(mix_scr[...].astype(BF16), wout_ref[...])
        x1_ref[...] = x1
        h, ids, gates = _route(x1, nf_ref[...], wr_ref[...], br_ref[...])
        h_ref[...] = h.reshape(h_ref.shape)
        ri_ref[...] = ids
        rg_ref[...] = gates


def _mix0_sample(x_all, nm, win, lng, lnb, wcoef, bcoef, ck, cv, bias_sc, bias_sn, sink_s, wout, nf, wr, br,
                 x1_all, h_all, ri_all, rg_all):
    sample_rows = pl.BlockSpec((TM, D_MODEL), lambda g: (N_PROMPT_BLOCKS, 0))
    sample_rows3 = pl.BlockSpec((TM, ROW_TILE, LANES), lambda g: (N_PROMPT_BLOCKS, 0, 0))
    sample_lanes = pl.BlockSpec((TM, LANES), lambda g: (N_PROMPT_BLOCKS, 0))
    cache_spec = pl.BlockSpec((SAMPLE_GROUP * WINDOW, KV_WIDTH), lambda g: (g, 0))
    anyspec = pl.BlockSpec(memory_space=pl.ANY)
    n_in = 16
    return pl.pallas_call(
        _mix0_sample_kernel,
        grid=(N_SAMPLE_GROUPS,),
        in_specs=[sample_rows, _const_spec((1, D_MODEL)), _const_spec((D_MODEL, IN_WIDTH)),
                  _const_spec((1, A_WIDTH)), _const_spec((1, A_WIDTH)),
                  _const_spec((16, A_WIDTH)), _const_spec((8, A_WIDTH)),
                  cache_spec, cache_spec,
                  _const_spec((2, 4 * 32, SAMPLE_GROUP * WINDOW)), _const_spec((2, 4 * 32, LANES)),
                  _const_spec((2, 4 * 32, 1)),
                  _const_spec((A_WIDTH + Q_WIDTH, D_MODEL)), _const_spec((1, D_MODEL)),
                  _const_spec((D_MODEL, LANES)), _const_spec((1, LANES)),
                  anyspec, anyspec, anyspec, anyspec],
        out_specs=[sample_rows, sample_rows3, sample_lanes, sample_lanes,
                   _const_spec((T_SAMPLE, KV_WIDTH)), _const_spec((T_SAMPLE, KV_WIDTH)),
                   _const_spec((T_SAMPLE, A_WIDTH))],
        out_shape=[jax.ShapeDtypeStruct((T_ALL, D_MODEL), F32), jax.ShapeDtypeStruct((T_ALL, ROW_TILE, LANES), F32),
                   jax.ShapeDtypeStruct((T_ALL, LANES), jnp.int32), jax.ShapeDtypeStruct((T_ALL, LANES), F32),
                   jax.ShapeDtypeStruct((T_SAMPLE, KV_WIDTH), F32), jax.ShapeDtypeStruct((T_SAMPLE, KV_WIDTH), F32),
                   jax.ShapeDtypeStruct((T_SAMPLE, A_WIDTH), F32)],
        scratch_shapes=[pltpu.VMEM((T_SAMPLE, Q_WIDTH), F32), pltpu.VMEM((T_SAMPLE, KV_WIDTH), F32),
                        pltpu.VMEM((T_SAMPLE, KV_WIDTH), F32), pltpu.VMEM((T_SAMPLE, D_MODEL), F32)],
        input_output_aliases={n_in: 0, n_in + 1: 1, n_in + 2: 2, n_in + 3: 3},
        compiler_params=_cparams(("arbitrary",)),
        name="mix0_sample",
    )(x_all, nm, win, lng, lnb, wcoef, bcoef, ck, cv, bias_sc, bias_sn, sink_s, wout, nf, wr, br,
      x1_all, h_all, ri_all, rg_all)


def _moe_metadata(ri_all):
    flat = ri_all[:, :TOP_K].reshape(N_SLOTS)
    onehot = (flat[:, None] == jnp.arange(N_EXPERTS, dtype=jnp.int32)[None, :]).astype(jnp.int32)
    csum = jnp.cumsum(onehot, axis=0)
    rank = jnp.sum(onehot * (csum - 1), axis=1)
    counts = csum[-1]
    padded = (counts + MOE_BLK - 1) // MOE_BLK * MOE_BLK
    pad_end = jnp.cumsum(padded)
    pad_start = pad_end - padded
    dest = (jnp.sum(onehot * pad_start[None, :], axis=1) + rank).astype(jnp.int32)
    n_valid = (pad_end[-1] // MOE_BLK).astype(jnp.int32).reshape(1)
    blk_start = jnp.arange(N_MOE_BLOCKS, dtype=jnp.int32) * MOE_BLK
    block_e = jnp.minimum(jnp.sum((blk_start[:, None] >= pad_end[None, :]).astype(jnp.int32), axis=1),
                          N_EXPERTS - 1).astype(jnp.int32)
    zero_start = (pad_start + counts).astype(jnp.int32)
    zero_len = (padded - counts).astype(jnp.int32)
    return dest, block_e, n_valid, jnp.concatenate([zero_start, zero_len, n_valid])


def _dispatch_kernel(dest_ref, zs_ref, h_ref, xs_ref, zero_scr, sem, zsem):
    i = pl.program_id(0)

    @pl.when(i == 0)
    def _():
        zero_scr[...] = jnp.zeros_like(zero_scr)

        def pieces(e, do):
            off = zs_ref[e]
            rem = zs_ref[N_EXPERTS + e]
            bit = MOE_BLK // 2
            while bit >= 1:
                take = (rem & bit) != 0

                @pl.when(take)
                def _(off=off, bit=bit):
                    do(pltpu.make_async_copy(zero_scr.at[pl.ds(0, bit)], xs_ref.at[pl.ds(off, bit)], zsem))

                off = off + jnp.where(take, bit, 0)
                bit //= 2

        def start_e(e, c):
            pieces(e, lambda cp: cp.start())
            return c

        def wait_e(e, c):
            pieces(e, lambda cp: cp.wait())
            return c

        def tail(do):
            def step(b, c):
                do(pltpu.make_async_copy(zero_scr, xs_ref.at[pl.ds(b * MOE_BLK, MOE_BLK)], zsem))
                return c
            return step

        n_valid = zs_ref[2 * N_EXPERTS]
        lax.fori_loop(0, N_EXPERTS, start_e, 0)
        lax.fori_loop(n_valid, N_MOE_BLOCKS, tail(lambda cp: cp.start()), 0)
        lax.fori_loop(0, N_EXPERTS, wait_e, 0)
        lax.fori_loop(n_valid, N_MOE_BLOCKS, tail(lambda cp: cp.wait()), 0)

    base = i * (TM * TOP_K)

    def body(r, carry):
        for kk in range(TOP_K):
            d = dest_ref[base + r * TOP_K + kk]
            pltpu.make_async_copy(h_ref.at[r], xs_ref.at[d], sem).start()
        return carry

    lax.fori_loop(0, TM, body, 0)
    for kk in range(TOP_K):
        pltpu.make_async_copy(h_ref, xs_ref.at[pl.ds(0, TM)], sem).wait()


def _dispatch(dest, zero_start, h_all):
    return pl.pallas_call(
        _dispatch_kernel,
        grid_spec=pltpu.PrefetchScalarGridSpec(
            num_scalar_prefetch=2,
            grid=(N_ROW_BLOCKS,),
            in_specs=[pl.BlockSpec((TM, ROW_TILE, LANES), lambda i, d, z: (i, 0, 0))],
            out_specs=pl.BlockSpec(memory_space=pl.ANY),
            scratch_shapes=[pltpu.VMEM((MOE_BLK, ROW_TILE, LANES), F32), pltpu.SemaphoreType.DMA(()),
                            pltpu.SemaphoreType.DMA(())],
        ),
        out_shape=jax.ShapeDtypeStruct((N_SORT_ROWS, ROW_TILE, LANES), F32),
        compiler_params=_cparams(("arbitrary",)),
        name="moe_dispatch",
    )(dest, zero_start, h_all)


def _experts_kernel(be_ref, nv_ref, x_ref, wg_ref, wu_ref, wd_ref, y_ref, wg_s, wu_s, wd_s):
    i = pl.program_id(0)

    @pl.when(i < nv_ref[0])
    def _():
        e = be_ref[i]
        prev = be_ref[jnp.maximum(i - 1, 0)]

        @pl.when((i == 0) | (e != prev))
        def _():
            wg_s[...] = wg_ref[...].astype(BF16)
            wu_s[...] = wu_ref[...].astype(BF16)
            wd_s[...] = wd_ref[...].astype(BF16)

        xb = x_ref[...].reshape(MOE_BLK, D_MODEL).astype(BF16)
        a = jax.nn.silu(_dot(xb, wg_s[...])) * _dot(xb, wu_s[...])
        y_ref[...] = _dot(a.astype(BF16), wd_s[...]).reshape(y_ref.shape)

    @pl.when(i >= nv_ref[0])
    def _():
        y_ref[...] = jnp.zeros(y_ref.shape, y_ref.dtype)


def _experts(block_e, n_valid, xs, w_gate, w_up, w_down):
    def blk(i, be, nv):
        return jnp.minimum(i, nv[0] - 1)

    return pl.pallas_call(
        _experts_kernel,
        grid_spec=pltpu.PrefetchScalarGridSpec(
            num_scalar_prefetch=2,
            grid=(N_MOE_BLOCKS,),
            in_specs=[pl.BlockSpec((MOE_BLK, ROW_TILE, LANES), lambda i, be, nv: (blk(i, be, nv), 0, 0)),
                      pl.BlockSpec((None, D_MODEL, D_EXPERT), lambda i, be, nv: (be[blk(i, be, nv)], 0, 0)),
                      pl.BlockSpec((None, D_MODEL, D_EXPERT), lambda i, be, nv: (be[blk(i, be, nv)], 0, 0)),
                      pl.BlockSpec((None, D_EXPERT, D_MODEL), lambda i, be, nv: (be[blk(i, be, nv)], 0, 0))],
            out_specs=pl.BlockSpec((MOE_BLK, ROW_TILE, LANES), lambda i, be, nv: (i, 0, 0)),
            scratch_shapes=[pltpu.VMEM((D_MODEL, D_EXPERT), BF16), pltpu.VMEM((D_MODEL, D_EXPERT), BF16),
                            pltpu.VMEM((D_EXPERT, D_MODEL), BF16)],
        ),
        out_shape=jax.ShapeDtypeStruct((N_SORT_ROWS, ROW_TILE, LANES), F32),
        compiler_params=_cparams(("arbitrary",)),
        name="moe_experts",
    )(block_e, n_valid, xs, w_gate, w_up, w_down)


def _gather_rows(dest_ref, ys_ref, ybuf, sem, i):
    base = i * (TM * TOP_K)

    def body(r, carry):
        for kk in range(TOP_K):
            d = dest_ref[base + r * TOP_K + kk]
            pltpu.make_async_copy(ys_ref.at[d], ybuf.at[kk, r], sem).start()
        return carry

    lax.fori_loop(0, TM, body, 0)
    for kk in range(TOP_K):
        pltpu.make_async_copy(ys_ref.at[pl.ds(0, TM)], ybuf.at[kk], sem).wait()


def _combined(x_ref, rg_ref, ybuf):
    rg = rg_ref[...]
    y0 = ybuf[0].reshape(TM, D_MODEL)
    y1 = ybuf[1].reshape(TM, D_MODEL)
    return x_ref[...] + rg[:, 0:1] * y0 + rg[:, 1:2] * y1


def _combine_kernel(dest_ref, x_ref, rg_ref, ys_ref, o_ref, ybuf, sem):
    _gather_rows(dest_ref, ys_ref, ybuf, sem, pl.program_id(0))
    o_ref[...] = _combined(x_ref, rg_ref, ybuf)


def _combine(dest, x_all, rg_all, ys):
    return pl.pallas_call(
        _combine_kernel,
        grid_spec=pltpu.PrefetchScalarGridSpec(
            num_scalar_prefetch=1,
            grid=(N_ROW_BLOCKS,),
            in_specs=[pl.BlockSpec((TM, D_MODEL), lambda i, d: (i, 0)),
                      pl.BlockSpec((TM, LANES), lambda i, d: (i, 0)),
                      pl.BlockSpec(memory_space=pl.ANY)],
            out_specs=pl.BlockSpec((TM, D_MODEL), lambda i, d: (i, 0)),
            scratch_shapes=[pltpu.VMEM((TOP_K, TM, ROW_TILE, LANES), F32), pltpu.SemaphoreType.DMA(())],
        ),
        out_shape=jax.ShapeDtypeStruct((T_ALL, D_MODEL), F32),
        compiler_params=_cparams(("arbitrary",)),
        name="moe_combine",
    )(dest, x_all, rg_all, ys)


def _final_kernel(dest_ref, x_ref, rg_ref, ys_ref, nfin_ref, o_ref, ybuf, sem):
    _gather_rows(dest_ref, ys_ref, ybuf, sem, pl.program_id(0))
    o_ref[...] = _rms(_combined(x_ref, rg_ref, ybuf), nfin_ref[...])


def _final(dest, x_all, rg_all, ys, nfin):
    return pl.pallas_call(
        _final_kernel,
        grid_spec=pltpu.PrefetchScalarGridSpec(
            num_scalar_prefetch=1,
            grid=(N_ROW_BLOCKS,),
            in_specs=[pl.BlockSpec((TM, D_MODEL), lambda i, d: (i, 0)),
                      pl.BlockSpec((TM, LANES), lambda i, d: (i, 0)),
                      pl.BlockSpec(memory_space=pl.ANY),
                      pl.BlockSpec((1, D_MODEL), lambda i, d: (0, 0))],
            out_specs=pl.BlockSpec((TM, D_MODEL), lambda i, d: (i, 0)),
            scratch_shapes=[pltpu.VMEM((TOP_K, TM, ROW_TILE, LANES), F32), pltpu.SemaphoreType.DMA(())],
        ),
        out_shape=jax.ShapeDtypeStruct((T_ALL, D_MODEL), F32),
        compiler_params=_cparams(("arbitrary",)),
        name="moe_combine_final",
    )(dest, x_all, rg_all, ys, nfin)


def _moe(h_all, ri_all, w_gate, w_up, w_down):
    dest, block_e, n_valid, zero_start = _moe_metadata(ri_all)
    xs = _dispatch(dest, zero_start, h_all)
    ys = _experts(block_e, n_valid, xs, w_gate, w_up, w_down)
    return dest, ys


def _pool_project(d_groups, wp_ref, scale):
    outs = [_dot(d_groups[g].astype(BF16), wp_ref[g]) for g in range(len(POOL_SIZES))]
    return jnp.concatenate(outs, axis=1) * scale


def _mix1_prompt_kernel(x_ref, nm_ref, wp_ref, sc_ref, nf_ref, wr_ref, br_ref,
                        x3_ref, h_ref, ri_ref, rg_ref, pl_ref, ext):
    i = pl.program_id(0)
    x = x_ref[...]
    hp = _rms(x, nm_ref[...])

    @pl.when(i % STEPS_PER_BATCH == 0)
    def _():
        ext[0:POOL_MAX, :] = jnp.zeros((POOL_MAX, D_MODEL), F32)

    ext[POOL_MAX:, :] = hp
    pos = (i % STEPS_PER_BATCH) * TM + lax.broadcasted_iota(jnp.int32, (TM, 1), 0)
    d_groups = []
    for g, w in enumerate(POOL_SIZES):
        cols = slice(g * POOL_GROUP_DIM, (g + 1) * POOL_GROUP_DIM)
        acc = ext[:, cols]
        span = 1
        while span < w:
            acc = acc + pltpu.roll(acc, span, 0)
            span *= 2
        cnt = jnp.minimum(pos + 1, w).astype(F32)
        d_groups.append(acc[POOL_MAX:] / cnt - hp[:, cols])
    tail = hp[TM - POOL_MAX:, :]
    ext[0:POOL_MAX, :] = tail
    pl_ref[...] = tail

    x3 = x + _pool_project(d_groups, wp_ref, sc_ref[...])
    x3_ref[...] = x3
    h, ids, gates = _route(x3, nf_ref[...], wr_ref[...], br_ref[...])
    h_ref[...] = h.reshape(h_ref.shape)
    ri_ref[...] = ids
    rg_ref[...] = gates


def _mix1_prompt(x_all, nm, wp, sc, nf, wr, br):
    row_spec = pl.BlockSpec((TM, D_MODEL), lambda i: (i, 0))
    row3_spec = pl.BlockSpec((TM, ROW_TILE, LANES), lambda i: (i, 0, 0))
    lane_spec = pl.BlockSpec((TM, LANES), lambda i: (i, 0))
    return pl.pallas_call(
        _prompt_steps(_mix1_prompt_kernel, 7),
        grid=(N_ROW_BLOCKS,),
        in_specs=[row_spec, _const_spec((1, D_MODEL)),
                  _const_spec((len(POOL_SIZES), POOL_GROUP_DIM, POOL_GROUP_DIM)), _const_spec((1, D_MODEL)),
                  _const_spec((1, D_MODEL)), _const_spec((D_MODEL, LANES)), _const_spec((1, LANES))],
        out_specs=[row_spec, row3_spec, lane_spec, lane_spec,
                   pl.BlockSpec((None, POOL_MAX, D_MODEL),
                                lambda i: (jnp.minimum(i // STEPS_PER_BATCH, BATCH - 1), 0, 0))],
        out_shape=[jax.ShapeDtypeStruct((T_ALL, D_MODEL), F32), jax.ShapeDtypeStruct((T_ALL, ROW_TILE, LANES), F32),
                   jax.ShapeDtypeStruct((T_ALL, LANES), jnp.int32), jax.ShapeDtypeStruct((T_ALL, LANES), F32),
                   jax.ShapeDtypeStruct((BATCH, POOL_MAX, D_MODEL), F32)],
        scratch_shapes=[pltpu.VMEM((POOL_MAX + TM, D_MODEL), F32)],
        compiler_params=_cparams(("arbitrary",)),
        name="mix1_prompt",
    )(x_all, nm, wp, sc, nf, wr, br)


def _mix1_sample_kernel(x_ref, st_ref, nm_ref, wp_ref, sc_ref, nf_ref, wr_ref, br_ref,
                        x3_in, h_in, ri_in, rg_in,
                        x3_ref, h_ref, ri_ref, rg_ref, hs_ref):
    del x3_in, h_in, ri_in, rg_in
    x = x_ref[...]
    hs = _rms(x, nm_ref[...])
    hs_ref[...] = hs
    n_ctx = POOL_MAX - 1
    d_groups = []
    for g, w in enumerate(POOL_SIZES):
        cols = slice(g * POOL_GROUP_DIM, (g + 1) * POOL_GROUP_DIM)
        parts = []
        for t in range(DEC_SEQ):
            acc = hs[t * DEC_BATCH:(t + 1) * DEC_BATCH, cols]
            for back in range(1, w):
                src = t - back
                if src >= 0:
                    acc = acc + hs[src * DEC_BATCH:(src + 1) * DEC_BATCH, cols]
                else:
                    acc = acc + st_ref[n_ctx + src, :, cols]
            parts.append(acc / float(w) - hs[t * DEC_BATCH:(t + 1) * DEC_BATCH, cols])
        d_groups.append(jnp.concatenate(parts, axis=0))
    x3 = x + _pool_project(d_groups, wp_ref, sc_ref[...])
    x3_ref[...] = x3
    h, ids, gates = _route(x3, nf_ref[...], wr_ref[...], br_ref[...])
    h_ref[...] = h.reshape(h_ref.shape)
    ri_ref[...] = ids
    rg_ref[...] = gates


def _mix1_sample(x_all, state_t, nm, wp, sc, nf, wr, br, x3_all, h_all, ri_all, rg_all):
    sample_rows = pl.BlockSpec((TM, D_MODEL), lambda g: (N_PROMPT_BLOCKS, 0))
    sample_rows3 = pl.BlockSpec((TM, ROW_TILE, LANES), lambda g: (N_PROMPT_BLOCKS, 0, 0))
    sample_lanes = pl.BlockSpec((TM, LANES), lambda g: (N_PROMPT_BLOCKS, 0))
    anyspec = pl.BlockSpec(memory_space=pl.ANY)
    n_in = 8
    return pl.pallas_call(
        _mix1_sample_kernel,
        grid=(1,),
        in_specs=[sample_rows, _const_spec((POOL_MAX - 1, DEC_BATCH, D_MODEL)), _const_spec((1, D_MODEL)),
                  _const_spec((len(POOL_SIZES), POOL_GROUP_DIM, POOL_GROUP_DIM)), _const_spec((1, D_MODEL)),
                  _const_spec((1, D_MODEL)), _const_spec((D_MODEL, LANES)), _const_spec((1, LANES)),
                  anyspec, anyspec, anyspec, anyspec],
        out_specs=[sample_rows, sample_rows3, sample_lanes, sample_lanes, _const_spec((T_SAMPLE, D_MODEL))],
        out_shape=[jax.ShapeDtypeStruct((T_ALL, D_MODEL), F32), jax.ShapeDtypeStruct((T_ALL, ROW_TILE, LANES), F32),
                   jax.ShapeDtypeStruct((T_ALL, LANES), jnp.int32), jax.ShapeDtypeStruct((T_ALL, LANES), F32),
                   jax.ShapeDtypeStruct((T_SAMPLE, D_MODEL), F32)],
        input_output_aliases={n_in: 0, n_in + 1: 1, n_in + 2: 2, n_in + 3: 3},
        compiler_params=_cparams(("arbitrary",)),
        name="mix1_sample",
    )(x_all, state_t, nm, wp, sc, nf, wr, br, x3_all, h_all, ri_all, rg_all)


def _router_weights(wg, bg, we, be):
    w = jnp.concatenate([wg, jnp.transpose(we, (1, 0, 2)).reshape(D_MODEL, N_EXPERTS)], axis=1)
    b = jnp.concatenate([bg, be.reshape(N_EXPERTS)])
    pad = LANES - N_GROUPS - N_EXPERTS
    return jnp.pad(w, ((0, 0), (0, pad))), jnp.pad(b, (0, pad)).reshape(1, LANES)


def _stack(tab):
    return jnp.stack([jnp.concatenate([tab[h] for h in heads], axis=0) for heads in STACK_HEADS])


def kernel(x_prompt, x_sample, cache_k_win, cache_v_win, state_pool, norm_mix, norm_ffn, norm_final, w_in,
           a_ln_g, a_ln_b, a_w_s, a_b_s, b_sinks, rel_bias_table, w_out, c_w_pool, c_scale,
           router_group_w, router_group_b, router_expert_w, router_expert_b, w_gate, w_up, w_down):
    xs_t = jnp.transpose(x_sample, (1, 0, 2)).reshape(T_SAMPLE, D_MODEL)
    x_all = jnp.concatenate([x_prompt.reshape(T_PROMPT, D_MODEL), xs_t], axis=0)
    win = w_in[0].astype(BF16)
    wout = w_out[0].astype(BF16)
    lng = a_ln_g[0].reshape(1, A_WIDTH)
    lnb = a_ln_b[0].reshape(1, A_WIDTH)
    bias_p, bias_sc, bias_sn, ws_tril = _prep(rel_bias_table, a_w_s[0])
    wsp = ws_tril.reshape(A_HEADS // 2, 2, CHUNK, CHUNK).transpose(0, 2, 1, 3).reshape(A_HEADS // 2, CHUNK, 2 * CHUNK)
    bs_full = jnp.repeat(a_b_s[0].T, A_HEAD_DIM, axis=1)
    bias_p = jnp.stack([_stack(bias_p[0]), _stack(bias_p[1])])
    bias_sc = _stack(bias_sc)
    bias_sn = _stack(bias_sn)
    sinks = b_sinks[0]
    sink_p = jnp.stack([jnp.repeat(sinks[jnp.array(hh)], WINDOW) for hh in STACK_HEADS]).reshape(2, 4 * WINDOW, 1)
    sink_s = jnp.stack([jnp.repeat(sinks[jnp.array(hh)], 32) for hh in STACK_HEADS]).reshape(2, 4 * 32, 1)
    pairs = [(t, s) for t in range(DEC_SEQ) for s in range(t + 1)]
    wcoef = jnp.stack([jnp.repeat(a_w_s[0][:, t, s], A_HEAD_DIM) for t, s in pairs])
    wcoef = jnp.pad(wcoef, ((0, 16 - len(pairs)), (0, 0)))
    bcoef = jnp.pad(jnp.repeat(a_b_s[0][:, :DEC_SEQ].T, A_HEAD_DIM, axis=1), ((0, 8 - DEC_SEQ), (0, 0)))
    ck = cache_k_win[0].reshape(DEC_BATCH * WINDOW, KV_WIDTH)
    cv = cache_v_win[0].reshape(DEC_BATCH * WINDOW, KV_WIDTH)
    routers = [_router_weights(router_group_w[l], router_group_b[l], router_expert_w[l], router_expert_b[l])
               for l in range(2)]
    nm = [norm_mix[l].reshape(1, D_MODEL) for l in range(2)]
    nf = [norm_ffn[l].reshape(1, D_MODEL) for l in range(2)]

    x1_all, h_all, ri_all, rg_all, k_last, v_last, va_last = _mix0_prompt(
        x_all, nm[0], win, lng, lnb, wsp, bs_full, bias_p, sink_p, wout, nf[0], *routers[0])
    x1_all, h_all, ri_all, rg_all, k_new, v_new, va_s = _mix0_sample(
        x_all, nm[0], win, lng, lnb, wcoef, bcoef, ck, cv, bias_sc, bias_sn, sink_s, wout, nf[0], *routers[0],
        x1_all, h_all, ri_all, rg_all)
    dest0, ys0 = _moe(h_all, ri_all, w_gate[0], w_up[0], w_down[0])
    x2_all = _combine(dest0, x1_all, rg_all, ys0)

    wp = c_w_pool[0].astype(BF16)
    sc = c_scale[0].reshape(1, D_MODEL)
    x3_all, h2_all, ri2_all, rg2_all, pool_tail = _mix1_prompt(x2_all, nm[1], wp, sc, nf[1], *routers[1])
    state_t = jnp.transpose(state_pool[0], (1, 0, 2))
    x3_all, h2_all, ri2_all, rg2_all, hs1 = _mix1_sample(
        x2_all, state_t, nm[1], wp, sc, nf[1], *routers[1], x3_all, h2_all, ri2_all, rg2_all)
    dest1, ys1 = _moe(h2_all, ri2_all, w_gate[1], w_up[1], w_down[1])
    y_all = _final(dest1, x3_all, rg2_all, ys1, norm_final.reshape(1, D_MODEL))

    def from_tmajor(a, width):
        return jnp.transpose(a.reshape(DEC_SEQ, DEC_BATCH, width), (1, 0, 2))

    y_prompt = y_all[:T_PROMPT].reshape(BATCH, SEQ, D_MODEL)
    y_sample = from_tmajor(y_all[T_PROMPT:], D_MODEL)
    win_k_p = k_last.reshape(1, BATCH, WINDOW, B_KV_HEADS, B_HEAD_DIM)
    win_v_p = v_last.reshape(1, BATCH, WINDOW, B_KV_HEADS, B_HEAD_DIM)
    kn = from_tmajor(k_new, KV_WIDTH).reshape(DEC_BATCH, DEC_SEQ, B_KV_HEADS, B_HEAD_DIM)
    vn = from_tmajor(v_new, KV_WIDTH).reshape(DEC_BATCH, DEC_SEQ, B_KV_HEADS, B_HEAD_DIM)
    win_k_s = jnp.concatenate([cache_k_win[0][:, DEC_SEQ:], kn], axis=1)[None]
    win_v_s = jnp.concatenate([cache_v_win[0][:, DEC_SEQ:], vn], axis=1)[None]
    chunk_v_p = va_last.reshape(1, BATCH, CHUNK, A_HEADS, A_HEAD_DIM)
    chunk_v_s = from_tmajor(va_s, A_WIDTH).reshape(1, DEC_BATCH, DEC_SEQ, A_HEADS, A_HEAD_DIM)
    pool_p = pool_tail[:, 1:][None]
    pool_s = jnp.concatenate([state_pool[0][:, DEC_SEQ:], from_tmajor(hs1, D_MODEL)], axis=1)[None]
    return (y_prompt, y_sample, win_k_p, win_v_p, win_k_s, win_v_s, chunk_v_p, chunk_v_s, pool_p, pool_s)
```

```python
import functools
import math

import numpy as np
import jax
import jax.numpy as jnp
from jax import lax
from jax.experimental import pallas as pl
from jax.experimental.pallas import tpu as pltpu

F32 = jnp.float32
BF16 = jnp.bfloat16

D_MODEL = 1024
BATCH = 2
SEQ = 8192
DEC_BATCH = 128
DEC_SEQ = 4
A_WIDTH = 512
A_HEADS = 8
A_HEAD_DIM = 64
CHUNK = 128
B_HEADS = 8
B_KV_HEADS = 2
B_HEAD_DIM = 64
B_GROUP = 4
WINDOW = 128
N_BUCKETS = 32
MAX_DISTANCE = WINDOW
Q_WIDTH = 512
KV_WIDTH = 128
IN_WIDTH = 2 * A_WIDTH + Q_WIDTH + 2 * KV_WIDTH
ATTN_SCALE = B_HEAD_DIM ** -0.5
NEG_INF = -1e30
POOL_SIZES = (2, 4, 8, 16)
POOL_GROUP_DIM = 256
POOL_MAX = 16
N_GROUPS = 4
EXPERTS_PER_GROUP = 8
N_EXPERTS = 32
TOP_K = 2
D_EXPERT = 512
EPS = 1e-6

LANES = 128
ROW_TILE = D_MODEL // LANES
T_PROMPT = BATCH * SEQ
T_SAMPLE = DEC_BATCH * DEC_SEQ
T_ALL = T_PROMPT + T_SAMPLE
TM = 512
N_PROMPT_BLOCKS = T_PROMPT // TM
N_ROW_BLOCKS = T_ALL // TM
STEPS_PER_BATCH = SEQ // TM
SUB = TM // WINDOW
N_SLOTS = T_ALL * TOP_K
MOE_BLK = 256
N_MOE_BLOCKS = N_SLOTS // MOE_BLK + N_EXPERTS
N_SORT_ROWS = N_MOE_BLOCKS * MOE_BLK
SAMPLE_GROUP = 8
N_SAMPLE_GROUPS = DEC_BATCH // SAMPLE_GROUP
VMEM_LIMIT = 56 * 1024 * 1024

STACK_HEADS = ((0, 2, 5, 7), (1, 3, 4, 6))


def _t5_bucket_np(dist):
    n = np.maximum(dist, 0)
    max_exact = N_BUCKETS // 2
    nf = np.maximum(n, 1).astype(np.float32)
    large = max_exact + (np.log(nf / np.float32(max_exact)) / np.float32(math.log(MAX_DISTANCE / max_exact))
                         * np.float32(N_BUCKETS - max_exact)).astype(np.int32)
    large = np.minimum(large, N_BUCKETS - 1)
    return np.where(n < max_exact, n, large).astype(np.int32)


def _bucket_tables():
    qi = np.arange(WINDOW)[:, None]
    ki = np.arange(2 * WINDOW)[None, :]
    dist = qi + WINDOW - ki
    valid = (dist >= 0) & (dist < WINDOW)
    bp = np.where(valid, _t5_bucket_np(dist), -1)
    bp_first = np.where(ki >= WINDOW, bp, -1)
    bkt_p = np.stack([bp_first, bp]).astype(np.int32)

    t = np.repeat(np.arange(DEC_SEQ), SAMPLE_GROUP)[:, None]
    b = np.tile(np.arange(SAMPLE_GROUP), DEC_SEQ)[:, None]
    cb = np.repeat(np.arange(SAMPLE_GROUP), WINDOW)[None, :]
    cj = np.tile(np.arange(WINDOW), SAMPLE_GROUP)[None, :]
    dist_c = t + WINDOW - cj
    valid_c = (cb == b) & (dist_c >= 0) & (dist_c < WINDOW)
    bkt_sc = np.where(valid_c, _t5_bucket_np(dist_c), -1).astype(np.int32)
    nt = np.repeat(np.arange(DEC_SEQ), SAMPLE_GROUP)[None, :]
    nb = np.tile(np.arange(SAMPLE_GROUP), DEC_SEQ)[None, :]
    dist_n = t - nt
    valid_n = (nb == b) & (dist_n >= 0)
    bkt_sn = np.where(valid_n, _t5_bucket_np(dist_n), -1).astype(np.int32)
    bkt_sn = np.concatenate([bkt_sn, np.full((32, LANES - 32), -1, np.int32)], axis=1)
    return bkt_p, bkt_sc, bkt_sn


_BKT_P, _BKT_SC, _BKT_SN = _bucket_tables()


def _cparams(semantics):
    return pltpu.CompilerParams(dimension_semantics=semantics, vmem_limit_bytes=VMEM_LIMIT)


def _rms(x, g):
    return x * lax.rsqrt(jnp.mean(x * x, axis=-1, keepdims=True) + EPS) * g


def _layernorm(x, g, b):
    xc = x - jnp.mean(x, axis=-1, keepdims=True)
    return xc * lax.rsqrt(jnp.mean(xc * xc, axis=-1, keepdims=True) + EPS) * g + b


def _dot(a, b):
    return jnp.dot(a, b, preferred_element_type=F32)


def _dot_nt(a, b):
    return lax.dot_general(a, b, (((1,), (1,)), ((), ())), preferred_element_type=F32)


def _project(x, nm, win, lng, lnb):
    h = _rms(x, nm)
    z = _dot(h.astype(BF16), win)
    u = jax.nn.gelu(z[:, :A_WIDTH])
    va = _layernorm(jax.nn.gelu(z[:, A_WIDTH:2 * A_WIDTH]), lng, lnb)
    q = z[:, 2 * A_WIDTH:2 * A_WIDTH + Q_WIDTH] * ATTN_SCALE
    k = z[:, 2 * A_WIDTH + Q_WIDTH:2 * A_WIDTH + Q_WIDTH + KV_WIDTH]
    v = z[:, 2 * A_WIDTH + Q_WIDTH + KV_WIDTH:]
    return u, va, q, k, v


def _route(x1, nf, wr, br):
    h = _rms(x1, nf)
    logits = jnp.dot(h, wr, preferred_element_type=F32, precision=lax.Precision.HIGHEST) + br
    rows = logits.shape[0]
    lane = lax.broadcasted_iota(jnp.int32, (rows, LANES), 1)
    lanef = lane.astype(F32)
    big = jnp.float32(1e9)
    is_g = lane < N_GROUPS
    gl = jnp.where(is_g, logits, -jnp.inf)
    gmax = jnp.max(gl, axis=1, keepdims=True)
    gsel = jnp.min(jnp.where(gl == gmax, lanef, big), axis=1, keepdims=True)
    gsum = jnp.sum(jnp.where(is_g, jnp.exp(logits - gmax), 0.0), axis=1, keepdims=True)
    g1 = 1.0 / gsum
    lo = N_GROUPS + EXPERTS_PER_GROUP * gsel
    emask = (lanef >= lo) & (lanef < lo + EXPERTS_PER_GROUP)
    el = jnp.where(emask, logits, -jnp.inf)
    v1 = jnp.max(el, axis=1, keepdims=True)
    i1 = jnp.min(jnp.where(el == v1, lanef, big), axis=1, keepdims=True)
    el2 = jnp.where(lanef == i1, -jnp.inf, el)
    v2 = jnp.max(el2, axis=1, keepdims=True)
    i2 = jnp.min(jnp.where(el2 == v2, lanef, big), axis=1, keepdims=True)
    e2 = jnp.exp(v2 - v1)
    den = 1.0 + e2
    w1 = g1 / den
    w2 = g1 * e2 / den
    ids = jnp.where(lane == 0, i1 - N_GROUPS, jnp.where(lane == 1, i2 - N_GROUPS, 0.0)).astype(jnp.int32)
    gates = jnp.where(lane == 0, w1, jnp.where(lane == 1, w2, 0.0))
    return h, ids, gates


def _softmax_pv(s, sink, v_bf16):
    m = jnp.maximum(jnp.max(s, axis=-1, keepdims=True), sink)
    p = jnp.exp(s - m)
    den = jnp.sum(p, axis=-1, keepdims=True) + jnp.exp(sink - m)
    return p, m, den


def _prep_kernel(tab_ref, bp_ref, bsc_ref, bsn_ref, ws_ref, op_ref, osc_ref, osn_ref, ows_ref):
    def fill(bkt, write):
        for h in range(B_HEADS):
            acc = jnp.full(bkt.shape, NEG_INF, F32)
            for b in range(N_BUCKETS):
                acc = jnp.where(bkt == b, tab_ref[b, h], acc)
            write(h, acc)

    for var in range(2):
        def wr_p(h, acc, var=var):
            op_ref[var, h] = acc
        fill(bp_ref[var], wr_p)

    def wr_sc(h, acc):
        osc_ref[h] = acc
    fill(bsc_ref[...], wr_sc)

    def wr_sn(h, acc):
        osn_ref[h] = acc
    fill(bsn_ref[...], wr_sn)

    r = lax.broadcasted_iota(jnp.int32, (CHUNK, CHUNK), 0)
    c = lax.broadcasted_iota(jnp.int32, (CHUNK, CHUNK), 1)
    for h in range(A_HEADS):
        ows_ref[h] = jnp.where(r >= c, ws_ref[h], 0.0).astype(BF16)


def _prep(rel_bias_table, w_s):
    vm = pl.BlockSpec(memory_space=pltpu.VMEM)
    return pl.pallas_call(
        _prep_kernel,
        in_specs=[pl.BlockSpec(memory_space=pltpu.SMEM), vm, vm, vm, vm],
        out_specs=[vm, vm, vm, vm],
        out_shape=[
            jax.ShapeDtypeStruct((2, B_HEADS, WINDOW, 2 * WINDOW), F32),
            jax.ShapeDtypeStruct((B_HEADS, 32, SAMPLE_GROUP * WINDOW), F32),
            jax.ShapeDtypeStruct((B_HEADS, 32, LANES), F32),
            jax.ShapeDtypeStruct((A_HEADS, CHUNK, CHUNK), BF16),
        ],
        name="prep_tables",
    )(rel_bias_table, jnp.asarray(_BKT_P), jnp.asarray(_BKT_SC), jnp.asarray(_BKT_SN), w_s)


def _gate_pairs(va_rows, wsp_ref, lane_lo):
    outs = []
    for p in range(A_HEADS // 2):
        vp = va_rows[:, p * LANES:(p + 1) * LANES]
        rhs = jnp.concatenate([jnp.where(lane_lo, vp, 0.0), jnp.where(lane_lo, 0.0, vp)], axis=0).astype(BF16)
        outs.append(_dot(wsp_ref[p], rhs))
    return jnp.concatenate(outs, axis=1)


def _prompt_steps(body, first_row_out):
    def kern(*refs):
        i = pl.program_id(0)

        @pl.when(i < N_PROMPT_BLOCKS)
        def _():
            body(*refs)

        @pl.when(i >= N_PROMPT_BLOCKS)
        def _():
            for r in refs[first_row_out:first_row_out + 4]:
                r[...] = jnp.zeros(r.shape, r.dtype)

    return kern


def _mix0_prompt_kernel(x_ref, nm_ref, win_ref, lng_ref, lnb_ref, wsp_ref, bs_ref, bias_ref, sink_ref,
                        wout_ref, nf_ref, wr_ref, br_ref,
                        x1_ref, h_ref, ri_ref, rg_ref, kl_ref, vl_ref, val_ref,
                        kprev, vprev, mix_scr):
    x = x_ref[...]
    u, va, q, k, v = _project(x, nm_ref[...], win_ref[...], lng_ref[...], lnb_ref[...])
    lane_lo = lax.broadcasted_iota(jnp.int32, (WINDOW, LANES), 1) < B_HEAD_DIM
    first = pl.program_id(0) % STEPS_PER_BATCH == 0

    @pl.when(first)
    def _():
        kprev[...] = jnp.zeros_like(kprev)
        vprev[...] = jnp.zeros_like(vprev)

    for j in range(SUB):
        rows = slice(j * WINDOW, (j + 1) * WINDOW)
        s_gate = _gate_pairs(va[rows], wsp_ref, lane_lo)
        mix_scr[rows, :A_WIDTH] = u[rows] * (s_gate + bs_ref[...])

        if j == 0:
            kp, vp = kprev[...], vprev[...]
        else:
            prows = slice((j - 1) * WINDOW, j * WINDOW)
            kp, vp = k[prows], v[prows]
        kk = jnp.concatenate([kp, k[rows]], axis=0)
        vv = jnp.concatenate([vp, v[rows]], axis=0)
        kops = (kk.astype(BF16), pltpu.roll(kk, B_HEAD_DIM, 1).astype(BF16))
        vops = (vv.astype(BF16), pltpu.roll(vv, B_HEAD_DIM, 1).astype(BF16))
        qt = [q[rows, p * LANES:(p + 1) * LANES] for p in range(4)]
        q_even = [jnp.where(lane_lo, t, 0.0) for t in qt]
        q_odd = [jnp.where(lane_lo, 0.0, t) for t in qt]
        stacks = (jnp.concatenate([q_even[0], q_even[1], q_odd[2], q_odd[3]], axis=0),
                  jnp.concatenate([q_odd[0], q_odd[1], q_even[2], q_even[3]], axis=0))
        o = []
        for st in range(2):
            s = _dot_nt(stacks[st].astype(BF16), kops[st])
            if j == 0:
                bias = jnp.where(first, bias_ref[0, st], bias_ref[1, st])
            else:
                bias = bias_ref[1, st]
            s = s + bias
            sink = sink_ref[st]
            m = jnp.maximum(jnp.max(s, axis=-1, keepdims=True), sink)
            p = jnp.exp(s - m)
            den = jnp.sum(p, axis=-1, keepdims=True) + jnp.exp(sink - m)
            o.append(_dot(p.astype(BF16), vops[st]) / den)
        oa, ob = o
        sl = [slice(i * WINDOW, (i + 1) * WINDOW) for i in range(4)]
        tiles = (jnp.where(lane_lo, oa[sl[0]], ob[sl[0]]), jnp.where(lane_lo, oa[sl[1]], ob[sl[1]]),
                 jnp.where(lane_lo, ob[sl[2]], oa[sl[2]]), jnp.where(lane_lo, ob[sl[3]], oa[sl[3]]))
        for p in range(4):
            mix_scr[rows, A_WIDTH + p * LANES:A_WIDTH + (p + 1) * LANES] = tiles[p]

    last = slice(TM - WINDOW, TM)
    kprev[...] = k[last]
    vprev[...] = v[last]
    kl_ref[...] = k[last]
    vl_ref[...] = v[last]
    val_ref[...] = va[last]

    x1 = x + _dot(mix_scr[...].astype(BF16), wout_ref[...])
    x1_ref[...] = x1
    h, ids, gates = _route(x1, nf_ref[...], wr_ref[...], br_ref[...])
    h_ref[...] = h.reshape(h_ref.shape)
    ri_ref[...] = ids
    rg_ref[...] = gates


def _const_spec(shape):
    nd = len(shape)
    return pl.BlockSpec(shape, lambda i, _n=nd: (0,) * _n)


def _mix0_prompt(x_all, nm, win, lng, lnb, wsp, bs_full, bias_p, sink_p, wout, nf, wr, br):
    row_spec = pl.BlockSpec((TM, D_MODEL), lambda i: (i, 0))
    row3_spec = pl.BlockSpec((TM, ROW_TILE, LANES), lambda i: (i, 0, 0))
    lane_spec = pl.BlockSpec((TM, LANES), lambda i: (i, 0))
    last_kv = pl.BlockSpec((None, WINDOW, KV_WIDTH), lambda i: (jnp.minimum(i // STEPS_PER_BATCH, BATCH - 1), 0, 0))
    last_va = pl.BlockSpec((None, WINDOW, A_WIDTH), lambda i: (jnp.minimum(i // STEPS_PER_BATCH, BATCH - 1), 0, 0))
    return pl.pallas_call(
        _prompt_steps(_mix0_prompt_kernel, 13),
        grid=(N_ROW_BLOCKS,),
        in_specs=[pl.BlockSpec((TM, D_MODEL), lambda i: (jnp.minimum(i, N_PROMPT_BLOCKS - 1), 0)),
                  _const_spec((1, D_MODEL)), _const_spec((D_MODEL, IN_WIDTH)),
                  _const_spec((1, A_WIDTH)), _const_spec((1, A_WIDTH)),
                  _const_spec((A_HEADS // 2, CHUNK, 2 * CHUNK)), _const_spec((CHUNK, A_WIDTH)),
                  _const_spec((2, 2, 4 * WINDOW, 2 * WINDOW)), _const_spec((2, 4 * WINDOW, 1)),
                  _const_spec((A_WIDTH + Q_WIDTH, D_MODEL)), _const_spec((1, D_MODEL)),
                  _const_spec((D_MODEL, LANES)), _const_spec((1, LANES))],
        out_specs=[row_spec, row3_spec, lane_spec, lane_spec, last_kv, last_kv, last_va],
        out_shape=[jax.ShapeDtypeStruct((T_ALL, D_MODEL), F32), jax.ShapeDtypeStruct((T_ALL, ROW_TILE, LANES), F32),
                   jax.ShapeDtypeStruct((T_ALL, LANES), jnp.int32), jax.ShapeDtypeStruct((T_ALL, LANES), F32),
                   jax.ShapeDtypeStruct((BATCH, WINDOW, KV_WIDTH), F32),
                   jax.ShapeDtypeStruct((BATCH, WINDOW, KV_WIDTH), F32),
                   jax.ShapeDtypeStruct((BATCH, WINDOW, A_WIDTH), F32)],
        scratch_shapes=[pltpu.VMEM((WINDOW, KV_WIDTH), F32), pltpu.VMEM((WINDOW, KV_WIDTH), F32),
                        pltpu.VMEM((TM, D_MODEL), F32)],
        compiler_params=_cparams(("arbitrary",)),
        name="mix0_prompt",
    )(x_all, nm, win, lng, lnb, wsp, bs_full, bias_p, sink_p, wout, nf, wr, br)


def _mix0_sample_kernel(x_ref, nm_ref, win_ref, lng_ref, lnb_ref, wcoef_ref, bcoef_ref,
                        ck_ref, cv_ref, bsc_ref, bsn_ref, sink_ref,
                        wout_ref, nf_ref, wr_ref, br_ref,
                        x1_in, h_in, ri_in, rg_in,
                        x1_ref, h_ref, ri_ref, rg_ref, kn_ref, vn_ref, va_ref,
                        q_scr, k_scr, v_scr, mix_scr):
    del x1_in, h_in, ri_in, rg_in
    g = pl.program_id(0)

    @pl.when(g == 0)
    def _():
        u, va, q, k, v = _project(x_ref[...], nm_ref[...], win_ref[...], lng_ref[...], lnb_ref[...])
        q_scr[...] = q
        k_scr[...] = k
        v_scr[...] = v
        kn_ref[...] = k
        vn_ref[...] = v
        va_ref[...] = va
        idx = 0
        for t in range(DEC_SEQ):
            acc = jnp.zeros((DEC_BATCH, A_WIDTH), F32) + bcoef_ref[t:t + 1, :]
            for s in range(t + 1):
                acc = acc + wcoef_ref[idx:idx + 1, :] * va[s * DEC_BATCH:(s + 1) * DEC_BATCH]
                idx += 1
            mix_scr[t * DEC_BATCH:(t + 1) * DEC_BATCH, :A_WIDTH] = u[t * DEC_BATCH:(t + 1) * DEC_BATCH] * acc

    b0 = pl.multiple_of(g * SAMPLE_GROUP, SAMPLE_GROUP)
    lane_lo = lax.broadcasted_iota(jnp.int32, (DEC_SEQ * SAMPLE_GROUP, LANES), 1) < B_HEAD_DIM

    def grab(ref, width):
        return jnp.concatenate([ref[pl.ds(t * DEC_BATCH + b0, SAMPLE_GROUP), :] for t in range(DEC_SEQ)], axis=0)

    qg = grab(q_scr, Q_WIDTH)
    kn = grab(k_scr, KV_WIDTH)
    vn = grab(v_scr, KV_WIDTH)
    kc = ck_ref[...]
    vc = cv_ref[...]
    kc_ops = (kc.astype(BF16), pltpu.roll(kc, B_HEAD_DIM, 1).astype(BF16))
    vc_ops = (vc.astype(BF16), pltpu.roll(vc, B_HEAD_DIM, 1).astype(BF16))
    kn_ops = (kn.astype(BF16), pltpu.roll(kn, B_HEAD_DIM, 1).astype(BF16))
    vn_ops = (vn.astype(BF16), pltpu.roll(vn, B_HEAD_DIM, 1).astype(BF16))
    qt = [qg[:, p * LANES:(p + 1) * LANES] for p in range(4)]
    q_even = [jnp.where(lane_lo, t, 0.0) for t in qt]
    q_odd = [jnp.where(lane_lo, 0.0, t) for t in qt]
    stacks = (jnp.concatenate([q_even[0], q_even[1], q_odd[2], q_odd[3]], axis=0),
              jnp.concatenate([q_odd[0], q_odd[1], q_even[2], q_even[3]], axis=0))
    o = []
    for st in range(2):
        qs = stacks[st].astype(BF16)
        sc = _dot_nt(qs, kc_ops[st]) + bsc_ref[st]
        sn = _dot_nt(qs, kn_ops[st]) + bsn_ref[st][:, :DEC_SEQ * SAMPLE_GROUP]
        sink = sink_ref[st]
        m = jnp.maximum(jnp.maximum(jnp.max(sc, axis=-1, keepdims=True), jnp.max(sn, axis=-1, keepdims=True)), sink)
        pc = jnp.exp(sc - m)
        pn = jnp.exp(sn - m)
        den = jnp.sum(pc, axis=-1, keepdims=True) + jnp.sum(pn, axis=-1, keepdims=True) + jnp.exp(sink - m)
        o.append((_dot(pc.astype(BF16), vc_ops[st]) + _dot(pn.astype(BF16), vn_ops[st])) / den)
    oa, ob = o
    n = DEC_SEQ * SAMPLE_GROUP
    sl = [slice(i * n, (i + 1) * n) for i in range(4)]
    tiles = (jnp.where(lane_lo, oa[sl[0]], ob[sl[0]]), jnp.where(lane_lo, oa[sl[1]], ob[sl[1]]),
             jnp.where(lane_lo, ob[sl[2]], oa[sl[2]]), jnp.where(lane_lo, ob[sl[3]], oa[sl[3]]))
    for p in range(4):
        for t in range(DEC_SEQ):
            mix_scr[pl.ds(t * DEC_BATCH + b0, SAMPLE_GROUP), A_WIDTH + p * LANES:A_WIDTH + (p + 1) * LANES] = (
                tiles[p][t * SAMPLE_GROUP:(t + 1) * SAMPLE_GROUP])

    @pl.when(g == N_SAMPLE_GROUPS - 1)
    def _():
        x1 = x_ref[...] + _dot(mix_scr[...].astype(BF16), wout_ref[...])
        x1_ref[...] = x1
        h, ids, gates = _route(x1, nf_ref[...], wr_ref[...], br_ref[...])
        h_ref[...] = h.reshape(h_ref.shape)
        ri_ref[...] = ids
        rg_ref[...] = gates


def _mix0_sample(x_all, nm, win, lng, lnb, wcoef, bcoef, ck, cv, bias_sc, bias_sn, sink_s, wout, nf, wr, br,
                 x1_all, h_all, ri_all, rg_all):
    sample_rows = pl.BlockSpec((TM, D_MODEL), lambda g: (N_PROMPT_BLOCKS, 0))
    sample_rows3 = pl.BlockSpec((TM, ROW_TILE, LANES), lambda g: (N_PROMPT_BLOCKS, 0, 0))
    sample_lanes = pl.BlockSpec((TM, LANES), lambda g: (N_PROMPT_BLOCKS, 0))
    cache_spec = pl.BlockSpec((SAMPLE_GROUP * WINDOW, KV_WIDTH), lambda g: (g, 0))
    anyspec = pl.BlockSpec(memory_space=pl.ANY)
    n_in = 16
    return pl.pallas_call(
        _mix0_sample_kernel,
        grid=(N_SAMPLE_GROUPS,),
        in_specs=[_const_spec((TM, D_MODEL)), _const_spec((1, D_MODEL)), _const_spec((D_MODEL, IN_WIDTH)),
                  _const_spec((1, A_WIDTH)), _const_spec((1, A_WIDTH)),
                  _const_spec((16, A_WIDTH)), _const_spec((8, A_WIDTH)),
                  cache_spec, cache_spec,
                  _const_spec((2, 4 * 32, SAMPLE_GROUP * WINDOW)), _const_spec((2, 4 * 32, LANES)),
                  _const_spec((2, 4 * 32, 1)),
                  _const_spec((A_WIDTH + Q_WIDTH, D_MODEL)), _const_spec((1, D_MODEL)),
                  _const_spec((D_MODEL, LANES)), _const_spec((1, LANES)),
                  anyspec, anyspec, anyspec, anyspec],
        out_specs=[sample_rows, sample_rows3, sample_lanes, sample_lanes,
                   _const_spec((T_SAMPLE, KV_WIDTH)), _const_spec((T_SAMPLE, KV_WIDTH)),
                   _const_spec((T_SAMPLE, A_WIDTH))],
        out_shape=[jax.ShapeDtypeStruct((T_ALL, D_MODEL), F32), jax.ShapeDtypeStruct((T_ALL, ROW_TILE, LANES), F32),
                   jax.ShapeDtypeStruct((T_ALL, LANES), jnp.int32), jax.ShapeDtypeStruct((T_ALL, LANES), F32),
                   jax.ShapeDtypeStruct((T_SAMPLE, KV_WIDTH), F32), jax.ShapeDtypeStruct((T_SAMPLE, KV_WIDTH), F32),
                   jax.ShapeDtypeStruct((T_SAMPLE, A_WIDTH), F32)],
        scratch_shapes=[pltpu.VMEM((T_SAMPLE, Q_WIDTH), F32), pltpu.VMEM((T_SAMPLE, KV_WIDTH), F32),
                        pltpu.VMEM((T_SAMPLE, KV_WIDTH), F32), pltpu.VMEM((T_SAMPLE, D_MODEL), F32)],
        input_output_aliases={n_in: 0, n_in + 1: 1, n_in + 2: 2, n_in + 3: 3},
        compiler_params=_cparams(("arbitrary",)),
        name="mix0_sample",
    )(x_all, nm, win, lng, lnb, wcoef, bcoef, ck, cv, bias_sc, bias_sn, sink_s, wout, nf, wr, br,
      x1_all, h_all, ri_all, rg_all)


def _moe_metadata(ri_all):
    flat = ri_all[:, :TOP_K].reshape(N_SLOTS)
    onehot = (flat[:, None] == jnp.arange(N_EXPERTS, dtype=jnp.int32)[None, :]).astype(jnp.int32)
    csum = jnp.cumsum(onehot, axis=0)
    rank = jnp.sum(onehot * (csum - 1), axis=1)
    counts = csum[-1]
    padded = (counts + MOE_BLK - 1) // MOE_BLK * MOE_BLK
    pad_end = jnp.cumsum(padded)
    pad_start = pad_end - padded
    dest = (jnp.sum(onehot * pad_start[None, :], axis=1) + rank).astype(jnp.int32)
    n_valid = (pad_end[-1] // MOE_BLK).astype(jnp.int32).reshape(1)
    blk_start = jnp.arange(N_MOE_BLOCKS, dtype=jnp.int32) * MOE_BLK
    block_e = jnp.minimum(jnp.sum((blk_start[:, None] >= pad_end[None, :]).astype(jnp.int32), axis=1),
                          N_EXPERTS - 1).astype(jnp.int32)
    zero_start = (pad_start + counts).astype(jnp.int32)
    zero_len = (padded - counts).astype(jnp.int32)
    return dest, block_e, n_valid, jnp.concatenate([zero_start, zero_len, n_valid])


def _dispatch_kernel(dest_ref, zs_ref, h_ref, xs_ref, zero_scr, sem, zsem):
    i = pl.program_id(0)

    @pl.when(i == 0)
    def _():
        zero_scr[...] = jnp.zeros_like(zero_scr)

        def pieces(e, do):
            off = zs_ref[e]
            rem = zs_ref[N_EXPERTS + e]
            bit = MOE_BLK // 2
            while bit >= 1:
                take = (rem & bit) != 0

                @pl.when(take)
                def _(off=off, bit=bit):
                    do(pltpu.make_async_copy(zero_scr.at[pl.ds(0, bit)], xs_ref.at[pl.ds(off, bit)], zsem))

                off = off + jnp.where(take, bit, 0)
                bit //= 2

        def start_e(e, c):
            pieces(e, lambda cp: cp.start())
            return c

        def wait_e(e, c):
            pieces(e, lambda cp: cp.wait())
            return c

        def tail(do):
            def step(b, c):
                do(pltpu.make_async_copy(zero_scr, xs_ref.at[pl.ds(b * MOE_BLK, MOE_BLK)], zsem))
                return c
            return step

        n_valid = zs_ref[2 * N_EXPERTS]
        lax.fori_loop(0, N_EXPERTS, start_e, 0)
        lax.fori_loop(n_valid, N_MOE_BLOCKS, tail(lambda cp: cp.start()), 0)
        lax.fori_loop(0, N_EXPERTS, wait_e, 0)
        lax.fori_loop(n_valid, N_MOE_BLOCKS, tail(lambda cp: cp.wait()), 0)

    base = i * (TM * TOP_K)

    def body(r, carry):
        for kk in range(TOP_K):
            d = dest_ref[base + r * TOP_K + kk]
            pltpu.make_async_copy(h_ref.at[r], xs_ref.at[d], sem).start(priority=kk)
        return carry

    lax.fori_loop(0, TM, body, 0)
    for kk in range(TOP_K):
        pltpu.make_async_copy(h_ref, xs_ref.at[pl.ds(0, TM)], sem).wait()


def _dispatch(dest, zero_start, h_all):
    return pl.pallas_call(
        _dispatch_kernel,
        grid_spec=pltpu.PrefetchScalarGridSpec(
            num_scalar_prefetch=2,
            grid=(N_ROW_BLOCKS,),
            in_specs=[pl.BlockSpec((TM, ROW_TILE, LANES), lambda i, d, z: (i, 0, 0))],
            out_specs=pl.BlockSpec(memory_space=pl.ANY),
            scratch_shapes=[pltpu.VMEM((MOE_BLK, ROW_TILE, LANES), F32), pltpu.SemaphoreType.DMA(()),
                            pltpu.SemaphoreType.DMA(())],
        ),
        out_shape=jax.ShapeDtypeStruct((N_SORT_ROWS, ROW_TILE, LANES), F32),
        compiler_params=_cparams(("arbitrary",)),
        name="moe_dispatch",
    )(dest, zero_start, h_all)


def _experts_kernel(be_ref, nv_ref, x_ref, wg_ref, wu_ref, wd_ref, y_ref, wg_s, wu_s, wd_s):
    i = pl.program_id(0)

    @pl.when(i < nv_ref[0])
    def _():
        e = be_ref[i]
        prev = be_ref[jnp.maximum(i - 1, 0)]

        @pl.when((i == 0) | (e != prev))
        def _():
            wg_s[...] = wg_ref[...].astype(BF16)
            wu_s[...] = wu_ref[...].astype(BF16)
            wd_s[...] = wd_ref[...].astype(BF16)

        xb = x_ref[...].reshape(MOE_BLK, D_MODEL).astype(BF16)
        a = jax.nn.silu(_dot(xb, wg_s[...])) * _dot(xb, wu_s[...])
        y_ref[...] = _dot(a.astype(BF16), wd_s[...]).reshape(y_ref.shape)

    @pl.when(i >= nv_ref[0])
    def _():
        y_ref[...] = jnp.zeros(y_ref.shape, y_ref.dtype)


def _experts(block_e, n_valid, xs, w_gate, w_up, w_down, layer):
    def blk(i, be, nv):
        return jnp.minimum(i, nv[0] - 1)

    def wmap(i, be, nv):
        return (layer, be[blk(i, be, nv)], 0, 0)

    return pl.pallas_call(
        _experts_kernel,
        grid_spec=pltpu.PrefetchScalarGridSpec(
            num_scalar_prefetch=2,
            grid=(N_MOE_BLOCKS,),
            in_specs=[pl.BlockSpec((MOE_BLK, ROW_TILE, LANES), lambda i, be, nv: (blk(i, be, nv), 0, 0)),
                      pl.BlockSpec((None, None, D_MODEL, D_EXPERT), wmap),
                      pl.BlockSpec((None, None, D_MODEL, D_EXPERT), wmap),
                      pl.BlockSpec((None, None, D_EXPERT, D_MODEL), wmap)],
            out_specs=pl.BlockSpec((MOE_BLK, ROW_TILE, LANES), lambda i, be, nv: (i, 0, 0)),
            scratch_shapes=[pltpu.VMEM((D_MODEL, D_EXPERT), BF16), pltpu.VMEM((D_MODEL, D_EXPERT), BF16),
                            pltpu.VMEM((D_EXPERT, D_MODEL), BF16)],
        ),
        out_shape=jax.ShapeDtypeStruct((N_SORT_ROWS, ROW_TILE, LANES), F32),
        compiler_params=_cparams(("arbitrary",)),
        name="moe_experts",
    )(block_e, n_valid, xs, w_gate, w_up, w_down)


def _gather_rows(dest_ref, ys_ref, ybuf, sem, i):
    base = i * (TM * TOP_K)

    def body(r, carry):
        for kk in range(TOP_K):
            d = dest_ref[base + r * TOP_K + kk]
            pltpu.make_async_copy(ys_ref.at[d], ybuf.at[kk, r], sem).start(priority=kk)
        return carry

    lax.fori_loop(0, TM, body, 0)
    for kk in range(TOP_K):
        pltpu.make_async_copy(ys_ref.at[pl.ds(0, TM)], ybuf.at[kk], sem).wait()


def _combined(x_ref, rg_ref, ybuf):
    rg = rg_ref[...]
    y0 = ybuf[0].reshape(TM, D_MODEL)
    y1 = ybuf[1].reshape(TM, D_MODEL)
    return x_ref[...] + rg[:, 0:1] * y0 + rg[:, 1:2] * y1


def _combine_kernel(dest_ref, x_ref, rg_ref, ys_ref, o_ref, ybuf, sem):
    _gather_rows(dest_ref, ys_ref, ybuf, sem, pl.program_id(0))
    o_ref[...] = _combined(x_ref, rg_ref, ybuf)


def _combine(dest, x_all, rg_all, ys):
    return pl.pallas_call(
        _combine_kernel,
        grid_spec=pltpu.PrefetchScalarGridSpec(
            num_scalar_prefetch=1,
            grid=(N_ROW_BLOCKS,),
            in_specs=[pl.BlockSpec((TM, D_MODEL), lambda i, d: (i, 0)),
                      pl.BlockSpec((TM, LANES), lambda i, d: (i, 0)),
                      pl.BlockSpec(memory_space=pl.ANY)],
            out_specs=pl.BlockSpec((TM, D_MODEL), lambda i, d: (i, 0)),
            scratch_shapes=[pltpu.VMEM((TOP_K, TM, ROW_TILE, LANES), F32), pltpu.SemaphoreType.DMA(())],
        ),
        out_shape=jax.ShapeDtypeStruct((T_ALL, D_MODEL), F32),
        compiler_params=_cparams(("arbitrary",)),
        name="moe_combine",
    )(dest, x_all, rg_all, ys)


def _final_kernel(dest_ref, x_ref, rg_ref, ys_ref, nfin_ref, op_ref, os_ref, ybuf, sem):
    i = pl.program_id(0)
    _gather_rows(dest_ref, ys_ref, ybuf, sem, i)
    y = _rms(_combined(x_ref, rg_ref, ybuf), nfin_ref[...])

    @pl.when(i < N_PROMPT_BLOCKS)
    def _():
        op_ref[...] = y

    @pl.when(i >= N_PROMPT_BLOCKS)
    def _():
        os_ref[...] = y


def _final(dest, x_all, rg_all, ys, nfin):
    return pl.pallas_call(
        _final_kernel,
        grid_spec=pltpu.PrefetchScalarGridSpec(
            num_scalar_prefetch=1,
            grid=(N_ROW_BLOCKS,),
            in_specs=[pl.BlockSpec((TM, D_MODEL), lambda i, d: (i, 0)),
                      pl.BlockSpec((TM, LANES), lambda i, d: (i, 0)),
                      pl.BlockSpec(memory_space=pl.ANY),
                      pl.BlockSpec((1, D_MODEL), lambda i, d: (0, 0))],
            out_specs=[pl.BlockSpec((TM, D_MODEL), lambda i, d: (jnp.minimum(i, N_PROMPT_BLOCKS - 1), 0)),
                       pl.BlockSpec((TM, D_MODEL), lambda i, d: (0, 0))],
            scratch_shapes=[pltpu.VMEM((TOP_K, TM, ROW_TILE, LANES), F32), pltpu.SemaphoreType.DMA(())],
        ),
        out_shape=[jax.ShapeDtypeStruct((T_PROMPT, D_MODEL), F32), jax.ShapeDtypeStruct((T_SAMPLE, D_MODEL), F32)],
        compiler_params=_cparams(("arbitrary",)),
        name="moe_combine_final",
    )(dest, x_all, rg_all, ys, nfin)


def _moe(h_all, ri_all, w_gate, w_up, w_down, layer):
    dest, block_e, n_valid, zero_start = _moe_metadata(ri_all)
    xs = _dispatch(dest, zero_start, h_all)
    ys = _experts(block_e, n_valid, xs, w_gate, w_up, w_down, layer)
    return dest, ys


def _pool_project(d_groups, wp_ref, scale):
    outs = [_dot(d_groups[g].astype(BF16), wp_ref[g]) for g in range(len(POOL_SIZES))]
    return jnp.concatenate(outs, axis=1) * scale


def _mix1_prompt_kernel(x_ref, nm_ref, wp_ref, sc_ref, nf_ref, wr_ref, br_ref,
                        x3_ref, h_ref, ri_ref, rg_ref, pl_ref, ext):
    i = pl.program_id(0)
    x = x_ref[...]
    hp = _rms(x, nm_ref[...])

    @pl.when(i % STEPS_PER_BATCH == 0)
    def _():
        ext[0:POOL_MAX, :] = jnp.zeros((POOL_MAX, D_MODEL), F32)

    ext[POOL_MAX:, :] = hp
    pos = (i % STEPS_PER_BATCH) * TM + lax.broadcasted_iota(jnp.int32, (TM, 1), 0)
    d_groups = []
    for g, w in enumerate(POOL_SIZES):
        cols = slice(g * POOL_GROUP_DIM, (g + 1) * POOL_GROUP_DIM)
        acc = ext[:, cols]
        span = 1
        while span < w:
            acc = acc + pltpu.roll(acc, span, 0)
            span *= 2
        cnt = jnp.minimum(pos + 1, w).astype(F32)
        d_groups.append(acc[POOL_MAX:] / cnt - hp[:, cols])
    tail = hp[TM - POOL_MAX:, :]
    ext[0:POOL_MAX, :] = tail
    pl_ref[...] = tail

    x3 = x + _pool_project(d_groups, wp_ref, sc_ref[...])
    x3_ref[...] = x3
    h, ids, gates = _route(x3, nf_ref[...], wr_ref[...], br_ref[...])
    h_ref[...] = h.reshape(h_ref.shape)
    ri_ref[...] = ids
    rg_ref[...] = gates


def _mix1_prompt(x_all, nm, wp, sc, nf, wr, br):
    row_spec = pl.BlockSpec((TM, D_MODEL), lambda i: (i, 0))
    row3_spec = pl.BlockSpec((TM, ROW_TILE, LANES), lambda i: (i, 0, 0))
    lane_spec = pl.BlockSpec((TM, LANES), lambda i: (i, 0))
    return pl.pallas_call(
        _prompt_steps(_mix1_prompt_kernel, 7),
        grid=(N_ROW_BLOCKS,),
        in_specs=[row_spec, _const_spec((1, D_MODEL)),
                  _const_spec((len(POOL_SIZES), POOL_GROUP_DIM, POOL_GROUP_DIM)), _const_spec((1, D_MODEL)),
                  _const_spec((1, D_MODEL)), _const_spec((D_MODEL, LANES)), _const_spec((1, LANES))],
        out_specs=[row_spec, row3_spec, lane_spec, lane_spec,
                   pl.BlockSpec((None, POOL_MAX, D_MODEL),
                                lambda i: (jnp.minimum(i // STEPS_PER_BATCH, BATCH - 1), 0, 0))],
        out_shape=[jax.ShapeDtypeStruct((T_ALL, D_MODEL), F32), jax.ShapeDtypeStruct((T_ALL, ROW_TILE, LANES), F32),
                   jax.ShapeDtypeStruct((T_ALL, LANES), jnp.int32), jax.ShapeDtypeStruct((T_ALL, LANES), F32),
                   jax.ShapeDtypeStruct((BATCH, POOL_MAX, D_MODEL), F32)],
        scratch_shapes=[pltpu.VMEM((POOL_MAX + TM, D_MODEL), F32)],
        compiler_params=_cparams(("arbitrary",)),
        name="mix1_prompt",
    )(x_all, nm, wp, sc, nf, wr, br)


def _mix1_sample_kernel(x_ref, st_ref, nm_ref, wp_ref, sc_ref, nf_ref, wr_ref, br_ref,
                        x3_in, h_in, ri_in, rg_in,
                        x3_ref, h_ref, ri_ref, rg_ref, hs_ref):
    del x3_in, h_in, ri_in, rg_in
    x = x_ref[...]
    hs = _rms(x, nm_ref[...])
    hs_ref[...] = hs
    n_ctx = POOL_MAX - 1
    d_groups = []
    for g, w in enumerate(POOL_SIZES):
        cols = slice(g * POOL_GROUP_DIM, (g + 1) * POOL_GROUP_DIM)
        parts = []
        for t in range(DEC_SEQ):
            acc = hs[t * DEC_BATCH:(t + 1) * DEC_BATCH, cols]
            for back in range(1, w):
                src = t - back
                if src >= 0:
                    acc = acc + hs[src * DEC_BATCH:(src + 1) * DEC_BATCH, cols]
                else:
                    acc = acc + st_ref[n_ctx + src, :, cols]
            parts.append(acc / float(w) - hs[t * DEC_BATCH:(t + 1) * DEC_BATCH, cols])
        d_groups.append(jnp.concatenate(parts, axis=0))
    x3 = x + _pool_project(d_groups, wp_ref, sc_ref[...])
    x3_ref[...] = x3
    h, ids, gates = _route(x3, nf_ref[...], wr_ref[...], br_ref[...])
    h_ref[...] = h.reshape(h_ref.shape)
    ri_ref[...] = ids
    rg_ref[...] = gates


def _mix1_sample(x_all, state_t, nm, wp, sc, nf, wr, br, x3_all, h_all, ri_all, rg_all):
    sample_rows = pl.BlockSpec((TM, D_MODEL), lambda g: (N_PROMPT_BLOCKS, 0))
    sample_rows3 = pl.BlockSpec((TM, ROW_TILE, LANES), lambda g: (N_PROMPT_BLOCKS, 0, 0))
    sample_lanes = pl.BlockSpec((TM, LANES), lambda g: (N_PROMPT_BLOCKS, 0))
    anyspec = pl.BlockSpec(memory_space=pl.ANY)
    n_in = 8
    return pl.pallas_call(
        _mix1_sample_kernel,
        grid=(1,),
        in_specs=[sample_rows, _const_spec((POOL_MAX - 1, DEC_BATCH, D_MODEL)), _const_spec((1, D_MODEL)),
                  _const_spec((len(POOL_SIZES), POOL_GROUP_DIM, POOL_GROUP_DIM)), _const_spec((1, D_MODEL)),
                  _const_spec((1, D_MODEL)), _const_spec((D_MODEL, LANES)), _const_spec((1, LANES)),
                  anyspec, anyspec, anyspec, anyspec],
        out_specs=[sample_rows, sample_rows3, sample_lanes, sample_lanes, _const_spec((T_SAMPLE, D_MODEL))],
        out_shape=[jax.ShapeDtypeStruct((T_ALL, D_MODEL), F32), jax.ShapeDtypeStruct((T_ALL, ROW_TILE, LANES), F32),
                   jax.ShapeDtypeStruct((T_ALL, LANES), jnp.int32), jax.ShapeDtypeStruct((T_ALL, LANES), F32),
                   jax.ShapeDtypeStruct((T_SAMPLE, D_MODEL), F32)],
        input_output_aliases={n_in: 0, n_in + 1: 1, n_in + 2: 2, n_in + 3: 3},
        compiler_params=_cparams(("arbitrary",)),
        name="mix1_sample",
    )(x_all, state_t, nm, wp, sc, nf, wr, br, x3_all, h_all, ri_all, rg_all)


def _router_weights(wg, bg, we, be):
    w = jnp.concatenate([wg, jnp.transpose(we, (1, 0, 2)).reshape(D_MODEL, N_EXPERTS)], axis=1)
    b = jnp.concatenate([bg, be.reshape(N_EXPERTS)])
    pad = LANES - N_GROUPS - N_EXPERTS
    return jnp.pad(w, ((0, 0), (0, pad))), jnp.pad(b, (0, pad)).reshape(1, LANES)


def _stack(tab):
    return jnp.stack([jnp.concatenate([tab[h] for h in heads], axis=0) for heads in STACK_HEADS])


def kernel(x_prompt, x_sample, cache_k_win, cache_v_win, state_pool, norm_mix, norm_ffn, norm_final, w_in,
           a_ln_g, a_ln_b, a_w_s, a_b_s, b_sinks, rel_bias_table, w_out, c_w_pool, c_scale,
           router_group_w, router_group_b, router_expert_w, router_expert_b, w_gate, w_up, w_down):
    xs_t = jnp.transpose(x_sample, (1, 0, 2)).reshape(T_SAMPLE, D_MODEL)
    xp2 = x_prompt.reshape(T_PROMPT, D_MODEL)
    win =w_in[0].astype(BF16)
    wout = w_out[0].astype(BF16)
    lng = a_ln_g[0].reshape(1, A_WIDTH)
    lnb = a_ln_b[0].reshape(1, A_WIDTH)
    bias_p, bias_sc, bias_sn, ws_tril = _prep(rel_bias_table, a_w_s[0])
    wsp = ws_tril.reshape(A_HEADS // 2, 2, CHUNK, CHUNK).transpose(0, 2, 1, 3).reshape(A_HEADS // 2, CHUNK, 2 * CHUNK)
    bs_full = jnp.repeat(a_b_s[0].T, A_HEAD_DIM, axis=1)
    bias_p = jnp.stack([_stack(bias_p[0]), _stack(bias_p[1])])
    bias_sc = _stack(bias_sc)
    bias_sn = _stack(bias_sn)
    sinks = b_sinks[0]
    sink_p = jnp.stack([jnp.repeat(sinks[jnp.array(hh)], WINDOW) for hh in STACK_HEADS]).reshape(2, 4 * WINDOW, 1)
    sink_s = jnp.stack([jnp.repeat(sinks[jnp.array(hh)], 32) for hh in STACK_HEADS]).reshape(2, 4 * 32, 1)
    pairs = [(t, s) for t in range(DEC_SEQ) for s in range(t + 1)]
    wcoef = jnp.stack([jnp.repeat(a_w_s[0][:, t, s], A_HEAD_DIM) for t, s in pairs])
    wcoef = jnp.pad(wcoef, ((0, 16 - len(pairs)), (0, 0)))
    bcoef = jnp.pad(jnp.repeat(a_b_s[0][:, :DEC_SEQ].T, A_HEAD_DIM, axis=1), ((0, 8 - DEC_SEQ), (0, 0)))
    ck = cache_k_win[0].reshape(DEC_BATCH * WINDOW, KV_WIDTH)
    cv = cache_v_win[0].reshape(DEC_BATCH * WINDOW, KV_WIDTH)
    routers = [_router_weights(router_group_w[l], router_group_b[l], router_expert_w[l], router_expert_b[l])
               for l in range(2)]
    nm = [norm_mix[l].reshape(1, D_MODEL) for l in range(2)]
    nf = [norm_ffn[l].reshape(1, D_MODEL) for l in range(2)]

    x1_all, h_all, ri_all, rg_all, k_last, v_last, va_last = _mix0_prompt(
        xp2, nm[0], win, lng, lnb, wsp, bs_full, bias_p, sink_p, wout, nf[0], *routers[0])
    x1_all, h_all, ri_all, rg_all, k_new, v_new, va_s = _mix0_sample(
        xs_t, nm[0], win, lng, lnb, wcoef, bcoef, ck, cv, bias_sc, bias_sn, sink_s, wout, nf[0], *routers[0],
        x1_all, h_all, ri_all, rg_all)
    dest0, ys0 = _moe(h_all, ri_all, w_gate, w_up, w_down, 0)
    x2_all = _combine(dest0, x1_all, rg_all, ys0)

    wp = c_w_pool[0].astype(BF16)
    sc = c_scale[0].reshape(1, D_MODEL)
    x3_all, h2_all, ri2_all, rg2_all, pool_tail = _mix1_prompt(x2_all, nm[1], wp, sc, nf[1], *routers[1])
    state_t = jnp.transpose(state_pool[0], (1, 0, 2))
    x3_all, h2_all, ri2_all, rg2_all, hs1 = _mix1_sample(
        x2_all, state_t, nm[1], wp, sc, nf[1], *routers[1], x3_all, h2_all, ri2_all, rg2_all)
    dest1, ys1 = _moe(h2_all, ri2_all, w_gate, w_up, w_down, 1)
    y_p, y_s = _final(dest1, x3_all, rg2_all, ys1, norm_final.reshape(1, D_MODEL))

    def from_tmajor(a, width):
        return jnp.transpose(a.reshape(DEC_SEQ, DEC_BATCH, width), (1, 0, 2))

    y_prompt = y_p.reshape(BATCH, SEQ, D_MODEL)
    y_sample = from_tmajor(y_s, D_MODEL)
    win_k_p = k_last.reshape(1, BATCH, WINDOW, B_KV_HEADS, B_HEAD_DIM)
    win_v_p = v_last.reshape(1, BATCH, WINDOW, B_KV_HEADS, B_HEAD_DIM)
    kn = from_tmajor(k_new, KV_WIDTH).reshape(DEC_BATCH, DEC_SEQ, B_KV_HEADS, B_HEAD_DIM)
    vn = from_tmajor(v_new, KV_WIDTH).reshape(DEC_BATCH, DEC_SEQ, B_KV_HEADS, B_HEAD_DIM)
    win_k_s = jnp.concatenate([cache_k_win[0][:, DEC_SEQ:], kn], axis=1)[None]
    win_v_s = jnp.concatenate([cache_v_win[0][:, DEC_SEQ:], vn], axis=1)[None]
    chunk_v_p = va_last.reshape(1, BATCH, CHUNK, A_HEADS, A_HEAD_DIM)
    chunk_v_s = from_tmajor(va_s, A_WIDTH).reshape(1, DEC_BATCH, DEC_SEQ, A_HEADS, A_HEAD_DIM)
    pool_p = pool_tail[:, 1:][None]
    pool_s = jnp.concatenate([state_pool[0][:, DEC_SEQ:], from_tmajor(hs1, D_MODEL)], axis=1)[None]
    return (y_prompt, y_sample, win_k_p, win_v_p, win_k_s, win_v_s, chunk_v_p, chunk_v_s, pool_p, pool_s)
```

```python
import functools
import math

import numpy as np
import jax
import jax.numpy as jnp
from jax import lax
from jax.experimental import pallas as pl
from jax.experimental.pallas import tpu as pltpu

F32 = jnp.float32
BF16 = jnp.bfloat16

D_MODEL = 1024
BATCH = 2
SEQ = 8192
DEC_BATCH = 128
DEC_SEQ = 4
A_WIDTH = 512
A_HEADS = 8
A_HEAD_DIM = 64
CHUNK = 128
B_HEADS = 8
B_KV_HEADS = 2
B_HEAD_DIM = 64
B_GROUP = 4
WINDOW = 128
N_BUCKETS = 32
MAX_DISTANCE = WINDOW
Q_WIDTH = 512
KV_WIDTH = 128
IN_WIDTH = 2 * A_WIDTH + Q_WIDTH + 2 * KV_WIDTH
ATTN_SCALE = B_HEAD_DIM ** -0.5
NEG_INF = -1e30
POOL_SIZES = (2, 4, 8, 16)
POOL_GROUP_DIM = 256
POOL_MAX = 16
N_GROUPS = 4
EXPERTS_PER_GROUP = 8
N_EXPERTS = 32
TOP_K = 2
D_EXPERT = 512
EPS = 1e-6

LANES = 128
ROW_TILE = D_MODEL // LANES
T_PROMPT = BATCH * SEQ
T_SAMPLE = DEC_BATCH * DEC_SEQ
T_ALL = T_PROMPT + T_SAMPLE
TM = 512
N_PROMPT_BLOCKS = T_PROMPT // TM
N_ROW_BLOCKS = T_ALL // TM
STEPS_PER_BATCH = SEQ // TM
SUB = TM // WINDOW
N_SLOTS = T_ALL * TOP_K
MOE_BLK = 256
N_MOE_BLOCKS = N_SLOTS // MOE_BLK + N_EXPERTS
N_SORT_ROWS = N_MOE_BLOCKS * MOE_BLK
SAMPLE_GROUP = 8
N_SAMPLE_GROUPS = DEC_BATCH // SAMPLE_GROUP
VMEM_LIMIT = 56 * 1024 * 1024

STACK_HEADS = ((0, 2, 5, 7), (1, 3, 4, 6))


def _t5_bucket_np(dist):
    n = np.maximum(dist, 0)
    max_exact = N_BUCKETS // 2
    nf = np.maximum(n, 1).astype(np.float32)
    large = max_exact + (np.log(nf / np.float32(max_exact)) / np.float32(math.log(MAX_DISTANCE / max_exact))
                         * np.float32(N_BUCKETS - max_exact)).astype(np.int32)
    large = np.minimum(large, N_BUCKETS - 1)
    return np.where(n < max_exact, n, large).astype(np.int32)


def _bucket_tables():
    qi = np.arange(WINDOW)[:, None]
    ki = np.arange(2 * WINDOW)[None, :]
    dist = qi + WINDOW - ki
    valid = (dist >= 0) & (dist < WINDOW)
    bp = np.where(valid, _t5_bucket_np(dist), -1)
    bp_first = np.where(ki >= WINDOW, bp, -1)
    bkt_p = np.stack([bp_first, bp]).astype(np.int32)

    t = np.repeat(np.arange(DEC_SEQ), SAMPLE_GROUP)[:, None]
    b = np.tile(np.arange(SAMPLE_GROUP), DEC_SEQ)[:, None]
    cb = np.repeat(np.arange(SAMPLE_GROUP), WINDOW)[None, :]
    cj = np.tile(np.arange(WINDOW), SAMPLE_GROUP)[None, :]
    dist_c = t + WINDOW - cj
    valid_c = (cb == b) & (dist_c >= 0) & (dist_c < WINDOW)
    bkt_sc = np.where(valid_c, _t5_bucket_np(dist_c), -1).astype(np.int32)
    nt = np.repeat(np.arange(DEC_SEQ), SAMPLE_GROUP)[None, :]
    nb = np.tile(np.arange(SAMPLE_GROUP), DEC_SEQ)[None, :]
    dist_n = t - nt
    valid_n = (nb == b) & (dist_n >= 0)
    bkt_sn = np.where(valid_n, _t5_bucket_np(dist_n), -1).astype(np.int32)
    bkt_sn = np.concatenate([bkt_sn, np.full((32, LANES - 32), -1, np.int32)], axis=1)
    return bkt_p, bkt_sc, bkt_sn


_BKT_P, _BKT_SC, _BKT_SN = _bucket_tables()


def _cparams(semantics):
    return pltpu.CompilerParams(dimension_semantics=semantics, vmem_limit_bytes=VMEM_LIMIT)


def _rms(x, g):
    return x * lax.rsqrt(jnp.mean(x * x, axis=-1, keepdims=True) + EPS) * g


def _layernorm(x, g, b):
    xc = x - jnp.mean(x, axis=-1, keepdims=True)
    return xc * lax.rsqrt(jnp.mean(xc * xc, axis=-1, keepdims=True) + EPS) * g + b


def _dot(a, b):
    return jnp.dot(a, b, preferred_element_type=F32)


def _dot_nt(a, b):
    return lax.dot_general(a, b, (((1,), (1,)), ((), ())), preferred_element_type=F32)


def _project(x, nm, win, lng, lnb):
    h = _rms(x, nm)
    z = _dot(h.astype(BF16), win)
    u = jax.nn.gelu(z[:, :A_WIDTH])
    va = _layernorm(jax.nn.gelu(z[:, A_WIDTH:2 * A_WIDTH]), lng, lnb)
    q = z[:, 2 * A_WIDTH:2 * A_WIDTH + Q_WIDTH] * ATTN_SCALE
    k = z[:, 2 * A_WIDTH + Q_WIDTH:2 * A_WIDTH + Q_WIDTH + KV_WIDTH]
    v = z[:, 2 * A_WIDTH + Q_WIDTH + KV_WIDTH:]
    return u, va, q, k, v


def _route(x1, nf, wr, br):
    h = _rms(x1, nf)
    logits = jnp.dot(h, wr, preferred_element_type=F32, precision=lax.Precision.HIGHEST) + br
    rows = logits.shape[0]
    lane = lax.broadcasted_iota(jnp.int32, (rows, LANES), 1)
    lanef = lane.astype(F32)
    big = jnp.float32(1e9)
    is_g = lane < N_GROUPS
    gl = jnp.where(is_g, logits, -jnp.inf)
    gmax = jnp.max(gl, axis=1, keepdims=True)
    gsel = jnp.min(jnp.where(gl == gmax, lanef, big), axis=1, keepdims=True)
    gsum = jnp.sum(jnp.where(is_g, jnp.exp(logits - gmax), 0.0), axis=1, keepdims=True)
    g1 = 1.0 / gsum
    lo = N_GROUPS + EXPERTS_PER_GROUP * gsel
    emask = (lanef >= lo) & (lanef < lo + EXPERTS_PER_GROUP)
    el = jnp.where(emask, logits, -jnp.inf)
    v1 = jnp.max(el, axis=1, keepdims=True)
    i1 = jnp.min(jnp.where(el == v1, lanef, big), axis=1, keepdims=True)
    el2 = jnp.where(lanef == i1, -jnp.inf, el)
    v2 = jnp.max(el2, axis=1, keepdims=True)
    i2 = jnp.min(jnp.where(el2 == v2, lanef, big), axis=1, keepdims=True)
    e2 = jnp.exp(v2 - v1)
    den = 1.0 + e2
    w1 = g1 / den
    w2 = g1 * e2 / den
    ids = jnp.where(lane == 0, i1 - N_GROUPS, jnp.where(lane == 1, i2 - N_GROUPS, 0.0)).astype(jnp.int32)
    gates = jnp.where(lane == 0, w1, jnp.where(lane == 1, w2, 0.0))
    return h, ids, gates


def _softmax_pv(s, sink, v_bf16):
    m = jnp.maximum(jnp.max(s, axis=-1, keepdims=True), sink)
    p = jnp.exp(s - m)
    den = jnp.sum(p, axis=-1, keepdims=True) + jnp.exp(sink - m)
    return p, m, den


def _prep_kernel(tab_ref, bp_ref, bsc_ref, bsn_ref, ws_ref, op_ref, osc_ref, osn_ref, ows_ref):
    def fill(bkt, write):
        for h in range(B_HEADS):
            acc = jnp.full(bkt.shape, NEG_INF, F32)
            for b in range(N_BUCKETS):
                acc = jnp.where(bkt == b, tab_ref[b, h], acc)
            write(h, acc)

    for var in range(2):
        def wr_p(h, acc, var=var):
            op_ref[var, h] = acc
        fill(bp_ref[var], wr_p)

    def wr_sc(h, acc):
        osc_ref[h] = acc
    fill(bsc_ref[...], wr_sc)

    def wr_sn(h, acc):
        osn_ref[h] = acc
    fill(bsn_ref[...], wr_sn)

    r = lax.broadcasted_iota(jnp.int32, (CHUNK, CHUNK), 0)
    c = lax.broadcasted_iota(jnp.int32, (CHUNK, CHUNK), 1)
    for h in range(A_HEADS):
        ows_ref[h] = jnp.where(r >= c, ws_ref[h], 0.0).astype(BF16)


def _prep(rel_bias_table, w_s):
    vm = pl.BlockSpec(memory_space=pltpu.VMEM)
    return pl.pallas_call(
        _prep_kernel,
        in_specs=[pl.BlockSpec(memory_space=pltpu.SMEM), vm, vm, vm, vm],
        out_specs=[vm, vm, vm, vm],
        out_shape=[
            jax.ShapeDtypeStruct((2, B_HEADS, WINDOW, 2 * WINDOW), F32),
            jax.ShapeDtypeStruct((B_HEADS, 32, SAMPLE_GROUP * WINDOW), F32),
            jax.ShapeDtypeStruct((B_HEADS, 32, LANES), F32),
            jax.ShapeDtypeStruct((A_HEADS, CHUNK, CHUNK), BF16),
        ],
        name="prep_tables",
    )(rel_bias_table, jnp.asarray(_BKT_P), jnp.asarray(_BKT_SC), jnp.asarray(_BKT_SN), w_s)


def _gate_pairs(va_rows, wsp_ref, lane_lo):
    outs = []
    for p in range(A_HEADS // 2):
        vp = va_rows[:, p * LANES:(p + 1) * LANES]
        rhs = jnp.concatenate([jnp.where(lane_lo, vp, 0.0), jnp.where(lane_lo, 0.0, vp)], axis=0).astype(BF16)
        outs.append(_dot(wsp_ref[p], rhs))
    return jnp.concatenate(outs, axis=1)


def _prompt_steps(body, first_row_out):
    def kern(*refs):
        i = pl.program_id(0)

        @pl.when(i < N_PROMPT_BLOCKS)
        def _():
            body(*refs)

        @pl.when(i >= N_PROMPT_BLOCKS)
        def _():
            for r in refs[first_row_out:first_row_out + 4]:
                r[...] = jnp.zeros(r.shape, r.dtype)

    return kern


def _mix0_prompt_kernel(x_ref, nm_ref, win_ref, lng_ref, lnb_ref, wsp_ref, bs_ref, bias_ref, sink_ref,
                        wout_ref, nf_ref, wr_ref, br_ref,
                        x1_ref, h_ref, ri_ref, rg_ref, kl_ref, vl_ref, val_ref,
                        kprev, vprev, mix_scr):
    x = x_ref[...]
    u, va, q, k, v = _project(x, nm_ref[...], win_ref[...], lng_ref[...], lnb_ref[...])
    lane_lo = lax.broadcasted_iota(jnp.int32, (WINDOW, LANES), 1) < B_HEAD_DIM
    first = pl.program_id(0) % STEPS_PER_BATCH == 0

    @pl.when(first)
    def _():
        kprev[...] = jnp.zeros_like(kprev)
        vprev[...] = jnp.zeros_like(vprev)

    for j in range(SUB):
        rows = slice(j * WINDOW, (j + 1) * WINDOW)
        s_gate = _gate_pairs(va[rows], wsp_ref, lane_lo)
        mix_scr[rows, :A_WIDTH] = u[rows] * (s_gate + bs_ref[...])

        if j == 0:
            kp, vp = kprev[...], vprev[...]
        else:
            prows = slice((j - 1) * WINDOW, j * WINDOW)
            kp, vp = k[prows], v[prows]
        kk = jnp.concatenate([kp, k[rows]], axis=0)
        vv = jnp.concatenate([vp, v[rows]], axis=0)
        kops = (kk.astype(BF16), pltpu.roll(kk, B_HEAD_DIM, 1).astype(BF16))
        vops = (vv.astype(BF16), pltpu.roll(vv, B_HEAD_DIM, 1).astype(BF16))
        qt = [q[rows, p * LANES:(p + 1) * LANES] for p in range(4)]
        q_even = [jnp.where(lane_lo, t, 0.0) for t in qt]
        q_odd = [jnp.where(lane_lo, 0.0, t) for t in qt]
        stacks = (jnp.concatenate([q_even[0], q_even[1], q_odd[2], q_odd[3]], axis=0),
                  jnp.concatenate([q_odd[0], q_odd[1], q_even[2], q_even[3]], axis=0))
        o = []
        for st in range(2):
            s = _dot_nt(stacks[st].astype(BF16), kops[st])
            if j == 0:
                bias = jnp.where(first, bias_ref[0, st], bias_ref[1, st])
            else:
                bias = bias_ref[1, st]
            s = s + bias
            sink = sink_ref[st]
            m = jnp.maximum(jnp.max(s, axis=-1, keepdims=True), sink)
            p = jnp.exp(s - m)
            den = jnp.sum(p, axis=-1, keepdims=True) + jnp.exp(sink - m)
            o.append(_dot(p.astype(BF16), vops[st]) / den)
        oa, ob = o
        sl = [slice(i * WINDOW, (i + 1) * WINDOW) for i in range(4)]
        tiles = (jnp.where(lane_lo, oa[sl[0]], ob[sl[0]]), jnp.where(lane_lo, oa[sl[1]], ob[sl[1]]),
                 jnp.where(lane_lo, ob[sl[2]], oa[sl[2]]), jnp.where(lane_lo, ob[sl[3]], oa[sl[3]]))
        for p in range(4):
            mix_scr[rows, A_WIDTH + p * LANES:A_WIDTH + (p + 1) * LANES] = tiles[p]

    last = slice(TM - WINDOW, TM)
    kprev[...] = k[last]
    vprev[...] = v[last]
    kl_ref[...] = k[last]
    vl_ref[...] = v[last]
    val_ref[...] = va[last]

    x1 = x + _dot(mix_scr[...].astype(BF16), wout_ref[...])
    x1_ref[...] = x1
    h, ids, gates = _route(x1, nf_ref[...], wr_ref[...], br_ref[...])
    h_ref[...] = h.reshape(h_ref.shape)
    ri_ref[...] = ids
    rg_ref[...] = gates


def _const_spec(shape):
    nd = len(shape)
    return pl.BlockSpec(shape, lambda i, _n=nd: (0,) * _n)


def _mix0_prompt(x_all, nm, win, lng, lnb, wsp, bs_full, bias_p, sink_p, wout, nf, wr, br):
    row_spec = pl.BlockSpec((TM, D_MODEL), lambda i: (i, 0))
    row3_spec = pl.BlockSpec((TM, ROW_TILE, LANES), lambda i: (i, 0, 0))
    lane_spec = pl.BlockSpec((TM, LANES), lambda i: (i, 0))
    last_kv = pl.BlockSpec((None, WINDOW, KV_WIDTH), lambda i: (jnp.minimum(i // STEPS_PER_BATCH, BATCH - 1), 0, 0))
    last_va = pl.BlockSpec((None, WINDOW, A_WIDTH), lambda i: (jnp.minimum(i // STEPS_PER_BATCH, BATCH - 1), 0, 0))
    return pl.pallas_call(
        _prompt_steps(_mix0_prompt_kernel, 13),
        grid=(N_ROW_BLOCKS,),
        in_specs=[pl.BlockSpec((TM, D_MODEL), lambda i: (jnp.minimum(i, N_PROMPT_BLOCKS - 1), 0)),
                  _const_spec((1, D_MODEL)), _const_spec((D_MODEL, IN_WIDTH)),
                  _const_spec((1, A_WIDTH)), _const_spec((1, A_WIDTH)),
                  _const_spec((A_HEADS // 2, CHUNK, 2 * CHUNK)), _const_spec((CHUNK, A_WIDTH)),
                  _const_spec((2, 2, 4 * WINDOW, 2 * WINDOW)), _const_spec((2, 4 * WINDOW, 1)),
                  _const_spec((A_WIDTH + Q_WIDTH, D_MODEL)), _const_spec((1, D_MODEL)),
                  _const_spec((D_MODEL, LANES)), _const_spec((1, LANES))],
        out_specs=[row_spec, row3_spec, lane_spec, lane_spec, last_kv, last_kv, last_va],
        out_shape=[jax.ShapeDtypeStruct((T_ALL, D_MODEL), F32), jax.ShapeDtypeStruct((T_ALL, ROW_TILE, LANES), F32),
                   jax.ShapeDtypeStruct((T_ALL, LANES), jnp.int32), jax.ShapeDtypeStruct((T_ALL, LANES), F32),
                   jax.ShapeDtypeStruct((BATCH, WINDOW, KV_WIDTH), F32),
                   jax.ShapeDtypeStruct((BATCH, WINDOW, KV_WIDTH), F32),
                   jax.ShapeDtypeStruct((BATCH, WINDOW, A_WIDTH), F32)],
        scratch_shapes=[pltpu.VMEM((WINDOW, KV_WIDTH), F32), pltpu.VMEM((WINDOW, KV_WIDTH), F32),
                        pltpu.VMEM((TM, D_MODEL), F32)],
        compiler_params=_cparams(("arbitrary",)),
        name="mix0_prompt",
    )(x_all, nm, win, lng, lnb, wsp, bs_full, bias_p, sink_p, wout, nf, wr, br)


def _mix0_sample_kernel(x_ref, nm_ref, win_ref, lng_ref, lnb_ref, wcoef_ref, bcoef_ref,
                        ck_ref, cv_ref, bsc_ref, bsn_ref, sink_ref,
                        wout_ref, nf_ref, wr_ref, br_ref,
                        x1_in, h_in, ri_in, rg_in,
                        x1_ref, h_ref, ri_ref, rg_ref, kn_ref, vn_ref, va_ref,
                        q_scr, k_scr, v_scr, mix_scr):
    del x1_in, h_in, ri_in, rg_in
    g = pl.program_id(0)

    @pl.when(g == 0)
    def _():
        u, va, q, k, v = _project(x_ref[...], nm_ref[...], win_ref[...], lng_ref[...], lnb_ref[...])
        q_scr[...] = q
        k_scr[...] = k
        v_scr[...] = v
        kn_ref[...] = k
        vn_ref[...] = v
        va_ref[...] = va
        idx = 0
        for t in range(DEC_SEQ):
            acc = jnp.zeros((DEC_BATCH, A_WIDTH), F32) + bcoef_ref[t:t + 1, :]
            for s in range(t + 1):
                acc = acc + wcoef_ref[idx:idx + 1, :] * va[s * DEC_BATCH:(s + 1) * DEC_BATCH]
                idx += 1
            mix_scr[t * DEC_BATCH:(t + 1) * DEC_BATCH, :A_WIDTH] = u[t * DEC_BATCH:(t + 1) * DEC_BATCH] * acc

    b0 = pl.multiple_of(g * SAMPLE_GROUP, SAMPLE_GROUP)
    lane_lo = lax.broadcasted_iota(jnp.int32, (DEC_SEQ * SAMPLE_GROUP, LANES), 1) < B_HEAD_DIM

    def grab(ref, width):
        return jnp.concatenate([ref[pl.ds(t * DEC_BATCH + b0, SAMPLE_GROUP), :] for t in range(DEC_SEQ)], axis=0)

    qg = grab(q_scr, Q_WIDTH)
    kn = grab(k_scr, KV_WIDTH)
    vn = grab(v_scr, KV_WIDTH)
    kc = ck_ref[...]
    vc = cv_ref[...]
    kc_ops = (kc.astype(BF16), pltpu.roll(kc, B_HEAD_DIM, 1).astype(BF16))
    vc_ops = (vc.astype(BF16), pltpu.roll(vc, B_HEAD_DIM, 1).astype(BF16))
    kn_ops = (kn.astype(BF16), pltpu.roll(kn, B_HEAD_DIM, 1).astype(BF16))
    vn_ops = (vn.astype(BF16), pltpu.roll(vn, B_HEAD_DIM, 1).astype(BF16))
    qt = [qg[:, p * LANES:(p + 1) * LANES] for p in range(4)]
    q_even = [jnp.where(lane_lo, t, 0.0) for t in qt]
    q_odd = [jnp.where(lane_lo, 0.0, t) for t in qt]
    stacks = (jnp.concatenate([q_even[0], q_even[1], q_odd[2], q_odd[3]], axis=0),
              jnp.concatenate([q_odd[0], q_odd[1], q_even[2], q_even[3]], axis=0))
    o = []
    for st in range(2):
        qs = stacks[st].astype(BF16)
        sc = _dot_nt(qs, kc_ops[st]) + bsc_ref[st]
        sn = _dot_nt(qs, kn_ops[st]) + bsn_ref[st][:, :DEC_SEQ * SAMPLE_GROUP]
        sink = sink_ref[st]
        m = jnp.maximum(jnp.maximum(jnp.max(sc, axis=-1, keepdims=True), jnp.max(sn, axis=-1, keepdims=True)), sink)
        pc = jnp.exp(sc - m)
        pn = jnp.exp(sn - m)
        den = jnp.sum(pc, axis=-1, keepdims=True) + jnp.sum(pn, axis=-1, keepdims=True) + jnp.exp(sink - m)
        o.append((_dot(pc.astype(BF16), vc_ops[st]) + _dot(pn.astype(BF16), vn_ops[st])) / den)
    oa, ob = o
    n = DEC_SEQ * SAMPLE_GROUP
    sl = [slice(i * n, (i + 1) * n) for i in range(4)]
    tiles = (jnp.where(lane_lo, oa[sl[0]], ob[sl[0]]), jnp.where(lane_lo, oa[sl[1]], ob[sl[1]]),
             jnp.where(lane_lo, ob[sl[2]], oa[sl[2]]), jnp.where(lane_lo, ob[sl[3]], oa[sl[3]]))
    for p in range(4):
        for t in range(DEC_SEQ):
            mix_scr[pl.ds(t * DEC_BATCH + b0, SAMPLE_GROUP), A_WIDTH + p * LANES:A_WIDTH + (p + 1) * LANES] = (
                tiles[p][t * SAMPLE_GROUP:(t + 1) * SAMPLE_GROUP])

    @pl.when(g == N_SAMPLE_GROUPS - 1)
    def _():
        x1 = x_ref[...] + _dot(mix_scr[...].astype(BF16), wout_ref[...])
        x1_ref[...] = x1
        h, ids, gates = _route(x1, nf_ref[...], wr_ref[...], br_ref[...])
        h_ref[...] = h.reshape(h_ref.shape)
        ri_ref[...] = ids
        rg_ref[...] = gates


def _mix0_sample(x_all, nm, win, lng, lnb, wcoef, bcoef, ck, cv, bias_sc, bias_sn, sink_s, wout, nf, wr, br,
                 x1_all, h_all, ri_all, rg_all):
    sample_rows = pl.BlockSpec((TM, D_MODEL), lambda g: (N_PROMPT_BLOCKS, 0))
    sample_rows3 = pl.BlockSpec((TM, ROW_TILE, LANES), lambda g: (N_PROMPT_BLOCKS, 0, 0))
    sample_lanes = pl.BlockSpec((TM, LANES), lambda g: (N_PROMPT_BLOCKS, 0))
    cache_spec = pl.BlockSpec((SAMPLE_GROUP * WINDOW, KV_WIDTH), lambda g: (g, 0))
    anyspec = pl.BlockSpec(memory_space=pl.ANY)
    n_in = 16
    return pl.pallas_call(
        _mix0_sample_kernel,
        grid=(N_SAMPLE_GROUPS,),
        in_specs=[_const_spec((TM, D_MODEL)), _const_spec((1, D_MODEL)), _const_spec((D_MODEL, IN_WIDTH)),
                  _const_spec((1, A_WIDTH)), _const_spec((1, A_WIDTH)),
                  _const_spec((16, A_WIDTH)), _const_spec((8, A_WIDTH)),
                  cache_spec, cache_spec,
                  _const_spec((2, 4 * 32, SAMPLE_GROUP * WINDOW)), _const_spec((2, 4 * 32, LANES)),
                  _const_spec((2, 4 * 32, 1)),
                  _const_spec((A_WIDTH + Q_WIDTH, D_MODEL)), _const_spec((1, D_MODEL)),
                  _const_spec((D_MODEL, LANES)), _const_spec((1, LANES)),
                  anyspec, anyspec, anyspec, anyspec],
        out_specs=[sample_rows, sample_rows3, sample_lanes, sample_lanes,
                   _const_spec((T_SAMPLE, KV_WIDTH)), _const_spec((T_SAMPLE, KV_WIDTH)),
                   _const_spec((T_SAMPLE, A_WIDTH))],
        out_shape=[jax.ShapeDtypeStruct((T_ALL, D_MODEL), F32), jax.ShapeDtypeStruct((T_ALL, ROW_TILE, LANES), F32),
                   jax.ShapeDtypeStruct((T_ALL, LANES), jnp.int32), jax.ShapeDtypeStruct((T_ALL, LANES), F32),
                   jax.ShapeDtypeStruct((T_SAMPLE, KV_WIDTH), F32), jax.ShapeDtypeStruct((T_SAMPLE, KV_WIDTH), F32),
                   jax.ShapeDtypeStruct((T_SAMPLE, A_WIDTH), F32)],
        scratch_shapes=[pltpu.VMEM((T_SAMPLE, Q_WIDTH), F32), pltpu.VMEM((T_SAMPLE, KV_WIDTH), F32),
                        pltpu.VMEM((T_SAMPLE, KV_WIDTH), F32), pltpu.VMEM((T_SAMPLE, D_MODEL), F32)],
        input_output_aliases={n_in: 0, n_in + 1: 1, n_in + 2: 2, n_in + 3: 3},
        compiler_params=_cparams(("arbitrary",)),
        name="mix0_sample",
    )(x_all, nm, win, lng, lnb, wcoef, bcoef, ck, cv, bias_sc, bias_sn, sink_s, wout, nf, wr, br,
      x1_all, h_all, ri_all, rg_all)


def _moe_metadata(ri_all):
    flat = ri_all[:, :TOP_K].T.reshape(N_SLOTS)
    order = jnp.argsort(flat).astype(jnp.int32)
    counts = jnp.sum((flat[:, None] == jnp.arange(N_EXPERTS, dtype=jnp.int32)[None, :]).astype(jnp.int32), axis=0)
    start = jnp.cumsum(counts) - counts
    padded = (counts + MOE_BLK - 1) // MOE_BLK * MOE_BLK
    pad_end = jnp.cumsum(padded)
    pad_start = pad_end - padded
    n_valid = (pad_end[-1] // MOE_BLK).astype(jnp.int32).reshape(1)
    blk_row0 = jnp.arange(N_MOE_BLOCKS, dtype=jnp.int32) * MOE_BLK
    block_e = jnp.minimum(jnp.sum((blk_row0[:, None] >= pad_end[None, :]).astype(jnp.int32), axis=1),
                          N_EXPERTS - 1).astype(jnp.int32)
    in_expert = blk_row0 - pad_start[block_e]
    src0 = (start[block_e] + in_expert).astype(jnp.int32)
    n_rows = jnp.clip(counts[block_e] - in_expert, 1, MOE_BLK).astype(jnp.int32)
    return order, block_e, src0, n_rows, n_valid


def _experts_kernel(order_ref, be_ref, src0_ref, nrows_ref, nv_ref,
                    h_ref, wg_ref, wu_ref, wd_ref, ys_ref,
                    wg_s, wu_s, wd_s, xbuf, ybuf, gsem, ssem):
    i = pl.program_id(0)
    n_valid = nv_ref[0]

    def gather(b, do):
        par = b % 2
        base = src0_ref[b]
        last = nrows_ref[b] - 1

        def body(j, c):
            slot = order_ref[base + jnp.minimum(j, last)]
            tok = jnp.where(slot >= T_ALL, slot - T_ALL, slot)
            do(pltpu.make_async_copy(h_ref.at[tok], xbuf.at[par, j], gsem.at[par]))
            return c

        lax.fori_loop(0, MOE_BLK, body, 0)

    def scatter_copy(b, j):
        par = b % 2
        last = nrows_ref[b] - 1
        slot = order_ref[src0_ref[b] + jnp.minimum(j, last)]
        dst = jnp.where(j <= last, slot, N_SLOTS + par * MOE_BLK + j)
        return pltpu.make_async_copy(ybuf.at[par, j], ys_ref.at[dst], ssem.at[par])

    def scatter_start(b):
        def body(j, c):
            scatter_copy(b, j).start(priority=1)
            return c
        lax.fori_loop(0, MOE_BLK, body, 0)

    def scatter_wait(par):
        pltpu.make_async_copy(ybuf.at[par], ys_ref.at[pl.ds(0, MOE_BLK)], ssem.at[par]).wait()

    @pl.when(i == 0)
    def _():
        ybuf[...] = jnp.zeros_like(ybuf)
        for par in range(2):
            cp = pltpu.make_async_copy(ybuf.at[par], ys_ref.at[pl.ds(N_SLOTS + par * MOE_BLK, MOE_BLK)], ssem.at[par])
            cp.start()
            cp.wait()
        gather(i, lambda cp: cp.start())

    @pl.when(i < n_valid)
    def _():
        @pl.when(i + 1 < n_valid)
        def _():
            gather(i + 1, lambda cp: cp.start())

        e = be_ref[i]
        prev = be_ref[jnp.maximum(i - 1, 0)]

        @pl.when((i == 0) | (e != prev))
        def _():
            wg_s[...] = wg_ref[...].astype(BF16)
            wu_s[...] = wu_ref[...].astype(BF16)
            wd_s[...] = wd_ref[...].astype(BF16)

        par = i % 2
        pltpu.make_async_copy(h_ref.at[pl.ds(0, MOE_BLK)], xbuf.at[par], gsem.at[par]).wait()
        xb = xbuf[par].reshape(MOE_BLK, D_MODEL).astype(BF16)
        a = jax.nn.silu(_dot(xb, wg_s[...])) * _dot(xb, wu_s[...])
        y = _dot(a.astype(BF16), wd_s[...])

        @pl.when(i >= 2)
        def _():
            scatter_wait(par)

        ybuf[par] = y.reshape(MOE_BLK, ROW_TILE, LANES)
        scatter_start(i)

        @pl.when(i == n_valid - 1)
        def _():
            scatter_wait(par)

            @pl.when(i >= 1)
            def _():
                scatter_wait(1 - par)


def _experts(order, block_e, src0, n_rows, n_valid, h_all, w_gate, w_up, w_down, layer):
    def wmap(i, o, be, s0, nr, nv):
        return (layer, be[jnp.minimum(i, nv[0] - 1)], 0, 0)

    return pl.pallas_call(
        _experts_kernel,
        grid_spec=pltpu.PrefetchScalarGridSpec(
            num_scalar_prefetch=5,
            grid=(N_MOE_BLOCKS,),
            in_specs=[pl.BlockSpec(memory_space=pl.ANY),
                      pl.BlockSpec((None, None, D_MODEL, D_EXPERT), wmap),
                      pl.BlockSpec((None, None, D_MODEL, D_EXPERT), wmap),
                      pl.BlockSpec((None, None, D_EXPERT, D_MODEL), wmap)],
            out_specs=pl.BlockSpec(memory_space=pl.ANY),
            scratch_shapes=[pltpu.VMEM((D_MODEL, D_EXPERT), BF16), pltpu.VMEM((D_MODEL, D_EXPERT), BF16),
                            pltpu.VMEM((D_EXPERT, D_MODEL), BF16),
                            pltpu.VMEM((2, MOE_BLK, ROW_TILE, LANES), F32),
                            pltpu.VMEM((2, MOE_BLK, ROW_TILE, LANES), F32),
                            pltpu.SemaphoreType.DMA((2,)), pltpu.SemaphoreType.DMA((2,))],
        ),
        out_shape=jax.ShapeDtypeStruct((N_SLOTS + 2 * MOE_BLK, ROW_TILE, LANES), F32),
        compiler_params=_cparams(("arbitrary",)),
        name="moe_experts",
    )(order, block_e, src0, n_rows, n_valid, h_all, w_gate, w_up, w_down)


def _combined(x_ref, rg_ref, y0_ref, y1_ref):
    rg = rg_ref[...]
    y0 = y0_ref[...].reshape(TM, D_MODEL)
    y1 = y1_ref[...].reshape(TM, D_MODEL)
    return x_ref[...] + rg[:, 0:1] * y0 + rg[:, 1:2] * y1


def _combine_in_specs():
    return [pl.BlockSpec((TM, D_MODEL), lambda i: (i, 0)),
            pl.BlockSpec((TM, LANES), lambda i: (i, 0)),
            pl.BlockSpec((TM, ROW_TILE, LANES), lambda i: (i, 0, 0)),
            pl.BlockSpec((TM, ROW_TILE, LANES), lambda i: (N_ROW_BLOCKS + i, 0, 0))]


def _combine_kernel(x_ref, rg_ref, y0_ref, y1_ref, o_ref):
    o_ref[...] = _combined(x_ref, rg_ref, y0_ref, y1_ref)


def _combine(x_all, rg_all, ys):
    return pl.pallas_call(
        _combine_kernel,
        grid=(N_ROW_BLOCKS,),
        in_specs=_combine_in_specs(),
        out_specs=pl.BlockSpec((TM, D_MODEL), lambda i: (i, 0)),
        out_shape=jax.ShapeDtypeStruct((T_ALL, D_MODEL), F32),
        compiler_params=_cparams(("arbitrary",)),
        name="moe_combine",
    )(x_all, rg_all, ys, ys)


def _final_kernel(x_ref, rg_ref, y0_ref, y1_ref, nfin_ref, op_ref, os_ref):
    i = pl.program_id(0)
    y = _rms(_combined(x_ref, rg_ref, y0_ref, y1_ref), nfin_ref[...])

    @pl.when(i < N_PROMPT_BLOCKS)
    def _():
        op_ref[...] = y

    @pl.when(i >= N_PROMPT_BLOCKS)
    def _():
        os_ref[...] = y


def _final(x_all, rg_all, ys, nfin):
    return pl.pallas_call(
        _final_kernel,
        grid=(N_ROW_BLOCKS,),
        in_specs=_combine_in_specs() + [_const_spec((1, D_MODEL))],
        out_specs=[pl.BlockSpec((TM, D_MODEL), lambda i: (jnp.minimum(i, N_PROMPT_BLOCKS - 1), 0)),
                   pl.BlockSpec((TM, D_MODEL), lambda i: (0, 0))],
        out_shape=[jax.ShapeDtypeStruct((T_PROMPT, D_MODEL), F32), jax.ShapeDtypeStruct((T_SAMPLE, D_MODEL), F32)],
        compiler_params=_cparams(("arbitrary",)),
        name="moe_combine_final",
    )(x_all, rg_all, ys, ys, nfin)


def _moe(h_all, ri_all, w_gate, w_up, w_down, layer):
    order, block_e, src0, n_rows, n_valid = _moe_metadata(ri_all)
    return _experts(order, block_e, src0, n_rows, n_valid, h_all, w_gate, w_up, w_down, layer)


def _pool_project(d_groups, wp_ref, scale):
    outs = [_dot(d_groups[g].astype(BF16), wp_ref[g]) for g in range(len(POOL_SIZES))]
    return jnp.concatenate(outs, axis=1) * scale


def _mix1_prompt_kernel(x_ref, nm_ref, wp_ref, sc_ref, nf_ref, wr_ref, br_ref,
                        x3_ref, h_ref, ri_ref, rg_ref, pl_ref, ext):
    i = pl.program_id(0)
    x = x_ref[...]
    hp = _rms(x, nm_ref[...])

    @pl.when(i % STEPS_PER_BATCH == 0)
    def _():
        ext[0:POOL_MAX, :] = jnp.zeros((POOL_MAX, D_MODEL), F32)

    ext[POOL_MAX:, :] = hp
    pos = (i % STEPS_PER_BATCH) * TM + lax.broadcasted_iota(jnp.int32, (TM, 1), 0)
    d_groups = []
    for g, w in enumerate(POOL_SIZES):
        cols = slice(g * POOL_GROUP_DIM, (g + 1) * POOL_GROUP_DIM)
        acc = ext[:, cols]
        span = 1
        while span < w:
            acc = acc + pltpu.roll(acc, span, 0)
            span *= 2
        cnt = jnp.minimum(pos + 1, w).astype(F32)
        d_groups.append(acc[POOL_MAX:] / cnt - hp[:, cols])
    tail = hp[TM - POOL_MAX:, :]
    ext[0:POOL_MAX, :] = tail
    pl_ref[...] = tail

    x3 = x + _pool_project(d_groups, wp_ref, sc_ref[...])
    x3_ref[...] = x3
    h, ids, gates = _route(x3, nf_ref[...], wr_ref[...], br_ref[...])
    h_ref[...] = h.reshape(h_ref.shape)
    ri_ref[...] = ids
    rg_ref[...] = gates


def _mix1_prompt(x_all, nm, wp, sc, nf, wr, br):
    row_spec = pl.BlockSpec((TM, D_MODEL), lambda i: (i, 0))
    row3_spec = pl.BlockSpec((TM, ROW_TILE, LANES), lambda i: (i, 0, 0))
    lane_spec = pl.BlockSpec((TM, LANES), lambda i: (i, 0))
    return pl.pallas_call(
        _prompt_steps(_mix1_prompt_kernel, 7),
        grid=(N_ROW_BLOCKS,),
        in_specs=[row_spec, _const_spec((1, D_MODEL)),
                  _const_spec((len(POOL_SIZES), POOL_GROUP_DIM, POOL_GROUP_DIM)), _const_spec((1, D_MODEL)),
                  _const_spec((1, D_MODEL)), _const_spec((D_MODEL, LANES)), _const_spec((1, LANES))],
        out_specs=[row_spec, row3_spec, lane_spec, lane_spec,
                   pl.BlockSpec((None, POOL_MAX, D_MODEL),
                                lambda i: (jnp.minimum(i // STEPS_PER_BATCH, BATCH - 1), 0, 0))],
        out_shape=[jax.ShapeDtypeStruct((T_ALL, D_MODEL), F32), jax.ShapeDtypeStruct((T_ALL, ROW_TILE, LANES), F32),
                   jax.ShapeDtypeStruct((T_ALL, LANES), jnp.int32), jax.ShapeDtypeStruct((T_ALL, LANES), F32),
                   jax.ShapeDtypeStruct((BATCH, POOL_MAX, D_MODEL), F32)],
        scratch_shapes=[pltpu.VMEM((POOL_MAX + TM, D_MODEL), F32)],
        compiler_params=_cparams(("arbitrary",)),
        name="mix1_prompt",
    )(x_all, nm, wp, sc, nf, wr, br)


def _mix1_sample_kernel(x_ref, st_ref, nm_ref, wp_ref, sc_ref, nf_ref, wr_ref, br_ref,
                        x3_in, h_in, ri_in, rg_in,
                        x3_ref, h_ref, ri_ref, rg_ref, hs_ref):
    del x3_in, h_in, ri_in, rg_in
    x = x_ref[...]
    hs = _rms(x, nm_ref[...])
    hs_ref[...] = hs
    n_ctx = POOL_MAX - 1
    d_groups = []
    for g, w in enumerate(POOL_SIZES):
        cols = slice(g * POOL_GROUP_DIM, (g + 1) * POOL_GROUP_DIM)
        parts = []
        for t in range(DEC_SEQ):
            acc = hs[t * DEC_BATCH:(t + 1) * DEC_BATCH, cols]
            for back in range(1, w):
                src = t - back
                if src >= 0:
                    acc = acc + hs[src * DEC_BATCH:(src + 1) * DEC_BATCH, cols]
                else:
                    acc = acc + st_ref[n_ctx + src, :, cols]
            parts.append(acc / float(w) - hs[t * DEC_BATCH:(t + 1) * DEC_BATCH, cols])
        d_groups.append(jnp.concatenate(parts, axis=0))
    x3 = x + _pool_project(d_groups, wp_ref, sc_ref[...])
    x3_ref[...] = x3
    h, ids, gates = _route(x3, nf_ref[...], wr_ref[...], br_ref[...])
    h_ref[...] = h.reshape(h_ref.shape)
    ri_ref[...] = ids
    rg_ref[...] = gates


def _mix1_sample(x_all, state_t, nm, wp, sc, nf, wr, br, x3_all, h_all, ri_all, rg_all):
    sample_rows = pl.BlockSpec((TM, D_MODEL), lambda g: (N_PROMPT_BLOCKS, 0))
    sample_rows3 = pl.BlockSpec((TM, ROW_TILE, LANES), lambda g: (N_PROMPT_BLOCKS, 0, 0))
    sample_lanes = pl.BlockSpec((TM, LANES), lambda g: (N_PROMPT_BLOCKS, 0))
    anyspec = pl.BlockSpec(memory_space=pl.ANY)
    n_in = 8
    return pl.pallas_call(
        _mix1_sample_kernel,
        grid=(1,),
        in_specs=[sample_rows, _const_spec((POOL_MAX - 1, DEC_BATCH, D_MODEL)), _const_spec((1, D_MODEL)),
                  _const_spec((len(POOL_SIZES), POOL_GROUP_DIM, POOL_GROUP_DIM)), _const_spec((1, D_MODEL)),
                  _const_spec((1, D_MODEL)), _const_spec((D_MODEL, LANES)), _const_spec((1, LANES)),
                  anyspec, anyspec, anyspec, anyspec],
        out_specs=[sample_rows, sample_rows3, sample_lanes, sample_lanes, _const_spec((T_SAMPLE, D_MODEL))],
        out_shape=[jax.ShapeDtypeStruct((T_ALL, D_MODEL), F32), jax.ShapeDtypeStruct((T_ALL, ROW_TILE, LANES), F32),
                   jax.ShapeDtypeStruct((T_ALL, LANES), jnp.int32), jax.ShapeDtypeStruct((T_ALL, LANES), F32),
                   jax.ShapeDtypeStruct((T_SAMPLE, D_MODEL), F32)],
        input_output_aliases={n_in: 0, n_in + 1: 1, n_in + 2: 2, n_in + 3: 3},
        compiler_params=_cparams(("arbitrary",)),
        name="mix1_sample",
    )(x_all, state_t, nm, wp, sc, nf, wr, br, x3_all, h_all, ri_all, rg_all)


def _router_weights(wg, bg, we, be):
    w = jnp.concatenate([wg, jnp.transpose(we, (1, 0, 2)).reshape(D_MODEL, N_EXPERTS)], axis=1)
    b = jnp.concatenate([bg, be.reshape(N_EXPERTS)])
    pad = LANES - N_GROUPS - N_EXPERTS
    return jnp.pad(w, ((0, 0), (0, pad))), jnp.pad(b, (0, pad)).reshape(1, LANES)


def _stack(tab):
    return jnp.stack([jnp.concatenate([tab[h] for h in heads], axis=0) for heads in STACK_HEADS])


def kernel(x_prompt, x_sample, cache_k_win, cache_v_win, state_pool, norm_mix, norm_ffn, norm_final, w_in,
           a_ln_g, a_ln_b, a_w_s, a_b_s, b_sinks, rel_bias_table, w_out, c_w_pool, c_scale,
           router_group_w, router_group_b, router_expert_w, router_expert_b, w_gate, w_up, w_down):
    xs_t = jnp.transpose(x_sample, (1, 0, 2)).reshape(T_SAMPLE, D_MODEL)
    xp2 = x_prompt.reshape(T_PROMPT, D_MODEL)
    win =w_in[0].astype(BF16)
    wout = w_out[0].astype(BF16)
    lng = a_ln_g[0].reshape(1, A_WIDTH)
    lnb = a_ln_b[0].reshape(1, A_WIDTH)
    bias_p, bias_sc, bias_sn, ws_tril = _prep(rel_bias_table, a_w_s[0])
    wsp = ws_tril.reshape(A_HEADS // 2, 2, CHUNK, CHUNK).transpose(0, 2, 1, 3).reshape(A_HEADS // 2, CHUNK, 2 * CHUNK)
    bs_full = jnp.repeat(a_b_s[0].T, A_HEAD_DIM, axis=1)
    bias_p = jnp.stack([_stack(bias_p[0]), _stack(bias_p[1])])
    bias_sc = _stack(bias_sc)
    bias_sn = _stack(bias_sn)
    sinks = b_sinks[0]
    sink_p = jnp.stack([jnp.repeat(sinks[jnp.array(hh)], WINDOW) for hh in STACK_HEADS]).reshape(2, 4 * WINDOW, 1)
    sink_s = jnp.stack([jnp.repeat(sinks[jnp.array(hh)], 32) for hh in STACK_HEADS]).reshape(2, 4 * 32, 1)
    pairs = [(t, s) for t in range(DEC_SEQ) for s in range(t + 1)]
    wcoef = jnp.stack([jnp.repeat(a_w_s[0][:, t, s], A_HEAD_DIM) for t, s in pairs])
    wcoef = jnp.pad(wcoef, ((0, 16 - len(pairs)), (0, 0)))
    bcoef = jnp.pad(jnp.repeat(a_b_s[0][:, :DEC_SEQ].T, A_HEAD_DIM, axis=1), ((0, 8 - DEC_SEQ), (0, 0)))
    ck = cache_k_win[0].reshape(DEC_BATCH * WINDOW, KV_WIDTH)
    cv = cache_v_win[0].reshape(DEC_BATCH * WINDOW, KV_WIDTH)
    routers = [_router_weights(router_group_w[l], router_group_b[l], router_expert_w[l], router_expert_b[l])
               for l in range(2)]
    nm = [norm_mix[l].reshape(1, D_MODEL) for l in range(2)]
    nf = [norm_ffn[l].reshape(1, D_MODEL) for l in range(2)]

    x1_all, h_all, ri_all, rg_all, k_last, v_last, va_last = _mix0_prompt(
        xp2, nm[0], win, lng, lnb, wsp, bs_full, bias_p, sink_p, wout, nf[0], *routers[0])
    x1_all, h_all, ri_all, rg_all, k_new, v_new, va_s = _mix0_sample(
        xs_t, nm[0], win, lng, lnb, wcoef, bcoef, ck, cv, bias_sc, bias_sn, sink_s, wout, nf[0], *routers[0],
        x1_all, h_all, ri_all, rg_all)
    ys0 = _moe(h_all, ri_all, w_gate, w_up, w_down, 0)
    x2_all = _combine(x1_all, rg_all, ys0)

    wp = c_w_pool[0].astype(BF16)
    sc = c_scale[0].reshape(1, D_MODEL)
    x3_all, h2_all, ri2_all, rg2_all, pool_tail = _mix1_prompt(x2_all, nm[1], wp, sc, nf[1], *routers[1])
    state_t = jnp.transpose(state_pool[0], (1, 0, 2))
    x3_all, h2_all, ri2_all, rg2_all, hs1 = _mix1_sample(
        x2_all, state_t, nm[1], wp, sc, nf[1], *routers[1], x3_all, h2_all, ri2_all, rg2_all)
    ys1 = _moe(h2_all, ri2_all, w_gate, w_up, w_down, 1)
    y_p, y_s = _final(x3_all, rg2_all, ys1, norm_final.reshape(1, D_MODEL))

    def from_tmajor(a, width):
        return jnp.transpose(a.reshape(DEC_SEQ, DEC_BATCH, width), (1, 0, 2))

    y_prompt = y_p.reshape(BATCH, SEQ, D_MODEL)
    y_sample = from_tmajor(y_s, D_MODEL)
    win_k_p = k_last.reshape(1, BATCH, WINDOW, B_KV_HEADS, B_HEAD_DIM)
    win_v_p = v_last.reshape(1, BATCH, WINDOW, B_KV_HEADS, B_HEAD_DIM)
    kn = from_tmajor(k_new, KV_WIDTH).reshape(DEC_BATCH, DEC_SEQ, B_KV_HEADS, B_HEAD_DIM)
    vn = from_tmajor(v_new, KV_WIDTH).reshape(DEC_BATCH, DEC_SEQ, B_KV_HEADS, B_HEAD_DIM)
    win_k_s = jnp.concatenate([cache_k_win[0][:, DEC_SEQ:], kn], axis=1)[None]
    win_v_s = jnp.concatenate([cache_v_win[0][:, DEC_SEQ:], vn], axis=1)[None]
    chunk_v_p = va_last.reshape(1, BATCH, CHUNK, A_HEADS, A_HEAD_DIM)
    chunk_v_s = from_tmajor(va_s, A_WIDTH).reshape(1, DEC_BATCH, DEC_SEQ, A_HEADS, A_HEAD_DIM)
    pool_p = pool_tail[:, 1:][None]
    pool_s = jnp.concatenate([state_pool[0][:, DEC_SEQ:], from_tmajor(hs1, D_MODEL)], axis=1)[None]
    return (y_prompt, y_sample, win_k_p, win_v_p, win_k_s, win_v_s, chunk_v_p, chunk_v_s, pool_p, pool_s)
```

```python
import functools
import math

import numpy as np
import jax
import jax.numpy as jnp
from jax import lax
from jax.experimental import pallas as pl
from jax.experimental.pallas import tpu as pltpu

F32 = jnp.float32
BF16 = jnp.bfloat16

D_MODEL = 1024
BATCH = 2
SEQ = 8192
DEC_BATCH = 128
DEC_SEQ = 4
A_WIDTH = 512
A_HEADS = 8
A_HEAD_DIM = 64
CHUNK = 128
B_HEADS = 8
B_KV_HEADS = 2
B_HEAD_DIM = 64
B_GROUP = 4
WINDOW = 128
N_BUCKETS = 32
MAX_DISTANCE = WINDOW
Q_WIDTH = 512
KV_WIDTH = 128
IN_WIDTH = 2 * A_WIDTH + Q_WIDTH + 2 * KV_WIDTH
ATTN_SCALE = B_HEAD_DIM ** -0.5
NEG_INF = -1e30
POOL_SIZES = (2, 4, 8, 16)
POOL_GROUP_DIM = 256
POOL_MAX = 16
N_GROUPS = 4
EXPERTS_PER_GROUP = 8
N_EXPERTS = 32
TOP_K = 2
D_EXPERT = 512
EPS = 1e-6

LANES = 128
ROW_TILE = D_MODEL // LANES
T_PROMPT = BATCH * SEQ
T_SAMPLE = DEC_BATCH * DEC_SEQ
T_ALL = T_PROMPT + T_SAMPLE
TM = 512
N_PROMPT_BLOCKS = T_PROMPT // TM
N_ROW_BLOCKS = T_ALL // TM
STEPS_PER_BATCH = SEQ // TM
SUB = TM // WINDOW
N_SLOTS = T_ALL * TOP_K
MOE_BLK = 256
N_MOE_BLOCKS = N_SLOTS // MOE_BLK + N_EXPERTS
N_SORT_ROWS = N_MOE_BLOCKS * MOE_BLK
SAMPLE_GROUP = 8
N_SAMPLE_GROUPS = DEC_BATCH // SAMPLE_GROUP
VMEM_LIMIT = 56 * 1024 * 1024

STACK_HEADS = ((0, 2, 5, 7), (1, 3, 4, 6))


def _t5_bucket_np(dist):
    n = np.maximum(dist, 0)
    max_exact = N_BUCKETS // 2
    nf = np.maximum(n, 1).astype(np.float32)
    large = max_exact + (np.log(nf / np.float32(max_exact)) / np.float32(math.log(MAX_DISTANCE / max_exact))
                         * np.float32(N_BUCKETS - max_exact)).astype(np.int32)
    large = np.minimum(large, N_BUCKETS - 1)
    return np.where(n < max_exact, n, large).astype(np.int32)


def _bucket_tables():
    qi = np.arange(WINDOW)[:, None]
    ki = np.arange(2 * WINDOW)[None, :]
    dist = qi + WINDOW - ki
    valid = (dist >= 0) & (dist < WINDOW)
    bp = np.where(valid, _t5_bucket_np(dist), -1)
    bp_first = np.where(ki >= WINDOW, bp, -1)
    bkt_p = np.stack([bp_first, bp]).astype(np.int32)

    t = np.repeat(np.arange(DEC_SEQ), SAMPLE_GROUP)[:, None]
    b = np.tile(np.arange(SAMPLE_GROUP), DEC_SEQ)[:, None]
    cb = np.repeat(np.arange(SAMPLE_GROUP), WINDOW)[None, :]
    cj = np.tile(np.arange(WINDOW), SAMPLE_GROUP)[None, :]
    dist_c = t + WINDOW - cj
    valid_c = (cb == b) & (dist_c >= 0) & (dist_c < WINDOW)
    bkt_sc = np.where(valid_c, _t5_bucket_np(dist_c), -1).astype(np.int32)
    nt = np.repeat(np.arange(DEC_SEQ), SAMPLE_GROUP)[None, :]
    nb = np.tile(np.arange(SAMPLE_GROUP), DEC_SEQ)[None, :]
    dist_n = t - nt
    valid_n = (nb == b) & (dist_n >= 0)
    bkt_sn = np.where(valid_n, _t5_bucket_np(dist_n), -1).astype(np.int32)
    bkt_sn = np.concatenate([bkt_sn, np.full((32, LANES - 32), -1, np.int32)], axis=1)
    return bkt_p, bkt_sc, bkt_sn


_BKT_P, _BKT_SC, _BKT_SN = _bucket_tables()


def _cparams(semantics):
    return pltpu.CompilerParams(dimension_semantics=semantics, vmem_limit_bytes=VMEM_LIMIT)


def _rms(x, g):
    return x * lax.rsqrt(jnp.mean(x * x, axis=-1, keepdims=True) + EPS) * g


def _layernorm(x, g, b):
    xc = x - jnp.mean(x, axis=-1, keepdims=True)
    return xc * lax.rsqrt(jnp.mean(xc * xc, axis=-1, keepdims=True) + EPS) * g + b


def _dot(a, b):
    return jnp.dot(a, b, preferred_element_type=F32)


def _dot_nt(a, b):
    return lax.dot_general(a, b, (((1,), (1,)), ((), ())), preferred_element_type=F32)


def _project(x, nm, win, lng, lnb):
    h = _rms(x, nm)
    z = _dot(h.astype(BF16), win)
    u = jax.nn.gelu(z[:, :A_WIDTH])
    va = _layernorm(jax.nn.gelu(z[:, A_WIDTH:2 * A_WIDTH]), lng, lnb)
    q = z[:, 2 * A_WIDTH:2 * A_WIDTH + Q_WIDTH] * ATTN_SCALE
    k = z[:, 2 * A_WIDTH + Q_WIDTH:2 * A_WIDTH + Q_WIDTH + KV_WIDTH]
    v = z[:, 2 * A_WIDTH + Q_WIDTH + KV_WIDTH:]
    return u, va, q, k, v


def _route(x1, nf, wr, br):
    h = _rms(x1, nf)
    logits = jnp.dot(h, wr, preferred_element_type=F32, precision=lax.Precision.HIGHEST) + br
    rows = logits.shape[0]
    lane = lax.broadcasted_iota(jnp.int32, (rows, LANES), 1)
    lanef = lane.astype(F32)
    big = jnp.float32(1e9)
    is_g = lane < N_GROUPS
    gl = jnp.where(is_g, logits, -jnp.inf)
    gmax = jnp.max(gl, axis=1, keepdims=True)
    gsel = jnp.min(jnp.where(gl == gmax, lanef, big), axis=1, keepdims=True)
    gsum = jnp.sum(jnp.where(is_g, jnp.exp(logits - gmax), 0.0), axis=1, keepdims=True)
    g1 = 1.0 / gsum
    lo = N_GROUPS + EXPERTS_PER_GROUP * gsel
    emask = (lanef >= lo) & (lanef < lo + EXPERTS_PER_GROUP)
    el = jnp.where(emask, logits, -jnp.inf)
    v1 = jnp.max(el, axis=1, keepdims=True)
    i1 = jnp.min(jnp.where(el == v1, lanef, big), axis=1, keepdims=True)
    el2 = jnp.where(lanef == i1, -jnp.inf, el)
    v2 = jnp.max(el2, axis=1, keepdims=True)
    i2 = jnp.min(jnp.where(el2 == v2, lanef, big), axis=1, keepdims=True)
    e2 = jnp.exp(v2 - v1)
    den = 1.0 + e2
    w1 = g1 / den
    w2 = g1 * e2 / den
    ids = jnp.where(lane == 0, i1 - N_GROUPS, jnp.where(lane == 1, i2 - N_GROUPS, 0.0)).astype(jnp.int32)
    gates = jnp.where(lane == 0, w1, jnp.where(lane == 1, w2, 0.0))
    return h, ids, gates


def _rank_pack(ids, cnt_ref):
    rows = ids.shape[0]
    lane = lax.broadcasted_iota(jnp.int32, (rows, LANES), 1)
    o0 = (lane == ids[:, 0:1]).astype(F32)
    o1 = (lane == ids[:, 1:2]).astype(F32)
    r = lax.broadcasted_iota(jnp.int32, (rows, rows), 0)
    c = lax.broadcasted_iota(jnp.int32, (rows, rows), 1)
    before = (c < r).astype(BF16)
    p0 = _dot(before, o0.astype(BF16))
    p1 = _dot(before, o1.astype(BF16))
    c0 = jnp.sum(o0, axis=0, keepdims=True)
    c1 = jnp.sum(o1, axis=0, keepdims=True)
    carry = cnt_ref[...]
    rank0 = jnp.sum(o0 * (carry + p0), axis=1, keepdims=True)
    rank1 = jnp.sum(o1 * (carry + c0 + p1), axis=1, keepdims=True)
    cnt_ref[...] = carry + c0 + c1
    idf = ids.astype(F32)
    packed = jnp.where(lane < TOP_K, idf, jnp.where(lane == 2, rank0, jnp.where(lane == 3, rank1, 0.0)))
    return jnp.transpose(packed)[:8].astype(jnp.int32)


def _prep_kernel(tab_ref, bp_ref, bsc_ref, bsn_ref, ws_ref, op_ref, osc_ref, osn_ref, ows_ref):
    def fill(bkt, write):
        for h in range(B_HEADS):
            acc = jnp.full(bkt.shape, NEG_INF, F32)
            for b in range(N_BUCKETS):
                acc = jnp.where(bkt == b, tab_ref[b, h], acc)
            write(h, acc)

    for var in range(2):
        def wr_p(h, acc, var=var):
            op_ref[var, h] = acc
        fill(bp_ref[var], wr_p)

    def wr_sc(h, acc):
        osc_ref[h] = acc
    fill(bsc_ref[...], wr_sc)

    def wr_sn(h, acc):
        osn_ref[h] = acc
    fill(bsn_ref[...], wr_sn)

    r = lax.broadcasted_iota(jnp.int32, (CHUNK, CHUNK), 0)
    c = lax.broadcasted_iota(jnp.int32, (CHUNK, CHUNK), 1)
    for h in range(A_HEADS):
        ows_ref[h] = jnp.where(r >= c, ws_ref[h], 0.0).astype(BF16)


def _prep(rel_bias_table, w_s):
    vm = pl.BlockSpec(memory_space=pltpu.VMEM)
    return pl.pallas_call(
        _prep_kernel,
        in_specs=[pl.BlockSpec(memory_space=pltpu.SMEM), vm, vm, vm, vm],
        out_specs=[vm, vm, vm, vm],
        out_shape=[
            jax.ShapeDtypeStruct((2, B_HEADS, WINDOW, 2 * WINDOW), F32),
            jax.ShapeDtypeStruct((B_HEADS, 32, SAMPLE_GROUP * WINDOW), F32),
            jax.ShapeDtypeStruct((B_HEADS, 32, LANES), F32),
            jax.ShapeDtypeStruct((A_HEADS, CHUNK, CHUNK), BF16),
        ],
        name="prep_tables",
    )(rel_bias_table, jnp.asarray(_BKT_P), jnp.asarray(_BKT_SC), jnp.asarray(_BKT_SN), w_s)


def _gate_pairs(va_rows, wsp_ref, lane_lo):
    outs = []
    for p in range(A_HEADS // 2):
        vp = va_rows[:, p * LANES:(p + 1) * LANES]
        rhs = jnp.concatenate([jnp.where(lane_lo, vp, 0.0), jnp.where(lane_lo, 0.0, vp)], axis=0).astype(BF16)
        outs.append(_dot(wsp_ref[p], rhs))
    return jnp.concatenate(outs, axis=1)


def _prompt_steps(body, first_row_out):
    def kern(*refs):
        i = pl.program_id(0)

        @pl.when(i < N_PROMPT_BLOCKS)
        def _():
            body(*refs)

        @pl.when(i >= N_PROMPT_BLOCKS)
        def _():
            for r in refs[first_row_out:first_row_out + 4]:
                r[...] = jnp.zeros(r.shape, r.dtype)

    return kern


def _mix0_prompt_kernel(x_ref, nm_ref, win_ref, lng_ref, lnb_ref, wsp_ref, bs_ref, bias_ref, sink_ref,
                        wout_ref, nf_ref, wr_ref, br_ref,
                        x1_ref, h_ref, ri_ref, rg_ref, kl_ref, vl_ref, val_ref, cnt_ref,
                        kprev, vprev, mix_scr):
    @pl.when(pl.program_id(0) == 0)
    def _():
        cnt_ref[...] = jnp.zeros_like(cnt_ref)

    x = x_ref[...]
    u, va, q, k, v = _project(x, nm_ref[...], win_ref[...], lng_ref[...], lnb_ref[...])
    lane_lo = lax.broadcasted_iota(jnp.int32, (WINDOW, LANES), 1) < B_HEAD_DIM
    first = pl.program_id(0) % STEPS_PER_BATCH == 0

    @pl.when(first)
    def _():
        kprev[...] = jnp.zeros_like(kprev)
        vprev[...] = jnp.zeros_like(vprev)

    for j in range(SUB):
        rows = slice(j * WINDOW, (j + 1) * WINDOW)
        s_gate = _gate_pairs(va[rows], wsp_ref, lane_lo)
        mix_scr[rows, :A_WIDTH] = u[rows] * (s_gate + bs_ref[...])

        if j == 0:
            kp, vp = kprev[...], vprev[...]
        else:
            prows = slice((j - 1) * WINDOW, j * WINDOW)
            kp, vp = k[prows], v[prows]
        kk = jnp.concatenate([kp, k[rows]], axis=0)
        vv = jnp.concatenate([vp, v[rows]], axis=0)
        kops = (kk.astype(BF16), pltpu.roll(kk, B_HEAD_DIM, 1).astype(BF16))
        vops = (vv.astype(BF16), pltpu.roll(vv, B_HEAD_DIM, 1).astype(BF16))
        qt = [q[rows, p * LANES:(p + 1) * LANES] for p in range(4)]
        q_even = [jnp.where(lane_lo, t, 0.0) for t in qt]
        q_odd = [jnp.where(lane_lo, 0.0, t) for t in qt]
        stacks = (jnp.concatenate([q_even[0], q_even[1], q_odd[2], q_odd[3]], axis=0),
                  jnp.concatenate([q_odd[0], q_odd[1], q_even[2], q_even[3]], axis=0))
        o = []
        for st in range(2):
            s = _dot_nt(stacks[st].astype(BF16), kops[st])
            if j == 0:
                bias = jnp.where(first, bias_ref[0, st], bias_ref[1, st])
            else:
                bias = bias_ref[1, st]
            s = s + bias
            sink = sink_ref[st]
            m = jnp.maximum(jnp.max(s, axis=-1, keepdims=True), sink)
            p = jnp.exp(s - m)
            den = jnp.sum(p, axis=-1, keepdims=True) + jnp.exp(sink - m)
            o.append(_dot(p.astype(BF16), vops[st]) / den)
        oa, ob = o
        sl = [slice(i * WINDOW, (i + 1) * WINDOW) for i in range(4)]
        tiles = (jnp.where(lane_lo, oa[sl[0]], ob[sl[0]]), jnp.where(lane_lo, oa[sl[1]], ob[sl[1]]),
                 jnp.where(lane_lo, ob[sl[2]], oa[sl[2]]), jnp.where(lane_lo, ob[sl[3]], oa[sl[3]]))
        for p in range(4):
            mix_scr[rows, A_WIDTH + p * LANES:A_WIDTH + (p + 1) * LANES] = tiles[p]

    last = slice(TM - WINDOW, TM)
    kprev[...] = k[last]
    vprev[...] = v[last]
    kl_ref[...] = k[last]
    vl_ref[...] = v[last]
    val_ref[...] = va[last]

    x1 = x + _dot(mix_scr[...].astype(BF16), wout_ref[...])
    x1_ref[...] = x1
    h, ids, gates = _route(x1, nf_ref[...], wr_ref[...], br_ref[...])
    h_ref[...] = h.reshape(h_ref.shape)
    ri_ref[...] = _rank_pack(ids, cnt_ref)
    rg_ref[...] = gates


def _const_spec(shape):
    nd = len(shape)
    return pl.BlockSpec(shape, lambda i, _n=nd: (0,) * _n)


def _mix0_prompt(x_all, nm, win, lng, lnb, wsp, bs_full, bias_p, sink_p, wout, nf, wr, br):
    row_spec = pl.BlockSpec((TM, D_MODEL), lambda i: (i, 0))
    row3_spec = pl.BlockSpec((TM, ROW_TILE, LANES), lambda i: (i, 0, 0))
    lane_spec = pl.BlockSpec((TM, LANES), lambda i: (i, 0))
    last_kv = pl.BlockSpec((None, WINDOW, KV_WIDTH), lambda i: (jnp.minimum(i // STEPS_PER_BATCH, BATCH - 1), 0, 0))
    last_va = pl.BlockSpec((None, WINDOW, A_WIDTH), lambda i: (jnp.minimum(i // STEPS_PER_BATCH, BATCH - 1), 0, 0))
    return pl.pallas_call(
        _prompt_steps(_mix0_prompt_kernel, 13),
        grid=(N_ROW_BLOCKS,),
        in_specs=[pl.BlockSpec((TM, D_MODEL), lambda i: (jnp.minimum(i, N_PROMPT_BLOCKS - 1), 0)),
                  _const_spec((1, D_MODEL)), _const_spec((D_MODEL, IN_WIDTH)),
                  _const_spec((1, A_WIDTH)), _const_spec((1, A_WIDTH)),
                  _const_spec((A_HEADS // 2, CHUNK, 2 * CHUNK)), _const_spec((CHUNK, A_WIDTH)),
                  _const_spec((2, 2, 4 * WINDOW, 2 * WINDOW)), _const_spec((2, 4 * WINDOW, 1)),
                  _const_spec((A_WIDTH + Q_WIDTH, D_MODEL)), _const_spec((1, D_MODEL)),
                  _const_spec((D_MODEL, LANES)), _const_spec((1, LANES))],
        out_specs=[row_spec, row3_spec, pl.BlockSpec((8, TM), lambda i: (0, i)), lane_spec,
                   last_kv, last_kv, last_va, _const_spec((1, LANES))],
        out_shape=[jax.ShapeDtypeStruct((T_ALL, D_MODEL), F32), jax.ShapeDtypeStruct((T_ALL, ROW_TILE, LANES), F32),
                   jax.ShapeDtypeStruct((8, T_ALL), jnp.int32), jax.ShapeDtypeStruct((T_ALL, LANES), F32),
                   jax.ShapeDtypeStruct((BATCH, WINDOW, KV_WIDTH), F32),
                   jax.ShapeDtypeStruct((BATCH, WINDOW, KV_WIDTH), F32),
                   jax.ShapeDtypeStruct((BATCH, WINDOW, A_WIDTH), F32),
                   jax.ShapeDtypeStruct((1, LANES), F32)],
        scratch_shapes=[pltpu.VMEM((WINDOW, KV_WIDTH), F32), pltpu.VMEM((WINDOW, KV_WIDTH), F32),
                        pltpu.VMEM((TM, D_MODEL), F32)],
        compiler_params=_cparams(("arbitrary",)),
        name="mix0_prompt",
    )(x_all, nm, win, lng, lnb, wsp, bs_full, bias_p, sink_p, wout, nf, wr, br)


def _mix0_sample_kernel(x_ref, nm_ref, win_ref, lng_ref, lnb_ref, wcoef_ref, bcoef_ref,
                        ck_ref, cv_ref, bsc_ref, bsn_ref, sink_ref,
                        wout_ref, nf_ref, wr_ref, br_ref, cnt_in,
                        x1_in, h_in, ri_in, rg_in,
                        x1_ref, h_ref, ri_ref, rg_ref, kn_ref, vn_ref, va_ref, cnt_ref,
                        q_scr, k_scr, v_scr, mix_scr):
    del x1_in, h_in, ri_in, rg_in
    g = pl.program_id(0)

    @pl.when(g == 0)
    def _():
        u, va, q, k, v = _project(x_ref[...], nm_ref[...], win_ref[...], lng_ref[...], lnb_ref[...])
        q_scr[...] = q
        k_scr[...] = k
        v_scr[...] = v
        kn_ref[...] = k
        vn_ref[...] = v
        va_ref[...] = va
        idx = 0
        for t in range(DEC_SEQ):
            acc = jnp.zeros((DEC_BATCH, A_WIDTH), F32) + bcoef_ref[t:t + 1, :]
            for s in range(t + 1):
                acc = acc + wcoef_ref[idx:idx + 1, :] * va[s * DEC_BATCH:(s + 1) * DEC_BATCH]
                idx += 1
            mix_scr[t * DEC_BATCH:(t + 1) * DEC_BATCH, :A_WIDTH] = u[t * DEC_BATCH:(t + 1) * DEC_BATCH] * acc

    b0 = pl.multiple_of(g * SAMPLE_GROUP, SAMPLE_GROUP)
    lane_lo = lax.broadcasted_iota(jnp.int32, (DEC_SEQ * SAMPLE_GROUP, LANES), 1) < B_HEAD_DIM

    def grab(ref, width):
        return jnp.concatenate([ref[pl.ds(t * DEC_BATCH + b0, SAMPLE_GROUP), :] for t in range(DEC_SEQ)], axis=0)

    qg = grab(q_scr, Q_WIDTH)
    kn = grab(k_scr, KV_WIDTH)
    vn = grab(v_scr, KV_WIDTH)
    kc = ck_ref[...]
    vc = cv_ref[...]
    kc_ops = (kc.astype(BF16), pltpu.roll(kc, B_HEAD_DIM, 1).astype(BF16))
    vc_ops = (vc.astype(BF16), pltpu.roll(vc, B_HEAD_DIM, 1).astype(BF16))
    kn_ops = (kn.astype(BF16), pltpu.roll(kn, B_HEAD_DIM, 1).astype(BF16))
    vn_ops = (vn.astype(BF16), pltpu.roll(vn, B_HEAD_DIM, 1).astype(BF16))
    qt = [qg[:, p * LANES:(p + 1) * LANES] for p in range(4)]
    q_even = [jnp.where(lane_lo, t, 0.0) for t in qt]
    q_odd = [jnp.where(lane_lo, 0.0, t) for t in qt]
    stacks = (jnp.concatenate([q_even[0], q_even[1], q_odd[2], q_odd[3]], axis=0),
              jnp.concatenate([q_odd[0], q_odd[1], q_even[2], q_even[3]], axis=0))
    o = []
    for st in range(2):
        qs = stacks[st].astype(BF16)
        sc = _dot_nt(qs, kc_ops[st]) + bsc_ref[st]
        sn = _dot_nt(qs, kn_ops[st]) + bsn_ref[st][:, :DEC_SEQ * SAMPLE_GROUP]
        sink = sink_ref[st]
        m = jnp.maximum(jnp.maximum(jnp.max(sc, axis=-1, keepdims=True), jnp.max(sn, axis=-1, keepdims=True)), sink)
        pc = jnp.exp(sc - m)
        pn = jnp.exp(sn - m)
        den = jnp.sum(pc, axis=-1, keepdims=True) + jnp.sum(pn, axis=-1, keepdims=True) + jnp.exp(sink - m)
        o.append((_dot(pc.astype(BF16), vc_ops[st]) + _dot(pn.astype(BF16), vn_ops[st])) / den)
    oa, ob = o
    n = DEC_SEQ * SAMPLE_GROUP
    sl = [slice(i * n, (i + 1) * n) for i in range(4)]
    tiles = (jnp.where(lane_lo, oa[sl[0]], ob[sl[0]]), jnp.where(lane_lo, oa[sl[1]], ob[sl[1]]),
             jnp.where(lane_lo, ob[sl[2]], oa[sl[2]]), jnp.where(lane_lo, ob[sl[3]], oa[sl[3]]))
    for p in range(4):
        for t in range(DEC_SEQ):
            mix_scr[pl.ds(t * DEC_BATCH + b0, SAMPLE_GROUP), A_WIDTH + p * LANES:A_WIDTH + (p + 1) * LANES] = (
                tiles[p][t * SAMPLE_GROUP:(t + 1) * SAMPLE_GROUP])

    @pl.when(g == N_SAMPLE_GROUPS - 1)
    def _():
        x1 = x_ref[...] + _dot(mix_scr[...].astype(BF16), wout_ref[...])
        x1_ref[...] = x1
        h, ids, gates = _route(x1, nf_ref[...], wr_ref[...], br_ref[...])
        h_ref[...] = h.reshape(h_ref.shape)
        cnt_ref[...] = cnt_in[...]
        ri_ref[...] = _rank_pack(ids, cnt_ref)
        rg_ref[...] = gates


def _mix0_sample(x_all, nm, win, lng, lnb, wcoef, bcoef, ck, cv, bias_sc, bias_sn, sink_s, wout, nf, wr, br, cnt,
                 x1_all, h_all, ri_all, rg_all):
    sample_rows = pl.BlockSpec((TM, D_MODEL), lambda g: (N_PROMPT_BLOCKS, 0))
    sample_rows3 = pl.BlockSpec((TM, ROW_TILE, LANES), lambda g: (N_PROMPT_BLOCKS, 0, 0))
    sample_lanes = pl.BlockSpec((TM, LANES), lambda g: (N_PROMPT_BLOCKS, 0))
    cache_spec = pl.BlockSpec((SAMPLE_GROUP * WINDOW, KV_WIDTH), lambda g: (g, 0))
    anyspec = pl.BlockSpec(memory_space=pl.ANY)
    n_in = 17
    return pl.pallas_call(
        _mix0_sample_kernel,
        grid=(N_SAMPLE_GROUPS,),
        in_specs=[_const_spec((TM, D_MODEL)), _const_spec((1, D_MODEL)), _const_spec((D_MODEL, IN_WIDTH)),
                  _const_spec((1, A_WIDTH)), _const_spec((1, A_WIDTH)),
                  _const_spec((16, A_WIDTH)), _const_spec((8, A_WIDTH)),
                  cache_spec, cache_spec,
                  _const_spec((2, 4 * 32, SAMPLE_GROUP * WINDOW)), _const_spec((2, 4 * 32, LANES)),
                  _const_spec((2, 4 * 32, 1)),
                  _const_spec((A_WIDTH + Q_WIDTH, D_MODEL)), _const_spec((1, D_MODEL)),
                  _const_spec((D_MODEL, LANES)), _const_spec((1, LANES)), _const_spec((1, LANES)),
                  anyspec, anyspec, anyspec, anyspec],
        out_specs=[sample_rows, sample_rows3, pl.BlockSpec((8, TM), lambda g: (0, N_PROMPT_BLOCKS)), sample_lanes,
                   _const_spec((T_SAMPLE, KV_WIDTH)), _const_spec((T_SAMPLE, KV_WIDTH)),
                   _const_spec((T_SAMPLE, A_WIDTH)), _const_spec((1, LANES))],
        out_shape=[jax.ShapeDtypeStruct((T_ALL, D_MODEL), F32), jax.ShapeDtypeStruct((T_ALL, ROW_TILE, LANES), F32),
                   jax.ShapeDtypeStruct((8, T_ALL), jnp.int32), jax.ShapeDtypeStruct((T_ALL, LANES), F32),
                   jax.ShapeDtypeStruct((T_SAMPLE, KV_WIDTH), F32), jax.ShapeDtypeStruct((T_SAMPLE, KV_WIDTH), F32),
                   jax.ShapeDtypeStruct((T_SAMPLE, A_WIDTH), F32), jax.ShapeDtypeStruct((1, LANES), F32)],
        scratch_shapes=[pltpu.VMEM((T_SAMPLE, Q_WIDTH), F32), pltpu.VMEM((T_SAMPLE, KV_WIDTH), F32),
                        pltpu.VMEM((T_SAMPLE, KV_WIDTH), F32), pltpu.VMEM((T_SAMPLE, D_MODEL), F32)],
        input_output_aliases={n_in: 0, n_in + 1: 1, n_in + 2: 2, n_in + 3: 3},
        compiler_params=_cparams(("arbitrary",)),
        name="mix0_sample",
    )(x_all, nm, win, lng, lnb, wcoef, bcoef, ck, cv, bias_sc, bias_sn, sink_s, wout, nf, wr, br, cnt,
      x1_all, h_all, ri_all, rg_all)


def _moe_metadata(rt_all, cnt):
    counts = cnt[0, :N_EXPERTS].astype(jnp.int32)
    padded = (counts + MOE_BLK - 1) // MOE_BLK * MOE_BLK
    pad_end = jnp.cumsum(padded)
    pad_start = pad_end - padded
    experts = jnp.arange(N_EXPERTS, dtype=jnp.int32)
    eid = rt_all[0:TOP_K]
    base = jnp.sum(jnp.where(eid[:, :, None] == experts[None, None, :], pad_start[None, None, :], 0), axis=-1)
    dest = (base + rt_all[TOP_K:2 * TOP_K]).reshape(N_SLOTS).astype(jnp.int32)
    n_valid = (pad_end[-1] // MOE_BLK).astype(jnp.int32).reshape(1)
    blk_start = jnp.arange(N_MOE_BLOCKS, dtype=jnp.int32) * MOE_BLK
    block_e = jnp.minimum(jnp.sum((blk_start[:, None] >= pad_end[None, :]).astype(jnp.int32), axis=1),
                          N_EXPERTS - 1).astype(jnp.int32)
    zero_start = (pad_start + counts).astype(jnp.int32)
    zero_len = (padded - counts).astype(jnp.int32)
    return dest, block_e, n_valid, jnp.concatenate([zero_start, zero_len, n_valid])


def _dispatch_kernel(dest_ref, zs_ref, h_ref, xs_ref, zero_scr, sem, zsem):
    i = pl.program_id(0)

    @pl.when(i == 0)
    def _():
        zero_scr[...] = jnp.zeros_like(zero_scr)

        def pieces(e, do):
            off = zs_ref[e]
            rem = zs_ref[N_EXPERTS + e]
            bit = MOE_BLK // 2
            while bit >= 1:
                take = (rem & bit) != 0

                @pl.when(take)
                def _(off=off, bit=bit):
                    do(pltpu.make_async_copy(zero_scr.at[pl.ds(0, bit)], xs_ref.at[pl.ds(off, bit)], zsem))

                off = off + jnp.where(take, bit, 0)
                bit //= 2

        def start_e(e, c):
            pieces(e, lambda cp: cp.start())
            return c

        def wait_e(e, c):
            pieces(e, lambda cp: cp.wait())
            return c

        def tail(do):
            def step(b, c):
                do(pltpu.make_async_copy(zero_scr, xs_ref.at[pl.ds(b * MOE_BLK, MOE_BLK)], zsem))
                return c
            return step

        n_valid = zs_ref[2 * N_EXPERTS]
        lax.fori_loop(0, N_EXPERTS, start_e, 0)
        lax.fori_loop(n_valid, N_MOE_BLOCKS, tail(lambda cp: cp.start()), 0)
        lax.fori_loop(0, N_EXPERTS, wait_e, 0)
        lax.fori_loop(n_valid, N_MOE_BLOCKS, tail(lambda cp: cp.wait()), 0)

    base = i * TM

    def body(r, carry):
        for kk in range(TOP_K):
            d = dest_ref[kk * T_ALL + base + r]
            pltpu.make_async_copy(h_ref.at[r], xs_ref.at[d], sem).start(priority=kk)
        return carry

    lax.fori_loop(0, TM, body, 0)
    for kk in range(TOP_K):
        pltpu.make_async_copy(h_ref, xs_ref.at[pl.ds(0, TM)], sem).wait()


def _dispatch(dest, zero_start, h_all):
    return pl.pallas_call(
        _dispatch_kernel,
        grid_spec=pltpu.PrefetchScalarGridSpec(
            num_scalar_prefetch=2,
            grid=(N_ROW_BLOCKS,),
            in_specs=[pl.BlockSpec((TM, ROW_TILE, LANES), lambda i, d, z: (i, 0, 0))],
            out_specs=pl.BlockSpec(memory_space=pl.ANY),
            scratch_shapes=[pltpu.VMEM((MOE_BLK, ROW_TILE, LANES), F32), pltpu.SemaphoreType.DMA(()),
                            pltpu.SemaphoreType.DMA(())],
        ),
        out_shape=jax.ShapeDtypeStruct((N_SORT_ROWS, ROW_TILE, LANES), F32),
        compiler_params=_cparams(("arbitrary",)),
        name="moe_dispatch",
    )(dest, zero_start, h_all)


def _experts_kernel(be_ref, nv_ref, x_ref, wg_ref, wu_ref, wd_ref, y_ref, wg_s, wu_s, wd_s):
    i = pl.program_id(0)

    @pl.when(i < nv_ref[0])
    def _():
        e = be_ref[i]
        prev = be_ref[jnp.maximum(i - 1, 0)]

        @pl.when((i == 0) | (e != prev))
        def _():
            wg_s[...] = wg_ref[...].astype(BF16)
            wu_s[...] = wu_ref[...].astype(BF16)
            wd_s[...] = wd_ref[...].astype(BF16)

        xb = x_ref[...].reshape(MOE_BLK, D_MODEL).astype(BF16)
        a = jax.nn.silu(_dot(xb, wg_s[...])) * _dot(xb, wu_s[...])
        y_ref[...] = _dot(a.astype(BF16), wd_s[...]).reshape(y_ref.shape)

    @pl.when(i >= nv_ref[0])
    def _():
        y_ref[...] = jnp.zeros(y_ref.shape, y_ref.dtype)


def _experts(block_e, n_valid, xs, w_gate, w_up, w_down, layer):
    def blk(i, be, nv):
        return jnp.maximum(jnp.minimum(i, nv[0] - 1), 0)

    def wmap(i, be, nv):
        return (layer, be[blk(i, be, nv)], 0, 0)

    return pl.pallas_call(
        _experts_kernel,
        grid_spec=pltpu.PrefetchScalarGridSpec(
            num_scalar_prefetch=2,
            grid=(N_MOE_BLOCKS,),
            in_specs=[pl.BlockSpec((MOE_BLK, ROW_TILE, LANES), lambda i, be, nv: (blk(i, be, nv), 0, 0)),
                      pl.BlockSpec((None, None, D_MODEL, D_EXPERT), wmap),
                      pl.BlockSpec((None, None, D_MODEL, D_EXPERT), wmap),
                      pl.BlockSpec((None, None, D_EXPERT, D_MODEL), wmap)],
            out_specs=pl.BlockSpec((MOE_BLK, ROW_TILE, LANES), lambda i, be, nv: (i, 0, 0)),
            scratch_shapes=[pltpu.VMEM((D_MODEL, D_EXPERT), BF16), pltpu.VMEM((D_MODEL, D_EXPERT), BF16),
                            pltpu.VMEM((D_EXPERT, D_MODEL), BF16)],
        ),
        out_shape=jax.ShapeDtypeStruct((N_SORT_ROWS, ROW_TILE, LANES), F32),
        compiler_params=_cparams(("arbitrary",)),
        name="moe_experts",
    )(block_e, n_valid, xs, w_gate, w_up, w_down)


def _gather_rows(dest_ref, ys_ref, ybuf, sem, i):
    base = i * TM

    def body(r, carry):
        for kk in range(TOP_K):
            d = dest_ref[kk * T_ALL + base + r]
            pltpu.make_async_copy(ys_ref.at[d], ybuf.at[kk, r], sem).start(priority=kk)
        return carry

    lax.fori_loop(0, TM, body, 0)
    for kk in range(TOP_K):
        pltpu.make_async_copy(ys_ref.at[pl.ds(0, TM)], ybuf.at[kk], sem).wait()


def _combined(x_ref, rg_ref, ybuf):
    rg = rg_ref[...]
    y0 = ybuf[0].reshape(TM, D_MODEL)
    y1 = ybuf[1].reshape(TM, D_MODEL)
    return x_ref[...] + rg[:, 0:1] * y0 + rg[:, 1:2] * y1


def _combine_kernel(dest_ref, x_ref, rg_ref, ys_ref, o_ref, ybuf, sem):
    _gather_rows(dest_ref, ys_ref, ybuf, sem, pl.program_id(0))
    o_ref[...] = _combined(x_ref, rg_ref, ybuf)


def _combine(dest, x_all, rg_all, ys):
    return pl.pallas_call(
        _combine_kernel,
        grid_spec=pltpu.PrefetchScalarGridSpec(
            num_scalar_prefetch=1,
            grid=(N_ROW_BLOCKS,),
            in_specs=[pl.BlockSpec((TM, D_MODEL), lambda i, d: (i, 0)),
                      pl.BlockSpec((TM, LANES), lambda i, d: (i, 0)),
                      pl.BlockSpec(memory_space=pl.ANY)],
            out_specs=pl.BlockSpec((TM, D_MODEL), lambda i, d: (i, 0)),
            scratch_shapes=[pltpu.VMEM((TOP_K, TM, ROW_TILE, LANES), F32), pltpu.SemaphoreType.DMA(())],
        ),
        out_shape=jax.ShapeDtypeStruct((T_ALL, D_MODEL), F32),
        compiler_params=_cparams(("arbitrary",)),
        name="moe_combine",
    )(dest, x_all, rg_all, ys)


def _final_kernel(dest_ref, x_ref, rg_ref, ys_ref, nfin_ref, op_ref, os_ref, ybuf, sem):
    i = pl.program_id(0)
    _gather_rows(dest_ref, ys_ref, ybuf, sem, i)
    y = _rms(_combined(x_ref, rg_ref, ybuf), nfin_ref[...])

    @pl.when(i < N_PROMPT_BLOCKS)
    def _():
        op_ref[...] = y

    @pl.when(i >= N_PROMPT_BLOCKS)
    def _():
        os_ref[...] = y


def _final(dest, x_all, rg_all, ys, nfin):
    return pl.pallas_call(
        _final_kernel,
        grid_spec=pltpu.PrefetchScalarGridSpec(
            num_scalar_prefetch=1,
            grid=(N_ROW_BLOCKS,),
            in_specs=[pl.BlockSpec((TM, D_MODEL), lambda i, d: (i, 0)),
                      pl.BlockSpec((TM, LANES), lambda i, d: (i, 0)),
                      pl.BlockSpec(memory_space=pl.ANY),
                      pl.BlockSpec((1, D_MODEL), lambda i, d: (0, 0))],
            out_specs=[pl.BlockSpec((TM, D_MODEL), lambda i, d: (jnp.minimum(i, N_PROMPT_BLOCKS - 1), 0)),
                       pl.BlockSpec((TM, D_MODEL), lambda i, d: (0, 0))],
            scratch_shapes=[pltpu.VMEM((TOP_K, TM, ROW_TILE, LANES), F32), pltpu.SemaphoreType.DMA(())],
        ),
        out_shape=[jax.ShapeDtypeStruct((T_PROMPT, D_MODEL), F32), jax.ShapeDtypeStruct((T_SAMPLE, D_MODEL), F32)],
        compiler_params=_cparams(("arbitrary",)),
        name="moe_combine_final",
    )(dest, x_all, rg_all, ys, nfin)


def _moe(h_all, rt_all, cnt, w_gate, w_up, w_down, layer):
    dest, block_e, n_valid, zero_start = _moe_metadata(rt_all, cnt)
    xs = _dispatch(dest, zero_start, h_all)
    ys = _experts(block_e, n_valid, xs, w_gate, w_up, w_down, layer)
    return dest, ys


def _pool_project(d_groups, wp_ref, scale):
    outs = [_dot(d_groups[g].astype(BF16), wp_ref[g]) for g in range(len(POOL_SIZES))]
    return jnp.concatenate(outs, axis=1) * scale


def _mix1_prompt_kernel(x_ref, nm_ref, wp_ref, sc_ref, nf_ref, wr_ref, br_ref,
                        x3_ref, h_ref, ri_ref, rg_ref, pl_ref, cnt_ref, ext):
    i = pl.program_id(0)

    @pl.when(i == 0)
    def _():
        cnt_ref[...] = jnp.zeros_like(cnt_ref)

    x = x_ref[...]
    hp = _rms(x, nm_ref[...])

    @pl.when(i % STEPS_PER_BATCH == 0)
    def _():
        ext[0:POOL_MAX, :] = jnp.zeros((POOL_MAX, D_MODEL), F32)

    ext[POOL_MAX:, :] = hp
    pos = (i % STEPS_PER_BATCH) * TM + lax.broadcasted_iota(jnp.int32, (TM, 1), 0)
    d_groups = []
    for g, w in enumerate(POOL_SIZES):
        cols = slice(g * POOL_GROUP_DIM, (g + 1) * POOL_GROUP_DIM)
        acc = ext[:, cols]
        span = 1
        while span < w:
            acc = acc + pltpu.roll(acc, span, 0)
            span *= 2
        cnt = jnp.minimum(pos + 1, w).astype(F32)
        d_groups.append(acc[POOL_MAX:] / cnt - hp[:, cols])
    tail = hp[TM - POOL_MAX:, :]
    ext[0:POOL_MAX, :] = tail
    pl_ref[...] = tail

    x3 = x + _pool_project(d_groups, wp_ref, sc_ref[...])
    x3_ref[...] = x3
    h, ids, gates = _route(x3, nf_ref[...], wr_ref[...], br_ref[...])
    h_ref[...] = h.reshape(h_ref.shape)
    ri_ref[...] = _rank_pack(ids, cnt_ref)
    rg_ref[...] = gates


def _mix1_prompt(x_all, nm, wp, sc, nf, wr, br):
    row_spec = pl.BlockSpec((TM, D_MODEL), lambda i: (i, 0))
    row3_spec = pl.BlockSpec((TM, ROW_TILE, LANES), lambda i: (i, 0, 0))
    lane_spec = pl.BlockSpec((TM, LANES), lambda i: (i, 0))
    return pl.pallas_call(
        _prompt_steps(_mix1_prompt_kernel, 7),
        grid=(N_ROW_BLOCKS,),
        in_specs=[row_spec, _const_spec((1, D_MODEL)),
                  _const_spec((len(POOL_SIZES), POOL_GROUP_DIM, POOL_GROUP_DIM)), _const_spec((1, D_MODEL)),
                  _const_spec((1, D_MODEL)), _const_spec((D_MODEL, LANES)), _const_spec((1, LANES))],
        out_specs=[row_spec, row3_spec, pl.BlockSpec((8, TM), lambda i: (0, i)), lane_spec,
                   pl.BlockSpec((None, POOL_MAX, D_MODEL),
                                lambda i: (jnp.minimum(i // STEPS_PER_BATCH, BATCH - 1), 0, 0)),
                   _const_spec((1, LANES))],
        out_shape=[jax.ShapeDtypeStruct((T_ALL, D_MODEL), F32), jax.ShapeDtypeStruct((T_ALL, ROW_TILE, LANES), F32),
                   jax.ShapeDtypeStruct((8, T_ALL), jnp.int32), jax.ShapeDtypeStruct((T_ALL, LANES), F32),
                   jax.ShapeDtypeStruct((BATCH, POOL_MAX, D_MODEL), F32), jax.ShapeDtypeStruct((1, LANES), F32)],
        scratch_shapes=[pltpu.VMEM((POOL_MAX + TM, D_MODEL), F32)],
        compiler_params=_cparams(("arbitrary",)),
        name="mix1_prompt",
    )(x_all, nm, wp, sc, nf, wr, br)


def _mix1_sample_kernel(x_ref, st_ref, nm_ref, wp_ref, sc_ref, nf_ref, wr_ref, br_ref, cnt_in,
                        x3_in, h_in, ri_in, rg_in,
                        x3_ref, h_ref, ri_ref, rg_ref, hs_ref, cnt_ref):
    del x3_in, h_in, ri_in, rg_in
    x = x_ref[...]
    hs = _rms(x, nm_ref[...])
    hs_ref[...] = hs
    n_ctx = POOL_MAX - 1
    d_groups = []
    for g, w in enumerate(POOL_SIZES):
        cols = slice(g * POOL_GROUP_DIM, (g + 1) * POOL_GROUP_DIM)
        parts = []
        for t in range(DEC_SEQ):
            acc = hs[t * DEC_BATCH:(t + 1) * DEC_BATCH, cols]
            for back in range(1, w):
                src = t - back
                if src >= 0:
                    acc = acc + hs[src * DEC_BATCH:(src + 1) * DEC_BATCH, cols]
                else:
                    acc = acc + st_ref[n_ctx + src, :, cols]
            parts.append(acc / float(w) - hs[t * DEC_BATCH:(t + 1) * DEC_BATCH, cols])
        d_groups.append(jnp.concatenate(parts, axis=0))
    x3 = x + _pool_project(d_groups, wp_ref, sc_ref[...])
    x3_ref[...] = x3
    h, ids, gates = _route(x3, nf_ref[...], wr_ref[...], br_ref[...])
    h_ref[...] = h.reshape(h_ref.shape)
    cnt_ref[...] = cnt_in[...]
    ri_ref[...] = _rank_pack(ids, cnt_ref)
    rg_ref[...] = gates


def _mix1_sample(x_all, state_t, nm, wp, sc, nf, wr, br, cnt, x3_all, h_all, ri_all, rg_all):
    sample_rows = pl.BlockSpec((TM, D_MODEL), lambda g: (N_PROMPT_BLOCKS, 0))
    sample_rows3 = pl.BlockSpec((TM, ROW_TILE, LANES), lambda g: (N_PROMPT_BLOCKS, 0, 0))
    sample_lanes = pl.BlockSpec((TM, LANES), lambda g: (N_PROMPT_BLOCKS, 0))
    anyspec = pl.BlockSpec(memory_space=pl.ANY)
    n_in = 9
    return pl.pallas_call(
        _mix1_sample_kernel,
        grid=(1,),
        in_specs=[sample_rows, _const_spec((POOL_MAX - 1, DEC_BATCH, D_MODEL)), _const_spec((1, D_MODEL)),
                  _const_spec((len(POOL_SIZES), POOL_GROUP_DIM, POOL_GROUP_DIM)), _const_spec((1, D_MODEL)),
                  _const_spec((1, D_MODEL)), _const_spec((D_MODEL, LANES)), _const_spec((1, LANES)),
                  _const_spec((1, LANES)), anyspec, anyspec, anyspec, anyspec],
        out_specs=[sample_rows, sample_rows3, pl.BlockSpec((8, TM), lambda g: (0, N_PROMPT_BLOCKS)), sample_lanes,
                   _const_spec((T_SAMPLE, D_MODEL)), _const_spec((1, LANES))],
        out_shape=[jax.ShapeDtypeStruct((T_ALL, D_MODEL), F32), jax.ShapeDtypeStruct((T_ALL, ROW_TILE, LANES), F32),
                   jax.ShapeDtypeStruct((8, T_ALL), jnp.int32), jax.ShapeDtypeStruct((T_ALL, LANES), F32),
                   jax.ShapeDtypeStruct((T_SAMPLE, D_MODEL), F32), jax.ShapeDtypeStruct((1, LANES), F32)],
        input_output_aliases={n_in: 0, n_in + 1: 1, n_in + 2: 2, n_in + 3: 3},
        compiler_params=_cparams(("arbitrary",)),
        name="mix1_sample",
    )(x_all, state_t, nm, wp, sc, nf, wr, br, cnt, x3_all, h_all, ri_all, rg_all)


def _router_weights(wg, bg, we, be):
    w = jnp.concatenate([wg, jnp.transpose(we, (1, 0, 2)).reshape(D_MODEL, N_EXPERTS)], axis=1)
    b = jnp.concatenate([bg, be.reshape(N_EXPERTS)])
    pad = LANES - N_GROUPS - N_EXPERTS
    return jnp.pad(w, ((0, 0), (0, pad))), jnp.pad(b, (0, pad)).reshape(1, LANES)


def _stack(tab):
    return jnp.stack([jnp.concatenate([tab[h] for h in heads], axis=0) for heads in STACK_HEADS])


def kernel(x_prompt, x_sample, cache_k_win, cache_v_win, state_pool, norm_mix, norm_ffn, norm_final, w_in,
           a_ln_g, a_ln_b, a_w_s, a_b_s, b_sinks, rel_bias_table, w_out, c_w_pool, c_scale,
           router_group_w, router_group_b, router_expert_w, router_expert_b, w_gate, w_up, w_down):
    xs_t = jnp.transpose(x_sample, (1, 0, 2)).reshape(T_SAMPLE, D_MODEL)
    xp2 = x_prompt.reshape(T_PROMPT, D_MODEL)
    win =w_in[0].astype(BF16)
    wout = w_out[0].astype(BF16)
    lng = a_ln_g[0].reshape(1, A_WIDTH)
    lnb = a_ln_b[0].reshape(1, A_WIDTH)
    bias_p, bias_sc, bias_sn, ws_tril = _prep(rel_bias_table, a_w_s[0])
    wsp = ws_tril.reshape(A_HEADS // 2, 2, CHUNK, CHUNK).transpose(0, 2, 1, 3).reshape(A_HEADS // 2, CHUNK, 2 * CHUNK)
    bs_full = jnp.repeat(a_b_s[0].T, A_HEAD_DIM, axis=1)
    bias_p = jnp.stack([_stack(bias_p[0]), _stack(bias_p[1])])
    bias_sc = _stack(bias_sc)
    bias_sn = _stack(bias_sn)
    sinks = b_sinks[0]
    sink_p = jnp.stack([jnp.repeat(sinks[jnp.array(hh)], WINDOW) for hh in STACK_HEADS]).reshape(2, 4 * WINDOW, 1)
    sink_s = jnp.stack([jnp.repeat(sinks[jnp.array(hh)], 32) for hh in STACK_HEADS]).reshape(2, 4 * 32, 1)
    pairs = [(t, s) for t in range(DEC_SEQ) for s in range(t + 1)]
    wcoef = jnp.stack([jnp.repeat(a_w_s[0][:, t, s], A_HEAD_DIM) for t, s in pairs])
    wcoef = jnp.pad(wcoef, ((0, 16 - len(pairs)), (0, 0)))
    bcoef = jnp.pad(jnp.repeat(a_b_s[0][:, :DEC_SEQ].T, A_HEAD_DIM, axis=1), ((0, 8 - DEC_SEQ), (0, 0)))
    ck = cache_k_win[0].reshape(DEC_BATCH * WINDOW, KV_WIDTH)
    cv = cache_v_win[0].reshape(DEC_BATCH * WINDOW, KV_WIDTH)
    routers = [_router_weights(router_group_w[l], router_group_b[l], router_expert_w[l], router_expert_b[l])
               for l in range(2)]
    nm = [norm_mix[l].reshape(1, D_MODEL) for l in range(2)]
    nf = [norm_ffn[l].reshape(1, D_MODEL) for l in range(2)]

    x1_all, h_all, ri_all, rg_all, k_last, v_last, va_last, cnt0 = _mix0_prompt(
        xp2, nm[0], win, lng, lnb, wsp, bs_full, bias_p, sink_p, wout, nf[0], *routers[0])
    x1_all, h_all, ri_all, rg_all, k_new, v_new, va_s, cnt0 = _mix0_sample(
        xs_t, nm[0], win, lng, lnb, wcoef, bcoef, ck, cv, bias_sc, bias_sn, sink_s, wout, nf[0], *routers[0], cnt0,
        x1_all, h_all, ri_all, rg_all)
    dest0, ys0 = _moe(h_all, ri_all, cnt0, w_gate, w_up, w_down, 0)
    x2_all = _combine(dest0, x1_all, rg_all, ys0)

    wp = c_w_pool[0].astype(BF16)
    sc = c_scale[0].reshape(1, D_MODEL)
    x3_all, h2_all, ri2_all, rg2_all, pool_tail, cnt1 = _mix1_prompt(x2_all, nm[1], wp, sc, nf[1], *routers[1])
    state_t = jnp.transpose(state_pool[0], (1, 0, 2))
    x3_all, h2_all, ri2_all, rg2_all, hs1, cnt1 = _mix1_sample(
        x2_all, state_t, nm[1], wp, sc, nf[1], *routers[1], cnt1, x3_all, h2_all, ri2_all, rg2_all)
    dest1, ys1 = _moe(h2_all, ri2_all, cnt1, w_gate, w_up, w_down, 1)
    y_p, y_s = _final(dest1, x3_all, rg2_all, ys1, norm_final.reshape(1, D_MODEL))

    def from_tmajor(a, width):
        return jnp.transpose(a.reshape(DEC_SEQ, DEC_BATCH, width), (1, 0, 2))

    y_prompt = y_p.reshape(BATCH, SEQ, D_MODEL)
    y_sample = from_tmajor(y_s, D_MODEL)
    win_k_p = k_last.reshape(1, BATCH, WINDOW, B_KV_HEADS, B_HEAD_DIM)
    win_v_p = v_last.reshape(1, BATCH, WINDOW, B_KV_HEADS, B_HEAD_DIM)
    kn = from_tmajor(k_new, KV_WIDTH).reshape(DEC_BATCH, DEC_SEQ, B_KV_HEADS, B_HEAD_DIM)
    vn = from_tmajor(v_new, KV_WIDTH).reshape(DEC_BATCH, DEC_SEQ, B_KV_HEADS, B_HEAD_DIM)
    win_k_s = jnp.concatenate([cache_k_win[0][:, DEC_SEQ:], kn], axis=1)[None]
    win_v_s = jnp.concatenate([cache_v_win[0][:, DEC_SEQ:], vn], axis=1)[None]
    chunk_v_p = va_last.reshape(1, BATCH, CHUNK, A_HEADS, A_HEAD_DIM)
    chunk_v_s = from_tmajor(va_s, A_WIDTH).reshape(1, DEC_BATCH, DEC_SEQ, A_HEADS, A_HEAD_DIM)
    pool_p = pool_tail[:, 1:][None]
    pool_s = jnp.concatenate([state_pool[0][:, DEC_SEQ:], from_tmajor(hs1, D_MODEL)], axis=1)[None]
    return (y_prompt, y_sample, win_k_p, win_v_p, win_k_s, win_v_s, chunk_v_p, chunk_v_s, pool_p, pool_s)
```

```python
import functools
import math

import numpy as np
import jax
import jax.numpy as jnp
from jax import lax
from jax.experimental import pallas as pl
from jax.experimental.pallas import tpu as pltpu

F32 = jnp.float32
BF16 = jnp.bfloat16

D_MODEL = 1024
BATCH = 2
SEQ = 8192
DEC_BATCH = 128
DEC_SEQ = 4
A_WIDTH = 512
A_HEADS = 8
A_HEAD_DIM = 64
CHUNK = 128
B_HEADS = 8
B_KV_HEADS = 2
B_HEAD_DIM = 64
B_GROUP = 4
WINDOW = 128
N_BUCKETS = 32
MAX_DISTANCE = WINDOW
Q_WIDTH = 512
KV_WIDTH = 128
IN_WIDTH = 2 * A_WIDTH + Q_WIDTH + 2 * KV_WIDTH
ATTN_SCALE = B_HEAD_DIM ** -0.5
NEG_INF = -1e30
POOL_SIZES = (2, 4, 8, 16)
POOL_GROUP_DIM = 256
POOL_MAX = 16
N_GROUPS = 4
EXPERTS_PER_GROUP = 8
N_EXPERTS = 32
TOP_K = 2
D_EXPERT = 512
EPS = 1e-6

LANES = 128
ROW_TILE = D_MODEL // LANES
T_PROMPT = BATCH * SEQ
T_SAMPLE = DEC_BATCH * DEC_SEQ
T_ALL = T_PROMPT + T_SAMPLE
TM = 512
N_PROMPT_BLOCKS = T_PROMPT // TM
N_ROW_BLOCKS = T_ALL // TM
STEPS_PER_BATCH = SEQ // TM
SUB = TM // WINDOW
N_SLOTS = T_ALL * TOP_K
MOE_BLK = 256
N_MOE_BLOCKS = N_SLOTS // MOE_BLK + N_EXPERTS
N_SORT_ROWS = N_MOE_BLOCKS * MOE_BLK
SAMPLE_GROUP = 8
N_SAMPLE_GROUPS = DEC_BATCH // SAMPLE_GROUP
VMEM_LIMIT = 56 * 1024 * 1024

STACK_HEADS = ((0, 2, 5, 7), (1, 3, 4, 6))


def _t5_bucket_np(dist):
    n = np.maximum(dist, 0)
    max_exact = N_BUCKETS // 2
    nf = np.maximum(n, 1).astype(np.float32)
    large = max_exact + (np.log(nf / np.float32(max_exact)) / np.float32(math.log(MAX_DISTANCE / max_exact))
                         * np.float32(N_BUCKETS - max_exact)).astype(np.int32)
    large = np.minimum(large, N_BUCKETS - 1)
    return np.where(n < max_exact, n, large).astype(np.int32)


def _bucket_tables():
    qi = np.arange(WINDOW)[:, None]
    ki = np.arange(2 * WINDOW)[None, :]
    dist = qi + WINDOW - ki
    valid = (dist >= 0) & (dist < WINDOW)
    bp = np.where(valid, _t5_bucket_np(dist), -1)
    bp_first = np.where(ki >= WINDOW, bp, -1)
    bkt_p = np.stack([bp_first, bp]).astype(np.int32)

    t = np.repeat(np.arange(DEC_SEQ), SAMPLE_GROUP)[:, None]
    b = np.tile(np.arange(SAMPLE_GROUP), DEC_SEQ)[:, None]
    cb = np.repeat(np.arange(SAMPLE_GROUP), WINDOW)[None, :]
    cj = np.tile(np.arange(WINDOW), SAMPLE_GROUP)[None, :]
    dist_c = t + WINDOW - cj
    valid_c = (cb == b) & (dist_c >= 0) & (dist_c < WINDOW)
    bkt_sc = np.where(valid_c, _t5_bucket_np(dist_c), -1).astype(np.int32)
    nt = np.repeat(np.arange(DEC_SEQ), SAMPLE_GROUP)[None, :]
    nb = np.tile(np.arange(SAMPLE_GROUP), DEC_SEQ)[None, :]
    dist_n = t - nt
    valid_n = (nb == b) & (dist_n >= 0)
    bkt_sn = np.where(valid_n, _t5_bucket_np(dist_n), -1).astype(np.int32)
    bkt_sn = np.concatenate([bkt_sn, np.full((32, LANES - 32), -1, np.int32)], axis=1)
    return bkt_p, bkt_sc, bkt_sn


_BKT_P, _BKT_SC, _BKT_SN = _bucket_tables()


def _cparams(semantics):
    return pltpu.CompilerParams(dimension_semantics=semantics, vmem_limit_bytes=VMEM_LIMIT)


def _rms(x, g):
    return x * lax.rsqrt(jnp.mean(x * x, axis=-1, keepdims=True) + EPS) * g


def _layernorm(x, g, b):
    xc = x - jnp.mean(x, axis=-1, keepdims=True)
    return xc * lax.rsqrt(jnp.mean(xc * xc, axis=-1, keepdims=True) + EPS) * g + b


def _dot(a, b):
    return jnp.dot(a, b, preferred_element_type=F32)


def _dot_nt(a, b):
    return lax.dot_general(a, b, (((1,), (1,)), ((), ())), preferred_element_type=F32)


def _project(x, nm, win, lng, lnb):
    h = _rms(x, nm)
    z = _dot(h.astype(BF16), win)
    u = jax.nn.gelu(z[:, :A_WIDTH])
    va = _layernorm(jax.nn.gelu(z[:, A_WIDTH:2 * A_WIDTH]), lng, lnb)
    q = z[:, 2 * A_WIDTH:2 * A_WIDTH + Q_WIDTH] * ATTN_SCALE
    k = z[:, 2 * A_WIDTH + Q_WIDTH:2 * A_WIDTH + Q_WIDTH + KV_WIDTH]
    v = z[:, 2 * A_WIDTH + Q_WIDTH + KV_WIDTH:]
    return u, va, q, k, v


def _route(x1, nf, wr, br):
    hf = _rms(x1, nf)
    h = hf.astype(BF16)
    h_lo = (hf - h.astype(F32)).astype(BF16)
    part = _dot(h, wr)
    logits = part[:, :LANES] + part[:, LANES:] + _dot(h_lo, wr[:, :LANES]) + br
    rows = logits.shape[0]
    lane = lax.broadcasted_iota(jnp.int32, (rows, LANES), 1)
    lanef = lane.astype(F32)
    big = jnp.float32(1e9)
    is_g = lane < N_GROUPS
    gl = jnp.where(is_g, logits, -jnp.inf)
    gmax = jnp.max(gl, axis=1, keepdims=True)
    gsel = jnp.min(jnp.where(gl == gmax, lanef, big), axis=1, keepdims=True)
    gsum = jnp.sum(jnp.where(is_g, jnp.exp(logits - gmax), 0.0), axis=1, keepdims=True)
    g1 = 1.0 / gsum
    lo = N_GROUPS + EXPERTS_PER_GROUP * gsel
    emask = (lanef >= lo) & (lanef < lo + EXPERTS_PER_GROUP)
    el = jnp.where(emask, logits, -jnp.inf)
    v1 = jnp.max(el, axis=1, keepdims=True)
    i1 = jnp.min(jnp.where(el == v1, lanef, big), axis=1, keepdims=True)
    el2 = jnp.where(lanef == i1, -jnp.inf, el)
    v2 = jnp.max(el2, axis=1, keepdims=True)
    i2 = jnp.min(jnp.where(el2 == v2, lanef, big), axis=1, keepdims=True)
    e2 = jnp.exp(v2 - v1)
    den = 1.0 + e2
    w1 = g1 / den
    w2 = g1 * e2 / den
    ids = jnp.where(lane == 0, i1 - N_GROUPS, jnp.where(lane == 1, i2 - N_GROUPS, 0.0)).astype(jnp.int32)
    gates = jnp.where(lane == 0, w1, jnp.where(lane == 1, w2, 0.0))
    return h, ids, gates


def _rank_pack(ids, cnt_ref):
    rows = ids.shape[0]
    lane = lax.broadcasted_iota(jnp.int32, (rows, LANES), 1)
    o0 = (lane == ids[:, 0:1]).astype(F32)
    o1 = (lane == ids[:, 1:2]).astype(F32)
    r = lax.broadcasted_iota(jnp.int32, (rows, rows), 0)
    c = lax.broadcasted_iota(jnp.int32, (rows, rows), 1)
    before = (c < r).astype(BF16)
    p01 = _dot(before, jnp.concatenate([o0, o1], axis=1).astype(BF16))
    p0 = p01[:, :LANES]
    p1 = p01[:, LANES:]
    c0 = jnp.sum(o0, axis=0, keepdims=True)
    c1 = jnp.sum(o1, axis=0, keepdims=True)
    carry = cnt_ref[...]
    rank0 = jnp.sum(o0 * (carry + p0), axis=1, keepdims=True)
    rank1 = jnp.sum(o1 * (carry + c0 + p1), axis=1, keepdims=True)
    cnt_ref[...] = carry + c0 + c1
    idf = ids.astype(F32)
    packed = jnp.where(lane < TOP_K, idf, jnp.where(lane == 2, rank0, jnp.where(lane == 3, rank1, 0.0)))
    return jnp.transpose(packed)[:8].astype(jnp.int32)


def _prep_kernel(tab_ref, bp_ref, bsc_ref, bsn_ref, ws_ref, op_ref, osc_ref, osn_ref, ows_ref):
    def fill(bkt, write):
        for h in range(B_HEADS):
            acc = jnp.full(bkt.shape, NEG_INF, F32)
            for b in range(N_BUCKETS):
                acc = jnp.where(bkt == b, tab_ref[b, h], acc)
            write(h, acc)

    for var in range(2):
        def wr_p(h, acc, var=var):
            op_ref[var, h] = acc
        fill(bp_ref[var], wr_p)

    def wr_sc(h, acc):
        osc_ref[h] = acc
    fill(bsc_ref[...], wr_sc)

    def wr_sn(h, acc):
        osn_ref[h] = acc
    fill(bsn_ref[...], wr_sn)

    r = lax.broadcasted_iota(jnp.int32, (CHUNK, CHUNK), 0)
    c = lax.broadcasted_iota(jnp.int32, (CHUNK, CHUNK), 1)
    for h in range(A_HEADS):
        ows_ref[h] = jnp.where(r >= c, ws_ref[h], 0.0).astype(BF16)


def _prep(rel_bias_table, w_s):
    vm = pl.BlockSpec(memory_space=pltpu.VMEM)
    return pl.pallas_call(
        _prep_kernel,
        in_specs=[pl.BlockSpec(memory_space=pltpu.SMEM), vm, vm, vm, vm],
        out_specs=[vm, vm, vm, vm],
        out_shape=[
            jax.ShapeDtypeStruct((2, B_HEADS, WINDOW, 2 * WINDOW), F32),
            jax.ShapeDtypeStruct((B_HEADS, 32, SAMPLE_GROUP * WINDOW), F32),
            jax.ShapeDtypeStruct((B_HEADS, 32, LANES), F32),
            jax.ShapeDtypeStruct((A_HEADS, CHUNK, CHUNK), BF16),
        ],
        name="prep_tables",
    )(rel_bias_table, jnp.asarray(_BKT_P), jnp.asarray(_BKT_SC), jnp.asarray(_BKT_SN), w_s)


def _gate_pairs(va_rows, wsp_ref, lane_lo):
    outs = []
    for p in range(A_HEADS // 2):
        vp = va_rows[:, p * LANES:(p + 1) * LANES]
        rhs = jnp.concatenate([jnp.where(lane_lo, vp, 0.0), jnp.where(lane_lo, 0.0, vp)], axis=0).astype(BF16)
        outs.append(_dot(wsp_ref[p], rhs))
    return jnp.concatenate(outs, axis=1)


def _prompt_steps(body, first_row_out):
    def kern(*refs):
        i = pl.program_id(0)

        @pl.when(i < N_PROMPT_BLOCKS)
        def _():
            body(*refs)

        @pl.when(i >= N_PROMPT_BLOCKS)
        def _():
            for r in refs[first_row_out:first_row_out + 4]:
                r[...] = jnp.zeros(r.shape, r.dtype)

    return kern


def _mix0_prompt_kernel(x_ref, nm_ref, win_ref, lng_ref, lnb_ref, wsp_ref, bs_ref, bias_ref,
                        wout_ref, nf_ref, wr_ref, br_ref,
                        x1_ref, h_ref, ri_ref, rg_ref, kl_ref, vl_ref, val_ref, cnt_ref,
                        kprev, vprev, mix_scr):
    @pl.when(pl.program_id(0) == 0)
    def _():
        cnt_ref[...] = jnp.zeros_like(cnt_ref)

    x = x_ref[...]
    u, va, q, k, v = _project(x, nm_ref[...], win_ref[...], lng_ref[...], lnb_ref[...])
    lane_lo = lax.broadcasted_iota(jnp.int32, (WINDOW, LANES), 1) < B_HEAD_DIM
    row0 = lax.broadcasted_iota(jnp.int32, (WINDOW, KV_WIDTH), 0) == 0
    first = pl.program_id(0) % STEPS_PER_BATCH == 0

    @pl.when(first)
    def _():
        kprev[...] = jnp.zeros_like(kprev)
        vprev[...] = jnp.zeros_like(vprev)

    for j in range(SUB):
        rows = slice(j * WINDOW, (j + 1) * WINDOW)
        s_gate = _gate_pairs(va[rows], wsp_ref, lane_lo)
        mix_scr[rows, :A_WIDTH] = u[rows] * (s_gate + bs_ref[...])

        if j == 0:
            kp, vp = kprev[...], vprev[...]
        else:
            prows = slice((j - 1) * WINDOW, j * WINDOW)
            kp, vp = k[prows], v[prows]
        kk = jnp.concatenate([jnp.where(row0, 0.0, kp), k[rows]], axis=0)
        vv = jnp.concatenate([jnp.where(row0, 0.0, vp), v[rows]], axis=0)
        kops = (kk.astype(BF16), pltpu.roll(kk, B_HEAD_DIM, 1).astype(BF16))
        vops = (vv.astype(BF16), pltpu.roll(vv, B_HEAD_DIM, 1).astype(BF16))
        qt = [q[rows, p * LANES:(p + 1) * LANES] for p in range(4)]
        q_even = [jnp.where(lane_lo, t, 0.0) for t in qt]
        q_odd = [jnp.where(lane_lo, 0.0, t) for t in qt]
        stacks = (jnp.concatenate([q_even[0], q_even[1], q_odd[2], q_odd[3]], axis=0),
                  jnp.concatenate([q_odd[0], q_odd[1], q_even[2], q_even[3]], axis=0))
        o = []
        for st in range(2):
            s = _dot_nt(stacks[st].astype(BF16), kops[st])
            if j == 0:
                bias = bias_ref[jnp.where(first, 0, 1), st]
            else:
                bias = bias_ref[1, st]
            s = s + bias
            m = jnp.max(s, axis=-1, keepdims=True)
            p = jnp.exp(s - m)
            den = jnp.sum(p, axis=-1, keepdims=True)
            o.append(_dot(p.astype(BF16), vops[st]) / den)
        oa, ob = o
        sl = [slice(i * WINDOW, (i + 1) * WINDOW) for i in range(4)]
        tiles = (jnp.where(lane_lo, oa[sl[0]], ob[sl[0]]), jnp.where(lane_lo, oa[sl[1]], ob[sl[1]]),
                 jnp.where(lane_lo, ob[sl[2]], oa[sl[2]]), jnp.where(lane_lo, ob[sl[3]], oa[sl[3]]))
        for p in range(4):
            mix_scr[rows, A_WIDTH + p * LANES:A_WIDTH + (p + 1) * LANES] = tiles[p]

    last = slice(TM - WINDOW, TM)
    kprev[...] = k[last]
    vprev[...] = v[last]
    kl_ref[...] = k[last]
    vl_ref[...] = v[last]
    val_ref[...] = va[last]

    x1 = x + _dot(mix_scr[...].astype(BF16), wout_ref[...])
    x1_ref[...] = x1
    h, ids, gates = _route(x1, nf_ref[...], wr_ref[...], br_ref[...])
    h_ref[...] = h.reshape(h_ref.shape)
    ri_ref[...] = _rank_pack(ids, cnt_ref)
    rg_ref[...] = gates


def _const_spec(shape):
    nd = len(shape)
    return pl.BlockSpec(shape, lambda i, _n=nd: (0,) * _n)


def _mix0_prompt(x_all, nm, win, lng, lnb, wsp, bs_full, bias_p, wout, nf, wr, br):
    row_spec = pl.BlockSpec((TM, D_MODEL), lambda i: (i, 0))
    row3_spec = pl.BlockSpec((TM, ROW_TILE, LANES), lambda i: (i, 0, 0))
    lane_spec = pl.BlockSpec((TM, LANES), lambda i: (i, 0))
    last_kv = pl.BlockSpec((None, WINDOW, KV_WIDTH), lambda i: (jnp.minimum(i // STEPS_PER_BATCH, BATCH - 1), 0, 0))
    last_va = pl.BlockSpec((None, WINDOW, A_WIDTH), lambda i: (jnp.minimum(i // STEPS_PER_BATCH, BATCH - 1), 0, 0))
    return pl.pallas_call(
        _prompt_steps(_mix0_prompt_kernel, 12),
        grid=(N_ROW_BLOCKS,),
        in_specs=[pl.BlockSpec((TM, D_MODEL), lambda i: (jnp.minimum(i, N_PROMPT_BLOCKS - 1), 0)),
                  _const_spec((1, D_MODEL)), _const_spec((D_MODEL, IN_WIDTH)),
                  _const_spec((1, A_WIDTH)), _const_spec((1, A_WIDTH)),
                  _const_spec((A_HEADS // 2, CHUNK, 2 * CHUNK)), _const_spec((CHUNK, A_WIDTH)),
                  _const_spec((2, 2, 4 * WINDOW, 2 * WINDOW)),
                  _const_spec((A_WIDTH + Q_WIDTH, D_MODEL)), _const_spec((1, D_MODEL)),
                  _const_spec((D_MODEL, 2 * LANES)), _const_spec((1, LANES))],
        out_specs=[row_spec, row3_spec, pl.BlockSpec((8, TM), lambda i: (0, i)), lane_spec,
                   last_kv, last_kv, last_va, _const_spec((1, LANES))],
        out_shape=[jax.ShapeDtypeStruct((T_ALL, D_MODEL), F32), jax.ShapeDtypeStruct((T_ALL, ROW_TILE, LANES), BF16),
                   jax.ShapeDtypeStruct((8, T_ALL), jnp.int32), jax.ShapeDtypeStruct((T_ALL, LANES), F32),
                   jax.ShapeDtypeStruct((BATCH, WINDOW, KV_WIDTH), F32),
                   jax.ShapeDtypeStruct((BATCH, WINDOW, KV_WIDTH), F32),
                   jax.ShapeDtypeStruct((BATCH, WINDOW, A_WIDTH), F32),
                   jax.ShapeDtypeStruct((1, LANES), F32)],
        scratch_shapes=[pltpu.VMEM((WINDOW, KV_WIDTH), F32), pltpu.VMEM((WINDOW, KV_WIDTH), F32),
                        pltpu.VMEM((TM, D_MODEL), F32)],
        compiler_params=_cparams(("arbitrary",)),
        name="mix0_prompt",
    )(x_all, nm, win, lng, lnb, wsp, bs_full, bias_p, wout, nf, wr, br)


def _mix0_sample_kernel(x_ref, nm_ref, win_ref, lng_ref, lnb_ref, wcoef_ref, bcoef_ref,
                        ck_ref, cv_ref, bsc_ref, bsn_ref,
                        wout_ref, nf_ref, wr_ref, br_ref, cnt_in,
                        x1_in, h_in, ri_in, rg_in,
                        x1_ref, h_ref, ri_ref, rg_ref, kn_ref, vn_ref, va_ref, cnt_ref,
                        q_scr, k_scr, v_scr, mix_scr):
    del x1_in, h_in, ri_in, rg_in
    g = pl.program_id(0)

    @pl.when(g == 0)
    def _():
        u, va, q, k, v = _project(x_ref[...], nm_ref[...], win_ref[...], lng_ref[...], lnb_ref[...])
        q_scr[...] = q
        k_scr[...] = k
        v_scr[...] = v
        kn_ref[...] = k
        vn_ref[...] = v
        va_ref[...] = va
        idx = 0
        for t in range(DEC_SEQ):
            acc = jnp.zeros((DEC_BATCH, A_WIDTH), F32) + bcoef_ref[t:t + 1, :]
            for s in range(t + 1):
                acc = acc + wcoef_ref[idx:idx + 1, :] * va[s * DEC_BATCH:(s + 1) * DEC_BATCH]
                idx += 1
            mix_scr[t * DEC_BATCH:(t + 1) * DEC_BATCH, :A_WIDTH] = u[t * DEC_BATCH:(t + 1) * DEC_BATCH] * acc

    b0 = pl.multiple_of(g * SAMPLE_GROUP, SAMPLE_GROUP)
    lane_lo = lax.broadcasted_iota(jnp.int32, (DEC_SEQ * SAMPLE_GROUP, LANES), 1) < B_HEAD_DIM

    def grab(ref, width):
        return jnp.concatenate([ref[pl.ds(t * DEC_BATCH + b0, SAMPLE_GROUP), :] for t in range(DEC_SEQ)], axis=0)

    qg = grab(q_scr, Q_WIDTH)
    kn = grab(k_scr, KV_WIDTH)
    vn = grab(v_scr, KV_WIDTH)
    crow0 = lax.broadcasted_iota(jnp.int32, (SAMPLE_GROUP * WINDOW, KV_WIDTH), 0) == 0
    kc = jnp.where(crow0, 0.0, ck_ref[...])
    vc = jnp.where(crow0, 0.0, cv_ref[...])
    kc_ops = (kc.astype(BF16), pltpu.roll(kc, B_HEAD_DIM, 1).astype(BF16))
    vc_ops = (vc.astype(BF16), pltpu.roll(vc, B_HEAD_DIM, 1).astype(BF16))
    kn_ops = (kn.astype(BF16), pltpu.roll(kn, B_HEAD_DIM, 1).astype(BF16))
    vn_ops = (vn.astype(BF16), pltpu.roll(vn, B_HEAD_DIM, 1).astype(BF16))
    qt = [qg[:, p * LANES:(p + 1) * LANES] for p in range(4)]
    q_even = [jnp.where(lane_lo, t, 0.0) for t in qt]
    q_odd = [jnp.where(lane_lo, 0.0, t) for t in qt]
    stacks = (jnp.concatenate([q_even[0], q_even[1], q_odd[2], q_odd[3]], axis=0),
              jnp.concatenate([q_odd[0], q_odd[1], q_even[2], q_even[3]], axis=0))
    o = []
    for st in range(2):
        qs = stacks[st].astype(BF16)
        sc = _dot_nt(qs, kc_ops[st]) + bsc_ref[st]
        sn = _dot_nt(qs, kn_ops[st]) + bsn_ref[st][:, :DEC_SEQ * SAMPLE_GROUP]
        m = jnp.maximum(jnp.max(sc, axis=-1, keepdims=True), jnp.max(sn, axis=-1, keepdims=True))
        pc = jnp.exp(sc - m)
        pn = jnp.exp(sn - m)
        den = jnp.sum(pc, axis=-1, keepdims=True) + jnp.sum(pn, axis=-1, keepdims=True)
        o.append((_dot(pc.astype(BF16), vc_ops[st]) + _dot(pn.astype(BF16), vn_ops[st])) / den)
    oa, ob = o
    n = DEC_SEQ * SAMPLE_GROUP
    sl = [slice(i * n, (i + 1) * n) for i in range(4)]
    tiles = (jnp.where(lane_lo, oa[sl[0]], ob[sl[0]]), jnp.where(lane_lo, oa[sl[1]], ob[sl[1]]),
             jnp.where(lane_lo, ob[sl[2]], oa[sl[2]]), jnp.where(lane_lo, ob[sl[3]], oa[sl[3]]))
    for p in range(4):
        for t in range(DEC_SEQ):
            mix_scr[pl.ds(t * DEC_BATCH + b0, SAMPLE_GROUP), A_WIDTH + p * LANES:A_WIDTH + (p + 1) * LANES] = (
                tiles[p][t * SAMPLE_GROUP:(t + 1) * SAMPLE_GROUP])

    @pl.when(g == N_SAMPLE_GROUPS - 1)
    def _():
        x1 = x_ref[...] + _dot(mix_scr[...].astype(BF16), wout_ref[...])
        x1_ref[...] = x1
        h, ids, gates = _route(x1, nf_ref[...], wr_ref[...], br_ref[...])
        h_ref[...] = h.reshape(h_ref.shape)
        cnt_ref[...] = cnt_in[...]
        ri_ref[...] = _rank_pack(ids, cnt_ref)
        rg_ref[...] = gates


def _mix0_sample(x_all, nm, win, lng, lnb, wcoef, bcoef, ck, cv, bias_sc, bias_sn, wout, nf, wr, br, cnt,
                 x1_all, h_all, ri_all, rg_all):
    sample_rows = pl.BlockSpec((TM, D_MODEL), lambda g: (N_PROMPT_BLOCKS, 0))
    sample_rows3 = pl.BlockSpec((TM, ROW_TILE, LANES), lambda g: (N_PROMPT_BLOCKS, 0, 0))
    sample_lanes = pl.BlockSpec((TM, LANES), lambda g: (N_PROMPT_BLOCKS, 0))
    cache_spec = pl.BlockSpec((SAMPLE_GROUP * WINDOW, KV_WIDTH), lambda g: (g, 0))
    anyspec = pl.BlockSpec(memory_space=pl.ANY)
    n_in = 16
    return pl.pallas_call(
        _mix0_sample_kernel,
        grid=(N_SAMPLE_GROUPS,),
        in_specs=[_const_spec((TM, D_MODEL)), _const_spec((1, D_MODEL)), _const_spec((D_MODEL, IN_WIDTH)),
                  _const_spec((1, A_WIDTH)), _const_spec((1, A_WIDTH)),
                  _const_spec((16, A_WIDTH)), _const_spec((8, A_WIDTH)),
                  cache_spec, cache_spec,
                  _const_spec((2, 4 * 32, SAMPLE_GROUP * WINDOW)), _const_spec((2, 4 * 32, LANES)),
                  _const_spec((A_WIDTH + Q_WIDTH, D_MODEL)), _const_spec((1, D_MODEL)),
                  _const_spec((D_MODEL, 2 * LANES)), _const_spec((1, LANES)), _const_spec((1, LANES)),
                  anyspec, anyspec, anyspec, anyspec],
        out_specs=[sample_rows, sample_rows3, pl.BlockSpec((8, TM), lambda g: (0, N_PROMPT_BLOCKS)), sample_lanes,
                   _const_spec((T_SAMPLE, KV_WIDTH)), _const_spec((T_SAMPLE, KV_WIDTH)),
                   _const_spec((T_SAMPLE, A_WIDTH)), _const_spec((1, LANES))],
        out_shape=[jax.ShapeDtypeStruct((T_ALL, D_MODEL), F32), jax.ShapeDtypeStruct((T_ALL, ROW_TILE, LANES), BF16),
                   jax.ShapeDtypeStruct((8, T_ALL), jnp.int32), jax.ShapeDtypeStruct((T_ALL, LANES), F32),
                   jax.ShapeDtypeStruct((T_SAMPLE, KV_WIDTH), F32), jax.ShapeDtypeStruct((T_SAMPLE, KV_WIDTH), F32),
                   jax.ShapeDtypeStruct((T_SAMPLE, A_WIDTH), F32), jax.ShapeDtypeStruct((1, LANES), F32)],
        scratch_shapes=[pltpu.VMEM((T_SAMPLE, Q_WIDTH), F32), pltpu.VMEM((T_SAMPLE, KV_WIDTH), F32),
                        pltpu.VMEM((T_SAMPLE, KV_WIDTH), F32), pltpu.VMEM((T_SAMPLE, D_MODEL), F32)],
        input_output_aliases={n_in: 0, n_in + 1: 1, n_in + 2: 2, n_in + 3: 3},
        compiler_params=_cparams(("arbitrary",)),
        name="mix0_sample",
    )(x_all, nm, win, lng, lnb, wcoef, bcoef, ck, cv, bias_sc, bias_sn, wout, nf, wr, br, cnt,
      x1_all, h_all, ri_all, rg_all)


def _moe_metadata(rt_all, cnt):
    counts = cnt[0, :N_EXPERTS].astype(jnp.int32)
    padded = (counts + MOE_BLK - 1) // MOE_BLK * MOE_BLK
    pad_end = jnp.cumsum(padded)
    pad_start = pad_end - padded
    experts = jnp.arange(N_EXPERTS, dtype=jnp.int32)
    eid = rt_all[0:TOP_K]
    base = jnp.sum(jnp.where(eid[:, :, None] == experts[None, None, :], pad_start[None, None, :], 0), axis=-1)
    dest = (base + rt_all[TOP_K:2 * TOP_K]).reshape(N_SLOTS).astype(jnp.int32)
    n_valid = (pad_end[-1] // MOE_BLK).astype(jnp.int32).reshape(1)
    blk_start = jnp.arange(N_MOE_BLOCKS, dtype=jnp.int32) * MOE_BLK
    block_e = jnp.minimum(jnp.sum((blk_start[:, None] >= pad_end[None, :]).astype(jnp.int32), axis=1),
                          N_EXPERTS - 1).astype(jnp.int32)
    zero_start = (pad_start + counts).astype(jnp.int32)
    zero_len = (padded - counts).astype(jnp.int32)
    return dest, block_e, n_valid, jnp.concatenate([zero_start, zero_len, n_valid])


def _dispatch_kernel(dest_ref, zs_ref, h_ref, xs_ref, zero_scr, sem, zsem):
    i = pl.program_id(0)

    @pl.when(i == 0)
    def _():
        zero_scr[...] = jnp.zeros_like(zero_scr)

        def pieces(e, do):
            off = zs_ref[e]
            rem = zs_ref[N_EXPERTS + e]
            bit = MOE_BLK // 2
            while bit >= 1:
                take = (rem & bit) != 0

                @pl.when(take)
                def _(off=off, bit=bit):
                    do(pltpu.make_async_copy(zero_scr.at[pl.ds(0, bit)], xs_ref.at[pl.ds(off, bit)], zsem))

                off = off + jnp.where(take, bit, 0)
                bit //= 2

        def start_e(e, c):
            pieces(e, lambda cp: cp.start())
            return c

        def wait_e(e, c):
            pieces(e, lambda cp: cp.wait())
            return c

        def tail(do):
            def step(b, c):
                do(pltpu.make_async_copy(zero_scr, xs_ref.at[pl.ds(b * MOE_BLK, MOE_BLK)], zsem))
                return c
            return step

        n_valid = zs_ref[2 * N_EXPERTS]
        lax.fori_loop(0, N_EXPERTS, start_e, 0)
        lax.fori_loop(n_valid, N_MOE_BLOCKS, tail(lambda cp: cp.start()), 0)
        lax.fori_loop(0, N_EXPERTS, wait_e, 0)
        lax.fori_loop(n_valid, N_MOE_BLOCKS, tail(lambda cp: cp.wait()), 0)

    base = i * TM

    def body(r, carry):
        for kk in range(TOP_K):
            d = dest_ref[kk * T_ALL + base + r]
            pltpu.make_async_copy(h_ref.at[r], xs_ref.at[d], sem).start(priority=kk)
        return carry

    lax.fori_loop(0, TM, body, 0)
    for kk in range(TOP_K):
        pltpu.make_async_copy(h_ref, xs_ref.at[pl.ds(0, TM)], sem).wait()


def _dispatch(dest, zero_start, h_all):
    return pl.pallas_call(
        _dispatch_kernel,
        grid_spec=pltpu.PrefetchScalarGridSpec(
            num_scalar_prefetch=2,
            grid=(N_ROW_BLOCKS,),
            in_specs=[pl.BlockSpec((TM, ROW_TILE, LANES), lambda i, d, z: (i, 0, 0))],
            out_specs=pl.BlockSpec(memory_space=pl.ANY),
            scratch_shapes=[pltpu.VMEM((MOE_BLK, ROW_TILE, LANES), BF16), pltpu.SemaphoreType.DMA(()),
                            pltpu.SemaphoreType.DMA(())],
        ),
        out_shape=jax.ShapeDtypeStruct((N_SORT_ROWS, ROW_TILE, LANES), BF16),
        compiler_params=_cparams(("arbitrary",)),
        name="moe_dispatch",
    )(dest, zero_start, h_all)


def _experts_kernel(be_ref, nv_ref, x_ref, wg_ref, wu_ref, wd_ref, y_ref, wg_s, wu_s, wd_s):
    i = pl.program_id(0)

    @pl.when(i < nv_ref[0])
    def _():
        e = be_ref[i]
        prev = be_ref[jnp.maximum(i - 1, 0)]

        @pl.when((i == 0) | (e != prev))
        def _():
            wg_s[...] = wg_ref[...].astype(BF16)
            wu_s[...] = wu_ref[...].astype(BF16)
            wd_s[...] = wd_ref[...].astype(BF16)

        xb = x_ref[...].reshape(MOE_BLK, D_MODEL)
        a = jax.nn.silu(_dot(xb, wg_s[...])) * _dot(xb, wu_s[...])
        y_ref[...] = _dot(a.astype(BF16), wd_s[...]).reshape(y_ref.shape)

    @pl.when(i >= nv_ref[0])
    def _():
        y_ref[...] = jnp.zeros(y_ref.shape, y_ref.dtype)


def _experts(block_e, n_valid, xs, w_gate, w_up, w_down, layer):
    def blk(i, be, nv):
        return jnp.maximum(jnp.minimum(i, nv[0] - 1), 0)

    def wmap(i, be, nv):
        return (layer, be[blk(i, be, nv)], 0, 0)

    return pl.pallas_call(
        _experts_kernel,
        grid_spec=pltpu.PrefetchScalarGridSpec(
            num_scalar_prefetch=2,
            grid=(N_MOE_BLOCKS,),
            in_specs=[pl.BlockSpec((MOE_BLK, ROW_TILE, LANES), lambda i, be, nv: (blk(i, be, nv), 0, 0)),
                      pl.BlockSpec((None, None, D_MODEL, D_EXPERT), wmap),
                      pl.BlockSpec((None, None, D_MODEL, D_EXPERT), wmap),
                      pl.BlockSpec((None, None, D_EXPERT, D_MODEL), wmap)],
            out_specs=pl.BlockSpec((MOE_BLK, ROW_TILE, LANES), lambda i, be, nv: (i, 0, 0)),
            scratch_shapes=[pltpu.VMEM((D_MODEL, D_EXPERT), BF16), pltpu.VMEM((D_MODEL, D_EXPERT), BF16),
                            pltpu.VMEM((D_EXPERT, D_MODEL), BF16)],
        ),
        out_shape=jax.ShapeDtypeStruct((N_SORT_ROWS, ROW_TILE, LANES), F32),
        compiler_params=_cparams(("arbitrary",)),
        name="moe_experts",
    )(block_e, n_valid, xs, w_gate, w_up, w_down)


def _gather_rows(dest_ref, ys_ref, ybuf, sem, i):
    base = i * TM

    def body(r, carry):
        for kk in range(TOP_K):
            d = dest_ref[kk * T_ALL + base + r]
            pltpu.make_async_copy(ys_ref.at[d], ybuf.at[kk, r], sem).start(priority=kk)
        return carry

    lax.fori_loop(0, TM, body, 0)
    for kk in range(TOP_K):
        pltpu.make_async_copy(ys_ref.at[pl.ds(0, TM)], ybuf.at[kk], sem).wait()


def _combined(x_ref, rg_ref, ybuf):
    rg = rg_ref[...]
    y0 = ybuf[0].reshape(TM, D_MODEL)
    y1 = ybuf[1].reshape(TM, D_MODEL)
    return x_ref[...] + rg[:, 0:1] * y0 + rg[:, 1:2] * y1


def _combine_kernel(dest_ref, x_ref, rg_ref, ys_ref, o_ref, ybuf, sem):
    _gather_rows(dest_ref, ys_ref, ybuf, sem, pl.program_id(0))
    o_ref[...] = _combined(x_ref, rg_ref, ybuf)


def _combine(dest, x_all, rg_all, ys):
    return pl.pallas_call(
        _combine_kernel,
        grid_spec=pltpu.PrefetchScalarGridSpec(
            num_scalar_prefetch=1,
            grid=(N_ROW_BLOCKS,),
            in_specs=[pl.BlockSpec((TM, D_MODEL), lambda i, d: (i, 0)),
                      pl.BlockSpec((TM, LANES), lambda i, d: (i, 0)),
                      pl.BlockSpec(memory_space=pl.ANY)],
            out_specs=pl.BlockSpec((TM, D_MODEL), lambda i, d: (i, 0)),
            scratch_shapes=[pltpu.VMEM((TOP_K, TM, ROW_TILE, LANES), F32), pltpu.SemaphoreType.DMA(())],
        ),
        out_shape=jax.ShapeDtypeStruct((T_ALL, D_MODEL), F32),
        compiler_params=_cparams(("arbitrary",)),
        name="moe_combine",
    )(dest, x_all, rg_all, ys)


def _final_kernel(dest_ref, x_ref, rg_ref, ys_ref, nfin_ref, op_ref, os_ref, ybuf, sem):
    i = pl.program_id(0)
    _gather_rows(dest_ref, ys_ref, ybuf, sem, i)
    y = _rms(_combined(x_ref, rg_ref, ybuf), nfin_ref[...])

    @pl.when(i < N_PROMPT_BLOCKS)
    def _():
        op_ref[...] = y

    @pl.when(i >= N_PROMPT_BLOCKS)
    def _():
        os_ref[...] = y


def _final(dest, x_all, rg_all, ys, nfin):
    return pl.pallas_call(
        _final_kernel,
        grid_spec=pltpu.PrefetchScalarGridSpec(
            num_scalar_prefetch=1,
            grid=(N_ROW_BLOCKS,),
            in_specs=[pl.BlockSpec((TM, D_MODEL), lambda i, d: (i, 0)),
                      pl.BlockSpec((TM, LANES), lambda i, d: (i, 0)),
                      pl.BlockSpec(memory_space=pl.ANY),
                      pl.BlockSpec((1, D_MODEL), lambda i, d: (0, 0))],
            out_specs=[pl.BlockSpec((TM, D_MODEL), lambda i, d: (jnp.minimum(i, N_PROMPT_BLOCKS - 1), 0)),
                       pl.BlockSpec((TM, D_MODEL), lambda i, d: (0, 0))],
            scratch_shapes=[pltpu.VMEM((TOP_K, TM, ROW_TILE, LANES), F32), pltpu.SemaphoreType.DMA(())],
        ),
        out_shape=[jax.ShapeDtypeStruct((T_PROMPT, D_MODEL), F32), jax.ShapeDtypeStruct((T_SAMPLE, D_MODEL), F32)],
        compiler_params=_cparams(("arbitrary",)),
        name="moe_combine_final",
    )(dest, x_all, rg_all, ys, nfin)


def _moe(h_all, rt_all, cnt, w_gate, w_up, w_down, layer):
    dest, block_e, n_valid, zero_start = _moe_metadata(rt_all, cnt)
    xs = _dispatch(dest, zero_start, h_all)
    ys = _experts(block_e, n_valid, xs, w_gate, w_up, w_down, layer)
    return dest, ys


def _pool_project(d_groups, wp_ref, scale):
    outs = [_dot(d_groups[g].astype(BF16), wp_ref[g]) for g in range(len(POOL_SIZES))]
    return jnp.concatenate(outs, axis=1) * scale


def _mix1_prompt_kernel(x_ref, nm_ref, wp_ref, sc_ref, nf_ref, wr_ref, br_ref,
                        x3_ref, h_ref, ri_ref, rg_ref, pl_ref, cnt_ref, ext):
    i = pl.program_id(0)

    @pl.when(i == 0)
    def _():
        cnt_ref[...] = jnp.zeros_like(cnt_ref)

    x = x_ref[...]
    hp = _rms(x, nm_ref[...])

    @pl.when(i % STEPS_PER_BATCH == 0)
    def _():
        ext[0:POOL_MAX, :] = jnp.zeros((POOL_MAX, D_MODEL), F32)

    ext[POOL_MAX:, :] = hp
    pos = (i % STEPS_PER_BATCH) * TM + lax.broadcasted_iota(jnp.int32, (TM, 1), 0)
    d_groups = []
    for g, w in enumerate(POOL_SIZES):
        cols = slice(g * POOL_GROUP_DIM, (g + 1) * POOL_GROUP_DIM)
        acc = ext[:, cols]
        span = 1
        while span < w:
            acc = acc + pltpu.roll(acc, span, 0)
            span *= 2
        cnt = jnp.minimum(pos + 1, w).astype(F32)
        d_groups.append(acc[POOL_MAX:] / cnt - hp[:, cols])
    tail = hp[TM - POOL_MAX:, :]
    ext[0:POOL_MAX, :] = tail
    pl_ref[...] = tail

    x3 = x + _pool_project(d_groups, wp_ref, sc_ref[...])
    x3_ref[...] = x3
    h, ids, gates = _route(x3, nf_ref[...], wr_ref[...], br_ref[...])
    h_ref[...] = h.reshape(h_ref.shape)
    ri_ref[...] = _rank_pack(ids, cnt_ref)
    rg_ref[...] = gates


def _mix1_prompt(x_all, nm, wp, sc, nf, wr, br):
    row_spec = pl.BlockSpec((TM, D_MODEL), lambda i: (i, 0))
    row3_spec = pl.BlockSpec((TM, ROW_TILE, LANES), lambda i: (i, 0, 0))
    lane_spec = pl.BlockSpec((TM, LANES), lambda i: (i, 0))
    return pl.pallas_call(
        _prompt_steps(_mix1_prompt_kernel, 7),
        grid=(N_ROW_BLOCKS,),
        in_specs=[row_spec, _const_spec((1, D_MODEL)),
                  _const_spec((len(POOL_SIZES), POOL_GROUP_DIM, POOL_GROUP_DIM)), _const_spec((1, D_MODEL)),
                  _const_spec((1, D_MODEL)), _const_spec((D_MODEL, 2 * LANES)), _const_spec((1, LANES))],
        out_specs=[row_spec, row3_spec, pl.BlockSpec((8, TM), lambda i: (0, i)), lane_spec,
                   pl.BlockSpec((None, POOL_MAX, D_MODEL),
                                lambda i: (jnp.minimum(i // STEPS_PER_BATCH, BATCH - 1), 0, 0)),
                   _const_spec((1, LANES))],
        out_shape=[jax.ShapeDtypeStruct((T_ALL, D_MODEL), F32), jax.ShapeDtypeStruct((T_ALL, ROW_TILE, LANES), BF16),
                   jax.ShapeDtypeStruct((8, T_ALL), jnp.int32), jax.ShapeDtypeStruct((T_ALL, LANES), F32),
                   jax.ShapeDtypeStruct((BATCH, POOL_MAX, D_MODEL), F32), jax.ShapeDtypeStruct((1, LANES), F32)],
        scratch_shapes=[pltpu.VMEM((POOL_MAX + TM, D_MODEL), F32)],
        compiler_params=_cparams(("arbitrary",)),
        name="mix1_prompt",
    )(x_all, nm, wp, sc, nf, wr, br)


def _mix1_sample_kernel(x_ref, st_ref, nm_ref, wp_ref, sc_ref, nf_ref, wr_ref, br_ref, cnt_in,
                        x3_in, h_in, ri_in, rg_in,
                        x3_ref, h_ref, ri_ref, rg_ref, hs_ref, cnt_ref):
    del x3_in, h_in, ri_in, rg_in
    x = x_ref[...]
    hs = _rms(x, nm_ref[...])
    hs_ref[...] = hs
    n_ctx = POOL_MAX - 1
    d_groups = []
    for g, w in enumerate(POOL_SIZES):
        cols = slice(g * POOL_GROUP_DIM, (g + 1) * POOL_GROUP_DIM)
        parts = []
        for t in range(DEC_SEQ):
            acc = hs[t * DEC_BATCH:(t + 1) * DEC_BATCH, cols]
            for back in range(1, w):
                src = t - back
                if src >= 0:
                    acc = acc + hs[src * DEC_BATCH:(src + 1) * DEC_BATCH, cols]
                else:
                    acc = acc + st_ref[n_ctx + src, :, cols]
            parts.append(acc / float(w) - hs[t * DEC_BATCH:(t + 1) * DEC_BATCH, cols])
        d_groups.append(jnp.concatenate(parts, axis=0))
    x3 = x + _pool_project(d_groups, wp_ref, sc_ref[...])
    x3_ref[...] = x3
    h, ids, gates = _route(x3, nf_ref[...], wr_ref[...], br_ref[...])
    h_ref[...] = h.reshape(h_ref.shape)
    cnt_ref[...] = cnt_in[...]
    ri_ref[...] = _rank_pack(ids, cnt_ref)
    rg_ref[...] = gates


def _mix1_sample(x_all, state_t, nm, wp, sc, nf, wr, br, cnt, x3_all, h_all, ri_all, rg_all):
    sample_rows = pl.BlockSpec((TM, D_MODEL), lambda g: (N_PROMPT_BLOCKS, 0))
    sample_rows3 = pl.BlockSpec((TM, ROW_TILE, LANES), lambda g: (N_PROMPT_BLOCKS, 0, 0))
    sample_lanes = pl.BlockSpec((TM, LANES), lambda g: (N_PROMPT_BLOCKS, 0))
    anyspec = pl.BlockSpec(memory_space=pl.ANY)
    n_in = 9
    return pl.pallas_call(
        _mix1_sample_kernel,
        grid=(1,),
        in_specs=[sample_rows, _const_spec((POOL_MAX - 1, DEC_BATCH, D_MODEL)), _const_spec((1, D_MODEL)),
                  _const_spec((len(POOL_SIZES), POOL_GROUP_DIM, POOL_GROUP_DIM)), _const_spec((1, D_MODEL)),
                  _const_spec((1, D_MODEL)), _const_spec((D_MODEL, 2 * LANES)), _const_spec((1, LANES)),
                  _const_spec((1, LANES)), anyspec, anyspec, anyspec, anyspec],
        out_specs=[sample_rows, sample_rows3, pl.BlockSpec((8, TM), lambda g: (0, N_PROMPT_BLOCKS)), sample_lanes,
                   _const_spec((T_SAMPLE, D_MODEL)), _const_spec((1, LANES))],
        out_shape=[jax.ShapeDtypeStruct((T_ALL, D_MODEL), F32), jax.ShapeDtypeStruct((T_ALL, ROW_TILE, LANES), BF16),
                   jax.ShapeDtypeStruct((8, T_ALL), jnp.int32), jax.ShapeDtypeStruct((T_ALL, LANES), F32),
                   jax.ShapeDtypeStruct((T_SAMPLE, D_MODEL), F32), jax.ShapeDtypeStruct((1, LANES), F32)],
        input_output_aliases={n_in: 0, n_in + 1: 1, n_in + 2: 2, n_in + 3: 3},
        compiler_params=_cparams(("arbitrary",)),
        name="mix1_sample",
    )(x_all, state_t, nm, wp, sc, nf, wr, br, cnt, x3_all, h_all, ri_all, rg_all)


def _router_weights(wg, bg, we, be):
    w = jnp.concatenate([wg, jnp.transpose(we, (1, 0, 2)).reshape(D_MODEL, N_EXPERTS)], axis=1)
    b = jnp.concatenate([bg, be.reshape(N_EXPERTS)])
    pad = LANES - N_GROUPS - N_EXPERTS
    w = jnp.pad(w, ((0, 0), (0, pad)))
    w_hi = w.astype(BF16)
    w_lo = (w - w_hi.astype(F32)).astype(BF16)
    return jnp.concatenate([w_hi, w_lo], axis=1), jnp.pad(b, (0, pad)).reshape(1, LANES)


def _stack(tab):
    return jnp.stack([jnp.concatenate([tab[h] for h in heads], axis=0) for heads in STACK_HEADS])


def kernel(x_prompt, x_sample, cache_k_win, cache_v_win, state_pool, norm_mix, norm_ffn, norm_final, w_in,
           a_ln_g, a_ln_b, a_w_s, a_b_s, b_sinks, rel_bias_table, w_out, c_w_pool, c_scale,
           router_group_w, router_group_b, router_expert_w, router_expert_b, w_gate, w_up, w_down):
    xs_t = jnp.transpose(x_sample, (1, 0, 2)).reshape(T_SAMPLE, D_MODEL)
    xp2 = x_prompt.reshape(T_PROMPT, D_MODEL)
    win =w_in[0].astype(BF16)
    wout = w_out[0].astype(BF16)
    lng = a_ln_g[0].reshape(1, A_WIDTH)
    lnb = a_ln_b[0].reshape(1, A_WIDTH)
    bias_p, bias_sc, bias_sn, ws_tril = _prep(rel_bias_table, a_w_s[0])
    wsp = ws_tril.reshape(A_HEADS // 2, 2, CHUNK, CHUNK).transpose(0, 2, 1, 3).reshape(A_HEADS // 2, CHUNK, 2 * CHUNK)
    bs_full = jnp.repeat(a_b_s[0].T, A_HEAD_DIM, axis=1)
    bias_p = jnp.stack([_stack(bias_p[0]), _stack(bias_p[1])])
    bias_sc = _stack(bias_sc)
    bias_sn = _stack(bias_sn)
    sinks = b_sinks[0]
    sink_p = jnp.stack([jnp.repeat(sinks[jnp.array(hh)], WINDOW) for hh in STACK_HEADS])
    sink_s = jnp.stack([jnp.repeat(sinks[jnp.array(hh)], 32) for hh in STACK_HEADS])
    bias_p = bias_p.at[:, :, :, 0].set(jnp.broadcast_to(sink_p[None], (2, 2, 4 * WINDOW)))
    bias_sc = bias_sc.at[:, :, 0].set(sink_s)
    pairs = [(t, s) for t in range(DEC_SEQ) for s in range(t + 1)]
    wcoef = jnp.stack([jnp.repeat(a_w_s[0][:, t, s], A_HEAD_DIM) for t, s in pairs])
    wcoef = jnp.pad(wcoef, ((0, 16 - len(pairs)), (0, 0)))
    bcoef = jnp.pad(jnp.repeat(a_b_s[0][:, :DEC_SEQ].T, A_HEAD_DIM, axis=1), ((0, 8 - DEC_SEQ), (0, 0)))
    ck = cache_k_win[0].reshape(DEC_BATCH * WINDOW, KV_WIDTH)
    cv = cache_v_win[0].reshape(DEC_BATCH * WINDOW, KV_WIDTH)
    routers = [_router_weights(router_group_w[l], router_group_b[l], router_expert_w[l], router_expert_b[l])
               for l in range(2)]
    nm = [norm_mix[l].reshape(1, D_MODEL) for l in range(2)]
    nf = [norm_ffn[l].reshape(1, D_MODEL) for l in range(2)]

    x1_all, h_all, ri_all, rg_all, k_last, v_last, va_last, cnt0 = _mix0_prompt(
        xp2, nm[0], win, lng, lnb, wsp, bs_full, bias_p, wout, nf[0], *routers[0])
    x1_all, h_all, ri_all, rg_all, k_new, v_new, va_s, cnt0 = _mix0_sample(
        xs_t, nm[0], win, lng, lnb, wcoef, bcoef, ck, cv, bias_sc, bias_sn, wout, nf[0], *routers[0], cnt0,
        x1_all, h_all, ri_all, rg_all)
    dest0, ys0 = _moe(h_all, ri_all, cnt0, w_gate, w_up, w_down, 0)
    x2_all = _combine(dest0, x1_all, rg_all, ys0)

    wp = c_w_pool[0].astype(BF16)
    sc = c_scale[0].reshape(1, D_MODEL)
    x3_all, h2_all, ri2_all, rg2_all, pool_tail, cnt1 = _mix1_prompt(x2_all, nm[1], wp, sc, nf[1], *routers[1])
    state_t = jnp.transpose(state_pool[0], (1, 0, 2))
    x3_all, h2_all, ri2_all, rg2_all, hs1, cnt1 = _mix1_sample(
        x2_all, state_t, nm[1], wp, sc, nf[1], *routers[1], cnt1, x3_all, h2_all, ri2_all, rg2_all)
    dest1, ys1 = _moe(h2_all, ri2_all, cnt1, w_gate, w_up, w_down, 1)
    y_p, y_s = _final(dest1, x3_all, rg2_all, ys1, norm_final.reshape(1, D_MODEL))

    def from_tmajor(a, width):
        return jnp.transpose(a.reshape(DEC_SEQ, DEC_BATCH, width), (1, 0, 2))

    y_prompt = y_p.reshape(BATCH, SEQ, D_MODEL)
    y_sample = from_tmajor(y_s, D_MODEL)
    win_k_p = k_last.reshape(1, BATCH, WINDOW, B_KV_HEADS, B_HEAD_DIM)
    win_v_p = v_last.reshape(1, BATCH, WINDOW, B_KV_HEADS, B_HEAD_DIM)
    kn = from_tmajor(k_new, KV_WIDTH).reshape(DEC_BATCH, DEC_SEQ, B_KV_HEADS, B_HEAD_DIM)
    vn = from_tmajor(v_new, KV_WIDTH).reshape(DEC_BATCH, DEC_SEQ, B_KV_HEADS, B_HEAD_DIM)
    win_k_s = jnp.concatenate([cache_k_win[0][:, DEC_SEQ:], kn], axis=1)[None]
    win_v_s = jnp.concatenate([cache_v_win[0][:, DEC_SEQ:], vn], axis=1)[None]
    chunk_v_p = va_last.reshape(1, BATCH, CHUNK, A_HEADS, A_HEAD_DIM)
    chunk_v_s = from_tmajor(va_s, A_WIDTH).reshape(1, DEC_BATCH, DEC_SEQ, A_HEADS, A_HEAD_DIM)
    pool_p = pool_tail[:, 1:][None]
    pool_s = jnp.concatenate([state_pool[0][:, DEC_SEQ:], from_tmajor(hs1, D_MODEL)], axis=1)[None]
    return (y_prompt, y_sample, win_k_p, win_v_p, win_k_s, win_v_s, chunk_v_p, chunk_v_s, pool_p, pool_s)
```

```python
import functools
import math

import numpy as np
import jax
import jax.numpy as jnp
from jax import lax
from jax.experimental import pallas as pl
from jax.experimental.pallas import tpu as pltpu

F32 = jnp.float32
BF16 = jnp.bfloat16

D_MODEL = 1024
BATCH = 2
SEQ = 8192
DEC_BATCH = 128
DEC_SEQ = 4
A_WIDTH = 512
A_HEADS = 8
A_HEAD_DIM = 64
CHUNK = 128
B_HEADS = 8
B_KV_HEADS = 2
B_HEAD_DIM = 64
B_GROUP = 4
WINDOW = 128
N_BUCKETS = 32
MAX_DISTANCE = WINDOW
Q_WIDTH = 512
KV_WIDTH = 128
IN_WIDTH = 2 * A_WIDTH + Q_WIDTH + 2 * KV_WIDTH
ATTN_SCALE = B_HEAD_DIM ** -0.5
NEG_INF = -1e30
POOL_SIZES = (2, 4, 8, 16)
POOL_GROUP_DIM = 256
POOL_MAX = 16
N_GROUPS = 4
EXPERTS_PER_GROUP = 8
N_EXPERTS = 32
TOP_K = 2
D_EXPERT = 512
EPS = 1e-6

LANES = 128
ROW_TILE = D_MODEL // LANES
T_PROMPT = BATCH * SEQ
T_SAMPLE = DEC_BATCH * DEC_SEQ
T_ALL = T_PROMPT + T_SAMPLE
TM = 512
N_PROMPT_BLOCKS = T_PROMPT // TM
N_ROW_BLOCKS = T_ALL // TM
STEPS_PER_BATCH = SEQ // TM
SUB = TM // WINDOW
N_SLOTS = T_ALL * TOP_K
MOE_BLK = 512
N_MOE_BLOCKS = N_SLOTS // MOE_BLK + N_EXPERTS
N_SORT_ROWS = N_MOE_BLOCKS * MOE_BLK
SAMPLE_GROUP = 8
N_SAMPLE_GROUPS = DEC_BATCH // SAMPLE_GROUP
VMEM_LIMIT = 56 * 1024 * 1024

STACK_HEADS = ((0, 2, 5, 7), (1, 3, 4, 6))


def _t5_bucket_np(dist):
    n = np.maximum(dist, 0)
    max_exact = N_BUCKETS // 2
    nf = np.maximum(n, 1).astype(np.float32)
    large = max_exact + (np.log(nf / np.float32(max_exact)) / np.float32(math.log(MAX_DISTANCE / max_exact))
                         * np.float32(N_BUCKETS - max_exact)).astype(np.int32)
    large = np.minimum(large, N_BUCKETS - 1)
    return np.where(n < max_exact, n, large).astype(np.int32)


def _bucket_tables():
    qi = np.arange(WINDOW)[:, None]
    ki = np.arange(2 * WINDOW)[None, :]
    dist = qi + WINDOW - ki
    valid = (dist >= 0) & (dist < WINDOW)
    bp = np.where(valid, _t5_bucket_np(dist), -1)
    bp_first = np.where(ki >= WINDOW, bp, -1)
    bkt_p = np.stack([bp_first, bp]).astype(np.int32)

    t = np.repeat(np.arange(DEC_SEQ), SAMPLE_GROUP)[:, None]
    b = np.tile(np.arange(SAMPLE_GROUP), DEC_SEQ)[:, None]
    cb = np.repeat(np.arange(SAMPLE_GROUP), WINDOW)[None, :]
    cj = np.tile(np.arange(WINDOW), SAMPLE_GROUP)[None, :]
    dist_c = t + WINDOW - cj
    valid_c = (cb == b) & (dist_c >= 0) & (dist_c < WINDOW)
    bkt_sc = np.where(valid_c, _t5_bucket_np(dist_c), -1).astype(np.int32)
    nt = np.repeat(np.arange(DEC_SEQ), SAMPLE_GROUP)[None, :]
    nb = np.tile(np.arange(SAMPLE_GROUP), DEC_SEQ)[None, :]
    dist_n = t - nt
    valid_n = (nb == b) & (dist_n >= 0)
    bkt_sn = np.where(valid_n, _t5_bucket_np(dist_n), -1).astype(np.int32)
    bkt_sn = np.concatenate([bkt_sn, np.full((32, LANES - 32), -1, np.int32)], axis=1)
    return bkt_p, bkt_sc, bkt_sn


_BKT_P, _BKT_SC, _BKT_SN = _bucket_tables()


def _cparams(semantics):
    return pltpu.CompilerParams(dimension_semantics=semantics, vmem_limit_bytes=VMEM_LIMIT)


def _rms(x, g):
    return x * lax.rsqrt(jnp.mean(x * x, axis=-1, keepdims=True) + EPS) * g


def _layernorm(x, g, b):
    xc = x - jnp.mean(x, axis=-1, keepdims=True)
    return xc * lax.rsqrt(jnp.mean(xc * xc, axis=-1, keepdims=True) + EPS) * g + b


def _dot(a, b):
    return jnp.dot(a, b, preferred_element_type=F32)


def _dot_nt(a, b):
    return lax.dot_general(a, b, (((1,), (1,)), ((), ())), preferred_element_type=F32)


def _project(x, nm, win, lng, lnb):
    h = _rms(x, nm)
    z = _dot(h.astype(BF16), win)
    u = jax.nn.gelu(z[:, :A_WIDTH])
    va = _layernorm(jax.nn.gelu(z[:, A_WIDTH:2 * A_WIDTH]), lng, lnb)
    q = z[:, 2 * A_WIDTH:2 * A_WIDTH + Q_WIDTH] * ATTN_SCALE
    k = z[:, 2 * A_WIDTH + Q_WIDTH:2 * A_WIDTH + Q_WIDTH + KV_WIDTH]
    v = z[:, 2 * A_WIDTH + Q_WIDTH + KV_WIDTH:]
    return u, va, q, k, v


def _route(x1, nf, wr, br):
    hf = _rms(x1, nf)
    h = hf.astype(BF16)
    h_lo = (hf - h.astype(F32)).astype(BF16)
    part = _dot(h, wr)
    logits = part[:, :LANES] + part[:, LANES:] + _dot(h_lo, wr[:, :LANES]) + br
    rows = logits.shape[0]
    lane = lax.broadcasted_iota(jnp.int32, (rows, LANES), 1)
    lanef = lane.astype(F32)
    big = jnp.float32(1e9)
    is_g = lane < N_GROUPS
    gl = jnp.where(is_g, logits, -jnp.inf)
    gmax = jnp.max(gl, axis=1, keepdims=True)
    gsel = jnp.min(jnp.where(gl == gmax, lanef, big), axis=1, keepdims=True)
    gsum = jnp.sum(jnp.where(is_g, jnp.exp(logits - gmax), 0.0), axis=1, keepdims=True)
    g1 = 1.0 / gsum
    lo = N_GROUPS + EXPERTS_PER_GROUP * gsel
    emask = (lanef >= lo) & (lanef < lo + EXPERTS_PER_GROUP)
    el = jnp.where(emask, logits, -jnp.inf)
    v1 = jnp.max(el, axis=1, keepdims=True)
    i1 = jnp.min(jnp.where(el == v1, lanef, big), axis=1, keepdims=True)
    el2 = jnp.where(lanef == i1, -jnp.inf, el)
    v2 = jnp.max(el2, axis=1, keepdims=True)
    i2 = jnp.min(jnp.where(el2 == v2, lanef, big), axis=1, keepdims=True)
    e2 = jnp.exp(v2 - v1)
    den = 1.0 + e2
    w1 = g1 / den
    w2 = g1 * e2 / den
    ids = jnp.where(lane == 0, i1 - N_GROUPS, jnp.where(lane == 1, i2 - N_GROUPS, 0.0)).astype(jnp.int32)
    gates = jnp.where(lane == 0, w1, jnp.where(lane == 1, w2, 0.0))
    return h, ids, gates


def _rank_pack(ids, cnt_ref):
    rows = ids.shape[0]
    lane = lax.broadcasted_iota(jnp.int32, (rows, LANES), 1)
    o0 = (lane == ids[:, 0:1]).astype(F32)
    o1 = (lane == ids[:, 1:2]).astype(F32)
    r = lax.broadcasted_iota(jnp.int32, (rows, rows), 0)
    c = lax.broadcasted_iota(jnp.int32, (rows, rows), 1)
    before = (c < r).astype(BF16)
    p01 = _dot(before, jnp.concatenate([o0, o1], axis=1).astype(BF16))
    p0 = p01[:, :LANES]
    p1 = p01[:, LANES:]
    c0 = jnp.sum(o0, axis=0, keepdims=True)
    c1 = jnp.sum(o1, axis=0, keepdims=True)
    carry = cnt_ref[...]
    rank0 = jnp.sum(o0 * (carry + p0), axis=1, keepdims=True)
    rank1 = jnp.sum(o1 * (carry + c0 + p1), axis=1, keepdims=True)
    cnt_ref[...] = carry + c0 + c1
    idf = ids.astype(F32)
    packed = jnp.where(lane < TOP_K, idf, jnp.where(lane == 2, rank0, jnp.where(lane == 3, rank1, 0.0)))
    return jnp.transpose(packed)[:8].astype(jnp.int32)


def _prep_kernel(tab_ref, bp_ref, bsc_ref, bsn_ref, ws_ref, op_ref, osc_ref, osn_ref, ows_ref):
    def fill(bkt, write):
        for h in range(B_HEADS):
            acc = jnp.full(bkt.shape, NEG_INF, F32)
            for b in range(N_BUCKETS):
                acc = jnp.where(bkt == b, tab_ref[b, h], acc)
            write(h, acc)

    for var in range(2):
        def wr_p(h, acc, var=var):
            op_ref[var, h] = acc
        fill(bp_ref[var], wr_p)

    def wr_sc(h, acc):
        osc_ref[h] = acc
    fill(bsc_ref[...], wr_sc)

    def wr_sn(h, acc):
        osn_ref[h] = acc
    fill(bsn_ref[...], wr_sn)

    r = lax.broadcasted_iota(jnp.int32, (CHUNK, CHUNK), 0)
    c = lax.broadcasted_iota(jnp.int32, (CHUNK, CHUNK), 1)
    for h in range(A_HEADS):
        ows_ref[h] = jnp.where(r >= c, ws_ref[h], 0.0).astype(BF16)


def _prep(rel_bias_table, w_s):
    vm = pl.BlockSpec(memory_space=pltpu.VMEM)
    return pl.pallas_call(
        _prep_kernel,
        in_specs=[pl.BlockSpec(memory_space=pltpu.SMEM), vm, vm, vm, vm],
        out_specs=[vm, vm, vm, vm],
        out_shape=[
            jax.ShapeDtypeStruct((2, B_HEADS, WINDOW, 2 * WINDOW), F32),
            jax.ShapeDtypeStruct((B_HEADS, 32, SAMPLE_GROUP * WINDOW), F32),
            jax.ShapeDtypeStruct((B_HEADS, 32, LANES), F32),
            jax.ShapeDtypeStruct((A_HEADS, CHUNK, CHUNK), BF16),
        ],
        name="prep_tables",
    )(rel_bias_table, jnp.asarray(_BKT_P), jnp.asarray(_BKT_SC), jnp.asarray(_BKT_SN), w_s)


def _gate_pairs(va_rows, wsp_ref, lane_lo):
    outs = []
    for p in range(A_HEADS // 2):
        vp = va_rows[:, p * LANES:(p + 1) * LANES]
        rhs = jnp.concatenate([jnp.where(lane_lo, vp, 0.0), jnp.where(lane_lo, 0.0, vp)], axis=0).astype(BF16)
        outs.append(_dot(wsp_ref[p], rhs))
    return jnp.concatenate(outs, axis=1)


def _prompt_steps(body, first_row_out):
    def kern(*refs):
        i = pl.program_id(0)

        @pl.when(i < N_PROMPT_BLOCKS)
        def _():
            body(*refs)

        @pl.when(i >= N_PROMPT_BLOCKS)
        def _():
            for r in refs[first_row_out:first_row_out + 4]:
                r[...] = jnp.zeros(r.shape, r.dtype)

    return kern


def _mix0_prompt_kernel(x_ref, nm_ref, win_ref, lng_ref, lnb_ref, wsp_ref, bs_ref, bias_ref,
                        wout_ref, nf_ref, wr_ref, br_ref,
                        x1_ref, h_ref, ri_ref, rg_ref, kl_ref, vl_ref, val_ref, cnt_ref,
                        kprev, vprev, mix_scr):
    @pl.when(pl.program_id(0) == 0)
    def _():
        cnt_ref[...] = jnp.zeros_like(cnt_ref)

    x = x_ref[...]
    u, va, q, k, v = _project(x, nm_ref[...], win_ref[...], lng_ref[...], lnb_ref[...])
    lane_lo = lax.broadcasted_iota(jnp.int32, (WINDOW, LANES), 1) < B_HEAD_DIM
    row0 = lax.broadcasted_iota(jnp.int32, (WINDOW, KV_WIDTH), 0) == 0
    first = pl.program_id(0) % STEPS_PER_BATCH == 0

    @pl.when(first)
    def _():
        kprev[...] = jnp.zeros_like(kprev)
        vprev[...] = jnp.zeros_like(vprev)

    for j in range(SUB):
        rows = slice(j * WINDOW, (j + 1) * WINDOW)
        s_gate = _gate_pairs(va[rows], wsp_ref, lane_lo)
        mix_scr[rows, :A_WIDTH] = u[rows] * (s_gate + bs_ref[...])

        if j == 0:
            kp, vp = kprev[...], vprev[...]
        else:
            prows = slice((j - 1) * WINDOW, j * WINDOW)
            kp, vp = k[prows], v[prows]
        kk = jnp.concatenate([jnp.where(row0, 0.0, kp), k[rows]], axis=0)
        vv = jnp.concatenate([jnp.where(row0, 0.0, vp), v[rows]], axis=0)
        kops = (kk.astype(BF16), pltpu.roll(kk, B_HEAD_DIM, 1).astype(BF16))
        vops = (vv.astype(BF16), pltpu.roll(vv, B_HEAD_DIM, 1).astype(BF16))
        qt = [q[rows, p * LANES:(p + 1) * LANES] for p in range(4)]
        q_even = [jnp.where(lane_lo, t, 0.0) for t in qt]
        q_odd = [jnp.where(lane_lo, 0.0, t) for t in qt]
        stacks = (jnp.concatenate([q_even[0], q_even[1], q_odd[2], q_odd[3]], axis=0),
                  jnp.concatenate([q_odd[0], q_odd[1], q_even[2], q_even[3]], axis=0))
        o = []
        for st in range(2):
            s = _dot_nt(stacks[st].astype(BF16), kops[st])
            if j == 0:
                bias = bias_ref[jnp.where(first, 0, 1), st]
            else:
                bias = bias_ref[1, st]
            s = s + bias
            m = jnp.max(s, axis=-1, keepdims=True)
            p = jnp.exp(s - m)
            den = jnp.sum(p, axis=-1, keepdims=True)
            o.append(_dot(p.astype(BF16), vops[st]) / den)
        oa, ob = o
        sl = [slice(i * WINDOW, (i + 1) * WINDOW) for i in range(4)]
        tiles = (jnp.where(lane_lo, oa[sl[0]], ob[sl[0]]), jnp.where(lane_lo, oa[sl[1]], ob[sl[1]]),
                 jnp.where(lane_lo, ob[sl[2]], oa[sl[2]]), jnp.where(lane_lo, ob[sl[3]], oa[sl[3]]))
        for p in range(4):
            mix_scr[rows, A_WIDTH + p * LANES:A_WIDTH + (p + 1) * LANES] = tiles[p]

    last = slice(TM - WINDOW, TM)
    kprev[...] = k[last]
    vprev[...] = v[last]
    kl_ref[...] = k[last]
    vl_ref[...] = v[last]
    val_ref[...] = va[last]

    x1 = x + _dot(mix_scr[...].astype(BF16), wout_ref[...])
    x1_ref[...] = x1
    h, ids, gates = _route(x1, nf_ref[...], wr_ref[...], br_ref[...])
    h_ref[...] = h.reshape(h_ref.shape)
    ri_ref[...] = _rank_pack(ids, cnt_ref)
    rg_ref[...] = gates


def _const_spec(shape):
    nd = len(shape)
    return pl.BlockSpec(shape, lambda i, _n=nd: (0,) * _n)


def _mix0_prompt(x_all, nm, win, lng, lnb, wsp, bs_full, bias_p, wout, nf, wr, br):
    row_spec = pl.BlockSpec((TM, D_MODEL), lambda i: (i, 0))
    row3_spec = pl.BlockSpec((TM, ROW_TILE, LANES), lambda i: (i, 0, 0))
    lane_spec = pl.BlockSpec((TM, LANES), lambda i: (i, 0))
    last_kv = pl.BlockSpec((None, WINDOW, KV_WIDTH), lambda i: (jnp.minimum(i // STEPS_PER_BATCH, BATCH - 1), 0, 0))
    last_va = pl.BlockSpec((None, WINDOW, A_WIDTH), lambda i: (jnp.minimum(i // STEPS_PER_BATCH, BATCH - 1), 0, 0))
    return pl.pallas_call(
        _prompt_steps(_mix0_prompt_kernel, 12),
        grid=(N_ROW_BLOCKS,),
        in_specs=[pl.BlockSpec((TM, D_MODEL), lambda i: (jnp.minimum(i, N_PROMPT_BLOCKS - 1), 0)),
                  _const_spec((1, D_MODEL)), _const_spec((D_MODEL, IN_WIDTH)),
                  _const_spec((1, A_WIDTH)), _const_spec((1, A_WIDTH)),
                  _const_spec((A_HEADS // 2, CHUNK, 2 * CHUNK)), _const_spec((CHUNK, A_WIDTH)),
                  _const_spec((2, 2, 4 * WINDOW, 2 * WINDOW)),
                  _const_spec((A_WIDTH + Q_WIDTH, D_MODEL)), _const_spec((1, D_MODEL)),
                  _const_spec((D_MODEL, 2 * LANES)), _const_spec((1, LANES))],
        out_specs=[row_spec, row3_spec, pl.BlockSpec((8, TM), lambda i: (0, i)), lane_spec,
                   last_kv, last_kv, last_va, _const_spec((1, LANES))],
        out_shape=[jax.ShapeDtypeStruct((T_ALL, D_MODEL), F32), jax.ShapeDtypeStruct((T_ALL, ROW_TILE, LANES), BF16),
                   jax.ShapeDtypeStruct((8, T_ALL), jnp.int32), jax.ShapeDtypeStruct((T_ALL, LANES), F32),
                   jax.ShapeDtypeStruct((BATCH, WINDOW, KV_WIDTH), F32),
                   jax.ShapeDtypeStruct((BATCH, WINDOW, KV_WIDTH), F32),
                   jax.ShapeDtypeStruct((BATCH, WINDOW, A_WIDTH), F32),
                   jax.ShapeDtypeStruct((1, LANES), F32)],
        scratch_shapes=[pltpu.VMEM((WINDOW, KV_WIDTH), F32), pltpu.VMEM((WINDOW, KV_WIDTH), F32),
                        pltpu.VMEM((TM, D_MODEL), F32)],
        compiler_params=_cparams(("arbitrary",)),
        name="mix0_prompt",
    )(x_all, nm, win, lng, lnb, wsp, bs_full, bias_p, wout, nf, wr, br)


def _mix0_sample_kernel(x_ref, nm_ref, win_ref, lng_ref, lnb_ref, wcoef_ref, bcoef_ref,
                        ck_ref, cv_ref, bsc_ref, bsn_ref,
                        wout_ref, nf_ref, wr_ref, br_ref, cnt_in,
                        x1_in, h_in, ri_in, rg_in,
                        x1_ref, h_ref, ri_ref, rg_ref, kn_ref, vn_ref, va_ref, cnt_ref,
                        q_scr, k_scr, v_scr, mix_scr):
    del x1_in, h_in, ri_in, rg_in
    g = pl.program_id(0)

    @pl.when(g == 0)
    def _():
        u, va, q, k, v = _project(x_ref[...], nm_ref[...], win_ref[...], lng_ref[...], lnb_ref[...])
        q_scr[...] = q
        k_scr[...] = k
        v_scr[...] = v
        kn_ref[...] = k
        vn_ref[...] = v
        va_ref[...] = va
        idx = 0
        for t in range(DEC_SEQ):
            acc = jnp.zeros((DEC_BATCH, A_WIDTH), F32) + bcoef_ref[t:t + 1, :]
            for s in range(t + 1):
                acc = acc + wcoef_ref[idx:idx + 1, :] * va[s * DEC_BATCH:(s + 1) * DEC_BATCH]
                idx += 1
            mix_scr[t * DEC_BATCH:(t + 1) * DEC_BATCH, :A_WIDTH] = u[t * DEC_BATCH:(t + 1) * DEC_BATCH] * acc

    b0 = pl.multiple_of(g * SAMPLE_GROUP, SAMPLE_GROUP)
    lane_lo = lax.broadcasted_iota(jnp.int32, (DEC_SEQ * SAMPLE_GROUP, LANES), 1) < B_HEAD_DIM

    def grab(ref, width):
        return jnp.concatenate([ref[pl.ds(t * DEC_BATCH + b0, SAMPLE_GROUP), :] for t in range(DEC_SEQ)], axis=0)

    qg = grab(q_scr, Q_WIDTH)
    kn = grab(k_scr, KV_WIDTH)
    vn = grab(v_scr, KV_WIDTH)
    crow0 = lax.broadcasted_iota(jnp.int32, (SAMPLE_GROUP * WINDOW, KV_WIDTH), 0) == 0
    rows_kv = (SAMPLE_GROUP * WINDOW, KV_WIDTH)
    kc = jnp.where(crow0, 0.0, ck_ref[...].reshape(rows_kv))
    vc = jnp.where(crow0, 0.0, cv_ref[...].reshape(rows_kv))
    kc_ops = (kc.astype(BF16), pltpu.roll(kc, B_HEAD_DIM, 1).astype(BF16))
    vc_ops = (vc.astype(BF16), pltpu.roll(vc, B_HEAD_DIM, 1).astype(BF16))
    kn_ops = (kn.astype(BF16), pltpu.roll(kn, B_HEAD_DIM, 1).astype(BF16))
    vn_ops = (vn.astype(BF16), pltpu.roll(vn, B_HEAD_DIM, 1).astype(BF16))
    qt = [qg[:, p * LANES:(p + 1) * LANES] for p in range(4)]
    q_even = [jnp.where(lane_lo, t, 0.0) for t in qt]
    q_odd = [jnp.where(lane_lo, 0.0, t) for t in qt]
    stacks = (jnp.concatenate([q_even[0], q_even[1], q_odd[2], q_odd[3]], axis=0),
              jnp.concatenate([q_odd[0], q_odd[1], q_even[2], q_even[3]], axis=0))
    o = []
    for st in range(2):
        qs = stacks[st].astype(BF16)
        sc = _dot_nt(qs, kc_ops[st]) + bsc_ref[st]
        sn = _dot_nt(qs, kn_ops[st]) + bsn_ref[st][:, :DEC_SEQ * SAMPLE_GROUP]
        m = jnp.maximum(jnp.max(sc, axis=-1, keepdims=True), jnp.max(sn, axis=-1, keepdims=True))
        pc = jnp.exp(sc - m)
        pn = jnp.exp(sn - m)
        den = jnp.sum(pc, axis=-1, keepdims=True) + jnp.sum(pn, axis=-1, keepdims=True)
        o.append((_dot(pc.astype(BF16), vc_ops[st]) + _dot(pn.astype(BF16), vn_ops[st])) / den)
    oa, ob = o
    n = DEC_SEQ * SAMPLE_GROUP
    sl = [slice(i * n, (i + 1) * n) for i in range(4)]
    tiles = (jnp.where(lane_lo, oa[sl[0]], ob[sl[0]]), jnp.where(lane_lo, oa[sl[1]], ob[sl[1]]),
             jnp.where(lane_lo, ob[sl[2]], oa[sl[2]]), jnp.where(lane_lo, ob[sl[3]], oa[sl[3]]))
    for p in range(4):
        for t in range(DEC_SEQ):
            mix_scr[pl.ds(t * DEC_BATCH + b0, SAMPLE_GROUP), A_WIDTH + p * LANES:A_WIDTH + (p + 1) * LANES] = (
                tiles[p][t * SAMPLE_GROUP:(t + 1) * SAMPLE_GROUP])

    @pl.when(g == N_SAMPLE_GROUPS - 1)
    def _():
        x1 = x_ref[...] + _dot(mix_scr[...].astype(BF16), wout_ref[...])
        x1_ref[...] = x1
        h, ids, gates = _route(x1, nf_ref[...], wr_ref[...], br_ref[...])
        h_ref[...] = h.reshape(h_ref.shape)
        cnt_ref[...] = cnt_in[...]
        ri_ref[...] = _rank_pack(ids, cnt_ref)
        rg_ref[...] = gates


def _mix0_sample(x_all, nm, win, lng, lnb, wcoef, bcoef, ck, cv, bias_sc, bias_sn, wout, nf, wr, br, cnt,
                 x1_all, h_all, ri_all, rg_all):
    sample_rows = pl.BlockSpec((TM, D_MODEL), lambda g: (N_PROMPT_BLOCKS, 0))
    sample_rows3 = pl.BlockSpec((TM, ROW_TILE, LANES), lambda g: (N_PROMPT_BLOCKS, 0, 0))
    sample_lanes = pl.BlockSpec((TM, LANES), lambda g: (N_PROMPT_BLOCKS, 0))
    cache_spec = pl.BlockSpec((None, SAMPLE_GROUP, WINDOW, B_KV_HEADS, B_HEAD_DIM), lambda g: (0, g, 0, 0, 0))
    anyspec = pl.BlockSpec(memory_space=pl.ANY)
    n_in = 16
    return pl.pallas_call(
        _mix0_sample_kernel,
        grid=(N_SAMPLE_GROUPS,),
        in_specs=[_const_spec((TM, D_MODEL)), _const_spec((1, D_MODEL)), _const_spec((D_MODEL, IN_WIDTH)),
                  _const_spec((1, A_WIDTH)), _const_spec((1, A_WIDTH)),
                  _const_spec((16, A_WIDTH)), _const_spec((8, A_WIDTH)),
                  cache_spec, cache_spec,
                  _const_spec((2, 4 * 32, SAMPLE_GROUP * WINDOW)), _const_spec((2, 4 * 32, LANES)),
                  _const_spec((A_WIDTH + Q_WIDTH, D_MODEL)), _const_spec((1, D_MODEL)),
                  _const_spec((D_MODEL, 2 * LANES)), _const_spec((1, LANES)), _const_spec((1, LANES)),
                  anyspec, anyspec, anyspec, anyspec],
        out_specs=[sample_rows, sample_rows3, pl.BlockSpec((8, TM), lambda g: (0, N_PROMPT_BLOCKS)), sample_lanes,
                   _const_spec((T_SAMPLE, KV_WIDTH)), _const_spec((T_SAMPLE, KV_WIDTH)),
                   _const_spec((T_SAMPLE, A_WIDTH)), _const_spec((1, LANES))],
        out_shape=[jax.ShapeDtypeStruct((T_ALL, D_MODEL), F32), jax.ShapeDtypeStruct((T_ALL, ROW_TILE, LANES), BF16),
                   jax.ShapeDtypeStruct((8, T_ALL), jnp.int32), jax.ShapeDtypeStruct((T_ALL, LANES), F32),
                   jax.ShapeDtypeStruct((T_SAMPLE, KV_WIDTH), F32), jax.ShapeDtypeStruct((T_SAMPLE, KV_WIDTH), F32),
                   jax.ShapeDtypeStruct((T_SAMPLE, A_WIDTH), F32), jax.ShapeDtypeStruct((1, LANES), F32)],
        scratch_shapes=[pltpu.VMEM((T_SAMPLE, Q_WIDTH), F32), pltpu.VMEM((T_SAMPLE, KV_WIDTH), F32),
                        pltpu.VMEM((T_SAMPLE, KV_WIDTH), F32), pltpu.VMEM((T_SAMPLE, D_MODEL), F32)],
        input_output_aliases={n_in: 0, n_in + 1: 1, n_in + 2: 2, n_in + 3: 3},
        compiler_params=_cparams(("arbitrary",)),
        name="mix0_sample",
    )(x_all, nm, win, lng, lnb, wcoef, bcoef, ck, cv, bias_sc, bias_sn, wout, nf, wr, br, cnt,
      x1_all, h_all, ri_all, rg_all)


def _moe_metadata(rt_all, cnt):
    counts = cnt[0, :N_EXPERTS].astype(jnp.int32)
    padded = (counts + MOE_BLK - 1) // MOE_BLK * MOE_BLK
    pad_end = jnp.cumsum(padded)
    pad_start = pad_end - padded
    experts = jnp.arange(N_EXPERTS, dtype=jnp.int32)
    eid = rt_all[0:TOP_K]
    base = jnp.sum(jnp.where(eid[:, :, None] == experts[None, None, :], pad_start[None, None, :], 0), axis=-1)
    dest = (base + rt_all[TOP_K:2 * TOP_K]).reshape(N_SLOTS).astype(jnp.int32)
    n_valid = (pad_end[-1] // MOE_BLK).astype(jnp.int32).reshape(1)
    blk_start = jnp.arange(N_MOE_BLOCKS, dtype=jnp.int32) * MOE_BLK
    block_e = jnp.minimum(jnp.sum((blk_start[:, None] >= pad_end[None, :]).astype(jnp.int32), axis=1),
                          N_EXPERTS - 1).astype(jnp.int32)
    zero_start = (pad_start + counts).astype(jnp.int32)
    zero_len = (padded - counts).astype(jnp.int32)
    return dest, block_e, n_valid, jnp.concatenate([zero_start, zero_len, n_valid])


def _dispatch_kernel(dest_ref, zs_ref, h_ref, xs_ref, zero_scr, sem, zsem):
    i = pl.program_id(0)

    @pl.when(i == 0)
    def _():
        zero_scr[...] = jnp.zeros_like(zero_scr)

        def pieces(e, do):
            off = zs_ref[e]
            rem = zs_ref[N_EXPERTS + e]
            bit = MOE_BLK // 2
            while bit >= 1:
                take = (rem & bit) != 0

                @pl.when(take)
                def _(off=off, bit=bit):
                    do(pltpu.make_async_copy(zero_scr.at[pl.ds(0, bit)], xs_ref.at[pl.ds(off, bit)], zsem))

                off = off + jnp.where(take, bit, 0)
                bit //= 2

        def start_e(e, c):
            pieces(e, lambda cp: cp.start())
            return c

        def wait_e(e, c):
            pieces(e, lambda cp: cp.wait())
            return c

        def tail(do):
            def step(b, c):
                do(pltpu.make_async_copy(zero_scr, xs_ref.at[pl.ds(b * MOE_BLK, MOE_BLK)], zsem))
                return c
            return step

        n_valid = zs_ref[2 * N_EXPERTS]
        lax.fori_loop(0, N_EXPERTS, start_e, 0)
        lax.fori_loop(n_valid, N_MOE_BLOCKS, tail(lambda cp: cp.start()), 0)
        lax.fori_loop(0, N_EXPERTS, wait_e, 0)
        lax.fori_loop(n_valid, N_MOE_BLOCKS, tail(lambda cp: cp.wait()), 0)

    base = i * TM

    def body(r, carry):
        for kk in range(TOP_K):
            d = dest_ref[kk * T_ALL + base + r]
            pltpu.make_async_copy(h_ref.at[r], xs_ref.at[d], sem).start(priority=kk)
        return carry

    lax.fori_loop(0, TM, body, 0)
    for kk in range(TOP_K):
        pltpu.make_async_copy(h_ref, xs_ref.at[pl.ds(0, TM)], sem).wait()


def _dispatch(dest, zero_start, h_all):
    return pl.pallas_call(
        _dispatch_kernel,
        grid_spec=pltpu.PrefetchScalarGridSpec(
            num_scalar_prefetch=2,
            grid=(N_ROW_BLOCKS,),
            in_specs=[pl.BlockSpec((TM, ROW_TILE, LANES), lambda i, d, z: (i, 0, 0))],
            out_specs=pl.BlockSpec(memory_space=pl.ANY),
            scratch_shapes=[pltpu.VMEM((MOE_BLK, ROW_TILE, LANES), BF16), pltpu.SemaphoreType.DMA(()),
                            pltpu.SemaphoreType.DMA(())],
        ),
        out_shape=jax.ShapeDtypeStruct((N_SORT_ROWS, ROW_TILE, LANES), BF16),
        compiler_params=_cparams(("arbitrary",)),
        name="moe_dispatch",
    )(dest, zero_start, h_all)


def _experts_kernel(be_ref, nv_ref, x_ref, wg_ref, wu_ref, wd_ref, y_ref, wg_s, wu_s, wd_s):
    i = pl.program_id(0)

    @pl.when(i < nv_ref[0])
    def _():
        e = be_ref[i]
        prev = be_ref[jnp.maximum(i - 1, 0)]

        @pl.when((i == 0) | (e != prev))
        def _():
            wg_s[...] = wg_ref[...].astype(BF16)
            wu_s[...] = wu_ref[...].astype(BF16)
            wd_s[...] = wd_ref[...].astype(BF16)

        xb = x_ref[...].reshape(MOE_BLK, D_MODEL)
        a = jax.nn.silu(_dot(xb, wg_s[...])) * _dot(xb, wu_s[...])
        y_ref[...] = _dot(a.astype(BF16), wd_s[...]).reshape(y_ref.shape)

    @pl.when(i >= nv_ref[0])
    def _():
        y_ref[...] = jnp.zeros(y_ref.shape, y_ref.dtype)


def _experts(block_e, n_valid, xs, w_gate, w_up, w_down, layer):
    def blk(i, be, nv):
        return jnp.maximum(jnp.minimum(i, nv[0] - 1), 0)

    def wmap(i, be, nv):
        return (layer, be[blk(i, be, nv)], 0, 0)

    return pl.pallas_call(
        _experts_kernel,
        grid_spec=pltpu.PrefetchScalarGridSpec(
            num_scalar_prefetch=2,
            grid=(N_MOE_BLOCKS,),
            in_specs=[pl.BlockSpec((MOE_BLK, ROW_TILE, LANES), lambda i, be, nv: (blk(i, be, nv), 0, 0)),
                      pl.BlockSpec((None, None, D_MODEL, D_EXPERT), wmap),
                      pl.BlockSpec((None, None, D_MODEL, D_EXPERT), wmap),
                      pl.BlockSpec((None, None, D_EXPERT, D_MODEL), wmap)],
            out_specs=pl.BlockSpec((MOE_BLK, ROW_TILE, LANES), lambda i, be, nv: (i, 0, 0)),
            scratch_shapes=[pltpu.VMEM((D_MODEL, D_EXPERT), BF16), pltpu.VMEM((D_MODEL, D_EXPERT), BF16),
                            pltpu.VMEM((D_EXPERT, D_MODEL), BF16)],
        ),
        out_shape=jax.ShapeDtypeStruct((N_SORT_ROWS, ROW_TILE, LANES), F32),
        compiler_params=_cparams(("arbitrary",)),
        name="moe_experts",
    )(block_e, n_valid, xs, w_gate, w_up, w_down)


def _gather_rows(dest_ref, ys_ref, ybuf, sem, i):
    base = i * TM

    def body(r, carry):
        for kk in range(TOP_K):
            d = dest_ref[kk * T_ALL + base + r]
            pltpu.make_async_copy(ys_ref.at[d], ybuf.at[kk, r], sem).start(priority=kk)
        return carry

    lax.fori_loop(0, TM, body, 0)
    for kk in range(TOP_K):
        pltpu.make_async_copy(ys_ref.at[pl.ds(0, TM)], ybuf.at[kk], sem).wait()


def _combined(x_ref, rg_ref, ybuf):
    rg = rg_ref[...]
    y0 = ybuf[0].reshape(TM, D_MODEL)
    y1 = ybuf[1].reshape(TM, D_MODEL)
    return x_ref[...] + rg[:, 0:1] * y0 + rg[:, 1:2] * y1


def _combine_kernel(dest_ref, x_ref, rg_ref, ys_ref, o_ref, ybuf, sem):
    _gather_rows(dest_ref, ys_ref, ybuf, sem, pl.program_id(0))
    o_ref[...] = _combined(x_ref, rg_ref, ybuf)


def _combine(dest, x_all, rg_all, ys):
    return pl.pallas_call(
        _combine_kernel,
        grid_spec=pltpu.PrefetchScalarGridSpec(
            num_scalar_prefetch=1,
            grid=(N_ROW_BLOCKS,),
            in_specs=[pl.BlockSpec((TM, D_MODEL), lambda i, d: (i, 0)),
                      pl.BlockSpec((TM, LANES), lambda i, d: (i, 0)),
                      pl.BlockSpec(memory_space=pl.ANY)],
            out_specs=pl.BlockSpec((TM, D_MODEL), lambda i, d: (i, 0)),
            scratch_shapes=[pltpu.VMEM((TOP_K, TM, ROW_TILE, LANES), F32), pltpu.SemaphoreType.DMA(())],
        ),
        out_shape=jax.ShapeDtypeStruct((T_ALL, D_MODEL), F32),
        compiler_params=_cparams(("arbitrary",)),
        name="moe_combine",
    )(dest, x_all, rg_all, ys)


def _final_kernel(dest_ref, x_ref, rg_ref, ys_ref, nfin_ref, op_ref, os_ref, ybuf, sem):
    i = pl.program_id(0)
    _gather_rows(dest_ref, ys_ref, ybuf, sem, i)
    y = _rms(_combined(x_ref, rg_ref, ybuf), nfin_ref[...])

    @pl.when(i < N_PROMPT_BLOCKS)
    def _():
        op_ref[...] = y

    @pl.when(i >= N_PROMPT_BLOCKS)
    def _():
        os_ref[...] = y


def _final(dest, x_all, rg_all, ys, nfin):
    return pl.pallas_call(
        _final_kernel,
        grid_spec=pltpu.PrefetchScalarGridSpec(
            num_scalar_prefetch=1,
            grid=(N_ROW_BLOCKS,),
            in_specs=[pl.BlockSpec((TM, D_MODEL), lambda i, d: (i, 0)),
                      pl.BlockSpec((TM, LANES), lambda i, d: (i, 0)),
                      pl.BlockSpec(memory_space=pl.ANY),
                      pl.BlockSpec((1, D_MODEL), lambda i, d: (0, 0))],
            out_specs=[pl.BlockSpec((TM, D_MODEL), lambda i, d: (jnp.minimum(i, N_PROMPT_BLOCKS - 1), 0)),
                       pl.BlockSpec((TM, D_MODEL), lambda i, d: (0, 0))],
            scratch_shapes=[pltpu.VMEM((TOP_K, TM, ROW_TILE, LANES), F32), pltpu.SemaphoreType.DMA(())],
        ),
        out_shape=[jax.ShapeDtypeStruct((T_PROMPT, D_MODEL), F32), jax.ShapeDtypeStruct((T_SAMPLE, D_MODEL), F32)],
        compiler_params=_cparams(("arbitrary",)),
        name="moe_combine_final",
    )(dest, x_all, rg_all, ys, nfin)


def _moe(h_all, rt_all, cnt, w_gate, w_up, w_down, layer):
    dest, block_e, n_valid, zero_start = _moe_metadata(rt_all, cnt)
    xs = _dispatch(dest, zero_start, h_all)
    ys = _experts(block_e, n_valid, xs, w_gate, w_up, w_down, layer)
    return dest, ys


def _pool_project(d_groups, wp_ref, scale):
    outs = [_dot(d_groups[g].astype(BF16), wp_ref[g]) for g in range(len(POOL_SIZES))]
    return jnp.concatenate(outs, axis=1) * scale


def _mix1_prompt_kernel(x_ref, nm_ref, wp_ref, sc_ref, nf_ref, wr_ref, br_ref,
                        x3_ref, h_ref, ri_ref, rg_ref, pl_ref, cnt_ref, ext):
    i = pl.program_id(0)

    @pl.when(i == 0)
    def _():
        cnt_ref[...] = jnp.zeros_like(cnt_ref)

    x = x_ref[...]
    hp = _rms(x, nm_ref[...])

    @pl.when(i % STEPS_PER_BATCH == 0)
    def _():
        ext[0:POOL_MAX, :] = jnp.zeros((POOL_MAX, D_MODEL), F32)

    ext[POOL_MAX:, :] = hp
    pos = (i % STEPS_PER_BATCH) * TM + lax.broadcasted_iota(jnp.int32, (TM, 1), 0)
    d_groups = []
    for g, w in enumerate(POOL_SIZES):
        cols = slice(g * POOL_GROUP_DIM, (g + 1) * POOL_GROUP_DIM)
        acc = ext[:, cols]
        span = 1
        while span < w:
            acc = acc + pltpu.roll(acc, span, 0)
            span *= 2
        cnt = jnp.minimum(pos + 1, w).astype(F32)
        d_groups.append(acc[POOL_MAX:] / cnt - hp[:, cols])
    tail = hp[TM - POOL_MAX:, :]
    ext[0:POOL_MAX, :] = tail
    pl_ref[...] = tail

    x3 = x + _pool_project(d_groups, wp_ref, sc_ref[...])
    x3_ref[...] = x3
    h, ids, gates = _route(x3, nf_ref[...], wr_ref[...], br_ref[...])
    h_ref[...] = h.reshape(h_ref.shape)
    ri_ref[...] = _rank_pack(ids, cnt_ref)
    rg_ref[...] = gates


def _mix1_prompt(x_all, nm, wp, sc, nf, wr, br):
    row_spec = pl.BlockSpec((TM, D_MODEL), lambda i: (i, 0))
    row3_spec = pl.BlockSpec((TM, ROW_TILE, LANES), lambda i: (i, 0, 0))
    lane_spec = pl.BlockSpec((TM, LANES), lambda i: (i, 0))
    return pl.pallas_call(
        _prompt_steps(_mix1_prompt_kernel, 7),
        grid=(N_ROW_BLOCKS,),
        in_specs=[row_spec, _const_spec((1, D_MODEL)),
                  _const_spec((len(POOL_SIZES), POOL_GROUP_DIM, POOL_GROUP_DIM)), _const_spec((1, D_MODEL)),
                  _const_spec((1, D_MODEL)), _const_spec((D_MODEL, 2 * LANES)), _const_spec((1, LANES))],
        out_specs=[row_spec, row3_spec, pl.BlockSpec((8, TM), lambda i: (0, i)), lane_spec,
                   pl.BlockSpec((None, POOL_MAX, D_MODEL),
                                lambda i: (jnp.minimum(i // STEPS_PER_BATCH, BATCH - 1), 0, 0)),
                   _const_spec((1, LANES))],
        out_shape=[jax.ShapeDtypeStruct((T_ALL, D_MODEL), F32), jax.ShapeDtypeStruct((T_ALL, ROW_TILE, LANES), BF16),
                   jax.ShapeDtypeStruct((8, T_ALL), jnp.int32), jax.ShapeDtypeStruct((T_ALL, LANES), F32),
                   jax.ShapeDtypeStruct((BATCH, POOL_MAX, D_MODEL), F32), jax.ShapeDtypeStruct((1, LANES), F32)],
        scratch_shapes=[pltpu.VMEM((POOL_MAX + TM, D_MODEL), F32)],
        compiler_params=_cparams(("arbitrary",)),
        name="mix1_prompt",
    )(x_all, nm, wp, sc, nf, wr, br)


def _mix1_sample_kernel(x_ref, st_ref, nm_ref, wp_ref, sc_ref, nf_ref, wr_ref, br_ref, cnt_in,
                        x3_in, h_in, ri_in, rg_in,
                        x3_ref, h_ref, ri_ref, rg_ref, hs_ref, cnt_ref):
    del x3_in, h_in, ri_in, rg_in
    x = x_ref[...]
    hs = _rms(x, nm_ref[...])
    hs_ref[...] = hs
    n_ctx = POOL_MAX - 1
    d_groups = []
    for g, w in enumerate(POOL_SIZES):
        cols = slice(g * POOL_GROUP_DIM, (g + 1) * POOL_GROUP_DIM)
        parts = []
        for t in range(DEC_SEQ):
            acc = hs[t * DEC_BATCH:(t + 1) * DEC_BATCH, cols]
            for back in range(1, w):
                src = t - back
                if src >= 0:
                    acc = acc + hs[src * DEC_BATCH:(src + 1) * DEC_BATCH, cols]
                else:
                    acc = acc + st_ref[n_ctx + src, :, cols]
            parts.append(acc / float(w) - hs[t * DEC_BATCH:(t + 1) * DEC_BATCH, cols])
        d_groups.append(jnp.concatenate(parts, axis=0))
    x3 = x + _pool_project(d_groups, wp_ref, sc_ref[...])
    x3_ref[...] = x3
    h, ids, gates = _route(x3, nf_ref[...], wr_ref[...], br_ref[...])
    h_ref[...] = h.reshape(h_ref.shape)
    cnt_ref[...] = cnt_in[...]
    ri_ref[...] = _rank_pack(ids, cnt_ref)
    rg_ref[...] = gates


def _mix1_sample(x_all, state_t, nm, wp, sc, nf, wr, br, cnt, x3_all, h_all, ri_all, rg_all):
    sample_rows = pl.BlockSpec((TM, D_MODEL), lambda g: (N_PROMPT_BLOCKS, 0))
    sample_rows3 = pl.BlockSpec((TM, ROW_TILE, LANES), lambda g: (N_PROMPT_BLOCKS, 0, 0))
    sample_lanes = pl.BlockSpec((TM, LANES), lambda g: (N_PROMPT_BLOCKS, 0))
    anyspec = pl.BlockSpec(memory_space=pl.ANY)
    n_in = 9
    return pl.pallas_call(
        _mix1_sample_kernel,
        grid=(1,),
        in_specs=[sample_rows, _const_spec((POOL_MAX - 1, DEC_BATCH, D_MODEL)), _const_spec((1, D_MODEL)),
                  _const_spec((len(POOL_SIZES), POOL_GROUP_DIM, POOL_GROUP_DIM)), _const_spec((1, D_MODEL)),
                  _const_spec((1, D_MODEL)), _const_spec((D_MODEL, 2 * LANES)), _const_spec((1, LANES)),
                  _const_spec((1, LANES)), anyspec, anyspec, anyspec, anyspec],
        out_specs=[sample_rows, sample_rows3, pl.BlockSpec((8, TM), lambda g: (0, N_PROMPT_BLOCKS)), sample_lanes,
                   _const_spec((T_SAMPLE, D_MODEL)), _const_spec((1, LANES))],
        out_shape=[jax.ShapeDtypeStruct((T_ALL, D_MODEL), F32), jax.ShapeDtypeStruct((T_ALL, ROW_TILE, LANES), BF16),
                   jax.ShapeDtypeStruct((8, T_ALL), jnp.int32), jax.ShapeDtypeStruct((T_ALL, LANES), F32),
                   jax.ShapeDtypeStruct((T_SAMPLE, D_MODEL), F32), jax.ShapeDtypeStruct((1, LANES), F32)],
        input_output_aliases={n_in: 0, n_in + 1: 1, n_in + 2: 2, n_in + 3: 3},
        compiler_params=_cparams(("arbitrary",)),
        name="mix1_sample",
    )(x_all, state_t, nm, wp, sc, nf, wr, br, cnt, x3_all, h_all, ri_all, rg_all)


def _router_weights(wg, bg, we, be):
    w = jnp.concatenate([wg, jnp.transpose(we, (1, 0, 2)).reshape(D_MODEL, N_EXPERTS)], axis=1)
    b = jnp.concatenate([bg, be.reshape(N_EXPERTS)])
    pad = LANES - N_GROUPS - N_EXPERTS
    w = jnp.pad(w, ((0, 0), (0, pad)))
    w_hi = w.astype(BF16)
    w_lo = (w - w_hi.astype(F32)).astype(BF16)
    return jnp.concatenate([w_hi, w_lo], axis=1), jnp.pad(b, (0, pad)).reshape(1, LANES)


def _stack(tab):
    return jnp.stack([jnp.concatenate([tab[h] for h in heads], axis=0) for heads in STACK_HEADS])


def kernel(x_prompt, x_sample, cache_k_win, cache_v_win, state_pool, norm_mix, norm_ffn, norm_final, w_in,
           a_ln_g, a_ln_b, a_w_s, a_b_s, b_sinks, rel_bias_table, w_out, c_w_pool, c_scale,
           router_group_w, router_group_b, router_expert_w, router_expert_b, w_gate, w_up, w_down):
    xs_t = jnp.transpose(x_sample, (1, 0, 2)).reshape(T_SAMPLE, D_MODEL)
    xp2 = x_prompt.reshape(T_PROMPT, D_MODEL)
    win =w_in[0].astype(BF16)
    wout = w_out[0].astype(BF16)
    lng = a_ln_g[0].reshape(1, A_WIDTH)
    lnb = a_ln_b[0].reshape(1, A_WIDTH)
    bias_p, bias_sc, bias_sn, ws_tril = _prep(rel_bias_table, a_w_s[0])
    wsp = ws_tril.reshape(A_HEADS // 2, 2, CHUNK, CHUNK).transpose(0, 2, 1, 3).reshape(A_HEADS // 2, CHUNK, 2 * CHUNK)
    bs_full = jnp.repeat(a_b_s[0].T, A_HEAD_DIM, axis=1)
    bias_p = jnp.stack([_stack(bias_p[0]), _stack(bias_p[1])])
    bias_sc = _stack(bias_sc)
    bias_sn = _stack(bias_sn)
    sinks = b_sinks[0]
    sink_p = jnp.stack([jnp.repeat(sinks[jnp.array(hh)], WINDOW) for hh in STACK_HEADS])
    sink_s = jnp.stack([jnp.repeat(sinks[jnp.array(hh)], 32) for hh in STACK_HEADS])
    bias_p = bias_p.at[:, :, :, 0].set(jnp.broadcast_to(sink_p[None], (2, 2, 4 * WINDOW)))
    bias_sc = bias_sc.at[:, :, 0].set(sink_s)
    pairs = [(t, s) for t in range(DEC_SEQ) for s in range(t + 1)]
    wcoef = jnp.stack([jnp.repeat(a_w_s[0][:, t, s], A_HEAD_DIM) for t, s in pairs])
    wcoef = jnp.pad(wcoef, ((0, 16 - len(pairs)), (0, 0)))
    bcoef = jnp.pad(jnp.repeat(a_b_s[0][:, :DEC_SEQ].T, A_HEAD_DIM, axis=1), ((0, 8 - DEC_SEQ), (0, 0)))
    ck = cache_k_win
    cv = cache_v_win
    routers = [_router_weights(router_group_w[l], router_group_b[l], router_expert_w[l], router_expert_b[l])
               for l in range(2)]
    nm = [norm_mix[l].reshape(1, D_MODEL) for l in range(2)]
    nf = [norm_ffn[l].reshape(1, D_MODEL) for l in range(2)]

    x1_all, h_all, ri_all, rg_all, k_last, v_last, va_last, cnt0 = _mix0_prompt(
        xp2, nm[0], win, lng, lnb, wsp, bs_full, bias_p, wout, nf[0], *routers[0])
    x1_all, h_all, ri_all, rg_all, k_new, v_new, va_s, cnt0 = _mix0_sample(
        xs_t, nm[0], win, lng, lnb, wcoef, bcoef, ck, cv, bias_sc, bias_sn, wout, nf[0], *routers[0], cnt0,
        x1_all, h_all, ri_all, rg_all)
    dest0, ys0 = _moe(h_all, ri_all, cnt0, w_gate, w_up, w_down, 0)
    x2_all = _combine(dest0, x1_all, rg_all, ys0)

    wp = c_w_pool[0].astype(BF16)
    sc = c_scale[0].reshape(1, D_MODEL)
    x3_all, h2_all, ri2_all, rg2_all, pool_tail, cnt1 = _mix1_prompt(x2_all, nm[1], wp, sc, nf[1], *routers[1])
    state_t = jnp.transpose(state_pool[0], (1, 0, 2))
    x3_all, h2_all, ri2_all, rg2_all, hs1, cnt1 = _mix1_sample(
        x2_all, state_t, nm[1], wp, sc, nf[1], *routers[1], cnt1, x3_all, h2_all, ri2_all, rg2_all)
    dest1, ys1 = _moe(h2_all, ri2_all, cnt1, w_gate, w_up, w_down, 1)
    y_p, y_s = _final(dest1, x3_all, rg2_all, ys1, norm_final.reshape(1, D_MODEL))

    def from_tmajor(a, width):
        return jnp.transpose(a.reshape(DEC_SEQ, DEC_BATCH, width), (1, 0, 2))

    y_prompt = y_p.reshape(BATCH, SEQ, D_MODEL)
    y_sample = from_tmajor(y_s, D_MODEL)
    win_k_p = k_last.reshape(1, BATCH, WINDOW, B_KV_HEADS, B_HEAD_DIM)
    win_v_p = v_last.reshape(1, BATCH, WINDOW, B_KV_HEADS, B_HEAD_DIM)
    kn = from_tmajor(k_new, KV_WIDTH).reshape(DEC_BATCH, DEC_SEQ, B_KV_HEADS, B_HEAD_DIM)
    vn = from_tmajor(v_new, KV_WIDTH).reshape(DEC_BATCH, DEC_SEQ, B_KV_HEADS, B_HEAD_DIM)
    win_k_s = jnp.concatenate([cache_k_win[0][:, DEC_SEQ:], kn], axis=1)[None]
    win_v_s = jnp.concatenate([cache_v_win[0][:, DEC_SEQ:], vn], axis=1)[None]
    chunk_v_p = va_last.reshape(1, BATCH, CHUNK, A_HEADS, A_HEAD_DIM)
    chunk_v_s = from_tmajor(va_s, A_WIDTH).reshape(1, DEC_BATCH, DEC_SEQ, A_HEADS, A_HEAD_DIM)
    pool_p = pool_tail[:, 1:][None]
    pool_s = jnp.concatenate([state_pool[0][:, DEC_SEQ:], from_tmajor(hs1, D_MODEL)], axis=1)[None]
    return (y_prompt, y_sample, win_k_p, win_v_p, win_k_s, win_v_s, chunk_v_p, chunk_v_s, pool_p, pool_s)
```

```python
import functools
import math

import numpy as np
import jax
import jax.numpy as jnp
from jax import lax
from jax.experimental import pallas as pl
from jax.experimental.pallas import tpu as pltpu

F32 = jnp.float32
BF16 = jnp.bfloat16

D_MODEL = 1024
BATCH = 2
SEQ = 8192
DEC_BATCH = 128
DEC_SEQ = 4
A_WIDTH = 512
A_HEADS = 8
A_HEAD_DIM = 64
CHUNK = 128
B_HEADS = 8
B_KV_HEADS = 2
B_HEAD_DIM = 64
B_GROUP = 4
WINDOW = 128
N_BUCKETS = 32
MAX_DISTANCE = WINDOW
Q_WIDTH = 512
KV_WIDTH = 128
IN_WIDTH = 2 * A_WIDTH + Q_WIDTH + 2 * KV_WIDTH
ATTN_SCALE = B_HEAD_DIM ** -0.5
NEG_INF = -1e30
POOL_SIZES = (2, 4, 8, 16)
POOL_GROUP_DIM = 256
POOL_MAX = 16
N_GROUPS = 4
EXPERTS_PER_GROUP = 8
N_EXPERTS = 32
TOP_K = 2
D_EXPERT = 512
EPS = 1e-6

LANES = 128
ROW_TILE = D_MODEL // LANES
T_PROMPT = BATCH * SEQ
T_SAMPLE = DEC_BATCH * DEC_SEQ
T_ALL = T_PROMPT + T_SAMPLE
TM = 512
N_PROMPT_BLOCKS = T_PROMPT // TM
N_ROW_BLOCKS = T_ALL // TM
STEPS_PER_BATCH = SEQ // TM
SUB = TM // WINDOW
N_SLOTS = T_ALL * TOP_K
MOE_BLK = 512
N_MOE_BLOCKS = N_SLOTS // MOE_BLK + N_EXPERTS
N_SORT_ROWS = N_MOE_BLOCKS * MOE_BLK
SAMPLE_GROUP = 8
N_SAMPLE_GROUPS = DEC_BATCH // SAMPLE_GROUP
VMEM_LIMIT = 56 * 1024 * 1024

STACK_HEADS = ((0, 2, 5, 7), (1, 3, 4, 6))


def _t5_bucket_np(dist):
    n = np.maximum(dist, 0)
    max_exact = N_BUCKETS // 2
    nf = np.maximum(n, 1).astype(np.float32)
    large = max_exact + (np.log(nf / np.float32(max_exact)) / np.float32(math.log(MAX_DISTANCE / max_exact))
                         * np.float32(N_BUCKETS - max_exact)).astype(np.int32)
    large = np.minimum(large, N_BUCKETS - 1)
    return np.where(n < max_exact, n, large).astype(np.int32)


def _bucket_tables():
    qi = np.arange(WINDOW)[:, None]
    ki = np.arange(2 * WINDOW)[None, :]
    dist = qi + WINDOW - ki
    valid = (dist >= 0) & (dist < WINDOW)
    bp = np.where(valid, _t5_bucket_np(dist), -1)
    bp_first = np.where(ki >= WINDOW, bp, -1)
    bkt_p = np.stack([bp_first, bp]).astype(np.int32)

    t = np.repeat(np.arange(DEC_SEQ), SAMPLE_GROUP)[:, None]
    b = np.tile(np.arange(SAMPLE_GROUP), DEC_SEQ)[:, None]
    cb = np.repeat(np.arange(SAMPLE_GROUP), WINDOW)[None, :]
    cj = np.tile(np.arange(WINDOW), SAMPLE_GROUP)[None, :]
    dist_c = t + WINDOW - cj
    valid_c = (cb == b) & (dist_c >= 0) & (dist_c < WINDOW)
    bkt_sc = np.where(valid_c, _t5_bucket_np(dist_c), -1).astype(np.int32)
    nt = np.repeat(np.arange(DEC_SEQ), SAMPLE_GROUP)[None, :]
    nb = np.tile(np.arange(SAMPLE_GROUP), DEC_SEQ)[None, :]
    dist_n = t - nt
    valid_n = (nb == b) & (dist_n >= 0)
    bkt_sn = np.where(valid_n, _t5_bucket_np(dist_n), -1).astype(np.int32)
    bkt_sn = np.concatenate([bkt_sn, np.full((32, LANES - 32), -1, np.int32)], axis=1)
    return bkt_p, bkt_sc, bkt_sn


_BKT_P, _BKT_SC, _BKT_SN = _bucket_tables()


def _cparams(semantics):
    return pltpu.CompilerParams(dimension_semantics=semantics, vmem_limit_bytes=VMEM_LIMIT)


def _rms(x, g):
    return x * lax.rsqrt(jnp.mean(x * x, axis=-1, keepdims=True) + EPS) * g


def _layernorm(x, g, b):
    xc = x - jnp.mean(x, axis=-1, keepdims=True)
    return xc * lax.rsqrt(jnp.mean(xc * xc, axis=-1, keepdims=True) + EPS) * g + b


def _dot(a, b):
    return jnp.dot(a, b, preferred_element_type=F32)


def _dot_nt(a, b):
    return lax.dot_general(a, b, (((1,), (1,)), ((), ())), preferred_element_type=F32)


def _project(x, nm, win, lng, lnb):
    h = _rms(x, nm)
    z = _dot(h.astype(BF16), win)
    u = jax.nn.gelu(z[:, :A_WIDTH])
    va = _layernorm(jax.nn.gelu(z[:, A_WIDTH:2 * A_WIDTH]), lng, lnb)
    q = z[:, 2 * A_WIDTH:2 * A_WIDTH + Q_WIDTH] * ATTN_SCALE
    k = z[:, 2 * A_WIDTH + Q_WIDTH:2 * A_WIDTH + Q_WIDTH + KV_WIDTH]
    v = z[:, 2 * A_WIDTH + Q_WIDTH + KV_WIDTH:]
    return u, va, q, k, v


def _route(x1, nf, wr, br):
    hf = _rms(x1, nf)
    h = hf.astype(BF16)
    h_lo = (hf - h.astype(F32)).astype(BF16)
    part = _dot(h, wr)
    logits = part[:, :LANES] + part[:, LANES:] + _dot(h_lo, wr[:, :LANES]) + br
    rows = logits.shape[0]
    lane = lax.broadcasted_iota(jnp.int32, (rows, LANES), 1)
    lanef = lane.astype(F32)
    big = jnp.float32(1e9)
    is_g = lane < N_GROUPS
    gl = jnp.where(is_g, logits, -jnp.inf)
    gmax = jnp.max(gl, axis=1, keepdims=True)
    gsel = jnp.min(jnp.where(gl == gmax, lanef, big), axis=1, keepdims=True)
    gsum = jnp.sum(jnp.where(is_g, jnp.exp(logits - gmax), 0.0), axis=1, keepdims=True)
    g1 = 1.0 / gsum
    lo = N_GROUPS + EXPERTS_PER_GROUP * gsel
    emask = (lanef >= lo) & (lanef < lo + EXPERTS_PER_GROUP)
    el = jnp.where(emask, logits, -jnp.inf)
    v1 = jnp.max(el, axis=1, keepdims=True)
    i1 = jnp.min(jnp.where(el == v1, lanef, big), axis=1, keepdims=True)
    el2 = jnp.where(lanef == i1, -jnp.inf, el)
    v2 = jnp.max(el2, axis=1, keepdims=True)
    i2 = jnp.min(jnp.where(el2 == v2, lanef, big), axis=1, keepdims=True)
    e2 = jnp.exp(v2 - v1)
    den = 1.0 + e2
    w1 = g1 / den
    w2 = g1 * e2 / den
    ids = jnp.where(lane == 0, i1 - N_GROUPS, jnp.where(lane == 1, i2 - N_GROUPS, 0.0)).astype(jnp.int32)
    gates = jnp.where(lane == 0, w1, jnp.where(lane == 1, w2, 0.0))
    return h, ids, gates


def _rank_pack(ids, cnt_ref):
    rows = ids.shape[0]
    lane = lax.broadcasted_iota(jnp.int32, (rows, LANES), 1)
    o0 = (lane == ids[:, 0:1]).astype(F32)
    o1 = (lane == ids[:, 1:2]).astype(F32)
    r = lax.broadcasted_iota(jnp.int32, (rows, rows), 0)
    c = lax.broadcasted_iota(jnp.int32, (rows, rows), 1)
    before = (c < r).astype(BF16)
    p01 = _dot(before, jnp.concatenate([o0, o1], axis=1).astype(BF16))
    p0 = p01[:, :LANES]
    p1 = p01[:, LANES:]
    c0 = jnp.sum(o0, axis=0, keepdims=True)
    c1 = jnp.sum(o1, axis=0, keepdims=True)
    carry = cnt_ref[...]
    rank0 = jnp.sum(o0 * (carry + p0), axis=1, keepdims=True)
    rank1 = jnp.sum(o1 * (carry + c0 + p1), axis=1, keepdims=True)
    cnt_ref[...] = carry + c0 + c1
    idf = ids.astype(F32)
    packed = jnp.where(lane < TOP_K, idf, jnp.where(lane == 2, rank0, jnp.where(lane == 3, rank1, 0.0)))
    return jnp.transpose(packed)[:8].astype(jnp.int32)


def _prep_kernel(tab_ref, bp_ref, bsc_ref, bsn_ref, ws_ref, op_ref, osc_ref, osn_ref, ows_ref):
    def fill(bkt, write):
        for h in range(B_HEADS):
            acc = jnp.full(bkt.shape, NEG_INF, F32)
            for b in range(N_BUCKETS):
                acc = jnp.where(bkt == b, tab_ref[b, h], acc)
            write(h, acc)

    for var in range(2):
        def wr_p(h, acc, var=var):
            op_ref[var, h] = acc
        fill(bp_ref[var], wr_p)

    def wr_sc(h, acc):
        osc_ref[h] = acc
    fill(bsc_ref[...], wr_sc)

    def wr_sn(h, acc):
        osn_ref[h] = acc
    fill(bsn_ref[...], wr_sn)

    r = lax.broadcasted_iota(jnp.int32, (CHUNK, CHUNK), 0)
    c = lax.broadcasted_iota(jnp.int32, (CHUNK, CHUNK), 1)
    for h in range(A_HEADS):
        ows_ref[h] = jnp.where(r >= c, ws_ref[h], 0.0).astype(BF16)


def _prep(rel_bias_table, w_s):
    vm = pl.BlockSpec(memory_space=pltpu.VMEM)
    return pl.pallas_call(
        _prep_kernel,
        in_specs=[pl.BlockSpec(memory_space=pltpu.SMEM), vm, vm, vm, vm],
        out_specs=[vm, vm, vm, vm],
        out_shape=[
            jax.ShapeDtypeStruct((2, B_HEADS, WINDOW, 2 * WINDOW), F32),
            jax.ShapeDtypeStruct((B_HEADS, 32, SAMPLE_GROUP * WINDOW), F32),
            jax.ShapeDtypeStruct((B_HEADS, 32, LANES), F32),
            jax.ShapeDtypeStruct((A_HEADS, CHUNK, CHUNK), BF16),
        ],
        name="prep_tables",
    )(rel_bias_table, jnp.asarray(_BKT_P), jnp.asarray(_BKT_SC), jnp.asarray(_BKT_SN), w_s)


def _gate_pairs(va_rows, wsp_ref, lane_lo):
    outs = []
    for p in range(A_HEADS // 2):
        vp = va_rows[:, p * LANES:(p + 1) * LANES]
        rhs = jnp.concatenate([jnp.where(lane_lo, vp, 0.0), jnp.where(lane_lo, 0.0, vp)], axis=0).astype(BF16)
        outs.append(_dot(wsp_ref[p], rhs))
    return jnp.concatenate(outs, axis=1)


def _prompt_steps(body, first_row_out):
    def kern(*refs):
        i = pl.program_id(0)

        @pl.when(i < N_PROMPT_BLOCKS)
        def _():
            body(*refs)

        @pl.when(i >= N_PROMPT_BLOCKS)
        def _():
            for r in refs[first_row_out:first_row_out + 4]:
                r[...] = jnp.zeros(r.shape, r.dtype)

    return kern


def _mix0_prompt_kernel(x_ref, nm_ref, win_ref, lng_ref, lnb_ref, wsp_ref, bs_ref, bias_ref,
                        wout_ref, nf_ref, wr_ref, br_ref,
                        x1_ref, h_ref, ri_ref, rg_ref, kl_ref, vl_ref, val_ref, cnt_ref,
                        kprev, vprev, mix_scr):
    @pl.when(pl.program_id(0) == 0)
    def _():
        cnt_ref[...] = jnp.zeros_like(cnt_ref)

    x = x_ref[...]
    u, va, q, k, v = _project(x, nm_ref[...], win_ref[...], lng_ref[...], lnb_ref[...])
    lane_lo = lax.broadcasted_iota(jnp.int32, (WINDOW, LANES), 1) < B_HEAD_DIM
    row0 = lax.broadcasted_iota(jnp.int32, (WINDOW, KV_WIDTH), 0) == 0
    first = pl.program_id(0) % STEPS_PER_BATCH == 0

    @pl.when(first)
    def _():
        kprev[...] = jnp.zeros_like(kprev)
        vprev[...] = jnp.zeros_like(vprev)

    for j in range(SUB):
        rows = slice(j * WINDOW, (j + 1) * WINDOW)
        s_gate = _gate_pairs(va[rows], wsp_ref, lane_lo)
        mix_scr[rows, :A_WIDTH] = u[rows] * (s_gate + bs_ref[...])

        if j == 0:
            kp, vp = kprev[...], vprev[...]
        else:
            prows = slice((j - 1) * WINDOW, j * WINDOW)
            kp, vp = k[prows], v[prows]
        kk = jnp.concatenate([jnp.where(row0, 0.0, kp), k[rows]], axis=0)
        vv = jnp.concatenate([jnp.where(row0, 0.0, vp), v[rows]], axis=0)
        kops = (kk.astype(BF16), pltpu.roll(kk, B_HEAD_DIM, 1).astype(BF16))
        vops = (vv.astype(BF16), pltpu.roll(vv, B_HEAD_DIM, 1).astype(BF16))
        qt = [q[rows, p * LANES:(p + 1) * LANES] for p in range(4)]
        q_even = [jnp.where(lane_lo, t, 0.0) for t in qt]
        q_odd = [jnp.where(lane_lo, 0.0, t) for t in qt]
        stacks = (jnp.concatenate([q_even[0], q_even[1], q_odd[2], q_odd[3]], axis=0),
                  jnp.concatenate([q_odd[0], q_odd[1], q_even[2], q_even[3]], axis=0))
        o = []
        for st in range(2):
            s = _dot_nt(stacks[st].astype(BF16), kops[st])
            if j == 0:
                bias = bias_ref[jnp.where(first, 0, 1), st]
            else:
                bias = bias_ref[1, st]
            s = s + bias
            m = jnp.max(s, axis=-1, keepdims=True)
            p = jnp.exp(s - m)
            den = jnp.sum(p, axis=-1, keepdims=True)
            o.append(_dot(p.astype(BF16), vops[st]) / den)
        oa, ob = o
        sl = [slice(i * WINDOW, (i + 1) * WINDOW) for i in range(4)]
        tiles = (jnp.where(lane_lo, oa[sl[0]], ob[sl[0]]), jnp.where(lane_lo, oa[sl[1]], ob[sl[1]]),
                 jnp.where(lane_lo, ob[sl[2]], oa[sl[2]]), jnp.where(lane_lo, ob[sl[3]], oa[sl[3]]))
        for p in range(4):
            mix_scr[rows, A_WIDTH + p * LANES:A_WIDTH + (p + 1) * LANES] = tiles[p]

    last = slice(TM - WINDOW, TM)
    kprev[...] = k[last]
    vprev[...] = v[last]
    kl_ref[...] = k[last]
    vl_ref[...] = v[last]
    val_ref[...] = va[last]

    x1 = x + _dot(mix_scr[...].astype(BF16), wout_ref[...])
    x1_ref[...] = x1
    h, ids, gates = _route(x1, nf_ref[...], wr_ref[...], br_ref[...])
    h_ref[...] = h.reshape(h_ref.shape)
    ri_ref[...] = _rank_pack(ids, cnt_ref)
    rg_ref[...] = gates


def _const_spec(shape):
    nd = len(shape)
    return pl.BlockSpec(shape, lambda i, _n=nd: (0,) * _n)


def _mix0_prompt(x_all, nm, win, lng, lnb, wsp, bs_full, bias_p, wout, nf, wr, br):
    row_spec = pl.BlockSpec((TM, D_MODEL), lambda i: (i, 0))
    row3_spec = pl.BlockSpec((TM, ROW_TILE, LANES), lambda i: (i, 0, 0))
    lane_spec = pl.BlockSpec((TM, LANES), lambda i: (i, 0))
    last_kv = pl.BlockSpec((None, WINDOW, KV_WIDTH), lambda i: (jnp.minimum(i // STEPS_PER_BATCH, BATCH - 1), 0, 0))
    last_va = pl.BlockSpec((None, WINDOW, A_WIDTH), lambda i: (jnp.minimum(i // STEPS_PER_BATCH, BATCH - 1), 0, 0))
    return pl.pallas_call(
        _prompt_steps(_mix0_prompt_kernel, 12),
        grid=(N_ROW_BLOCKS,),
        in_specs=[pl.BlockSpec((TM, D_MODEL), lambda i: (jnp.minimum(i, N_PROMPT_BLOCKS - 1), 0)),
                  _const_spec((1, D_MODEL)), _const_spec((D_MODEL, IN_WIDTH)),
                  _const_spec((1, A_WIDTH)), _const_spec((1, A_WIDTH)),
                  _const_spec((A_HEADS // 2, CHUNK, 2 * CHUNK)), _const_spec((CHUNK, A_WIDTH)),
                  _const_spec((2, 2, 4 * WINDOW, 2 * WINDOW)),
                  _const_spec((A_WIDTH + Q_WIDTH, D_MODEL)), _const_spec((1, D_MODEL)),
                  _const_spec((D_MODEL, 2 * LANES)), _const_spec((1, LANES))],
        out_specs=[row_spec, row3_spec, pl.BlockSpec((8, TM), lambda i: (0, i)), lane_spec,
                   last_kv, last_kv, last_va, _const_spec((1, LANES))],
        out_shape=[jax.ShapeDtypeStruct((T_ALL, D_MODEL), F32), jax.ShapeDtypeStruct((T_ALL, ROW_TILE, LANES), BF16),
                   jax.ShapeDtypeStruct((8, T_ALL), jnp.int32), jax.ShapeDtypeStruct((T_ALL, LANES), F32),
                   jax.ShapeDtypeStruct((BATCH, WINDOW, KV_WIDTH), F32),
                   jax.ShapeDtypeStruct((BATCH, WINDOW, KV_WIDTH), F32),
                   jax.ShapeDtypeStruct((BATCH, WINDOW, A_WIDTH), F32),
                   jax.ShapeDtypeStruct((1, LANES), F32)],
        scratch_shapes=[pltpu.VMEM((WINDOW, KV_WIDTH), F32), pltpu.VMEM((WINDOW, KV_WIDTH), F32),
                        pltpu.VMEM((TM, D_MODEL), F32)],
        compiler_params=_cparams(("arbitrary",)),
        name="mix0_prompt",
    )(x_all, nm, win, lng, lnb, wsp, bs_full, bias_p, wout, nf, wr, br)


def _mix0_sample_kernel(x_ref, nm_ref, win_ref, lng_ref, lnb_ref, wcoef_ref, bcoef_ref,
                        ck_ref, cv_ref, bsc_ref, bsn_ref,
                        wout_ref, nf_ref, wr_ref, br_ref, cnt_in,
                        x1_in, h_in, ri_in, rg_in,
                        x1_ref, h_ref, ri_ref, rg_ref, kn_ref, vn_ref, va_ref, cnt_ref,
                        q_scr, k_scr, v_scr, mix_scr):
    del x1_in, h_in, ri_in, rg_in
    g = pl.program_id(0)

    @pl.when(g == 0)
    def _():
        u, va, q, k, v = _project(x_ref[...], nm_ref[...], win_ref[...], lng_ref[...], lnb_ref[...])
        q_scr[...] = q
        k_scr[...] = k
        v_scr[...] = v
        kn_ref[...] = k
        vn_ref[...] = v
        va_ref[...] = va
        idx = 0
        for t in range(DEC_SEQ):
            acc = jnp.zeros((DEC_BATCH, A_WIDTH), F32) + bcoef_ref[t:t + 1, :]
            for s in range(t + 1):
                acc = acc + wcoef_ref[idx:idx + 1, :] * va[s * DEC_BATCH:(s + 1) * DEC_BATCH]
                idx += 1
            mix_scr[t * DEC_BATCH:(t + 1) * DEC_BATCH, :A_WIDTH] = u[t * DEC_BATCH:(t + 1) * DEC_BATCH] * acc

    b0 = pl.multiple_of(g * SAMPLE_GROUP, SAMPLE_GROUP)
    lane_lo = lax.broadcasted_iota(jnp.int32, (DEC_SEQ * SAMPLE_GROUP, LANES), 1) < B_HEAD_DIM

    def grab(ref, width):
        return jnp.concatenate([ref[pl.ds(t * DEC_BATCH + b0, SAMPLE_GROUP), :] for t in range(DEC_SEQ)], axis=0)

    qg = grab(q_scr, Q_WIDTH)
    kn = grab(k_scr, KV_WIDTH)
    vn = grab(v_scr, KV_WIDTH)
    crow0 = lax.broadcasted_iota(jnp.int32, (SAMPLE_GROUP * WINDOW, KV_WIDTH), 0) == 0
    rows_kv = (SAMPLE_GROUP * WINDOW, KV_WIDTH)
    kc = jnp.where(crow0, 0.0, ck_ref[...].reshape(rows_kv))
    vc = jnp.where(crow0, 0.0, cv_ref[...].reshape(rows_kv))
    kc_ops = (kc.astype(BF16), pltpu.roll(kc, B_HEAD_DIM, 1).astype(BF16))
    vc_ops = (vc.astype(BF16), pltpu.roll(vc, B_HEAD_DIM, 1).astype(BF16))
    kn_ops = (kn.astype(BF16), pltpu.roll(kn, B_HEAD_DIM, 1).astype(BF16))
    vn_ops = (vn.astype(BF16), pltpu.roll(vn, B_HEAD_DIM, 1).astype(BF16))
    qt = [qg[:, p * LANES:(p + 1) * LANES] for p in range(4)]
    q_even = [jnp.where(lane_lo, t, 0.0) for t in qt]
    q_odd = [jnp.where(lane_lo, 0.0, t) for t in qt]
    stacks = (jnp.concatenate([q_even[0], q_even[1], q_odd[2], q_odd[3]], axis=0),
              jnp.concatenate([q_odd[0], q_odd[1], q_even[2], q_even[3]], axis=0))
    o = []
    for st in range(2):
        qs = stacks[st].astype(BF16)
        sc = _dot_nt(qs, kc_ops[st]) + bsc_ref[st]
        sn = _dot_nt(qs, kn_ops[st]) + bsn_ref[st][:, :DEC_SEQ * SAMPLE_GROUP]
        m = jnp.maximum(jnp.max(sc, axis=-1, keepdims=True), jnp.max(sn, axis=-1, keepdims=True))
        pc = jnp.exp(sc - m)
        pn = jnp.exp(sn - m)
        den = jnp.sum(pc, axis=-1, keepdims=True) + jnp.sum(pn, axis=-1, keepdims=True)
        o.append((_dot(pc.astype(BF16), vc_ops[st]) + _dot(pn.astype(BF16), vn_ops[st])) / den)
    oa, ob = o
    n = DEC_SEQ * SAMPLE_GROUP
    sl = [slice(i * n, (i + 1) * n) for i in range(4)]
    tiles = (jnp.where(lane_lo, oa[sl[0]], ob[sl[0]]), jnp.where(lane_lo, oa[sl[1]], ob[sl[1]]),
             jnp.where(lane_lo, ob[sl[2]], oa[sl[2]]), jnp.where(lane_lo, ob[sl[3]], oa[sl[3]]))
    for p in range(4):
        for t in range(DEC_SEQ):
            mix_scr[pl.ds(t * DEC_BATCH + b0, SAMPLE_GROUP), A_WIDTH + p * LANES:A_WIDTH + (p + 1) * LANES] = (
                tiles[p][t * SAMPLE_GROUP:(t + 1) * SAMPLE_GROUP])

    @pl.when(g == N_SAMPLE_GROUPS - 1)
    def _():
        x1 = x_ref[...] + _dot(mix_scr[...].astype(BF16), wout_ref[...])
        x1_ref[...] = x1
        h, ids, gates = _route(x1, nf_ref[...], wr_ref[...], br_ref[...])
        h_ref[...] = h.reshape(h_ref.shape)
        cnt_ref[...] = cnt_in[...]
        ri_ref[...] = _rank_pack(ids, cnt_ref)
        rg_ref[...] = gates


def _mix0_sample(x_all, nm, win, lng, lnb, wcoef, bcoef, ck, cv, bias_sc, bias_sn, wout, nf, wr, br, cnt,
                 x1_all, h_all, ri_all, rg_all):
    sample_rows = pl.BlockSpec((TM, D_MODEL), lambda g: (N_PROMPT_BLOCKS, 0))
    sample_rows3 = pl.BlockSpec((TM, ROW_TILE, LANES), lambda g: (N_PROMPT_BLOCKS, 0, 0))
    sample_lanes = pl.BlockSpec((TM, LANES), lambda g: (N_PROMPT_BLOCKS, 0))
    cache_spec = pl.BlockSpec((None, SAMPLE_GROUP, WINDOW, B_KV_HEADS, B_HEAD_DIM), lambda g: (0, g, 0, 0, 0))
    anyspec = pl.BlockSpec(memory_space=pl.ANY)
    n_in = 16
    return pl.pallas_call(
        _mix0_sample_kernel,
        grid=(N_SAMPLE_GROUPS,),
        in_specs=[_const_spec((TM, D_MODEL)), _const_spec((1, D_MODEL)), _const_spec((D_MODEL, IN_WIDTH)),
                  _const_spec((1, A_WIDTH)), _const_spec((1, A_WIDTH)),
                  _const_spec((16, A_WIDTH)), _const_spec((8, A_WIDTH)),
                  cache_spec, cache_spec,
                  _const_spec((2, 4 * 32, SAMPLE_GROUP * WINDOW)), _const_spec((2, 4 * 32, LANES)),
                  _const_spec((A_WIDTH + Q_WIDTH, D_MODEL)), _const_spec((1, D_MODEL)),
                  _const_spec((D_MODEL, 2 * LANES)), _const_spec((1, LANES)), _const_spec((1, LANES)),
                  anyspec, anyspec, anyspec, anyspec],
        out_specs=[sample_rows, sample_rows3, pl.BlockSpec((8, TM), lambda g: (0, N_PROMPT_BLOCKS)), sample_lanes,
                   _const_spec((T_SAMPLE, KV_WIDTH)), _const_spec((T_SAMPLE, KV_WIDTH)),
                   _const_spec((T_SAMPLE, A_WIDTH)), _const_spec((1, LANES))],
        out_shape=[jax.ShapeDtypeStruct((T_ALL, D_MODEL), F32), jax.ShapeDtypeStruct((T_ALL, ROW_TILE, LANES), BF16),
                   jax.ShapeDtypeStruct((8, T_ALL), jnp.int32), jax.ShapeDtypeStruct((T_ALL, LANES), F32),
                   jax.ShapeDtypeStruct((T_SAMPLE, KV_WIDTH), F32), jax.ShapeDtypeStruct((T_SAMPLE, KV_WIDTH), F32),
                   jax.ShapeDtypeStruct((T_SAMPLE, A_WIDTH), F32), jax.ShapeDtypeStruct((1, LANES), F32)],
        scratch_shapes=[pltpu.VMEM((T_SAMPLE, Q_WIDTH), F32), pltpu.VMEM((T_SAMPLE, KV_WIDTH), F32),
                        pltpu.VMEM((T_SAMPLE, KV_WIDTH), F32), pltpu.VMEM((T_SAMPLE, D_MODEL), F32)],
        input_output_aliases={n_in: 0, n_in + 1: 1, n_in + 2: 2, n_in + 3: 3},
        compiler_params=_cparams(("arbitrary",)),
        name="mix0_sample",
    )(x_all, nm, win, lng, lnb, wcoef, bcoef, ck, cv, bias_sc, bias_sn, wout, nf, wr, br, cnt,
      x1_all, h_all, ri_all, rg_all)


def _moe_metadata(rt_all, cnt):
    counts = cnt[0, :N_EXPERTS].astype(jnp.int32)
    padded = (counts + MOE_BLK - 1) // MOE_BLK * MOE_BLK
    pad_end = jnp.cumsum(padded)
    pad_start = pad_end - padded
    experts = jnp.arange(N_EXPERTS, dtype=jnp.int32)
    eid = rt_all[0:TOP_K]
    base = jnp.sum(jnp.where(eid[:, :, None] == experts[None, None, :], pad_start[None, None, :], 0), axis=-1)
    dest = (base + rt_all[TOP_K:2 * TOP_K]).reshape(N_SLOTS).astype(jnp.int32)
    n_valid = (pad_end[-1] // MOE_BLK).astype(jnp.int32).reshape(1)
    blk_start = jnp.arange(N_MOE_BLOCKS, dtype=jnp.int32) * MOE_BLK
    block_e = jnp.minimum(jnp.sum((blk_start[:, None] >= pad_end[None, :]).astype(jnp.int32), axis=1),
                          N_EXPERTS - 1).astype(jnp.int32)
    zero_start = (pad_start + counts).astype(jnp.int32)
    zero_len = (padded - counts).astype(jnp.int32)
    first = (blk_start == pad_start[block_e]).astype(jnp.int32)
    used = counts > 0
    parity = ((jnp.cumsum(used.astype(jnp.int32)) - 1) % 2)[block_e].astype(jnp.int32)
    nearest = lax.cummin(jnp.where(used, experts, N_EXPERTS)[::-1])[::-1]
    next_used = jnp.concatenate([nearest[1:], jnp.full((1,), N_EXPERTS, jnp.int32)])
    nxt = jnp.where(next_used < N_EXPERTS, next_used, -1)[block_e].astype(jnp.int32)
    plan = (block_e, first, parity, nxt, n_valid)
    return dest, plan, jnp.concatenate([zero_start, zero_len, n_valid])


def _dispatch_kernel(dest_ref, zs_ref, h_ref, xs_ref, zero_scr, sem, zsem):
    i = pl.program_id(0)

    @pl.when(i == 0)
    def _():
        zero_scr[...] = jnp.zeros_like(zero_scr)

        def pieces(e, do):
            off = zs_ref[e]
            rem = zs_ref[N_EXPERTS + e]
            bit = MOE_BLK // 2
            while bit >= 1:
                take = (rem & bit) != 0

                @pl.when(take)
                def _(off=off, bit=bit):
                    do(pltpu.make_async_copy(zero_scr.at[pl.ds(0, bit)], xs_ref.at[pl.ds(off, bit)], zsem))

                off = off + jnp.where(take, bit, 0)
                bit //= 2

        def start_e(e, c):
            pieces(e, lambda cp: cp.start())
            return c

        def wait_e(e, c):
            pieces(e, lambda cp: cp.wait())
            return c

        def tail(do):
            def step(b, c):
                do(pltpu.make_async_copy(zero_scr, xs_ref.at[pl.ds(b * MOE_BLK, MOE_BLK)], zsem))
                return c
            return step

        n_valid = zs_ref[2 * N_EXPERTS]
        lax.fori_loop(0, N_EXPERTS, start_e, 0)
        lax.fori_loop(n_valid, N_MOE_BLOCKS, tail(lambda cp: cp.start()), 0)
        lax.fori_loop(0, N_EXPERTS, wait_e, 0)
        lax.fori_loop(n_valid, N_MOE_BLOCKS, tail(lambda cp: cp.wait()), 0)

    base = i * TM

    def body(r, carry):
        for kk in range(TOP_K):
            d = dest_ref[kk * T_ALL + base + r]
            pltpu.make_async_copy(h_ref.at[r], xs_ref.at[d], sem).start(priority=kk)
        return carry

    lax.fori_loop(0, TM, body, 0)
    for kk in range(TOP_K):
        pltpu.make_async_copy(h_ref, xs_ref.at[pl.ds(0, TM)], sem).wait()


def _dispatch(dest, zero_start, h_all):
    return pl.pallas_call(
        _dispatch_kernel,
        grid_spec=pltpu.PrefetchScalarGridSpec(
            num_scalar_prefetch=2,
            grid=(N_ROW_BLOCKS,),
            in_specs=[pl.BlockSpec((TM, ROW_TILE, LANES), lambda i, d, z: (i, 0, 0))],
            out_specs=pl.BlockSpec(memory_space=pl.ANY),
            scratch_shapes=[pltpu.VMEM((MOE_BLK, ROW_TILE, LANES), BF16), pltpu.SemaphoreType.DMA(()),
                            pltpu.SemaphoreType.DMA(())],
        ),
        out_shape=jax.ShapeDtypeStruct((N_SORT_ROWS, ROW_TILE, LANES), BF16),
        compiler_params=_cparams(("arbitrary",)),
        name="moe_dispatch",
    )(dest, zero_start, h_all)


def _experts_kernel(layer, be_ref, first_ref, par_ref, nxt_ref, nv_ref,
                    x_ref, wg_hbm, wu_hbm, wd_hbm, y_ref,
                    wg_s, wu_s, wd_s, wg_f, wu_f, wd_f, wsem):
    i = pl.program_id(0)

    def fetch(e, slot):
        return (pltpu.make_async_copy(wg_hbm.at[layer, e], wg_f.at[slot], wsem.at[slot]),
                pltpu.make_async_copy(wu_hbm.at[layer, e], wu_f.at[slot], wsem.at[slot]),
                pltpu.make_async_copy(wd_hbm.at[layer, e], wd_f.at[slot], wsem.at[slot]))

    @pl.when(i < nv_ref[0])
    def _():
        e = be_ref[i]
        slot = par_ref[i]

        @pl.when(i == 0)
        def _():
            for cp in fetch(e, slot):
                cp.start()

        @pl.when(first_ref[i] == 1)
        def _():
            for cp in fetch(e, slot):
                cp.wait()
            wg_s[...] = wg_f[slot].astype(BF16)
            wu_s[...] = wu_f[slot].astype(BF16)
            wd_s[...] = wd_f[slot].astype(BF16)
            nxt = nxt_ref[i]

            @pl.when(nxt >= 0)
            def _():
                for cp in fetch(nxt, 1 - slot):
                    cp.start()

        xb = x_ref[...].reshape(MOE_BLK, D_MODEL)
        a = jax.nn.silu(_dot(xb, wg_s[...])) * _dot(xb, wu_s[...])
        y_ref[...] = _dot(a.astype(BF16), wd_s[...]).reshape(y_ref.shape)

    @pl.when(i >= nv_ref[0])
    def _():
        y_ref[...] = jnp.zeros(y_ref.shape, y_ref.dtype)


def _experts(block_e, first, parity, nxt, n_valid, xs, w_gate, w_up, w_down, layer):
    def blk(i, be, fi, pa, nx, nv):
        return (jnp.maximum(jnp.minimum(i, nv[0] - 1), 0), 0, 0)

    anyspec = pl.BlockSpec(memory_space=pl.ANY)
    return pl.pallas_call(
        functools.partial(_experts_kernel, layer),
        grid_spec=pltpu.PrefetchScalarGridSpec(
            num_scalar_prefetch=5,
            grid=(N_MOE_BLOCKS,),
            in_specs=[pl.BlockSpec((MOE_BLK, ROW_TILE, LANES), blk), anyspec, anyspec, anyspec],
            out_specs=pl.BlockSpec((MOE_BLK, ROW_TILE, LANES), lambda i, be, fi, pa, nx, nv: (i, 0, 0)),
            scratch_shapes=[pltpu.VMEM((D_MODEL, D_EXPERT), BF16), pltpu.VMEM((D_MODEL, D_EXPERT), BF16),
                            pltpu.VMEM((D_EXPERT, D_MODEL), BF16),
                            pltpu.VMEM((2, D_MODEL, D_EXPERT), F32), pltpu.VMEM((2, D_MODEL, D_EXPERT), F32),
                            pltpu.VMEM((2, D_EXPERT, D_MODEL), F32), pltpu.SemaphoreType.DMA((2,))],
        ),
        out_shape=jax.ShapeDtypeStruct((N_SORT_ROWS, ROW_TILE, LANES), F32),
        compiler_params=_cparams(("arbitrary",)),
        name="moe_experts",
    )(block_e, first, parity, nxt, n_valid, xs, w_gate, w_up, w_down)


def _gather_rows(dest_ref, ys_ref, ybuf, sem, i):
    base = i * TM

    def body(r, carry):
        for kk in range(TOP_K):
            d = dest_ref[kk * T_ALL + base + r]
            pltpu.make_async_copy(ys_ref.at[d], ybuf.at[kk, r], sem).start(priority=kk)
        return carry

    lax.fori_loop(0, TM, body, 0)
    for kk in range(TOP_K):
        pltpu.make_async_copy(ys_ref.at[pl.ds(0, TM)], ybuf.at[kk], sem).wait()


def _combined(x_ref, rg_ref, ybuf):
    rg = rg_ref[...]
    y0 = ybuf[0].reshape(TM, D_MODEL)
    y1 = ybuf[1].reshape(TM, D_MODEL)
    return x_ref[...] + rg[:, 0:1] * y0 + rg[:, 1:2] * y1


def _combine_kernel(dest_ref, x_ref, rg_ref, ys_ref, o_ref, ybuf, sem):
    _gather_rows(dest_ref, ys_ref, ybuf, sem, pl.program_id(0))
    o_ref[...] = _combined(x_ref, rg_ref, ybuf)


def _combine(dest, x_all, rg_all, ys):
    return pl.pallas_call(
        _combine_kernel,
        grid_spec=pltpu.PrefetchScalarGridSpec(
            num_scalar_prefetch=1,
            grid=(N_ROW_BLOCKS,),
            in_specs=[pl.BlockSpec((TM, D_MODEL), lambda i, d: (i, 0)),
                      pl.BlockSpec((TM, LANES), lambda i, d: (i, 0)),
                      pl.BlockSpec(memory_space=pl.ANY)],
            out_specs=pl.BlockSpec((TM, D_MODEL), lambda i, d: (i, 0)),
            scratch_shapes=[pltpu.VMEM((TOP_K, TM, ROW_TILE, LANES), F32), pltpu.SemaphoreType.DMA(())],
        ),
        out_shape=jax.ShapeDtypeStruct((T_ALL, D_MODEL), F32),
        compiler_params=_cparams(("arbitrary",)),
        name="moe_combine",
    )(dest, x_all, rg_all, ys)


def _final_kernel(dest_ref, x_ref, rg_ref, ys_ref, nfin_ref, op_ref, os_ref, ybuf, sem):
    i = pl.program_id(0)
    _gather_rows(dest_ref, ys_ref, ybuf, sem, i)
    y = _rms(_combined(x_ref, rg_ref, ybuf), nfin_ref[...])

    @pl.when(i < N_PROMPT_BLOCKS)
    def _():
        op_ref[...] = y

    @pl.when(i >= N_PROMPT_BLOCKS)
    def _():
        os_ref[...] = y


def _final(dest, x_all, rg_all, ys, nfin):
    return pl.pallas_call(
        _final_kernel,
        grid_spec=pltpu.PrefetchScalarGridSpec(
            num_scalar_prefetch=1,
            grid=(N_ROW_BLOCKS,),
            in_specs=[pl.BlockSpec((TM, D_MODEL), lambda i, d: (i, 0)),
                      pl.BlockSpec((TM, LANES), lambda i, d: (i, 0)),
                      pl.BlockSpec(memory_space=pl.ANY),
                      pl.BlockSpec((1, D_MODEL), lambda i, d: (0, 0))],
            out_specs=[pl.BlockSpec((TM, D_MODEL), lambda i, d: (jnp.minimum(i, N_PROMPT_BLOCKS - 1), 0)),
                       pl.BlockSpec((TM, D_MODEL), lambda i, d: (0, 0))],
            scratch_shapes=[pltpu.VMEM((TOP_K, TM, ROW_TILE, LANES), F32), pltpu.SemaphoreType.DMA(())],
        ),
        out_shape=[jax.ShapeDtypeStruct((T_PROMPT, D_MODEL), F32), jax.ShapeDtypeStruct((T_SAMPLE, D_MODEL), F32)],
        compiler_params=_cparams(("arbitrary",)),
        name="moe_combine_final",
    )(dest, x_all, rg_all, ys, nfin)


def _moe(h_all, rt_all, cnt, w_gate, w_up, w_down, layer):
    dest, plan, zero_start = _moe_metadata(rt_all, cnt)
    xs = _dispatch(dest, zero_start, h_all)
    ys = _experts(*plan, xs, w_gate, w_up, w_down, layer)
    return dest, ys


def _pool_project(d_groups, wp_ref, scale):
    outs = [_dot(d_groups[g].astype(BF16), wp_ref[g]) for g in range(len(POOL_SIZES))]
    return jnp.concatenate(outs, axis=1) * scale


def _mix1_prompt_kernel(x_ref, nm_ref, wp_ref, sc_ref, nf_ref, wr_ref, br_ref,
                        x3_ref, h_ref, ri_ref, rg_ref, pl_ref, cnt_ref, ext):
    i = pl.program_id(0)

    @pl.when(i == 0)
    def _():
        cnt_ref[...] = jnp.zeros_like(cnt_ref)

    x = x_ref[...]
    hp = _rms(x, nm_ref[...])

    @pl.when(i % STEPS_PER_BATCH == 0)
    def _():
        ext[0:POOL_MAX, :] = jnp.zeros((POOL_MAX, D_MODEL), F32)

    ext[POOL_MAX:, :] = hp
    pos = (i % STEPS_PER_BATCH) * TM + lax.broadcasted_iota(jnp.int32, (TM, 1), 0)
    d_groups = []
    for g, w in enumerate(POOL_SIZES):
        cols = slice(g * POOL_GROUP_DIM, (g + 1) * POOL_GROUP_DIM)
        acc = ext[:, cols]
        span = 1
        while span < w:
            acc = acc + pltpu.roll(acc, span, 0)
            span *= 2
        cnt = jnp.minimum(pos + 1, w).astype(F32)
        d_groups.append(acc[POOL_MAX:] / cnt - hp[:, cols])
    tail = hp[TM - POOL_MAX:, :]
    ext[0:POOL_MAX, :] = tail
    pl_ref[...] = tail

    x3 = x + _pool_project(d_groups, wp_ref, sc_ref[...])
    x3_ref[...] = x3
    h, ids, gates = _route(x3, nf_ref[...], wr_ref[...], br_ref[...])
    h_ref[...] = h.reshape(h_ref.shape)
    ri_ref[...] = _rank_pack(ids, cnt_ref)
    rg_ref[...] = gates


def _mix1_prompt(x_all, nm, wp, sc, nf, wr, br):
    row_spec = pl.BlockSpec((TM, D_MODEL), lambda i: (i, 0))
    row3_spec = pl.BlockSpec((TM, ROW_TILE, LANES), lambda i: (i, 0, 0))
    lane_spec = pl.BlockSpec((TM, LANES), lambda i: (i, 0))
    return pl.pallas_call(
        _prompt_steps(_mix1_prompt_kernel, 7),
        grid=(N_ROW_BLOCKS,),
        in_specs=[row_spec, _const_spec((1, D_MODEL)),
                  _const_spec((len(POOL_SIZES), POOL_GROUP_DIM, POOL_GROUP_DIM)), _const_spec((1, D_MODEL)),
                  _const_spec((1, D_MODEL)), _const_spec((D_MODEL, 2 * LANES)), _const_spec((1, LANES))],
        out_specs=[row_spec, row3_spec, pl.BlockSpec((8, TM), lambda i: (0, i)), lane_spec,
                   pl.BlockSpec((None, POOL_MAX, D_MODEL),
                                lambda i: (jnp.minimum(i // STEPS_PER_BATCH, BATCH - 1), 0, 0)),
                   _const_spec((1, LANES))],
        out_shape=[jax.ShapeDtypeStruct((T_ALL, D_MODEL), F32), jax.ShapeDtypeStruct((T_ALL, ROW_TILE, LANES), BF16),
                   jax.ShapeDtypeStruct((8, T_ALL), jnp.int32), jax.ShapeDtypeStruct((T_ALL, LANES), F32),
                   jax.ShapeDtypeStruct((BATCH, POOL_MAX, D_MODEL), F32), jax.ShapeDtypeStruct((1, LANES), F32)],
        scratch_shapes=[pltpu.VMEM((POOL_MAX + TM, D_MODEL), F32)],
        compiler_params=_cparams(("arbitrary",)),
        name="mix1_prompt",
    )(x_all, nm, wp, sc, nf, wr, br)


def _mix1_sample_kernel(x_ref, st_ref, nm_ref, wp_ref, sc_ref, nf_ref, wr_ref, br_ref, cnt_in,
                        x3_in, h_in, ri_in, rg_in,
                        x3_ref, h_ref, ri_ref, rg_ref, hs_ref, cnt_ref):
    del x3_in, h_in, ri_in, rg_in
    x = x_ref[...]
    hs = _rms(x, nm_ref[...])
    hs_ref[...] = hs
    n_ctx = POOL_MAX - 1
    d_groups = []
    for g, w in enumerate(POOL_SIZES):
        cols = slice(g * POOL_GROUP_DIM, (g + 1) * POOL_GROUP_DIM)
        parts = []
        for t in range(DEC_SEQ):
            acc = hs[t * DEC_BATCH:(t + 1) * DEC_BATCH, cols]
            for back in range(1, w):
                src = t - back
                if src >= 0:
                    acc = acc + hs[src * DEC_BATCH:(src + 1) * DEC_BATCH, cols]
                else:
                    acc = acc + st_ref[n_ctx + src, :, cols]
            parts.append(acc / float(w) - hs[t * DEC_BATCH:(t + 1) * DEC_BATCH, cols])
        d_groups.append(jnp.concatenate(parts, axis=0))
    x3 = x + _pool_project(d_groups, wp_ref, sc_ref[...])
    x3_ref[...] = x3
    h, ids, gates = _route(x3, nf_ref[...], wr_ref[...], br_ref[...])
    h_ref[...] = h.reshape(h_ref.shape)
    cnt_ref[...] = cnt_in[...]
    ri_ref[...] = _rank_pack(ids, cnt_ref)
    rg_ref[...] = gates


def _mix1_sample(x_all, state_t, nm, wp, sc, nf, wr, br, cnt, x3_all, h_all, ri_all, rg_all):
    sample_rows = pl.BlockSpec((TM, D_MODEL), lambda g: (N_PROMPT_BLOCKS, 0))
    sample_rows3 = pl.BlockSpec((TM, ROW_TILE, LANES), lambda g: (N_PROMPT_BLOCKS, 0, 0))
    sample_lanes = pl.BlockSpec((TM, LANES), lambda g: (N_PROMPT_BLOCKS, 0))
    anyspec = pl.BlockSpec(memory_space=pl.ANY)
    n_in = 9
    return pl.pallas_call(
        _mix1_sample_kernel,
        grid=(1,),
        in_specs=[sample_rows, _const_spec((POOL_MAX - 1, DEC_BATCH, D_MODEL)), _const_spec((1, D_MODEL)),
                  _const_spec((len(POOL_SIZES), POOL_GROUP_DIM, POOL_GROUP_DIM)), _const_spec((1, D_MODEL)),
                  _const_spec((1, D_MODEL)), _const_spec((D_MODEL, 2 * LANES)), _const_spec((1, LANES)),
                  _const_spec((1, LANES)), anyspec, anyspec, anyspec, anyspec],
        out_specs=[sample_rows, sample_rows3, pl.BlockSpec((8, TM), lambda g: (0, N_PROMPT_BLOCKS)), sample_lanes,
                   _const_spec((T_SAMPLE, D_MODEL)), _const_spec((1, LANES))],
        out_shape=[jax.ShapeDtypeStruct((T_ALL, D_MODEL), F32), jax.ShapeDtypeStruct((T_ALL, ROW_TILE, LANES), BF16),
                   jax.ShapeDtypeStruct((8, T_ALL), jnp.int32), jax.ShapeDtypeStruct((T_ALL, LANES), F32),
                   jax.ShapeDtypeStruct((T_SAMPLE, D_MODEL), F32), jax.ShapeDtypeStruct((1, LANES), F32)],
        input_output_aliases={n_in: 0, n_in + 1: 1, n_in + 2: 2, n_in + 3: 3},
        compiler_params=_cparams(("arbitrary",)),
        name="mix1_sample",
    )(x_all, state_t, nm, wp, sc, nf, wr, br, cnt, x3_all, h_all, ri_all, rg_all)


def _router_weights(wg, bg, we, be):
    w = jnp.concatenate([wg, jnp.transpose(we, (1, 0, 2)).reshape(D_MODEL, N_EXPERTS)], axis=1)
    b = jnp.concatenate([bg, be.reshape(N_EXPERTS)])
    pad = LANES - N_GROUPS - N_EXPERTS
    w = jnp.pad(w, ((0, 0), (0, pad)))
    w_hi = w.astype(BF16)
    w_lo = (w - w_hi.astype(F32)).astype(BF16)
    return jnp.concatenate([w_hi, w_lo], axis=1), jnp.pad(b, (0, pad)).reshape(1, LANES)


def _stack(tab):
    return jnp.stack([jnp.concatenate([tab[h] for h in heads], axis=0) for heads in STACK_HEADS])


def kernel(x_prompt, x_sample, cache_k_win, cache_v_win, state_pool, norm_mix, norm_ffn, norm_final, w_in,
           a_ln_g, a_ln_b, a_w_s, a_b_s, b_sinks, rel_bias_table, w_out, c_w_pool, c_scale,
           router_group_w, router_group_b, router_expert_w, router_expert_b, w_gate, w_up, w_down):
    xs_t = jnp.transpose(x_sample, (1, 0, 2)).reshape(T_SAMPLE, D_MODEL)
    xp2 = x_prompt.reshape(T_PROMPT, D_MODEL)
    win =w_in[0].astype(BF16)
    wout = w_out[0].astype(BF16)
    lng = a_ln_g[0].reshape(1, A_WIDTH)
    lnb = a_ln_b[0].reshape(1, A_WIDTH)
    bias_p, bias_sc, bias_sn, ws_tril = _prep(rel_bias_table, a_w_s[0])
    wsp = ws_tril.reshape(A_HEADS // 2, 2, CHUNK, CHUNK).transpose(0, 2, 1, 3).reshape(A_HEADS // 2, CHUNK, 2 * CHUNK)
    bs_full = jnp.repeat(a_b_s[0].T, A_HEAD_DIM, axis=1)
    bias_p = jnp.stack([_stack(bias_p[0]), _stack(bias_p[1])])
    bias_sc = _stack(bias_sc)
    bias_sn = _stack(bias_sn)
    sinks = b_sinks[0]
    sink_p = jnp.stack([jnp.repeat(sinks[jnp.array(hh)], WINDOW) for hh in STACK_HEADS])
    sink_s = jnp.stack([jnp.repeat(sinks[jnp.array(hh)], 32) for hh in STACK_HEADS])
    bias_p = bias_p.at[:, :, :, 0].set(jnp.broadcast_to(sink_p[None], (2, 2, 4 * WINDOW)))
    bias_sc = bias_sc.at[:, :, 0].set(sink_s)
    pairs = [(t, s) for t in range(DEC_SEQ) for s in range(t + 1)]
    wcoef = jnp.stack([jnp.repeat(a_w_s[0][:, t, s], A_HEAD_DIM) for t, s in pairs])
    wcoef = jnp.pad(wcoef, ((0, 16 - len(pairs)), (0, 0)))
    bcoef = jnp.pad(jnp.repeat(a_b_s[0][:, :DEC_SEQ].T, A_HEAD_DIM, axis=1), ((0, 8 - DEC_SEQ), (0, 0)))
    ck = cache_k_win
    cv = cache_v_win
    routers = [_router_weights(router_group_w[l], router_group_b[l], router_expert_w[l], router_expert_b[l])
               for l in range(2)]
    nm = [norm_mix[l].reshape(1, D_MODEL) for l in range(2)]
    nf = [norm_ffn[l].reshape(1, D_MODEL) for l in range(2)]

    x1_all, h_all, ri_all, rg_all, k_last, v_last, va_last, cnt0 = _mix0_prompt(
        xp2, nm[0], win, lng, lnb, wsp, bs_full, bias_p, wout, nf[0], *routers[0])
    x1_all, h_all, ri_all, rg_all, k_new, v_new, va_s, cnt0 = _mix0_sample(
        xs_t, nm[0], win, lng, lnb, wcoef, bcoef, ck, cv, bias_sc, bias_sn, wout, nf[0], *routers[0], cnt0,
        x1_all, h_all, ri_all, rg_all)
    dest0, ys0 = _moe(h_all, ri_all, cnt0, w_gate, w_up, w_down, 0)
    x2_all = _combine(dest0, x1_all, rg_all, ys0)

    wp = c_w_pool[0].astype(BF16)
    sc = c_scale[0].reshape(1, D_MODEL)
    x3_all, h2_all, ri2_all, rg2_all, pool_tail, cnt1 = _mix1_prompt(x2_all, nm[1], wp, sc, nf[1], *routers[1])
    state_t = jnp.transpose(state_pool[0], (1, 0, 2))
    x3_all, h2_all, ri2_all, rg2_all, hs1, cnt1 = _mix1_sample(
        x2_all, state_t, nm[1], wp, sc, nf[1], *routers[1], cnt1, x3_all, h2_all, ri2_all, rg2_all)
    dest1, ys1 = _moe(h2_all, ri2_all, cnt1, w_gate, w_up, w_down, 1)
    y_p, y_s = _final(dest1, x3_all, rg2_all, ys1, norm_final.reshape(1, D_MODEL))

    def from_tmajor(a, width):
        return jnp.transpose(a.reshape(DEC_SEQ, DEC_BATCH, width), (1, 0, 2))

    y_prompt = y_p.reshape(BATCH, SEQ, D_MODEL)
    y_sample = from_tmajor(y_s, D_MODEL)
    win_k_p = k_last.reshape(1, BATCH, WINDOW, B_KV_HEADS, B_HEAD_DIM)
    win_v_p = v_last.reshape(1, BATCH, WINDOW, B_KV_HEADS, B_HEAD_DIM)
    kn = from_tmajor(k_new, KV_WIDTH).reshape(DEC_BATCH, DEC_SEQ, B_KV_HEADS, B_HEAD_DIM)
    vn = from_tmajor(v_new, KV_WIDTH).reshape(DEC_BATCH, DEC_SEQ, B_KV_HEADS, B_HEAD_DIM)
    win_k_s = jnp.concatenate([cache_k_win[0][:, DEC_SEQ:], kn], axis=1)[None]
    win_v_s = jnp.concatenate([cache_v_win[0][:, DEC_SEQ:], vn], axis=1)[None]
    chunk_v_p = va_last.reshape(1, BATCH, CHUNK, A_HEADS, A_HEAD_DIM)
    chunk_v_s = from_tmajor(va_s, A_WIDTH).reshape(1, DEC_BATCH, DEC_SEQ, A_HEADS, A_HEAD_DIM)
    pool_p = pool_tail[:, 1:][None]
    pool_s = jnp.concatenate([state_pool[0][:, DEC_SEQ:], from_tmajor(hs1, D_MODEL)], axis=1)[None]
    return (y_prompt, y_sample, win_k_p, win_v_p, win_k_s, win_v_s, chunk_v_p, chunk_v_s, pool_p, pool_s)
```

```python
import functools
import math

import numpy as np
import jax
import jax.numpy as jnp
from jax import lax
from jax.experimental import pallas as pl
from jax.experimental.pallas import tpu as pltpu

F32 = jnp.float32
BF16 = jnp.bfloat16

D_MODEL = 1024
BATCH = 2
SEQ = 8192
DEC_BATCH = 128
DEC_SEQ = 4
A_WIDTH = 512
A_HEADS = 8
A_HEAD_DIM = 64
CHUNK = 128
B_HEADS = 8
B_KV_HEADS = 2
B_HEAD_DIM = 64
B_GROUP = 4
WINDOW = 128
N_BUCKETS = 32
MAX_DISTANCE = WINDOW
Q_WIDTH = 512
KV_WIDTH = 128
IN_WIDTH = 2 * A_WIDTH + Q_WIDTH + 2 * KV_WIDTH
ATTN_SCALE = B_HEAD_DIM ** -0.5
NEG_INF = -1e30
POOL_SIZES = (2, 4, 8, 16)
POOL_GROUP_DIM = 256
POOL_MAX = 16
N_GROUPS = 4
EXPERTS_PER_GROUP = 8
N_EXPERTS = 32
TOP_K = 2
D_EXPERT = 512
EPS = 1e-6

LANES = 128
ROW_TILE = D_MODEL // LANES
T_PROMPT = BATCH * SEQ
T_SAMPLE = DEC_BATCH * DEC_SEQ
T_ALL = T_PROMPT + T_SAMPLE
TM = 512
N_PROMPT_BLOCKS = T_PROMPT // TM
N_ROW_BLOCKS = T_ALL // TM
STEPS_PER_BATCH = SEQ // TM
SUB = TM // WINDOW
N_SLOTS = T_ALL * TOP_K
MOE_BLK = 512
N_MOE_BLOCKS = N_SLOTS // MOE_BLK + N_EXPERTS
N_SORT_ROWS = N_MOE_BLOCKS * MOE_BLK
SAMPLE_GROUP = 8
N_SAMPLE_GROUPS = DEC_BATCH // SAMPLE_GROUP
VMEM_LIMIT = 56 * 1024 * 1024

STACK_HEADS = ((0, 2, 5, 7), (1, 3, 4, 6))


def _t5_bucket_np(dist):
    n = np.maximum(dist, 0)
    max_exact = N_BUCKETS // 2
    nf = np.maximum(n, 1).astype(np.float32)
    large = max_exact + (np.log(nf / np.float32(max_exact)) / np.float32(math.log(MAX_DISTANCE / max_exact))
                         * np.float32(N_BUCKETS - max_exact)).astype(np.int32)
    large = np.minimum(large, N_BUCKETS - 1)
    return np.where(n < max_exact, n, large).astype(np.int32)


def _bucket_tables():
    qi = np.arange(WINDOW)[:, None]
    ki = np.arange(2 * WINDOW)[None, :]
    dist = qi + WINDOW - ki
    valid = (dist >= 0) & (dist < WINDOW)
    bp = np.where(valid, _t5_bucket_np(dist), -1)
    bp_first = np.where(ki >= WINDOW, bp, -1)
    bkt_p = np.stack([bp_first, bp]).astype(np.int32)

    t = np.repeat(np.arange(DEC_SEQ), SAMPLE_GROUP)[:, None]
    b = np.tile(np.arange(SAMPLE_GROUP), DEC_SEQ)[:, None]
    cb = np.repeat(np.arange(SAMPLE_GROUP), WINDOW)[None, :]
    cj = np.tile(np.arange(WINDOW), SAMPLE_GROUP)[None, :]
    dist_c = t + WINDOW - cj
    valid_c = (cb == b) & (dist_c >= 0) & (dist_c < WINDOW)
    bkt_sc = np.where(valid_c, _t5_bucket_np(dist_c), -1).astype(np.int32)
    nt = np.repeat(np.arange(DEC_SEQ), SAMPLE_GROUP)[None, :]
    nb = np.tile(np.arange(SAMPLE_GROUP), DEC_SEQ)[None, :]
    dist_n = t - nt
    valid_n = (nb == b) & (dist_n >= 0)
    bkt_sn = np.where(valid_n, _t5_bucket_np(dist_n), -1).astype(np.int32)
    bkt_sn = np.concatenate([bkt_sn, np.full((32, LANES - 32), -1, np.int32)], axis=1)
    return bkt_p, bkt_sc, bkt_sn


_BKT_P, _BKT_SC, _BKT_SN = _bucket_tables()


def _cparams(semantics):
    return pltpu.CompilerParams(dimension_semantics=semantics, vmem_limit_bytes=VMEM_LIMIT)


def _rms(x, g):
    return x * lax.rsqrt(jnp.mean(x * x, axis=-1, keepdims=True) + EPS) * g


def _layernorm(x, g, b):
    xc = x - jnp.mean(x, axis=-1, keepdims=True)
    return xc * lax.rsqrt(jnp.mean(xc * xc, axis=-1, keepdims=True) + EPS) * g + b


def _dot(a, b):
    return jnp.dot(a, b, preferred_element_type=F32)


def _dot_nt(a, b):
    return lax.dot_general(a, b, (((1,), (1,)), ((), ())), preferred_element_type=F32)


def _project(x, nm, win, lng, lnb):
    h = _rms(x, nm)
    z = _dot(h.astype(BF16), win)
    u = jax.nn.gelu(z[:, :A_WIDTH])
    va = _layernorm(jax.nn.gelu(z[:, A_WIDTH:2 * A_WIDTH]), lng, lnb)
    q = z[:, 2 * A_WIDTH:2 * A_WIDTH + Q_WIDTH] * ATTN_SCALE
    k = z[:, 2 * A_WIDTH + Q_WIDTH:2 * A_WIDTH + Q_WIDTH + KV_WIDTH]
    v = z[:, 2 * A_WIDTH + Q_WIDTH + KV_WIDTH:]
    return u, va, q, k, v


def _route(x1, nf, wr, br):
    hf = _rms(x1, nf)
    h = hf.astype(BF16)
    h_lo = (hf - h.astype(F32)).astype(BF16)
    part = _dot(h, wr)
    logits = part[:, :LANES] + part[:, LANES:] + _dot(h_lo, wr[:, :LANES]) + br
    rows = logits.shape[0]
    lane = lax.broadcasted_iota(jnp.int32, (rows, LANES), 1)
    lanef = lane.astype(F32)
    big = jnp.float32(1e9)
    is_g = lane < N_GROUPS
    gl = jnp.where(is_g, logits, -jnp.inf)
    gmax = jnp.max(gl, axis=1, keepdims=True)
    gsel = jnp.min(jnp.where(gl == gmax, lanef, big), axis=1, keepdims=True)
    gsum = jnp.sum(jnp.where(is_g, jnp.exp(logits - gmax), 0.0), axis=1, keepdims=True)
    g1 = 1.0 / gsum
    lo = N_GROUPS + EXPERTS_PER_GROUP * gsel
    emask = (lanef >= lo) & (lanef < lo + EXPERTS_PER_GROUP)
    el = jnp.where(emask, logits, -jnp.inf)
    v1 = jnp.max(el, axis=1, keepdims=True)
    i1 = jnp.min(jnp.where(el == v1, lanef, big), axis=1, keepdims=True)
    el2 = jnp.where(lanef == i1, -jnp.inf, el)
    v2 = jnp.max(el2, axis=1, keepdims=True)
    i2 = jnp.min(jnp.where(el2 == v2, lanef, big), axis=1, keepdims=True)
    e2 = jnp.exp(v2 - v1)
    den = 1.0 + e2
    w1 = g1 / den
    w2 = g1 * e2 / den
    ids = jnp.where(lane == 0, i1 - N_GROUPS, jnp.where(lane == 1, i2 - N_GROUPS, 0.0)).astype(jnp.int32)
    gates = jnp.where(lane == 0, w1, jnp.where(lane == 1, w2, 0.0))
    return h, ids, gates


def _rank_pack(ids, cnt_ref):
    rows = ids.shape[0]
    lane = lax.broadcasted_iota(jnp.int32, (rows, LANES), 1)
    o0 = (lane == ids[:, 0:1]).astype(F32)
    o1 = (lane == ids[:, 1:2]).astype(F32)
    r = lax.broadcasted_iota(jnp.int32, (rows, rows), 0)
    c = lax.broadcasted_iota(jnp.int32, (rows, rows), 1)
    before = (c < r).astype(BF16)
    p01 = _dot(before, jnp.concatenate([o0, o1], axis=1).astype(BF16))
    p0 = p01[:, :LANES]
    p1 = p01[:, LANES:]
    c0 = jnp.sum(o0, axis=0, keepdims=True)
    c1 = jnp.sum(o1, axis=0, keepdims=True)
    carry = cnt_ref[...]
    rank0 = jnp.sum(o0 * (carry + p0), axis=1, keepdims=True)
    rank1 = jnp.sum(o1 * (carry + c0 + p1), axis=1, keepdims=True)
    cnt_ref[...] = carry + c0 + c1
    idf = ids.astype(F32)
    packed = jnp.where(lane < TOP_K, idf, jnp.where(lane == 2, rank0, jnp.where(lane == 3, rank1, 0.0)))
    return jnp.transpose(packed)[:8].astype(jnp.int32)


def _prep_kernel(tab_ref, bp_ref, bsc_ref, bsn_ref, ws_ref, op_ref, osc_ref, osn_ref, ows_ref):
    def fill(bkt, write):
        for h in range(B_HEADS):
            acc = jnp.full(bkt.shape, NEG_INF, F32)
            for b in range(N_BUCKETS):
                acc = jnp.where(bkt == b, tab_ref[b, h], acc)
            write(h, acc)

    for var in range(2):
        def wr_p(h, acc, var=var):
            op_ref[var, h] = acc
        fill(bp_ref[var], wr_p)

    def wr_sc(h, acc):
        osc_ref[h] = acc
    fill(bsc_ref[...], wr_sc)

    def wr_sn(h, acc):
        osn_ref[h] = acc
    fill(bsn_ref[...], wr_sn)

    r = lax.broadcasted_iota(jnp.int32, (CHUNK, CHUNK), 0)
    c = lax.broadcasted_iota(jnp.int32, (CHUNK, CHUNK), 1)
    for h in range(A_HEADS):
        ows_ref[h] = jnp.where(r >= c, ws_ref[h], 0.0).astype(BF16)


def _prep(rel_bias_table, w_s):
    vm = pl.BlockSpec(memory_space=pltpu.VMEM)
    return pl.pallas_call(
        _prep_kernel,
        in_specs=[pl.BlockSpec(memory_space=pltpu.SMEM), vm, vm, vm, vm],
        out_specs=[vm, vm, vm, vm],
        out_shape=[
            jax.ShapeDtypeStruct((2, B_HEADS, WINDOW, 2 * WINDOW), F32),
            jax.ShapeDtypeStruct((B_HEADS, 32, SAMPLE_GROUP * WINDOW), F32),
            jax.ShapeDtypeStruct((B_HEADS, 32, LANES), F32),
            jax.ShapeDtypeStruct((A_HEADS, CHUNK, CHUNK), BF16),
        ],
        name="prep_tables",
    )(rel_bias_table, jnp.asarray(_BKT_P), jnp.asarray(_BKT_SC), jnp.asarray(_BKT_SN), w_s)


def _gate_pairs(va_rows, wsp_ref, lane_lo):
    outs = []
    for p in range(A_HEADS // 2):
        vp = va_rows[:, p * LANES:(p + 1) * LANES]
        rhs = jnp.concatenate([jnp.where(lane_lo, vp, 0.0), jnp.where(lane_lo, 0.0, vp)], axis=0).astype(BF16)
        outs.append(_dot(wsp_ref[p], rhs))
    return jnp.concatenate(outs, axis=1)


def _prompt_steps(body, first_row_out):
    def kern(*refs):
        i = pl.program_id(0)

        @pl.when(i < N_PROMPT_BLOCKS)
        def _():
            body(*refs)

        @pl.when(i >= N_PROMPT_BLOCKS)
        def _():
            for r in refs[first_row_out:first_row_out + 4]:
                r[...] = jnp.zeros(r.shape, r.dtype)

    return kern


def _mix0_prompt_kernel(x_ref, nm_ref, win_ref, lng_ref, lnb_ref, wsp_ref, bs_ref, bias_ref,
                        wout_ref, nf_ref, wr_ref, br_ref,
                        x1_ref, h_ref, ri_ref, rg_ref, kl_ref, vl_ref, val_ref, cnt_ref,
                        kprev, vprev, mix_scr):
    @pl.when(pl.program_id(0) == 0)
    def _():
        cnt_ref[...] = jnp.zeros_like(cnt_ref)

    x = x_ref[...]
    u, va, q, k, v = _project(x, nm_ref[...], win_ref[...], lng_ref[...], lnb_ref[...])
    lane_lo = lax.broadcasted_iota(jnp.int32, (WINDOW, LANES), 1) < B_HEAD_DIM
    row0 = lax.broadcasted_iota(jnp.int32, (WINDOW, KV_WIDTH), 0) == 0
    first = pl.program_id(0) % STEPS_PER_BATCH == 0

    @pl.when(first)
    def _():
        kprev[...] = jnp.zeros_like(kprev)
        vprev[...] = jnp.zeros_like(vprev)

    for j in range(SUB):
        rows = slice(j * WINDOW, (j + 1) * WINDOW)
        s_gate = _gate_pairs(va[rows], wsp_ref, lane_lo)
        mix_scr[rows, :A_WIDTH] = u[rows] * (s_gate + bs_ref[...])

        if j == 0:
            kp, vp = kprev[...], vprev[...]
        else:
            prows = slice((j - 1) * WINDOW, j * WINDOW)
            kp, vp = k[prows], v[prows]
        kk = jnp.concatenate([jnp.where(row0, 0.0, kp), k[rows]], axis=0)
        vv = jnp.concatenate([jnp.where(row0, 0.0, vp), v[rows]], axis=0)
        kops = (kk.astype(BF16), pltpu.roll(kk, B_HEAD_DIM, 1).astype(BF16))
        vops = (vv.astype(BF16), pltpu.roll(vv, B_HEAD_DIM, 1).astype(BF16))
        qt = [q[rows, p * LANES:(p + 1) * LANES] for p in range(4)]
        q_even = [jnp.where(lane_lo, t, 0.0) for t in qt]
        q_odd = [jnp.where(lane_lo, 0.0, t) for t in qt]
        stacks = (jnp.concatenate([q_even[0], q_even[1], q_odd[2], q_odd[3]], axis=0),
                  jnp.concatenate([q_odd[0], q_odd[1], q_even[2], q_even[3]], axis=0))
        o = []
        for st in range(2):
            s = _dot_nt(stacks[st].astype(BF16), kops[st])
            if j == 0:
                bias = bias_ref[jnp.where(first, 0, 1), st]
            else:
                bias = bias_ref[1, st]
            s = s + bias
            m = jnp.max(s, axis=-1, keepdims=True)
            p = jnp.exp(s - m)
            den = jnp.sum(p, axis=-1, keepdims=True)
            o.append(_dot(p.astype(BF16), vops[st]) / den)
        oa, ob = o
        sl = [slice(i * WINDOW, (i + 1) * WINDOW) for i in range(4)]
        tiles = (jnp.where(lane_lo, oa[sl[0]], ob[sl[0]]), jnp.where(lane_lo, oa[sl[1]], ob[sl[1]]),
                 jnp.where(lane_lo, ob[sl[2]], oa[sl[2]]), jnp.where(lane_lo, ob[sl[3]], oa[sl[3]]))
        for p in range(4):
            mix_scr[rows, A_WIDTH + p * LANES:A_WIDTH + (p + 1) * LANES] = tiles[p]

    last = slice(TM - WINDOW, TM)
    kprev[...] = k[last]
    vprev[...] = v[last]
    kl_ref[...] = k[last]
    vl_ref[...] = v[last]
    val_ref[...] = va[last]

    x1 = x + _dot(mix_scr[...].astype(BF16), wout_ref[...])
    x1_ref[...] = x1
    h, ids, gates = _route(x1, nf_ref[...], wr_ref[...], br_ref[...])
    h_ref[...] = h.reshape(h_ref.shape)
    ri_ref[...] = _rank_pack(ids, cnt_ref)
    rg_ref[...] = gates


def _const_spec(shape):
    nd = len(shape)
    return pl.BlockSpec(shape, lambda i, _n=nd: (0,) * _n)


def _mix0_prompt(x_all, nm, win, lng, lnb, wsp, bs_full, bias_p, wout, nf, wr, br):
    row_spec = pl.BlockSpec((TM, D_MODEL), lambda i: (i, 0))
    row3_spec = pl.BlockSpec((TM, ROW_TILE, LANES), lambda i: (i, 0, 0))
    lane_spec = pl.BlockSpec((TM, LANES), lambda i: (i, 0))
    last_kv = pl.BlockSpec((None, WINDOW, KV_WIDTH), lambda i: (jnp.minimum(i // STEPS_PER_BATCH, BATCH - 1), 0, 0))
    last_va = pl.BlockSpec((None, WINDOW, A_WIDTH), lambda i: (jnp.minimum(i // STEPS_PER_BATCH, BATCH - 1), 0, 0))
    return pl.pallas_call(
        _prompt_steps(_mix0_prompt_kernel, 12),
        grid=(N_ROW_BLOCKS,),
        in_specs=[pl.BlockSpec((TM, D_MODEL), lambda i: (jnp.minimum(i, N_PROMPT_BLOCKS - 1), 0)),
                  _const_spec((1, D_MODEL)), _const_spec((D_MODEL, IN_WIDTH)),
                  _const_spec((1, A_WIDTH)), _const_spec((1, A_WIDTH)),
                  _const_spec((A_HEADS // 2, CHUNK, 2 * CHUNK)), _const_spec((CHUNK, A_WIDTH)),
                  _const_spec((2, 2, 4 * WINDOW, 2 * WINDOW)),
                  _const_spec((A_WIDTH + Q_WIDTH, D_MODEL)), _const_spec((1, D_MODEL)),
                  _const_spec((D_MODEL, 2 * LANES)), _const_spec((1, LANES))],
        out_specs=[row_spec, row3_spec, pl.BlockSpec((8, TM), lambda i: (0, i)), lane_spec,
                   last_kv, last_kv, last_va, _const_spec((1, LANES))],
        out_shape=[jax.ShapeDtypeStruct((T_ALL, D_MODEL), F32), jax.ShapeDtypeStruct((T_ALL, ROW_TILE, LANES), BF16),
                   jax.ShapeDtypeStruct((8, T_ALL), jnp.int32), jax.ShapeDtypeStruct((T_ALL, LANES), F32),
                   jax.ShapeDtypeStruct((BATCH, WINDOW, KV_WIDTH), F32),
                   jax.ShapeDtypeStruct((BATCH, WINDOW, KV_WIDTH), F32),
                   jax.ShapeDtypeStruct((BATCH, WINDOW, A_WIDTH), F32),
                   jax.ShapeDtypeStruct((1, LANES), F32)],
        scratch_shapes=[pltpu.VMEM((WINDOW, KV_WIDTH), F32), pltpu.VMEM((WINDOW, KV_WIDTH), F32),
                        pltpu.VMEM((TM, D_MODEL), F32)],
        compiler_params=_cparams(("arbitrary",)),
        name="mix0_prompt",
    )(x_all, nm, win, lng, lnb, wsp, bs_full, bias_p, wout, nf, wr, br)


def _mix0_sample_kernel(x_ref, nm_ref, win_ref, lng_ref, lnb_ref, wcoef_ref, bcoef_ref,
                        ck_ref, cv_ref, bsc_ref, bsn_ref,
                        wout_ref, nf_ref, wr_ref, br_ref, cnt_in,
                        x1_in, h_in, ri_in, rg_in,
                        x1_ref, h_ref, ri_ref, rg_ref, kn_ref, vn_ref, va_ref, cnt_ref,
                        q_scr, k_scr, v_scr, mix_scr):
    del x1_in, h_in, ri_in, rg_in
    g = pl.program_id(0)

    @pl.when(g == 0)
    def _():
        u, va, q, k, v = _project(x_ref[...], nm_ref[...], win_ref[...], lng_ref[...], lnb_ref[...])
        q_scr[...] = q
        k_scr[...] = k
        v_scr[...] = v
        kn_ref[...] = k
        vn_ref[...] = v
        va_ref[...] = va
        idx = 0
        for t in range(DEC_SEQ):
            acc = jnp.zeros((DEC_BATCH, A_WIDTH), F32) + bcoef_ref[t:t + 1, :]
            for s in range(t + 1):
                acc = acc + wcoef_ref[idx:idx + 1, :] * va[s * DEC_BATCH:(s + 1) * DEC_BATCH]
                idx += 1
            mix_scr[t * DEC_BATCH:(t + 1) * DEC_BATCH, :A_WIDTH] = u[t * DEC_BATCH:(t + 1) * DEC_BATCH] * acc

    b0 = pl.multiple_of(g * SAMPLE_GROUP, SAMPLE_GROUP)
    lane_lo = lax.broadcasted_iota(jnp.int32, (DEC_SEQ * SAMPLE_GROUP, LANES), 1) < B_HEAD_DIM

    def grab(ref, width):
        return jnp.concatenate([ref[pl.ds(t * DEC_BATCH + b0, SAMPLE_GROUP), :] for t in range(DEC_SEQ)], axis=0)

    qg = grab(q_scr, Q_WIDTH)
    kn = grab(k_scr, KV_WIDTH)
    vn = grab(v_scr, KV_WIDTH)
    crow0 = lax.broadcasted_iota(jnp.int32, (SAMPLE_GROUP * WINDOW, KV_WIDTH), 0) == 0
    rows_kv = (SAMPLE_GROUP * WINDOW, KV_WIDTH)
    kc = jnp.where(crow0, 0.0, ck_ref[...].reshape(rows_kv))
    vc = jnp.where(crow0, 0.0, cv_ref[...].reshape(rows_kv))
    kc_ops = (kc.astype(BF16), pltpu.roll(kc, B_HEAD_DIM, 1).astype(BF16))
    vc_ops = (vc.astype(BF16), pltpu.roll(vc, B_HEAD_DIM, 1).astype(BF16))
    kn_ops = (kn.astype(BF16), pltpu.roll(kn, B_HEAD_DIM, 1).astype(BF16))
    vn_ops = (vn.astype(BF16), pltpu.roll(vn, B_HEAD_DIM, 1).astype(BF16))
    qt = [qg[:, p * LANES:(p + 1) * LANES] for p in range(4)]
    q_even = [jnp.where(lane_lo, t, 0.0) for t in qt]
    q_odd = [jnp.where(lane_lo, 0.0, t) for t in qt]
    stacks = (jnp.concatenate([q_even[0], q_even[1], q_odd[2], q_odd[3]], axis=0),
              jnp.concatenate([q_odd[0], q_odd[1], q_even[2], q_even[3]], axis=0))
    o = []
    for st in range(2):
        qs = stacks[st].astype(BF16)
        sc = _dot_nt(qs, kc_ops[st]) + bsc_ref[st]
        sn = _dot_nt(qs, kn_ops[st]) + bsn_ref[st][:, :DEC_SEQ * SAMPLE_GROUP]
        m = jnp.maximum(jnp.max(sc, axis=-1, keepdims=True), jnp.max(sn, axis=-1, keepdims=True))
        pc = jnp.exp(sc - m)
        pn = jnp.exp(sn - m)
        den = jnp.sum(pc, axis=-1, keepdims=True) + jnp.sum(pn, axis=-1, keepdims=True)
        o.append((_dot(pc.astype(BF16), vc_ops[st]) + _dot(pn.astype(BF16), vn_ops[st])) / den)
    oa, ob = o
    n = DEC_SEQ * SAMPLE_GROUP
    sl = [slice(i * n, (i + 1) * n) for i in range(4)]
    tiles = (jnp.where(lane_lo, oa[sl[0]], ob[sl[0]]), jnp.where(lane_lo, oa[sl[1]], ob[sl[1]]),
             jnp.where(lane_lo, ob[sl[2]], oa[sl[2]]), jnp.where(lane_lo, ob[sl[3]], oa[sl[3]]))
    for p in range(4):
        for t in range(DEC_SEQ):
            mix_scr[pl.ds(t * DEC_BATCH + b0, SAMPLE_GROUP), A_WIDTH + p * LANES:A_WIDTH + (p + 1) * LANES] = (
                tiles[p][t * SAMPLE_GROUP:(t + 1) * SAMPLE_GROUP])

    @pl.when(g == N_SAMPLE_GROUPS - 1)
    def _():
        x1 = x_ref[...] + _dot(mix_scr[...].astype(BF16), wout_ref[...])
        x1_ref[...] = x1
        h, ids, gates = _route(x1, nf_ref[...], wr_ref[...], br_ref[...])
        h_ref[...] = h.reshape(h_ref.shape)
        cnt_ref[...] = cnt_in[...]
        ri_ref[...] = _rank_pack(ids, cnt_ref)
        rg_ref[...] = gates


def _mix0_sample(x_all, nm, win, lng, lnb, wcoef, bcoef, ck, cv, bias_sc, bias_sn, wout, nf, wr, br, cnt,
                 x1_all, h_all, ri_all, rg_all):
    sample_rows = pl.BlockSpec((TM, D_MODEL), lambda g: (N_PROMPT_BLOCKS, 0))
    sample_rows3 = pl.BlockSpec((TM, ROW_TILE, LANES), lambda g: (N_PROMPT_BLOCKS, 0, 0))
    sample_lanes = pl.BlockSpec((TM, LANES), lambda g: (N_PROMPT_BLOCKS, 0))
    cache_spec = pl.BlockSpec((None, SAMPLE_GROUP, WINDOW, B_KV_HEADS, B_HEAD_DIM), lambda g: (0, g, 0, 0, 0))
    anyspec = pl.BlockSpec(memory_space=pl.ANY)
    n_in = 16
    return pl.pallas_call(
        _mix0_sample_kernel,
        grid=(N_SAMPLE_GROUPS,),
        in_specs=[_const_spec((TM, D_MODEL)), _const_spec((1, D_MODEL)), _const_spec((D_MODEL, IN_WIDTH)),
                  _const_spec((1, A_WIDTH)), _const_spec((1, A_WIDTH)),
                  _const_spec((16, A_WIDTH)), _const_spec((8, A_WIDTH)),
                  cache_spec, cache_spec,
                  _const_spec((2, 4 * 32, SAMPLE_GROUP * WINDOW)), _const_spec((2, 4 * 32, LANES)),
                  _const_spec((A_WIDTH + Q_WIDTH, D_MODEL)), _const_spec((1, D_MODEL)),
                  _const_spec((D_MODEL, 2 * LANES)), _const_spec((1, LANES)), _const_spec((1, LANES)),
                  anyspec, anyspec, anyspec, anyspec],
        out_specs=[sample_rows, sample_rows3, pl.BlockSpec((8, TM), lambda g: (0, N_PROMPT_BLOCKS)), sample_lanes,
                   _const_spec((T_SAMPLE, KV_WIDTH)), _const_spec((T_SAMPLE, KV_WIDTH)),
                   _const_spec((T_SAMPLE, A_WIDTH)), _const_spec((1, LANES))],
        out_shape=[jax.ShapeDtypeStruct((T_ALL, D_MODEL), F32), jax.ShapeDtypeStruct((T_ALL, ROW_TILE, LANES), BF16),
                   jax.ShapeDtypeStruct((8, T_ALL), jnp.int32), jax.ShapeDtypeStruct((T_ALL, LANES), F32),
                   jax.ShapeDtypeStruct((T_SAMPLE, KV_WIDTH), F32), jax.ShapeDtypeStruct((T_SAMPLE, KV_WIDTH), F32),
                   jax.ShapeDtypeStruct((T_SAMPLE, A_WIDTH), F32), jax.ShapeDtypeStruct((1, LANES), F32)],
        scratch_shapes=[pltpu.VMEM((T_SAMPLE, Q_WIDTH), F32), pltpu.VMEM((T_SAMPLE, KV_WIDTH), F32),
                        pltpu.VMEM((T_SAMPLE, KV_WIDTH), F32), pltpu.VMEM((T_SAMPLE, D_MODEL), F32)],
        input_output_aliases={n_in: 0, n_in + 1: 1, n_in + 2: 2, n_in + 3: 3},
        compiler_params=_cparams(("arbitrary",)),
        name="mix0_sample",
    )(x_all, nm, win, lng, lnb, wcoef, bcoef, ck, cv, bias_sc, bias_sn, wout, nf, wr, br, cnt,
      x1_all, h_all, ri_all, rg_all)


def _moe_metadata(rt_all, cnt):
    counts = cnt[0, :N_EXPERTS].astype(jnp.int32)
    padded = (counts + MOE_BLK - 1) // MOE_BLK * MOE_BLK
    pad_end = jnp.cumsum(padded)
    pad_start = pad_end - padded
    experts = jnp.arange(N_EXPERTS, dtype=jnp.int32)
    eid = rt_all[0:TOP_K]
    base = jnp.sum(jnp.where(eid[:, :, None] == experts[None, None, :], pad_start[None, None, :], 0), axis=-1)
    dest = (base + rt_all[TOP_K:2 * TOP_K]).reshape(N_SLOTS).astype(jnp.int32)
    n_valid = (pad_end[-1] // MOE_BLK).astype(jnp.int32).reshape(1)
    blk_start = jnp.arange(N_MOE_BLOCKS, dtype=jnp.int32) * MOE_BLK
    block_e = jnp.minimum(jnp.sum((blk_start[:, None] >= pad_end[None, :]).astype(jnp.int32), axis=1),
                          N_EXPERTS - 1).astype(jnp.int32)
    zero_start = (pad_start + counts).astype(jnp.int32)
    zero_len = (padded - counts).astype(jnp.int32)
    first = (blk_start == pad_start[block_e]).astype(jnp.int32)
    used = counts > 0
    parity = ((jnp.cumsum(used.astype(jnp.int32)) - 1) % 2)[block_e].astype(jnp.int32)
    nearest = lax.cummin(jnp.where(used, experts, N_EXPERTS)[::-1])[::-1]
    next_used = jnp.concatenate([nearest[1:], jnp.full((1,), N_EXPERTS, jnp.int32)])
    nxt = jnp.where(next_used < N_EXPERTS, next_used, -1)[block_e].astype(jnp.int32)
    plan = (block_e, first, parity, nxt, n_valid)
    return dest, plan, jnp.concatenate([zero_start, zero_len, n_valid])


def _dispatch_kernel(dest_ref, zs_ref, h_ref, xs_ref, zero_scr, sem, zsem):
    i = pl.program_id(0)

    @pl.when(i == 0)
    def _():
        zero_scr[...] = jnp.zeros_like(zero_scr)

        def pieces(e, do):
            off = zs_ref[e]
            rem = zs_ref[N_EXPERTS + e]
            bit = MOE_BLK // 2
            while bit >= 1:
                take = (rem & bit) != 0

                @pl.when(take)
                def _(off=off, bit=bit):
                    do(pltpu.make_async_copy(zero_scr.at[pl.ds(0, bit)], xs_ref.at[pl.ds(off, bit)], zsem))

                off = off + jnp.where(take, bit, 0)
                bit //= 2

        def start_e(e, c):
            pieces(e, lambda cp: cp.start())
            return c

        def wait_e(e, c):
            pieces(e, lambda cp: cp.wait())
            return c

        def tail(do):
            def step(b, c):
                do(pltpu.make_async_copy(zero_scr, xs_ref.at[pl.ds(b * MOE_BLK, MOE_BLK)], zsem))
                return c
            return step

        n_valid = zs_ref[2 * N_EXPERTS]
        lax.fori_loop(0, N_EXPERTS, start_e, 0)
        lax.fori_loop(n_valid, N_MOE_BLOCKS, tail(lambda cp: cp.start()), 0)
        lax.fori_loop(0, N_EXPERTS, wait_e, 0)
        lax.fori_loop(n_valid, N_MOE_BLOCKS, tail(lambda cp: cp.wait()), 0)

    base = i * TM

    def body(r, carry):
        for kk in range(TOP_K):
            d = dest_ref[kk * T_ALL + base + r]
            pltpu.make_async_copy(h_ref.at[r], xs_ref.at[d], sem).start(priority=kk)
        return carry

    lax.fori_loop(0, TM, body, 0)
    for kk in range(TOP_K):
        pltpu.make_async_copy(h_ref, xs_ref.at[pl.ds(0, TM)], sem).wait()


def _dispatch(dest, zero_start, h_all):
    return pl.pallas_call(
        _dispatch_kernel,
        grid_spec=pltpu.PrefetchScalarGridSpec(
            num_scalar_prefetch=2,
            grid=(N_ROW_BLOCKS,),
            in_specs=[pl.BlockSpec((TM, ROW_TILE, LANES), lambda i, d, z: (i, 0, 0))],
            out_specs=pl.BlockSpec(memory_space=pl.ANY),
            scratch_shapes=[pltpu.VMEM((MOE_BLK, ROW_TILE, LANES), BF16), pltpu.SemaphoreType.DMA(()),
                            pltpu.SemaphoreType.DMA(())],
        ),
        out_shape=jax.ShapeDtypeStruct((N_SORT_ROWS, ROW_TILE, LANES), BF16),
        compiler_params=_cparams(("arbitrary",)),
        name="moe_dispatch",
    )(dest, zero_start, h_all)


def _experts_kernel(layer, be_ref, first_ref, par_ref, nxt_ref, nv_ref,
                    x_ref, wg_hbm, wu_hbm, wd_hbm, y_ref,
                    wg_s, wu_s, wd_s, wg_f, wu_f, wd_f, wsem):
    i = pl.program_id(0)

    def fetch(e, slot):
        return (pltpu.make_async_copy(wg_hbm.at[layer, e], wg_f.at[slot], wsem.at[slot]),
                pltpu.make_async_copy(wu_hbm.at[layer, e], wu_f.at[slot], wsem.at[slot]),
                pltpu.make_async_copy(wd_hbm.at[layer, e], wd_f.at[slot], wsem.at[slot]))

    @pl.when(i < nv_ref[0])
    def _():
        e = be_ref[i]
        slot = par_ref[i]

        @pl.when(i == 0)
        def _():
            for cp in fetch(e, slot):
                cp.start()

        @pl.when(first_ref[i] == 1)
        def _():
            for cp in fetch(e, slot):
                cp.wait()
            wg_s[...] = wg_f[slot].astype(BF16)
            wu_s[...] = wu_f[slot].astype(BF16)
            wd_s[...] = wd_f[slot].astype(BF16)
            nxt = nxt_ref[i]

            @pl.when(nxt >= 0)
            def _():
                for cp in fetch(nxt, 1 - slot):
                    cp.start()

        xb = x_ref[...].reshape(MOE_BLK, D_MODEL)
        a = jax.nn.silu(_dot(xb, wg_s[...])) * _dot(xb, wu_s[...])
        y_ref[...] = _dot(a.astype(BF16), wd_s[...]).reshape(y_ref.shape)

    @pl.when(i >= nv_ref[0])
    def _():
        y_ref[...] = jnp.zeros(y_ref.shape, y_ref.dtype)


def _experts(block_e, first, parity, nxt, n_valid, xs, w_gate, w_up, w_down, layer):
    def blk(i, be, fi, pa, nx, nv):
        return (jnp.maximum(jnp.minimum(i, nv[0] - 1), 0), 0, 0)

    anyspec = pl.BlockSpec(memory_space=pl.ANY)
    return pl.pallas_call(
        functools.partial(_experts_kernel, layer),
        grid_spec=pltpu.PrefetchScalarGridSpec(
            num_scalar_prefetch=5,
            grid=(N_MOE_BLOCKS,),
            in_specs=[pl.BlockSpec((MOE_BLK, ROW_TILE, LANES), blk), anyspec, anyspec, anyspec],
            out_specs=pl.BlockSpec((MOE_BLK, ROW_TILE, LANES), lambda i, be, fi, pa, nx, nv: (i, 0, 0)),
            scratch_shapes=[pltpu.VMEM((D_MODEL, D_EXPERT), BF16), pltpu.VMEM((D_MODEL, D_EXPERT), BF16),
                            pltpu.VMEM((D_EXPERT, D_MODEL), BF16),
                            pltpu.VMEM((2, D_MODEL, D_EXPERT), F32), pltpu.VMEM((2, D_MODEL, D_EXPERT), F32),
                            pltpu.VMEM((2, D_EXPERT, D_MODEL), F32), pltpu.SemaphoreType.DMA((2,))],
        ),
        out_shape=jax.ShapeDtypeStruct((N_SORT_ROWS, ROW_TILE, LANES), F32),
        compiler_params=_cparams(("arbitrary",)),
        name="moe_experts",
    )(block_e, first, parity, nxt, n_valid, xs, w_gate, w_up, w_down)


def _gather_rows(dest_ref, ys_ref, ybuf, sem, i):
    base = i * TM

    def body(r, carry):
        for kk in range(TOP_K):
            d = dest_ref[kk * T_ALL + base + r]
            pltpu.make_async_copy(ys_ref.at[d], ybuf.at[kk, r], sem).start(priority=kk)
        return carry

    lax.fori_loop(0, TM, body, 0)
    for kk in range(TOP_K):
        pltpu.make_async_copy(ys_ref.at[pl.ds(0, TM)], ybuf.at[kk], sem).wait()


def _combined(x_ref, rg_ref, ybuf):
    rg = rg_ref[...]
    y0 = ybuf[0].reshape(TM, D_MODEL)
    y1 = ybuf[1].reshape(TM, D_MODEL)
    return x_ref[...] + rg[:, 0:1] * y0 + rg[:, 1:2] * y1


def _combine_sample_kernel(dest_ref, x_ref, rg_ref, ys_ref, o_ref, ybuf, sem):
    _gather_rows(dest_ref, ys_ref, ybuf, sem, N_PROMPT_BLOCKS)
    o_ref[...] = _combined(x_ref, rg_ref, ybuf)


def _combine_sample(dest, x_all, rg_all, ys):
    return pl.pallas_call(
        _combine_sample_kernel,
        grid_spec=pltpu.PrefetchScalarGridSpec(
            num_scalar_prefetch=1,
            grid=(1,),
            in_specs=[pl.BlockSpec((TM, D_MODEL), lambda i, d: (N_PROMPT_BLOCKS, 0)),
                      pl.BlockSpec((TM, LANES), lambda i, d: (N_PROMPT_BLOCKS, 0)),
                      pl.BlockSpec(memory_space=pl.ANY)],
            out_specs=pl.BlockSpec((TM, D_MODEL), lambda i, d: (0, 0)),
            scratch_shapes=[pltpu.VMEM((TOP_K, TM, ROW_TILE, LANES), F32), pltpu.SemaphoreType.DMA(())],
        ),
        out_shape=jax.ShapeDtypeStruct((T_SAMPLE, D_MODEL), F32),
        compiler_params=_cparams(("arbitrary",)),
        name="moe_combine_sample",
    )(dest, x_all, rg_all, ys)


def _final_kernel(dest_ref, x_ref, rg_ref, ys_ref, nfin_ref, op_ref, os_ref, ybuf, sem):
    i = pl.program_id(0)
    _gather_rows(dest_ref, ys_ref, ybuf, sem, i)
    y = _rms(_combined(x_ref, rg_ref, ybuf), nfin_ref[...])

    @pl.when(i < N_PROMPT_BLOCKS)
    def _():
        op_ref[...] = y

    @pl.when(i >= N_PROMPT_BLOCKS)
    def _():
        os_ref[...] = y


def _final(dest, x_all, rg_all, ys, nfin):
    return pl.pallas_call(
        _final_kernel,
        grid_spec=pltpu.PrefetchScalarGridSpec(
            num_scalar_prefetch=1,
            grid=(N_ROW_BLOCKS,),
            in_specs=[pl.BlockSpec((TM, D_MODEL), lambda i, d: (i, 0)),
                      pl.BlockSpec((TM, LANES), lambda i, d: (i, 0)),
                      pl.BlockSpec(memory_space=pl.ANY),
                      pl.BlockSpec((1, D_MODEL), lambda i, d: (0, 0))],
            out_specs=[pl.BlockSpec((TM, D_MODEL), lambda i, d: (jnp.minimum(i, N_PROMPT_BLOCKS - 1), 0)),
                       pl.BlockSpec((TM, D_MODEL), lambda i, d: (0, 0))],
            scratch_shapes=[pltpu.VMEM((TOP_K, TM, ROW_TILE, LANES), F32), pltpu.SemaphoreType.DMA(())],
        ),
        out_shape=[jax.ShapeDtypeStruct((T_PROMPT, D_MODEL), F32), jax.ShapeDtypeStruct((T_SAMPLE, D_MODEL), F32)],
        compiler_params=_cparams(("arbitrary",)),
        name="moe_combine_final",
    )(dest, x_all, rg_all, ys, nfin)


def _moe(h_all, rt_all, cnt, w_gate, w_up, w_down, layer):
    dest, plan, zero_start = _moe_metadata(rt_all, cnt)
    xs = _dispatch(dest, zero_start, h_all)
    ys = _experts(*plan, xs, w_gate, w_up, w_down, layer)
    return dest, ys


def _pool_project(d_groups, wp_ref, scale):
    outs = [_dot(d_groups[g].astype(BF16), wp_ref[g]) for g in range(len(POOL_SIZES))]
    return jnp.concatenate(outs, axis=1) * scale


GATHER_CHUNKS = 8


def _mix1_prompt_kernel(dest_ref, x_ref, rgin_ref, ys_ref, nm_ref, wp_ref, sc_ref, nf_ref, wr_ref, br_ref,
                        x3_ref, h_ref, ri_ref, rg_ref, pl_ref, cnt_ref, ext, ybuf, sem):
    i = pl.program_id(0)
    slot = i % 2
    rows_per_chunk = TM // GATHER_CHUNKS

    def issue(tile, buf, chunk):
        n = jnp.where(tile < N_PROMPT_BLOCKS, rows_per_chunk, 0)
        row0 = chunk * rows_per_chunk
        base = tile * TM + row0

        def body(r, c):
            for kk in range(TOP_K):
                d = dest_ref[kk * T_ALL + base + r]
                pltpu.make_async_copy(ys_ref.at[d], ybuf.at[buf, kk, row0 + r], sem.at[buf]).start(priority=kk)
            return c

        lax.fori_loop(0, n, body, 0)

    @pl.when(i == 0)
    def _():
        cnt_ref[...] = jnp.zeros_like(cnt_ref)
        for c in range(GATHER_CHUNKS):
            issue(i, slot, c)

    for kk in range(TOP_K):
        pltpu.make_async_copy(ys_ref.at[pl.ds(0, TM)], ybuf.at[slot, kk], sem.at[slot]).wait()
    rg_in = rgin_ref[...]
    x = (x_ref[...] + rg_in[:, 0:1] * ybuf[slot, 0].reshape(TM, D_MODEL)
         + rg_in[:, 1:2] * ybuf[slot, 1].reshape(TM, D_MODEL))
    issue(i + 1, 1 - slot, 0)
    hp = _rms(x, nm_ref[...])

    @pl.when(i % STEPS_PER_BATCH == 0)
    def _():
        ext[0:POOL_MAX, :] = jnp.zeros((POOL_MAX, D_MODEL), F32)

    ext[POOL_MAX:, :] = hp
    issue(i + 1, 1 - slot, 1)
    pos = (i % STEPS_PER_BATCH) * TM + lax.broadcasted_iota(jnp.int32, (TM, 1), 0)
    d_groups = []
    for g, w in enumerate(POOL_SIZES):
        cols = slice(g * POOL_GROUP_DIM, (g + 1) * POOL_GROUP_DIM)
        acc = ext[:, cols]
        span = 1
        while span < w:
            acc = acc + pltpu.roll(acc, span, 0)
            span *= 2
        cnt = jnp.minimum(pos + 1, w).astype(F32)
        d_groups.append(acc[POOL_MAX:] / cnt - hp[:, cols])
        issue(i + 1, 1 - slot, 2 + g)
    tail = hp[TM - POOL_MAX:, :]
    ext[0:POOL_MAX, :] = tail
    pl_ref[...] = tail

    x3 = x + _pool_project(d_groups, wp_ref, sc_ref[...])
    x3_ref[...] = x3
    issue(i + 1, 1 - slot, 6)
    h, ids, gates = _route(x3, nf_ref[...], wr_ref[...], br_ref[...])
    h_ref[...] = h.reshape(h_ref.shape)
    issue(i + 1, 1 - slot, 7)
    ri_ref[...] = _rank_pack(ids, cnt_ref)
    rg_ref[...] = gates


def _mix1_prompt(dest, x_all, rg_all, ys, nm, wp, sc, nf, wr, br):
    def const(shape):
        nd = len(shape)
        return pl.BlockSpec(shape, lambda i, d, _n=nd: (0,) * _n)

    row_spec = pl.BlockSpec((TM, D_MODEL), lambda i, d: (i, 0))
    row3_spec = pl.BlockSpec((TM, ROW_TILE, LANES), lambda i, d: (i, 0, 0))
    lane_spec = pl.BlockSpec((TM, LANES), lambda i, d: (i, 0))
    return pl.pallas_call(
        _prompt_steps(_mix1_prompt_kernel, 10),
        grid_spec=pltpu.PrefetchScalarGridSpec(
            num_scalar_prefetch=1,
            grid=(N_ROW_BLOCKS,),
            in_specs=[row_spec, lane_spec, pl.BlockSpec(memory_space=pl.ANY), const((1, D_MODEL)),
                      const((len(POOL_SIZES), POOL_GROUP_DIM, POOL_GROUP_DIM)), const((1, D_MODEL)),
                      const((1, D_MODEL)), const((D_MODEL, 2 * LANES)), const((1, LANES))],
            out_specs=[row_spec, row3_spec, pl.BlockSpec((8, TM), lambda i, d: (0, i)), lane_spec,
                       pl.BlockSpec((None, POOL_MAX, D_MODEL),
                                    lambda i, d: (jnp.minimum(i // STEPS_PER_BATCH, BATCH - 1), 0, 0)),
                       const((1, LANES))],
            scratch_shapes=[pltpu.VMEM((POOL_MAX + TM, D_MODEL), F32),
                            pltpu.VMEM((2, TOP_K, TM, ROW_TILE, LANES), F32), pltpu.SemaphoreType.DMA((2,))],
        ),
        out_shape=[jax.ShapeDtypeStruct((T_ALL, D_MODEL), F32), jax.ShapeDtypeStruct((T_ALL, ROW_TILE, LANES), BF16),
                   jax.ShapeDtypeStruct((8, T_ALL), jnp.int32), jax.ShapeDtypeStruct((T_ALL, LANES), F32),
                   jax.ShapeDtypeStruct((BATCH, POOL_MAX, D_MODEL), F32), jax.ShapeDtypeStruct((1, LANES), F32)],
        compiler_params=_cparams(("arbitrary",)),
        name="mix1_prompt",
    )(dest, x_all, rg_all, ys, nm, wp, sc, nf, wr, br)


def _mix1_sample_kernel(x_ref, st_ref, nm_ref, wp_ref, sc_ref, nf_ref, wr_ref, br_ref, cnt_in,
                        x3_in, h_in, ri_in, rg_in,
                        x3_ref, h_ref, ri_ref, rg_ref, hs_ref, cnt_ref):
    del x3_in, h_in, ri_in, rg_in
    x = x_ref[...]
    hs = _rms(x, nm_ref[...])
    hs_ref[...] = hs
    n_ctx = POOL_MAX - 1
    d_groups = []
    for g, w in enumerate(POOL_SIZES):
        cols = slice(g * POOL_GROUP_DIM, (g + 1) * POOL_GROUP_DIM)
        parts = []
        for t in range(DEC_SEQ):
            acc = hs[t * DEC_BATCH:(t + 1) * DEC_BATCH, cols]
            for back in range(1, w):
                src = t - back
                if src >= 0:
                    acc = acc + hs[src * DEC_BATCH:(src + 1) * DEC_BATCH, cols]
                else:
                    acc = acc + st_ref[n_ctx + src, :, cols]
            parts.append(acc / float(w) - hs[t * DEC_BATCH:(t + 1) * DEC_BATCH, cols])
        d_groups.append(jnp.concatenate(parts, axis=0))
    x3 = x + _pool_project(d_groups, wp_ref, sc_ref[...])
    x3_ref[...] = x3
    h, ids, gates = _route(x3, nf_ref[...], wr_ref[...], br_ref[...])
    h_ref[...] = h.reshape(h_ref.shape)
    cnt_ref[...] = cnt_in[...]
    ri_ref[...] = _rank_pack(ids, cnt_ref)
    rg_ref[...] = gates


def _mix1_sample(x_all, state_t, nm, wp, sc, nf, wr, br, cnt, x3_all, h_all, ri_all, rg_all):
    sample_rows = pl.BlockSpec((TM, D_MODEL), lambda g: (N_PROMPT_BLOCKS, 0))
    sample_rows3 = pl.BlockSpec((TM, ROW_TILE, LANES), lambda g: (N_PROMPT_BLOCKS, 0, 0))
    sample_lanes = pl.BlockSpec((TM, LANES), lambda g: (N_PROMPT_BLOCKS, 0))
    anyspec = pl.BlockSpec(memory_space=pl.ANY)
    n_in = 9
    return pl.pallas_call(
        _mix1_sample_kernel,
        grid=(1,),
        in_specs=[_const_spec((TM, D_MODEL)), _const_spec((POOL_MAX - 1, DEC_BATCH, D_MODEL)), _const_spec((1, D_MODEL)),
                  _const_spec((len(POOL_SIZES), POOL_GROUP_DIM, POOL_GROUP_DIM)), _const_spec((1, D_MODEL)),
                  _const_spec((1, D_MODEL)), _const_spec((D_MODEL, 2 * LANES)), _const_spec((1, LANES)),
                  _const_spec((1, LANES)), anyspec, anyspec, anyspec, anyspec],
        out_specs=[sample_rows, sample_rows3, pl.BlockSpec((8, TM), lambda g: (0, N_PROMPT_BLOCKS)), sample_lanes,
                   _const_spec((T_SAMPLE, D_MODEL)), _const_spec((1, LANES))],
        out_shape=[jax.ShapeDtypeStruct((T_ALL, D_MODEL), F32), jax.ShapeDtypeStruct((T_ALL, ROW_TILE, LANES), BF16),
                   jax.ShapeDtypeStruct((8, T_ALL), jnp.int32), jax.ShapeDtypeStruct((T_ALL, LANES), F32),
                   jax.ShapeDtypeStruct((T_SAMPLE, D_MODEL), F32), jax.ShapeDtypeStruct((1, LANES), F32)],
        input_output_aliases={n_in: 0, n_in + 1: 1, n_in + 2: 2, n_in + 3: 3},
        compiler_params=_cparams(("arbitrary",)),
        name="mix1_sample",
    )(x_all, state_t, nm, wp, sc, nf, wr, br, cnt, x3_all, h_all, ri_all, rg_all)


def _router_weights(wg, bg, we, be):
    w = jnp.concatenate([wg, jnp.transpose(we, (1, 0, 2)).reshape(D_MODEL, N_EXPERTS)], axis=1)
    b = jnp.concatenate([bg, be.reshape(N_EXPERTS)])
    pad = LANES - N_GROUPS - N_EXPERTS
    w = jnp.pad(w, ((0, 0), (0, pad)))
    w_hi = w.astype(BF16)
    w_lo = (w - w_hi.astype(F32)).astype(BF16)
    return jnp.concatenate([w_hi, w_lo], axis=1), jnp.pad(b, (0, pad)).reshape(1, LANES)


def _stack(tab):
    return jnp.stack([jnp.concatenate([tab[h] for h in heads], axis=0) for heads in STACK_HEADS])


def kernel(x_prompt, x_sample, cache_k_win, cache_v_win, state_pool, norm_mix, norm_ffn, norm_final, w_in,
           a_ln_g, a_ln_b, a_w_s, a_b_s, b_sinks, rel_bias_table, w_out, c_w_pool, c_scale,
           router_group_w, router_group_b, router_expert_w, router_expert_b, w_gate, w_up, w_down):
    xs_t = jnp.transpose(x_sample, (1, 0, 2)).reshape(T_SAMPLE, D_MODEL)
    xp2 = x_prompt.reshape(T_PROMPT, D_MODEL)
    win =w_in[0].astype(BF16)
    wout = w_out[0].astype(BF16)
    lng = a_ln_g[0].reshape(1, A_WIDTH)
    lnb = a_ln_b[0].reshape(1, A_WIDTH)
    bias_p, bias_sc, bias_sn, ws_tril = _prep(rel_bias_table, a_w_s[0])
    wsp = ws_tril.reshape(A_HEADS // 2, 2, CHUNK, CHUNK).transpose(0, 2, 1, 3).reshape(A_HEADS // 2, CHUNK, 2 * CHUNK)
    bs_full = jnp.repeat(a_b_s[0].T, A_HEAD_DIM, axis=1)
    bias_p = jnp.stack([_stack(bias_p[0]), _stack(bias_p[1])])
    bias_sc = _stack(bias_sc)
    bias_sn = _stack(bias_sn)
    sinks = b_sinks[0]
    sink_p = jnp.stack([jnp.repeat(sinks[jnp.array(hh)], WINDOW) for hh in STACK_HEADS])
    sink_s = jnp.stack([jnp.repeat(sinks[jnp.array(hh)], 32) for hh in STACK_HEADS])
    bias_p = bias_p.at[:, :, :, 0].set(jnp.broadcast_to(sink_p[None], (2, 2, 4 * WINDOW)))
    bias_sc = bias_sc.at[:, :, 0].set(sink_s)
    pairs = [(t, s) for t in range(DEC_SEQ) for s in range(t + 1)]
    wcoef = jnp.stack([jnp.repeat(a_w_s[0][:, t, s], A_HEAD_DIM) for t, s in pairs])
    wcoef = jnp.pad(wcoef, ((0, 16 - len(pairs)), (0, 0)))
    bcoef = jnp.pad(jnp.repeat(a_b_s[0][:, :DEC_SEQ].T, A_HEAD_DIM, axis=1), ((0, 8 - DEC_SEQ), (0, 0)))
    ck = cache_k_win
    cv = cache_v_win
    routers = [_router_weights(router_group_w[l], router_group_b[l], router_expert_w[l], router_expert_b[l])
               for l in range(2)]
    nm = [norm_mix[l].reshape(1, D_MODEL) for l in range(2)]
    nf = [norm_ffn[l].reshape(1, D_MODEL) for l in range(2)]

    x1_all, h_all, ri_all, rg_all, k_last, v_last, va_last, cnt0 = _mix0_prompt(
        xp2, nm[0], win, lng, lnb, wsp, bs_full, bias_p, wout, nf[0], *routers[0])
    x1_all, h_all, ri_all, rg_all, k_new, v_new, va_s, cnt0 = _mix0_sample(
        xs_t, nm[0], win, lng, lnb, wcoef, bcoef, ck, cv, bias_sc, bias_sn, wout, nf[0], *routers[0], cnt0,
        x1_all, h_all, ri_all, rg_all)
    dest0, ys0 = _moe(h_all, ri_all, cnt0, w_gate, w_up, w_down, 0)
    x2_s = _combine_sample(dest0, x1_all, rg_all, ys0)

    wp = c_w_pool[0].astype(BF16)
    sc = c_scale[0].reshape(1, D_MODEL)
    x3_all, h2_all, ri2_all, rg2_all, pool_tail, cnt1 = _mix1_prompt(
        dest0, x1_all, rg_all, ys0, nm[1], wp, sc, nf[1], *routers[1])
    state_t = jnp.transpose(state_pool[0], (1, 0, 2))
    x3_all, h2_all, ri2_all, rg2_all, hs1, cnt1 = _mix1_sample(
        x2_s, state_t, nm[1], wp, sc, nf[1], *routers[1], cnt1, x3_all, h2_all, ri2_all, rg2_all)
    dest1, ys1 = _moe(h2_all, ri2_all, cnt1, w_gate, w_up, w_down, 1)
    y_p, y_s = _final(dest1, x3_all, rg2_all, ys1, norm_final.reshape(1, D_MODEL))

    def from_tmajor(a, width):
        return jnp.transpose(a.reshape(DEC_SEQ, DEC_BATCH, width), (1, 0, 2))

    y_prompt = y_p.reshape(BATCH, SEQ, D_MODEL)
    y_sample = from_tmajor(y_s, D_MODEL)
    win_k_p = k_last.reshape(1, BATCH, WINDOW, B_KV_HEADS, B_HEAD_DIM)
    win_v_p = v_last.reshape(1, BATCH, WINDOW, B_KV_HEADS, B_HEAD_DIM)
    kn = from_tmajor(k_new, KV_WIDTH).reshape(DEC_BATCH, DEC_SEQ, B_KV_HEADS, B_HEAD_DIM)
    vn = from_tmajor(v_new, KV_WIDTH).reshape(DEC_BATCH, DEC_SEQ, B_KV_HEADS, B_HEAD_DIM)
    win_k_s = jnp.concatenate([cache_k_win[0][:, DEC_SEQ:], kn], axis=1)[None]
    win_v_s = jnp.concatenate([cache_v_win[0][:, DEC_SEQ:], vn], axis=1)[None]
    chunk_v_p = va_last.reshape(1, BATCH, CHUNK, A_HEADS, A_HEAD_DIM)
    chunk_v_s = from_tmajor(va_s, A_WIDTH).reshape(1, DEC_BATCH, DEC_SEQ, A_HEADS, A_HEAD_DIM)
    pool_p = pool_tail[:, 1:][None]
    pool_s = jnp.concatenate([state_pool[0][:, DEC_SEQ:], from_tmajor(hs1, D_MODEL)], axis=1)[None]
    return (y_prompt, y_sample, win_k_p, win_v_p, win_k_s, win_v_s, chunk_v_p, chunk_v_s, pool_p, pool_s)
```

```python
import functools
import math

import numpy as np
import jax
import jax.numpy as jnp
from jax import lax
from jax.experimental import pallas as pl
from jax.experimental.pallas import tpu as pltpu

F32 = jnp.float32
BF16 = jnp.bfloat16

D_MODEL = 1024
BATCH = 2
SEQ = 8192
DEC_BATCH = 128
DEC_SEQ = 4
A_WIDTH = 512
A_HEADS = 8
A_HEAD_DIM = 64
CHUNK = 128
B_HEADS = 8
B_KV_HEADS = 2
B_HEAD_DIM = 64
B_GROUP = 4
WINDOW = 128
N_BUCKETS = 32
MAX_DISTANCE = WINDOW
Q_WIDTH = 512
KV_WIDTH = 128
IN_WIDTH = 2 * A_WIDTH + Q_WIDTH + 2 * KV_WIDTH
ATTN_SCALE = B_HEAD_DIM ** -0.5
NEG_INF = -1e30
POOL_SIZES = (2, 4, 8, 16)
POOL_GROUP_DIM = 256
POOL_MAX = 16
N_GROUPS = 4
EXPERTS_PER_GROUP = 8
N_EXPERTS = 32
TOP_K = 2
D_EXPERT = 512
EPS = 1e-6

LANES = 128
ROW_TILE = D_MODEL // LANES
T_PROMPT = BATCH * SEQ
T_SAMPLE = DEC_BATCH * DEC_SEQ
T_ALL = T_PROMPT + T_SAMPLE
TM = 512
N_PROMPT_BLOCKS = T_PROMPT // TM
N_ROW_BLOCKS = T_ALL // TM
STEPS_PER_BATCH = SEQ // TM
SUB = TM // WINDOW
N_SLOTS = T_ALL * TOP_K
MOE_BLK = 512
N_MOE_BLOCKS = N_SLOTS // MOE_BLK + N_EXPERTS
N_SORT_ROWS = N_MOE_BLOCKS * MOE_BLK
SAMPLE_GROUP = 8
N_SAMPLE_GROUPS = DEC_BATCH // SAMPLE_GROUP
VMEM_LIMIT = 56 * 1024 * 1024

STACK_HEADS = ((0, 2, 5, 7), (1, 3, 4, 6))


def _t5_bucket_np(dist):
    n = np.maximum(dist, 0)
    max_exact = N_BUCKETS // 2
    nf = np.maximum(n, 1).astype(np.float32)
    large = max_exact + (np.log(nf / np.float32(max_exact)) / np.float32(math.log(MAX_DISTANCE / max_exact))
                         * np.float32(N_BUCKETS - max_exact)).astype(np.int32)
    large = np.minimum(large, N_BUCKETS - 1)
    return np.where(n < max_exact, n, large).astype(np.int32)


def _bucket_tables():
    qi = np.arange(WINDOW)[:, None]
    ki = np.arange(2 * WINDOW)[None, :]
    dist = qi + WINDOW - ki
    valid = (dist >= 0) & (dist < WINDOW)
    bp = np.where(valid, _t5_bucket_np(dist), -1)
    bp_first = np.where(ki >= WINDOW, bp, -1)
    bkt_p = np.stack([bp_first, bp]).astype(np.int32)

    t = np.repeat(np.arange(DEC_SEQ), SAMPLE_GROUP)[:, None]
    b = np.tile(np.arange(SAMPLE_GROUP), DEC_SEQ)[:, None]
    cb = np.repeat(np.arange(SAMPLE_GROUP), WINDOW)[None, :]
    cj = np.tile(np.arange(WINDOW), SAMPLE_GROUP)[None, :]
    dist_c = t + WINDOW - cj
    valid_c = (cb == b) & (dist_c >= 0) & (dist_c < WINDOW)
    bkt_sc = np.where(valid_c, _t5_bucket_np(dist_c), -1).astype(np.int32)
    nt = np.repeat(np.arange(DEC_SEQ), SAMPLE_GROUP)[None, :]
    nb = np.tile(np.arange(SAMPLE_GROUP), DEC_SEQ)[None, :]
    dist_n = t - nt
    valid_n = (nb == b) & (dist_n >= 0)
    bkt_sn = np.where(valid_n, _t5_bucket_np(dist_n), -1).astype(np.int32)
    bkt_sn = np.concatenate([bkt_sn, np.full((32, LANES - 32), -1, np.int32)], axis=1)
    return bkt_p, bkt_sc, bkt_sn


_BKT_P, _BKT_SC, _BKT_SN = _bucket_tables()


def _cparams(semantics):
    return pltpu.CompilerParams(dimension_semantics=semantics, vmem_limit_bytes=VMEM_LIMIT)


def _rms(x, g):
    return x * lax.rsqrt(jnp.mean(x * x, axis=-1, keepdims=True) + EPS) * g


def _layernorm(x, g, b):
    xc = x - jnp.mean(x, axis=-1, keepdims=True)
    return xc * lax.rsqrt(jnp.mean(xc * xc, axis=-1, keepdims=True) + EPS) * g + b


def _dot(a, b):
    return jnp.dot(a, b, preferred_element_type=F32)


def _dot_nt(a, b):
    return lax.dot_general(a, b, (((1,), (1,)), ((), ())), preferred_element_type=F32)


def _project(x, nm, win, lng, lnb):
    h = _rms(x, nm)
    z = _dot(h.astype(BF16), win)
    u = jax.nn.gelu(z[:, :A_WIDTH])
    va = _layernorm(jax.nn.gelu(z[:, A_WIDTH:2 * A_WIDTH]), lng, lnb)
    q = z[:, 2 * A_WIDTH:2 * A_WIDTH + Q_WIDTH] * ATTN_SCALE
    k = z[:, 2 * A_WIDTH + Q_WIDTH:2 * A_WIDTH + Q_WIDTH + KV_WIDTH]
    v = z[:, 2 * A_WIDTH + Q_WIDTH + KV_WIDTH:]
    return u, va, q, k, v


def _route(x1, nf, wr, br):
    hf = _rms(x1, nf)
    h = hf.astype(BF16)
    h_lo = (hf - h.astype(F32)).astype(BF16)
    part = _dot(h, wr)
    logits = part[:, :LANES] + part[:, LANES:] + _dot(h_lo, wr[:, :LANES]) + br
    rows = logits.shape[0]
    lane = lax.broadcasted_iota(jnp.int32, (rows, LANES), 1)
    lanef = lane.astype(F32)
    big = jnp.float32(1e9)
    is_g = lane < N_GROUPS
    gl = jnp.where(is_g, logits, -jnp.inf)
    gmax = jnp.max(gl, axis=1, keepdims=True)
    gsel = jnp.min(jnp.where(gl == gmax, lanef, big), axis=1, keepdims=True)
    gsum = jnp.sum(jnp.where(is_g, jnp.exp(logits - gmax), 0.0), axis=1, keepdims=True)
    g1 = 1.0 / gsum
    lo = N_GROUPS + EXPERTS_PER_GROUP * gsel
    emask = (lanef >= lo) & (lanef < lo + EXPERTS_PER_GROUP)
    el = jnp.where(emask, logits, -jnp.inf)
    v1 = jnp.max(el, axis=1, keepdims=True)
    i1 = jnp.min(jnp.where(el == v1, lanef, big), axis=1, keepdims=True)
    el2 = jnp.where(lanef == i1, -jnp.inf, el)
    v2 = jnp.max(el2, axis=1, keepdims=True)
    i2 = jnp.min(jnp.where(el2 == v2, lanef, big), axis=1, keepdims=True)
    e2 = jnp.exp(v2 - v1)
    den = 1.0 + e2
    w1 = g1 / den
    w2 = g1 * e2 / den
    ids = jnp.where(lane == 0, i1 - N_GROUPS, jnp.where(lane == 1, i2 - N_GROUPS, 0.0)).astype(jnp.int32)
    gates = jnp.where(lane == 0, w1, jnp.where(lane == 1, w2, 0.0))
    return h, ids, gates


def _rank_pack(ids, cnt_ref, tcnt_ref):
    rows = ids.shape[0]
    lane = lax.broadcasted_iota(jnp.int32, (rows, LANES), 1)
    o0 = (lane == ids[:, 0:1]).astype(F32)
    o1 = (lane == ids[:, 1:2]).astype(F32)
    r = lax.broadcasted_iota(jnp.int32, (rows, rows), 0)
    c = lax.broadcasted_iota(jnp.int32, (rows, rows), 1)
    before = (c < r).astype(BF16)
    p01 = _dot(before, jnp.concatenate([o0, o1], axis=1).astype(BF16))
    p0 = p01[:, :LANES]
    p1 = p01[:, LANES:]
    c0 = jnp.sum(o0, axis=0, keepdims=True)
    c1 = jnp.sum(o1, axis=0, keepdims=True)
    carry = cnt_ref[...]
    rank0 = jnp.sum(o0 * (carry + p0), axis=1, keepdims=True)
    rank1 = jnp.sum(o1 * (carry + c0 + p1), axis=1, keepdims=True)
    ctile = c0 + c1
    cnt_ref[...] = carry + ctile
    tcnt_ref[...] = ctile
    inc = jnp.broadcast_to(ctile, (8, LANES))
    lane8 = lax.broadcasted_iota(jnp.int32, (8, LANES), 1)
    for sh in (1, 2, 4, 8, 16, 32, 64):
        inc = inc + jnp.where(lane8 >= sh, pltpu.roll(inc, sh, 1), 0.0)
    start = inc[0:1] - ctile
    lpos0 = jnp.sum(o0 * (start + p0), axis=1, keepdims=True)
    lpos1 = jnp.sum(o1 * (start + c0 + p1), axis=1, keepdims=True)
    idf = ids.astype(F32)
    packed = jnp.where(lane < TOP_K, idf, 0.0)
    for ln, col in ((2, rank0), (3, rank1), (4, lpos0), (5, lpos1)):
        packed = jnp.where(lane == ln, col, packed)
    return jnp.transpose(packed)[:8].astype(jnp.int32)


def _prep_kernel(tab_ref, bp_ref, bsc_ref, bsn_ref, ws_ref, op_ref, osc_ref, osn_ref, ows_ref):
    def fill(bkt, write):
        for h in range(B_HEADS):
            acc = jnp.full(bkt.shape, NEG_INF, F32)
            for b in range(N_BUCKETS):
                acc = jnp.where(bkt == b, tab_ref[b, h], acc)
            write(h, acc)

    for var in range(2):
        def wr_p(h, acc, var=var):
            op_ref[var, h] = acc
        fill(bp_ref[var], wr_p)

    def wr_sc(h, acc):
        osc_ref[h] = acc
    fill(bsc_ref[...], wr_sc)

    def wr_sn(h, acc):
        osn_ref[h] = acc
    fill(bsn_ref[...], wr_sn)

    r = lax.broadcasted_iota(jnp.int32, (CHUNK, CHUNK), 0)
    c = lax.broadcasted_iota(jnp.int32, (CHUNK, CHUNK), 1)
    for h in range(A_HEADS):
        ows_ref[h] = jnp.where(r >= c, ws_ref[h], 0.0).astype(BF16)


def _prep(rel_bias_table, w_s):
    vm = pl.BlockSpec(memory_space=pltpu.VMEM)
    return pl.pallas_call(
        _prep_kernel,
        in_specs=[pl.BlockSpec(memory_space=pltpu.SMEM), vm, vm, vm, vm],
        out_specs=[vm, vm, vm, vm],
        out_shape=[
            jax.ShapeDtypeStruct((2, B_HEADS, WINDOW, 2 * WINDOW), F32),
            jax.ShapeDtypeStruct((B_HEADS, 32, SAMPLE_GROUP * WINDOW), F32),
            jax.ShapeDtypeStruct((B_HEADS, 32, LANES), F32),
            jax.ShapeDtypeStruct((A_HEADS, CHUNK, CHUNK), BF16),
        ],
        name="prep_tables",
    )(rel_bias_table, jnp.asarray(_BKT_P), jnp.asarray(_BKT_SC), jnp.asarray(_BKT_SN), w_s)


def _gate_pairs(va_rows, wsp_ref, lane_lo):
    outs = []
    for p in range(A_HEADS // 2):
        vp = va_rows[:, p * LANES:(p + 1) * LANES]
        rhs = jnp.concatenate([jnp.where(lane_lo, vp, 0.0), jnp.where(lane_lo, 0.0, vp)], axis=0).astype(BF16)
        outs.append(_dot(wsp_ref[p], rhs))
    return jnp.concatenate(outs, axis=1)


def _prompt_steps(body, first_row_out):
    def kern(*refs):
        i = pl.program_id(0)

        @pl.when(i < N_PROMPT_BLOCKS)
        def _():
            body(*refs)

        @pl.when(i >= N_PROMPT_BLOCKS)
        def _():
            for r in refs[first_row_out:first_row_out + 5]:
                r[...] = jnp.zeros(r.shape, r.dtype)

    return kern


def _mix0_prompt_kernel(x_ref, nm_ref, win_ref, lng_ref, lnb_ref, wsp_ref, bs_ref, bias_ref,
                        wout_ref, nf_ref, wr_ref, br_ref,
                        x1_ref, h_ref, ri_ref, rg_ref, tc_ref, kl_ref, vl_ref, val_ref, cnt_ref,
                        kprev, vprev, mix_scr):
    @pl.when(pl.program_id(0) == 0)
    def _():
        cnt_ref[...] = jnp.zeros_like(cnt_ref)

    x = x_ref[...]
    u, va, q, k, v = _project(x, nm_ref[...], win_ref[...], lng_ref[...], lnb_ref[...])
    lane_lo = lax.broadcasted_iota(jnp.int32, (WINDOW, LANES), 1) < B_HEAD_DIM
    row0 = lax.broadcasted_iota(jnp.int32, (WINDOW, KV_WIDTH), 0) == 0
    first = pl.program_id(0) % STEPS_PER_BATCH == 0

    @pl.when(first)
    def _():
        kprev[...] = jnp.zeros_like(kprev)
        vprev[...] = jnp.zeros_like(vprev)

    for j in range(SUB):
        rows = slice(j * WINDOW, (j + 1) * WINDOW)
        s_gate = _gate_pairs(va[rows], wsp_ref, lane_lo)
        mix_scr[rows, :A_WIDTH] = u[rows] * (s_gate + bs_ref[...])

        if j == 0:
            kp, vp = kprev[...], vprev[...]
        else:
            prows = slice((j - 1) * WINDOW, j * WINDOW)
            kp, vp = k[prows], v[prows]
        kk = jnp.concatenate([jnp.where(row0, 0.0, kp), k[rows]], axis=0)
        vv = jnp.concatenate([jnp.where(row0, 0.0, vp), v[rows]], axis=0)
        kops = (kk.astype(BF16), pltpu.roll(kk, B_HEAD_DIM, 1).astype(BF16))
        vops = (vv.astype(BF16), pltpu.roll(vv, B_HEAD_DIM, 1).astype(BF16))
        qt = [q[rows, p * LANES:(p + 1) * LANES] for p in range(4)]
        q_even = [jnp.where(lane_lo, t, 0.0) for t in qt]
        q_odd = [jnp.where(lane_lo, 0.0, t) for t in qt]
        stacks = (jnp.concatenate([q_even[0], q_even[1], q_odd[2], q_odd[3]], axis=0),
                  jnp.concatenate([q_odd[0], q_odd[1], q_even[2], q_even[3]], axis=0))
        o = []
        for st in range(2):
            s = _dot_nt(stacks[st].astype(BF16), kops[st])
            if j == 0:
                bias = bias_ref[jnp.where(first, 0, 1), st]
            else:
                bias = bias_ref[1, st]
            s = s + bias
            m = jnp.max(s, axis=-1, keepdims=True)
            p = jnp.exp(s - m)
            den = jnp.sum(p, axis=-1, keepdims=True)
            o.append(_dot(p.astype(BF16), vops[st]) / den)
        oa, ob = o
        sl = [slice(i * WINDOW, (i + 1) * WINDOW) for i in range(4)]
        tiles = (jnp.where(lane_lo, oa[sl[0]], ob[sl[0]]), jnp.where(lane_lo, oa[sl[1]], ob[sl[1]]),
                 jnp.where(lane_lo, ob[sl[2]], oa[sl[2]]), jnp.where(lane_lo, ob[sl[3]], oa[sl[3]]))
        for p in range(4):
            mix_scr[rows, A_WIDTH + p * LANES:A_WIDTH + (p + 1) * LANES] = tiles[p]

    last = slice(TM - WINDOW, TM)
    kprev[...] = k[last]
    vprev[...] = v[last]
    kl_ref[...] = k[last]
    vl_ref[...] = v[last]
    val_ref[...] = va[last]

    x1 = x + _dot(mix_scr[...].astype(BF16), wout_ref[...])
    x1_ref[...] = x1
    h, ids, gates = _route(x1, nf_ref[...], wr_ref[...], br_ref[...])
    h_ref[...] = h.reshape(h_ref.shape)
    ri_ref[...] = _rank_pack(ids, cnt_ref, tc_ref)
    rg_ref[...] = gates


def _const_spec(shape):
    nd = len(shape)
    return pl.BlockSpec(shape, lambda i, _n=nd: (0,) * _n)


def _mix0_prompt(x_all, nm, win, lng, lnb, wsp, bs_full, bias_p, wout, nf, wr, br):
    row_spec = pl.BlockSpec((TM, D_MODEL), lambda i: (i, 0))
    row3_spec = pl.BlockSpec((TM, ROW_TILE, LANES), lambda i: (i, 0, 0))
    lane_spec = pl.BlockSpec((TM, LANES), lambda i: (i, 0))
    last_kv = pl.BlockSpec((None, WINDOW, KV_WIDTH), lambda i: (jnp.minimum(i // STEPS_PER_BATCH, BATCH - 1), 0, 0))
    last_va = pl.BlockSpec((None, WINDOW, A_WIDTH), lambda i: (jnp.minimum(i // STEPS_PER_BATCH, BATCH - 1), 0, 0))
    return pl.pallas_call(
        _prompt_steps(_mix0_prompt_kernel, 12),
        grid=(N_ROW_BLOCKS,),
        in_specs=[pl.BlockSpec((TM, D_MODEL), lambda i: (jnp.minimum(i, N_PROMPT_BLOCKS - 1), 0)),
                  _const_spec((1, D_MODEL)), _const_spec((D_MODEL, IN_WIDTH)),
                  _const_spec((1, A_WIDTH)), _const_spec((1, A_WIDTH)),
                  _const_spec((A_HEADS // 2, CHUNK, 2 * CHUNK)), _const_spec((CHUNK, A_WIDTH)),
                  _const_spec((2, 2, 4 * WINDOW, 2 * WINDOW)),
                  _const_spec((A_WIDTH + Q_WIDTH, D_MODEL)), _const_spec((1, D_MODEL)),
                  _const_spec((D_MODEL, 2 * LANES)), _const_spec((1, LANES))],
        out_specs=[row_spec, row3_spec, pl.BlockSpec((8, TM), lambda i: (0, i)), lane_spec,
                   pl.BlockSpec((None, 1, LANES), lambda i: (i, 0, 0)),
                   last_kv, last_kv, last_va, _const_spec((1, LANES))],
        out_shape=[jax.ShapeDtypeStruct((T_ALL, D_MODEL), F32), jax.ShapeDtypeStruct((T_ALL, ROW_TILE, LANES), BF16),
                   jax.ShapeDtypeStruct((8, T_ALL), jnp.int32), jax.ShapeDtypeStruct((T_ALL, LANES), F32),
                   jax.ShapeDtypeStruct((N_ROW_BLOCKS, 1, LANES), F32),
                   jax.ShapeDtypeStruct((BATCH, WINDOW, KV_WIDTH), F32),
                   jax.ShapeDtypeStruct((BATCH, WINDOW, KV_WIDTH), F32),
                   jax.ShapeDtypeStruct((BATCH, WINDOW, A_WIDTH), F32),
                   jax.ShapeDtypeStruct((1, LANES), F32)],
        scratch_shapes=[pltpu.VMEM((WINDOW, KV_WIDTH), F32), pltpu.VMEM((WINDOW, KV_WIDTH), F32),
                        pltpu.VMEM((TM, D_MODEL), F32)],
        compiler_params=_cparams(("arbitrary",)),
        name="mix0_prompt",
    )(x_all, nm, win, lng, lnb, wsp, bs_full, bias_p, wout, nf, wr, br)


def _mix0_sample_kernel(x_ref, nm_ref, win_ref, lng_ref, lnb_ref, wcoef_ref, bcoef_ref,
                        ck_ref, cv_ref, bsc_ref, bsn_ref,
                        wout_ref, nf_ref, wr_ref, br_ref, cnt_in,
                        x1_in, h_in, ri_in, rg_in, tc_in,
                        x1_ref, h_ref, ri_ref, rg_ref, tc_ref, kn_ref, vn_ref, va_ref, cnt_ref,
                        q_scr, k_scr, v_scr, mix_scr):
    del x1_in, h_in, ri_in, rg_in, tc_in
    g = pl.program_id(0)

    @pl.when(g == 0)
    def _():
        u, va, q, k, v = _project(x_ref[...], nm_ref[...], win_ref[...], lng_ref[...], lnb_ref[...])
        q_scr[...] = q
        k_scr[...] = k
        v_scr[...] = v
        kn_ref[...] = k
        vn_ref[...] = v
        va_ref[...] = va
        idx = 0
        for t in range(DEC_SEQ):
            acc = jnp.zeros((DEC_BATCH, A_WIDTH), F32) + bcoef_ref[t:t + 1, :]
            for s in range(t + 1):
                acc = acc + wcoef_ref[idx:idx + 1, :] * va[s * DEC_BATCH:(s + 1) * DEC_BATCH]
                idx += 1
            mix_scr[t * DEC_BATCH:(t + 1) * DEC_BATCH, :A_WIDTH] = u[t * DEC_BATCH:(t + 1) * DEC_BATCH] * acc

    b0 = pl.multiple_of(g * SAMPLE_GROUP, SAMPLE_GROUP)
    lane_lo = lax.broadcasted_iota(jnp.int32, (DEC_SEQ * SAMPLE_GROUP, LANES), 1) < B_HEAD_DIM

    def grab(ref, width):
        return jnp.concatenate([ref[pl.ds(t * DEC_BATCH + b0, SAMPLE_GROUP), :] for t in range(DEC_SEQ)], axis=0)

    qg = grab(q_scr, Q_WIDTH)
    kn = grab(k_scr, KV_WIDTH)
    vn = grab(v_scr, KV_WIDTH)
    crow0 = lax.broadcasted_iota(jnp.int32, (SAMPLE_GROUP * WINDOW, KV_WIDTH), 0) == 0
    rows_kv = (SAMPLE_GROUP * WINDOW, KV_WIDTH)
    kc = jnp.where(crow0, 0.0, ck_ref[...].reshape(rows_kv))
    vc = jnp.where(crow0, 0.0, cv_ref[...].reshape(rows_kv))
    kc_ops = (kc.astype(BF16), pltpu.roll(kc, B_HEAD_DIM, 1).astype(BF16))
    vc_ops = (vc.astype(BF16), pltpu.roll(vc, B_HEAD_DIM, 1).astype(BF16))
    kn_ops = (kn.astype(BF16), pltpu.roll(kn, B_HEAD_DIM, 1).astype(BF16))
    vn_ops = (vn.astype(BF16), pltpu.roll(vn, B_HEAD_DIM, 1).astype(BF16))
    qt = [qg[:, p * LANES:(p + 1) * LANES] for p in range(4)]
    q_even = [jnp.where(lane_lo, t, 0.0) for t in qt]
    q_odd = [jnp.where(lane_lo, 0.0, t) for t in qt]
    stacks = (jnp.concatenate([q_even[0], q_even[1], q_odd[2], q_odd[3]], axis=0),
              jnp.concatenate([q_odd[0], q_odd[1], q_even[2], q_even[3]], axis=0))
    o = []
    for st in range(2):
        qs = stacks[st].astype(BF16)
        sc = _dot_nt(qs, kc_ops[st]) + bsc_ref[st]
        sn = _dot_nt(qs, kn_ops[st]) + bsn_ref[st][:, :DEC_SEQ * SAMPLE_GROUP]
        m = jnp.maximum(jnp.max(sc, axis=-1, keepdims=True), jnp.max(sn, axis=-1, keepdims=True))
        pc = jnp.exp(sc - m)
        pn = jnp.exp(sn - m)
        den = jnp.sum(pc, axis=-1, keepdims=True) + jnp.sum(pn, axis=-1, keepdims=True)
        o.append((_dot(pc.astype(BF16), vc_ops[st]) + _dot(pn.astype(BF16), vn_ops[st])) / den)
    oa, ob = o
    n = DEC_SEQ * SAMPLE_GROUP
    sl = [slice(i * n, (i + 1) * n) for i in range(4)]
    tiles = (jnp.where(lane_lo, oa[sl[0]], ob[sl[0]]), jnp.where(lane_lo, oa[sl[1]], ob[sl[1]]),
             jnp.where(lane_lo, ob[sl[2]], oa[sl[2]]), jnp.where(lane_lo, ob[sl[3]], oa[sl[3]]))
    for p in range(4):
        for t in range(DEC_SEQ):
            mix_scr[pl.ds(t * DEC_BATCH + b0, SAMPLE_GROUP), A_WIDTH + p * LANES:A_WIDTH + (p + 1) * LANES] = (
                tiles[p][t * SAMPLE_GROUP:(t + 1) * SAMPLE_GROUP])

    @pl.when(g == N_SAMPLE_GROUPS - 1)
    def _():
        x1 = x_ref[...] + _dot(mix_scr[...].astype(BF16), wout_ref[...])
        x1_ref[...] = x1
        h, ids, gates = _route(x1, nf_ref[...], wr_ref[...], br_ref[...])
        h_ref[...] = h.reshape(h_ref.shape)
        cnt_ref[...] = cnt_in[...]
        ri_ref[...] = _rank_pack(ids, cnt_ref, tc_ref)
        rg_ref[...] = gates


def _mix0_sample(x_all, nm, win, lng, lnb, wcoef, bcoef, ck, cv, bias_sc, bias_sn, wout, nf, wr, br, cnt,
                 x1_all, h_all, ri_all, rg_all, tc_all):
    sample_rows = pl.BlockSpec((TM, D_MODEL), lambda g: (N_PROMPT_BLOCKS, 0))
    sample_rows3 = pl.BlockSpec((TM, ROW_TILE, LANES), lambda g: (N_PROMPT_BLOCKS, 0, 0))
    sample_lanes = pl.BlockSpec((TM, LANES), lambda g: (N_PROMPT_BLOCKS, 0))
    cache_spec = pl.BlockSpec((None, SAMPLE_GROUP, WINDOW, B_KV_HEADS, B_HEAD_DIM), lambda g: (0, g, 0, 0, 0))
    anyspec = pl.BlockSpec(memory_space=pl.ANY)
    n_in = 16
    return pl.pallas_call(
        _mix0_sample_kernel,
        grid=(N_SAMPLE_GROUPS,),
        in_specs=[_const_spec((TM, D_MODEL)), _const_spec((1, D_MODEL)), _const_spec((D_MODEL, IN_WIDTH)),
                  _const_spec((1, A_WIDTH)), _const_spec((1, A_WIDTH)),
                  _const_spec((16, A_WIDTH)), _const_spec((8, A_WIDTH)),
                  cache_spec, cache_spec,
                  _const_spec((2, 4 * 32, SAMPLE_GROUP * WINDOW)), _const_spec((2, 4 * 32, LANES)),
                  _const_spec((A_WIDTH + Q_WIDTH, D_MODEL)), _const_spec((1, D_MODEL)),
                  _const_spec((D_MODEL, 2 * LANES)), _const_spec((1, LANES)), _const_spec((1, LANES)),
                  anyspec, anyspec, anyspec, anyspec, anyspec],
        out_specs=[sample_rows, sample_rows3, pl.BlockSpec((8, TM), lambda g: (0, N_PROMPT_BLOCKS)), sample_lanes,
                   pl.BlockSpec((None, 1, LANES), lambda g: (N_PROMPT_BLOCKS, 0, 0)),
                   _const_spec((T_SAMPLE, KV_WIDTH)), _const_spec((T_SAMPLE, KV_WIDTH)),
                   _const_spec((T_SAMPLE, A_WIDTH)), _const_spec((1, LANES))],
        out_shape=[jax.ShapeDtypeStruct((T_ALL, D_MODEL), F32), jax.ShapeDtypeStruct((T_ALL, ROW_TILE, LANES), BF16),
                   jax.ShapeDtypeStruct((8, T_ALL), jnp.int32), jax.ShapeDtypeStruct((T_ALL, LANES), F32),
                   jax.ShapeDtypeStruct((N_ROW_BLOCKS, 1, LANES), F32),
                   jax.ShapeDtypeStruct((T_SAMPLE, KV_WIDTH), F32), jax.ShapeDtypeStruct((T_SAMPLE, KV_WIDTH), F32),
                   jax.ShapeDtypeStruct((T_SAMPLE, A_WIDTH), F32), jax.ShapeDtypeStruct((1, LANES), F32)],
        scratch_shapes=[pltpu.VMEM((T_SAMPLE, Q_WIDTH), F32), pltpu.VMEM((T_SAMPLE, KV_WIDTH), F32),
                        pltpu.VMEM((T_SAMPLE, KV_WIDTH), F32), pltpu.VMEM((T_SAMPLE, D_MODEL), F32)],
        input_output_aliases={n_in: 0, n_in + 1: 1, n_in + 2: 2, n_in + 3: 3, n_in + 4: 4},
        compiler_params=_cparams(("arbitrary",)),
        name="mix0_sample",
    )(x_all, nm, win, lng, lnb, wcoef, bcoef, ck, cv, bias_sc, bias_sn, wout, nf, wr, br, cnt,
      x1_all, h_all, ri_all, rg_all, tc_all)


def _moe_metadata(rt_all, cnt, tcnt):
    counts = cnt[0, :N_EXPERTS].astype(jnp.int32)
    padded = (counts + MOE_BLK - 1) // MOE_BLK * MOE_BLK
    pad_end = jnp.cumsum(padded)
    pad_start = pad_end - padded
    experts = jnp.arange(N_EXPERTS, dtype=jnp.int32)
    eid = rt_all[0:TOP_K]
    base = jnp.sum(jnp.where(eid[:, :, None] == experts[None, None, :], pad_start[None, None, :], 0), axis=-1)
    dest = (base + rt_all[TOP_K:2 * TOP_K]).reshape(N_SLOTS).astype(jnp.int32)
    n_valid = (pad_end[-1] // MOE_BLK).astype(jnp.int32).reshape(1)
    blk_start = jnp.arange(N_MOE_BLOCKS, dtype=jnp.int32) * MOE_BLK
    block_e = jnp.minimum(jnp.sum((blk_start[:, None] >= pad_end[None, :]).astype(jnp.int32), axis=1),
                          N_EXPERTS - 1).astype(jnp.int32)
    zero_start = (pad_start + counts).astype(jnp.int32)
    zero_len = (padded - counts).astype(jnp.int32)
    first = (blk_start == pad_start[block_e]).astype(jnp.int32)
    used = counts > 0
    parity = ((jnp.cumsum(used.astype(jnp.int32)) - 1) % 2)[block_e].astype(jnp.int32)
    nearest = lax.cummin(jnp.where(used, experts, N_EXPERTS)[::-1])[::-1]
    next_used = jnp.concatenate([nearest[1:], jnp.full((1,), N_EXPERTS, jnp.int32)])
    nxt = jnp.where(next_used < N_EXPERTS, next_used, -1)[block_e].astype(jnp.int32)
    plan = (block_e, first, parity, nxt, n_valid)
    runs = tcnt[:, 0, :N_EXPERTS].astype(jnp.int32)
    run_dst = pad_start[None, :] + jnp.cumsum(runs, axis=0) - runs
    lpos = rt_all[2 * TOP_K:3 * TOP_K].reshape(N_SLOTS).astype(jnp.int32)
    dplan = (lpos, runs.reshape(-1), run_dst.reshape(-1).astype(jnp.int32),
             jnp.concatenate([zero_start, zero_len, n_valid]))
    return dest, plan, dplan


def _dispatch_kernel(lpos_ref, run_ref, rdst_ref, zs_ref, h_ref, xs_ref, zero_scr, stage, sem, zsem):
    i = pl.program_id(0)

    @pl.when(i == 0)
    def _():
        zero_scr[...] = jnp.zeros_like(zero_scr)

        def pieces(e, do):
            off = zs_ref[e]
            rem = zs_ref[N_EXPERTS + e]
            bit = MOE_BLK // 2
            while bit >= 1:
                take = (rem & bit) != 0

                @pl.when(take)
                def _(off=off, bit=bit):
                    do(pltpu.make_async_copy(zero_scr.at[pl.ds(0, bit)], xs_ref.at[pl.ds(off, bit)], zsem))

                off = off + jnp.where(take, bit, 0)
                bit //= 2

        def start_e(e, c):
            pieces(e, lambda cp: cp.start())
            return c

        def wait_e(e, c):
            pieces(e, lambda cp: cp.wait())
            return c

        def tail(do):
            def step(b, c):
                do(pltpu.make_async_copy(zero_scr, xs_ref.at[pl.ds(b * MOE_BLK, MOE_BLK)], zsem))
                return c
            return step

        n_valid = zs_ref[2 * N_EXPERTS]
        lax.fori_loop(0, N_EXPERTS, start_e, 0)
        lax.fori_loop(n_valid, N_MOE_BLOCKS, tail(lambda cp: cp.start()), 0)
        lax.fori_loop(0, N_EXPERTS, wait_e, 0)
        lax.fori_loop(n_valid, N_MOE_BLOCKS, tail(lambda cp: cp.wait()), 0)

    base = i * TM

    def place(r, carry):
        row = h_ref[r]
        for kk in range(TOP_K):
            stage[lpos_ref[kk * T_ALL + base + r]] = row
        return carry

    lax.fori_loop(0, TM, place, 0, unroll=8)

    def send_run(e, off):
        n = run_ref[i * N_EXPERTS + e]
        dst = rdst_ref[i * N_EXPERTS + e]
        bit = TM * TOP_K
        while bit >= 1:
            take = (n & bit) != 0

            @pl.when(take)
            def _(off=off, dst=dst, bit=bit):
                pltpu.make_async_copy(stage.at[pl.ds(off, bit)], xs_ref.at[pl.ds(dst, bit)], sem).start(
                    priority=bit.bit_length() % 2)

            step = jnp.where(take, bit, 0)
            off = off + step
            dst = dst + step
            bit //= 2
        return off

    lax.fori_loop(0, N_EXPERTS, send_run, 0)
    pltpu.make_async_copy(stage, xs_ref.at[pl.ds(0, TM * TOP_K)], sem).wait()


def _dispatch(dplan, h_all):
    return pl.pallas_call(
        _dispatch_kernel,
        grid_spec=pltpu.PrefetchScalarGridSpec(
            num_scalar_prefetch=4,
            grid=(N_ROW_BLOCKS,),
            in_specs=[pl.BlockSpec((TM, ROW_TILE, LANES), lambda i, lp, rn, rd, z: (i, 0, 0))],
            out_specs=pl.BlockSpec(memory_space=pl.ANY),
            scratch_shapes=[pltpu.VMEM((MOE_BLK, ROW_TILE, LANES), BF16),
                            pltpu.VMEM((TM * TOP_K, ROW_TILE, LANES), BF16),
                            pltpu.SemaphoreType.DMA(()), pltpu.SemaphoreType.DMA(())],
        ),
        out_shape=jax.ShapeDtypeStruct((N_SORT_ROWS, ROW_TILE, LANES), BF16),
        compiler_params=_cparams(("arbitrary",)),
        name="moe_dispatch",
    )(*dplan, h_all)


def _experts_kernel(layer, be_ref, first_ref, par_ref, nxt_ref, nv_ref,
                    x_ref, wg_hbm, wu_hbm, wd_hbm, y_ref,
                    wg_s, wu_s, wd_s, wg_f, wu_f, wd_f, wsem):
    i = pl.program_id(0)

    def fetch(e, slot):
        return (pltpu.make_async_copy(wg_hbm.at[layer, e], wg_f.at[slot], wsem.at[slot]),
                pltpu.make_async_copy(wu_hbm.at[layer, e], wu_f.at[slot], wsem.at[slot]),
                pltpu.make_async_copy(wd_hbm.at[layer, e], wd_f.at[slot], wsem.at[slot]))

    @pl.when(i < nv_ref[0])
    def _():
        e = be_ref[i]
        slot = par_ref[i]

        @pl.when(i == 0)
        def _():
            for cp in fetch(e, slot):
                cp.start()

        @pl.when(first_ref[i] == 1)
        def _():
            for cp in fetch(e, slot):
                cp.wait()
            wg_s[...] = wg_f[slot].astype(BF16)
            wu_s[...] = wu_f[slot].astype(BF16)
            wd_s[...] = wd_f[slot].astype(BF16)
            nxt = nxt_ref[i]

            @pl.when(nxt >= 0)
            def _():
                for cp in fetch(nxt, 1 - slot):
                    cp.start()

        xb = x_ref[...].reshape(MOE_BLK, D_MODEL)
        a = jax.nn.silu(_dot(xb, wg_s[...])) * _dot(xb, wu_s[...])
        y_ref[...] = _dot(a.astype(BF16), wd_s[...]).reshape(y_ref.shape)

    @pl.when(i >= nv_ref[0])
    def _():
        y_ref[...] = jnp.zeros(y_ref.shape, y_ref.dtype)


def _experts(block_e, first, parity, nxt, n_valid, xs, w_gate, w_up, w_down, layer):
    def blk(i, be, fi, pa, nx, nv):
        return (jnp.maximum(jnp.minimum(i, nv[0] - 1), 0), 0, 0)

    anyspec = pl.BlockSpec(memory_space=pl.ANY)
    return pl.pallas_call(
        functools.partial(_experts_kernel, layer),
        grid_spec=pltpu.PrefetchScalarGridSpec(
            num_scalar_prefetch=5,
            grid=(N_MOE_BLOCKS,),
            in_specs=[pl.BlockSpec((MOE_BLK, ROW_TILE, LANES), blk), anyspec, anyspec, anyspec],
            out_specs=pl.BlockSpec((MOE_BLK, ROW_TILE, LANES), lambda i, be, fi, pa, nx, nv: (i, 0, 0)),
            scratch_shapes=[pltpu.VMEM((D_MODEL, D_EXPERT), BF16), pltpu.VMEM((D_MODEL, D_EXPERT), BF16),
                            pltpu.VMEM((D_EXPERT, D_MODEL), BF16),
                            pltpu.VMEM((2, D_MODEL, D_EXPERT), F32), pltpu.VMEM((2, D_MODEL, D_EXPERT), F32),
                            pltpu.VMEM((2, D_EXPERT, D_MODEL), F32), pltpu.SemaphoreType.DMA((2,))],
        ),
        out_shape=jax.ShapeDtypeStruct((N_SORT_ROWS, ROW_TILE, LANES), F32),
        compiler_params=_cparams(("arbitrary",)),
        name="moe_experts",
    )(block_e, first, parity, nxt, n_valid, xs, w_gate, w_up, w_down)


def _gather_rows(dest_ref, ys_ref, ybuf, sem, i):
    base = i * TM

    def body(r, carry):
        for kk in range(TOP_K):
            d = dest_ref[kk * T_ALL + base + r]
            pltpu.make_async_copy(ys_ref.at[d], ybuf.at[kk, r], sem).start(priority=kk)
        return carry

    lax.fori_loop(0, TM, body, 0)
    for kk in range(TOP_K):
        pltpu.make_async_copy(ys_ref.at[pl.ds(0, TM)], ybuf.at[kk], sem).wait()


def _combined(x_ref, rg_ref, ybuf):
    rg = rg_ref[...]
    y0 = ybuf[0].reshape(TM, D_MODEL)
    y1 = ybuf[1].reshape(TM, D_MODEL)
    return x_ref[...] + rg[:, 0:1] * y0 + rg[:, 1:2] * y1


def _combine_kernel(dest_ref, x_ref, rg_ref, ys_ref, o_ref, ybuf, sem):
    _gather_rows(dest_ref, ys_ref, ybuf, sem, pl.program_id(0))
    o_ref[...] = _combined(x_ref, rg_ref, ybuf)


def _combine(dest, x_all, rg_all, ys):
    return pl.pallas_call(
        _combine_kernel,
        grid_spec=pltpu.PrefetchScalarGridSpec(
            num_scalar_prefetch=1,
            grid=(N_ROW_BLOCKS,),
            in_specs=[pl.BlockSpec((TM, D_MODEL), lambda i, d: (i, 0)),
                      pl.BlockSpec((TM, LANES), lambda i, d: (i, 0)),
                      pl.BlockSpec(memory_space=pl.ANY)],
            out_specs=pl.BlockSpec((TM, D_MODEL), lambda i, d: (i, 0)),
            scratch_shapes=[pltpu.VMEM((TOP_K, TM, ROW_TILE, LANES), F32), pltpu.SemaphoreType.DMA(())],
        ),
        out_shape=jax.ShapeDtypeStruct((T_ALL, D_MODEL), F32),
        compiler_params=_cparams(("arbitrary",)),
        name="moe_combine",
    )(dest, x_all, rg_all, ys)


def _final_kernel(dest_ref, x_ref, rg_ref, ys_ref, nfin_ref, op_ref, os_ref, ybuf, sem):
    i = pl.program_id(0)
    _gather_rows(dest_ref, ys_ref, ybuf, sem, i)
    y = _rms(_combined(x_ref, rg_ref, ybuf), nfin_ref[...])

    @pl.when(i < N_PROMPT_BLOCKS)
    def _():
        op_ref[...] = y

    @pl.when(i >= N_PROMPT_BLOCKS)
    def _():
        os_ref[...] = y


def _final(dest, x_all, rg_all, ys, nfin):
    return pl.pallas_call(
        _final_kernel,
        grid_spec=pltpu.PrefetchScalarGridSpec(
            num_scalar_prefetch=1,
            grid=(N_ROW_BLOCKS,),
            in_specs=[pl.BlockSpec((TM, D_MODEL), lambda i, d: (i, 0)),
                      pl.BlockSpec((TM, LANES), lambda i, d: (i, 0)),
                      pl.BlockSpec(memory_space=pl.ANY),
                      pl.BlockSpec((1, D_MODEL), lambda i, d: (0, 0))],
            out_specs=[pl.BlockSpec((TM, D_MODEL), lambda i, d: (jnp.minimum(i, N_PROMPT_BLOCKS - 1), 0)),
                       pl.BlockSpec((TM, D_MODEL), lambda i, d: (0, 0))],
            scratch_shapes=[pltpu.VMEM((TOP_K, TM, ROW_TILE, LANES), F32), pltpu.SemaphoreType.DMA(())],
        ),
        out_shape=[jax.ShapeDtypeStruct((T_PROMPT, D_MODEL), F32), jax.ShapeDtypeStruct((T_SAMPLE, D_MODEL), F32)],
        compiler_params=_cparams(("arbitrary",)),
        name="moe_combine_final",
    )(dest, x_all, rg_all, ys, nfin)


def _moe(h_all, rt_all, cnt, tcnt, w_gate, w_up, w_down, layer):
    dest, plan, dplan = _moe_metadata(rt_all, cnt, tcnt)
    xs = _dispatch(dplan, h_all)
    ys = _experts(*plan, xs, w_gate, w_up, w_down, layer)
    return dest, ys


def _pool_project(d_groups, wp_ref, scale):
    outs = [_dot(d_groups[g].astype(BF16), wp_ref[g]) for g in range(len(POOL_SIZES))]
    return jnp.concatenate(outs, axis=1) * scale


def _mix1_prompt_kernel(x_ref, nm_ref, wp_ref, sc_ref, nf_ref, wr_ref, br_ref,
                        x3_ref, h_ref, ri_ref, rg_ref, tc_ref, pl_ref, cnt_ref, ext):
    i = pl.program_id(0)

    @pl.when(i == 0)
    def _():
        cnt_ref[...] = jnp.zeros_like(cnt_ref)

    x = x_ref[...]
    hp = _rms(x, nm_ref[...])

    @pl.when(i % STEPS_PER_BATCH == 0)
    def _():
        ext[0:POOL_MAX, :] = jnp.zeros((POOL_MAX, D_MODEL), F32)

    ext[POOL_MAX:, :] = hp
    pos = (i % STEPS_PER_BATCH) * TM + lax.broadcasted_iota(jnp.int32, (TM, 1), 0)
    d_groups = []
    for g, w in enumerate(POOL_SIZES):
        cols = slice(g * POOL_GROUP_DIM, (g + 1) * POOL_GROUP_DIM)
        acc = ext[:, cols]
        span = 1
        while span < w:
            acc = acc + pltpu.roll(acc, span, 0)
            span *= 2
        cnt = jnp.minimum(pos + 1, w).astype(F32)
        d_groups.append(acc[POOL_MAX:] / cnt - hp[:, cols])
    tail = hp[TM - POOL_MAX:, :]
    ext[0:POOL_MAX, :] = tail
    pl_ref[...] = tail

    x3 = x + _pool_project(d_groups, wp_ref, sc_ref[...])
    x3_ref[...] = x3
    h, ids, gates = _route(x3, nf_ref[...], wr_ref[...], br_ref[...])
    h_ref[...] = h.reshape(h_ref.shape)
    ri_ref[...] = _rank_pack(ids, cnt_ref, tc_ref)
    rg_ref[...] = gates


def _mix1_prompt(x_all, nm, wp, sc, nf, wr, br):
    row_spec = pl.BlockSpec((TM, D_MODEL), lambda i: (i, 0))
    row3_spec = pl.BlockSpec((TM, ROW_TILE, LANES), lambda i: (i, 0, 0))
    lane_spec = pl.BlockSpec((TM, LANES), lambda i: (i, 0))
    return pl.pallas_call(
        _prompt_steps(_mix1_prompt_kernel, 7),
        grid=(N_ROW_BLOCKS,),
        in_specs=[row_spec, _const_spec((1, D_MODEL)),
                  _const_spec((len(POOL_SIZES), POOL_GROUP_DIM, POOL_GROUP_DIM)), _const_spec((1, D_MODEL)),
                  _const_spec((1, D_MODEL)), _const_spec((D_MODEL, 2 * LANES)), _const_spec((1, LANES))],
        out_specs=[row_spec, row3_spec, pl.BlockSpec((8, TM), lambda i: (0, i)), lane_spec,
                   pl.BlockSpec((None, 1, LANES), lambda i: (i, 0, 0)),
                   pl.BlockSpec((None, POOL_MAX, D_MODEL),
                                lambda i: (jnp.minimum(i // STEPS_PER_BATCH, BATCH - 1), 0, 0)),
                   _const_spec((1, LANES))],
        out_shape=[jax.ShapeDtypeStruct((T_ALL, D_MODEL), F32), jax.ShapeDtypeStruct((T_ALL, ROW_TILE, LANES), BF16),
                   jax.ShapeDtypeStruct((8, T_ALL), jnp.int32), jax.ShapeDtypeStruct((T_ALL, LANES), F32),
                   jax.ShapeDtypeStruct((N_ROW_BLOCKS, 1, LANES), F32),
                   jax.ShapeDtypeStruct((BATCH, POOL_MAX, D_MODEL), F32), jax.ShapeDtypeStruct((1, LANES), F32)],
        scratch_shapes=[pltpu.VMEM((POOL_MAX + TM, D_MODEL), F32)],
        compiler_params=_cparams(("arbitrary",)),
        name="mix1_prompt",
    )(x_all, nm, wp, sc, nf, wr, br)


def _mix1_sample_kernel(x_ref, st_ref, nm_ref, wp_ref, sc_ref, nf_ref, wr_ref, br_ref, cnt_in,
                        x3_in, h_in, ri_in, rg_in, tc_in,
                        x3_ref, h_ref, ri_ref, rg_ref, tc_ref, hs_ref, cnt_ref):
    del x3_in, h_in, ri_in, rg_in, tc_in
    x = x_ref[...]
    hs = _rms(x, nm_ref[...])
    hs_ref[...] = hs
    n_ctx = POOL_MAX - 1
    d_groups = []
    for g, w in enumerate(POOL_SIZES):
        cols = slice(g * POOL_GROUP_DIM, (g + 1) * POOL_GROUP_DIM)
        parts = []
        for t in range(DEC_SEQ):
            acc = hs[t * DEC_BATCH:(t + 1) * DEC_BATCH, cols]
            for back in range(1, w):
                src = t - back
                if src >= 0:
                    acc = acc + hs[src * DEC_BATCH:(src + 1) * DEC_BATCH, cols]
                else:
                    acc = acc + st_ref[n_ctx + src, :, cols]
            parts.append(acc / float(w) - hs[t * DEC_BATCH:(t + 1) * DEC_BATCH, cols])
        d_groups.append(jnp.concatenate(parts, axis=0))
    x3 = x + _pool_project(d_groups, wp_ref, sc_ref[...])
    x3_ref[...] = x3
    h, ids, gates = _route(x3, nf_ref[...], wr_ref[...], br_ref[...])
    h_ref[...] = h.reshape(h_ref.shape)
    cnt_ref[...] = cnt_in[...]
    ri_ref[...] = _rank_pack(ids, cnt_ref, tc_ref)
    rg_ref[...] = gates


def _mix1_sample(x_all, state_t, nm, wp, sc, nf, wr, br, cnt, x3_all, h_all, ri_all, rg_all, tc_all):
    sample_rows = pl.BlockSpec((TM, D_MODEL), lambda g: (N_PROMPT_BLOCKS, 0))
    sample_rows3 = pl.BlockSpec((TM, ROW_TILE, LANES), lambda g: (N_PROMPT_BLOCKS, 0, 0))
    sample_lanes = pl.BlockSpec((TM, LANES), lambda g: (N_PROMPT_BLOCKS, 0))
    anyspec = pl.BlockSpec(memory_space=pl.ANY)
    n_in = 9
    return pl.pallas_call(
        _mix1_sample_kernel,
        grid=(1,),
        in_specs=[sample_rows, _const_spec((POOL_MAX - 1, DEC_BATCH, D_MODEL)), _const_spec((1, D_MODEL)),
                  _const_spec((len(POOL_SIZES), POOL_GROUP_DIM, POOL_GROUP_DIM)), _const_spec((1, D_MODEL)),
                  _const_spec((1, D_MODEL)), _const_spec((D_MODEL, 2 * LANES)), _const_spec((1, LANES)),
                  _const_spec((1, LANES)), anyspec, anyspec, anyspec, anyspec, anyspec],
        out_specs=[sample_rows, sample_rows3, pl.BlockSpec((8, TM), lambda g: (0, N_PROMPT_BLOCKS)), sample_lanes,
                   pl.BlockSpec((None, 1, LANES), lambda g: (N_PROMPT_BLOCKS, 0, 0)),
                   _const_spec((T_SAMPLE, D_MODEL)), _const_spec((1, LANES))],
        out_shape=[jax.ShapeDtypeStruct((T_ALL, D_MODEL), F32), jax.ShapeDtypeStruct((T_ALL, ROW_TILE, LANES), BF16),
                   jax.ShapeDtypeStruct((8, T_ALL), jnp.int32), jax.ShapeDtypeStruct((T_ALL, LANES), F32),
                   jax.ShapeDtypeStruct((N_ROW_BLOCKS, 1, LANES), F32),
                   jax.ShapeDtypeStruct((T_SAMPLE, D_MODEL), F32), jax.ShapeDtypeStruct((1, LANES), F32)],
        input_output_aliases={n_in: 0, n_in + 1: 1, n_in + 2: 2, n_in + 3: 3, n_in + 4: 4},
        compiler_params=_cparams(("arbitrary",)),
        name="mix1_sample",
    )(x_all, state_t, nm, wp, sc, nf, wr, br, cnt, x3_all, h_all, ri_all, rg_all, tc_all)


def _router_weights(wg, bg, we, be):
    w = jnp.concatenate([wg, jnp.transpose(we, (1, 0, 2)).reshape(D_MODEL, N_EXPERTS)], axis=1)
    b = jnp.concatenate([bg, be.reshape(N_EXPERTS)])
    pad = LANES - N_GROUPS - N_EXPERTS
    w = jnp.pad(w, ((0, 0), (0, pad)))
    w_hi = w.astype(BF16)
    w_lo = (w - w_hi.astype(F32)).astype(BF16)
    return jnp.concatenate([w_hi, w_lo], axis=1), jnp.pad(b, (0, pad)).reshape(1, LANES)


def _stack(tab):
    return jnp.stack([jnp.concatenate([tab[h] for h in heads], axis=0) for heads in STACK_HEADS])


def kernel(x_prompt, x_sample, cache_k_win, cache_v_win, state_pool, norm_mix, norm_ffn, norm_final, w_in,
           a_ln_g, a_ln_b, a_w_s, a_b_s, b_sinks, rel_bias_table, w_out, c_w_pool, c_scale,
           router_group_w, router_group_b, router_expert_w, router_expert_b, w_gate, w_up, w_down):
    xs_t = jnp.transpose(x_sample, (1, 0, 2)).reshape(T_SAMPLE, D_MODEL)
    xp2 = x_prompt.reshape(T_PROMPT, D_MODEL)
    win =w_in[0].astype(BF16)
    wout = w_out[0].astype(BF16)
    lng = a_ln_g[0].reshape(1, A_WIDTH)
    lnb = a_ln_b[0].reshape(1, A_WIDTH)
    bias_p, bias_sc, bias_sn, ws_tril = _prep(rel_bias_table, a_w_s[0])
    wsp = ws_tril.reshape(A_HEADS // 2, 2, CHUNK, CHUNK).transpose(0, 2, 1, 3).reshape(A_HEADS // 2, CHUNK, 2 * CHUNK)
    bs_full = jnp.repeat(a_b_s[0].T, A_HEAD_DIM, axis=1)
    bias_p = jnp.stack([_stack(bias_p[0]), _stack(bias_p[1])])
    bias_sc = _stack(bias_sc)
    bias_sn = _stack(bias_sn)
    sinks = b_sinks[0]
    sink_p = jnp.stack([jnp.repeat(sinks[jnp.array(hh)], WINDOW) for hh in STACK_HEADS])
    sink_s = jnp.stack([jnp.repeat(sinks[jnp.array(hh)], 32) for hh in STACK_HEADS])
    bias_p = bias_p.at[:, :, :, 0].set(jnp.broadcast_to(sink_p[None], (2, 2, 4 * WINDOW)))
    bias_sc = bias_sc.at[:, :, 0].set(sink_s)
    pairs = [(t, s) for t in range(DEC_SEQ) for s in range(t + 1)]
    wcoef = jnp.stack([jnp.repeat(a_w_s[0][:, t, s], A_HEAD_DIM) for t, s in pairs])
    wcoef = jnp.pad(wcoef, ((0, 16 - len(pairs)), (0, 0)))
    bcoef = jnp.pad(jnp.repeat(a_b_s[0][:, :DEC_SEQ].T, A_HEAD_DIM, axis=1), ((0, 8 - DEC_SEQ), (0, 0)))
    ck = cache_k_win
    cv = cache_v_win
    routers = [_router_weights(router_group_w[l], router_group_b[l], router_expert_w[l], router_expert_b[l])
               for l in range(2)]
    nm = [norm_mix[l].reshape(1, D_MODEL) for l in range(2)]
    nf = [norm_ffn[l].reshape(1, D_MODEL) for l in range(2)]

    x1_all, h_all, ri_all, rg_all, tc_all, k_last, v_last, va_last, cnt0 = _mix0_prompt(
        xp2, nm[0], win, lng, lnb, wsp, bs_full, bias_p, wout, nf[0], *routers[0])
    x1_all, h_all, ri_all, rg_all, tc_all, k_new, v_new, va_s, cnt0 = _mix0_sample(
        xs_t, nm[0], win, lng, lnb, wcoef, bcoef, ck, cv, bias_sc, bias_sn, wout, nf[0], *routers[0], cnt0,
        x1_all, h_all, ri_all, rg_all, tc_all)
    dest0, ys0 = _moe(h_all, ri_all, cnt0, tc_all, w_gate, w_up, w_down, 0)
    x2_all = _combine(dest0, x1_all, rg_all, ys0)

    wp = c_w_pool[0].astype(BF16)
    sc = c_scale[0].reshape(1, D_MODEL)
    x3_all, h2_all, ri2_all, rg2_all, tc2_all, pool_tail, cnt1 = _mix1_prompt(
        x2_all, nm[1], wp, sc, nf[1], *routers[1])
    state_t = jnp.transpose(state_pool[0], (1, 0, 2))
    x3_all, h2_all, ri2_all, rg2_all, tc2_all, hs1, cnt1 = _mix1_sample(
        x2_all, state_t, nm[1], wp, sc, nf[1], *routers[1], cnt1, x3_all, h2_all, ri2_all, rg2_all, tc2_all)
    dest1, ys1 = _moe(h2_all, ri2_all, cnt1, tc2_all, w_gate, w_up, w_down, 1)
    y_p, y_s = _final(dest1, x3_all, rg2_all, ys1, norm_final.reshape(1, D_MODEL))

    def from_tmajor(a, width):
        return jnp.transpose(a.reshape(DEC_SEQ, DEC_BATCH, width), (1, 0, 2))

    y_prompt = y_p.reshape(BATCH, SEQ, D_MODEL)
    y_sample = from_tmajor(y_s, D_MODEL)
    win_k_p = k_last.reshape(1, BATCH, WINDOW, B_KV_HEADS, B_HEAD_DIM)
    win_v_p = v_last.reshape(1, BATCH, WINDOW, B_KV_HEADS, B_HEAD_DIM)
    kn = from_tmajor(k_new, KV_WIDTH).reshape(DEC_BATCH, DEC_SEQ, B_KV_HEADS, B_HEAD_DIM)
    vn = from_tmajor(v_new, KV_WIDTH).reshape(DEC_BATCH, DEC_SEQ, B_KV_HEADS, B_HEAD_DIM)
    win_k_s = jnp.concatenate([cache_k_win[0][:, DEC_SEQ:], kn], axis=1)[None]
    win_v_s = jnp.concatenate([cache_v_win[0][:, DEC_SEQ:], vn], axis=1)[None]
    chunk_v_p = va_last.reshape(1, BATCH, CHUNK, A_HEADS, A_HEAD_DIM)
    chunk_v_s = from_tmajor(va_s, A_WIDTH).reshape(1, DEC_BATCH, DEC_SEQ, A_HEADS, A_HEAD_DIM)
    pool_p = pool_tail[:, 1:][None]
    pool_s = jnp.concatenate([state_pool[0][:, DEC_SEQ:], from_tmajor(hs1, D_MODEL)], axis=1)[None]
    return (y_prompt, y_sample, win_k_p, win_v_p, win_k_s, win_v_s, chunk_v_p, chunk_v_s, pool_p, pool_s)
```

```python
import functools
import math

import numpy as np
import jax
import jax.numpy as jnp
from jax import lax
from jax.experimental import pallas as pl
from jax.experimental.pallas import tpu as pltpu

F32 = jnp.float32
BF16 = jnp.bfloat16

D_MODEL = 1024
BATCH = 2
SEQ = 8192
DEC_BATCH = 128
DEC_SEQ = 4
A_WIDTH = 512
A_HEADS = 8
A_HEAD_DIM = 64
CHUNK = 128
B_HEADS = 8
B_KV_HEADS = 2
B_HEAD_DIM = 64
B_GROUP = 4
WINDOW = 128
N_BUCKETS = 32
MAX_DISTANCE = WINDOW
Q_WIDTH = 512
KV_WIDTH = 128
IN_WIDTH = 2 * A_WIDTH + Q_WIDTH + 2 * KV_WIDTH
ATTN_SCALE = B_HEAD_DIM ** -0.5
NEG_INF = -1e30
POOL_SIZES = (2, 4, 8, 16)
POOL_GROUP_DIM = 256
POOL_MAX = 16
N_GROUPS = 4
EXPERTS_PER_GROUP = 8
N_EXPERTS = 32
TOP_K = 2
D_EXPERT = 512
EPS = 1e-6

LANES = 128
ROW_TILE = D_MODEL // LANES
T_PROMPT = BATCH * SEQ
T_SAMPLE = DEC_BATCH * DEC_SEQ
T_ALL = T_PROMPT + T_SAMPLE
TM = 512
N_PROMPT_BLOCKS = T_PROMPT // TM
N_ROW_BLOCKS = T_ALL // TM
STEPS_PER_BATCH = SEQ // TM
SUB = TM // WINDOW
N_SLOTS = T_ALL * TOP_K
MOE_BLK = 512
N_MOE_BLOCKS = N_SLOTS // MOE_BLK + N_EXPERTS
N_SORT_ROWS = N_MOE_BLOCKS * MOE_BLK
SAMPLE_GROUP = 8
N_SAMPLE_GROUPS = DEC_BATCH // SAMPLE_GROUP
VMEM_LIMIT = 56 * 1024 * 1024

STACK_HEADS = ((0, 2, 5, 7), (1, 3, 4, 6))


def _t5_bucket_np(dist):
    n = np.maximum(dist, 0)
    max_exact = N_BUCKETS // 2
    nf = np.maximum(n, 1).astype(np.float32)
    large = max_exact + (np.log(nf / np.float32(max_exact)) / np.float32(math.log(MAX_DISTANCE / max_exact))
                         * np.float32(N_BUCKETS - max_exact)).astype(np.int32)
    large = np.minimum(large, N_BUCKETS - 1)
    return np.where(n < max_exact, n, large).astype(np.int32)


def _bucket_tables():
    qi = np.arange(WINDOW)[:, None]
    ki = np.arange(2 * WINDOW)[None, :]
    dist = qi + WINDOW - ki
    valid = (dist >= 0) & (dist < WINDOW)
    bp = np.where(valid, _t5_bucket_np(dist), -1)
    bp_first = np.where(ki >= WINDOW, bp, -1)
    bkt_p = np.stack([bp_first, bp]).astype(np.int32)

    t = np.repeat(np.arange(DEC_SEQ), SAMPLE_GROUP)[:, None]
    b = np.tile(np.arange(SAMPLE_GROUP), DEC_SEQ)[:, None]
    cb = np.repeat(np.arange(SAMPLE_GROUP), WINDOW)[None, :]
    cj = np.tile(np.arange(WINDOW), SAMPLE_GROUP)[None, :]
    dist_c = t + WINDOW - cj
    valid_c = (cb == b) & (dist_c >= 0) & (dist_c < WINDOW)
    bkt_sc = np.where(valid_c, _t5_bucket_np(dist_c), -1).astype(np.int32)
    nt = np.repeat(np.arange(DEC_SEQ), SAMPLE_GROUP)[None, :]
    nb = np.tile(np.arange(SAMPLE_GROUP), DEC_SEQ)[None, :]
    dist_n = t - nt
    valid_n = (nb == b) & (dist_n >= 0)
    bkt_sn = np.where(valid_n, _t5_bucket_np(dist_n), -1).astype(np.int32)
    bkt_sn = np.concatenate([bkt_sn, np.full((32, LANES - 32), -1, np.int32)], axis=1)
    return bkt_p, bkt_sc, bkt_sn


_BKT_P, _BKT_SC, _BKT_SN = _bucket_tables()


def _cparams(semantics):
    return pltpu.CompilerParams(dimension_semantics=semantics, vmem_limit_bytes=VMEM_LIMIT)


def _rms(x, g):
    return x * lax.rsqrt(jnp.mean(x * x, axis=-1, keepdims=True) + EPS) * g


def _layernorm(x, g, b):
    xc = x - jnp.mean(x, axis=-1, keepdims=True)
    return xc * lax.rsqrt(jnp.mean(xc * xc, axis=-1, keepdims=True) + EPS) * g + b


def _dot(a, b):
    return jnp.dot(a, b, preferred_element_type=F32)


def _dot_nt(a, b):
    return lax.dot_general(a, b, (((1,), (1,)), ((), ())), preferred_element_type=F32)


def _project(x, nm, win, lng, lnb):
    h = _rms(x, nm)
    z = _dot(h.astype(BF16), win)
    u = jax.nn.gelu(z[:, :A_WIDTH])
    va = _layernorm(jax.nn.gelu(z[:, A_WIDTH:2 * A_WIDTH]), lng, lnb)
    q = z[:, 2 * A_WIDTH:2 * A_WIDTH + Q_WIDTH] * ATTN_SCALE
    k = z[:, 2 * A_WIDTH + Q_WIDTH:2 * A_WIDTH + Q_WIDTH + KV_WIDTH]
    v = z[:, 2 * A_WIDTH + Q_WIDTH + KV_WIDTH:]
    return u, va, q, k, v


def _route(x1, nf, wr, br):
    hf = _rms(x1, nf)
    h = hf.astype(BF16)
    h_lo = (hf - h.astype(F32)).astype(BF16)
    part = _dot(h, wr)
    logits = part[:, :LANES] + part[:, LANES:] + _dot(h_lo, wr[:, :LANES]) + br
    rows = logits.shape[0]
    lane = lax.broadcasted_iota(jnp.int32, (rows, LANES), 1)
    lanef = lane.astype(F32)
    big = jnp.float32(1e9)
    is_g = lane < N_GROUPS
    gl = jnp.where(is_g, logits, -jnp.inf)
    gmax = jnp.max(gl, axis=1, keepdims=True)
    gsel = jnp.min(jnp.where(gl == gmax, lanef, big), axis=1, keepdims=True)
    gsum = jnp.sum(jnp.where(is_g, jnp.exp(logits - gmax), 0.0), axis=1, keepdims=True)
    g1 = 1.0 / gsum
    lo = N_GROUPS + EXPERTS_PER_GROUP * gsel
    emask = (lanef >= lo) & (lanef < lo + EXPERTS_PER_GROUP)
    el = jnp.where(emask, logits, -jnp.inf)
    v1 = jnp.max(el, axis=1, keepdims=True)
    i1 = jnp.min(jnp.where(el == v1, lanef, big), axis=1, keepdims=True)
    el2 = jnp.where(lanef == i1, -jnp.inf, el)
    v2 = jnp.max(el2, axis=1, keepdims=True)
    i2 = jnp.min(jnp.where(el2 == v2, lanef, big), axis=1, keepdims=True)
    e2 = jnp.exp(v2 - v1)
    den = 1.0 + e2
    w1 = g1 / den
    w2 = g1 * e2 / den
    ids = jnp.where(lane == 0, i1 - N_GROUPS, jnp.where(lane == 1, i2 - N_GROUPS, 0.0)).astype(jnp.int32)
    gates = jnp.where(lane == 0, w1, jnp.where(lane == 1, w2, 0.0))
    return h, ids, gates


def _rank_pack(ids, cnt_ref, tcnt_ref):
    rows = ids.shape[0]
    lane = lax.broadcasted_iota(jnp.int32, (rows, LANES), 1)
    o0 = (lane == ids[:, 0:1]).astype(F32)
    o1 = (lane == ids[:, 1:2]).astype(F32)
    r = lax.broadcasted_iota(jnp.int32, (rows, rows), 0)
    c = lax.broadcasted_iota(jnp.int32, (rows, rows), 1)
    before = (c < r).astype(BF16)
    p01 = _dot(before, jnp.concatenate([o0, o1], axis=1).astype(BF16))
    p0 = p01[:, :LANES]
    p1 = p01[:, LANES:]
    c0 = jnp.sum(o0, axis=0, keepdims=True)
    c1 = jnp.sum(o1, axis=0, keepdims=True)
    ctile = c0 + c1
    cnt_ref[...] = cnt_ref[...] + ctile
    tcnt_ref[...] = ctile
    inc = jnp.broadcast_to(ctile, (8, LANES))
    lane8 = lax.broadcasted_iota(jnp.int32, (8, LANES), 1)
    for sh in (1, 2, 4, 8, 16, 32, 64):
        inc = inc + jnp.where(lane8 >= sh, pltpu.roll(inc, sh, 1), 0.0)
    start = inc[0:1] - ctile
    lpos0 = jnp.sum(o0 * (start + p0), axis=1, keepdims=True)
    lpos1 = jnp.sum(o1 * (start + c0 + p1), axis=1, keepdims=True)
    idf = ids.astype(F32)
    packed = jnp.where(lane < TOP_K, idf, 0.0)
    for ln, col in ((4, lpos0), (5, lpos1)):
        packed = jnp.where(lane == ln, col, packed)
    return jnp.transpose(packed)[:8].astype(jnp.int32)


def _prep_kernel(tab_ref, bp_ref, bsc_ref, bsn_ref, ws_ref, op_ref, osc_ref, osn_ref, ows_ref):
    def fill(bkt, write):
        for h in range(B_HEADS):
            acc = jnp.full(bkt.shape, NEG_INF, F32)
            for b in range(N_BUCKETS):
                acc = jnp.where(bkt == b, tab_ref[b, h], acc)
            write(h, acc)

    for var in range(2):
        def wr_p(h, acc, var=var):
            op_ref[var, h] = acc
        fill(bp_ref[var], wr_p)

    def wr_sc(h, acc):
        osc_ref[h] = acc
    fill(bsc_ref[...], wr_sc)

    def wr_sn(h, acc):
        osn_ref[h] = acc
    fill(bsn_ref[...], wr_sn)

    r = lax.broadcasted_iota(jnp.int32, (CHUNK, CHUNK), 0)
    c = lax.broadcasted_iota(jnp.int32, (CHUNK, CHUNK), 1)
    for h in range(A_HEADS):
        ows_ref[h] = jnp.where(r >= c, ws_ref[h], 0.0).astype(BF16)


def _prep(rel_bias_table, w_s):
    vm = pl.BlockSpec(memory_space=pltpu.VMEM)
    return pl.pallas_call(
        _prep_kernel,
        in_specs=[pl.BlockSpec(memory_space=pltpu.SMEM), vm, vm, vm, vm],
        out_specs=[vm, vm, vm, vm],
        out_shape=[
            jax.ShapeDtypeStruct((2, B_HEADS, WINDOW, 2 * WINDOW), F32),
            jax.ShapeDtypeStruct((B_HEADS, 32, SAMPLE_GROUP * WINDOW), F32),
            jax.ShapeDtypeStruct((B_HEADS, 32, LANES), F32),
            jax.ShapeDtypeStruct((A_HEADS, CHUNK, CHUNK), BF16),
        ],
        name="prep_tables",
    )(rel_bias_table, jnp.asarray(_BKT_P), jnp.asarray(_BKT_SC), jnp.asarray(_BKT_SN), w_s)


def _gate_pairs(va_rows, wsp_ref, lane_lo):
    outs = []
    for p in range(A_HEADS // 2):
        vp = va_rows[:, p * LANES:(p + 1) * LANES]
        rhs = jnp.concatenate([jnp.where(lane_lo, vp, 0.0), jnp.where(lane_lo, 0.0, vp)], axis=0).astype(BF16)
        outs.append(_dot(wsp_ref[p], rhs))
    return jnp.concatenate(outs, axis=1)


def _prompt_steps(body, first_row_out):
    def kern(*refs):
        i = pl.program_id(0)

        @pl.when(i < N_PROMPT_BLOCKS)
        def _():
            body(*refs)

        @pl.when(i >= N_PROMPT_BLOCKS)
        def _():
            for r in refs[first_row_out:first_row_out + 5]:
                r[...] = jnp.zeros(r.shape, r.dtype)

    return kern


def _mix0_prompt_kernel(x_ref, nm_ref, win_ref, lng_ref, lnb_ref, wsp_ref, bs_ref, bias_ref,
                        wout_ref, nf_ref, wr_ref, br_ref,
                        x1_ref, h_ref, ri_ref, rg_ref, tc_ref, kl_ref, vl_ref, val_ref, cnt_ref,
                        kprev, vprev, mix_scr):
    @pl.when(pl.program_id(0) == 0)
    def _():
        cnt_ref[...] = jnp.zeros_like(cnt_ref)

    x = x_ref[...]
    u, va, q, k, v = _project(x, nm_ref[...], win_ref[...], lng_ref[...], lnb_ref[...])
    lane_lo = lax.broadcasted_iota(jnp.int32, (WINDOW, LANES), 1) < B_HEAD_DIM
    row0 = lax.broadcasted_iota(jnp.int32, (WINDOW, KV_WIDTH), 0) == 0
    first = pl.program_id(0) % STEPS_PER_BATCH == 0

    @pl.when(first)
    def _():
        kprev[...] = jnp.zeros_like(kprev)
        vprev[...] = jnp.zeros_like(vprev)

    for j in range(SUB):
        rows = slice(j * WINDOW, (j + 1) * WINDOW)
        s_gate = _gate_pairs(va[rows], wsp_ref, lane_lo)
        mix_scr[rows, :A_WIDTH] = u[rows] * (s_gate + bs_ref[...])

        if j == 0:
            kp, vp = kprev[...], vprev[...]
        else:
            prows = slice((j - 1) * WINDOW, j * WINDOW)
            kp, vp = k[prows], v[prows]
        kk = jnp.concatenate([jnp.where(row0, 0.0, kp), k[rows]], axis=0)
        vv = jnp.concatenate([jnp.where(row0, 0.0, vp), v[rows]], axis=0)
        kops = (kk.astype(BF16), pltpu.roll(kk, B_HEAD_DIM, 1).astype(BF16))
        vops = (vv.astype(BF16), pltpu.roll(vv, B_HEAD_DIM, 1).astype(BF16))
        qt = [q[rows, p * LANES:(p + 1) * LANES] for p in range(4)]
        q_even = [jnp.where(lane_lo, t, 0.0) for t in qt]
        q_odd = [jnp.where(lane_lo, 0.0, t) for t in qt]
        stacks = (jnp.concatenate([q_even[0], q_even[1], q_odd[2], q_odd[3]], axis=0),
                  jnp.concatenate([q_odd[0], q_odd[1], q_even[2], q_even[3]], axis=0))
        o = []
        for st in range(2):
            s = _dot_nt(stacks[st].astype(BF16), kops[st])
            if j == 0:
                bias = bias_ref[jnp.where(first, 0, 1), st]
            else:
                bias = bias_ref[1, st]
            s = s + bias
            m = jnp.max(s, axis=-1, keepdims=True)
            p = jnp.exp(s - m)
            den = jnp.sum(p, axis=-1, keepdims=True)
            o.append(_dot(p.astype(BF16), vops[st]) / den)
        oa, ob = o
        sl = [slice(i * WINDOW, (i + 1) * WINDOW) for i in range(4)]
        tiles = (jnp.where(lane_lo, oa[sl[0]], ob[sl[0]]), jnp.where(lane_lo, oa[sl[1]], ob[sl[1]]),
                 jnp.where(lane_lo, ob[sl[2]], oa[sl[2]]), jnp.where(lane_lo, ob[sl[3]], oa[sl[3]]))
        for p in range(4):
            mix_scr[rows, A_WIDTH + p * LANES:A_WIDTH + (p + 1) * LANES] = tiles[p]

    last = slice(TM - WINDOW, TM)
    kprev[...] = k[last]
    vprev[...] = v[last]
    kl_ref[...] = k[last]
    vl_ref[...] = v[last]
    val_ref[...] = va[last]

    x1 = x + _dot(mix_scr[...].astype(BF16), wout_ref[...])
    x1_ref[...] = x1
    h, ids, gates = _route(x1, nf_ref[...], wr_ref[...], br_ref[...])
    h_ref[...] = h.reshape(h_ref.shape)
    ri_ref[...] = _rank_pack(ids, cnt_ref, tc_ref)
    rg_ref[...] = gates


def _const_spec(shape):
    nd = len(shape)
    return pl.BlockSpec(shape, lambda i, _n=nd: (0,) * _n)


def _mix0_prompt(x_all, nm, win, lng, lnb, wsp, bs_full, bias_p, wout, nf, wr, br):
    row_spec = pl.BlockSpec((TM, D_MODEL), lambda i: (i, 0))
    row3_spec = pl.BlockSpec((TM, ROW_TILE, LANES), lambda i: (i, 0, 0))
    lane_spec = pl.BlockSpec((TM, LANES), lambda i: (i, 0))
    last_kv = pl.BlockSpec((None, WINDOW, KV_WIDTH), lambda i: (jnp.minimum(i // STEPS_PER_BATCH, BATCH - 1), 0, 0))
    last_va = pl.BlockSpec((None, WINDOW, A_WIDTH), lambda i: (jnp.minimum(i // STEPS_PER_BATCH, BATCH - 1), 0, 0))
    return pl.pallas_call(
        _prompt_steps(_mix0_prompt_kernel, 12),
        grid=(N_ROW_BLOCKS,),
        in_specs=[pl.BlockSpec((TM, D_MODEL), lambda i: (jnp.minimum(i, N_PROMPT_BLOCKS - 1), 0)),
                  _const_spec((1, D_MODEL)), _const_spec((D_MODEL, IN_WIDTH)),
                  _const_spec((1, A_WIDTH)), _const_spec((1, A_WIDTH)),
                  _const_spec((A_HEADS // 2, CHUNK, 2 * CHUNK)), _const_spec((CHUNK, A_WIDTH)),
                  _const_spec((2, 2, 4 * WINDOW, 2 * WINDOW)),
                  _const_spec((A_WIDTH + Q_WIDTH, D_MODEL)), _const_spec((1, D_MODEL)),
                  _const_spec((D_MODEL, 2 * LANES)), _const_spec((1, LANES))],
        out_specs=[row_spec, row3_spec, pl.BlockSpec((8, TM), lambda i: (0, i)), lane_spec,
                   pl.BlockSpec((None, 1, LANES), lambda i: (i, 0, 0)),
                   last_kv, last_kv, last_va, _const_spec((1, LANES))],
        out_shape=[jax.ShapeDtypeStruct((T_ALL, D_MODEL), F32), jax.ShapeDtypeStruct((T_ALL, ROW_TILE, LANES), BF16),
                   jax.ShapeDtypeStruct((8, T_ALL), jnp.int32), jax.ShapeDtypeStruct((T_ALL, LANES), F32),
                   jax.ShapeDtypeStruct((N_ROW_BLOCKS, 1, LANES), F32),
                   jax.ShapeDtypeStruct((BATCH, WINDOW, KV_WIDTH), F32),
                   jax.ShapeDtypeStruct((BATCH, WINDOW, KV_WIDTH), F32),
                   jax.ShapeDtypeStruct((BATCH, WINDOW, A_WIDTH), F32),
                   jax.ShapeDtypeStruct((1, LANES), F32)],
        scratch_shapes=[pltpu.VMEM((WINDOW, KV_WIDTH), F32), pltpu.VMEM((WINDOW, KV_WIDTH), F32),
                        pltpu.VMEM((TM, D_MODEL), F32)],
        compiler_params=_cparams(("arbitrary",)),
        name="mix0_prompt",
    )(x_all, nm, win, lng, lnb, wsp, bs_full, bias_p, wout, nf, wr, br)


def _mix0_sample_kernel(x_ref, nm_ref, win_ref, lng_ref, lnb_ref, wcoef_ref, bcoef_ref,
                        ck_ref, cv_ref, bsc_ref, bsn_ref,
                        wout_ref, nf_ref, wr_ref, br_ref, cnt_in,
                        x1_in, h_in, ri_in, rg_in, tc_in,
                        x1_ref, h_ref, ri_ref, rg_ref, tc_ref, kn_ref, vn_ref, va_ref, cnt_ref,
                        q_scr, k_scr, v_scr, mix_scr):
    del x1_in, h_in, ri_in, rg_in, tc_in
    g = pl.program_id(0)

    @pl.when(g == 0)
    def _():
        u, va, q, k, v = _project(x_ref[...], nm_ref[...], win_ref[...], lng_ref[...], lnb_ref[...])
        q_scr[...] = q
        k_scr[...] = k
        v_scr[...] = v
        kn_ref[...] = k
        vn_ref[...] = v
        va_ref[...] = va
        idx = 0
        for t in range(DEC_SEQ):
            acc = jnp.zeros((DEC_BATCH, A_WIDTH), F32) + bcoef_ref[t:t + 1, :]
            for s in range(t + 1):
                acc = acc + wcoef_ref[idx:idx + 1, :] * va[s * DEC_BATCH:(s + 1) * DEC_BATCH]
                idx += 1
            mix_scr[t * DEC_BATCH:(t + 1) * DEC_BATCH, :A_WIDTH] = u[t * DEC_BATCH:(t + 1) * DEC_BATCH] * acc

    b0 = pl.multiple_of(g * SAMPLE_GROUP, SAMPLE_GROUP)
    lane_lo = lax.broadcasted_iota(jnp.int32, (DEC_SEQ * SAMPLE_GROUP, LANES), 1) < B_HEAD_DIM

    def grab(ref, width):
        return jnp.concatenate([ref[pl.ds(t * DEC_BATCH + b0, SAMPLE_GROUP), :] for t in range(DEC_SEQ)], axis=0)

    qg = grab(q_scr, Q_WIDTH)
    kn = grab(k_scr, KV_WIDTH)
    vn = grab(v_scr, KV_WIDTH)
    crow0 = lax.broadcasted_iota(jnp.int32, (SAMPLE_GROUP * WINDOW, KV_WIDTH), 0) == 0
    rows_kv = (SAMPLE_GROUP * WINDOW, KV_WIDTH)
    kc = jnp.where(crow0, 0.0, ck_ref[...].reshape(rows_kv))
    vc = jnp.where(crow0, 0.0, cv_ref[...].reshape(rows_kv))
    kc_ops = (kc.astype(BF16), pltpu.roll(kc, B_HEAD_DIM, 1).astype(BF16))
    vc_ops = (vc.astype(BF16), pltpu.roll(vc, B_HEAD_DIM, 1).astype(BF16))
    kn_ops = (kn.astype(BF16), pltpu.roll(kn, B_HEAD_DIM, 1).astype(BF16))
    vn_ops = (vn.astype(BF16), pltpu.roll(vn, B_HEAD_DIM, 1).astype(BF16))
    qt = [qg[:, p * LANES:(p + 1) * LANES] for p in range(4)]
    q_even = [jnp.where(lane_lo, t, 0.0) for t in qt]
    q_odd = [jnp.where(lane_lo, 0.0, t) for t in qt]
    stacks = (jnp.concatenate([q_even[0], q_even[1], q_odd[2], q_odd[3]], axis=0),
              jnp.concatenate([q_odd[0], q_odd[1], q_even[2], q_even[3]], axis=0))
    o = []
    for st in range(2):
        qs = stacks[st].astype(BF16)
        sc = _dot_nt(qs, kc_ops[st]) + bsc_ref[st]
        sn = _dot_nt(qs, kn_ops[st]) + bsn_ref[st][:, :DEC_SEQ * SAMPLE_GROUP]
        m = jnp.maximum(jnp.max(sc, axis=-1, keepdims=True), jnp.max(sn, axis=-1, keepdims=True))
        pc = jnp.exp(sc - m)
        pn = jnp.exp(sn - m)
        den = jnp.sum(pc, axis=-1, keepdims=True) + jnp.sum(pn, axis=-1, keepdims=True)
        o.append((_dot(pc.astype(BF16), vc_ops[st]) + _dot(pn.astype(BF16), vn_ops[st])) / den)
    oa, ob = o
    n = DEC_SEQ * SAMPLE_GROUP
    sl = [slice(i * n, (i + 1) * n) for i in range(4)]
    tiles = (jnp.where(lane_lo, oa[sl[0]], ob[sl[0]]), jnp.where(lane_lo, oa[sl[1]], ob[sl[1]]),
             jnp.where(lane_lo, ob[sl[2]], oa[sl[2]]), jnp.where(lane_lo, ob[sl[3]], oa[sl[3]]))
    for p in range(4):
        for t in range(DEC_SEQ):
            mix_scr[pl.ds(t * DEC_BATCH + b0, SAMPLE_GROUP), A_WIDTH + p * LANES:A_WIDTH + (p + 1) * LANES] = (
                tiles[p][t * SAMPLE_GROUP:(t + 1) * SAMPLE_GROUP])

    @pl.when(g == N_SAMPLE_GROUPS - 1)
    def _():
        x1 = x_ref[...] + _dot(mix_scr[...].astype(BF16), wout_ref[...])
        x1_ref[...] = x1
        h, ids, gates = _route(x1, nf_ref[...], wr_ref[...], br_ref[...])
        h_ref[...] = h.reshape(h_ref.shape)
        cnt_ref[...] = cnt_in[...]
        ri_ref[...] = _rank_pack(ids, cnt_ref, tc_ref)
        rg_ref[...] = gates


def _mix0_sample(x_all, nm, win, lng, lnb, wcoef, bcoef, ck, cv, bias_sc, bias_sn, wout, nf, wr, br, cnt,
                 x1_all, h_all, ri_all, rg_all, tc_all):
    sample_rows = pl.BlockSpec((TM, D_MODEL), lambda g: (N_PROMPT_BLOCKS, 0))
    sample_rows3 = pl.BlockSpec((TM, ROW_TILE, LANES), lambda g: (N_PROMPT_BLOCKS, 0, 0))
    sample_lanes = pl.BlockSpec((TM, LANES), lambda g: (N_PROMPT_BLOCKS, 0))
    cache_spec = pl.BlockSpec((None, SAMPLE_GROUP, WINDOW, B_KV_HEADS, B_HEAD_DIM), lambda g: (0, g, 0, 0, 0))
    anyspec = pl.BlockSpec(memory_space=pl.ANY)
    n_in = 16
    return pl.pallas_call(
        _mix0_sample_kernel,
        grid=(N_SAMPLE_GROUPS,),
        in_specs=[_const_spec((TM, D_MODEL)), _const_spec((1, D_MODEL)), _const_spec((D_MODEL, IN_WIDTH)),
                  _const_spec((1, A_WIDTH)), _const_spec((1, A_WIDTH)),
                  _const_spec((16, A_WIDTH)), _const_spec((8, A_WIDTH)),
                  cache_spec, cache_spec,
                  _const_spec((2, 4 * 32, SAMPLE_GROUP * WINDOW)), _const_spec((2, 4 * 32, LANES)),
                  _const_spec((A_WIDTH + Q_WIDTH, D_MODEL)), _const_spec((1, D_MODEL)),
                  _const_spec((D_MODEL, 2 * LANES)), _const_spec((1, LANES)), _const_spec((1, LANES)),
                  anyspec, anyspec, anyspec, anyspec, anyspec],
        out_specs=[sample_rows, sample_rows3, pl.BlockSpec((8, TM), lambda g: (0, N_PROMPT_BLOCKS)), sample_lanes,
                   pl.BlockSpec((None, 1, LANES), lambda g: (N_PROMPT_BLOCKS, 0, 0)),
                   _const_spec((T_SAMPLE, KV_WIDTH)), _const_spec((T_SAMPLE, KV_WIDTH)),
                   _const_spec((T_SAMPLE, A_WIDTH)), _const_spec((1, LANES))],
        out_shape=[jax.ShapeDtypeStruct((T_ALL, D_MODEL), F32), jax.ShapeDtypeStruct((T_ALL, ROW_TILE, LANES), BF16),
                   jax.ShapeDtypeStruct((8, T_ALL), jnp.int32), jax.ShapeDtypeStruct((T_ALL, LANES), F32),
                   jax.ShapeDtypeStruct((N_ROW_BLOCKS, 1, LANES), F32),
                   jax.ShapeDtypeStruct((T_SAMPLE, KV_WIDTH), F32), jax.ShapeDtypeStruct((T_SAMPLE, KV_WIDTH), F32),
                   jax.ShapeDtypeStruct((T_SAMPLE, A_WIDTH), F32), jax.ShapeDtypeStruct((1, LANES), F32)],
        scratch_shapes=[pltpu.VMEM((T_SAMPLE, Q_WIDTH), F32), pltpu.VMEM((T_SAMPLE, KV_WIDTH), F32),
                        pltpu.VMEM((T_SAMPLE, KV_WIDTH), F32), pltpu.VMEM((T_SAMPLE, D_MODEL), F32)],
        input_output_aliases={n_in: 0, n_in + 1: 1, n_in + 2: 2, n_in + 3: 3, n_in + 4: 4},
        compiler_params=_cparams(("arbitrary",)),
        name="mix0_sample",
    )(x_all, nm, win, lng, lnb, wcoef, bcoef, ck, cv, bias_sc, bias_sn, wout, nf, wr, br, cnt,
      x1_all, h_all, ri_all, rg_all, tc_all)


def _moe_metadata(rt_all, cnt, tcnt):
    counts = cnt[0, :N_EXPERTS].astype(jnp.int32)
    padded = (counts + MOE_BLK - 1) // MOE_BLK * MOE_BLK
    pad_end = jnp.cumsum(padded)
    pad_start = pad_end - padded
    experts = jnp.arange(N_EXPERTS, dtype=jnp.int32)
    n_valid = (pad_end[-1] // MOE_BLK).astype(jnp.int32).reshape(1)
    blk_start = jnp.arange(N_MOE_BLOCKS, dtype=jnp.int32) * MOE_BLK
    block_e = jnp.minimum(jnp.sum((blk_start[:, None] >= pad_end[None, :]).astype(jnp.int32), axis=1),
                          N_EXPERTS - 1).astype(jnp.int32)
    zero_start = (pad_start + counts).astype(jnp.int32)
    zero_len = (padded - counts).astype(jnp.int32)
    first = (blk_start == pad_start[block_e]).astype(jnp.int32)
    used = counts > 0
    parity = ((jnp.cumsum(used.astype(jnp.int32)) - 1) % 2)[block_e].astype(jnp.int32)
    nearest = lax.cummin(jnp.where(used, experts, N_EXPERTS)[::-1])[::-1]
    next_used = jnp.concatenate([nearest[1:], jnp.full((1,), N_EXPERTS, jnp.int32)])
    nxt = jnp.where(next_used < N_EXPERTS, next_used, -1)[block_e].astype(jnp.int32)
    plan = (block_e, first, parity, nxt, n_valid)
    runs = tcnt[:, 0, :N_EXPERTS].astype(jnp.int32)
    run_dst = pad_start[None, :] + jnp.cumsum(runs, axis=0) - runs
    lpos = rt_all[2 * TOP_K:3 * TOP_K].reshape(N_SLOTS).astype(jnp.int32)
    cplan = (lpos, runs.reshape(-1), run_dst.reshape(-1).astype(jnp.int32))
    dplan = cplan + (jnp.concatenate([zero_start, zero_len, n_valid]),)
    return plan, dplan, cplan


def _dispatch_kernel(lpos_ref, run_ref, rdst_ref, zs_ref, h_ref, xs_ref, zero_scr, stage, sem, zsem):
    i = pl.program_id(0)

    @pl.when(i == 0)
    def _():
        zero_scr[...] = jnp.zeros_like(zero_scr)

        def pieces(e, do):
            off = zs_ref[e]
            rem = zs_ref[N_EXPERTS + e]
            bit = MOE_BLK // 2
            while bit >= 1:
                take = (rem & bit) != 0

                @pl.when(take)
                def _(off=off, bit=bit):
                    do(pltpu.make_async_copy(zero_scr.at[pl.ds(0, bit)], xs_ref.at[pl.ds(off, bit)], zsem))

                off = off + jnp.where(take, bit, 0)
                bit //= 2

        def start_e(e, c):
            pieces(e, lambda cp: cp.start())
            return c

        def wait_e(e, c):
            pieces(e, lambda cp: cp.wait())
            return c

        def tail(do):
            def step(b, c):
                do(pltpu.make_async_copy(zero_scr, xs_ref.at[pl.ds(b * MOE_BLK, MOE_BLK)], zsem))
                return c
            return step

        n_valid = zs_ref[2 * N_EXPERTS]
        lax.fori_loop(0, N_EXPERTS, start_e, 0)
        lax.fori_loop(n_valid, N_MOE_BLOCKS, tail(lambda cp: cp.start()), 0)
        lax.fori_loop(0, N_EXPERTS, wait_e, 0)
        lax.fori_loop(n_valid, N_MOE_BLOCKS, tail(lambda cp: cp.wait()), 0)

    base = i * TM

    def place(r, carry):
        row = h_ref[r]
        for kk in range(TOP_K):
            stage[lpos_ref[kk * T_ALL + base + r]] = row
        return carry

    lax.fori_loop(0, TM, place, 0, unroll=8)

    def send_run(e, off):
        n = run_ref[i * N_EXPERTS + e]
        dst = rdst_ref[i * N_EXPERTS + e]
        bit = TM * TOP_K
        while bit >= 1:
            take = (n & bit) != 0

            @pl.when(take)
            def _(off=off, dst=dst, bit=bit):
                pltpu.make_async_copy(stage.at[pl.ds(off, bit)], xs_ref.at[pl.ds(dst, bit)], sem).start(
                    priority=bit.bit_length() % 2)

            step = jnp.where(take, bit, 0)
            off = off + step
            dst = dst + step
            bit //= 2
        return off

    lax.fori_loop(0, N_EXPERTS, send_run, 0)
    pltpu.make_async_copy(stage, xs_ref.at[pl.ds(0, TM * TOP_K)], sem).wait()


def _dispatch(dplan, h_all):
    return pl.pallas_call(
        _dispatch_kernel,
        grid_spec=pltpu.PrefetchScalarGridSpec(
            num_scalar_prefetch=4,
            grid=(N_ROW_BLOCKS,),
            in_specs=[pl.BlockSpec((TM, ROW_TILE, LANES), lambda i, lp, rn, rd, z: (i, 0, 0))],
            out_specs=pl.BlockSpec(memory_space=pl.ANY),
            scratch_shapes=[pltpu.VMEM((MOE_BLK, ROW_TILE, LANES), BF16),
                            pltpu.VMEM((TM * TOP_K, ROW_TILE, LANES), BF16),
                            pltpu.SemaphoreType.DMA(()), pltpu.SemaphoreType.DMA(())],
        ),
        out_shape=jax.ShapeDtypeStruct((N_SORT_ROWS, ROW_TILE, LANES), BF16),
        compiler_params=_cparams(("arbitrary",)),
        name="moe_dispatch",
    )(*dplan, h_all)


def _experts_kernel(layer, be_ref, first_ref, par_ref, nxt_ref, nv_ref,
                    x_ref, wg_hbm, wu_hbm, wd_hbm, y_ref,
                    wg_s, wu_s, wd_s, wg_f, wu_f, wd_f, wsem):
    i = pl.program_id(0)

    def fetch(e, slot):
        return (pltpu.make_async_copy(wg_hbm.at[layer, e], wg_f.at[slot], wsem.at[slot]),
                pltpu.make_async_copy(wu_hbm.at[layer, e], wu_f.at[slot], wsem.at[slot]),
                pltpu.make_async_copy(wd_hbm.at[layer, e], wd_f.at[slot], wsem.at[slot]))

    @pl.when(i < nv_ref[0])
    def _():
        e = be_ref[i]
        slot = par_ref[i]

        @pl.when(i == 0)
        def _():
            for cp in fetch(e, slot):
                cp.start()

        @pl.when(first_ref[i] == 1)
        def _():
            for cp in fetch(e, slot):
                cp.wait()
            wg_s[...] = wg_f[slot].astype(BF16)
            wu_s[...] = wu_f[slot].astype(BF16)
            wd_s[...] = wd_f[slot].astype(BF16)
            nxt = nxt_ref[i]

            @pl.when(nxt >= 0)
            def _():
                for cp in fetch(nxt, 1 - slot):
                    cp.start()

        xb = x_ref[...].reshape(MOE_BLK, D_MODEL)
        a = jax.nn.silu(_dot(xb, wg_s[...])) * _dot(xb, wu_s[...])
        y_ref[...] = _dot(a.astype(BF16), wd_s[...]).reshape(y_ref.shape)

    @pl.when(i >= nv_ref[0])
    def _():
        y_ref[...] = jnp.zeros(y_ref.shape, y_ref.dtype)


def _experts(block_e, first, parity, nxt, n_valid, xs, w_gate, w_up, w_down, layer):
    def blk(i, be, fi, pa, nx, nv):
        return (jnp.maximum(jnp.minimum(i, nv[0] - 1), 0), 0, 0)

    anyspec = pl.BlockSpec(memory_space=pl.ANY)
    return pl.pallas_call(
        functools.partial(_experts_kernel, layer),
        grid_spec=pltpu.PrefetchScalarGridSpec(
            num_scalar_prefetch=5,
            grid=(N_MOE_BLOCKS,),
            in_specs=[pl.BlockSpec((MOE_BLK, ROW_TILE, LANES), blk), anyspec, anyspec, anyspec],
            out_specs=pl.BlockSpec((MOE_BLK, ROW_TILE, LANES), lambda i, be, fi, pa, nx, nv: (i, 0, 0)),
            scratch_shapes=[pltpu.VMEM((D_MODEL, D_EXPERT), BF16), pltpu.VMEM((D_MODEL, D_EXPERT), BF16),
                            pltpu.VMEM((D_EXPERT, D_MODEL), BF16),
                            pltpu.VMEM((2, D_MODEL, D_EXPERT), F32), pltpu.VMEM((2, D_MODEL, D_EXPERT), F32),
                            pltpu.VMEM((2, D_EXPERT, D_MODEL), F32), pltpu.SemaphoreType.DMA((2,))],
        ),
        out_shape=jax.ShapeDtypeStruct((N_SORT_ROWS, ROW_TILE, LANES), F32),
        compiler_params=_cparams(("arbitrary",)),
        name="moe_experts",
    )(block_e, first, parity, nxt, n_valid, xs, w_gate, w_up, w_down)


def _gather_rows(lpos_ref, run_ref, rdst_ref, ys_ref, ystage, ybuf, sem, i):
    def fetch(tile, buf):
        def fetch_run(e, off):
            n = run_ref[tile * N_EXPERTS + e]
            src = rdst_ref[tile * N_EXPERTS + e]
            bit = TM * TOP_K
            while bit >= 1:
                take = (n & bit) != 0

                @pl.when(take)
                def _(off=off, src=src, bit=bit):
                    pltpu.make_async_copy(ys_ref.at[pl.ds(src, bit)], ystage.at[buf, pl.ds(off, bit)],
                                          sem.at[buf]).start(priority=bit.bit_length() % 2)

                step = jnp.where(take, bit, 0)
                off = off + step
                src = src + step
                bit //= 2
            return off

        lax.fori_loop(0, N_EXPERTS, fetch_run, 0)

    buf = i % 2

    @pl.when(i == 0)
    def _():
        fetch(i, buf)

    @pl.when(i + 1 < N_ROW_BLOCKS)
    def _():
        fetch(i + 1, 1 - buf)

    pltpu.make_async_copy(ys_ref.at[pl.ds(0, TM * TOP_K)], ystage.at[buf], sem.at[buf]).wait()
    base = i * TM

    def unplace(r, carry):
        for kk in range(TOP_K):
            ybuf[kk, r] = ystage[buf, lpos_ref[kk * T_ALL + base + r]]
        return carry

    lax.fori_loop(0, TM, unplace, 0, unroll=8)


def _combined(x_ref, rg_ref, ybuf):
    rg = rg_ref[...]
    y0 = ybuf[0].reshape(TM, D_MODEL)
    y1 = ybuf[1].reshape(TM, D_MODEL)
    return x_ref[...] + rg[:, 0:1] * y0 + rg[:, 1:2] * y1


_COMBINE_SCRATCH = [pltpu.VMEM((2, TM * TOP_K, ROW_TILE, LANES), F32), pltpu.VMEM((TOP_K, TM, ROW_TILE, LANES), F32),
                    pltpu.SemaphoreType.DMA((2,))]


def _combine_kernel(lpos_ref, run_ref, rdst_ref, x_ref, rg_ref, ys_ref, o_ref, ystage, ybuf, sem):
    _gather_rows(lpos_ref, run_ref, rdst_ref, ys_ref, ystage, ybuf, sem, pl.program_id(0))
    o_ref[...] = _combined(x_ref, rg_ref, ybuf)


def _combine(cplan, x_all, rg_all, ys):
    return pl.pallas_call(
        _combine_kernel,
        grid_spec=pltpu.PrefetchScalarGridSpec(
            num_scalar_prefetch=3,
            grid=(N_ROW_BLOCKS,),
            in_specs=[pl.BlockSpec((TM, D_MODEL), lambda i, a, b, c: (i, 0)),
                      pl.BlockSpec((TM, LANES), lambda i, a, b, c: (i, 0)),
                      pl.BlockSpec(memory_space=pl.ANY)],
            out_specs=pl.BlockSpec((TM, D_MODEL), lambda i, a, b, c: (i, 0)),
            scratch_shapes=_COMBINE_SCRATCH,
        ),
        out_shape=jax.ShapeDtypeStruct((T_ALL, D_MODEL), F32),
        compiler_params=_cparams(("arbitrary",)),
        name="moe_combine",
    )(*cplan, x_all, rg_all, ys)


def _final_kernel(lpos_ref, run_ref, rdst_ref, x_ref, rg_ref, ys_ref, nfin_ref, op_ref, os_ref, ystage, ybuf, sem):
    i = pl.program_id(0)
    _gather_rows(lpos_ref, run_ref, rdst_ref, ys_ref, ystage, ybuf, sem, i)
    y = _rms(_combined(x_ref, rg_ref, ybuf), nfin_ref[...])

    @pl.when(i < N_PROMPT_BLOCKS)
    def _():
        op_ref[...] = y

    @pl.when(i >= N_PROMPT_BLOCKS)
    def _():
        os_ref[...] = y


def _final(cplan, x_all, rg_all, ys, nfin):
    return pl.pallas_call(
        _final_kernel,
        grid_spec=pltpu.PrefetchScalarGridSpec(
            num_scalar_prefetch=3,
            grid=(N_ROW_BLOCKS,),
            in_specs=[pl.BlockSpec((TM, D_MODEL), lambda i, a, b, c: (i, 0)),
                      pl.BlockSpec((TM, LANES), lambda i, a, b, c: (i, 0)),
                      pl.BlockSpec(memory_space=pl.ANY),
                      pl.BlockSpec((1, D_MODEL), lambda i, a, b, c: (0, 0))],
            out_specs=[pl.BlockSpec((TM, D_MODEL), lambda i, a, b, c: (jnp.minimum(i, N_PROMPT_BLOCKS - 1), 0)),
                       pl.BlockSpec((TM, D_MODEL), lambda i, a, b, c: (0, 0))],
            scratch_shapes=_COMBINE_SCRATCH,
        ),
        out_shape=[jax.ShapeDtypeStruct((T_PROMPT, D_MODEL), F32), jax.ShapeDtypeStruct((T_SAMPLE, D_MODEL), F32)],
        compiler_params=_cparams(("arbitrary",)),
        name="moe_combine_final",
    )(*cplan, x_all, rg_all, ys, nfin)


def _moe(h_all, rt_all, cnt, tcnt, w_gate, w_up, w_down, layer):
    plan, dplan, cplan = _moe_metadata(rt_all, cnt, tcnt)
    xs = _dispatch(dplan, h_all)
    ys = _experts(*plan, xs, w_gate, w_up, w_down, layer)
    return cplan, ys


def _pool_project(d_groups, wp_ref, scale):
    outs = [_dot(d_groups[g].astype(BF16), wp_ref[g]) for g in range(len(POOL_SIZES))]
    return jnp.concatenate(outs, axis=1) * scale


def _mix1_prompt_kernel(x_ref, nm_ref, wp_ref, sc_ref, nf_ref, wr_ref, br_ref,
                        x3_ref, h_ref, ri_ref, rg_ref, tc_ref, pl_ref, cnt_ref, ext):
    i = pl.program_id(0)

    @pl.when(i == 0)
    def _():
        cnt_ref[...] = jnp.zeros_like(cnt_ref)

    x = x_ref[...]
    hp = _rms(x, nm_ref[...])

    @pl.when(i % STEPS_PER_BATCH == 0)
    def _():
        ext[0:POOL_MAX, :] = jnp.zeros((POOL_MAX, D_MODEL), F32)

    ext[POOL_MAX:, :] = hp
    pos = (i % STEPS_PER_BATCH) * TM + lax.broadcasted_iota(jnp.int32, (TM, 1), 0)
    d_groups = []
    for g, w in enumerate(POOL_SIZES):
        cols = slice(g * POOL_GROUP_DIM, (g + 1) * POOL_GROUP_DIM)
        acc = ext[:, cols]
        span = 1
        while span < w:
            acc = acc + pltpu.roll(acc, span, 0)
            span *= 2
        cnt = jnp.minimum(pos + 1, w).astype(F32)
        d_groups.append(acc[POOL_MAX:] / cnt - hp[:, cols])
    tail = hp[TM - POOL_MAX:, :]
    ext[0:POOL_MAX, :] = tail
    pl_ref[...] = tail

    x3 = x + _pool_project(d_groups, wp_ref, sc_ref[...])
    x3_ref[...] = x3
    h, ids, gates = _route(x3, nf_ref[...], wr_ref[...], br_ref[...])
    h_ref[...] = h.reshape(h_ref.shape)
    ri_ref[...] = _rank_pack(ids, cnt_ref, tc_ref)
    rg_ref[...] = gates


def _mix1_prompt(x_all, nm, wp, sc, nf, wr, br):
    row_spec = pl.BlockSpec((TM, D_MODEL), lambda i: (i, 0))
    row3_spec = pl.BlockSpec((TM, ROW_TILE, LANES), lambda i: (i, 0, 0))
    lane_spec = pl.BlockSpec((TM, LANES), lambda i: (i, 0))
    return pl.pallas_call(
        _prompt_steps(_mix1_prompt_kernel, 7),
        grid=(N_ROW_BLOCKS,),
        in_specs=[row_spec, _const_spec((1, D_MODEL)),
                  _const_spec((len(POOL_SIZES), POOL_GROUP_DIM, POOL_GROUP_DIM)), _const_spec((1, D_MODEL)),
                  _const_spec((1, D_MODEL)), _const_spec((D_MODEL, 2 * LANES)), _const_spec((1, LANES))],
        out_specs=[row_spec, row3_spec, pl.BlockSpec((8, TM), lambda i: (0, i)), lane_spec,
                   pl.BlockSpec((None, 1, LANES), lambda i: (i, 0, 0)),
                   pl.BlockSpec((None, POOL_MAX, D_MODEL),
                                lambda i: (jnp.minimum(i // STEPS_PER_BATCH, BATCH - 1), 0, 0)),
                   _const_spec((1, LANES))],
        out_shape=[jax.ShapeDtypeStruct((T_ALL, D_MODEL), F32), jax.ShapeDtypeStruct((T_ALL, ROW_TILE, LANES), BF16),
                   jax.ShapeDtypeStruct((8, T_ALL), jnp.int32), jax.ShapeDtypeStruct((T_ALL, LANES), F32),
                   jax.ShapeDtypeStruct((N_ROW_BLOCKS, 1, LANES), F32),
                   jax.ShapeDtypeStruct((BATCH, POOL_MAX, D_MODEL), F32), jax.ShapeDtypeStruct((1, LANES), F32)],
        scratch_shapes=[pltpu.VMEM((POOL_MAX + TM, D_MODEL), F32)],
        compiler_params=_cparams(("arbitrary",)),
        name="mix1_prompt",
    )(x_all, nm, wp, sc, nf, wr, br)


def _mix1_sample_kernel(x_ref, st_ref, nm_ref, wp_ref, sc_ref, nf_ref, wr_ref, br_ref, cnt_in,
                        x3_in, h_in, ri_in, rg_in, tc_in,
                        x3_ref, h_ref, ri_ref, rg_ref, tc_ref, hs_ref, cnt_ref):
    del x3_in, h_in, ri_in, rg_in, tc_in
    x = x_ref[...]
    hs = _rms(x, nm_ref[...])
    hs_ref[...] = hs
    n_ctx = POOL_MAX - 1
    d_groups = []
    for g, w in enumerate(POOL_SIZES):
        cols = slice(g * POOL_GROUP_DIM, (g + 1) * POOL_GROUP_DIM)
        parts = []
        for t in range(DEC_SEQ):
            acc = hs[t * DEC_BATCH:(t + 1) * DEC_BATCH, cols]
            for back in range(1, w):
                src = t - back
                if src >= 0:
                    acc = acc + hs[src * DEC_BATCH:(src + 1) * DEC_BATCH, cols]
                else:
                    acc = acc + st_ref[n_ctx + src, :, cols]
            parts.append(acc / float(w) - hs[t * DEC_BATCH:(t + 1) * DEC_BATCH, cols])
        d_groups.append(jnp.concatenate(parts, axis=0))
    x3 = x + _pool_project(d_groups, wp_ref, sc_ref[...])
    x3_ref[...] = x3
    h, ids, gates = _route(x3, nf_ref[...], wr_ref[...], br_ref[...])
    h_ref[...] = h.reshape(h_ref.shape)
    cnt_ref[...] = cnt_in[...]
    ri_ref[...] = _rank_pack(ids, cnt_ref, tc_ref)
    rg_ref[...] = gates


def _mix1_sample(x_all, state_t, nm, wp, sc, nf, wr, br, cnt, x3_all, h_all, ri_all, rg_all, tc_all):
    sample_rows = pl.BlockSpec((TM, D_MODEL), lambda g: (N_PROMPT_BLOCKS, 0))
    sample_rows3 = pl.BlockSpec((TM, ROW_TILE, LANES), lambda g: (N_PROMPT_BLOCKS, 0, 0))
    sample_lanes = pl.BlockSpec((TM, LANES), lambda g: (N_PROMPT_BLOCKS, 0))
    anyspec = pl.BlockSpec(memory_space=pl.ANY)
    n_in = 9
    return pl.pallas_call(
        _mix1_sample_kernel,
        grid=(1,),
        in_specs=[sample_rows, _const_spec((POOL_MAX - 1, DEC_BATCH, D_MODEL)), _const_spec((1, D_MODEL)),
                  _const_spec((len(POOL_SIZES), POOL_GROUP_DIM, POOL_GROUP_DIM)), _const_spec((1, D_MODEL)),
                  _const_spec((1, D_MODEL)), _const_spec((D_MODEL, 2 * LANES)), _const_spec((1, LANES)),
                  _const_spec((1, LANES)), anyspec, anyspec, anyspec, anyspec, anyspec],
        out_specs=[sample_rows, sample_rows3, pl.BlockSpec((8, TM), lambda g: (0, N_PROMPT_BLOCKS)), sample_lanes,
                   pl.BlockSpec((None, 1, LANES), lambda g: (N_PROMPT_BLOCKS, 0, 0)),
                   _const_spec((T_SAMPLE, D_MODEL)), _const_spec((1, LANES))],
        out_shape=[jax.ShapeDtypeStruct((T_ALL, D_MODEL), F32), jax.ShapeDtypeStruct((T_ALL, ROW_TILE, LANES), BF16),
                   jax.ShapeDtypeStruct((8, T_ALL), jnp.int32), jax.ShapeDtypeStruct((T_ALL, LANES), F32),
                   jax.ShapeDtypeStruct((N_ROW_BLOCKS, 1, LANES), F32),
                   jax.ShapeDtypeStruct((T_SAMPLE, D_MODEL), F32), jax.ShapeDtypeStruct((1, LANES), F32)],
        input_output_aliases={n_in: 0, n_in + 1: 1, n_in + 2: 2, n_in + 3: 3, n_in + 4: 4},
        compiler_params=_cparams(("arbitrary",)),
        name="mix1_sample",
    )(x_all, state_t, nm, wp, sc, nf, wr, br, cnt, x3_all, h_all, ri_all, rg_all, tc_all)


def _router_weights(wg, bg, we, be):
    w = jnp.concatenate([wg, jnp.transpose(we, (1, 0, 2)).reshape(D_MODEL, N_EXPERTS)], axis=1)
    b = jnp.concatenate([bg, be.reshape(N_EXPERTS)])
    pad = LANES - N_GROUPS - N_EXPERTS
    w = jnp.pad(w, ((0, 0), (0, pad)))
    w_hi = w.astype(BF16)
    w_lo = (w - w_hi.astype(F32)).astype(BF16)
    return jnp.concatenate([w_hi, w_lo], axis=1), jnp.pad(b, (0, pad)).reshape(1, LANES)


def _stack(tab):
    return jnp.stack([jnp.concatenate([tab[h] for h in heads], axis=0) for heads in STACK_HEADS])


def kernel(x_prompt, x_sample, cache_k_win, cache_v_win, state_pool, norm_mix, norm_ffn, norm_final, w_in,
           a_ln_g, a_ln_b, a_w_s, a_b_s, b_sinks, rel_bias_table, w_out, c_w_pool, c_scale,
           router_group_w, router_group_b, router_expert_w, router_expert_b, w_gate, w_up, w_down):
    xs_t = jnp.transpose(x_sample, (1, 0, 2)).reshape(T_SAMPLE, D_MODEL)
    xp2 = x_prompt.reshape(T_PROMPT, D_MODEL)
    win =w_in[0].astype(BF16)
    wout = w_out[0].astype(BF16)
    lng = a_ln_g[0].reshape(1, A_WIDTH)
    lnb = a_ln_b[0].reshape(1, A_WIDTH)
    bias_p, bias_sc, bias_sn, ws_tril = _prep(rel_bias_table, a_w_s[0])
    wsp = ws_tril.reshape(A_HEADS // 2, 2, CHUNK, CHUNK).transpose(0, 2, 1, 3).reshape(A_HEADS // 2, CHUNK, 2 * CHUNK)
    bs_full = jnp.repeat(a_b_s[0].T, A_HEAD_DIM, axis=1)
    bias_p = jnp.stack([_stack(bias_p[0]), _stack(bias_p[1])])
    bias_sc = _stack(bias_sc)
    bias_sn = _stack(bias_sn)
    sinks = b_sinks[0]
    sink_p = jnp.stack([jnp.repeat(sinks[jnp.array(hh)], WINDOW) for hh in STACK_HEADS])
    sink_s = jnp.stack([jnp.repeat(sinks[jnp.array(hh)], 32) for hh in STACK_HEADS])
    bias_p = bias_p.at[:, :, :, 0].set(jnp.broadcast_to(sink_p[None], (2, 2, 4 * WINDOW)))
    bias_sc = bias_sc.at[:, :, 0].set(sink_s)
    pairs = [(t, s) for t in range(DEC_SEQ) for s in range(t + 1)]
    wcoef = jnp.stack([jnp.repeat(a_w_s[0][:, t, s], A_HEAD_DIM) for t, s in pairs])
    wcoef = jnp.pad(wcoef, ((0, 16 - len(pairs)), (0, 0)))
    bcoef = jnp.pad(jnp.repeat(a_b_s[0][:, :DEC_SEQ].T, A_HEAD_DIM, axis=1), ((0, 8 - DEC_SEQ), (0, 0)))
    ck = cache_k_win
    cv = cache_v_win
    routers = [_router_weights(router_group_w[l], router_group_b[l], router_expert_w[l], router_expert_b[l])
               for l in range(2)]
    nm = [norm_mix[l].reshape(1, D_MODEL) for l in range(2)]
    nf = [norm_ffn[l].reshape(1, D_MODEL) for l in range(2)]

    x1_all, h_all, ri_all, rg_all, tc_all, k_last, v_last, va_last, cnt0 = _mix0_prompt(
        xp2, nm[0], win, lng, lnb, wsp, bs_full, bias_p, wout, nf[0], *routers[0])
    x1_all, h_all, ri_all, rg_all, tc_all, k_new, v_new, va_s, cnt0 = _mix0_sample(
        xs_t, nm[0], win, lng, lnb, wcoef, bcoef, ck, cv, bias_sc, bias_sn, wout, nf[0], *routers[0], cnt0,
        x1_all, h_all, ri_all, rg_all, tc_all)
    cplan0, ys0 = _moe(h_all, ri_all, cnt0, tc_all, w_gate, w_up, w_down, 0)
    x2_all = _combine(cplan0, x1_all, rg_all, ys0)

    wp = c_w_pool[0].astype(BF16)
    sc = c_scale[0].reshape(1, D_MODEL)
    x3_all, h2_all, ri2_all, rg2_all, tc2_all, pool_tail, cnt1 = _mix1_prompt(
        x2_all, nm[1], wp, sc, nf[1], *routers[1])
    state_t = jnp.transpose(state_pool[0], (1, 0, 2))
    x3_all, h2_all, ri2_all, rg2_all, tc2_all, hs1, cnt1 = _mix1_sample(
        x2_all, state_t, nm[1], wp, sc, nf[1], *routers[1], cnt1, x3_all, h2_all, ri2_all, rg2_all, tc2_all)
    cplan1, ys1 = _moe(h2_all, ri2_all, cnt1, tc2_all, w_gate, w_up, w_down, 1)
    y_p, y_s = _final(cplan1, x3_all, rg2_all, ys1, norm_final.reshape(1, D_MODEL))

    def from_tmajor(a, width):
        return jnp.transpose(a.reshape(DEC_SEQ, DEC_BATCH, width), (1, 0, 2))

    y_prompt = y_p.reshape(BATCH, SEQ, D_MODEL)
    y_sample = from_tmajor(y_s, D_MODEL)
    win_k_p = k_last.reshape(1, BATCH, WINDOW, B_KV_HEADS, B_HEAD_DIM)
    win_v_p = v_last.reshape(1, BATCH, WINDOW, B_KV_HEADS, B_HEAD_DIM)
    kn = from_tmajor(k_new, KV_WIDTH).reshape(DEC_BATCH, DEC_SEQ, B_KV_HEADS, B_HEAD_DIM)
    vn = from_tmajor(v_new, KV_WIDTH).reshape(DEC_BATCH, DEC_SEQ, B_KV_HEADS, B_HEAD_DIM)
    win_k_s = jnp.concatenate([cache_k_win[0][:, DEC_SEQ:], kn], axis=1)[None]
    win_v_s = jnp.concatenate([cache_v_win[0][:, DEC_SEQ:], vn], axis=1)[None]
    chunk_v_p = va_last.reshape(1, BATCH, CHUNK, A_HEADS, A_HEAD_DIM)
    chunk_v_s = from_tmajor(va_s, A_WIDTH).reshape(1, DEC_BATCH, DEC_SEQ, A_HEADS, A_HEAD_DIM)
    pool_p = pool_tail[:, 1:][None]
    pool_s = jnp.concatenate([state_pool[0][:, DEC_SEQ:], from_tmajor(hs1, D_MODEL)], axis=1)[None]
    return (y_prompt, y_sample, win_k_p, win_v_p, win_k_s, win_v_s, chunk_v_p, chunk_v_s, pool_p, pool_s)
```

```python
import functools
import math

import numpy as np
import jax
import jax.numpy as jnp
from jax import lax
from jax.experimental import pallas as pl
from jax.experimental.pallas import tpu as pltpu

F32 = jnp.float32
BF16 = jnp.bfloat16

D_MODEL = 1024
BATCH = 2
SEQ = 8192
DEC_BATCH = 128
DEC_SEQ = 4
A_WIDTH = 512
A_HEADS = 8
A_HEAD_DIM = 64
CHUNK = 128
B_HEADS = 8
B_KV_HEADS = 2
B_HEAD_DIM = 64
B_GROUP = 4
WINDOW = 128
N_BUCKETS = 32
MAX_DISTANCE = WINDOW
Q_WIDTH = 512
KV_WIDTH = 128
IN_WIDTH = 2 * A_WIDTH + Q_WIDTH + 2 * KV_WIDTH
ATTN_SCALE = B_HEAD_DIM ** -0.5
NEG_INF = -1e30
POOL_SIZES = (2, 4, 8, 16)
POOL_GROUP_DIM = 256
POOL_MAX = 16
N_GROUPS = 4
EXPERTS_PER_GROUP = 8
N_EXPERTS = 32
TOP_K = 2
D_EXPERT = 512
EPS = 1e-6

LANES = 128
ROW_TILE = D_MODEL // LANES
T_PROMPT = BATCH * SEQ
T_SAMPLE = DEC_BATCH * DEC_SEQ
T_ALL = T_PROMPT + T_SAMPLE
TM = 512
N_PROMPT_BLOCKS = T_PROMPT // TM
N_ROW_BLOCKS = T_ALL // TM
STEPS_PER_BATCH = SEQ // TM
SUB = TM // WINDOW
N_SLOTS = T_ALL * TOP_K
MOE_BLK = 512
N_MOE_BLOCKS = N_SLOTS // MOE_BLK + N_EXPERTS
N_SORT_ROWS = N_MOE_BLOCKS * MOE_BLK
SAMPLE_GROUP = 8
N_SAMPLE_GROUPS = DEC_BATCH // SAMPLE_GROUP
VMEM_LIMIT = 56 * 1024 * 1024

STACK_HEADS = ((0, 2, 5, 7), (1, 3, 4, 6))


def _t5_bucket_np(dist):
    n = np.maximum(dist, 0)
    max_exact = N_BUCKETS // 2
    nf = np.maximum(n, 1).astype(np.float32)
    large = max_exact + (np.log(nf / np.float32(max_exact)) / np.float32(math.log(MAX_DISTANCE / max_exact))
                         * np.float32(N_BUCKETS - max_exact)).astype(np.int32)
    large = np.minimum(large, N_BUCKETS - 1)
    return np.where(n < max_exact, n, large).astype(np.int32)


def _bucket_tables():
    qi = np.arange(WINDOW)[:, None]
    ki = np.arange(2 * WINDOW)[None, :]
    dist = qi + WINDOW - ki
    valid = (dist >= 0) & (dist < WINDOW)
    bp = np.where(valid, _t5_bucket_np(dist), -1)
    bp_first = np.where(ki >= WINDOW, bp, -1)
    bkt_p = np.stack([bp_first, bp]).astype(np.int32)

    t = np.repeat(np.arange(DEC_SEQ), SAMPLE_GROUP)[:, None]
    b = np.tile(np.arange(SAMPLE_GROUP), DEC_SEQ)[:, None]
    cb = np.repeat(np.arange(SAMPLE_GROUP), WINDOW)[None, :]
    cj = np.tile(np.arange(WINDOW), SAMPLE_GROUP)[None, :]
    dist_c = t + WINDOW - cj
    valid_c = (cb == b) & (dist_c >= 0) & (dist_c < WINDOW)
    bkt_sc = np.where(valid_c, _t5_bucket_np(dist_c), -1).astype(np.int32)
    nt = np.repeat(np.arange(DEC_SEQ), SAMPLE_GROUP)[None, :]
    nb = np.tile(np.arange(SAMPLE_GROUP), DEC_SEQ)[None, :]
    dist_n = t - nt
    valid_n = (nb == b) & (dist_n >= 0)
    bkt_sn = np.where(valid_n, _t5_bucket_np(dist_n), -1).astype(np.int32)
    bkt_sn = np.concatenate([bkt_sn, np.full((32, LANES - 32), -1, np.int32)], axis=1)
    return bkt_p, bkt_sc, bkt_sn


_BKT_P, _BKT_SC, _BKT_SN = _bucket_tables()


def _cparams(semantics):
    return pltpu.CompilerParams(dimension_semantics=semantics, vmem_limit_bytes=VMEM_LIMIT)


def _rms(x, g):
    return x * lax.rsqrt(jnp.mean(x * x, axis=-1, keepdims=True) + EPS) * g


def _layernorm(x, g, b):
    xc = x - jnp.mean(x, axis=-1, keepdims=True)
    return xc * lax.rsqrt(jnp.mean(xc * xc, axis=-1, keepdims=True) + EPS) * g + b


def _dot(a, b):
    return jnp.dot(a, b, preferred_element_type=F32)


def _dot_nt(a, b):
    return lax.dot_general(a, b, (((1,), (1,)), ((), ())), preferred_element_type=F32)


def _project(x, nm, win, lng, lnb):
    h = _rms(x, nm)
    z = _dot(h.astype(BF16), win)
    u = jax.nn.gelu(z[:, :A_WIDTH])
    va = _layernorm(jax.nn.gelu(z[:, A_WIDTH:2 * A_WIDTH]), lng, lnb)
    q = z[:, 2 * A_WIDTH:2 * A_WIDTH + Q_WIDTH] * ATTN_SCALE
    k = z[:, 2 * A_WIDTH + Q_WIDTH:2 * A_WIDTH + Q_WIDTH + KV_WIDTH]
    v = z[:, 2 * A_WIDTH + Q_WIDTH + KV_WIDTH:]
    return u, va, q, k, v


def _route(x1, nf, wr, br):
    hf = _rms(x1, nf)
    h = hf.astype(BF16)
    h_lo = (hf - h.astype(F32)).astype(BF16)
    part = _dot(h, wr)
    logits = part[:, :LANES] + part[:, LANES:] + _dot(h_lo, wr[:, :LANES]) + br
    rows = logits.shape[0]
    lane = lax.broadcasted_iota(jnp.int32, (rows, LANES), 1)
    lanef = lane.astype(F32)
    big = jnp.float32(1e9)
    is_g = lane < N_GROUPS
    gl = jnp.where(is_g, logits, -jnp.inf)
    gmax = jnp.max(gl, axis=1, keepdims=True)
    gsel = jnp.min(jnp.where(gl == gmax, lanef, big), axis=1, keepdims=True)
    gsum = jnp.sum(jnp.where(is_g, jnp.exp(logits - gmax), 0.0), axis=1, keepdims=True)
    g1 = 1.0 / gsum
    lo = N_GROUPS + EXPERTS_PER_GROUP * gsel
    emask = (lanef >= lo) & (lanef < lo + EXPERTS_PER_GROUP)
    el = jnp.where(emask, logits, -jnp.inf)
    v1 = jnp.max(el, axis=1, keepdims=True)
    i1 = jnp.min(jnp.where(el == v1, lanef, big), axis=1, keepdims=True)
    el2 = jnp.where(lanef == i1, -jnp.inf, el)
    v2 = jnp.max(el2, axis=1, keepdims=True)
    i2 = jnp.min(jnp.where(el2 == v2, lanef, big), axis=1, keepdims=True)
    e2 = jnp.exp(v2 - v1)
    den = 1.0 + e2
    w1 = g1 / den
    w2 = g1 * e2 / den
    ids = jnp.where(lane == 0, i1 - N_GROUPS, jnp.where(lane == 1, i2 - N_GROUPS, 0.0)).astype(jnp.int32)
    gates = jnp.where(lane == 0, w1, jnp.where(lane == 1, w2, 0.0))
    return h, ids, gates


def _rank_pack(ids, cnt_ref, tcnt_ref):
    rows = ids.shape[0]
    lane = lax.broadcasted_iota(jnp.int32, (rows, LANES), 1)
    o0 = (lane == ids[:, 0:1]).astype(F32)
    o1 = (lane == ids[:, 1:2]).astype(F32)
    r = lax.broadcasted_iota(jnp.int32, (rows, rows), 0)
    c = lax.broadcasted_iota(jnp.int32, (rows, rows), 1)
    before = (c < r).astype(BF16)
    p01 = _dot(before, jnp.concatenate([o0, o1], axis=1).astype(BF16))
    p0 = p01[:, :LANES]
    p1 = p01[:, LANES:]
    c0 = jnp.sum(o0, axis=0, keepdims=True)
    c1 = jnp.sum(o1, axis=0, keepdims=True)
    ctile = c0 + c1
    cnt_ref[...] = cnt_ref[...] + ctile
    tcnt_ref[...] = ctile
    inc = jnp.broadcast_to(ctile, (8, LANES))
    lane8 = lax.broadcasted_iota(jnp.int32, (8, LANES), 1)
    for sh in (1, 2, 4, 8, 16, 32, 64):
        inc = inc + jnp.where(lane8 >= sh, pltpu.roll(inc, sh, 1), 0.0)
    start = inc[0:1] - ctile
    lpos0 = jnp.sum(o0 * (start + p0), axis=1, keepdims=True)
    lpos1 = jnp.sum(o1 * (start + c0 + p1), axis=1, keepdims=True)
    idf = ids.astype(F32)
    packed = jnp.where(lane < TOP_K, idf, 0.0)
    for ln, col in ((4, lpos0), (5, lpos1)):
        packed = jnp.where(lane == ln, col, packed)
    return jnp.transpose(packed)[:8].astype(jnp.int32)


def _prep_kernel(tab_ref, bp_ref, bsc_ref, bsn_ref, ws_ref, op_ref, osc_ref, osn_ref, ows_ref):
    def fill(bkt, write):
        for h in range(B_HEADS):
            acc = jnp.full(bkt.shape, NEG_INF, F32)
            for b in range(N_BUCKETS):
                acc = jnp.where(bkt == b, tab_ref[b, h], acc)
            write(h, acc)

    for var in range(2):
        def wr_p(h, acc, var=var):
            op_ref[var, h] = acc
        fill(bp_ref[var], wr_p)

    def wr_sc(h, acc):
        osc_ref[h] = acc
    fill(bsc_ref[...], wr_sc)

    def wr_sn(h, acc):
        osn_ref[h] = acc
    fill(bsn_ref[...], wr_sn)

    r = lax.broadcasted_iota(jnp.int32, (CHUNK, CHUNK), 0)
    c = lax.broadcasted_iota(jnp.int32, (CHUNK, CHUNK), 1)
    for h in range(A_HEADS):
        ows_ref[h] = jnp.where(r >= c, ws_ref[h], 0.0).astype(BF16)


def _prep(rel_bias_table, w_s):
    vm = pl.BlockSpec(memory_space=pltpu.VMEM)
    return pl.pallas_call(
        _prep_kernel,
        in_specs=[pl.BlockSpec(memory_space=pltpu.SMEM), vm, vm, vm, vm],
        out_specs=[vm, vm, vm, vm],
        out_shape=[
            jax.ShapeDtypeStruct((2, B_HEADS, WINDOW, 2 * WINDOW), F32),
            jax.ShapeDtypeStruct((B_HEADS, 32, SAMPLE_GROUP * WINDOW), F32),
            jax.ShapeDtypeStruct((B_HEADS, 32, LANES), F32),
            jax.ShapeDtypeStruct((A_HEADS, CHUNK, CHUNK), BF16),
        ],
        name="prep_tables",
    )(rel_bias_table, jnp.asarray(_BKT_P), jnp.asarray(_BKT_SC), jnp.asarray(_BKT_SN), w_s)


def _gate_pairs(va_rows, wsp_ref, lane_lo):
    outs = []
    for p in range(A_HEADS // 2):
        vp = va_rows[:, p * LANES:(p + 1) * LANES]
        rhs = jnp.concatenate([jnp.where(lane_lo, vp, 0.0), jnp.where(lane_lo, 0.0, vp)], axis=0).astype(BF16)
        outs.append(_dot(wsp_ref[p], rhs))
    return jnp.concatenate(outs, axis=1)


def _prompt_steps(body, first_row_out):
    def kern(*refs):
        i = pl.program_id(0)

        @pl.when(i < N_PROMPT_BLOCKS)
        def _():
            body(*refs)

        @pl.when(i >= N_PROMPT_BLOCKS)
        def _():
            for r in refs[first_row_out:first_row_out + 5]:
                r[...] = jnp.zeros(r.shape, r.dtype)

    return kern


def _mix0_prompt_kernel(x_ref, nm_ref, win_ref, lng_ref, lnb_ref, wsp_ref, bs_ref, bias_ref,
                        wout_ref, nf_ref, wr_ref, br_ref,
                        x1_ref, h_ref, ri_ref, rg_ref, tc_ref, kl_ref, vl_ref, val_ref, cnt_ref,
                        kprev, vprev, mix_scr):
    @pl.when(pl.program_id(0) == 0)
    def _():
        cnt_ref[...] = jnp.zeros_like(cnt_ref)

    x = x_ref[...]
    u, va, q, k, v = _project(x, nm_ref[...], win_ref[...], lng_ref[...], lnb_ref[...])
    lane_lo = lax.broadcasted_iota(jnp.int32, (WINDOW, LANES), 1) < B_HEAD_DIM
    row0 = lax.broadcasted_iota(jnp.int32, (WINDOW, KV_WIDTH), 0) == 0
    first = pl.program_id(0) % STEPS_PER_BATCH == 0

    @pl.when(first)
    def _():
        kprev[...] = jnp.zeros_like(kprev)
        vprev[...] = jnp.zeros_like(vprev)

    for j in range(SUB):
        rows = slice(j * WINDOW, (j + 1) * WINDOW)
        s_gate = _gate_pairs(va[rows], wsp_ref, lane_lo)
        mix_scr[rows, :A_WIDTH] = u[rows] * (s_gate + bs_ref[...])

        if j == 0:
            kp, vp = kprev[...], vprev[...]
        else:
            prows = slice((j - 1) * WINDOW, j * WINDOW)
            kp, vp = k[prows], v[prows]
        kk = jnp.concatenate([jnp.where(row0, 0.0, kp), k[rows]], axis=0)
        vv = jnp.concatenate([jnp.where(row0, 0.0, vp), v[rows]], axis=0)
        kops = (kk.astype(BF16), pltpu.roll(kk, B_HEAD_DIM, 1).astype(BF16))
        vops = (vv.astype(BF16), pltpu.roll(vv, B_HEAD_DIM, 1).astype(BF16))
        qt = [q[rows, p * LANES:(p + 1) * LANES] for p in range(4)]
        q_even = [jnp.where(lane_lo, t, 0.0) for t in qt]
        q_odd = [jnp.where(lane_lo, 0.0, t) for t in qt]
        stacks = (jnp.concatenate([q_even[0], q_even[1], q_odd[2], q_odd[3]], axis=0),
                  jnp.concatenate([q_odd[0], q_odd[1], q_even[2], q_even[3]], axis=0))
        o = []
        for st in range(2):
            s = _dot_nt(stacks[st].astype(BF16), kops[st])
            if j == 0:
                bias = bias_ref[jnp.where(first, 0, 1), st]
            else:
                bias = bias_ref[1, st]
            s = s + bias
            m = jnp.max(s, axis=-1, keepdims=True)
            p = jnp.exp(s - m)
            den = jnp.sum(p, axis=-1, keepdims=True)
            o.append(_dot(p.astype(BF16), vops[st]) / den)
        oa, ob = o
        sl = [slice(i * WINDOW, (i + 1) * WINDOW) for i in range(4)]
        tiles = (jnp.where(lane_lo, oa[sl[0]], ob[sl[0]]), jnp.where(lane_lo, oa[sl[1]], ob[sl[1]]),
                 jnp.where(lane_lo, ob[sl[2]], oa[sl[2]]), jnp.where(lane_lo, ob[sl[3]], oa[sl[3]]))
        for p in range(4):
            mix_scr[rows, A_WIDTH + p * LANES:A_WIDTH + (p + 1) * LANES] = tiles[p]

    last = slice(TM - WINDOW, TM)
    kprev[...] = k[last]
    vprev[...] = v[last]
    kl_ref[...] = k[last]
    vl_ref[...] = v[last]
    val_ref[...] = va[last]

    x1 = x + _dot(mix_scr[...].astype(BF16), wout_ref[...])
    x1_ref[...] = x1
    h, ids, gates = _route(x1, nf_ref[...], wr_ref[...], br_ref[...])
    h_ref[...] = h.reshape(h_ref.shape)
    ri_ref[...] = _rank_pack(ids, cnt_ref, tc_ref)
    rg_ref[...] = gates


def _const_spec(shape):
    nd = len(shape)
    return pl.BlockSpec(shape, lambda i, _n=nd: (0,) * _n)


def _mix0_prompt(x_all, nm, win, lng, lnb, wsp, bs_full, bias_p, wout, nf, wr, br):
    row_spec = pl.BlockSpec((TM, D_MODEL), lambda i: (i, 0))
    row3_spec = pl.BlockSpec((TM, ROW_TILE, LANES), lambda i: (i, 0, 0))
    lane_spec = pl.BlockSpec((TM, LANES), lambda i: (i, 0))
    last_kv = pl.BlockSpec((None, WINDOW, KV_WIDTH), lambda i: (jnp.minimum(i // STEPS_PER_BATCH, BATCH - 1), 0, 0))
    last_va = pl.BlockSpec((None, WINDOW, A_WIDTH), lambda i: (jnp.minimum(i // STEPS_PER_BATCH, BATCH - 1), 0, 0))
    return pl.pallas_call(
        _prompt_steps(_mix0_prompt_kernel, 12),
        grid=(N_ROW_BLOCKS,),
        in_specs=[pl.BlockSpec((TM, D_MODEL), lambda i: (jnp.minimum(i, N_PROMPT_BLOCKS - 1), 0)),
                  _const_spec((1, D_MODEL)), _const_spec((D_MODEL, IN_WIDTH)),
                  _const_spec((1, A_WIDTH)), _const_spec((1, A_WIDTH)),
                  _const_spec((A_HEADS // 2, CHUNK, 2 * CHUNK)), _const_spec((CHUNK, A_WIDTH)),
                  _const_spec((2, 2, 4 * WINDOW, 2 * WINDOW)),
                  _const_spec((A_WIDTH + Q_WIDTH, D_MODEL)), _const_spec((1, D_MODEL)),
                  _const_spec((D_MODEL, 2 * LANES)), _const_spec((1, LANES))],
        out_specs=[row_spec, row3_spec, pl.BlockSpec((8, TM), lambda i: (0, i)), lane_spec,
                   pl.BlockSpec((None, 1, LANES), lambda i: (i, 0, 0)),
                   last_kv, last_kv, last_va, _const_spec((1, LANES))],
        out_shape=[jax.ShapeDtypeStruct((T_ALL, D_MODEL), F32), jax.ShapeDtypeStruct((T_ALL, ROW_TILE, LANES), BF16),
                   jax.ShapeDtypeStruct((8, T_ALL), jnp.int32), jax.ShapeDtypeStruct((T_ALL, LANES), F32),
                   jax.ShapeDtypeStruct((N_ROW_BLOCKS, 1, LANES), F32),
                   jax.ShapeDtypeStruct((BATCH, WINDOW, KV_WIDTH), F32),
                   jax.ShapeDtypeStruct((BATCH, WINDOW, KV_WIDTH), F32),
                   jax.ShapeDtypeStruct((BATCH, WINDOW, A_WIDTH), F32),
                   jax.ShapeDtypeStruct((1, LANES), F32)],
        scratch_shapes=[pltpu.VMEM((WINDOW, KV_WIDTH), F32), pltpu.VMEM((WINDOW, KV_WIDTH), F32),
                        pltpu.VMEM((TM, D_MODEL), F32)],
        compiler_params=_cparams(("arbitrary",)),
        name="mix0_prompt",
    )(x_all, nm, win, lng, lnb, wsp, bs_full, bias_p, wout, nf, wr, br)


def _mix0_sample_kernel(x_ref, nm_ref, win_ref, lng_ref, lnb_ref, wcoef_ref, bcoef_ref,
                        ck_ref, cv_ref, bsc_ref, bsn_ref,
                        wout_ref, nf_ref, wr_ref, br_ref, cnt_in,
                        x1_in, h_in, ri_in, rg_in, tc_in,
                        x1_ref, h_ref, ri_ref, rg_ref, tc_ref, kn_ref, vn_ref, va_ref, cnt_ref,
                        q_scr, k_scr, v_scr, mix_scr):
    del x1_in, h_in, ri_in, rg_in, tc_in
    g = pl.program_id(0)

    @pl.when(g == 0)
    def _():
        u, va, q, k, v = _project(x_ref[...], nm_ref[...], win_ref[...], lng_ref[...], lnb_ref[...])
        q_scr[...] = q
        k_scr[...] = k
        v_scr[...] = v
        kn_ref[...] = k
        vn_ref[...] = v
        va_ref[...] = va
        idx = 0
        for t in range(DEC_SEQ):
            acc = jnp.zeros((DEC_BATCH, A_WIDTH), F32) + bcoef_ref[t:t + 1, :]
            for s in range(t + 1):
                acc = acc + wcoef_ref[idx:idx + 1, :] * va[s * DEC_BATCH:(s + 1) * DEC_BATCH]
                idx += 1
            mix_scr[t * DEC_BATCH:(t + 1) * DEC_BATCH, :A_WIDTH] = u[t * DEC_BATCH:(t + 1) * DEC_BATCH] * acc

    b0 = pl.multiple_of(g * SAMPLE_GROUP, SAMPLE_GROUP)
    lane_lo = lax.broadcasted_iota(jnp.int32, (DEC_SEQ * SAMPLE_GROUP, LANES), 1) < B_HEAD_DIM

    def grab(ref, width):
        return jnp.concatenate([ref[pl.ds(t * DEC_BATCH + b0, SAMPLE_GROUP), :] for t in range(DEC_SEQ)], axis=0)

    qg = grab(q_scr, Q_WIDTH)
    kn = grab(k_scr, KV_WIDTH)
    vn = grab(v_scr, KV_WIDTH)
    crow0 = lax.broadcasted_iota(jnp.int32, (SAMPLE_GROUP * WINDOW, KV_WIDTH), 0) == 0
    rows_kv = (SAMPLE_GROUP * WINDOW, KV_WIDTH)
    kc = jnp.where(crow0, 0.0, ck_ref[...].reshape(rows_kv))
    vc = jnp.where(crow0, 0.0, cv_ref[...].reshape(rows_kv))
    kc_ops = (kc.astype(BF16), pltpu.roll(kc, B_HEAD_DIM, 1).astype(BF16))
    vc_ops = (vc.astype(BF16), pltpu.roll(vc, B_HEAD_DIM, 1).astype(BF16))
    kn_ops = (kn.astype(BF16), pltpu.roll(kn, B_HEAD_DIM, 1).astype(BF16))
    vn_ops = (vn.astype(BF16), pltpu.roll(vn, B_HEAD_DIM, 1).astype(BF16))
    qt = [qg[:, p * LANES:(p + 1) * LANES] for p in range(4)]
    q_even = [jnp.where(lane_lo, t, 0.0) for t in qt]
    q_odd = [jnp.where(lane_lo, 0.0, t) for t in qt]
    stacks = (jnp.concatenate([q_even[0], q_even[1], q_odd[2], q_odd[3]], axis=0),
              jnp.concatenate([q_odd[0], q_odd[1], q_even[2], q_even[3]], axis=0))
    o = []
    for st in range(2):
        qs = stacks[st].astype(BF16)
        sc = _dot_nt(qs, kc_ops[st]) + bsc_ref[st]
        sn = _dot_nt(qs, kn_ops[st]) + bsn_ref[st][:, :DEC_SEQ * SAMPLE_GROUP]
        m = jnp.maximum(jnp.max(sc, axis=-1, keepdims=True), jnp.max(sn, axis=-1, keepdims=True))
        pc = jnp.exp(sc - m)
        pn = jnp.exp(sn - m)
        den = jnp.sum(pc, axis=-1, keepdims=True) + jnp.sum(pn, axis=-1, keepdims=True)
        o.append((_dot(pc.astype(BF16), vc_ops[st]) + _dot(pn.astype(BF16), vn_ops[st])) / den)
    oa, ob = o
    n = DEC_SEQ * SAMPLE_GROUP
    sl = [slice(i * n, (i + 1) * n) for i in range(4)]
    tiles = (jnp.where(lane_lo, oa[sl[0]], ob[sl[0]]), jnp.where(lane_lo, oa[sl[1]], ob[sl[1]]),
             jnp.where(lane_lo, ob[sl[2]], oa[sl[2]]), jnp.where(lane_lo, ob[sl[3]], oa[sl[3]]))
    for p in range(4):
        for t in range(DEC_SEQ):
            mix_scr[pl.ds(t * DEC_BATCH + b0, SAMPLE_GROUP), A_WIDTH + p * LANES:A_WIDTH + (p + 1) * LANES] = (
                tiles[p][t * SAMPLE_GROUP:(t + 1) * SAMPLE_GROUP])

    @pl.when(g == N_SAMPLE_GROUPS - 1)
    def _():
        x1 = x_ref[...] + _dot(mix_scr[...].astype(BF16), wout_ref[...])
        x1_ref[...] = x1
        h, ids, gates = _route(x1, nf_ref[...], wr_ref[...], br_ref[...])
        h_ref[...] = h.reshape(h_ref.shape)
        cnt_ref[...] = cnt_in[...]
        ri_ref[...] = _rank_pack(ids, cnt_ref, tc_ref)
        rg_ref[...] = gates


def _mix0_sample(x_all, nm, win, lng, lnb, wcoef, bcoef, ck, cv, bias_sc, bias_sn, wout, nf, wr, br, cnt,
                 x1_all, h_all, ri_all, rg_all, tc_all):
    sample_rows = pl.BlockSpec((TM, D_MODEL), lambda g: (N_PROMPT_BLOCKS, 0))
    sample_rows3 = pl.BlockSpec((TM, ROW_TILE, LANES), lambda g: (N_PROMPT_BLOCKS, 0, 0))
    sample_lanes = pl.BlockSpec((TM, LANES), lambda g: (N_PROMPT_BLOCKS, 0))
    cache_spec = pl.BlockSpec((None, SAMPLE_GROUP, WINDOW, B_KV_HEADS, B_HEAD_DIM), lambda g: (0, g, 0, 0, 0))
    anyspec = pl.BlockSpec(memory_space=pl.ANY)
    n_in = 16
    return pl.pallas_call(
        _mix0_sample_kernel,
        grid=(N_SAMPLE_GROUPS,),
        in_specs=[_const_spec((TM, D_MODEL)), _const_spec((1, D_MODEL)), _const_spec((D_MODEL, IN_WIDTH)),
                  _const_spec((1, A_WIDTH)), _const_spec((1, A_WIDTH)),
                  _const_spec((16, A_WIDTH)), _const_spec((8, A_WIDTH)),
                  cache_spec, cache_spec,
                  _const_spec((2, 4 * 32, SAMPLE_GROUP * WINDOW)), _const_spec((2, 4 * 32, LANES)),
                  _const_spec((A_WIDTH + Q_WIDTH, D_MODEL)), _const_spec((1, D_MODEL)),
                  _const_spec((D_MODEL, 2 * LANES)), _const_spec((1, LANES)), _const_spec((1, LANES)),
                  anyspec, anyspec, anyspec, anyspec, anyspec],
        out_specs=[sample_rows, sample_rows3, pl.BlockSpec((8, TM), lambda g: (0, N_PROMPT_BLOCKS)), sample_lanes,
                   pl.BlockSpec((None, 1, LANES), lambda g: (N_PROMPT_BLOCKS, 0, 0)),
                   _const_spec((T_SAMPLE, KV_WIDTH)), _const_spec((T_SAMPLE, KV_WIDTH)),
                   _const_spec((T_SAMPLE, A_WIDTH)), _const_spec((1, LANES))],
        out_shape=[jax.ShapeDtypeStruct((T_ALL, D_MODEL), F32), jax.ShapeDtypeStruct((T_ALL, ROW_TILE, LANES), BF16),
                   jax.ShapeDtypeStruct((8, T_ALL), jnp.int32), jax.ShapeDtypeStruct((T_ALL, LANES), F32),
                   jax.ShapeDtypeStruct((N_ROW_BLOCKS, 1, LANES), F32),
                   jax.ShapeDtypeStruct((T_SAMPLE, KV_WIDTH), F32), jax.ShapeDtypeStruct((T_SAMPLE, KV_WIDTH), F32),
                   jax.ShapeDtypeStruct((T_SAMPLE, A_WIDTH), F32), jax.ShapeDtypeStruct((1, LANES), F32)],
        scratch_shapes=[pltpu.VMEM((T_SAMPLE, Q_WIDTH), F32), pltpu.VMEM((T_SAMPLE, KV_WIDTH), F32),
                        pltpu.VMEM((T_SAMPLE, KV_WIDTH), F32), pltpu.VMEM((T_SAMPLE, D_MODEL), F32)],
        input_output_aliases={n_in: 0, n_in + 1: 1, n_in + 2: 2, n_in + 3: 3, n_in + 4: 4},
        compiler_params=_cparams(("arbitrary",)),
        name="mix0_sample",
    )(x_all, nm, win, lng, lnb, wcoef, bcoef, ck, cv, bias_sc, bias_sn, wout, nf, wr, br, cnt,
      x1_all, h_all, ri_all, rg_all, tc_all)


def _moe_metadata(rt_all, cnt, tcnt):
    counts = cnt[0, :N_EXPERTS].astype(jnp.int32)
    padded = (counts + MOE_BLK - 1) // MOE_BLK * MOE_BLK
    pad_end = jnp.cumsum(padded)
    pad_start = pad_end - padded
    experts = jnp.arange(N_EXPERTS, dtype=jnp.int32)
    n_valid = (pad_end[-1] // MOE_BLK).astype(jnp.int32).reshape(1)
    blk_start = jnp.arange(N_MOE_BLOCKS, dtype=jnp.int32) * MOE_BLK
    block_e = jnp.minimum(jnp.sum((blk_start[:, None] >= pad_end[None, :]).astype(jnp.int32), axis=1),
                          N_EXPERTS - 1).astype(jnp.int32)
    zero_start = (pad_start + counts).astype(jnp.int32)
    zero_len = (padded - counts).astype(jnp.int32)
    first = (blk_start == pad_start[block_e]).astype(jnp.int32)
    used = counts > 0
    parity = ((jnp.cumsum(used.astype(jnp.int32)) - 1) % 2)[block_e].astype(jnp.int32)
    nearest = lax.cummin(jnp.where(used, experts, N_EXPERTS)[::-1])[::-1]
    next_used = jnp.concatenate([nearest[1:], jnp.full((1,), N_EXPERTS, jnp.int32)])
    nxt = jnp.where(next_used < N_EXPERTS, next_used, -1)[block_e].astype(jnp.int32)
    plan = (block_e, first, parity, nxt, n_valid)
    runs = tcnt[:, 0, :N_EXPERTS].astype(jnp.int32)
    run_dst = pad_start[None, :] + jnp.cumsum(runs, axis=0) - runs
    lpos = rt_all[2 * TOP_K:3 * TOP_K].reshape(N_SLOTS).astype(jnp.int32)
    cplan = (lpos, runs.reshape(-1), run_dst.reshape(-1).astype(jnp.int32))
    dplan = cplan + (jnp.concatenate([zero_start, zero_len, n_valid]),)
    return plan, dplan, cplan


RUN_SMALL = 64


def _for_run_pieces(n, start_piece):
    big = n - (n & (RUN_SMALL - 1))

    @pl.when(big != 0)
    def _():
        o = 0
        bit = TM * TOP_K
        while bit >= RUN_SMALL:
            take = (big & bit) != 0

            @pl.when(take)
            def _(o=o, bit=bit):
                start_piece(o, bit)

            o = o + jnp.where(take, bit, 0)
            bit //= 2

    o = big
    bit = RUN_SMALL // 2
    while bit >= 1:
        take = (n & bit) != 0

        @pl.when(take)
        def _(o=o, bit=bit):
            start_piece(o, bit)

        o = o + jnp.where(take, bit, 0)
        bit //= 2


def _dispatch_kernel(lpos_ref, run_ref, rdst_ref, zs_ref, h_ref, xs_ref, zero_scr, stage, sem, zsem):
    i = pl.program_id(0)

    @pl.when(i == 0)
    def _():
        zero_scr[...] = jnp.zeros_like(zero_scr)

        def pieces(e, do):
            off = zs_ref[e]
            rem = zs_ref[N_EXPERTS + e]
            bit = MOE_BLK // 2
            while bit >= 1:
                take = (rem & bit) != 0

                @pl.when(take)
                def _(off=off, bit=bit):
                    do(pltpu.make_async_copy(zero_scr.at[pl.ds(0, bit)], xs_ref.at[pl.ds(off, bit)], zsem))

                off = off + jnp.where(take, bit, 0)
                bit //= 2

        def start_e(e, c):
            pieces(e, lambda cp: cp.start())
            return c

        def wait_e(e, c):
            pieces(e, lambda cp: cp.wait())
            return c

        def tail(do):
            def step(b, c):
                do(pltpu.make_async_copy(zero_scr, xs_ref.at[pl.ds(b * MOE_BLK, MOE_BLK)], zsem))
                return c
            return step

        n_valid = zs_ref[2 * N_EXPERTS]
        lax.fori_loop(0, N_EXPERTS, start_e, 0)
        lax.fori_loop(n_valid, N_MOE_BLOCKS, tail(lambda cp: cp.start()), 0)
        lax.fori_loop(0, N_EXPERTS, wait_e, 0)
        lax.fori_loop(n_valid, N_MOE_BLOCKS, tail(lambda cp: cp.wait()), 0)

    base = i * TM

    buf = i % 2

    def place(r, carry):
        row = h_ref[r]
        for kk in range(TOP_K):
            stage[buf, lpos_ref[kk * T_ALL + base + r]] = row
        return carry

    lax.fori_loop(0, TM, place, 0, unroll=8)

    def wait_tile(b):
        pltpu.make_async_copy(stage.at[b], xs_ref.at[pl.ds(0, TM * TOP_K)], sem.at[b]).wait()

    @pl.when(i >= 1)
    def _():
        wait_tile(1 - buf)

    def send_run(e, off):
        n = run_ref[i * N_EXPERTS + e]
        dst = rdst_ref[i * N_EXPERTS + e]
        _for_run_pieces(n, lambda o, size: pltpu.make_async_copy(
            stage.at[buf, pl.ds(off + o, size)], xs_ref.at[pl.ds(dst + o, size)], sem.at[buf]).start(
                priority=size.bit_length() % 2))
        return off + n

    lax.fori_loop(0, N_EXPERTS, send_run, 0)

    @pl.when(i == N_ROW_BLOCKS - 1)
    def _():
        wait_tile(buf)


def _dispatch(dplan, h_all):
    return pl.pallas_call(
        _dispatch_kernel,
        grid_spec=pltpu.PrefetchScalarGridSpec(
            num_scalar_prefetch=4,
            grid=(N_ROW_BLOCKS,),
            in_specs=[pl.BlockSpec((TM, ROW_TILE, LANES), lambda i, lp, rn, rd, z: (i, 0, 0))],
            out_specs=pl.BlockSpec(memory_space=pl.ANY),
            scratch_shapes=[pltpu.VMEM((MOE_BLK, ROW_TILE, LANES), BF16),
                            pltpu.VMEM((2, TM * TOP_K, ROW_TILE, LANES), BF16),
                            pltpu.SemaphoreType.DMA((2,)), pltpu.SemaphoreType.DMA(())],
        ),
        out_shape=jax.ShapeDtypeStruct((N_SORT_ROWS, ROW_TILE, LANES), BF16),
        compiler_params=_cparams(("arbitrary",)),
        name="moe_dispatch",
    )(*dplan, h_all)


def _experts_kernel(layer, be_ref, first_ref, par_ref, nxt_ref, nv_ref,
                    x_ref, wg_hbm, wu_hbm, wd_hbm, y_ref,
                    wg_s, wu_s, wd_s, wg_f, wu_f, wd_f, wsem):
    i = pl.program_id(0)

    def fetch(e, slot):
        return (pltpu.make_async_copy(wg_hbm.at[layer, e], wg_f.at[slot], wsem.at[slot]),
                pltpu.make_async_copy(wu_hbm.at[layer, e], wu_f.at[slot], wsem.at[slot]),
                pltpu.make_async_copy(wd_hbm.at[layer, e], wd_f.at[slot], wsem.at[slot]))

    @pl.when(i < nv_ref[0])
    def _():
        e = be_ref[i]
        slot = par_ref[i]

        @pl.when(i == 0)
        def _():
            for cp in fetch(e, slot):
                cp.start()

        @pl.when(first_ref[i] == 1)
        def _():
            for cp in fetch(e, slot):
                cp.wait()
            wg_s[...] = wg_f[slot].astype(BF16)
            wu_s[...] = wu_f[slot].astype(BF16)
            wd_s[...] = wd_f[slot].astype(BF16)
            nxt = nxt_ref[i]

            @pl.when(nxt >= 0)
            def _():
                for cp in fetch(nxt, 1 - slot):
                    cp.start()

        xb = x_ref[...].reshape(MOE_BLK, D_MODEL)
        a = jax.nn.silu(_dot(xb, wg_s[...])) * _dot(xb, wu_s[...])
        y_ref[...] = _dot(a.astype(BF16), wd_s[...]).reshape(y_ref.shape)

    @pl.when(i >= nv_ref[0])
    def _():
        y_ref[...] = jnp.zeros(y_ref.shape, y_ref.dtype)


def _experts(block_e, first, parity, nxt, n_valid, xs, w_gate, w_up, w_down, layer):
    def blk(i, be, fi, pa, nx, nv):
        return (jnp.maximum(jnp.minimum(i, nv[0] - 1), 0), 0, 0)

    anyspec = pl.BlockSpec(memory_space=pl.ANY)
    return pl.pallas_call(
        functools.partial(_experts_kernel, layer),
        grid_spec=pltpu.PrefetchScalarGridSpec(
            num_scalar_prefetch=5,
            grid=(N_MOE_BLOCKS,),
            in_specs=[pl.BlockSpec((MOE_BLK, ROW_TILE, LANES), blk), anyspec, anyspec, anyspec],
            out_specs=pl.BlockSpec((MOE_BLK, ROW_TILE, LANES), lambda i, be, fi, pa, nx, nv: (i, 0, 0)),
            scratch_shapes=[pltpu.VMEM((D_MODEL, D_EXPERT), BF16), pltpu.VMEM((D_MODEL, D_EXPERT), BF16),
                            pltpu.VMEM((D_EXPERT, D_MODEL), BF16),
                            pltpu.VMEM((2, D_MODEL, D_EXPERT), F32), pltpu.VMEM((2, D_MODEL, D_EXPERT), F32),
                            pltpu.VMEM((2, D_EXPERT, D_MODEL), F32), pltpu.SemaphoreType.DMA((2,))],
        ),
        out_shape=jax.ShapeDtypeStruct((N_SORT_ROWS, ROW_TILE, LANES), F32),
        compiler_params=_cparams(("arbitrary",)),
        name="moe_experts",
    )(block_e, first, parity, nxt, n_valid, xs, w_gate, w_up, w_down)


def _gather_rows(lpos_ref, run_ref, rdst_ref, ys_ref, ystage, ybuf, sem, i):
    def fetch(tile, buf):
        def fetch_run(e, off):
            n = run_ref[tile * N_EXPERTS + e]
            src = rdst_ref[tile * N_EXPERTS + e]
            _for_run_pieces(n, lambda o, size: pltpu.make_async_copy(
                ys_ref.at[pl.ds(src + o, size)], ystage.at[buf, pl.ds(off + o, size)], sem.at[buf]).start(
                    priority=size.bit_length() % 2))
            return off + n

        lax.fori_loop(0, N_EXPERTS, fetch_run, 0)

    buf = i % 2

    @pl.when(i == 0)
    def _():
        fetch(i, buf)

    @pl.when(i + 1 < N_ROW_BLOCKS)
    def _():
        fetch(i + 1, 1 - buf)

    pltpu.make_async_copy(ys_ref.at[pl.ds(0, TM * TOP_K)], ystage.at[buf], sem.at[buf]).wait()
    base = i * TM

    def unplace(r, carry):
        for kk in range(TOP_K):
            ybuf[kk, r] = ystage[buf, lpos_ref[kk * T_ALL + base + r]]
        return carry

    lax.fori_loop(0, TM, unplace, 0, unroll=8)


def _combined(x_ref, rg_ref, ybuf):
    rg = rg_ref[...]
    y0 = ybuf[0].reshape(TM, D_MODEL)
    y1 = ybuf[1].reshape(TM, D_MODEL)
    return x_ref[...] + rg[:, 0:1] * y0 + rg[:, 1:2] * y1


_COMBINE_SCRATCH = [pltpu.VMEM((2, TM * TOP_K, ROW_TILE, LANES), F32), pltpu.VMEM((TOP_K, TM, ROW_TILE, LANES), F32),
                    pltpu.SemaphoreType.DMA((2,))]


def _combine_kernel(lpos_ref, run_ref, rdst_ref, x_ref, rg_ref, ys_ref, o_ref, ystage, ybuf, sem):
    _gather_rows(lpos_ref, run_ref, rdst_ref, ys_ref, ystage, ybuf, sem, pl.program_id(0))
    o_ref[...] = _combined(x_ref, rg_ref, ybuf)


def _combine(cplan, x_all, rg_all, ys):
    return pl.pallas_call(
        _combine_kernel,
        grid_spec=pltpu.PrefetchScalarGridSpec(
            num_scalar_prefetch=3,
            grid=(N_ROW_BLOCKS,),
            in_specs=[pl.BlockSpec((TM, D_MODEL), lambda i, a, b, c: (i, 0)),
                      pl.BlockSpec((TM, LANES), lambda i, a, b, c: (i, 0)),
                      pl.BlockSpec(memory_space=pl.ANY)],
            out_specs=pl.BlockSpec((TM, D_MODEL), lambda i, a, b, c: (i, 0)),
            scratch_shapes=_COMBINE_SCRATCH,
        ),
        out_shape=jax.ShapeDtypeStruct((T_ALL, D_MODEL), F32),
        compiler_params=_cparams(("arbitrary",)),
        name="moe_combine",
    )(*cplan, x_all, rg_all, ys)


def _final_kernel(lpos_ref, run_ref, rdst_ref, x_ref, rg_ref, ys_ref, nfin_ref, op_ref, os_ref, ystage, ybuf, sem):
    i = pl.program_id(0)
    _gather_rows(lpos_ref, run_ref, rdst_ref, ys_ref, ystage, ybuf, sem, i)
    y = _rms(_combined(x_ref, rg_ref, ybuf), nfin_ref[...])

    @pl.when(i < N_PROMPT_BLOCKS)
    def _():
        op_ref[...] = y

    @pl.when(i >= N_PROMPT_BLOCKS)
    def _():
        os_ref[...] = y


def _final(cplan, x_all, rg_all, ys, nfin):
    return pl.pallas_call(
        _final_kernel,
        grid_spec=pltpu.PrefetchScalarGridSpec(
            num_scalar_prefetch=3,
            grid=(N_ROW_BLOCKS,),
            in_specs=[pl.BlockSpec((TM, D_MODEL), lambda i, a, b, c: (i, 0)),
                      pl.BlockSpec((TM, LANES), lambda i, a, b, c: (i, 0)),
                      pl.BlockSpec(memory_space=pl.ANY),
                      pl.BlockSpec((1, D_MODEL), lambda i, a, b, c: (0, 0))],
            out_specs=[pl.BlockSpec((TM, D_MODEL), lambda i, a, b, c: (jnp.minimum(i, N_PROMPT_BLOCKS - 1), 0)),
                       pl.BlockSpec((TM, D_MODEL), lambda i, a, b, c: (0, 0))],
            scratch_shapes=_COMBINE_SCRATCH,
        ),
        out_shape=[jax.ShapeDtypeStruct((T_PROMPT, D_MODEL), F32), jax.ShapeDtypeStruct((T_SAMPLE, D_MODEL), F32)],
        compiler_params=_cparams(("arbitrary",)),
        name="moe_combine_final",
    )(*cplan, x_all, rg_all, ys, nfin)


def _moe(h_all, rt_all, cnt, tcnt, w_gate, w_up, w_down, layer):
    plan, dplan, cplan = _moe_metadata(rt_all, cnt, tcnt)
    xs = _dispatch(dplan, h_all)
    ys = _experts(*plan, xs, w_gate, w_up, w_down, layer)
    return cplan, ys


def _pool_project(d_groups, wp_ref, scale):
    outs = [_dot(d_groups[g].astype(BF16), wp_ref[g]) for g in range(len(POOL_SIZES))]
    return jnp.concatenate(outs, axis=1) * scale


def _mix1_prompt_kernel(x_ref, nm_ref, wp_ref, sc_ref, nf_ref, wr_ref, br_ref,
                        x3_ref, h_ref, ri_ref, rg_ref, tc_ref, pl_ref, cnt_ref, ext):
    i = pl.program_id(0)

    @pl.when(i == 0)
    def _():
        cnt_ref[...] = jnp.zeros_like(cnt_ref)

    x = x_ref[...]
    hp = _rms(x, nm_ref[...])

    @pl.when(i % STEPS_PER_BATCH == 0)
    def _():
        ext[0:POOL_MAX, :] = jnp.zeros((POOL_MAX, D_MODEL), F32)

    ext[POOL_MAX:, :] = hp
    pos = (i % STEPS_PER_BATCH) * TM + lax.broadcasted_iota(jnp.int32, (TM, 1), 0)
    d_groups = []
    for g, w in enumerate(POOL_SIZES):
        cols = slice(g * POOL_GROUP_DIM, (g + 1) * POOL_GROUP_DIM)
        acc = ext[:, cols]
        span = 1
        while span < w:
            acc = acc + pltpu.roll(acc, span, 0)
            span *= 2
        cnt = jnp.minimum(pos + 1, w).astype(F32)
        d_groups.append(acc[POOL_MAX:] / cnt - hp[:, cols])
    tail = hp[TM - POOL_MAX:, :]
    ext[0:POOL_MAX, :] = tail
    pl_ref[...] = tail

    x3 = x + _pool_project(d_groups, wp_ref, sc_ref[...])
    x3_ref[...] = x3
    h, ids, gates = _route(x3, nf_ref[...], wr_ref[...], br_ref[...])
    h_ref[...] = h.reshape(h_ref.shape)
    ri_ref[...] = _rank_pack(ids, cnt_ref, tc_ref)
    rg_ref[...] = gates


def _mix1_prompt(x_all, nm, wp, sc, nf, wr, br):
    row_spec = pl.BlockSpec((TM, D_MODEL), lambda i: (i, 0))
    row3_spec = pl.BlockSpec((TM, ROW_TILE, LANES), lambda i: (i, 0, 0))
    lane_spec = pl.BlockSpec((TM, LANES), lambda i: (i, 0))
    return pl.pallas_call(
        _prompt_steps(_mix1_prompt_kernel, 7),
        grid=(N_ROW_BLOCKS,),
        in_specs=[row_spec, _const_spec((1, D_MODEL)),
                  _const_spec((len(POOL_SIZES), POOL_GROUP_DIM, POOL_GROUP_DIM)), _const_spec((1, D_MODEL)),
                  _const_spec((1, D_MODEL)), _const_spec((D_MODEL, 2 * LANES)), _const_spec((1, LANES))],
        out_specs=[row_spec, row3_spec, pl.BlockSpec((8, TM), lambda i: (0, i)), lane_spec,
                   pl.BlockSpec((None, 1, LANES), lambda i: (i, 0, 0)),
                   pl.BlockSpec((None, POOL_MAX, D_MODEL),
                                lambda i: (jnp.minimum(i // STEPS_PER_BATCH, BATCH - 1), 0, 0)),
                   _const_spec((1, LANES))],
        out_shape=[jax.ShapeDtypeStruct((T_ALL, D_MODEL), F32), jax.ShapeDtypeStruct((T_ALL, ROW_TILE, LANES), BF16),
                   jax.ShapeDtypeStruct((8, T_ALL), jnp.int32), jax.ShapeDtypeStruct((T_ALL, LANES), F32),
                   jax.ShapeDtypeStruct((N_ROW_BLOCKS, 1, LANES), F32),
                   jax.ShapeDtypeStruct((BATCH, POOL_MAX, D_MODEL), F32), jax.ShapeDtypeStruct((1, LANES), F32)],
        scratch_shapes=[pltpu.VMEM((POOL_MAX + TM, D_MODEL), F32)],
        compiler_params=_cparams(("arbitrary",)),
        name="mix1_prompt",
    )(x_all, nm, wp, sc, nf, wr, br)


def _mix1_sample_kernel(x_ref, st_ref, nm_ref, wp_ref, sc_ref, nf_ref, wr_ref, br_ref, cnt_in,
                        x3_in, h_in, ri_in, rg_in, tc_in,
                        x3_ref, h_ref, ri_ref, rg_ref, tc_ref, hs_ref, cnt_ref):
    del x3_in, h_in, ri_in, rg_in, tc_in
    x = x_ref[...]
    hs = _rms(x, nm_ref[...])
    hs_ref[...] = hs
    n_ctx = POOL_MAX - 1
    d_groups = []
    for g, w in enumerate(POOL_SIZES):
        cols = slice(g * POOL_GROUP_DIM, (g + 1) * POOL_GROUP_DIM)
        parts = []
        for t in range(DEC_SEQ):
            acc = hs[t * DEC_BATCH:(t + 1) * DEC_BATCH, cols]
            for back in range(1, w):
                src = t - back
                if src >= 0:
                    acc = acc + hs[src * DEC_BATCH:(src + 1) * DEC_BATCH, cols]
                else:
                    acc = acc + st_ref[n_ctx + src, :, cols]
            parts.append(acc / float(w) - hs[t * DEC_BATCH:(t + 1) * DEC_BATCH, cols])
        d_groups.append(jnp.concatenate(parts, axis=0))
    x3 = x + _pool_project(d_groups, wp_ref, sc_ref[...])
    x3_ref[...] = x3
    h, ids, gates = _route(x3, nf_ref[...], wr_ref[...], br_ref[...])
    h_ref[...] = h.reshape(h_ref.shape)
    cnt_ref[...] = cnt_in[...]
    ri_ref[...] = _rank_pack(ids, cnt_ref, tc_ref)
    rg_ref[...] = gates


def _mix1_sample(x_all, state_t, nm, wp, sc, nf, wr, br, cnt, x3_all, h_all, ri_all, rg_all, tc_all):
    sample_rows = pl.BlockSpec((TM, D_MODEL), lambda g: (N_PROMPT_BLOCKS, 0))
    sample_rows3 = pl.BlockSpec((TM, ROW_TILE, LANES), lambda g: (N_PROMPT_BLOCKS, 0, 0))
    sample_lanes = pl.BlockSpec((TM, LANES), lambda g: (N_PROMPT_BLOCKS, 0))
    anyspec = pl.BlockSpec(memory_space=pl.ANY)
    n_in = 9
    return pl.pallas_call(
        _mix1_sample_kernel,
        grid=(1,),
        in_specs=[sample_rows, _const_spec((POOL_MAX - 1, DEC_BATCH, D_MODEL)), _const_spec((1, D_MODEL)),
                  _const_spec((len(POOL_SIZES), POOL_GROUP_DIM, POOL_GROUP_DIM)), _const_spec((1, D_MODEL)),
                  _const_spec((1, D_MODEL)), _const_spec((D_MODEL, 2 * LANES)), _const_spec((1, LANES)),
                  _const_spec((1, LANES)), anyspec, anyspec, anyspec, anyspec, anyspec],
        out_specs=[sample_rows, sample_rows3, pl.BlockSpec((8, TM), lambda g: (0, N_PROMPT_BLOCKS)), sample_lanes,
                   pl.BlockSpec((None, 1, LANES), lambda g: (N_PROMPT_BLOCKS, 0, 0)),
                   _const_spec((T_SAMPLE, D_MODEL)), _const_spec((1, LANES))],
        out_shape=[jax.ShapeDtypeStruct((T_ALL, D_MODEL), F32), jax.ShapeDtypeStruct((T_ALL, ROW_TILE, LANES), BF16),
                   jax.ShapeDtypeStruct((8, T_ALL), jnp.int32), jax.ShapeDtypeStruct((T_ALL, LANES), F32),
                   jax.ShapeDtypeStruct((N_ROW_BLOCKS, 1, LANES), F32),
                   jax.ShapeDtypeStruct((T_SAMPLE, D_MODEL), F32), jax.ShapeDtypeStruct((1, LANES), F32)],
        input_output_aliases={n_in: 0, n_in + 1: 1, n_in + 2: 2, n_in + 3: 3, n_in + 4: 4},
        compiler_params=_cparams(("arbitrary",)),
        name="mix1_sample",
    )(x_all, state_t, nm, wp, sc, nf, wr, br, cnt, x3_all, h_all, ri_all, rg_all, tc_all)


def _router_weights(wg, bg, we, be):
    w = jnp.concatenate([wg, jnp.transpose(we, (1, 0, 2)).reshape(D_MODEL, N_EXPERTS)], axis=1)
    b = jnp.concatenate([bg, be.reshape(N_EXPERTS)])
    pad = LANES - N_GROUPS - N_EXPERTS
    w = jnp.pad(w, ((0, 0), (0, pad)))
    w_hi = w.astype(BF16)
    w_lo = (w - w_hi.astype(F32)).astype(BF16)
    return jnp.concatenate([w_hi, w_lo], axis=1), jnp.pad(b, (0, pad)).reshape(1, LANES)


def _stack(tab):
    return jnp.stack([jnp.concatenate([tab[h] for h in heads], axis=0) for heads in STACK_HEADS])


def kernel(x_prompt, x_sample, cache_k_win, cache_v_win, state_pool, norm_mix, norm_ffn, norm_final, w_in,
           a_ln_g, a_ln_b, a_w_s, a_b_s, b_sinks, rel_bias_table, w_out, c_w_pool, c_scale,
           router_group_w, router_group_b, router_expert_w, router_expert_b, w_gate, w_up, w_down):
    xs_t = jnp.transpose(x_sample, (1, 0, 2)).reshape(T_SAMPLE, D_MODEL)
    xp2 = x_prompt.reshape(T_PROMPT, D_MODEL)
    win =w_in[0].astype(BF16)
    wout = w_out[0].astype(BF16)
    lng = a_ln_g[0].reshape(1, A_WIDTH)
    lnb = a_ln_b[0].reshape(1, A_WIDTH)
    bias_p, bias_sc, bias_sn, ws_tril = _prep(rel_bias_table, a_w_s[0])
    wsp = ws_tril.reshape(A_HEADS // 2, 2, CHUNK, CHUNK).transpose(0, 2, 1, 3).reshape(A_HEADS // 2, CHUNK, 2 * CHUNK)
    bs_full = jnp.repeat(a_b_s[0].T, A_HEAD_DIM, axis=1)
    bias_p = jnp.stack([_stack(bias_p[0]), _stack(bias_p[1])])
    bias_sc = _stack(bias_sc)
    bias_sn = _stack(bias_sn)
    sinks = b_sinks[0]
    sink_p = jnp.stack([jnp.repeat(sinks[jnp.array(hh)], WINDOW) for hh in STACK_HEADS])
    sink_s = jnp.stack([jnp.repeat(sinks[jnp.array(hh)], 32) for hh in STACK_HEADS])
    bias_p = bias_p.at[:, :, :, 0].set(jnp.broadcast_to(sink_p[None], (2, 2, 4 * WINDOW)))
    bias_sc = bias_sc.at[:, :, 0].set(sink_s)
    pairs = [(t, s) for t in range(DEC_SEQ) for s in range(t + 1)]
    wcoef = jnp.stack([jnp.repeat(a_w_s[0][:, t, s], A_HEAD_DIM) for t, s in pairs])
    wcoef = jnp.pad(wcoef, ((0, 16 - len(pairs)), (0, 0)))
    bcoef = jnp.pad(jnp.repeat(a_b_s[0][:, :DEC_SEQ].T, A_HEAD_DIM, axis=1), ((0, 8 - DEC_SEQ), (0, 0)))
    ck = cache_k_win
    cv = cache_v_win
    routers = [_router_weights(router_group_w[l], router_group_b[l], router_expert_w[l], router_expert_b[l])
               for l in range(2)]
    nm = [norm_mix[l].reshape(1, D_MODEL) for l in range(2)]
    nf = [norm_ffn[l].reshape(1, D_MODEL) for l in range(2)]

    x1_all, h_all, ri_all, rg_all, tc_all, k_last, v_last, va_last, cnt0 = _mix0_prompt(
        xp2, nm[0], win, lng, lnb, wsp, bs_full, bias_p, wout, nf[0], *routers[0])
    x1_all, h_all, ri_all, rg_all, tc_all, k_new, v_new, va_s, cnt0 = _mix0_sample(
        xs_t, nm[0], win, lng, lnb, wcoef, bcoef, ck, cv, bias_sc, bias_sn, wout, nf[0], *routers[0], cnt0,
        x1_all, h_all, ri_all, rg_all, tc_all)
    cplan0, ys0 = _moe(h_all, ri_all, cnt0, tc_all, w_gate, w_up, w_down, 0)
    x2_all = _combine(cplan0, x1_all, rg_all, ys0)

    wp = c_w_pool[0].astype(BF16)
    sc = c_scale[0].reshape(1, D_MODEL)
    x3_all, h2_all, ri2_all, rg2_all, tc2_all, pool_tail, cnt1 = _mix1_prompt(
        x2_all, nm[1], wp, sc, nf[1], *routers[1])
    state_t = jnp.transpose(state_pool[0], (1, 0, 2))
    x3_all, h2_all, ri2_all, rg2_all, tc2_all, hs1, cnt1 = _mix1_sample(
        x2_all, state_t, nm[1], wp, sc, nf[1], *routers[1], cnt1, x3_all, h2_all, ri2_all, rg2_all, tc2_all)
    cplan1, ys1 = _moe(h2_all, ri2_all, cnt1, tc2_all, w_gate, w_up, w_down, 1)
    y_p, y_s = _final(cplan1, x3_all, rg2_all, ys1, norm_final.reshape(1, D_MODEL))

    def from_tmajor(a, width):
        return jnp.transpose(a.reshape(DEC_SEQ, DEC_BATCH, width), (1, 0, 2))

    y_prompt = y_p.reshape(BATCH, SEQ, D_MODEL)
    y_sample = from_tmajor(y_s, D_MODEL)
    win_k_p = k_last.reshape(1, BATCH, WINDOW, B_KV_HEADS, B_HEAD_DIM)
    win_v_p = v_last.reshape(1, BATCH, WINDOW, B_KV_HEADS, B_HEAD_DIM)
    kn = from_tmajor(k_new, KV_WIDTH).reshape(DEC_BATCH, DEC_SEQ, B_KV_HEADS, B_HEAD_DIM)
    vn = from_tmajor(v_new, KV_WIDTH).reshape(DEC_BATCH, DEC_SEQ, B_KV_HEADS, B_HEAD_DIM)
    win_k_s = jnp.concatenate([cache_k_win[0][:, DEC_SEQ:], kn], axis=1)[None]
    win_v_s = jnp.concatenate([cache_v_win[0][:, DEC_SEQ:], vn], axis=1)[None]
    chunk_v_p = va_last.reshape(1, BATCH, CHUNK, A_HEADS, A_HEAD_DIM)
    chunk_v_s = from_tmajor(va_s, A_WIDTH).reshape(1, DEC_BATCH, DEC_SEQ, A_HEADS, A_HEAD_DIM)
    pool_p = pool_tail[:, 1:][None]
    pool_s = jnp.concatenate([state_pool[0][:, DEC_SEQ:], from_tmajor(hs1, D_MODEL)], axis=1)[None]
    return (y_prompt, y_sample, win_k_p, win_v_p, win_k_s, win_v_s, chunk_v_p, chunk_v_s, pool_p, pool_s)
```

```python
import functools
import math

import numpy as np
import jax
import jax.numpy as jnp
from jax import lax
from jax.experimental import pallas as pl
from jax.experimental.pallas import tpu as pltpu

F32 = jnp.float32
BF16 = jnp.bfloat16

D_MODEL = 1024
BATCH = 2
SEQ = 8192
DEC_BATCH = 128
DEC_SEQ = 4
A_WIDTH = 512
A_HEADS = 8
A_HEAD_DIM = 64
CHUNK = 128
B_HEADS = 8
B_KV_HEADS = 2
B_HEAD_DIM = 64
B_GROUP = 4
WINDOW = 128
N_BUCKETS = 32
MAX_DISTANCE = WINDOW
Q_WIDTH = 512
KV_WIDTH = 128
IN_WIDTH = 2 * A_WIDTH + Q_WIDTH + 2 * KV_WIDTH
ATTN_SCALE = B_HEAD_DIM ** -0.5
NEG_INF = -1e30
POOL_SIZES = (2, 4, 8, 16)
POOL_GROUP_DIM = 256
POOL_MAX = 16
N_GROUPS = 4
EXPERTS_PER_GROUP = 8
N_EXPERTS = 32
TOP_K = 2
D_EXPERT = 512
EPS = 1e-6

LANES = 128
ROW_TILE = D_MODEL // LANES
T_PROMPT = BATCH * SEQ
T_SAMPLE = DEC_BATCH * DEC_SEQ
T_ALL = T_PROMPT + T_SAMPLE
TM = 512
N_PROMPT_BLOCKS = T_PROMPT // TM
N_ROW_BLOCKS = T_ALL // TM
STEPS_PER_BATCH = SEQ // TM
SUB = TM // WINDOW
N_SLOTS = T_ALL * TOP_K
MOE_BLK = 512
N_MOE_BLOCKS = N_SLOTS // MOE_BLK + N_EXPERTS
N_SORT_ROWS = N_MOE_BLOCKS * MOE_BLK
SAMPLE_GROUP = 8
N_SAMPLE_GROUPS = DEC_BATCH // SAMPLE_GROUP
VMEM_LIMIT = 56 * 1024 * 1024

STACK_HEADS = ((0, 2, 5, 7), (1, 3, 4, 6))


def _t5_bucket_np(dist):
    n = np.maximum(dist, 0)
    max_exact = N_BUCKETS // 2
    nf = np.maximum(n, 1).astype(np.float32)
    large = max_exact + (np.log(nf / np.float32(max_exact)) / np.float32(math.log(MAX_DISTANCE / max_exact))
                         * np.float32(N_BUCKETS - max_exact)).astype(np.int32)
    large = np.minimum(large, N_BUCKETS - 1)
    return np.where(n < max_exact, n, large).astype(np.int32)


def _bucket_tables():
    qi = np.arange(WINDOW)[:, None]
    ki = np.arange(2 * WINDOW)[None, :]
    dist = qi + WINDOW - ki
    valid = (dist >= 0) & (dist < WINDOW)
    bp = np.where(valid, _t5_bucket_np(dist), -1)
    bp_first = np.where(ki >= WINDOW, bp, -1)
    bkt_p = np.stack([bp_first, bp]).astype(np.int32)

    t = np.repeat(np.arange(DEC_SEQ), SAMPLE_GROUP)[:, None]
    b = np.tile(np.arange(SAMPLE_GROUP), DEC_SEQ)[:, None]
    cb = np.repeat(np.arange(SAMPLE_GROUP), WINDOW)[None, :]
    cj = np.tile(np.arange(WINDOW), SAMPLE_GROUP)[None, :]
    dist_c = t + WINDOW - cj
    valid_c = (cb == b) & (dist_c >= 0) & (dist_c < WINDOW)
    bkt_sc = np.where(valid_c, _t5_bucket_np(dist_c), -1).astype(np.int32)
    nt = np.repeat(np.arange(DEC_SEQ), SAMPLE_GROUP)[None, :]
    nb = np.tile(np.arange(SAMPLE_GROUP), DEC_SEQ)[None, :]
    dist_n = t - nt
    valid_n = (nb == b) & (dist_n >= 0)
    bkt_sn = np.where(valid_n, _t5_bucket_np(dist_n), -1).astype(np.int32)
    bkt_sn = np.concatenate([bkt_sn, np.full((32, LANES - 32), -1, np.int32)], axis=1)
    return bkt_p, bkt_sc, bkt_sn


_BKT_P, _BKT_SC, _BKT_SN = _bucket_tables()


def _cparams(semantics):
    return pltpu.CompilerParams(dimension_semantics=semantics, vmem_limit_bytes=VMEM_LIMIT)


def _rms(x, g):
    return x * lax.rsqrt(jnp.mean(x * x, axis=-1, keepdims=True) + EPS) * g


def _layernorm(x, g, b):
    xc = x - jnp.mean(x, axis=-1, keepdims=True)
    return xc * lax.rsqrt(jnp.mean(xc * xc, axis=-1, keepdims=True) + EPS) * g + b


def _dot(a, b):
    return jnp.dot(a, b, preferred_element_type=F32)


def _dot_nt(a, b):
    return lax.dot_general(a, b, (((1,), (1,)), ((), ())), preferred_element_type=F32)


def _project(x, nm, win, lng, lnb):
    h = _rms(x, nm)
    z = _dot(h.astype(BF16), win)
    u = jax.nn.gelu(z[:, :A_WIDTH])
    va = _layernorm(jax.nn.gelu(z[:, A_WIDTH:2 * A_WIDTH]), lng, lnb)
    q = z[:, 2 * A_WIDTH:2 * A_WIDTH + Q_WIDTH] * ATTN_SCALE
    k = z[:, 2 * A_WIDTH + Q_WIDTH:2 * A_WIDTH + Q_WIDTH + KV_WIDTH]
    v = z[:, 2 * A_WIDTH + Q_WIDTH + KV_WIDTH:]
    return u, va, q, k, v


def _route(x1, nf, wr, br):
    hf = _rms(x1, nf)
    h = hf.astype(BF16)
    h_lo = (hf - h.astype(F32)).astype(BF16)
    part = _dot(h, wr)
    logits = part[:, :LANES] + part[:, LANES:] + _dot(h_lo, wr[:, :LANES]) + br
    rows = logits.shape[0]
    lane = lax.broadcasted_iota(jnp.int32, (rows, LANES), 1)
    lanef = lane.astype(F32)
    big = jnp.float32(1e9)
    is_g = lane < N_GROUPS
    gl = jnp.where(is_g, logits, -jnp.inf)
    gmax = jnp.max(gl, axis=1, keepdims=True)
    gsel = jnp.min(jnp.where(gl == gmax, lanef, big), axis=1, keepdims=True)
    gsum = jnp.sum(jnp.where(is_g, jnp.exp(logits - gmax), 0.0), axis=1, keepdims=True)
    g1 = 1.0 / gsum
    lo = N_GROUPS + EXPERTS_PER_GROUP * gsel
    emask = (lanef >= lo) & (lanef < lo + EXPERTS_PER_GROUP)
    el = jnp.where(emask, logits, -jnp.inf)
    v1 = jnp.max(el, axis=1, keepdims=True)
    i1 = jnp.min(jnp.where(el == v1, lanef, big), axis=1, keepdims=True)
    el2 = jnp.where(lanef == i1, -jnp.inf, el)
    v2 = jnp.max(el2, axis=1, keepdims=True)
    i2 = jnp.min(jnp.where(el2 == v2, lanef, big), axis=1, keepdims=True)
    e2 = jnp.exp(v2 - v1)
    den = 1.0 + e2
    w1 = g1 / den
    w2 = g1 * e2 / den
    ids = jnp.where(lane == 0, i1 - N_GROUPS, jnp.where(lane == 1, i2 - N_GROUPS, 0.0)).astype(jnp.int32)
    gates = jnp.where(lane == 0, w1, jnp.where(lane == 1, w2, 0.0))
    return h, ids, gates


def _rank_pack(ids, cnt_ref, tcnt_ref):
    rows = ids.shape[0]
    lane = lax.broadcasted_iota(jnp.int32, (rows, LANES), 1)
    o0 = (lane == ids[:, 0:1]).astype(F32)
    o1 = (lane == ids[:, 1:2]).astype(F32)
    r = lax.broadcasted_iota(jnp.int32, (rows, rows), 0)
    c = lax.broadcasted_iota(jnp.int32, (rows, rows), 1)
    before = (c < r).astype(BF16)
    p01 = _dot(before, jnp.concatenate([o0, o1], axis=1).astype(BF16))
    p0 = p01[:, :LANES]
    p1 = p01[:, LANES:]
    c0 = jnp.sum(o0, axis=0, keepdims=True)
    c1 = jnp.sum(o1, axis=0, keepdims=True)
    ctile = c0 + c1
    cnt_ref[...] = cnt_ref[...] + ctile
    tcnt_ref[...] = ctile
    inc = jnp.broadcast_to(ctile, (8, LANES))
    lane8 = lax.broadcasted_iota(jnp.int32, (8, LANES), 1)
    for sh in (1, 2, 4, 8, 16, 32, 64):
        inc = inc + jnp.where(lane8 >= sh, pltpu.roll(inc, sh, 1), 0.0)
    start = inc[0:1] - ctile
    lpos0 = jnp.sum(o0 * (start + p0), axis=1, keepdims=True)
    lpos1 = jnp.sum(o1 * (start + c0 + p1), axis=1, keepdims=True)
    idf = ids.astype(F32)
    packed = jnp.where(lane < TOP_K, idf, 0.0)
    for ln, col in ((4, lpos0), (5, lpos1)):
        packed = jnp.where(lane == ln, col, packed)
    return jnp.transpose(packed)[:8].astype(jnp.int32)


def _prep_kernel(tab_ref, bp_ref, bsc_ref, bsn_ref, ws_ref, op_ref, osc_ref, osn_ref, ows_ref):
    def fill(bkt, write):
        for h in range(B_HEADS):
            acc = jnp.full(bkt.shape, NEG_INF, F32)
            for b in range(N_BUCKETS):
                acc = jnp.where(bkt == b, tab_ref[b, h], acc)
            write(h, acc)

    for var in range(2):
        def wr_p(h, acc, var=var):
            op_ref[var, h] = acc
        fill(bp_ref[var], wr_p)

    def wr_sc(h, acc):
        osc_ref[h] = acc
    fill(bsc_ref[...], wr_sc)

    def wr_sn(h, acc):
        osn_ref[h] = acc
    fill(bsn_ref[...], wr_sn)

    r = lax.broadcasted_iota(jnp.int32, (CHUNK, CHUNK), 0)
    c = lax.broadcasted_iota(jnp.int32, (CHUNK, CHUNK), 1)
    for h in range(A_HEADS):
        ows_ref[h] = jnp.where(r >= c, ws_ref[h], 0.0).astype(BF16)


def _prep(rel_bias_table, w_s):
    vm = pl.BlockSpec(memory_space=pltpu.VMEM)
    return pl.pallas_call(
        _prep_kernel,
        in_specs=[pl.BlockSpec(memory_space=pltpu.SMEM), vm, vm, vm, vm],
        out_specs=[vm, vm, vm, vm],
        out_shape=[
            jax.ShapeDtypeStruct((2, B_HEADS, WINDOW, 2 * WINDOW), F32),
            jax.ShapeDtypeStruct((B_HEADS, 32, SAMPLE_GROUP * WINDOW), F32),
            jax.ShapeDtypeStruct((B_HEADS, 32, LANES), F32),
            jax.ShapeDtypeStruct((A_HEADS, CHUNK, CHUNK), BF16),
        ],
        name="prep_tables",
    )(rel_bias_table, jnp.asarray(_BKT_P), jnp.asarray(_BKT_SC), jnp.asarray(_BKT_SN), w_s)


def _gate_pairs(va_rows, wsp_ref, lane_lo):
    outs = []
    for p in range(A_HEADS // 2):
        vp = va_rows[:, p * LANES:(p + 1) * LANES]
        rhs = jnp.concatenate([jnp.where(lane_lo, vp, 0.0), jnp.where(lane_lo, 0.0, vp)], axis=0).astype(BF16)
        outs.append(_dot(wsp_ref[p], rhs))
    return jnp.concatenate(outs, axis=1)


def _prompt_steps(body, first_row_out):
    def kern(*refs):
        i = pl.program_id(0)

        @pl.when(i < N_PROMPT_BLOCKS)
        def _():
            body(*refs)

        @pl.when(i >= N_PROMPT_BLOCKS)
        def _():
            for r in refs[first_row_out:first_row_out + 5]:
                r[...] = jnp.zeros(r.shape, r.dtype)

    return kern


def _mix0_prompt_kernel(x_ref, nm_ref, win_ref, lng_ref, lnb_ref, wsp_ref, bs_ref, bias_ref,
                        wout_ref, nf_ref, wr_ref, br_ref,
                        x1_ref, h_ref, ri_ref, rg_ref, tc_ref, kl_ref, vl_ref, val_ref, cnt_ref,
                        kprev, vprev, mix_scr):
    @pl.when(pl.program_id(0) == 0)
    def _():
        cnt_ref[...] = jnp.zeros_like(cnt_ref)

    x = x_ref[...]
    u, va, q, k, v = _project(x, nm_ref[...], win_ref[...], lng_ref[...], lnb_ref[...])
    lane_lo = lax.broadcasted_iota(jnp.int32, (WINDOW, LANES), 1) < B_HEAD_DIM
    row0 = lax.broadcasted_iota(jnp.int32, (WINDOW, KV_WIDTH), 0) == 0
    first = pl.program_id(0) % STEPS_PER_BATCH == 0

    @pl.when(first)
    def _():
        kprev[...] = jnp.zeros_like(kprev)
        vprev[...] = jnp.zeros_like(vprev)

    for j in range(SUB):
        rows = slice(j * WINDOW, (j + 1) * WINDOW)
        s_gate = _gate_pairs(va[rows], wsp_ref, lane_lo)
        mix_scr[rows, :A_WIDTH] = u[rows] * (s_gate + bs_ref[...])

        if j == 0:
            kp, vp = kprev[...], vprev[...]
        else:
            prows = slice((j - 1) * WINDOW, j * WINDOW)
            kp, vp = k[prows], v[prows]
        kk = jnp.concatenate([jnp.where(row0, 0.0, kp), k[rows]], axis=0)
        vv = jnp.concatenate([jnp.where(row0, 0.0, vp), v[rows]], axis=0)
        kops = (kk.astype(BF16), pltpu.roll(kk, B_HEAD_DIM, 1).astype(BF16))
        vops = (vv.astype(BF16), pltpu.roll(vv, B_HEAD_DIM, 1).astype(BF16))
        qt = [q[rows, p * LANES:(p + 1) * LANES] for p in range(4)]
        q_even = [jnp.where(lane_lo, t, 0.0) for t in qt]
        q_odd = [jnp.where(lane_lo, 0.0, t) for t in qt]
        stacks = (jnp.concatenate([q_even[0], q_even[1], q_odd[2], q_odd[3]], axis=0),
                  jnp.concatenate([q_odd[0], q_odd[1], q_even[2], q_even[3]], axis=0))
        o = []
        for st in range(2):
            s = _dot_nt(stacks[st].astype(BF16), kops[st])
            if j == 0:
                bias = bias_ref[jnp.where(first, 0, 1), st]
            else:
                bias = bias_ref[1, st]
            s = s + bias
            m = jnp.max(s, axis=-1, keepdims=True)
            p = jnp.exp(s - m)
            den = jnp.sum(p, axis=-1, keepdims=True)
            o.append(_dot(p.astype(BF16), vops[st]) / den)
        oa, ob = o
        sl = [slice(i * WINDOW, (i + 1) * WINDOW) for i in range(4)]
        tiles = (jnp.where(lane_lo, oa[sl[0]], ob[sl[0]]), jnp.where(lane_lo, oa[sl[1]], ob[sl[1]]),
                 jnp.where(lane_lo, ob[sl[2]], oa[sl[2]]), jnp.where(lane_lo, ob[sl[3]], oa[sl[3]]))
        for p in range(4):
            mix_scr[rows, A_WIDTH + p * LANES:A_WIDTH + (p + 1) * LANES] = tiles[p]

    last = slice(TM - WINDOW, TM)
    kprev[...] = k[last]
    vprev[...] = v[last]
    kl_ref[...] = k[last]
    vl_ref[...] = v[last]
    val_ref[...] = va[last]

    x1 = x + _dot(mix_scr[...].astype(BF16), wout_ref[...])
    x1_ref[...] = x1
    h, ids, gates = _route(x1, nf_ref[...], wr_ref[...], br_ref[...])
    h_ref[...] = h.reshape(h_ref.shape)
    ri_ref[...] = _rank_pack(ids, cnt_ref, tc_ref)
    rg_ref[...] = gates


def _const_spec(shape):
    nd = len(shape)
    return pl.BlockSpec(shape, lambda i, _n=nd: (0,) * _n)


def _mix0_prompt(x_all, nm, win, lng, lnb, wsp, bs_full, bias_p, wout, nf, wr, br):
    row_spec = pl.BlockSpec((TM, D_MODEL), lambda i: (i, 0))
    row3_spec = pl.BlockSpec((TM, ROW_TILE, LANES), lambda i: (i, 0, 0))
    lane_spec = pl.BlockSpec((TM, LANES), lambda i: (i, 0))
    last_kv = pl.BlockSpec((None, WINDOW, KV_WIDTH), lambda i: (jnp.minimum(i // STEPS_PER_BATCH, BATCH - 1), 0, 0))
    last_va = pl.BlockSpec((None, WINDOW, A_WIDTH), lambda i: (jnp.minimum(i // STEPS_PER_BATCH, BATCH - 1), 0, 0))
    return pl.pallas_call(
        _prompt_steps(_mix0_prompt_kernel, 12),
        grid=(N_ROW_BLOCKS,),
        in_specs=[pl.BlockSpec((TM, D_MODEL), lambda i: (jnp.minimum(i, N_PROMPT_BLOCKS - 1), 0)),
                  _const_spec((1, D_MODEL)), _const_spec((D_MODEL, IN_WIDTH)),
                  _const_spec((1, A_WIDTH)), _const_spec((1, A_WIDTH)),
                  _const_spec((A_HEADS // 2, CHUNK, 2 * CHUNK)), _const_spec((CHUNK, A_WIDTH)),
                  _const_spec((2, 2, 4 * WINDOW, 2 * WINDOW)),
                  _const_spec((A_WIDTH + Q_WIDTH, D_MODEL)), _const_spec((1, D_MODEL)),
                  _const_spec((D_MODEL, 2 * LANES)), _const_spec((1, LANES))],
        out_specs=[row_spec, row3_spec, pl.BlockSpec((8, TM), lambda i: (0, i)), lane_spec,
                   pl.BlockSpec((None, 1, LANES), lambda i: (i, 0, 0)),
                   last_kv, last_kv, last_va, _const_spec((1, LANES))],
        out_shape=[jax.ShapeDtypeStruct((T_ALL, D_MODEL), F32), jax.ShapeDtypeStruct((T_ALL, ROW_TILE, LANES), BF16),
                   jax.ShapeDtypeStruct((8, T_ALL), jnp.int32), jax.ShapeDtypeStruct((T_ALL, LANES), F32),
                   jax.ShapeDtypeStruct((N_ROW_BLOCKS, 1, LANES), F32),
                   jax.ShapeDtypeStruct((BATCH, WINDOW, KV_WIDTH), F32),
                   jax.ShapeDtypeStruct((BATCH, WINDOW, KV_WIDTH), F32),
                   jax.ShapeDtypeStruct((BATCH, WINDOW, A_WIDTH), F32),
                   jax.ShapeDtypeStruct((1, LANES), F32)],
        scratch_shapes=[pltpu.VMEM((WINDOW, KV_WIDTH), F32), pltpu.VMEM((WINDOW, KV_WIDTH), F32),
                        pltpu.VMEM((TM, D_MODEL), F32)],
        compiler_params=_cparams(("arbitrary",)),
        name="mix0_prompt",
    )(x_all, nm, win, lng, lnb, wsp, bs_full, bias_p, wout, nf, wr, br)


def _mix0_sample_kernel(x_ref, nm_ref, win_ref, lng_ref, lnb_ref, wcoef_ref, bcoef_ref,
                        ck_ref, cv_ref, bsc_ref, bsn_ref,
                        wout_ref, nf_ref, wr_ref, br_ref, cnt_in,
                        x1_in, h_in, ri_in, rg_in, tc_in,
                        x1_ref, h_ref, ri_ref, rg_ref, tc_ref, kn_ref, vn_ref, va_ref, cnt_ref,
                        q_scr, k_scr, v_scr, mix_scr):
    del x1_in, h_in, ri_in, rg_in, tc_in
    g = pl.program_id(0)

    @pl.when(g == 0)
    def _():
        u, va, q, k, v = _project(x_ref[...], nm_ref[...], win_ref[...], lng_ref[...], lnb_ref[...])
        q_scr[...] = q
        k_scr[...] = k
        v_scr[...] = v
        kn_ref[...] = k
        vn_ref[...] = v
        va_ref[...] = va
        idx = 0
        for t in range(DEC_SEQ):
            acc = jnp.zeros((DEC_BATCH, A_WIDTH), F32) + bcoef_ref[t:t + 1, :]
            for s in range(t + 1):
                acc = acc + wcoef_ref[idx:idx + 1, :] * va[s * DEC_BATCH:(s + 1) * DEC_BATCH]
                idx += 1
            mix_scr[t * DEC_BATCH:(t + 1) * DEC_BATCH, :A_WIDTH] = u[t * DEC_BATCH:(t + 1) * DEC_BATCH] * acc

    b0 = pl.multiple_of(g * SAMPLE_GROUP, SAMPLE_GROUP)
    lane_lo = lax.broadcasted_iota(jnp.int32, (DEC_SEQ * SAMPLE_GROUP, LANES), 1) < B_HEAD_DIM

    def grab(ref, width):
        return jnp.concatenate([ref[pl.ds(t * DEC_BATCH + b0, SAMPLE_GROUP), :] for t in range(DEC_SEQ)], axis=0)

    qg = grab(q_scr, Q_WIDTH)
    kn = grab(k_scr, KV_WIDTH)
    vn = grab(v_scr, KV_WIDTH)
    crow0 = lax.broadcasted_iota(jnp.int32, (SAMPLE_GROUP * WINDOW, KV_WIDTH), 0) == 0
    rows_kv = (SAMPLE_GROUP * WINDOW, KV_WIDTH)
    kc = jnp.where(crow0, 0.0, ck_ref[...].reshape(rows_kv))
    vc = jnp.where(crow0, 0.0, cv_ref[...].reshape(rows_kv))
    kc_ops = (kc.astype(BF16), pltpu.roll(kc, B_HEAD_DIM, 1).astype(BF16))
    vc_ops = (vc.astype(BF16), pltpu.roll(vc, B_HEAD_DIM, 1).astype(BF16))
    kn_ops = (kn.astype(BF16), pltpu.roll(kn, B_HEAD_DIM, 1).astype(BF16))
    vn_ops = (vn.astype(BF16), pltpu.roll(vn, B_HEAD_DIM, 1).astype(BF16))
    qt = [qg[:, p * LANES:(p + 1) * LANES] for p in range(4)]
    q_even = [jnp.where(lane_lo, t, 0.0) for t in qt]
    q_odd = [jnp.where(lane_lo, 0.0, t) for t in qt]
    stacks = (jnp.concatenate([q_even[0], q_even[1], q_odd[2], q_odd[3]], axis=0),
              jnp.concatenate([q_odd[0], q_odd[1], q_even[2], q_even[3]], axis=0))
    o = []
    for st in range(2):
        qs = stacks[st].astype(BF16)
        sc = _dot_nt(qs, kc_ops[st]) + bsc_ref[st]
        sn = _dot_nt(qs, kn_ops[st]) + bsn_ref[st][:, :DEC_SEQ * SAMPLE_GROUP]
        m = jnp.maximum(jnp.max(sc, axis=-1, keepdims=True), jnp.max(sn, axis=-1, keepdims=True))
        pc = jnp.exp(sc - m)
        pn = jnp.exp(sn - m)
        den = jnp.sum(pc, axis=-1, keepdims=True) + jnp.sum(pn, axis=-1, keepdims=True)
        o.append((_dot(pc.astype(BF16), vc_ops[st]) + _dot(pn.astype(BF16), vn_ops[st])) / den)
    oa, ob = o
    n = DEC_SEQ * SAMPLE_GROUP
    sl = [slice(i * n, (i + 1) * n) for i in range(4)]
    tiles = (jnp.where(lane_lo, oa[sl[0]], ob[sl[0]]), jnp.where(lane_lo, oa[sl[1]], ob[sl[1]]),
             jnp.where(lane_lo, ob[sl[2]], oa[sl[2]]), jnp.where(lane_lo, ob[sl[3]], oa[sl[3]]))
    for p in range(4):
        for t in range(DEC_SEQ):
            mix_scr[pl.ds(t * DEC_BATCH + b0, SAMPLE_GROUP), A_WIDTH + p * LANES:A_WIDTH + (p + 1) * LANES] = (
                tiles[p][t * SAMPLE_GROUP:(t + 1) * SAMPLE_GROUP])

    @pl.when(g == N_SAMPLE_GROUPS - 1)
    def _():
        x1 = x_ref[...] + _dot(mix_scr[...].astype(BF16), wout_ref[...])
        x1_ref[...] = x1
        h, ids, gates = _route(x1, nf_ref[...], wr_ref[...], br_ref[...])
        h_ref[...] = h.reshape(h_ref.shape)
        cnt_ref[...] = cnt_in[...]
        ri_ref[...] = _rank_pack(ids, cnt_ref, tc_ref)
        rg_ref[...] = gates


def _mix0_sample(x_all, nm, win, lng, lnb, wcoef, bcoef, ck, cv, bias_sc, bias_sn, wout, nf, wr, br, cnt,
                 x1_all, h_all, ri_all, rg_all, tc_all):
    sample_rows = pl.BlockSpec((TM, D_MODEL), lambda g: (N_PROMPT_BLOCKS, 0))
    sample_rows3 = pl.BlockSpec((TM, ROW_TILE, LANES), lambda g: (N_PROMPT_BLOCKS, 0, 0))
    sample_lanes = pl.BlockSpec((TM, LANES), lambda g: (N_PROMPT_BLOCKS, 0))
    cache_spec = pl.BlockSpec((None, SAMPLE_GROUP, WINDOW, B_KV_HEADS, B_HEAD_DIM), lambda g: (0, g, 0, 0, 0))
    anyspec = pl.BlockSpec(memory_space=pl.ANY)
    n_in = 16
    return pl.pallas_call(
        _mix0_sample_kernel,
        grid=(N_SAMPLE_GROUPS,),
        in_specs=[_const_spec((TM, D_MODEL)), _const_spec((1, D_MODEL)), _const_spec((D_MODEL, IN_WIDTH)),
                  _const_spec((1, A_WIDTH)), _const_spec((1, A_WIDTH)),
                  _const_spec((16, A_WIDTH)), _const_spec((8, A_WIDTH)),
                  cache_spec, cache_spec,
                  _const_spec((2, 4 * 32, SAMPLE_GROUP * WINDOW)), _const_spec((2, 4 * 32, LANES)),
                  _const_spec((A_WIDTH + Q_WIDTH, D_MODEL)), _const_spec((1, D_MODEL)),
                  _const_spec((D_MODEL, 2 * LANES)), _const_spec((1, LANES)), _const_spec((1, LANES)),
                  anyspec, anyspec, anyspec, anyspec, anyspec],
        out_specs=[sample_rows, sample_rows3, pl.BlockSpec((8, TM), lambda g: (0, N_PROMPT_BLOCKS)), sample_lanes,
                   pl.BlockSpec((None, 1, LANES), lambda g: (N_PROMPT_BLOCKS, 0, 0)),
                   _const_spec((T_SAMPLE, KV_WIDTH)), _const_spec((T_SAMPLE, KV_WIDTH)),
                   _const_spec((T_SAMPLE, A_WIDTH)), _const_spec((1, LANES))],
        out_shape=[jax.ShapeDtypeStruct((T_ALL, D_MODEL), F32), jax.ShapeDtypeStruct((T_ALL, ROW_TILE, LANES), BF16),
                   jax.ShapeDtypeStruct((8, T_ALL), jnp.int32), jax.ShapeDtypeStruct((T_ALL, LANES), F32),
                   jax.ShapeDtypeStruct((N_ROW_BLOCKS, 1, LANES), F32),
                   jax.ShapeDtypeStruct((T_SAMPLE, KV_WIDTH), F32), jax.ShapeDtypeStruct((T_SAMPLE, KV_WIDTH), F32),
                   jax.ShapeDtypeStruct((T_SAMPLE, A_WIDTH), F32), jax.ShapeDtypeStruct((1, LANES), F32)],
        scratch_shapes=[pltpu.VMEM((T_SAMPLE, Q_WIDTH), F32), pltpu.VMEM((T_SAMPLE, KV_WIDTH), F32),
                        pltpu.VMEM((T_SAMPLE, KV_WIDTH), F32), pltpu.VMEM((T_SAMPLE, D_MODEL), F32)],
        input_output_aliases={n_in: 0, n_in + 1: 1, n_in + 2: 2, n_in + 3: 3, n_in + 4: 4},
        compiler_params=_cparams(("arbitrary",)),
        name="mix0_sample",
    )(x_all, nm, win, lng, lnb, wcoef, bcoef, ck, cv, bias_sc, bias_sn, wout, nf, wr, br, cnt,
      x1_all, h_all, ri_all, rg_all, tc_all)


def _moe_metadata(rt_all, cnt, tcnt):
    counts = cnt[0, :N_EXPERTS].astype(jnp.int32)
    padded = (counts + MOE_BLK - 1) // MOE_BLK * MOE_BLK
    pad_end = jnp.cumsum(padded)
    pad_start = pad_end - padded
    experts = jnp.arange(N_EXPERTS, dtype=jnp.int32)
    n_valid = (pad_end[-1] // MOE_BLK).astype(jnp.int32).reshape(1)
    blk_start = jnp.arange(N_MOE_BLOCKS, dtype=jnp.int32) * MOE_BLK
    block_e = jnp.minimum(jnp.sum((blk_start[:, None] >= pad_end[None, :]).astype(jnp.int32), axis=1),
                          N_EXPERTS - 1).astype(jnp.int32)
    zero_start = (pad_start + counts).astype(jnp.int32)
    zero_len = (padded - counts).astype(jnp.int32)
    first = (blk_start == pad_start[block_e]).astype(jnp.int32)
    used = counts > 0
    parity = ((jnp.cumsum(used.astype(jnp.int32)) - 1) % 2)[block_e].astype(jnp.int32)
    nearest = lax.cummin(jnp.where(used, experts, N_EXPERTS)[::-1])[::-1]
    next_used = jnp.concatenate([nearest[1:], jnp.full((1,), N_EXPERTS, jnp.int32)])
    nxt = jnp.where(next_used < N_EXPERTS, next_used, -1)[block_e].astype(jnp.int32)
    plan = (block_e, first, parity, nxt, n_valid)
    runs = tcnt[:, 0, :N_EXPERTS].astype(jnp.int32)
    run_dst = pad_start[None, :] + jnp.cumsum(runs, axis=0) - runs
    lpos = rt_all[2 * TOP_K:3 * TOP_K].reshape(N_SLOTS).astype(jnp.int32)
    cplan = (lpos, runs.reshape(-1), run_dst.reshape(-1).astype(jnp.int32))
    dplan = cplan + (jnp.concatenate([zero_start, zero_len, n_valid]),)
    return plan, dplan, cplan


RUN_PIECE = 32


def _for_run_pieces(n, start_piece):
    whole = n // RUN_PIECE

    def body(j, carry):
        start_piece(j * RUN_PIECE, RUN_PIECE)
        return carry

    lax.fori_loop(0, whole, body, 0)
    o = whole * RUN_PIECE
    bit = RUN_PIECE // 2
    while bit >= 1:
        take = (n & bit) != 0

        @pl.when(take)
        def _(o=o, bit=bit):
            start_piece(o, bit)

        o = o + jnp.where(take, bit, 0)
        bit //= 2


def _dispatch_kernel(lpos_ref, run_ref, rdst_ref, zs_ref, h_ref, xs_ref, zero_scr, stage, sem, zsem):
    i = pl.program_id(0)

    @pl.when(i == 0)
    def _():
        zero_scr[...] = jnp.zeros_like(zero_scr)

        def pieces(e, do):
            off = zs_ref[e]
            rem = zs_ref[N_EXPERTS + e]
            bit = MOE_BLK // 2
            while bit >= 1:
                take = (rem & bit) != 0

                @pl.when(take)
                def _(off=off, bit=bit):
                    do(pltpu.make_async_copy(zero_scr.at[pl.ds(0, bit)], xs_ref.at[pl.ds(off, bit)], zsem))

                off = off + jnp.where(take, bit, 0)
                bit //= 2

        def start_e(e, c):
            pieces(e, lambda cp: cp.start())
            return c

        def wait_e(e, c):
            pieces(e, lambda cp: cp.wait())
            return c

        def tail(do):
            def step(b, c):
                do(pltpu.make_async_copy(zero_scr, xs_ref.at[pl.ds(b * MOE_BLK, MOE_BLK)], zsem))
                return c
            return step

        n_valid = zs_ref[2 * N_EXPERTS]
        lax.fori_loop(0, N_EXPERTS, start_e, 0)
        lax.fori_loop(n_valid, N_MOE_BLOCKS, tail(lambda cp: cp.start()), 0)
        lax.fori_loop(0, N_EXPERTS, wait_e, 0)
        lax.fori_loop(n_valid, N_MOE_BLOCKS, tail(lambda cp: cp.wait()), 0)

    base = i * TM

    def place(r, carry):
        row = h_ref[r]
        for kk in range(TOP_K):
            stage[lpos_ref[kk * T_ALL + base + r]] = row
        return carry

    lax.fori_loop(0, TM, place, 0, unroll=8)

    def send_run(e, off):
        n = run_ref[i * N_EXPERTS + e]
        dst = rdst_ref[i * N_EXPERTS + e]
        _for_run_pieces(n, lambda o, size: pltpu.make_async_copy(
            stage.at[pl.ds(off + o, size)], xs_ref.at[pl.ds(dst + o, size)], sem).start(
                priority=size.bit_length() % 2))
        return off + n

    lax.fori_loop(0, N_EXPERTS, send_run, 0)
    pltpu.make_async_copy(stage, xs_ref.at[pl.ds(0, TM * TOP_K)], sem).wait()


def _dispatch(dplan, h_all):
    return pl.pallas_call(
        _dispatch_kernel,
        grid_spec=pltpu.PrefetchScalarGridSpec(
            num_scalar_prefetch=4,
            grid=(N_ROW_BLOCKS,),
            in_specs=[pl.BlockSpec((TM, ROW_TILE, LANES), lambda i, lp, rn, rd, z: (i, 0, 0))],
            out_specs=pl.BlockSpec(memory_space=pl.ANY),
            scratch_shapes=[pltpu.VMEM((MOE_BLK, ROW_TILE, LANES), BF16),
                            pltpu.VMEM((TM * TOP_K, ROW_TILE, LANES), BF16),
                            pltpu.SemaphoreType.DMA(()), pltpu.SemaphoreType.DMA(())],
        ),
        out_shape=jax.ShapeDtypeStruct((N_SORT_ROWS, ROW_TILE, LANES), BF16),
        compiler_params=_cparams(("arbitrary",)),
        name="moe_dispatch",
    )(*dplan, h_all)


def _experts_kernel(layer, be_ref, first_ref, par_ref, nxt_ref, nv_ref,
                    x_ref, wg_hbm, wu_hbm, wd_hbm, y_ref,
                    wg_s, wu_s, wd_s, wg_f, wu_f, wd_f, wsem):
    i = pl.program_id(0)

    def fetch(e, slot):
        return (pltpu.make_async_copy(wg_hbm.at[layer, e], wg_f.at[slot], wsem.at[slot]),
                pltpu.make_async_copy(wu_hbm.at[layer, e], wu_f.at[slot], wsem.at[slot]),
                pltpu.make_async_copy(wd_hbm.at[layer, e], wd_f.at[slot], wsem.at[slot]))

    @pl.when(i < nv_ref[0])
    def _():
        e = be_ref[i]
        slot = par_ref[i]

        @pl.when(i == 0)
        def _():
            for cp in fetch(e, slot):
                cp.start()

        @pl.when(first_ref[i] == 1)
        def _():
            for cp in fetch(e, slot):
                cp.wait()
            wg_s[...] = wg_f[slot].astype(BF16)
            wu_s[...] = wu_f[slot].astype(BF16)
            wd_s[...] = wd_f[slot].astype(BF16)
            nxt = nxt_ref[i]

            @pl.when(nxt >= 0)
            def _():
                for cp in fetch(nxt, 1 - slot):
                    cp.start()

        xb = x_ref[...].reshape(MOE_BLK, D_MODEL)
        a = jax.nn.silu(_dot(xb, wg_s[...])) * _dot(xb, wu_s[...])
        y_ref[...] = _dot(a.astype(BF16), wd_s[...]).reshape(y_ref.shape)

    @pl.when(i >= nv_ref[0])
    def _():
        y_ref[...] = jnp.zeros(y_ref.shape, y_ref.dtype)


def _experts(block_e, first, parity, nxt, n_valid, xs, w_gate, w_up, w_down, layer):
    def blk(i, be, fi, pa, nx, nv):
        return (jnp.maximum(jnp.minimum(i, nv[0] - 1), 0), 0, 0)

    anyspec = pl.BlockSpec(memory_space=pl.ANY)
    return pl.pallas_call(
        functools.partial(_experts_kernel, layer),
        grid_spec=pltpu.PrefetchScalarGridSpec(
            num_scalar_prefetch=5,
            grid=(N_MOE_BLOCKS,),
            in_specs=[pl.BlockSpec((MOE_BLK, ROW_TILE, LANES), blk), anyspec, anyspec, anyspec],
            out_specs=pl.BlockSpec((MOE_BLK, ROW_TILE, LANES), lambda i, be, fi, pa, nx, nv: (i, 0, 0)),
            scratch_shapes=[pltpu.VMEM((D_MODEL, D_EXPERT), BF16), pltpu.VMEM((D_MODEL, D_EXPERT), BF16),
                            pltpu.VMEM((D_EXPERT, D_MODEL), BF16),
                            pltpu.VMEM((2, D_MODEL, D_EXPERT), F32), pltpu.VMEM((2, D_MODEL, D_EXPERT), F32),
                            pltpu.VMEM((2, D_EXPERT, D_MODEL), F32), pltpu.SemaphoreType.DMA((2,))],
        ),
        out_shape=jax.ShapeDtypeStruct((N_SORT_ROWS, ROW_TILE, LANES), F32),
        compiler_params=_cparams(("arbitrary",)),
        name="moe_experts",
    )(block_e, first, parity, nxt, n_valid, xs, w_gate, w_up, w_down)


def _gather_rows(lpos_ref, run_ref, rdst_ref, ys_ref, ystage, ybuf, sem, i):
    def fetch(tile, buf):
        def fetch_run(e, off):
            n = run_ref[tile * N_EXPERTS + e]
            src = rdst_ref[tile * N_EXPERTS + e]
            _for_run_pieces(n, lambda o, size: pltpu.make_async_copy(
                ys_ref.at[pl.ds(src + o, size)], ystage.at[buf, pl.ds(off + o, size)], sem.at[buf]).start(
                    priority=size.bit_length() % 2))
            return off + n

        lax.fori_loop(0, N_EXPERTS, fetch_run, 0)

    buf = i % 2

    @pl.when(i == 0)
    def _():
        fetch(i, buf)

    @pl.when(i + 1 < N_ROW_BLOCKS)
    def _():
        fetch(i + 1, 1 - buf)

    pltpu.make_async_copy(ys_ref.at[pl.ds(0, TM * TOP_K)], ystage.at[buf], sem.at[buf]).wait()
    base = i * TM

    def unplace(r, carry):
        for kk in range(TOP_K):
            ybuf[kk, r] = ystage[buf, lpos_ref[kk * T_ALL + base + r]]
        return carry

    lax.fori_loop(0, TM, unplace, 0, unroll=8)


def _combined(x_ref, rg_ref, ybuf):
    rg = rg_ref[...]
    y0 = ybuf[0].reshape(TM, D_MODEL)
    y1 = ybuf[1].reshape(TM, D_MODEL)
    return x_ref[...] + rg[:, 0:1] * y0 + rg[:, 1:2] * y1


_COMBINE_SCRATCH = [pltpu.VMEM((2, TM * TOP_K, ROW_TILE, LANES), F32), pltpu.VMEM((TOP_K, TM, ROW_TILE, LANES), F32),
                    pltpu.SemaphoreType.DMA((2,))]


def _combine_kernel(lpos_ref, run_ref, rdst_ref, x_ref, rg_ref, ys_ref, o_ref, ystage, ybuf, sem):
    _gather_rows(lpos_ref, run_ref, rdst_ref, ys_ref, ystage, ybuf, sem, pl.program_id(0))
    o_ref[...] = _combined(x_ref, rg_ref, ybuf)


def _combine(cplan, x_all, rg_all, ys):
    return pl.pallas_call(
        _combine_kernel,
        grid_spec=pltpu.PrefetchScalarGridSpec(
            num_scalar_prefetch=3,
            grid=(N_ROW_BLOCKS,),
            in_specs=[pl.BlockSpec((TM, D_MODEL), lambda i, a, b, c: (i, 0)),
                      pl.BlockSpec((TM, LANES), lambda i, a, b, c: (i, 0)),
                      pl.BlockSpec(memory_space=pl.ANY)],
            out_specs=pl.BlockSpec((TM, D_MODEL), lambda i, a, b, c: (i, 0)),
            scratch_shapes=_COMBINE_SCRATCH,
        ),
        out_shape=jax.ShapeDtypeStruct((T_ALL, D_MODEL), F32),
        compiler_params=_cparams(("arbitrary",)),
        name="moe_combine",
    )(*cplan, x_all, rg_all, ys)


def _final_kernel(lpos_ref, run_ref, rdst_ref, x_ref, rg_ref, ys_ref, nfin_ref, op_ref, os_ref, ystage, ybuf, sem):
    i = pl.program_id(0)
    _gather_rows(lpos_ref, run_ref, rdst_ref, ys_ref, ystage, ybuf, sem, i)
    y = _rms(_combined(x_ref, rg_ref, ybuf), nfin_ref[...])

    @pl.when(i < N_PROMPT_BLOCKS)
    def _():
        op_ref[...] = y

    @pl.when(i >= N_PROMPT_BLOCKS)
    def _():
        os_ref[...] = y


def _final(cplan, x_all, rg_all, ys, nfin):
    return pl.pallas_call(
        _final_kernel,
        grid_spec=pltpu.PrefetchScalarGridSpec(
            num_scalar_prefetch=3,
            grid=(N_ROW_BLOCKS,),
            in_specs=[pl.BlockSpec((TM, D_MODEL), lambda i, a, b, c: (i, 0)),
                      pl.BlockSpec((TM, LANES), lambda i, a, b, c: (i, 0)),
                      pl.BlockSpec(memory_space=pl.ANY),
                      pl.BlockSpec((1, D_MODEL), lambda i, a, b, c: (0, 0))],
            out_specs=[pl.BlockSpec((TM, D_MODEL), lambda i, a, b, c: (jnp.minimum(i, N_PROMPT_BLOCKS - 1), 0)),
                       pl.BlockSpec((TM, D_MODEL), lambda i, a, b, c: (0, 0))],
            scratch_shapes=_COMBINE_SCRATCH,
        ),
        out_shape=[jax.ShapeDtypeStruct((T_PROMPT, D_MODEL), F32), jax.ShapeDtypeStruct((T_SAMPLE, D_MODEL), F32)],
        compiler_params=_cparams(("arbitrary",)),
        name="moe_combine_final",
    )(*cplan, x_all, rg_all, ys, nfin)


def _moe(h_all, rt_all, cnt, tcnt, w_gate, w_up, w_down, layer):
    plan, dplan, cplan = _moe_metadata(rt_all, cnt, tcnt)
    xs = _dispatch(dplan, h_all)
    ys = _experts(*plan, xs, w_gate, w_up, w_down, layer)
    return cplan, ys


def _pool_project(d_groups, wp_ref, scale):
    outs = [_dot(d_groups[g].astype(BF16), wp_ref[g]) for g in range(len(POOL_SIZES))]
    return jnp.concatenate(outs, axis=1) * scale


def _mix1_prompt_kernel(x_ref, nm_ref, wp_ref, sc_ref, nf_ref, wr_ref, br_ref,
                        x3_ref, h_ref, ri_ref, rg_ref, tc_ref, pl_ref, cnt_ref, ext):
    i = pl.program_id(0)

    @pl.when(i == 0)
    def _():
        cnt_ref[...] = jnp.zeros_like(cnt_ref)

    x = x_ref[...]
    hp = _rms(x, nm_ref[...])

    @pl.when(i % STEPS_PER_BATCH == 0)
    def _():
        ext[0:POOL_MAX, :] = jnp.zeros((POOL_MAX, D_MODEL), F32)

    ext[POOL_MAX:, :] = hp
    pos = (i % STEPS_PER_BATCH) * TM + lax.broadcasted_iota(jnp.int32, (TM, 1), 0)
    d_groups = []
    for g, w in enumerate(POOL_SIZES):
        cols = slice(g * POOL_GROUP_DIM, (g + 1) * POOL_GROUP_DIM)
        acc = ext[:, cols]
        span = 1
        while span < w:
            acc = acc + pltpu.roll(acc, span, 0)
            span *= 2
        cnt = jnp.minimum(pos + 1, w).astype(F32)
        d_groups.append(acc[POOL_MAX:] / cnt - hp[:, cols])
    tail = hp[TM - POOL_MAX:, :]
    ext[0:POOL_MAX, :] = tail
    pl_ref[...] = tail

    x3 = x + _pool_project(d_groups, wp_ref, sc_ref[...])
    x3_ref[...] = x3
    h, ids, gates = _route(x3, nf_ref[...], wr_ref[...], br_ref[...])
    h_ref[...] = h.reshape(h_ref.shape)
    ri_ref[...] = _rank_pack(ids, cnt_ref, tc_ref)
    rg_ref[...] = gates


def _mix1_prompt(x_all, nm, wp, sc, nf, wr, br):
    row_spec = pl.BlockSpec((TM, D_MODEL), lambda i: (i, 0))
    row3_spec = pl.BlockSpec((TM, ROW_TILE, LANES), lambda i: (i, 0, 0))
    lane_spec = pl.BlockSpec((TM, LANES), lambda i: (i, 0))
    return pl.pallas_call(
        _prompt_steps(_mix1_prompt_kernel, 7),
        grid=(N_ROW_BLOCKS,),
        in_specs=[row_spec, _const_spec((1, D_MODEL)),
                  _const_spec((len(POOL_SIZES), POOL_GROUP_DIM, POOL_GROUP_DIM)), _const_spec((1, D_MODEL)),
                  _const_spec((1, D_MODEL)), _const_spec((D_MODEL, 2 * LANES)), _const_spec((1, LANES))],
        out_specs=[row_spec, row3_spec, pl.BlockSpec((8, TM), lambda i: (0, i)), lane_spec,
                   pl.BlockSpec((None, 1, LANES), lambda i: (i, 0, 0)),
                   pl.BlockSpec((None, POOL_MAX, D_MODEL),
                                lambda i: (jnp.minimum(i // STEPS_PER_BATCH, BATCH - 1), 0, 0)),
                   _const_spec((1, LANES))],
        out_shape=[jax.ShapeDtypeStruct((T_ALL, D_MODEL), F32), jax.ShapeDtypeStruct((T_ALL, ROW_TILE, LANES), BF16),
                   jax.ShapeDtypeStruct((8, T_ALL), jnp.int32), jax.ShapeDtypeStruct((T_ALL, LANES), F32),
                   jax.ShapeDtypeStruct((N_ROW_BLOCKS, 1, LANES), F32),
                   jax.ShapeDtypeStruct((BATCH, POOL_MAX, D_MODEL), F32), jax.ShapeDtypeStruct((1, LANES), F32)],
        scratch_shapes=[pltpu.VMEM((POOL_MAX + TM, D_MODEL), F32)],
        compiler_params=_cparams(("arbitrary",)),
        name="mix1_prompt",
    )(x_all, nm, wp, sc, nf, wr, br)


def _mix1_sample_kernel(x_ref, st_ref, nm_ref, wp_ref, sc_ref, nf_ref, wr_ref, br_ref, cnt_in,
                        x3_in, h_in, ri_in, rg_in, tc_in,
                        x3_ref, h_ref, ri_ref, rg_ref, tc_ref, hs_ref, cnt_ref):
    del x3_in, h_in, ri_in, rg_in, tc_in
    x = x_ref[...]
    hs = _rms(x, nm_ref[...])
    hs_ref[...] = hs
    n_ctx = POOL_MAX - 1
    d_groups = []
    for g, w in enumerate(POOL_SIZES):
        cols = slice(g * POOL_GROUP_DIM, (g + 1) * POOL_GROUP_DIM)
        parts = []
        for t in range(DEC_SEQ):
            acc = hs[t * DEC_BATCH:(t + 1) * DEC_BATCH, cols]
            for back in range(1, w):
                src = t - back
                if src >= 0:
                    acc = acc + hs[src * DEC_BATCH:(src + 1) * DEC_BATCH, cols]
                else:
                    acc = acc + st_ref[n_ctx + src, :, cols]
            parts.append(acc / float(w) - hs[t * DEC_BATCH:(t + 1) * DEC_BATCH, cols])
        d_groups.append(jnp.concatenate(parts, axis=0))
    x3 = x + _pool_project(d_groups, wp_ref, sc_ref[...])
    x3_ref[...] = x3
    h, ids, gates = _route(x3, nf_ref[...], wr_ref[...], br_ref[...])
    h_ref[...] = h.reshape(h_ref.shape)
    cnt_ref[...] = cnt_in[...]
    ri_ref[...] = _rank_pack(ids, cnt_ref, tc_ref)
    rg_ref[...] = gates


def _mix1_sample(x_all, state_t, nm, wp, sc, nf, wr, br, cnt, x3_all, h_all, ri_all, rg_all, tc_all):
    sample_rows = pl.BlockSpec((TM, D_MODEL), lambda g: (N_PROMPT_BLOCKS, 0))
    sample_rows3 = pl.BlockSpec((TM, ROW_TILE, LANES), lambda g: (N_PROMPT_BLOCKS, 0, 0))
    sample_lanes = pl.BlockSpec((TM, LANES), lambda g: (N_PROMPT_BLOCKS, 0))
    anyspec = pl.BlockSpec(memory_space=pl.ANY)
    n_in = 9
    return pl.pallas_call(
        _mix1_sample_kernel,
        grid=(1,),
        in_specs=[sample_rows, _const_spec((POOL_MAX - 1, DEC_BATCH, D_MODEL)), _const_spec((1, D_MODEL)),
                  _const_spec((len(POOL_SIZES), POOL_GROUP_DIM, POOL_GROUP_DIM)), _const_spec((1, D_MODEL)),
                  _const_spec((1, D_MODEL)), _const_spec((D_MODEL, 2 * LANES)), _const_spec((1, LANES)),
                  _const_spec((1, LANES)), anyspec, anyspec, anyspec, anyspec, anyspec],
        out_specs=[sample_rows, sample_rows3, pl.BlockSpec((8, TM), lambda g: (0, N_PROMPT_BLOCKS)), sample_lanes,
                   pl.BlockSpec((None, 1, LANES), lambda g: (N_PROMPT_BLOCKS, 0, 0)),
                   _const_spec((T_SAMPLE, D_MODEL)), _const_spec((1, LANES))],
        out_shape=[jax.ShapeDtypeStruct((T_ALL, D_MODEL), F32), jax.ShapeDtypeStruct((T_ALL, ROW_TILE, LANES), BF16),
                   jax.ShapeDtypeStruct((8, T_ALL), jnp.int32), jax.ShapeDtypeStruct((T_ALL, LANES), F32),
                   jax.ShapeDtypeStruct((N_ROW_BLOCKS, 1, LANES), F32),
                   jax.ShapeDtypeStruct((T_SAMPLE, D_MODEL), F32), jax.ShapeDtypeStruct((1, LANES), F32)],
        input_output_aliases={n_in: 0, n_in + 1: 1, n_in + 2: 2, n_in + 3: 3, n_in + 4: 4},
        compiler_params=_cparams(("arbitrary",)),
        name="mix1_sample",
    )(x_all, state_t, nm, wp, sc, nf, wr, br, cnt, x3_all, h_all, ri_all, rg_all, tc_all)


def _router_weights(wg, bg, we, be):
    w = jnp.concatenate([wg, jnp.transpose(we, (1, 0, 2)).reshape(D_MODEL, N_EXPERTS)], axis=1)
    b = jnp.concatenate([bg, be.reshape(N_EXPERTS)])
    pad = LANES - N_GROUPS - N_EXPERTS
    w = jnp.pad(w, ((0, 0), (0, pad)))
    w_hi = w.astype(BF16)
    w_lo = (w - w_hi.astype(F32)).astype(BF16)
    return jnp.concatenate([w_hi, w_lo], axis=1), jnp.pad(b, (0, pad)).reshape(1, LANES)


def _stack(tab):
    return jnp.stack([jnp.concatenate([tab[h] for h in heads], axis=0) for heads in STACK_HEADS])


def kernel(x_prompt, x_sample, cache_k_win, cache_v_win, state_pool, norm_mix, norm_ffn, norm_final, w_in,
           a_ln_g, a_ln_b, a_w_s, a_b_s, b_sinks, rel_bias_table, w_out, c_w_pool, c_scale,
           router_group_w, router_group_b, router_expert_w, router_expert_b, w_gate, w_up, w_down):
    xs_t = jnp.transpose(x_sample, (1, 0, 2)).reshape(T_SAMPLE, D_MODEL)
    xp2 = x_prompt.reshape(T_PROMPT, D_MODEL)
    win =w_in[0].astype(BF16)
    wout = w_out[0].astype(BF16)
    lng = a_ln_g[0].reshape(1, A_WIDTH)
    lnb = a_ln_b[0].reshape(1, A_WIDTH)
    bias_p, bias_sc, bias_sn, ws_tril = _prep(rel_bias_table, a_w_s[0])
    wsp = ws_tril.reshape(A_HEADS // 2, 2, CHUNK, CHUNK).transpose(0, 2, 1, 3).reshape(A_HEADS // 2, CHUNK, 2 * CHUNK)
    bs_full = jnp.repeat(a_b_s[0].T, A_HEAD_DIM, axis=1)
    bias_p = jnp.stack([_stack(bias_p[0]), _stack(bias_p[1])])
    bias_sc = _stack(bias_sc)
    bias_sn = _stack(bias_sn)
    sinks = b_sinks[0]
    sink_p = jnp.stack([jnp.repeat(sinks[jnp.array(hh)], WINDOW) for hh in STACK_HEADS])
    sink_s = jnp.stack([jnp.repeat(sinks[jnp.array(hh)], 32) for hh in STACK_HEADS])
    bias_p = bias_p.at[:, :, :, 0].set(jnp.broadcast_to(sink_p[None], (2, 2, 4 * WINDOW)))
    bias_sc = bias_sc.at[:, :, 0].set(sink_s)
    pairs = [(t, s) for t in range(DEC_SEQ) for s in range(t + 1)]
    wcoef = jnp.stack([jnp.repeat(a_w_s[0][:, t, s], A_HEAD_DIM) for t, s in pairs])
    wcoef = jnp.pad(wcoef, ((0, 16 - len(pairs)), (0, 0)))
    bcoef = jnp.pad(jnp.repeat(a_b_s[0][:, :DEC_SEQ].T, A_HEAD_DIM, axis=1), ((0, 8 - DEC_SEQ), (0, 0)))
    ck = cache_k_win
    cv = cache_v_win
    routers = [_router_weights(router_group_w[l], router_group_b[l], router_expert_w[l], router_expert_b[l])
               for l in range(2)]
    nm = [norm_mix[l].reshape(1, D_MODEL) for l in range(2)]
    nf = [norm_ffn[l].reshape(1, D_MODEL) for l in range(2)]

    x1_all, h_all, ri_all, rg_all, tc_all, k_last, v_last, va_last, cnt0 = _mix0_prompt(
        xp2, nm[0], win, lng, lnb, wsp, bs_full, bias_p, wout, nf[0], *routers[0])
    x1_all, h_all, ri_all, rg_all, tc_all, k_new, v_new, va_s, cnt0 = _mix0_sample(
        xs_t, nm[0], win, lng, lnb, wcoef, bcoef, ck, cv, bias_sc, bias_sn, wout, nf[0], *routers[0], cnt0,
        x1_all, h_all, ri_all, rg_all, tc_all)
    cplan0, ys0 = _moe(h_all, ri_all, cnt0, tc_all, w_gate, w_up, w_down, 0)
    x2_all = _combine(cplan0, x1_all, rg_all, ys0)

    wp = c_w_pool[0].astype(BF16)
    sc = c_scale[0].reshape(1, D_MODEL)
    x3_all, h2_all, ri2_all, rg2_all, tc2_all, pool_tail, cnt1 = _mix1_prompt(
        x2_all, nm[1], wp, sc, nf[1], *routers[1])
    state_t = jnp.transpose(state_pool[0], (1, 0, 2))
    x3_all, h2_all, ri2_all, rg2_all, tc2_all, hs1, cnt1 = _mix1_sample(
        x2_all, state_t, nm[1], wp, sc, nf[1], *routers[1], cnt1, x3_all, h2_all, ri2_all, rg2_all, tc2_all)
    cplan1, ys1 = _moe(h2_all, ri2_all, cnt1, tc2_all, w_gate, w_up, w_down, 1)
    y_p, y_s = _final(cplan1, x3_all, rg2_all, ys1, norm_final.reshape(1, D_MODEL))

    def from_tmajor(a, width):
        return jnp.transpose(a.reshape(DEC_SEQ, DEC_BATCH, width), (1, 0, 2))

    y_prompt = y_p.reshape(BATCH, SEQ, D_MODEL)
    y_sample = from_tmajor(y_s, D_MODEL)
    win_k_p = k_last.reshape(1, BATCH, WINDOW, B_KV_HEADS, B_HEAD_DIM)
    win_v_p = v_last.reshape(1, BATCH, WINDOW, B_KV_HEADS, B_HEAD_DIM)
    kn = from_tmajor(k_new, KV_WIDTH).reshape(DEC_BATCH, DEC_SEQ, B_KV_HEADS, B_HEAD_DIM)
    vn = from_tmajor(v_new, KV_WIDTH).reshape(DEC_BATCH, DEC_SEQ, B_KV_HEADS, B_HEAD_DIM)
    win_k_s = jnp.concatenate([cache_k_win[0][:, DEC_SEQ:], kn], axis=1)[None]
    win_v_s = jnp.concatenate([cache_v_win[0][:, DEC_SEQ:], vn], axis=1)[None]
    chunk_v_p = va_last.reshape(1, BATCH, CHUNK, A_HEADS, A_HEAD_DIM)
    chunk_v_s = from_tmajor(va_s, A_WIDTH).reshape(1, DEC_BATCH, DEC_SEQ, A_HEADS, A_HEAD_DIM)
    pool_p = pool_tail[:, 1:][None]
    pool_s = jnp.concatenate([state_pool[0][:, DEC_SEQ:], from_tmajor(hs1, D_MODEL)], axis=1)[None]
    return (y_prompt, y_sample, win_k_p, win_v_p, win_k_s, win_v_s, chunk_v_p, chunk_v_s, pool_p, pool_s)
```

```python
import functools
import math

import numpy as np
import jax
import jax.numpy as jnp
from jax import lax
from jax.experimental import pallas as pl
from jax.experimental.pallas import tpu as pltpu

F32 = jnp.float32
BF16 = jnp.bfloat16

D_MODEL = 1024
BATCH = 2
SEQ = 8192
DEC_BATCH = 128
DEC_SEQ = 4
A_WIDTH = 512
A_HEADS = 8
A_HEAD_DIM = 64
CHUNK = 128
B_HEADS = 8
B_KV_HEADS = 2
B_HEAD_DIM = 64
B_GROUP = 4
WINDOW = 128
N_BUCKETS = 32
MAX_DISTANCE = WINDOW
Q_WIDTH = 512
KV_WIDTH = 128
IN_WIDTH = 2 * A_WIDTH + Q_WIDTH + 2 * KV_WIDTH
ATTN_SCALE = B_HEAD_DIM ** -0.5
NEG_INF = -1e30
POOL_SIZES = (2, 4, 8, 16)
POOL_GROUP_DIM = 256
POOL_MAX = 16
N_GROUPS = 4
EXPERTS_PER_GROUP = 8
N_EXPERTS = 32
TOP_K = 2
D_EXPERT = 512
EPS = 1e-6

LANES = 128
ROW_TILE = D_MODEL // LANES
T_PROMPT = BATCH * SEQ
T_SAMPLE = DEC_BATCH * DEC_SEQ
T_ALL = T_PROMPT + T_SAMPLE
TM = 512
N_PROMPT_BLOCKS = T_PROMPT // TM
N_ROW_BLOCKS = T_ALL // TM
STEPS_PER_BATCH = SEQ // TM
SUB = TM // WINDOW
N_SLOTS = T_ALL * TOP_K
MOE_BLK = 512
N_MOE_BLOCKS = N_SLOTS // MOE_BLK + N_EXPERTS
N_SORT_ROWS = N_MOE_BLOCKS * MOE_BLK
SAMPLE_GROUP = 8
N_SAMPLE_GROUPS = DEC_BATCH // SAMPLE_GROUP
VMEM_LIMIT = 56 * 1024 * 1024

STACK_HEADS = ((0, 2, 5, 7), (1, 3, 4, 6))


def _t5_bucket_np(dist):
    n = np.maximum(dist, 0)
    max_exact = N_BUCKETS // 2
    nf = np.maximum(n, 1).astype(np.float32)
    large = max_exact + (np.log(nf / np.float32(max_exact)) / np.float32(math.log(MAX_DISTANCE / max_exact))
                         * np.float32(N_BUCKETS - max_exact)).astype(np.int32)
    large = np.minimum(large, N_BUCKETS - 1)
    return np.where(n < max_exact, n, large).astype(np.int32)


def _bucket_tables():
    qi = np.arange(WINDOW)[:, None]
    ki = np.arange(2 * WINDOW)[None, :]
    dist = qi + WINDOW - ki
    valid = (dist >= 0) & (dist < WINDOW)
    bp = np.where(valid, _t5_bucket_np(dist), -1)
    bp_first = np.where(ki >= WINDOW, bp, -1)
    bkt_p = np.stack([bp_first, bp]).astype(np.int32)

    t = np.repeat(np.arange(DEC_SEQ), SAMPLE_GROUP)[:, None]
    b = np.tile(np.arange(SAMPLE_GROUP), DEC_SEQ)[:, None]
    cb = np.repeat(np.arange(SAMPLE_GROUP), WINDOW)[None, :]
    cj = np.tile(np.arange(WINDOW), SAMPLE_GROUP)[None, :]
    dist_c = t + WINDOW - cj
    valid_c = (cb == b) & (dist_c >= 0) & (dist_c < WINDOW)
    bkt_sc = np.where(valid_c, _t5_bucket_np(dist_c), -1).astype(np.int32)
    nt = np.repeat(np.arange(DEC_SEQ), SAMPLE_GROUP)[None, :]
    nb = np.tile(np.arange(SAMPLE_GROUP), DEC_SEQ)[None, :]
    dist_n = t - nt
    valid_n = (nb == b) & (dist_n >= 0)
    bkt_sn = np.where(valid_n, _t5_bucket_np(dist_n), -1).astype(np.int32)
    bkt_sn = np.concatenate([bkt_sn, np.full((32, LANES - 32), -1, np.int32)], axis=1)
    return bkt_p, bkt_sc, bkt_sn


_BKT_P, _BKT_SC, _BKT_SN = _bucket_tables()


def _cparams(semantics):
    return pltpu.CompilerParams(dimension_semantics=semantics, vmem_limit_bytes=VMEM_LIMIT)


def _rms(x, g):
    return x * lax.rsqrt(jnp.mean(x * x, axis=-1, keepdims=True) + EPS) * g


def _layernorm(x, g, b):
    xc = x - jnp.mean(x, axis=-1, keepdims=True)
    return xc * lax.rsqrt(jnp.mean(xc * xc, axis=-1, keepdims=True) + EPS) * g + b


def _dot(a, b):
    return jnp.dot(a, b, preferred_element_type=F32)


def _dot_nt(a, b):
    return lax.dot_general(a, b, (((1,), (1,)), ((), ())), preferred_element_type=F32)


def _project(x, nm, win, lng, lnb):
    h = _rms(x, nm)
    z = _dot(h.astype(BF16), win)
    u = jax.nn.gelu(z[:, :A_WIDTH])
    va = _layernorm(jax.nn.gelu(z[:, A_WIDTH:2 * A_WIDTH]), lng, lnb)
    q = z[:, 2 * A_WIDTH:2 * A_WIDTH + Q_WIDTH] * ATTN_SCALE
    k = z[:, 2 * A_WIDTH + Q_WIDTH:2 * A_WIDTH + Q_WIDTH + KV_WIDTH]
    v = z[:, 2 * A_WIDTH + Q_WIDTH + KV_WIDTH:]
    return u, va, q, k, v


def _route(x1, nf, wr, br):
    hf = _rms(x1, nf)
    h = hf.astype(BF16)
    h_lo = (hf - h.astype(F32)).astype(BF16)
    part = _dot(h, wr)
    logits = part[:, :LANES] + part[:, LANES:] + _dot(h_lo, wr[:, :LANES]) + br
    rows = logits.shape[0]
    lane = lax.broadcasted_iota(jnp.int32, (rows, LANES), 1)
    lanef = lane.astype(F32)
    big = jnp.float32(1e9)
    is_g = lane < N_GROUPS
    gl = jnp.where(is_g, logits, -jnp.inf)
    gmax = jnp.max(gl, axis=1, keepdims=True)
    gsel = jnp.min(jnp.where(gl == gmax, lanef, big), axis=1, keepdims=True)
    gsum = jnp.sum(jnp.where(is_g, jnp.exp(logits - gmax), 0.0), axis=1, keepdims=True)
    g1 = 1.0 / gsum
    lo = N_GROUPS + EXPERTS_PER_GROUP * gsel
    emask = (lanef >= lo) & (lanef < lo + EXPERTS_PER_GROUP)
    el = jnp.where(emask, logits, -jnp.inf)
    v1 = jnp.max(el, axis=1, keepdims=True)
    i1 = jnp.min(jnp.where(el == v1, lanef, big), axis=1, keepdims=True)
    el2 = jnp.where(lanef == i1, -jnp.inf, el)
    v2 = jnp.max(el2, axis=1, keepdims=True)
    i2 = jnp.min(jnp.where(el2 == v2, lanef, big), axis=1, keepdims=True)
    e2 = jnp.exp(v2 - v1)
    den = 1.0 + e2
    w1 = g1 / den
    w2 = g1 * e2 / den
    ids = jnp.where(lane == 0, i1 - N_GROUPS, jnp.where(lane == 1, i2 - N_GROUPS, 0.0)).astype(jnp.int32)
    gates = jnp.where(lane == 0, w1, jnp.where(lane == 1, w2, 0.0))
    return h, ids, gates


def _rank_pack(ids, cnt_ref, tcnt_ref):
    rows = ids.shape[0]
    lane = lax.broadcasted_iota(jnp.int32, (rows, LANES), 1)
    o0 = (lane == ids[:, 0:1]).astype(F32)
    o1 = (lane == ids[:, 1:2]).astype(F32)
    r = lax.broadcasted_iota(jnp.int32, (rows, rows), 0)
    c = lax.broadcasted_iota(jnp.int32, (rows, rows), 1)
    before = (c < r).astype(BF16)
    p01 = _dot(before, jnp.concatenate([o0, o1], axis=1).astype(BF16))
    p0 = p01[:, :LANES]
    p1 = p01[:, LANES:]
    c0 = jnp.sum(o0, axis=0, keepdims=True)
    c1 = jnp.sum(o1, axis=0, keepdims=True)
    ctile = c0 + c1
    cnt_ref[...] = cnt_ref[...] + ctile
    tcnt_ref[...] = ctile
    inc = jnp.broadcast_to(ctile, (8, LANES))
    lane8 = lax.broadcasted_iota(jnp.int32, (8, LANES), 1)
    for sh in (1, 2, 4, 8, 16, 32, 64):
        inc = inc + jnp.where(lane8 >= sh, pltpu.roll(inc, sh, 1), 0.0)
    start = inc[0:1] - ctile
    lpos0 = jnp.sum(o0 * (start + p0), axis=1, keepdims=True)
    lpos1 = jnp.sum(o1 * (start + c0 + p1), axis=1, keepdims=True)
    idf = ids.astype(F32)
    packed = jnp.where(lane < TOP_K, idf, 0.0)
    for ln, col in ((4, lpos0), (5, lpos1)):
        packed = jnp.where(lane == ln, col, packed)
    return jnp.transpose(packed)[:8].astype(jnp.int32)


def _prep_kernel(tab_ref, bp_ref, bsc_ref, bsn_ref, ws_ref, op_ref, osc_ref, osn_ref, ows_ref):
    def fill(bkt, write):
        for h in range(B_HEADS):
            acc = jnp.full(bkt.shape, NEG_INF, F32)
            for b in range(N_BUCKETS):
                acc = jnp.where(bkt == b, tab_ref[b, h], acc)
            write(h, acc)

    for var in range(2):
        def wr_p(h, acc, var=var):
            op_ref[var, h] = acc
        fill(bp_ref[var], wr_p)

    def wr_sc(h, acc):
        osc_ref[h] = acc
    fill(bsc_ref[...], wr_sc)

    def wr_sn(h, acc):
        osn_ref[h] = acc
    fill(bsn_ref[...], wr_sn)

    r = lax.broadcasted_iota(jnp.int32, (CHUNK, CHUNK), 0)
    c = lax.broadcasted_iota(jnp.int32, (CHUNK, CHUNK), 1)
    for h in range(A_HEADS):
        ows_ref[h] = jnp.where(r >= c, ws_ref[h], 0.0).astype(BF16)


def _prep(rel_bias_table, w_s):
    vm = pl.BlockSpec(memory_space=pltpu.VMEM)
    return pl.pallas_call(
        _prep_kernel,
        in_specs=[pl.BlockSpec(memory_space=pltpu.SMEM), vm, vm, vm, vm],
        out_specs=[vm, vm, vm, vm],
        out_shape=[
            jax.ShapeDtypeStruct((2, B_HEADS, WINDOW, 2 * WINDOW), F32),
            jax.ShapeDtypeStruct((B_HEADS, 32, SAMPLE_GROUP * WINDOW), F32),
            jax.ShapeDtypeStruct((B_HEADS, 32, LANES), F32),
            jax.ShapeDtypeStruct((A_HEADS, CHUNK, CHUNK), BF16),
        ],
        name="prep_tables",
    )(rel_bias_table, jnp.asarray(_BKT_P), jnp.asarray(_BKT_SC), jnp.asarray(_BKT_SN), w_s)


def _gate_pairs(va_rows, wsp_ref, lane_lo):
    outs = []
    for p in range(A_HEADS // 2):
        vp = va_rows[:, p * LANES:(p + 1) * LANES]
        rhs = jnp.concatenate([jnp.where(lane_lo, vp, 0.0), jnp.where(lane_lo, 0.0, vp)], axis=0).astype(BF16)
        outs.append(_dot(wsp_ref[p], rhs))
    return jnp.concatenate(outs, axis=1)


def _prompt_steps(body, first_row_out):
    def kern(*refs):
        i = pl.program_id(0)

        @pl.when(i < N_PROMPT_BLOCKS)
        def _():
            body(*refs)

        @pl.when(i >= N_PROMPT_BLOCKS)
        def _():
            for r in refs[first_row_out:first_row_out + 5]:
                r[...] = jnp.zeros(r.shape, r.dtype)

    return kern


def _mix0_prompt_kernel(x_ref, nm_ref, win_ref, lng_ref, lnb_ref, wsp_ref, bs_ref, bias_ref,
                        wout_ref, nf_ref, wr_ref, br_ref,
                        x1_ref, h_ref, ri_ref, rg_ref, tc_ref, kl_ref, vl_ref, val_ref, cnt_ref,
                        kprev, vprev, mix_scr):
    @pl.when(pl.program_id(0) == 0)
    def _():
        cnt_ref[...] = jnp.zeros_like(cnt_ref)

    x = x_ref[...]
    u, va, q, k, v = _project(x, nm_ref[...], win_ref[...], lng_ref[...], lnb_ref[...])
    lane_lo = lax.broadcasted_iota(jnp.int32, (WINDOW, LANES), 1) < B_HEAD_DIM
    row0 = lax.broadcasted_iota(jnp.int32, (WINDOW, KV_WIDTH), 0) == 0
    first = pl.program_id(0) % STEPS_PER_BATCH == 0

    @pl.when(first)
    def _():
        kprev[...] = jnp.zeros_like(kprev)
        vprev[...] = jnp.zeros_like(vprev)

    for j in range(SUB):
        rows = slice(j * WINDOW, (j + 1) * WINDOW)
        s_gate = _gate_pairs(va[rows], wsp_ref, lane_lo)
        mix_scr[rows, :A_WIDTH] = u[rows] * (s_gate + bs_ref[...])

        if j == 0:
            kp, vp = kprev[...], vprev[...]
        else:
            prows = slice((j - 1) * WINDOW, j * WINDOW)
            kp, vp = k[prows], v[prows]
        kk = jnp.concatenate([jnp.where(row0, 0.0, kp), k[rows]], axis=0)
        vv = jnp.concatenate([jnp.where(row0, 0.0, vp), v[rows]], axis=0)
        kops = (kk.astype(BF16), pltpu.roll(kk, B_HEAD_DIM, 1).astype(BF16))
        vops = (vv.astype(BF16), pltpu.roll(vv, B_HEAD_DIM, 1).astype(BF16))
        qt = [q[rows, p * LANES:(p + 1) * LANES] for p in range(4)]
        q_even = [jnp.where(lane_lo, t, 0.0) for t in qt]
        q_odd = [jnp.where(lane_lo, 0.0, t) for t in qt]
        stacks = (jnp.concatenate([q_even[0], q_even[1], q_odd[2], q_odd[3]], axis=0),
                  jnp.concatenate([q_odd[0], q_odd[1], q_even[2], q_even[3]], axis=0))
        o = []
        for st in range(2):
            s = _dot_nt(stacks[st].astype(BF16), kops[st])
            if j == 0:
                bias = bias_ref[jnp.where(first, 0, 1), st]
            else:
                bias = bias_ref[1, st]
            s = s + bias
            m = jnp.max(s, axis=-1, keepdims=True)
            p = jnp.exp(s - m)
            den = jnp.sum(p, axis=-1, keepdims=True)
            o.append(_dot(p.astype(BF16), vops[st]) / den)
        oa, ob = o
        sl = [slice(i * WINDOW, (i + 1) * WINDOW) for i in range(4)]
        tiles = (jnp.where(lane_lo, oa[sl[0]], ob[sl[0]]), jnp.where(lane_lo, oa[sl[1]], ob[sl[1]]),
                 jnp.where(lane_lo, ob[sl[2]], oa[sl[2]]), jnp.where(lane_lo, ob[sl[3]], oa[sl[3]]))
        for p in range(4):
            mix_scr[rows, A_WIDTH + p * LANES:A_WIDTH + (p + 1) * LANES] = tiles[p]

    last = slice(TM - WINDOW, TM)
    kprev[...] = k[last]
    vprev[...] = v[last]
    kl_ref[...] = k[last]
    vl_ref[...] = v[last]
    val_ref[...] = va[last]

    x1 = x + _dot(mix_scr[...].astype(BF16), wout_ref[...])
    x1_ref[...] = x1
    h, ids, gates = _route(x1, nf_ref[...], wr_ref[...], br_ref[...])
    h_ref[...] = h.reshape(h_ref.shape)
    ri_ref[...] = _rank_pack(ids, cnt_ref, tc_ref)
    rg_ref[...] = gates


def _const_spec(shape):
    nd = len(shape)
    return pl.BlockSpec(shape, lambda i, _n=nd: (0,) * _n)


def _mix0_prompt(x_all, nm, win, lng, lnb, wsp, bs_full, bias_p, wout, nf, wr, br):
    row_spec = pl.BlockSpec((TM, D_MODEL), lambda i: (i, 0))
    row3_spec = pl.BlockSpec((TM, ROW_TILE, LANES), lambda i: (i, 0, 0))
    lane_spec = pl.BlockSpec((TM, LANES), lambda i: (i, 0))
    last_kv = pl.BlockSpec((None, WINDOW, KV_WIDTH), lambda i: (jnp.minimum(i // STEPS_PER_BATCH, BATCH - 1), 0, 0))
    last_va = pl.BlockSpec((None, WINDOW, A_WIDTH), lambda i: (jnp.minimum(i // STEPS_PER_BATCH, BATCH - 1), 0, 0))
    return pl.pallas_call(
        _prompt_steps(_mix0_prompt_kernel, 12),
        grid=(N_ROW_BLOCKS,),
        in_specs=[pl.BlockSpec((TM, D_MODEL), lambda i: (jnp.minimum(i, N_PROMPT_BLOCKS - 1), 0)),
                  _const_spec((1, D_MODEL)), _const_spec((D_MODEL, IN_WIDTH)),
                  _const_spec((1, A_WIDTH)), _const_spec((1, A_WIDTH)),
                  _const_spec((A_HEADS // 2, CHUNK, 2 * CHUNK)), _const_spec((CHUNK, A_WIDTH)),
                  _const_spec((2, 2, 4 * WINDOW, 2 * WINDOW)),
                  _const_spec((A_WIDTH + Q_WIDTH, D_MODEL)), _const_spec((1, D_MODEL)),
                  _const_spec((D_MODEL, 2 * LANES)), _const_spec((1, LANES))],
        out_specs=[row_spec, row3_spec, pl.BlockSpec((8, TM), lambda i: (0, i)), lane_spec,
                   pl.BlockSpec((None, 1, LANES), lambda i: (i, 0, 0)),
                   last_kv, last_kv, last_va, _const_spec((1, LANES))],
        out_shape=[jax.ShapeDtypeStruct((T_ALL, D_MODEL), F32), jax.ShapeDtypeStruct((T_ALL, ROW_TILE, LANES), BF16),
                   jax.ShapeDtypeStruct((8, T_ALL), jnp.int32), jax.ShapeDtypeStruct((T_ALL, LANES), F32),
                   jax.ShapeDtypeStruct((N_ROW_BLOCKS, 1, LANES), F32),
                   jax.ShapeDtypeStruct((BATCH, WINDOW, KV_WIDTH), F32),
                   jax.ShapeDtypeStruct((BATCH, WINDOW, KV_WIDTH), F32),
                   jax.ShapeDtypeStruct((BATCH, WINDOW, A_WIDTH), F32),
                   jax.ShapeDtypeStruct((1, LANES), F32)],
        scratch_shapes=[pltpu.VMEM((WINDOW, KV_WIDTH), F32), pltpu.VMEM((WINDOW, KV_WIDTH), F32),
                        pltpu.VMEM((TM, D_MODEL), F32)],
        compiler_params=_cparams(("arbitrary",)),
        name="mix0_prompt",
    )(x_all, nm, win, lng, lnb, wsp, bs_full, bias_p, wout, nf, wr, br)


def _mix0_sample_kernel(x_ref, nm_ref, win_ref, lng_ref, lnb_ref, wcoef_ref, bcoef_ref,
                        ck_ref, cv_ref, bsc_ref, bsn_ref,
                        wout_ref, nf_ref, wr_ref, br_ref, cnt_in,
                        x1_in, h_in, ri_in, rg_in, tc_in,
                        x1_ref, h_ref, ri_ref, rg_ref, tc_ref, kn_ref, vn_ref, va_ref, cnt_ref,
                        q_scr, k_scr, v_scr, mix_scr):
    del x1_in, h_in, ri_in, rg_in, tc_in
    g = pl.program_id(0)

    @pl.when(g == 0)
    def _():
        u, va, q, k, v = _project(x_ref[...], nm_ref[...], win_ref[...], lng_ref[...], lnb_ref[...])
        q_scr[...] = q
        k_scr[...] = k
        v_scr[...] = v
        kn_ref[...] = k
        vn_ref[...] = v
        va_ref[...] = va
        idx = 0
        for t in range(DEC_SEQ):
            acc = jnp.zeros((DEC_BATCH, A_WIDTH), F32) + bcoef_ref[t:t + 1, :]
            for s in range(t + 1):
                acc = acc + wcoef_ref[idx:idx + 1, :] * va[s * DEC_BATCH:(s + 1) * DEC_BATCH]
                idx += 1
            mix_scr[t * DEC_BATCH:(t + 1) * DEC_BATCH, :A_WIDTH] = u[t * DEC_BATCH:(t + 1) * DEC_BATCH] * acc

    b0 = pl.multiple_of(g * SAMPLE_GROUP, SAMPLE_GROUP)
    lane_lo = lax.broadcasted_iota(jnp.int32, (DEC_SEQ * SAMPLE_GROUP, LANES), 1) < B_HEAD_DIM

    def grab(ref, width):
        return jnp.concatenate([ref[pl.ds(t * DEC_BATCH + b0, SAMPLE_GROUP), :] for t in range(DEC_SEQ)], axis=0)

    qg = grab(q_scr, Q_WIDTH)
    kn = grab(k_scr, KV_WIDTH)
    vn = grab(v_scr, KV_WIDTH)
    crow0 = lax.broadcasted_iota(jnp.int32, (SAMPLE_GROUP * WINDOW, KV_WIDTH), 0) == 0
    rows_kv = (SAMPLE_GROUP * WINDOW, KV_WIDTH)
    kc = jnp.where(crow0, 0.0, ck_ref[...].reshape(rows_kv))
    vc = jnp.where(crow0, 0.0, cv_ref[...].reshape(rows_kv))
    kc_ops = (kc.astype(BF16), pltpu.roll(kc, B_HEAD_DIM, 1).astype(BF16))
    vc_ops = (vc.astype(BF16), pltpu.roll(vc, B_HEAD_DIM, 1).astype(BF16))
    kn_ops = (kn.astype(BF16), pltpu.roll(kn, B_HEAD_DIM, 1).astype(BF16))
    vn_ops = (vn.astype(BF16), pltpu.roll(vn, B_HEAD_DIM, 1).astype(BF16))
    qt = [qg[:, p * LANES:(p + 1) * LANES] for p in range(4)]
    q_even = [jnp.where(lane_lo, t, 0.0) for t in qt]
    q_odd = [jnp.where(lane_lo, 0.0, t) for t in qt]
    stacks = (jnp.concatenate([q_even[0], q_even[1], q_odd[2], q_odd[3]], axis=0),
              jnp.concatenate([q_odd[0], q_odd[1], q_even[2], q_even[3]], axis=0))
    o = []
    for st in range(2):
        qs = stacks[st].astype(BF16)
        sc = _dot_nt(qs, kc_ops[st]) + bsc_ref[st]
        sn = _dot_nt(qs, kn_ops[st]) + bsn_ref[st][:, :DEC_SEQ * SAMPLE_GROUP]
        m = jnp.maximum(jnp.max(sc, axis=-1, keepdims=True), jnp.max(sn, axis=-1, keepdims=True))
        pc = jnp.exp(sc - m)
        pn = jnp.exp(sn - m)
        den = jnp.sum(pc, axis=-1, keepdims=True) + jnp.sum(pn, axis=-1, keepdims=True)
        o.append((_dot(pc.astype(BF16), vc_ops[st]) + _dot(pn.astype(BF16), vn_ops[st])) / den)
    oa, ob = o
    n = DEC_SEQ * SAMPLE_GROUP
    sl = [slice(i * n, (i + 1) * n) for i in range(4)]
    tiles = (jnp.where(lane_lo, oa[sl[0]], ob[sl[0]]), jnp.where(lane_lo, oa[sl[1]], ob[sl[1]]),
             jnp.where(lane_lo, ob[sl[2]], oa[sl[2]]), jnp.where(lane_lo, ob[sl[3]], oa[sl[3]]))
    for p in range(4):
        for t in range(DEC_SEQ):
            mix_scr[pl.ds(t * DEC_BATCH + b0, SAMPLE_GROUP), A_WIDTH + p * LANES:A_WIDTH + (p + 1) * LANES] = (
                tiles[p][t * SAMPLE_GROUP:(t + 1) * SAMPLE_GROUP])

    @pl.when(g == N_SAMPLE_GROUPS - 1)
    def _():
        x1 = x_ref[...] + _dot(mix_scr[...].astype(BF16), wout_ref[...])
        x1_ref[...] = x1
        h, ids, gates = _route(x1, nf_ref[...], wr_ref[...], br_ref[...])
        h_ref[...] = h.reshape(h_ref.shape)
        cnt_ref[...] = cnt_in[...]
        ri_ref[...] = _rank_pack(ids, cnt_ref, tc_ref)
        rg_ref[...] = gates


def _mix0_sample(x_all, nm, win, lng, lnb, wcoef, bcoef, ck, cv, bias_sc, bias_sn, wout, nf, wr, br, cnt,
                 x1_all, h_all, ri_all, rg_all, tc_all):
    sample_rows = pl.BlockSpec((TM, D_MODEL), lambda g: (N_PROMPT_BLOCKS, 0))
    sample_rows3 = pl.BlockSpec((TM, ROW_TILE, LANES), lambda g: (N_PROMPT_BLOCKS, 0, 0))
    sample_lanes = pl.BlockSpec((TM, LANES), lambda g: (N_PROMPT_BLOCKS, 0))
    cache_spec = pl.BlockSpec((None, SAMPLE_GROUP, WINDOW, B_KV_HEADS, B_HEAD_DIM), lambda g: (0, g, 0, 0, 0))
    anyspec = pl.BlockSpec(memory_space=pl.ANY)
    n_in = 16
    return pl.pallas_call(
        _mix0_sample_kernel,
        grid=(N_SAMPLE_GROUPS,),
        in_specs=[_const_spec((TM, D_MODEL)), _const_spec((1, D_MODEL)), _const_spec((D_MODEL, IN_WIDTH)),
                  _const_spec((1, A_WIDTH)), _const_spec((1, A_WIDTH)),
                  _const_spec((16, A_WIDTH)), _const_spec((8, A_WIDTH)),
                  cache_spec, cache_spec,
                  _const_spec((2, 4 * 32, SAMPLE_GROUP * WINDOW)), _const_spec((2, 4 * 32, LANES)),
                  _const_spec((A_WIDTH + Q_WIDTH, D_MODEL)), _const_spec((1, D_MODEL)),
                  _const_spec((D_MODEL, 2 * LANES)), _const_spec((1, LANES)), _const_spec((1, LANES)),
                  anyspec, anyspec, anyspec, anyspec, anyspec],
        out_specs=[sample_rows, sample_rows3, pl.BlockSpec((8, TM), lambda g: (0, N_PROMPT_BLOCKS)), sample_lanes,
                   pl.BlockSpec((None, 1, LANES), lambda g: (N_PROMPT_BLOCKS, 0, 0)),
                   _const_spec((T_SAMPLE, KV_WIDTH)), _const_spec((T_SAMPLE, KV_WIDTH)),
                   _const_spec((T_SAMPLE, A_WIDTH)), _const_spec((1, LANES))],
        out_shape=[jax.ShapeDtypeStruct((T_ALL, D_MODEL), F32), jax.ShapeDtypeStruct((T_ALL, ROW_TILE, LANES), BF16),
                   jax.ShapeDtypeStruct((8, T_ALL), jnp.int32), jax.ShapeDtypeStruct((T_ALL, LANES), F32),
                   jax.ShapeDtypeStruct((N_ROW_BLOCKS, 1, LANES), F32),
                   jax.ShapeDtypeStruct((T_SAMPLE, KV_WIDTH), F32), jax.ShapeDtypeStruct((T_SAMPLE, KV_WIDTH), F32),
                   jax.ShapeDtypeStruct((T_SAMPLE, A_WIDTH), F32), jax.ShapeDtypeStruct((1, LANES), F32)],
        scratch_shapes=[pltpu.VMEM((T_SAMPLE, Q_WIDTH), F32), pltpu.VMEM((T_SAMPLE, KV_WIDTH), F32),
                        pltpu.VMEM((T_SAMPLE, KV_WIDTH), F32), pltpu.VMEM((T_SAMPLE, D_MODEL), F32)],
        input_output_aliases={n_in: 0, n_in + 1: 1, n_in + 2: 2, n_in + 3: 3, n_in + 4: 4},
        compiler_params=_cparams(("arbitrary",)),
        name="mix0_sample",
    )(x_all, nm, win, lng, lnb, wcoef, bcoef, ck, cv, bias_sc, bias_sn, wout, nf, wr, br, cnt,
      x1_all, h_all, ri_all, rg_all, tc_all)


def _moe_metadata(rt_all, cnt, tcnt):
    counts = cnt[0, :N_EXPERTS].astype(jnp.int32)
    padded = (counts + MOE_BLK - 1) // MOE_BLK * MOE_BLK
    pad_end = jnp.cumsum(padded)
    pad_start = pad_end - padded
    experts = jnp.arange(N_EXPERTS, dtype=jnp.int32)
    n_valid = (pad_end[-1] // MOE_BLK).astype(jnp.int32).reshape(1)
    blk_start = jnp.arange(N_MOE_BLOCKS, dtype=jnp.int32) * MOE_BLK
    block_e = jnp.minimum(jnp.sum((blk_start[:, None] >= pad_end[None, :]).astype(jnp.int32), axis=1),
                          N_EXPERTS - 1).astype(jnp.int32)
    zero_start = (pad_start + counts).astype(jnp.int32)
    zero_len = (padded - counts).astype(jnp.int32)
    first = (blk_start == pad_start[block_e]).astype(jnp.int32)
    used = counts > 0
    parity = ((jnp.cumsum(used.astype(jnp.int32)) - 1) % 2)[block_e].astype(jnp.int32)
    nearest = lax.cummin(jnp.where(used, experts, N_EXPERTS)[::-1])[::-1]
    next_used = jnp.concatenate([nearest[1:], jnp.full((1,), N_EXPERTS, jnp.int32)])
    nxt = jnp.where(next_used < N_EXPERTS, next_used, -1)[block_e].astype(jnp.int32)
    n_rows = jnp.clip(counts[block_e] - (blk_start - pad_start[block_e]), 1, MOE_BLK).astype(jnp.int32)
    plan = (block_e, first, parity, nxt, n_rows, n_valid)
    runs = tcnt[:, 0, :N_EXPERTS].astype(jnp.int32)
    run_dst = pad_start[None, :] + jnp.cumsum(runs, axis=0) - runs
    lpos = rt_all[2 * TOP_K:3 * TOP_K].reshape(N_SLOTS).astype(jnp.int32)
    cplan = (lpos, runs.reshape(-1), run_dst.reshape(-1).astype(jnp.int32))
    dplan = cplan + (jnp.concatenate([zero_start, zero_len, n_valid]),)
    return plan, dplan, cplan


RUN_PIECE = 32


def _for_run_pieces(n, start_piece):
    whole = n // RUN_PIECE

    def body(j, carry):
        start_piece(j * RUN_PIECE, RUN_PIECE)
        return carry

    lax.fori_loop(0, whole, body, 0)
    o = whole * RUN_PIECE
    bit = RUN_PIECE // 2
    while bit >= 1:
        take = (n & bit) != 0

        @pl.when(take)
        def _(o=o, bit=bit):
            start_piece(o, bit)

        o = o + jnp.where(take, bit, 0)
        bit //= 2


def _dispatch_kernel(lpos_ref, run_ref, rdst_ref, zs_ref, h_ref, xs_ref, zero_scr, stage, sem, zsem):
    i = pl.program_id(0)

    @pl.when(i == 0)
    def _():
        zero_scr[...] = jnp.zeros_like(zero_scr)

        def pieces(e, do):
            off = zs_ref[e]
            rem = zs_ref[N_EXPERTS + e]
            bit = MOE_BLK // 2
            while bit >= 1:
                take = (rem & bit) != 0

                @pl.when(take)
                def _(off=off, bit=bit):
                    do(pltpu.make_async_copy(zero_scr.at[pl.ds(0, bit)], xs_ref.at[pl.ds(off, bit)], zsem))

                off = off + jnp.where(take, bit, 0)
                bit //= 2

        def start_e(e, c):
            pieces(e, lambda cp: cp.start())
            return c

        def wait_e(e, c):
            pieces(e, lambda cp: cp.wait())
            return c

        def tail(do):
            def step(b, c):
                do(pltpu.make_async_copy(zero_scr, xs_ref.at[pl.ds(b * MOE_BLK, MOE_BLK)], zsem))
                return c
            return step

        n_valid = zs_ref[2 * N_EXPERTS]
        lax.fori_loop(0, N_EXPERTS, start_e, 0)
        lax.fori_loop(n_valid, N_MOE_BLOCKS, tail(lambda cp: cp.start()), 0)
        lax.fori_loop(0, N_EXPERTS, wait_e, 0)
        lax.fori_loop(n_valid, N_MOE_BLOCKS, tail(lambda cp: cp.wait()), 0)

    base = i * TM

    def place(r, carry):
        row = h_ref[r]
        for kk in range(TOP_K):
            stage[lpos_ref[kk * T_ALL + base + r]] = row
        return carry

    lax.fori_loop(0, TM, place, 0, unroll=8)

    def send_run(e, off):
        n = run_ref[i * N_EXPERTS + e]
        dst = rdst_ref[i * N_EXPERTS + e]
        _for_run_pieces(n, lambda o, size: pltpu.make_async_copy(
            stage.at[pl.ds(off + o, size)], xs_ref.at[pl.ds(dst + o, size)], sem).start(
                priority=size.bit_length() % 2))
        return off + n

    lax.fori_loop(0, N_EXPERTS, send_run, 0)
    pltpu.make_async_copy(stage, xs_ref.at[pl.ds(0, TM * TOP_K)], sem).wait()


def _dispatch(dplan, h_all):
    return pl.pallas_call(
        _dispatch_kernel,
        grid_spec=pltpu.PrefetchScalarGridSpec(
            num_scalar_prefetch=4,
            grid=(N_ROW_BLOCKS,),
            in_specs=[pl.BlockSpec((TM, ROW_TILE, LANES), lambda i, lp, rn, rd, z: (i, 0, 0))],
            out_specs=pl.BlockSpec(memory_space=pl.ANY),
            scratch_shapes=[pltpu.VMEM((MOE_BLK, ROW_TILE, LANES), BF16),
                            pltpu.VMEM((TM * TOP_K, ROW_TILE, LANES), BF16),
                            pltpu.SemaphoreType.DMA(()), pltpu.SemaphoreType.DMA(())],
        ),
        out_shape=jax.ShapeDtypeStruct((N_SORT_ROWS, ROW_TILE, LANES), BF16),
        compiler_params=_cparams(("arbitrary",)),
        name="moe_dispatch",
    )(*dplan, h_all)


def _experts_kernel(layer, be_ref, first_ref, par_ref, nxt_ref, nrows_ref, nv_ref,
                    x_ref, wg_hbm, wu_hbm, wd_hbm, y_ref,
                    wg_s, wu_s, wd_s, wg_f, wu_f, wd_f, wsem):
    i = pl.program_id(0)

    def fetch(e, slot):
        return (pltpu.make_async_copy(wg_hbm.at[layer, e], wg_f.at[slot], wsem.at[slot]),
                pltpu.make_async_copy(wu_hbm.at[layer, e], wu_f.at[slot], wsem.at[slot]),
                pltpu.make_async_copy(wd_hbm.at[layer, e], wd_f.at[slot], wsem.at[slot]))

    @pl.when(i < nv_ref[0])
    def _():
        e = be_ref[i]
        slot = par_ref[i]

        @pl.when(i == 0)
        def _():
            for cp in fetch(e, slot):
                cp.start()

        @pl.when(first_ref[i] == 1)
        def _():
            for cp in fetch(e, slot):
                cp.wait()
            wg_s[...] = wg_f[slot].astype(BF16)
            wu_s[...] = wu_f[slot].astype(BF16)
            wd_s[...] = wd_f[slot].astype(BF16)
            nxt = nxt_ref[i]

            @pl.when(nxt >= 0)
            def _():
                for cp in fetch(nxt, 1 - slot):
                    cp.start()

        quarter = MOE_BLK // 4
        n_quarters = (nrows_ref[i] + quarter - 1) // quarter

        def mlp(rows):
            xb = x_ref[0:rows].reshape(rows, D_MODEL)
            a = jax.nn.silu(_dot(xb, wg_s[...])) * _dot(xb, wu_s[...])
            y_ref[0:rows] = _dot(a.astype(BF16), wd_s[...]).reshape(rows, ROW_TILE, LANES)
            if rows < MOE_BLK:
                y_ref[rows:MOE_BLK] = jnp.zeros((MOE_BLK - rows, ROW_TILE, LANES), y_ref.dtype)

        for q in range(1, 5):
            @pl.when(n_quarters == q)
            def _(q=q):
                mlp(q * quarter)

    @pl.when(i >= nv_ref[0])
    def _():
        y_ref[...] = jnp.zeros(y_ref.shape, y_ref.dtype)


def _experts(block_e, first, parity, nxt, n_rows, n_valid, xs, w_gate, w_up, w_down, layer):
    def blk(i, be, fi, pa, nx, nr, nv):
        return (jnp.maximum(jnp.minimum(i, nv[0] - 1), 0), 0, 0)

    anyspec = pl.BlockSpec(memory_space=pl.ANY)
    return pl.pallas_call(
        functools.partial(_experts_kernel, layer),
        grid_spec=pltpu.PrefetchScalarGridSpec(
            num_scalar_prefetch=6,
            grid=(N_MOE_BLOCKS,),
            in_specs=[pl.BlockSpec((MOE_BLK, ROW_TILE, LANES), blk), anyspec, anyspec, anyspec],
            out_specs=pl.BlockSpec((MOE_BLK, ROW_TILE, LANES), lambda i, be, fi, pa, nx, nr, nv: (i, 0, 0)),
            scratch_shapes=[pltpu.VMEM((D_MODEL, D_EXPERT), BF16), pltpu.VMEM((D_MODEL, D_EXPERT), BF16),
                            pltpu.VMEM((D_EXPERT, D_MODEL), BF16),
                            pltpu.VMEM((2, D_MODEL, D_EXPERT), F32), pltpu.VMEM((2, D_MODEL, D_EXPERT), F32),
                            pltpu.VMEM((2, D_EXPERT, D_MODEL), F32), pltpu.SemaphoreType.DMA((2,))],
        ),
        out_shape=jax.ShapeDtypeStruct((N_SORT_ROWS, ROW_TILE, LANES), F32),
        compiler_params=_cparams(("arbitrary",)),
        name="moe_experts",
    )(block_e, first, parity, nxt, n_rows, n_valid, xs, w_gate, w_up, w_down)


def _gather_rows(lpos_ref, run_ref, rdst_ref, ys_ref, ystage, ybuf, sem, i):
    def fetch(tile, buf):
        def fetch_run(e, off):
            n = run_ref[tile * N_EXPERTS + e]
            src = rdst_ref[tile * N_EXPERTS + e]
            _for_run_pieces(n, lambda o, size: pltpu.make_async_copy(
                ys_ref.at[pl.ds(src + o, size)], ystage.at[buf, pl.ds(off + o, size)], sem.at[buf]).start(
                    priority=size.bit_length() % 2))
            return off + n

        lax.fori_loop(0, N_EXPERTS, fetch_run, 0)

    buf = i % 2

    @pl.when(i == 0)
    def _():
        fetch(i, buf)

    @pl.when(i + 1 < N_ROW_BLOCKS)
    def _():
        fetch(i + 1, 1 - buf)

    pltpu.make_async_copy(ys_ref.at[pl.ds(0, TM * TOP_K)], ystage.at[buf], sem.at[buf]).wait()
    base = i * TM

    def unplace(r, carry):
        for kk in range(TOP_K):
            ybuf[kk, r] = ystage[buf, lpos_ref[kk * T_ALL + base + r]]
        return carry

    lax.fori_loop(0, TM, unplace, 0, unroll=8)


def _combined(x_ref, rg_ref, ybuf):
    rg = rg_ref[...]
    y0 = ybuf[0].reshape(TM, D_MODEL)
    y1 = ybuf[1].reshape(TM, D_MODEL)
    return x_ref[...] + rg[:, 0:1] * y0 + rg[:, 1:2] * y1


_COMBINE_SCRATCH = [pltpu.VMEM((2, TM * TOP_K, ROW_TILE, LANES), F32), pltpu.VMEM((TOP_K, TM, ROW_TILE, LANES), F32),
                    pltpu.SemaphoreType.DMA((2,))]


def _combine_kernel(lpos_ref, run_ref, rdst_ref, x_ref, rg_ref, ys_ref, o_ref, ystage, ybuf, sem):
    _gather_rows(lpos_ref, run_ref, rdst_ref, ys_ref, ystage, ybuf, sem, pl.program_id(0))
    o_ref[...] = _combined(x_ref, rg_ref, ybuf)


def _combine(cplan, x_all, rg_all, ys):
    return pl.pallas_call(
        _combine_kernel,
        grid_spec=pltpu.PrefetchScalarGridSpec(
            num_scalar_prefetch=3,
            grid=(N_ROW_BLOCKS,),
            in_specs=[pl.BlockSpec((TM, D_MODEL), lambda i, a, b, c: (i, 0)),
                      pl.BlockSpec((TM, LANES), lambda i, a, b, c: (i, 0)),
                      pl.BlockSpec(memory_space=pl.ANY)],
            out_specs=pl.BlockSpec((TM, D_MODEL), lambda i, a, b, c: (i, 0)),
            scratch_shapes=_COMBINE_SCRATCH,
        ),
        out_shape=jax.ShapeDtypeStruct((T_ALL, D_MODEL), F32),
        compiler_params=_cparams(("arbitrary",)),
        name="moe_combine",
    )(*cplan, x_all, rg_all, ys)


def _final_kernel(lpos_ref, run_ref, rdst_ref, x_ref, rg_ref, ys_ref, nfin_ref, op_ref, os_ref, ystage, ybuf, sem):
    i = pl.program_id(0)
    _gather_rows(lpos_ref, run_ref, rdst_ref, ys_ref, ystage, ybuf, sem, i)
    y = _rms(_combined(x_ref, rg_ref, ybuf), nfin_ref[...])

    @pl.when(i < N_PROMPT_BLOCKS)
    def _():
        op_ref[...] = y

    @pl.when(i >= N_PROMPT_BLOCKS)
    def _():
        os_ref[...] = y


def _final(cplan, x_all, rg_all, ys, nfin):
    return pl.pallas_call(
        _final_kernel,
        grid_spec=pltpu.PrefetchScalarGridSpec(
            num_scalar_prefetch=3,
            grid=(N_ROW_BLOCKS,),
            in_specs=[pl.BlockSpec((TM, D_MODEL), lambda i, a, b, c: (i, 0)),
                      pl.BlockSpec((TM, LANES), lambda i, a, b, c: (i, 0)),
                      pl.BlockSpec(memory_space=pl.ANY),
                      pl.BlockSpec((1, D_MODEL), lambda i, a, b, c: (0, 0))],
            out_specs=[pl.BlockSpec((TM, D_MODEL), lambda i, a, b, c: (jnp.minimum(i, N_PROMPT_BLOCKS - 1), 0)),
                       pl.BlockSpec((TM, D_MODEL), lambda i, a, b, c: (0, 0))],
            scratch_shapes=_COMBINE_SCRATCH,
        ),
        out_shape=[jax.ShapeDtypeStruct((T_PROMPT, D_MODEL), F32), jax.ShapeDtypeStruct((T_SAMPLE, D_MODEL), F32)],
        compiler_params=_cparams(("arbitrary",)),
        name="moe_combine_final",
    )(*cplan, x_all, rg_all, ys, nfin)


def _moe(h_all, rt_all, cnt, tcnt, w_gate, w_up, w_down, layer):
    plan, dplan, cplan = _moe_metadata(rt_all, cnt, tcnt)
    xs = _dispatch(dplan, h_all)
    ys = _experts(*plan, xs, w_gate, w_up, w_down, layer)
    return cplan, ys


def _pool_project(d_groups, wp_ref, scale):
    outs = [_dot(d_groups[g].astype(BF16), wp_ref[g]) for g in range(len(POOL_SIZES))]
    return jnp.concatenate(outs, axis=1) * scale


def _mix1_prompt_kernel(x_ref, nm_ref, wp_ref, sc_ref, nf_ref, wr_ref, br_ref,
                        x3_ref, h_ref, ri_ref, rg_ref, tc_ref, pl_ref, cnt_ref, ext):
    i = pl.program_id(0)

    @pl.when(i == 0)
    def _():
        cnt_ref[...] = jnp.zeros_like(cnt_ref)

    x = x_ref[...]
    hp = _rms(x, nm_ref[...])

    @pl.when(i % STEPS_PER_BATCH == 0)
    def _():
        ext[0:POOL_MAX, :] = jnp.zeros((POOL_MAX, D_MODEL), F32)

    ext[POOL_MAX:, :] = hp
    pos = (i % STEPS_PER_BATCH) * TM + lax.broadcasted_iota(jnp.int32, (TM, 1), 0)
    d_groups = []
    for g, w in enumerate(POOL_SIZES):
        cols = slice(g * POOL_GROUP_DIM, (g + 1) * POOL_GROUP_DIM)
        acc = ext[:, cols]
        span = 1
        while span < w:
            acc = acc + pltpu.roll(acc, span, 0)
            span *= 2
        cnt = jnp.minimum(pos + 1, w).astype(F32)
        d_groups.append(acc[POOL_MAX:] / cnt - hp[:, cols])
    tail = hp[TM - POOL_MAX:, :]
    ext[0:POOL_MAX, :] = tail
    pl_ref[...] = tail

    x3 = x + _pool_project(d_groups, wp_ref, sc_ref[...])
    x3_ref[...] = x3
    h, ids, gates = _route(x3, nf_ref[...], wr_ref[...], br_ref[...])
    h_ref[...] = h.reshape(h_ref.shape)
    ri_ref[...] = _rank_pack(ids, cnt_ref, tc_ref)
    rg_ref[...] = gates


def _mix1_prompt(x_all, nm, wp, sc, nf, wr, br):
    row_spec = pl.BlockSpec((TM, D_MODEL), lambda i: (i, 0))
    row3_spec = pl.BlockSpec((TM, ROW_TILE, LANES), lambda i: (i, 0, 0))
    lane_spec = pl.BlockSpec((TM, LANES), lambda i: (i, 0))
    return pl.pallas_call(
        _prompt_steps(_mix1_prompt_kernel, 7),
        grid=(N_ROW_BLOCKS,),
        in_specs=[row_spec, _const_spec((1, D_MODEL)),
                  _const_spec((len(POOL_SIZES), POOL_GROUP_DIM, POOL_GROUP_DIM)), _const_spec((1, D_MODEL)),
                  _const_spec((1, D_MODEL)), _const_spec((D_MODEL, 2 * LANES)), _const_spec((1, LANES))],
        out_specs=[row_spec, row3_spec, pl.BlockSpec((8, TM), lambda i: (0, i)), lane_spec,
                   pl.BlockSpec((None, 1, LANES), lambda i: (i, 0, 0)),
                   pl.BlockSpec((None, POOL_MAX, D_MODEL),
                                lambda i: (jnp.minimum(i // STEPS_PER_BATCH, BATCH - 1), 0, 0)),
                   _const_spec((1, LANES))],
        out_shape=[jax.ShapeDtypeStruct((T_ALL, D_MODEL), F32), jax.ShapeDtypeStruct((T_ALL, ROW_TILE, LANES), BF16),
                   jax.ShapeDtypeStruct((8, T_ALL), jnp.int32), jax.ShapeDtypeStruct((T_ALL, LANES), F32),
                   jax.ShapeDtypeStruct((N_ROW_BLOCKS, 1, LANES), F32),
                   jax.ShapeDtypeStruct((BATCH, POOL_MAX, D_MODEL), F32), jax.ShapeDtypeStruct((1, LANES), F32)],
        scratch_shapes=[pltpu.VMEM((POOL_MAX + TM, D_MODEL), F32)],
        compiler_params=_cparams(("arbitrary",)),
        name="mix1_prompt",
    )(x_all, nm, wp, sc, nf, wr, br)


def _mix1_sample_kernel(x_ref, st_ref, nm_ref, wp_ref, sc_ref, nf_ref, wr_ref, br_ref, cnt_in,
                        x3_in, h_in, ri_in, rg_in, tc_in,
                        x3_ref, h_ref, ri_ref, rg_ref, tc_ref, hs_ref, cnt_ref):
    del x3_in, h_in, ri_in, rg_in, tc_in
    x = x_ref[...]
    hs = _rms(x, nm_ref[...])
    hs_ref[...] = hs
    n_ctx = POOL_MAX - 1
    d_groups = []
    for g, w in enumerate(POOL_SIZES):
        cols = slice(g * POOL_GROUP_DIM, (g + 1) * POOL_GROUP_DIM)
        parts = []
        for t in range(DEC_SEQ):
            acc = hs[t * DEC_BATCH:(t + 1) * DEC_BATCH, cols]
            for back in range(1, w):
                src = t - back
                if src >= 0:
                    acc = acc + hs[src * DEC_BATCH:(src + 1) * DEC_BATCH, cols]
                else:
                    acc = acc + st_ref[n_ctx + src, :, cols]
            parts.append(acc / float(w) - hs[t * DEC_BATCH:(t + 1) * DEC_BATCH, cols])
        d_groups.append(jnp.concatenate(parts, axis=0))
    x3 = x + _pool_project(d_groups, wp_ref, sc_ref[...])
    x3_ref[...] = x3
    h, ids, gates = _route(x3, nf_ref[...], wr_ref[...], br_ref[...])
    h_ref[...] = h.reshape(h_ref.shape)
    cnt_ref[...] = cnt_in[...]
    ri_ref[...] = _rank_pack(ids, cnt_ref, tc_ref)
    rg_ref[...] = gates


def _mix1_sample(x_all, state_t, nm, wp, sc, nf, wr, br, cnt, x3_all, h_all, ri_all, rg_all, tc_all):
    sample_rows = pl.BlockSpec((TM, D_MODEL), lambda g: (N_PROMPT_BLOCKS, 0))
    sample_rows3 = pl.BlockSpec((TM, ROW_TILE, LANES), lambda g: (N_PROMPT_BLOCKS, 0, 0))
    sample_lanes = pl.BlockSpec((TM, LANES), lambda g: (N_PROMPT_BLOCKS, 0))
    anyspec = pl.BlockSpec(memory_space=pl.ANY)
    n_in = 9
    return pl.pallas_call(
        _mix1_sample_kernel,
        grid=(1,),
        in_specs=[sample_rows, _const_spec((POOL_MAX - 1, DEC_BATCH, D_MODEL)), _const_spec((1, D_MODEL)),
                  _const_spec((len(POOL_SIZES), POOL_GROUP_DIM, POOL_GROUP_DIM)), _const_spec((1, D_MODEL)),
                  _const_spec((1, D_MODEL)), _const_spec((D_MODEL, 2 * LANES)), _const_spec((1, LANES)),
                  _const_spec((1, LANES)), anyspec, anyspec, anyspec, anyspec, anyspec],
        out_specs=[sample_rows, sample_rows3, pl.BlockSpec((8, TM), lambda g: (0, N_PROMPT_BLOCKS)), sample_lanes,
                   pl.BlockSpec((None, 1, LANES), lambda g: (N_PROMPT_BLOCKS, 0, 0)),
                   _const_spec((T_SAMPLE, D_MODEL)), _const_spec((1, LANES))],
        out_shape=[jax.ShapeDtypeStruct((T_ALL, D_MODEL), F32), jax.ShapeDtypeStruct((T_ALL, ROW_TILE, LANES), BF16),
                   jax.ShapeDtypeStruct((8, T_ALL), jnp.int32), jax.ShapeDtypeStruct((T_ALL, LANES), F32),
                   jax.ShapeDtypeStruct((N_ROW_BLOCKS, 1, LANES), F32),
                   jax.ShapeDtypeStruct((T_SAMPLE, D_MODEL), F32), jax.ShapeDtypeStruct((1, LANES), F32)],
        input_output_aliases={n_in: 0, n_in + 1: 1, n_in + 2: 2, n_in + 3: 3, n_in + 4: 4},
        compiler_params=_cparams(("arbitrary",)),
        name="mix1_sample",
    )(x_all, state_t, nm, wp, sc, nf, wr, br, cnt, x3_all, h_all, ri_all, rg_all, tc_all)


def _router_weights(wg, bg, we, be):
    w = jnp.concatenate([wg, jnp.transpose(we, (1, 0, 2)).reshape(D_MODEL, N_EXPERTS)], axis=1)
    b = jnp.concatenate([bg, be.reshape(N_EXPERTS)])
    pad = LANES - N_GROUPS - N_EXPERTS
    w = jnp.pad(w, ((0, 0), (0, pad)))
    w_hi = w.astype(BF16)
    w_lo = (w - w_hi.astype(F32)).astype(BF16)
    return jnp.concatenate([w_hi, w_lo], axis=1), jnp.pad(b, (0, pad)).reshape(1, LANES)


def _stack(tab):
    return jnp.stack([jnp.concatenate([tab[h] for h in heads], axis=0) for heads in STACK_HEADS])


def kernel(x_prompt, x_sample, cache_k_win, cache_v_win, state_pool, norm_mix, norm_ffn, norm_final, w_in,
           a_ln_g, a_ln_b, a_w_s, a_b_s, b_sinks, rel_bias_table, w_out, c_w_pool, c_scale,
           router_group_w, router_group_b, router_expert_w, router_expert_b, w_gate, w_up, w_down):
    xs_t = jnp.transpose(x_sample, (1, 0, 2)).reshape(T_SAMPLE, D_MODEL)
    xp2 = x_prompt.reshape(T_PROMPT, D_MODEL)
    win =w_in[0].astype(BF16)
    wout = w_out[0].astype(BF16)
    lng = a_ln_g[0].reshape(1, A_WIDTH)
    lnb = a_ln_b[0].reshape(1, A_WIDTH)
    bias_p, bias_sc, bias_sn, ws_tril = _prep(rel_bias_table, a_w_s[0])
    wsp = ws_tril.reshape(A_HEADS // 2, 2, CHUNK, CHUNK).transpose(0, 2, 1, 3).reshape(A_HEADS // 2, CHUNK, 2 * CHUNK)
    bs_full = jnp.repeat(a_b_s[0].T, A_HEAD_DIM, axis=1)
    bias_p = jnp.stack([_stack(bias_p[0]), _stack(bias_p[1])])
    bias_sc = _stack(bias_sc)
    bias_sn = _stack(bias_sn)
    sinks = b_sinks[0]
    sink_p = jnp.stack([jnp.repeat(sinks[jnp.array(hh)], WINDOW) for hh in STACK_HEADS])
    sink_s = jnp.stack([jnp.repeat(sinks[jnp.array(hh)], 32) for hh in STACK_HEADS])
    bias_p = bias_p.at[:, :, :, 0].set(jnp.broadcast_to(sink_p[None], (2, 2, 4 * WINDOW)))
    bias_sc = bias_sc.at[:, :, 0].set(sink_s)
    pairs = [(t, s) for t in range(DEC_SEQ) for s in range(t + 1)]
    wcoef = jnp.stack([jnp.repeat(a_w_s[0][:, t, s], A_HEAD_DIM) for t, s in pairs])
    wcoef = jnp.pad(wcoef, ((0, 16 - len(pairs)), (0, 0)))
    bcoef = jnp.pad(jnp.repeat(a_b_s[0][:, :DEC_SEQ].T, A_HEAD_DIM, axis=1), ((0, 8 - DEC_SEQ), (0, 0)))
    ck = cache_k_win
    cv = cache_v_win
    routers = [_router_weights(router_group_w[l], router_group_b[l], router_expert_w[l], router_expert_b[l])
               for l in range(2)]
    nm = [norm_mix[l].reshape(1, D_MODEL) for l in range(2)]
    nf = [norm_ffn[l].reshape(1, D_MODEL) for l in range(2)]

    x1_all, h_all, ri_all, rg_all, tc_all, k_last, v_last, va_last, cnt0 = _mix0_prompt(
        xp2, nm[0], win, lng, lnb, wsp, bs_full, bias_p, wout, nf[0], *routers[0])
    x1_all, h_all, ri_all, rg_all, tc_all, k_new, v_new, va_s, cnt0 = _mix0_sample(
        xs_t, nm[0], win, lng, lnb, wcoef, bcoef, ck, cv, bias_sc, bias_sn, wout, nf[0], *routers[0], cnt0,
        x1_all, h_all, ri_all, rg_all, tc_all)
    cplan0, ys0 = _moe(h_all, ri_all, cnt0, tc_all, w_gate, w_up, w_down, 0)
    x2_all = _combine(cplan0, x1_all, rg_all, ys0)

    wp = c_w_pool[0].astype(BF16)
    sc = c_scale[0].reshape(1, D_MODEL)
    x3_all, h2_all, ri2_all, rg2_all, tc2_all, pool_tail, cnt1 = _mix1_prompt(
        x2_all, nm[1], wp, sc, nf[1], *routers[1])
    state_t = jnp.transpose(state_pool[0], (1, 0, 2))
    x3_all, h2_all, ri2_all, rg2_all, tc2_all, hs1, cnt1 = _mix1_sample(
        x2_all, state_t, nm[1], wp, sc, nf[1], *routers[1], cnt1, x3_all, h2_all, ri2_all, rg2_all, tc2_all)
    cplan1, ys1 = _moe(h2_all, ri2_all, cnt1, tc2_all, w_gate, w_up, w_down, 1)
    y_p, y_s = _final(cplan1, x3_all, rg2_all, ys1, norm_final.reshape(1, D_MODEL))

    def from_tmajor(a, width):
        return jnp.transpose(a.reshape(DEC_SEQ, DEC_BATCH, width), (1, 0, 2))

    y_prompt = y_p.reshape(BATCH, SEQ, D_MODEL)
    y_sample = from_tmajor(y_s, D_MODEL)
    win_k_p = k_last.reshape(1, BATCH, WINDOW, B_KV_HEADS, B_HEAD_DIM)
    win_v_p = v_last.reshape(1, BATCH, WINDOW, B_KV_HEADS, B_HEAD_DIM)
    kn = from_tmajor(k_new, KV_WIDTH).reshape(DEC_BATCH, DEC_SEQ, B_KV_HEADS, B_HEAD_DIM)
    vn = from_tmajor(v_new, KV_WIDTH).reshape(DEC_BATCH, DEC_SEQ, B_KV_HEADS, B_HEAD_DIM)
    win_k_s = jnp.concatenate([cache_k_win[0][:, DEC_SEQ:], kn], axis=1)[None]
    win_v_s = jnp.concatenate([cache_v_win[0][:, DEC_SEQ:], vn], axis=1)[None]
    chunk_v_p = va_last.reshape(1, BATCH, CHUNK, A_HEADS, A_HEAD_DIM)
    chunk_v_s = from_tmajor(va_s, A_WIDTH).reshape(1, DEC_BATCH, DEC_SEQ, A_HEADS, A_HEAD_DIM)
    pool_p = pool_tail[:, 1:][None]
    pool_s = jnp.concatenate([state_pool[0][:, DEC_SEQ:], from_tmajor(hs1, D_MODEL)], axis=1)[None]
    return (y_prompt, y_sample, win_k_p, win_v_p, win_k_s, win_v_s, chunk_v_p, chunk_v_s, pool_p, pool_s)
```

```python
import functools
import math

import numpy as np
import jax
import jax.numpy as jnp
from jax import lax
from jax.experimental import pallas as pl
from jax.experimental.pallas import tpu as pltpu

F32 = jnp.float32
BF16 = jnp.bfloat16

D_MODEL = 1024
BATCH = 2
SEQ = 8192
DEC_BATCH = 128
DEC_SEQ = 4
A_WIDTH = 512
A_HEADS = 8
A_HEAD_DIM = 64
CHUNK = 128
B_HEADS = 8
B_KV_HEADS = 2
B_HEAD_DIM = 64
B_GROUP = 4
WINDOW = 128
N_BUCKETS = 32
MAX_DISTANCE = WINDOW
Q_WIDTH = 512
KV_WIDTH = 128
IN_WIDTH = 2 * A_WIDTH + Q_WIDTH + 2 * KV_WIDTH
ATTN_SCALE = B_HEAD_DIM ** -0.5
NEG_INF = -1e30
POOL_SIZES = (2, 4, 8, 16)
POOL_GROUP_DIM = 256
POOL_MAX = 16
N_GROUPS = 4
EXPERTS_PER_GROUP = 8
N_EXPERTS = 32
TOP_K = 2
D_EXPERT = 512
EPS = 1e-6

LANES = 128
ROW_TILE = D_MODEL // LANES
T_PROMPT = BATCH * SEQ
T_SAMPLE = DEC_BATCH * DEC_SEQ
T_ALL = T_PROMPT + T_SAMPLE
TM = 512
N_PROMPT_BLOCKS = T_PROMPT // TM
N_ROW_BLOCKS = T_ALL // TM
STEPS_PER_BATCH = SEQ // TM
SUB = TM // WINDOW
N_SLOTS = T_ALL * TOP_K
MOE_BLK = 512
N_MOE_BLOCKS = N_SLOTS // MOE_BLK + N_EXPERTS
N_SORT_ROWS = N_MOE_BLOCKS * MOE_BLK
SAMPLE_GROUP = 8
N_SAMPLE_GROUPS = DEC_BATCH // SAMPLE_GROUP
VMEM_LIMIT = 56 * 1024 * 1024

STACK_HEADS = ((0, 2, 5, 7), (1, 3, 4, 6))


def _t5_bucket_np(dist):
    n = np.maximum(dist, 0)
    max_exact = N_BUCKETS // 2
    nf = np.maximum(n, 1).astype(np.float32)
    large = max_exact + (np.log(nf / np.float32(max_exact)) / np.float32(math.log(MAX_DISTANCE / max_exact))
                         * np.float32(N_BUCKETS - max_exact)).astype(np.int32)
    large = np.minimum(large, N_BUCKETS - 1)
    return np.where(n < max_exact, n, large).astype(np.int32)


def _bucket_tables():
    qi = np.arange(WINDOW)[:, None]
    ki = np.arange(2 * WINDOW)[None, :]
    dist = qi + WINDOW - ki
    valid = (dist >= 0) & (dist < WINDOW)
    bp = np.where(valid, _t5_bucket_np(dist), -1)
    bp_first = np.where(ki >= WINDOW, bp, -1)
    bkt_p = np.stack([bp_first, bp]).astype(np.int32)

    t = np.repeat(np.arange(DEC_SEQ), SAMPLE_GROUP)[:, None]
    b = np.tile(np.arange(SAMPLE_GROUP), DEC_SEQ)[:, None]
    cb = np.repeat(np.arange(SAMPLE_GROUP), WINDOW)[None, :]
    cj = np.tile(np.arange(WINDOW), SAMPLE_GROUP)[None, :]
    dist_c = t + WINDOW - cj
    valid_c = (cb == b) & (dist_c >= 0) & (dist_c < WINDOW)
    bkt_sc = np.where(valid_c, _t5_bucket_np(dist_c), -1).astype(np.int32)
    nt = np.repeat(np.arange(DEC_SEQ), SAMPLE_GROUP)[None, :]
    nb = np.tile(np.arange(SAMPLE_GROUP), DEC_SEQ)[None, :]
    dist_n = t - nt
    valid_n = (nb == b) & (dist_n >= 0)
    bkt_sn = np.where(valid_n, _t5_bucket_np(dist_n), -1).astype(np.int32)
    bkt_sn = np.concatenate([bkt_sn, np.full((32, LANES - 32), -1, np.int32)], axis=1)
    return bkt_p, bkt_sc, bkt_sn


_BKT_P, _BKT_SC, _BKT_SN = _bucket_tables()


def _cparams(semantics):
    return pltpu.CompilerParams(dimension_semantics=semantics, vmem_limit_bytes=VMEM_LIMIT)


def _rms(x, g):
    return x * lax.rsqrt(jnp.mean(x * x, axis=-1, keepdims=True) + EPS) * g


def _layernorm(x, g, b):
    xc = x - jnp.mean(x, axis=-1, keepdims=True)
    return xc * lax.rsqrt(jnp.mean(xc * xc, axis=-1, keepdims=True) + EPS) * g + b


def _dot(a, b):
    return jnp.dot(a, b, preferred_element_type=F32)


def _dot_nt(a, b):
    return lax.dot_general(a, b, (((1,), (1,)), ((), ())), preferred_element_type=F32)


def _project(x, nm, win, lng, lnb):
    h = _rms(x, nm)
    z = _dot(h.astype(BF16), win)
    u = jax.nn.gelu(z[:, :A_WIDTH])
    va = _layernorm(jax.nn.gelu(z[:, A_WIDTH:2 * A_WIDTH]), lng, lnb)
    q = z[:, 2 * A_WIDTH:2 * A_WIDTH + Q_WIDTH] * ATTN_SCALE
    k = z[:, 2 * A_WIDTH + Q_WIDTH:2 * A_WIDTH + Q_WIDTH + KV_WIDTH]
    v = z[:, 2 * A_WIDTH + Q_WIDTH + KV_WIDTH:]
    return u, va, q, k, v


def _route(x1, nf, wr, br):
    hf = _rms(x1, nf)
    h = hf.astype(BF16)
    h_lo = (hf - h.astype(F32)).astype(BF16)
    part = _dot(h, wr)
    logits = part[:, :LANES] + part[:, LANES:] + _dot(h_lo, wr[:, :LANES]) + br
    rows = logits.shape[0]
    lane = lax.broadcasted_iota(jnp.int32, (rows, LANES), 1)
    lanef = lane.astype(F32)
    big = jnp.float32(1e9)
    is_g = lane < N_GROUPS
    gl = jnp.where(is_g, logits, -jnp.inf)
    gmax = jnp.max(gl, axis=1, keepdims=True)
    gsel = jnp.min(jnp.where(gl == gmax, lanef, big), axis=1, keepdims=True)
    gsum = jnp.sum(jnp.where(is_g, jnp.exp(logits - gmax), 0.0), axis=1, keepdims=True)
    g1 = 1.0 / gsum
    lo = N_GROUPS + EXPERTS_PER_GROUP * gsel
    emask = (lanef >= lo) & (lanef < lo + EXPERTS_PER_GROUP)
    el = jnp.where(emask, logits, -jnp.inf)
    v1 = jnp.max(el, axis=1, keepdims=True)
    i1 = jnp.min(jnp.where(el == v1, lanef, big), axis=1, keepdims=True)
    el2 = jnp.where(lanef == i1, -jnp.inf, el)
    v2 = jnp.max(el2, axis=1, keepdims=True)
    i2 = jnp.min(jnp.where(el2 == v2, lanef, big), axis=1, keepdims=True)
    e2 = jnp.exp(v2 - v1)
    den = 1.0 + e2
    w1 = g1 / den
    w2 = g1 * e2 / den
    ids = jnp.where(lane == 0, i1 - N_GROUPS, jnp.where(lane == 1, i2 - N_GROUPS, 0.0)).astype(jnp.int32)
    gates = jnp.where(lane == 0, w1, jnp.where(lane == 1, w2, 0.0))
    return h, ids, gates


def _rank_pack(ids, cnt_ref, tcnt_ref):
    rows = ids.shape[0]
    lane = lax.broadcasted_iota(jnp.int32, (rows, LANES), 1)
    o0 = (lane == ids[:, 0:1]).astype(F32)
    o1 = (lane == ids[:, 1:2]).astype(F32)
    r = lax.broadcasted_iota(jnp.int32, (rows, rows), 0)
    c = lax.broadcasted_iota(jnp.int32, (rows, rows), 1)
    before = (c < r).astype(BF16)
    p01 = _dot(before, jnp.concatenate([o0, o1], axis=1).astype(BF16))
    p0 = p01[:, :LANES]
    p1 = p01[:, LANES:]
    c0 = jnp.sum(o0, axis=0, keepdims=True)
    c1 = jnp.sum(o1, axis=0, keepdims=True)
    ctile = c0 + c1
    cnt_ref[...] = cnt_ref[...] + ctile
    tcnt_ref[...] = ctile
    inc = jnp.broadcast_to(ctile, (8, LANES))
    lane8 = lax.broadcasted_iota(jnp.int32, (8, LANES), 1)
    for sh in (1, 2, 4, 8, 16, 32, 64):
        inc = inc + jnp.where(lane8 >= sh, pltpu.roll(inc, sh, 1), 0.0)
    start = inc[0:1] - ctile
    lpos0 = jnp.sum(o0 * (start + p0), axis=1, keepdims=True)
    lpos1 = jnp.sum(o1 * (start + c0 + p1), axis=1, keepdims=True)
    idf = ids.astype(F32)
    packed = jnp.where(lane < TOP_K, idf, 0.0)
    for ln, col in ((4, lpos0), (5, lpos1)):
        packed = jnp.where(lane == ln, col, packed)
    return jnp.transpose(packed)[:8].astype(jnp.int32)


def _prep_kernel(tab_ref, bp_ref, bsc_ref, bsn_ref, ws_ref, op_ref, osc_ref, osn_ref, ows_ref):
    def fill(bkt, write):
        for h in range(B_HEADS):
            acc = jnp.full(bkt.shape, NEG_INF, F32)
            for b in range(N_BUCKETS):
                acc = jnp.where(bkt == b, tab_ref[b, h], acc)
            write(h, acc)

    for var in range(2):
        def wr_p(h, acc, var=var):
            op_ref[var, h] = acc
        fill(bp_ref[var], wr_p)

    def wr_sc(h, acc):
        osc_ref[h] = acc
    fill(bsc_ref[...], wr_sc)

    def wr_sn(h, acc):
        osn_ref[h] = acc
    fill(bsn_ref[...], wr_sn)

    r = lax.broadcasted_iota(jnp.int32, (CHUNK, CHUNK), 0)
    c = lax.broadcasted_iota(jnp.int32, (CHUNK, CHUNK), 1)
    for h in range(A_HEADS):
        ows_ref[h] = jnp.where(r >= c, ws_ref[h], 0.0).astype(BF16)


def _prep(rel_bias_table, w_s):
    vm = pl.BlockSpec(memory_space=pltpu.VMEM)
    return pl.pallas_call(
        _prep_kernel,
        in_specs=[pl.BlockSpec(memory_space=pltpu.SMEM), vm, vm, vm, vm],
        out_specs=[vm, vm, vm, vm],
        out_shape=[
            jax.ShapeDtypeStruct((2, B_HEADS, WINDOW, 2 * WINDOW), F32),
            jax.ShapeDtypeStruct((B_HEADS, 32, SAMPLE_GROUP * WINDOW), F32),
            jax.ShapeDtypeStruct((B_HEADS, 32, LANES), F32),
            jax.ShapeDtypeStruct((A_HEADS, CHUNK, CHUNK), BF16),
        ],
        name="prep_tables",
    )(rel_bias_table, jnp.asarray(_BKT_P), jnp.asarray(_BKT_SC), jnp.asarray(_BKT_SN), w_s)


def _gate_pairs(va_rows, wsp_ref, lane_lo):
    outs = []
    for p in range(A_HEADS // 2):
        vp = va_rows[:, p * LANES:(p + 1) * LANES]
        rhs = jnp.concatenate([jnp.where(lane_lo, vp, 0.0), jnp.where(lane_lo, 0.0, vp)], axis=0).astype(BF16)
        outs.append(_dot(wsp_ref[p], rhs))
    return jnp.concatenate(outs, axis=1)


def _prompt_steps(body, first_row_out):
    def kern(*refs):
        i = pl.program_id(0)

        @pl.when(i < N_PROMPT_BLOCKS)
        def _():
            body(*refs)

        @pl.when(i >= N_PROMPT_BLOCKS)
        def _():
            for r in refs[first_row_out:first_row_out + 5]:
                r[...] = jnp.zeros(r.shape, r.dtype)

    return kern


def _mix0_prompt_kernel(x_ref, nm_ref, win_ref, lng_ref, lnb_ref, wsp_ref, bs_ref, bias_ref,
                        wout_ref, nf_ref, wr_ref, br_ref,
                        x1_ref, h_ref, ri_ref, rg_ref, tc_ref, kl_ref, vl_ref, val_ref, cnt_ref,
                        kprev, vprev, mix_scr):
    @pl.when(pl.program_id(0) == 0)
    def _():
        cnt_ref[...] = jnp.zeros_like(cnt_ref)

    x = x_ref[...]
    u, va, q, k, v = _project(x, nm_ref[...], win_ref[...], lng_ref[...], lnb_ref[...])
    lane_lo = lax.broadcasted_iota(jnp.int32, (WINDOW, LANES), 1) < B_HEAD_DIM
    row0 = lax.broadcasted_iota(jnp.int32, (WINDOW, KV_WIDTH), 0) == 0
    first = pl.program_id(0) % STEPS_PER_BATCH == 0

    @pl.when(first)
    def _():
        kprev[...] = jnp.zeros_like(kprev)
        vprev[...] = jnp.zeros_like(vprev)

    for j in range(SUB):
        rows = slice(j * WINDOW, (j + 1) * WINDOW)
        s_gate = _gate_pairs(va[rows], wsp_ref, lane_lo)
        mix_scr[rows, :A_WIDTH] = u[rows] * (s_gate + bs_ref[...])

        if j == 0:
            kp, vp = kprev[...], vprev[...]
        else:
            prows = slice((j - 1) * WINDOW, j * WINDOW)
            kp, vp = k[prows], v[prows]
        kk = jnp.concatenate([jnp.where(row0, 0.0, kp), k[rows]], axis=0)
        vv = jnp.concatenate([jnp.where(row0, 0.0, vp), v[rows]], axis=0)
        kops = (kk.astype(BF16), pltpu.roll(kk, B_HEAD_DIM, 1).astype(BF16))
        vops = (vv.astype(BF16), pltpu.roll(vv, B_HEAD_DIM, 1).astype(BF16))
        qt = [q[rows, p * LANES:(p + 1) * LANES] for p in range(4)]
        q_even = [jnp.where(lane_lo, t, 0.0) for t in qt]
        q_odd = [jnp.where(lane_lo, 0.0, t) for t in qt]
        stacks = (jnp.concatenate([q_even[0], q_even[1], q_odd[2], q_odd[3]], axis=0),
                  jnp.concatenate([q_odd[0], q_odd[1], q_even[2], q_even[3]], axis=0))
        o = []
        for st in range(2):
            s = _dot_nt(stacks[st].astype(BF16), kops[st])
            if j == 0:
                bias = bias_ref[jnp.where(first, 0, 1), st]
            else:
                bias = bias_ref[1, st]
            s = s + bias
            m = jnp.max(s, axis=-1, keepdims=True)
            p = jnp.exp(s - m)
            den = jnp.sum(p, axis=-1, keepdims=True)
            o.append(_dot(p.astype(BF16), vops[st]) / den)
        oa, ob = o
        sl = [slice(i * WINDOW, (i + 1) * WINDOW) for i in range(4)]
        tiles = (jnp.where(lane_lo, oa[sl[0]], ob[sl[0]]), jnp.where(lane_lo, oa[sl[1]], ob[sl[1]]),
                 jnp.where(lane_lo, ob[sl[2]], oa[sl[2]]), jnp.where(lane_lo, ob[sl[3]], oa[sl[3]]))
        for p in range(4):
            mix_scr[rows, A_WIDTH + p * LANES:A_WIDTH + (p + 1) * LANES] = tiles[p]

    last = slice(TM - WINDOW, TM)
    kprev[...] = k[last]
    vprev[...] = v[last]
    kl_ref[...] = k[last]
    vl_ref[...] = v[last]
    val_ref[...] = va[last]

    x1 = x + _dot(mix_scr[...].astype(BF16), wout_ref[...])
    x1_ref[...] = x1
    h, ids, gates = _route(x1, nf_ref[...], wr_ref[...], br_ref[...])
    h_ref[...] = h.reshape(h_ref.shape)
    ri_ref[...] = _rank_pack(ids, cnt_ref, tc_ref)
    rg_ref[...] = gates


def _const_spec(shape):
    nd = len(shape)
    return pl.BlockSpec(shape, lambda i, _n=nd: (0,) * _n)


def _mix0_prompt(x_all, nm, win, lng, lnb, wsp, bs_full, bias_p, wout, nf, wr, br):
    row_spec = pl.BlockSpec((TM, D_MODEL), lambda i: (i, 0))
    row3_spec = pl.BlockSpec((TM, ROW_TILE, LANES), lambda i: (i, 0, 0))
    lane_spec = pl.BlockSpec((TM, LANES), lambda i: (i, 0))
    last_kv = pl.BlockSpec((None, WINDOW, KV_WIDTH), lambda i: (jnp.minimum(i // STEPS_PER_BATCH, BATCH - 1), 0, 0))
    last_va = pl.BlockSpec((None, WINDOW, A_WIDTH), lambda i: (jnp.minimum(i // STEPS_PER_BATCH, BATCH - 1), 0, 0))
    return pl.pallas_call(
        _prompt_steps(_mix0_prompt_kernel, 12),
        grid=(N_ROW_BLOCKS,),
        in_specs=[pl.BlockSpec((TM, D_MODEL), lambda i: (jnp.minimum(i, N_PROMPT_BLOCKS - 1), 0)),
                  _const_spec((1, D_MODEL)), _const_spec((D_MODEL, IN_WIDTH)),
                  _const_spec((1, A_WIDTH)), _const_spec((1, A_WIDTH)),
                  _const_spec((A_HEADS // 2, CHUNK, 2 * CHUNK)), _const_spec((CHUNK, A_WIDTH)),
                  _const_spec((2, 2, 4 * WINDOW, 2 * WINDOW)),
                  _const_spec((A_WIDTH + Q_WIDTH, D_MODEL)), _const_spec((1, D_MODEL)),
                  _const_spec((D_MODEL, 2 * LANES)), _const_spec((1, LANES))],
        out_specs=[row_spec, row3_spec, pl.BlockSpec((8, TM), lambda i: (0, i)), lane_spec,
                   pl.BlockSpec((None, 1, LANES), lambda i: (i, 0, 0)),
                   last_kv, last_kv, last_va, _const_spec((1, LANES))],
        out_shape=[jax.ShapeDtypeStruct((T_ALL, D_MODEL), F32), jax.ShapeDtypeStruct((T_ALL, ROW_TILE, LANES), BF16),
                   jax.ShapeDtypeStruct((8, T_ALL), jnp.int32), jax.ShapeDtypeStruct((T_ALL, LANES), F32),
                   jax.ShapeDtypeStruct((N_ROW_BLOCKS, 1, LANES), F32),
                   jax.ShapeDtypeStruct((BATCH, WINDOW, KV_WIDTH), F32),
                   jax.ShapeDtypeStruct((BATCH, WINDOW, KV_WIDTH), F32),
                   jax.ShapeDtypeStruct((BATCH, WINDOW, A_WIDTH), F32),
                   jax.ShapeDtypeStruct((1, LANES), F32)],
        scratch_shapes=[pltpu.VMEM((WINDOW, KV_WIDTH), F32), pltpu.VMEM((WINDOW, KV_WIDTH), F32),
                        pltpu.VMEM((TM, D_MODEL), F32)],
        compiler_params=_cparams(("arbitrary",)),
        name="mix0_prompt",
    )(x_all, nm, win, lng, lnb, wsp, bs_full, bias_p, wout, nf, wr, br)


def _mix0_sample_kernel(x_ref, nm_ref, win_ref, lng_ref, lnb_ref, wcoef_ref, bcoef_ref,
                        ck_ref, cv_ref, bsc_ref, bsn_ref,
                        wout_ref, nf_ref, wr_ref, br_ref, cnt_in,
                        x1_in, h_in, ri_in, rg_in, tc_in,
                        x1_ref, h_ref, ri_ref, rg_ref, tc_ref, kn_ref, vn_ref, va_ref, cnt_ref,
                        q_scr, k_scr, v_scr, mix_scr):
    del x1_in, h_in, ri_in, rg_in, tc_in
    g = pl.program_id(0)

    @pl.when(g == 0)
    def _():
        u, va, q, k, v = _project(x_ref[...], nm_ref[...], win_ref[...], lng_ref[...], lnb_ref[...])
        q_scr[...] = q
        k_scr[...] = k
        v_scr[...] = v
        kn_ref[...] = k
        vn_ref[...] = v
        va_ref[...] = va
        idx = 0
        for t in range(DEC_SEQ):
            acc = jnp.zeros((DEC_BATCH, A_WIDTH), F32) + bcoef_ref[t:t + 1, :]
            for s in range(t + 1):
                acc = acc + wcoef_ref[idx:idx + 1, :] * va[s * DEC_BATCH:(s + 1) * DEC_BATCH]
                idx += 1
            mix_scr[t * DEC_BATCH:(t + 1) * DEC_BATCH, :A_WIDTH] = u[t * DEC_BATCH:(t + 1) * DEC_BATCH] * acc

    b0 = pl.multiple_of(g * SAMPLE_GROUP, SAMPLE_GROUP)
    lane_lo = lax.broadcasted_iota(jnp.int32, (DEC_SEQ * SAMPLE_GROUP, LANES), 1) < B_HEAD_DIM

    def grab(ref, width):
        return jnp.concatenate([ref[pl.ds(t * DEC_BATCH + b0, SAMPLE_GROUP), :] for t in range(DEC_SEQ)], axis=0)

    qg = grab(q_scr, Q_WIDTH)
    kn = grab(k_scr, KV_WIDTH)
    vn = grab(v_scr, KV_WIDTH)
    crow0 = lax.broadcasted_iota(jnp.int32, (SAMPLE_GROUP * WINDOW, KV_WIDTH), 0) == 0
    rows_kv = (SAMPLE_GROUP * WINDOW, KV_WIDTH)
    kc = jnp.where(crow0, 0.0, ck_ref[...].reshape(rows_kv))
    vc = jnp.where(crow0, 0.0, cv_ref[...].reshape(rows_kv))
    kc_ops = (kc.astype(BF16), pltpu.roll(kc, B_HEAD_DIM, 1).astype(BF16))
    vc_ops = (vc.astype(BF16), pltpu.roll(vc, B_HEAD_DIM, 1).astype(BF16))
    kn_ops = (kn.astype(BF16), pltpu.roll(kn, B_HEAD_DIM, 1).astype(BF16))
    vn_ops = (vn.astype(BF16), pltpu.roll(vn, B_HEAD_DIM, 1).astype(BF16))
    qt = [qg[:, p * LANES:(p + 1) * LANES] for p in range(4)]
    q_even = [jnp.where(lane_lo, t, 0.0) for t in qt]
    q_odd = [jnp.where(lane_lo, 0.0, t) for t in qt]
    stacks = (jnp.concatenate([q_even[0], q_even[1], q_odd[2], q_odd[3]], axis=0),
              jnp.concatenate([q_odd[0], q_odd[1], q_even[2], q_even[3]], axis=0))
    o = []
    for st in range(2):
        qs = stacks[st].astype(BF16)
        sc = _dot_nt(qs, kc_ops[st]) + bsc_ref[st]
        sn = _dot_nt(qs, kn_ops[st]) + bsn_ref[st][:, :DEC_SEQ * SAMPLE_GROUP]
        m = jnp.maximum(jnp.max(sc, axis=-1, keepdims=True), jnp.max(sn, axis=-1, keepdims=True))
        pc = jnp.exp(sc - m)
        pn = jnp.exp(sn - m)
        den = jnp.sum(pc, axis=-1, keepdims=True) + jnp.sum(pn, axis=-1, keepdims=True)
        o.append((_dot(pc.astype(BF16), vc_ops[st]) + _dot(pn.astype(BF16), vn_ops[st])) / den)
    oa, ob = o
    n = DEC_SEQ * SAMPLE_GROUP
    sl = [slice(i * n, (i + 1) * n) for i in range(4)]
    tiles = (jnp.where(lane_lo, oa[sl[0]], ob[sl[0]]), jnp.where(lane_lo, oa[sl[1]], ob[sl[1]]),
             jnp.where(lane_lo, ob[sl[2]], oa[sl[2]]), jnp.where(lane_lo, ob[sl[3]], oa[sl[3]]))
    for p in range(4):
        for t in range(DEC_SEQ):
            mix_scr[pl.ds(t * DEC_BATCH + b0, SAMPLE_GROUP), A_WIDTH + p * LANES:A_WIDTH + (p + 1) * LANES] = (
                tiles[p][t * SAMPLE_GROUP:(t + 1) * SAMPLE_GROUP])

    @pl.when(g == N_SAMPLE_GROUPS - 1)
    def _():
        x1 = x_ref[...] + _dot(mix_scr[...].astype(BF16), wout_ref[...])
        x1_ref[...] = x1
        h, ids, gates = _route(x1, nf_ref[...], wr_ref[...], br_ref[...])
        h_ref[...] = h.reshape(h_ref.shape)
        cnt_ref[...] = cnt_in[...]
        ri_ref[...] = _rank_pack(ids, cnt_ref, tc_ref)
        rg_ref[...] = gates


def _mix0_sample(x_all, nm, win, lng, lnb, wcoef, bcoef, ck, cv, bias_sc, bias_sn, wout, nf, wr, br, cnt,
                 x1_all, h_all, ri_all, rg_all, tc_all):
    sample_rows = pl.BlockSpec((TM, D_MODEL), lambda g: (N_PROMPT_BLOCKS, 0))
    sample_rows3 = pl.BlockSpec((TM, ROW_TILE, LANES), lambda g: (N_PROMPT_BLOCKS, 0, 0))
    sample_lanes = pl.BlockSpec((TM, LANES), lambda g: (N_PROMPT_BLOCKS, 0))
    cache_spec = pl.BlockSpec((None, SAMPLE_GROUP, WINDOW, B_KV_HEADS, B_HEAD_DIM), lambda g: (0, g, 0, 0, 0))
    anyspec = pl.BlockSpec(memory_space=pl.ANY)
    n_in = 16
    return pl.pallas_call(
        _mix0_sample_kernel,
        grid=(N_SAMPLE_GROUPS,),
        in_specs=[_const_spec((TM, D_MODEL)), _const_spec((1, D_MODEL)), _const_spec((D_MODEL, IN_WIDTH)),
                  _const_spec((1, A_WIDTH)), _const_spec((1, A_WIDTH)),
                  _const_spec((16, A_WIDTH)), _const_spec((8, A_WIDTH)),
                  cache_spec, cache_spec,
                  _const_spec((2, 4 * 32, SAMPLE_GROUP * WINDOW)), _const_spec((2, 4 * 32, LANES)),
                  _const_spec((A_WIDTH + Q_WIDTH, D_MODEL)), _const_spec((1, D_MODEL)),
                  _const_spec((D_MODEL, 2 * LANES)), _const_spec((1, LANES)), _const_spec((1, LANES)),
                  anyspec, anyspec, anyspec, anyspec, anyspec],
        out_specs=[sample_rows, sample_rows3, pl.BlockSpec((8, TM), lambda g: (0, N_PROMPT_BLOCKS)), sample_lanes,
                   pl.BlockSpec((None, 1, LANES), lambda g: (N_PROMPT_BLOCKS, 0, 0)),
                   _const_spec((T_SAMPLE, KV_WIDTH)), _const_spec((T_SAMPLE, KV_WIDTH)),
                   _const_spec((T_SAMPLE, A_WIDTH)), _const_spec((1, LANES))],
        out_shape=[jax.ShapeDtypeStruct((T_ALL, D_MODEL), F32), jax.ShapeDtypeStruct((T_ALL, ROW_TILE, LANES), BF16),
                   jax.ShapeDtypeStruct((8, T_ALL), jnp.int32), jax.ShapeDtypeStruct((T_ALL, LANES), F32),
                   jax.ShapeDtypeStruct((N_ROW_BLOCKS, 1, LANES), F32),
                   jax.ShapeDtypeStruct((T_SAMPLE, KV_WIDTH), F32), jax.ShapeDtypeStruct((T_SAMPLE, KV_WIDTH), F32),
                   jax.ShapeDtypeStruct((T_SAMPLE, A_WIDTH), F32), jax.ShapeDtypeStruct((1, LANES), F32)],
        scratch_shapes=[pltpu.VMEM((T_SAMPLE, Q_WIDTH), F32), pltpu.VMEM((T_SAMPLE, KV_WIDTH), F32),
                        pltpu.VMEM((T_SAMPLE, KV_WIDTH), F32), pltpu.VMEM((T_SAMPLE, D_MODEL), F32)],
        input_output_aliases={n_in: 0, n_in + 1: 1, n_in + 2: 2, n_in + 3: 3, n_in + 4: 4},
        compiler_params=_cparams(("arbitrary",)),
        name="mix0_sample",
    )(x_all, nm, win, lng, lnb, wcoef, bcoef, ck, cv, bias_sc, bias_sn, wout, nf, wr, br, cnt,
      x1_all, h_all, ri_all, rg_all, tc_all)


def _moe_metadata(rt_all, cnt, tcnt):
    counts = cnt[0, :N_EXPERTS].astype(jnp.int32)
    padded = (counts + MOE_BLK - 1) // MOE_BLK * MOE_BLK
    pad_end = jnp.cumsum(padded)
    pad_start = pad_end - padded
    experts = jnp.arange(N_EXPERTS, dtype=jnp.int32)
    n_valid = (pad_end[-1] // MOE_BLK).astype(jnp.int32).reshape(1)
    blk_start = jnp.arange(N_MOE_BLOCKS, dtype=jnp.int32) * MOE_BLK
    block_e = jnp.minimum(jnp.sum((blk_start[:, None] >= pad_end[None, :]).astype(jnp.int32), axis=1),
                          N_EXPERTS - 1).astype(jnp.int32)
    zero_start = (pad_start + counts).astype(jnp.int32)
    zero_len = (padded - counts).astype(jnp.int32)
    first = (blk_start == pad_start[block_e]).astype(jnp.int32)
    used = counts > 0
    parity = ((jnp.cumsum(used.astype(jnp.int32)) - 1) % 2)[block_e].astype(jnp.int32)
    nearest = lax.cummin(jnp.where(used, experts, N_EXPERTS)[::-1])[::-1]
    next_used = jnp.concatenate([nearest[1:], jnp.full((1,), N_EXPERTS, jnp.int32)])
    nxt = jnp.where(next_used < N_EXPERTS, next_used, -1)[block_e].astype(jnp.int32)
    plan = (block_e, first, parity, nxt, n_valid)
    runs = tcnt[:, 0, :N_EXPERTS].astype(jnp.int32)
    run_dst = pad_start[None, :] + jnp.cumsum(runs, axis=0) - runs
    lpos = rt_all[2 * TOP_K:3 * TOP_K].reshape(N_SLOTS).astype(jnp.int32)
    cplan = (lpos, runs.reshape(-1), run_dst.reshape(-1).astype(jnp.int32))
    dplan = cplan + (jnp.concatenate([zero_start, zero_len, n_valid]),)
    return plan, dplan, cplan


RUN_PIECE = 32


def _for_run_pieces(n, start_piece):
    whole = n // RUN_PIECE

    def body(j, carry):
        start_piece(j * RUN_PIECE, RUN_PIECE)
        return carry

    lax.fori_loop(0, whole, body, 0)
    o = whole * RUN_PIECE
    bit = RUN_PIECE // 2
    while bit >= 1:
        take = (n & bit) != 0

        @pl.when(take)
        def _(o=o, bit=bit):
            start_piece(o, bit)

        o = o + jnp.where(take, bit, 0)
        bit //= 2


def _dispatch_kernel(lpos_ref, run_ref, rdst_ref, zs_ref, h_ref, xs_ref, zero_scr, stage, sem, zsem):
    i = pl.program_id(0)

    @pl.when(i == 0)
    def _():
        zero_scr[...] = jnp.zeros_like(zero_scr)

        def pieces(e, do):
            off = zs_ref[e]
            rem = zs_ref[N_EXPERTS + e]
            bit = MOE_BLK // 2
            while bit >= 1:
                take = (rem & bit) != 0

                @pl.when(take)
                def _(off=off, bit=bit):
                    do(pltpu.make_async_copy(zero_scr.at[pl.ds(0, bit)], xs_ref.at[pl.ds(off, bit)], zsem))

                off = off + jnp.where(take, bit, 0)
                bit //= 2

        def start_e(e, c):
            pieces(e, lambda cp: cp.start())
            return c

        def wait_e(e, c):
            pieces(e, lambda cp: cp.wait())
            return c

        def tail(do):
            def step(b, c):
                do(pltpu.make_async_copy(zero_scr, xs_ref.at[pl.ds(b * MOE_BLK, MOE_BLK)], zsem))
                return c
            return step

        n_valid = zs_ref[2 * N_EXPERTS]
        lax.fori_loop(0, N_EXPERTS, start_e, 0)
        lax.fori_loop(n_valid, N_MOE_BLOCKS, tail(lambda cp: cp.start()), 0)
        lax.fori_loop(0, N_EXPERTS, wait_e, 0)
        lax.fori_loop(n_valid, N_MOE_BLOCKS, tail(lambda cp: cp.wait()), 0)

    base = i * TM

    def place(r, carry):
        row = h_ref[r]
        for kk in range(TOP_K):
            stage[lpos_ref[kk * T_ALL + base + r]] = row
        return carry

    lax.fori_loop(0, TM, place, 0, unroll=8)

    def send_run(e, off):
        n = run_ref[i * N_EXPERTS + e]
        dst = rdst_ref[i * N_EXPERTS + e]
        _for_run_pieces(n, lambda o, size: pltpu.make_async_copy(
            stage.at[pl.ds(off + o, size)], xs_ref.at[pl.ds(dst + o, size)], sem).start(
                priority=size.bit_length() % 2))
        return off + n

    lax.fori_loop(0, N_EXPERTS, send_run, 0)
    pltpu.make_async_copy(stage, xs_ref.at[pl.ds(0, TM * TOP_K)], sem).wait()


def _dispatch(dplan, h_all):
    return pl.pallas_call(
        _dispatch_kernel,
        grid_spec=pltpu.PrefetchScalarGridSpec(
            num_scalar_prefetch=4,
            grid=(N_ROW_BLOCKS,),
            in_specs=[pl.BlockSpec((TM, ROW_TILE, LANES), lambda i, lp, rn, rd, z: (i, 0, 0))],
            out_specs=pl.BlockSpec(memory_space=pl.ANY),
            scratch_shapes=[pltpu.VMEM((MOE_BLK, ROW_TILE, LANES), BF16),
                            pltpu.VMEM((TM * TOP_K, ROW_TILE, LANES), BF16),
                            pltpu.SemaphoreType.DMA(()), pltpu.SemaphoreType.DMA(())],
        ),
        out_shape=jax.ShapeDtypeStruct((N_SORT_ROWS, ROW_TILE, LANES), BF16),
        compiler_params=_cparams(("arbitrary",)),
        name="moe_dispatch",
    )(*dplan, h_all)


def _experts_kernel(layer, be_ref, first_ref, par_ref, nxt_ref, nv_ref,
                    x_ref, wg_hbm, wu_hbm, wd_hbm, y_ref,
                    wg_s, wu_s, wd_s, wg_f, wu_f, wd_f, wsem):
    i = pl.program_id(0)

    def fetch(e, slot):
        return (pltpu.make_async_copy(wg_hbm.at[layer, e], wg_f.at[slot], wsem.at[slot]),
                pltpu.make_async_copy(wu_hbm.at[layer, e], wu_f.at[slot], wsem.at[slot]),
                pltpu.make_async_copy(wd_hbm.at[layer, e], wd_f.at[slot], wsem.at[slot]))

    @pl.when(i < nv_ref[0])
    def _():
        e = be_ref[i]
        slot = par_ref[i]

        @pl.when(i == 0)
        def _():
            for cp in fetch(e, slot):
                cp.start()

        @pl.when(first_ref[i] == 1)
        def _():
            for cp in fetch(e, slot):
                cp.wait()
            wg_s[...] = wg_f[slot].astype(BF16)
            wu_s[...] = wu_f[slot].astype(BF16)
            wd_s[...] = wd_f[slot].astype(BF16)
            nxt = nxt_ref[i]

            @pl.when(nxt >= 0)
            def _():
                for cp in fetch(nxt, 1 - slot):
                    cp.start(priority=1)

        xb = x_ref[...].reshape(MOE_BLK, D_MODEL)
        a = jax.nn.silu(_dot(xb, wg_s[...])) * _dot(xb, wu_s[...])
        y_ref[...] = _dot(a.astype(BF16), wd_s[...]).reshape(y_ref.shape)

    @pl.when(i >= nv_ref[0])
    def _():
        y_ref[...] = jnp.zeros(y_ref.shape, y_ref.dtype)


def _experts(block_e, first, parity, nxt, n_valid, xs, w_gate, w_up, w_down, layer):
    def blk(i, be, fi, pa, nx, nv):
        return (jnp.maximum(jnp.minimum(i, nv[0] - 1), 0), 0, 0)

    anyspec = pl.BlockSpec(memory_space=pl.ANY)
    return pl.pallas_call(
        functools.partial(_experts_kernel, layer),
        grid_spec=pltpu.PrefetchScalarGridSpec(
            num_scalar_prefetch=5,
            grid=(N_MOE_BLOCKS,),
            in_specs=[pl.BlockSpec((MOE_BLK, ROW_TILE, LANES), blk), anyspec, anyspec, anyspec],
            out_specs=pl.BlockSpec((MOE_BLK, ROW_TILE, LANES), lambda i, be, fi, pa, nx, nv: (i, 0, 0)),
            scratch_shapes=[pltpu.VMEM((D_MODEL, D_EXPERT), BF16), pltpu.VMEM((D_MODEL, D_EXPERT), BF16),
                            pltpu.VMEM((D_EXPERT, D_MODEL), BF16),
                            pltpu.VMEM((2, D_MODEL, D_EXPERT), F32), pltpu.VMEM((2, D_MODEL, D_EXPERT), F32),
                            pltpu.VMEM((2, D_EXPERT, D_MODEL), F32), pltpu.SemaphoreType.DMA((2,))],
        ),
        out_shape=jax.ShapeDtypeStruct((N_SORT_ROWS, ROW_TILE, LANES), F32),
        compiler_params=_cparams(("arbitrary",)),
        name="moe_experts",
    )(block_e, first, parity, nxt, n_valid, xs, w_gate, w_up, w_down)


def _gather_rows(lpos_ref, run_ref, rdst_ref, ys_ref, ystage, ybuf, sem, i):
    def fetch(tile, buf):
        def fetch_run(e, off):
            n = run_ref[tile * N_EXPERTS + e]
            src = rdst_ref[tile * N_EXPERTS + e]
            _for_run_pieces(n, lambda o, size: pltpu.make_async_copy(
                ys_ref.at[pl.ds(src + o, size)], ystage.at[buf, pl.ds(off + o, size)], sem.at[buf]).start(
                    priority=size.bit_length() % 2))
            return off + n

        lax.fori_loop(0, N_EXPERTS, fetch_run, 0)

    buf = i % 2

    @pl.when(i == 0)
    def _():
        fetch(i, buf)

    @pl.when(i + 1 < N_ROW_BLOCKS)
    def _():
        fetch(i + 1, 1 - buf)

    pltpu.make_async_copy(ys_ref.at[pl.ds(0, TM * TOP_K)], ystage.at[buf], sem.at[buf]).wait()
    base = i * TM

    def unplace(r, carry):
        for kk in range(TOP_K):
            ybuf[kk, r] = ystage[buf, lpos_ref[kk * T_ALL + base + r]]
        return carry

    lax.fori_loop(0, TM, unplace, 0, unroll=8)


def _combined(x_ref, rg_ref, ybuf):
    rg = rg_ref[...]
    y0 = ybuf[0].reshape(TM, D_MODEL)
    y1 = ybuf[1].reshape(TM, D_MODEL)
    return x_ref[...] + rg[:, 0:1] * y0 + rg[:, 1:2] * y1


_COMBINE_SCRATCH = [pltpu.VMEM((2, TM * TOP_K, ROW_TILE, LANES), F32), pltpu.VMEM((TOP_K, TM, ROW_TILE, LANES), F32),
                    pltpu.SemaphoreType.DMA((2,))]


def _combine_kernel(lpos_ref, run_ref, rdst_ref, x_ref, rg_ref, ys_ref, o_ref, ystage, ybuf, sem):
    _gather_rows(lpos_ref, run_ref, rdst_ref, ys_ref, ystage, ybuf, sem, pl.program_id(0))
    o_ref[...] = _combined(x_ref, rg_ref, ybuf)


def _combine(cplan, x_all, rg_all, ys):
    return pl.pallas_call(
        _combine_kernel,
        grid_spec=pltpu.PrefetchScalarGridSpec(
            num_scalar_prefetch=3,
            grid=(N_ROW_BLOCKS,),
            in_specs=[pl.BlockSpec((TM, D_MODEL), lambda i, a, b, c: (i, 0)),
                      pl.BlockSpec((TM, LANES), lambda i, a, b, c: (i, 0)),
                      pl.BlockSpec(memory_space=pl.ANY)],
            out_specs=pl.BlockSpec((TM, D_MODEL), lambda i, a, b, c: (i, 0)),
            scratch_shapes=_COMBINE_SCRATCH,
        ),
        out_shape=jax.ShapeDtypeStruct((T_ALL, D_MODEL), F32),
        compiler_params=_cparams(("arbitrary",)),
        name="moe_combine",
    )(*cplan, x_all, rg_all, ys)


def _final_kernel(lpos_ref, run_ref, rdst_ref, x_ref, rg_ref, ys_ref, nfin_ref, op_ref, os_ref, ystage, ybuf, sem):
    i = pl.program_id(0)
    _gather_rows(lpos_ref, run_ref, rdst_ref, ys_ref, ystage, ybuf, sem, i)
    y = _rms(_combined(x_ref, rg_ref, ybuf), nfin_ref[...])

    @pl.when(i < N_PROMPT_BLOCKS)
    def _():
        op_ref[...] = y

    @pl.when(i >= N_PROMPT_BLOCKS)
    def _():
        os_ref[...] = y


def _final(cplan, x_all, rg_all, ys, nfin):
    return pl.pallas_call(
        _final_kernel,
        grid_spec=pltpu.PrefetchScalarGridSpec(
            num_scalar_prefetch=3,
            grid=(N_ROW_BLOCKS,),
            in_specs=[pl.BlockSpec((TM, D_MODEL), lambda i, a, b, c: (i, 0)),
                      pl.BlockSpec((TM, LANES), lambda i, a, b, c: (i, 0)),
                      pl.BlockSpec(memory_space=pl.ANY),
                      pl.BlockSpec((1, D_MODEL), lambda i, a, b, c: (0, 0))],
            out_specs=[pl.BlockSpec((TM, D_MODEL), lambda i, a, b, c: (jnp.minimum(i, N_PROMPT_BLOCKS - 1), 0)),
                       pl.BlockSpec((TM, D_MODEL), lambda i, a, b, c: (0, 0))],
            scratch_shapes=_COMBINE_SCRATCH,
        ),
        out_shape=[jax.ShapeDtypeStruct((T_PROMPT, D_MODEL), F32), jax.ShapeDtypeStruct((T_SAMPLE, D_MODEL), F32)],
        compiler_params=_cparams(("arbitrary",)),
        name="moe_combine_final",
    )(*cplan, x_all, rg_all, ys, nfin)


def _moe(h_all, rt_all, cnt, tcnt, w_gate, w_up, w_down, layer):
    plan, dplan, cplan = _moe_metadata(rt_all, cnt, tcnt)
    xs = _dispatch(dplan, h_all)
    ys = _experts(*plan, xs, w_gate, w_up, w_down, layer)
    return cplan, ys


def _pool_project(d_groups, wp_ref, scale):
    outs = [_dot(d_groups[g].astype(BF16), wp_ref[g]) for g in range(len(POOL_SIZES))]
    return jnp.concatenate(outs, axis=1) * scale


def _mix1_prompt_kernel(x_ref, nm_ref, wp_ref, sc_ref, nf_ref, wr_ref, br_ref,
                        x3_ref, h_ref, ri_ref, rg_ref, tc_ref, pl_ref, cnt_ref, ext):
    i = pl.program_id(0)

    @pl.when(i == 0)
    def _():
        cnt_ref[...] = jnp.zeros_like(cnt_ref)

    x = x_ref[...]
    hp = _rms(x, nm_ref[...])

    @pl.when(i % STEPS_PER_BATCH == 0)
    def _():
        ext[0:POOL_MAX, :] = jnp.zeros((POOL_MAX, D_MODEL), F32)

    ext[POOL_MAX:, :] = hp
    pos = (i % STEPS_PER_BATCH) * TM + lax.broadcasted_iota(jnp.int32, (TM, 1), 0)
    d_groups = []
    for g, w in enumerate(POOL_SIZES):
        cols = slice(g * POOL_GROUP_DIM, (g + 1) * POOL_GROUP_DIM)
        acc = ext[:, cols]
        span = 1
        while span < w:
            acc = acc + pltpu.roll(acc, span, 0)
            span *= 2
        cnt = jnp.minimum(pos + 1, w).astype(F32)
        d_groups.append(acc[POOL_MAX:] / cnt - hp[:, cols])
    tail = hp[TM - POOL_MAX:, :]
    ext[0:POOL_MAX, :] = tail
    pl_ref[...] = tail

    x3 = x + _pool_project(d_groups, wp_ref, sc_ref[...])
    x3_ref[...] = x3
    h, ids, gates = _route(x3, nf_ref[...], wr_ref[...], br_ref[...])
    h_ref[...] = h.reshape(h_ref.shape)
    ri_ref[...] = _rank_pack(ids, cnt_ref, tc_ref)
    rg_ref[...] = gates


def _mix1_prompt(x_all, nm, wp, sc, nf, wr, br):
    row_spec = pl.BlockSpec((TM, D_MODEL), lambda i: (i, 0))
    row3_spec = pl.BlockSpec((TM, ROW_TILE, LANES), lambda i: (i, 0, 0))
    lane_spec = pl.BlockSpec((TM, LANES), lambda i: (i, 0))
    return pl.pallas_call(
        _prompt_steps(_mix1_prompt_kernel, 7),
        grid=(N_ROW_BLOCKS,),
        in_specs=[row_spec, _const_spec((1, D_MODEL)),
                  _const_spec((len(POOL_SIZES), POOL_GROUP_DIM, POOL_GROUP_DIM)), _const_spec((1, D_MODEL)),
                  _const_spec((1, D_MODEL)), _const_spec((D_MODEL, 2 * LANES)), _const_spec((1, LANES))],
        out_specs=[row_spec, row3_spec, pl.BlockSpec((8, TM), lambda i: (0, i)), lane_spec,
                   pl.BlockSpec((None, 1, LANES), lambda i: (i, 0, 0)),
                   pl.BlockSpec((None, POOL_MAX, D_MODEL),
                                lambda i: (jnp.minimum(i // STEPS_PER_BATCH, BATCH - 1), 0, 0)),
                   _const_spec((1, LANES))],
        out_shape=[jax.ShapeDtypeStruct((T_ALL, D_MODEL), F32), jax.ShapeDtypeStruct((T_ALL, ROW_TILE, LANES), BF16),
                   jax.ShapeDtypeStruct((8, T_ALL), jnp.int32), jax.ShapeDtypeStruct((T_ALL, LANES), F32),
                   jax.ShapeDtypeStruct((N_ROW_BLOCKS, 1, LANES), F32),
                   jax.ShapeDtypeStruct((BATCH, POOL_MAX, D_MODEL), F32), jax.ShapeDtypeStruct((1, LANES), F32)],
        scratch_shapes=[pltpu.VMEM((POOL_MAX + TM, D_MODEL), F32)],
        compiler_params=_cparams(("arbitrary",)),
        name="mix1_prompt",
    )(x_all, nm, wp, sc, nf, wr, br)


def _mix1_sample_kernel(x_ref, st_ref, nm_ref, wp_ref, sc_ref, nf_ref, wr_ref, br_ref, cnt_in,
                        x3_in, h_in, ri_in, rg_in, tc_in,
                        x3_ref, h_ref, ri_ref, rg_ref, tc_ref, hs_ref, cnt_ref):
    del x3_in, h_in, ri_in, rg_in, tc_in
    x = x_ref[...]
    hs = _rms(x, nm_ref[...])
    hs_ref[...] = hs
    n_ctx = POOL_MAX - 1
    d_groups = []
    for g, w in enumerate(POOL_SIZES):
        cols = slice(g * POOL_GROUP_DIM, (g + 1) * POOL_GROUP_DIM)
        parts = []
        for t in range(DEC_SEQ):
            acc = hs[t * DEC_BATCH:(t + 1) * DEC_BATCH, cols]
            for back in range(1, w):
                src = t - back
                if src >= 0:
                    acc = acc + hs[src * DEC_BATCH:(src + 1) * DEC_BATCH, cols]
                else:
                    acc = acc + st_ref[n_ctx + src, :, cols]
            parts.append(acc / float(w) - hs[t * DEC_BATCH:(t + 1) * DEC_BATCH, cols])
        d_groups.append(jnp.concatenate(parts, axis=0))
    x3 = x + _pool_project(d_groups, wp_ref, sc_ref[...])
    x3_ref[...] = x3
    h, ids, gates = _route(x3, nf_ref[...], wr_ref[...], br_ref[...])
    h_ref[...] = h.reshape(h_ref.shape)
    cnt_ref[...] = cnt_in[...]
    ri_ref[...] = _rank_pack(ids, cnt_ref, tc_ref)
    rg_ref[...] = gates


def _mix1_sample(x_all, state_t, nm, wp, sc, nf, wr, br, cnt, x3_all, h_all, ri_all, rg_all, tc_all):
    sample_rows = pl.BlockSpec((TM, D_MODEL), lambda g: (N_PROMPT_BLOCKS, 0))
    sample_rows3 = pl.BlockSpec((TM, ROW_TILE, LANES), lambda g: (N_PROMPT_BLOCKS, 0, 0))
    sample_lanes = pl.BlockSpec((TM, LANES), lambda g: (N_PROMPT_BLOCKS, 0))
    anyspec = pl.BlockSpec(memory_space=pl.ANY)
    n_in = 9
    return pl.pallas_call(
        _mix1_sample_kernel,
        grid=(1,),
        in_specs=[sample_rows, _const_spec((POOL_MAX - 1, DEC_BATCH, D_MODEL)), _const_spec((1, D_MODEL)),
                  _const_spec((len(POOL_SIZES), POOL_GROUP_DIM, POOL_GROUP_DIM)), _const_spec((1, D_MODEL)),
                  _const_spec((1, D_MODEL)), _const_spec((D_MODEL, 2 * LANES)), _const_spec((1, LANES)),
                  _const_spec((1, LANES)), anyspec, anyspec, anyspec, anyspec, anyspec],
        out_specs=[sample_rows, sample_rows3, pl.BlockSpec((8, TM), lambda g: (0, N_PROMPT_BLOCKS)), sample_lanes,
                   pl.BlockSpec((None, 1, LANES), lambda g: (N_PROMPT_BLOCKS, 0, 0)),
                   _const_spec((T_SAMPLE, D_MODEL)), _const_spec((1, LANES))],
        out_shape=[jax.ShapeDtypeStruct((T_ALL, D_MODEL), F32), jax.ShapeDtypeStruct((T_ALL, ROW_TILE, LANES), BF16),
                   jax.ShapeDtypeStruct((8, T_ALL), jnp.int32), jax.ShapeDtypeStruct((T_ALL, LANES), F32),
                   jax.ShapeDtypeStruct((N_ROW_BLOCKS, 1, LANES), F32),
                   jax.ShapeDtypeStruct((T_SAMPLE, D_MODEL), F32), jax.ShapeDtypeStruct((1, LANES), F32)],
        input_output_aliases={n_in: 0, n_in + 1: 1, n_in + 2: 2, n_in + 3: 3, n_in + 4: 4},
        compiler_params=_cparams(("arbitrary",)),
        name="mix1_sample",
    )(x_all, state_t, nm, wp, sc, nf, wr, br, cnt, x3_all, h_all, ri_all, rg_all, tc_all)


def _router_weights(wg, bg, we, be):
    w = jnp.concatenate([wg, jnp.transpose(we, (1, 0, 2)).reshape(D_MODEL, N_EXPERTS)], axis=1)
    b = jnp.concatenate([bg, be.reshape(N_EXPERTS)])
    pad = LANES - N_GROUPS - N_EXPERTS
    w = jnp.pad(w, ((0, 0), (0, pad)))
    w_hi = w.astype(BF16)
    w_lo = (w - w_hi.astype(F32)).astype(BF16)
    return jnp.concatenate([w_hi, w_lo], axis=1), jnp.pad(b, (0, pad)).reshape(1, LANES)


def _stack(tab):
    return jnp.stack([jnp.concatenate([tab[h] for h in heads], axis=0) for heads in STACK_HEADS])


def kernel(x_prompt, x_sample, cache_k_win, cache_v_win, state_pool, norm_mix, norm_ffn, norm_final, w_in,
           a_ln_g, a_ln_b, a_w_s, a_b_s, b_sinks, rel_bias_table, w_out, c_w_pool, c_scale,
           router_group_w, router_group_b, router_expert_w, router_expert_b, w_gate, w_up, w_down):
    xs_t = jnp.transpose(x_sample, (1, 0, 2)).reshape(T_SAMPLE, D_MODEL)
    xp2 = x_prompt.reshape(T_PROMPT, D_MODEL)
    win =w_in[0].astype(BF16)
    wout = w_out[0].astype(BF16)
    lng = a_ln_g[0].reshape(1, A_WIDTH)
    lnb = a_ln_b[0].reshape(1, A_WIDTH)
    bias_p, bias_sc, bias_sn, ws_tril = _prep(rel_bias_table, a_w_s[0])
    wsp = ws_tril.reshape(A_HEADS // 2, 2, CHUNK, CHUNK).transpose(0, 2, 1, 3).reshape(A_HEADS // 2, CHUNK, 2 * CHUNK)
    bs_full = jnp.repeat(a_b_s[0].T, A_HEAD_DIM, axis=1)
    bias_p = jnp.stack([_stack(bias_p[0]), _stack(bias_p[1])])
    bias_sc = _stack(bias_sc)
    bias_sn = _stack(bias_sn)
    sinks = b_sinks[0]
    sink_p = jnp.stack([jnp.repeat(sinks[jnp.array(hh)], WINDOW) for hh in STACK_HEADS])
    sink_s = jnp.stack([jnp.repeat(sinks[jnp.array(hh)], 32) for hh in STACK_HEADS])
    bias_p = bias_p.at[:, :, :, 0].set(jnp.broadcast_to(sink_p[None], (2, 2, 4 * WINDOW)))
    bias_sc = bias_sc.at[:, :, 0].set(sink_s)
    pairs = [(t, s) for t in range(DEC_SEQ) for s in range(t + 1)]
    wcoef = jnp.stack([jnp.repeat(a_w_s[0][:, t, s], A_HEAD_DIM) for t, s in pairs])
    wcoef = jnp.pad(wcoef, ((0, 16 - len(pairs)), (0, 0)))
    bcoef = jnp.pad(jnp.repeat(a_b_s[0][:, :DEC_SEQ].T, A_HEAD_DIM, axis=1), ((0, 8 - DEC_SEQ), (0, 0)))
    ck = cache_k_win
    cv = cache_v_win
    routers = [_router_weights(router_group_w[l], router_group_b[l], router_expert_w[l], router_expert_b[l])
               for l in range(2)]
    nm = [norm_mix[l].reshape(1, D_MODEL) for l in range(2)]
    nf = [norm_ffn[l].reshape(1, D_MODEL) for l in range(2)]

    x1_all, h_all, ri_all, rg_all, tc_all, k_last, v_last, va_last, cnt0 = _mix0_prompt(
        xp2, nm[0], win, lng, lnb, wsp, bs_full, bias_p, wout, nf[0], *routers[0])
    x1_all, h_all, ri_all, rg_all, tc_all, k_new, v_new, va_s, cnt0 = _mix0_sample(
        xs_t, nm[0], win, lng, lnb, wcoef, bcoef, ck, cv, bias_sc, bias_sn, wout, nf[0], *routers[0], cnt0,
        x1_all, h_all, ri_all, rg_all, tc_all)
    cplan0, ys0 = _moe(h_all, ri_all, cnt0, tc_all, w_gate, w_up, w_down, 0)
    x2_all = _combine(cplan0, x1_all, rg_all, ys0)

    wp = c_w_pool[0].astype(BF16)
    sc = c_scale[0].reshape(1, D_MODEL)
    x3_all, h2_all, ri2_all, rg2_all, tc2_all, pool_tail, cnt1 = _mix1_prompt(
        x2_all, nm[1], wp, sc, nf[1], *routers[1])
    state_t = jnp.transpose(state_pool[0], (1, 0, 2))
    x3_all, h2_all, ri2_all, rg2_all, tc2_all, hs1, cnt1 = _mix1_sample(
        x2_all, state_t, nm[1], wp, sc, nf[1], *routers[1], cnt1, x3_all, h2_all, ri2_all, rg2_all, tc2_all)
    cplan1, ys1 = _moe(h2_all, ri2_all, cnt1, tc2_all, w_gate, w_up, w_down, 1)
    y_p, y_s = _final(cplan1, x3_all, rg2_all, ys1, norm_final.reshape(1, D_MODEL))

    def from_tmajor(a, width):
        return jnp.transpose(a.reshape(DEC_SEQ, DEC_BATCH, width), (1, 0, 2))

    y_prompt = y_p.reshape(BATCH, SEQ, D_MODEL)
    y_sample = from_tmajor(y_s, D_MODEL)
    win_k_p = k_last.reshape(1, BATCH, WINDOW, B_KV_HEADS, B_HEAD_DIM)
    win_v_p = v_last.reshape(1, BATCH, WINDOW, B_KV_HEADS, B_HEAD_DIM)
    kn = from_tmajor(k_new, KV_WIDTH).reshape(DEC_BATCH, DEC_SEQ, B_KV_HEADS, B_HEAD_DIM)
    vn = from_tmajor(v_new, KV_WIDTH).reshape(DEC_BATCH, DEC_SEQ, B_KV_HEADS, B_HEAD_DIM)
    win_k_s = jnp.concatenate([cache_k_win[0][:, DEC_SEQ:], kn], axis=1)[None]
    win_v_s = jnp.concatenate([cache_v_win[0][:, DEC_SEQ:], vn], axis=1)[None]
    chunk_v_p = va_last.reshape(1, BATCH, CHUNK, A_HEADS, A_HEAD_DIM)
    chunk_v_s = from_tmajor(va_s, A_WIDTH).reshape(1, DEC_BATCH, DEC_SEQ, A_HEADS, A_HEAD_DIM)
    pool_p = pool_tail[:, 1:][None]
    pool_s = jnp.concatenate([state_pool[0][:, DEC_SEQ:], from_tmajor(hs1, D_MODEL)], axis=1)[None]
    return (y_prompt, y_sample, win_k_p, win_v_p, win_k_s, win_v_s, chunk_v_p, chunk_v_s, pool_p, pool_s)
```

```python
import functools
import math

import numpy as np
import jax
import jax.numpy as jnp
from jax import lax
from jax.experimental import pallas as pl
from jax.experimental.pallas import tpu as pltpu

F32 = jnp.float32
BF16 = jnp.bfloat16

D_MODEL = 1024
BATCH = 2
SEQ = 8192
DEC_BATCH = 128
DEC_SEQ = 4
A_WIDTH = 512
A_HEADS = 8
A_HEAD_DIM = 64
CHUNK = 128
B_HEADS = 8
B_KV_HEADS = 2
B_HEAD_DIM = 64
B_GROUP = 4
WINDOW = 128
N_BUCKETS = 32
MAX_DISTANCE = WINDOW
Q_WIDTH = 512
KV_WIDTH = 128
IN_WIDTH = 2 * A_WIDTH + Q_WIDTH + 2 * KV_WIDTH
ATTN_SCALE = B_HEAD_DIM ** -0.5
NEG_INF = -1e30
POOL_SIZES = (2, 4, 8, 16)
POOL_GROUP_DIM = 256
POOL_MAX = 16
N_GROUPS = 4
EXPERTS_PER_GROUP = 8
N_EXPERTS = 32
TOP_K = 2
D_EXPERT = 512
EPS = 1e-6

LANES = 128
ROW_TILE = D_MODEL // LANES
T_PROMPT = BATCH * SEQ
T_SAMPLE = DEC_BATCH * DEC_SEQ
T_ALL = T_PROMPT + T_SAMPLE
TM = 512
N_PROMPT_BLOCKS = T_PROMPT // TM
N_ROW_BLOCKS = T_ALL // TM
STEPS_PER_BATCH = SEQ // TM
SUB = TM // WINDOW
N_SLOTS = T_ALL * TOP_K
MOE_BLK = 512
N_MOE_BLOCKS = N_SLOTS // MOE_BLK + N_EXPERTS
N_SORT_ROWS = N_MOE_BLOCKS * MOE_BLK
SAMPLE_GROUP = 8
N_SAMPLE_GROUPS = DEC_BATCH // SAMPLE_GROUP
VMEM_LIMIT = 56 * 1024 * 1024

STACK_HEADS = ((0, 2, 5, 7), (1, 3, 4, 6))


def _t5_bucket_np(dist):
    n = np.maximum(dist, 0)
    max_exact = N_BUCKETS // 2
    nf = np.maximum(n, 1).astype(np.float32)
    large = max_exact + (np.log(nf / np.float32(max_exact)) / np.float32(math.log(MAX_DISTANCE / max_exact))
                         * np.float32(N_BUCKETS - max_exact)).astype(np.int32)
    large = np.minimum(large, N_BUCKETS - 1)
    return np.where(n < max_exact, n, large).astype(np.int32)


def _bucket_tables():
    qi = np.arange(WINDOW)[:, None]
    ki = np.arange(2 * WINDOW)[None, :]
    dist = qi + WINDOW - ki
    valid = (dist >= 0) & (dist < WINDOW)
    bp = np.where(valid, _t5_bucket_np(dist), -1)
    bp_first = np.where(ki >= WINDOW, bp, -1)
    bkt_p = np.stack([bp_first, bp]).astype(np.int32)

    t = np.repeat(np.arange(DEC_SEQ), SAMPLE_GROUP)[:, None]
    b = np.tile(np.arange(SAMPLE_GROUP), DEC_SEQ)[:, None]
    cb = np.repeat(np.arange(SAMPLE_GROUP), WINDOW)[None, :]
    cj = np.tile(np.arange(WINDOW), SAMPLE_GROUP)[None, :]
    dist_c = t + WINDOW - cj
    valid_c = (cb == b) & (dist_c >= 0) & (dist_c < WINDOW)
    bkt_sc = np.where(valid_c, _t5_bucket_np(dist_c), -1).astype(np.int32)
    nt = np.repeat(np.arange(DEC_SEQ), SAMPLE_GROUP)[None, :]
    nb = np.tile(np.arange(SAMPLE_GROUP), DEC_SEQ)[None, :]
    dist_n = t - nt
    valid_n = (nb == b) & (dist_n >= 0)
    bkt_sn = np.where(valid_n, _t5_bucket_np(dist_n), -1).astype(np.int32)
    bkt_sn = np.concatenate([bkt_sn, np.full((32, LANES - 32), -1, np.int32)], axis=1)
    return bkt_p, bkt_sc, bkt_sn


_BKT_P, _BKT_SC, _BKT_SN = _bucket_tables()


def _cparams(semantics):
    return pltpu.CompilerParams(dimension_semantics=semantics, vmem_limit_bytes=VMEM_LIMIT)


def _rms(x, g):
    return x * lax.rsqrt(jnp.mean(x * x, axis=-1, keepdims=True) + EPS) * g


def _layernorm(x, g, b):
    xc = x - jnp.mean(x, axis=-1, keepdims=True)
    return xc * lax.rsqrt(jnp.mean(xc * xc, axis=-1, keepdims=True) + EPS) * g + b


def _dot(a, b):
    return jnp.dot(a, b, preferred_element_type=F32)


def _dot_nt(a, b):
    return lax.dot_general(a, b, (((1,), (1,)), ((), ())), preferred_element_type=F32)


def _project(x, nm, win, lng, lnb):
    h = _rms(x, nm)
    z = _dot(h.astype(BF16), win)
    u = jax.nn.gelu(z[:, :A_WIDTH])
    va = _layernorm(jax.nn.gelu(z[:, A_WIDTH:2 * A_WIDTH]), lng, lnb)
    q = z[:, 2 * A_WIDTH:2 * A_WIDTH + Q_WIDTH] * ATTN_SCALE
    k = z[:, 2 * A_WIDTH + Q_WIDTH:2 * A_WIDTH + Q_WIDTH + KV_WIDTH]
    v = z[:, 2 * A_WIDTH + Q_WIDTH + KV_WIDTH:]
    return u, va, q, k, v


def _route(x1, nf, wr, br):
    hf = _rms(x1, nf)
    h = hf.astype(BF16)
    h_lo = (hf - h.astype(F32)).astype(BF16)
    part = _dot(h, wr)
    logits = part[:, :LANES] + part[:, LANES:] + _dot(h_lo, wr[:, :LANES]) + br
    rows = logits.shape[0]
    lane = lax.broadcasted_iota(jnp.int32, (rows, LANES), 1)
    lanef = lane.astype(F32)
    big = jnp.float32(1e9)
    is_g = lane < N_GROUPS
    gl = jnp.where(is_g, logits, -jnp.inf)
    gmax = jnp.max(gl, axis=1, keepdims=True)
    gsel = jnp.min(jnp.where(gl == gmax, lanef, big), axis=1, keepdims=True)
    gsum = jnp.sum(jnp.where(is_g, jnp.exp(logits - gmax), 0.0), axis=1, keepdims=True)
    g1 = 1.0 / gsum
    lo = N_GROUPS + EXPERTS_PER_GROUP * gsel
    emask = (lanef >= lo) & (lanef < lo + EXPERTS_PER_GROUP)
    el = jnp.where(emask, logits, -jnp.inf)
    v1 = jnp.max(el, axis=1, keepdims=True)
    i1 = jnp.min(jnp.where(el == v1, lanef, big), axis=1, keepdims=True)
    el2 = jnp.where(lanef == i1, -jnp.inf, el)
    v2 = jnp.max(el2, axis=1, keepdims=True)
    i2 = jnp.min(jnp.where(el2 == v2, lanef, big), axis=1, keepdims=True)
    e2 = jnp.exp(v2 - v1)
    den = 1.0 + e2
    w1 = g1 / den
    w2 = g1 * e2 / den
    ids = jnp.where(lane == 0, i1 - N_GROUPS, jnp.where(lane == 1, i2 - N_GROUPS, 0.0)).astype(jnp.int32)
    gates = jnp.where(lane == 0, w1, jnp.where(lane == 1, w2, 0.0))
    return h, ids, gates


def _rank_pack(ids, cnt_ref, tcnt_ref):
    rows = ids.shape[0]
    lane = lax.broadcasted_iota(jnp.int32, (rows, LANES), 1)
    o0 = (lane == ids[:, 0:1]).astype(F32)
    o1 = (lane == ids[:, 1:2]).astype(F32)
    r = lax.broadcasted_iota(jnp.int32, (rows, rows), 0)
    c = lax.broadcasted_iota(jnp.int32, (rows, rows), 1)
    before = (c < r).astype(BF16)
    p01 = _dot(before, jnp.concatenate([o0, o1], axis=1).astype(BF16))
    p0 = p01[:, :LANES]
    p1 = p01[:, LANES:]
    c0 = jnp.sum(o0, axis=0, keepdims=True)
    c1 = jnp.sum(o1, axis=0, keepdims=True)
    ctile = c0 + c1
    cnt_ref[...] = cnt_ref[...] + ctile
    tcnt_ref[...] = ctile
    inc = jnp.broadcast_to(ctile, (8, LANES))
    lane8 = lax.broadcasted_iota(jnp.int32, (8, LANES), 1)
    for sh in (1, 2, 4, 8, 16, 32, 64):
        inc = inc + jnp.where(lane8 >= sh, pltpu.roll(inc, sh, 1), 0.0)
    start = inc[0:1] - ctile
    lpos0 = jnp.sum(o0 * (start + p0), axis=1, keepdims=True)
    lpos1 = jnp.sum(o1 * (start + c0 + p1), axis=1, keepdims=True)
    idf = ids.astype(F32)
    packed = jnp.where(lane < TOP_K, idf, 0.0)
    for ln, col in ((4, lpos0), (5, lpos1)):
        packed = jnp.where(lane == ln, col, packed)
    return jnp.transpose(packed)[:8].astype(jnp.int32)


def _prep_kernel(tab_ref, bp_ref, bsc_ref, bsn_ref, ws_ref, op_ref, osc_ref, osn_ref, ows_ref):
    def fill(bkt, write):
        for h in range(B_HEADS):
            acc = jnp.full(bkt.shape, NEG_INF, F32)
            for b in range(N_BUCKETS):
                acc = jnp.where(bkt == b, tab_ref[b, h], acc)
            write(h, acc)

    for var in range(2):
        def wr_p(h, acc, var=var):
            op_ref[var, h] = acc
        fill(bp_ref[var], wr_p)

    def wr_sc(h, acc):
        osc_ref[h] = acc
    fill(bsc_ref[...], wr_sc)

    def wr_sn(h, acc):
        osn_ref[h] = acc
    fill(bsn_ref[...], wr_sn)

    r = lax.broadcasted_iota(jnp.int32, (CHUNK, CHUNK), 0)
    c = lax.broadcasted_iota(jnp.int32, (CHUNK, CHUNK), 1)
    for h in range(A_HEADS):
        ows_ref[h] = jnp.where(r >= c, ws_ref[h], 0.0).astype(BF16)


def _prep(rel_bias_table, w_s):
    vm = pl.BlockSpec(memory_space=pltpu.VMEM)
    return pl.pallas_call(
        _prep_kernel,
        in_specs=[pl.BlockSpec(memory_space=pltpu.SMEM), vm, vm, vm, vm],
        out_specs=[vm, vm, vm, vm],
        out_shape=[
            jax.ShapeDtypeStruct((2, B_HEADS, WINDOW, 2 * WINDOW), F32),
            jax.ShapeDtypeStruct((B_HEADS, 32, SAMPLE_GROUP * WINDOW), F32),
            jax.ShapeDtypeStruct((B_HEADS, 32, LANES), F32),
            jax.ShapeDtypeStruct((A_HEADS, CHUNK, CHUNK), BF16),
        ],
        name="prep_tables",
    )(rel_bias_table, jnp.asarray(_BKT_P), jnp.asarray(_BKT_SC), jnp.asarray(_BKT_SN), w_s)


def _gate_pairs(va_rows, wsp_ref, lane_lo):
    outs = []
    for p in range(A_HEADS // 2):
        vp = va_rows[:, p * LANES:(p + 1) * LANES]
        rhs = jnp.concatenate([jnp.where(lane_lo, vp, 0.0), jnp.where(lane_lo, 0.0, vp)], axis=0).astype(BF16)
        outs.append(_dot(wsp_ref[p], rhs))
    return jnp.concatenate(outs, axis=1)


def _prompt_steps(body, first_row_out):
    def kern(*refs):
        i = pl.program_id(0)

        @pl.when(i < N_PROMPT_BLOCKS)
        def _():
            body(*refs)

        @pl.when(i >= N_PROMPT_BLOCKS)
        def _():
            for r in refs[first_row_out:first_row_out + 5]:
                r[...] = jnp.zeros(r.shape, r.dtype)

    return kern


def _mix0_prompt_kernel(x_ref, nm_ref, win_ref, lng_ref, lnb_ref, wsp_ref, bs_ref, bias_ref,
                        wout_ref, nf_ref, wr_ref, br_ref,
                        x1_ref, h_ref, ri_ref, rg_ref, tc_ref, kl_ref, vl_ref, val_ref, cnt_ref,
                        kprev, vprev, mix_scr):
    @pl.when(pl.program_id(0) == 0)
    def _():
        cnt_ref[...] = jnp.zeros_like(cnt_ref)

    x = x_ref[...]
    u, va, q, k, v = _project(x, nm_ref[...], win_ref[...], lng_ref[...], lnb_ref[...])
    lane_lo = lax.broadcasted_iota(jnp.int32, (WINDOW, LANES), 1) < B_HEAD_DIM
    row0 = lax.broadcasted_iota(jnp.int32, (WINDOW, KV_WIDTH), 0) == 0
    first = pl.program_id(0) % STEPS_PER_BATCH == 0

    @pl.when(first)
    def _():
        kprev[...] = jnp.zeros_like(kprev)
        vprev[...] = jnp.zeros_like(vprev)

    for j in range(SUB):
        rows = slice(j * WINDOW, (j + 1) * WINDOW)
        s_gate = _gate_pairs(va[rows], wsp_ref, lane_lo)
        mix_scr[rows, :A_WIDTH] = u[rows] * (s_gate + bs_ref[...])

        if j == 0:
            kp, vp = kprev[...], vprev[...]
        else:
            prows = slice((j - 1) * WINDOW, j * WINDOW)
            kp, vp = k[prows], v[prows]
        kk = jnp.concatenate([jnp.where(row0, 0.0, kp), k[rows]], axis=0)
        vv = jnp.concatenate([jnp.where(row0, 0.0, vp), v[rows]], axis=0)
        kops = (kk.astype(BF16), pltpu.roll(kk, B_HEAD_DIM, 1).astype(BF16))
        vops = (vv.astype(BF16), pltpu.roll(vv, B_HEAD_DIM, 1).astype(BF16))
        qt = [q[rows, p * LANES:(p + 1) * LANES] for p in range(4)]
        q_even = [jnp.where(lane_lo, t, 0.0) for t in qt]
        q_odd = [jnp.where(lane_lo, 0.0, t) for t in qt]
        stacks = (jnp.concatenate([q_even[0], q_even[1], q_odd[2], q_odd[3]], axis=0),
                  jnp.concatenate([q_odd[0], q_odd[1], q_even[2], q_even[3]], axis=0))
        o = []
        for st in range(2):
            s = _dot_nt(stacks[st].astype(BF16), kops[st])
            if j == 0:
                bias = bias_ref[jnp.where(first, 0, 1), st]
            else:
                bias = bias_ref[1, st]
            s = s + bias
            m = jnp.max(s, axis=-1, keepdims=True)
            p = jnp.exp(s - m)
            den = jnp.sum(p, axis=-1, keepdims=True)
            o.append(_dot(p.astype(BF16), vops[st]) / den)
        oa, ob = o
        sl = [slice(i * WINDOW, (i + 1) * WINDOW) for i in range(4)]
        tiles = (jnp.where(lane_lo, oa[sl[0]], ob[sl[0]]), jnp.where(lane_lo, oa[sl[1]], ob[sl[1]]),
                 jnp.where(lane_lo, ob[sl[2]], oa[sl[2]]), jnp.where(lane_lo, ob[sl[3]], oa[sl[3]]))
        for p in range(4):
            mix_scr[rows, A_WIDTH + p * LANES:A_WIDTH + (p + 1) * LANES] = tiles[p]

    last = slice(TM - WINDOW, TM)
    kprev[...] = k[last]
    vprev[...] = v[last]
    kl_ref[...] = k[last]
    vl_ref[...] = v[last]
    val_ref[...] = va[last]

    x1 = x + _dot(mix_scr[...].astype(BF16), wout_ref[...])
    x1_ref[...] = x1
    h, ids, gates = _route(x1, nf_ref[...], wr_ref[...], br_ref[...])
    h_ref[...] = h.reshape(h_ref.shape)
    ri_ref[...] = _rank_pack(ids, cnt_ref, tc_ref)
    rg_ref[...] = gates


def _const_spec(shape):
    nd = len(shape)
    return pl.BlockSpec(shape, lambda i, _n=nd: (0,) * _n)


def _mix0_prompt(x_all, nm, win, lng, lnb, wsp, bs_full, bias_p, wout, nf, wr, br):
    row_spec = pl.BlockSpec((TM, D_MODEL), lambda i: (i, 0))
    row3_spec = pl.BlockSpec((TM, ROW_TILE, LANES), lambda i: (i, 0, 0))
    lane_spec = pl.BlockSpec((TM, LANES), lambda i: (i, 0))
    last_kv = pl.BlockSpec((None, WINDOW, KV_WIDTH), lambda i: (jnp.minimum(i // STEPS_PER_BATCH, BATCH - 1), 0, 0))
    last_va = pl.BlockSpec((None, WINDOW, A_WIDTH), lambda i: (jnp.minimum(i // STEPS_PER_BATCH, BATCH - 1), 0, 0))
    return pl.pallas_call(
        _prompt_steps(_mix0_prompt_kernel, 12),
        grid=(N_ROW_BLOCKS,),
        in_specs=[pl.BlockSpec((TM, D_MODEL), lambda i: (jnp.minimum(i, N_PROMPT_BLOCKS - 1), 0)),
                  _const_spec((1, D_MODEL)), _const_spec((D_MODEL, IN_WIDTH)),
                  _const_spec((1, A_WIDTH)), _const_spec((1, A_WIDTH)),
                  _const_spec((A_HEADS // 2, CHUNK, 2 * CHUNK)), _const_spec((CHUNK, A_WIDTH)),
                  _const_spec((2, 2, 4 * WINDOW, 2 * WINDOW)),
                  _const_spec((A_WIDTH + Q_WIDTH, D_MODEL)), _const_spec((1, D_MODEL)),
                  _const_spec((D_MODEL, 2 * LANES)), _const_spec((1, LANES))],
        out_specs=[row_spec, row3_spec, pl.BlockSpec((8, TM), lambda i: (0, i)), lane_spec,
                   pl.BlockSpec((None, 1, LANES), lambda i: (i, 0, 0)),
                   last_kv, last_kv, last_va, _const_spec((1, LANES))],
        out_shape=[jax.ShapeDtypeStruct((T_ALL, D_MODEL), F32), jax.ShapeDtypeStruct((T_ALL, ROW_TILE, LANES), BF16),
                   jax.ShapeDtypeStruct((8, T_ALL), jnp.int32), jax.ShapeDtypeStruct((T_ALL, LANES), F32),
                   jax.ShapeDtypeStruct((N_ROW_BLOCKS, 1, LANES), F32),
                   jax.ShapeDtypeStruct((BATCH, WINDOW, KV_WIDTH), F32),
                   jax.ShapeDtypeStruct((BATCH, WINDOW, KV_WIDTH), F32),
                   jax.ShapeDtypeStruct((BATCH, WINDOW, A_WIDTH), F32),
                   jax.ShapeDtypeStruct((1, LANES), F32)],
        scratch_shapes=[pltpu.VMEM((WINDOW, KV_WIDTH), F32), pltpu.VMEM((WINDOW, KV_WIDTH), F32),
                        pltpu.VMEM((TM, D_MODEL), F32)],
        compiler_params=_cparams(("arbitrary",)),
        name="mix0_prompt",
    )(x_all, nm, win, lng, lnb, wsp, bs_full, bias_p, wout, nf, wr, br)


def _mix0_sample_kernel(x_ref, nm_ref, win_ref, lng_ref, lnb_ref, wcoef_ref, bcoef_ref,
                        ck_ref, cv_ref, bsc_ref, bsn_ref,
                        wout_ref, nf_ref, wr_ref, br_ref, cnt_in,
                        x1_in, h_in, ri_in, rg_in, tc_in,
                        x1_ref, h_ref, ri_ref, rg_ref, tc_ref, kn_ref, vn_ref, va_ref, cnt_ref,
                        q_scr, k_scr, v_scr, mix_scr):
    del x1_in, h_in, ri_in, rg_in, tc_in
    g = pl.program_id(0)

    @pl.when(g == 0)
    def _():
        u, va, q, k, v = _project(x_ref[...], nm_ref[...], win_ref[...], lng_ref[...], lnb_ref[...])
        q_scr[...] = q
        k_scr[...] = k
        v_scr[...] = v
        kn_ref[...] = k
        vn_ref[...] = v
        va_ref[...] = va
        idx = 0
        for t in range(DEC_SEQ):
            acc = jnp.zeros((DEC_BATCH, A_WIDTH), F32) + bcoef_ref[t:t + 1, :]
            for s in range(t + 1):
                acc = acc + wcoef_ref[idx:idx + 1, :] * va[s * DEC_BATCH:(s + 1) * DEC_BATCH]
                idx += 1
            mix_scr[t * DEC_BATCH:(t + 1) * DEC_BATCH, :A_WIDTH] = u[t * DEC_BATCH:(t + 1) * DEC_BATCH] * acc

    b0 = pl.multiple_of(g * SAMPLE_GROUP, SAMPLE_GROUP)
    lane_lo = lax.broadcasted_iota(jnp.int32, (DEC_SEQ * SAMPLE_GROUP, LANES), 1) < B_HEAD_DIM

    def grab(ref, width):
        return jnp.concatenate([ref[pl.ds(t * DEC_BATCH + b0, SAMPLE_GROUP), :] for t in range(DEC_SEQ)], axis=0)

    qg = grab(q_scr, Q_WIDTH)
    kn = grab(k_scr, KV_WIDTH)
    vn = grab(v_scr, KV_WIDTH)
    ccol0 = lax.broadcasted_iota(jnp.int32, (KV_WIDTH, SAMPLE_GROUP * WINDOW), 1) == 0

    def cache_t(ref):
        t = jnp.concatenate([ref[b].reshape(KV_WIDTH, WINDOW) for b in range(SAMPLE_GROUP)], axis=1)
        return jnp.where(ccol0, 0.0, t)

    def head_swap(t):
        return jnp.concatenate([t[B_HEAD_DIM:], t[:B_HEAD_DIM]], axis=0)

    kct = cache_t(ck_ref)
    vct = cache_t(cv_ref)
    kc_ops = (kct.astype(BF16), head_swap(kct).astype(BF16))
    vc_ops = (vct.astype(BF16), head_swap(vct).astype(BF16))
    kn_ops = (kn.astype(BF16), pltpu.roll(kn, B_HEAD_DIM, 1).astype(BF16))
    vn_ops = (vn.astype(BF16), pltpu.roll(vn, B_HEAD_DIM, 1).astype(BF16))
    qt = [qg[:, p * LANES:(p + 1) * LANES] for p in range(4)]
    q_even = [jnp.where(lane_lo, t, 0.0) for t in qt]
    q_odd = [jnp.where(lane_lo, 0.0, t) for t in qt]
    stacks = (jnp.concatenate([q_even[0], q_even[1], q_odd[2], q_odd[3]], axis=0),
              jnp.concatenate([q_odd[0], q_odd[1], q_even[2], q_even[3]], axis=0))
    o = []
    for st in range(2):
        qs = stacks[st].astype(BF16)
        sc = _dot(qs, kc_ops[st]) + bsc_ref[st]
        sn = _dot_nt(qs, kn_ops[st]) + bsn_ref[st][:, :DEC_SEQ * SAMPLE_GROUP]
        m = jnp.maximum(jnp.max(sc, axis=-1, keepdims=True), jnp.max(sn, axis=-1, keepdims=True))
        pc = jnp.exp(sc - m)
        pn = jnp.exp(sn - m)
        den = jnp.sum(pc, axis=-1, keepdims=True) + jnp.sum(pn, axis=-1, keepdims=True)
        o.append((_dot_nt(pc.astype(BF16), vc_ops[st]) + _dot(pn.astype(BF16), vn_ops[st])) / den)
    oa, ob = o
    n = DEC_SEQ * SAMPLE_GROUP
    sl = [slice(i * n, (i + 1) * n) for i in range(4)]
    tiles = (jnp.where(lane_lo, oa[sl[0]], ob[sl[0]]), jnp.where(lane_lo, oa[sl[1]], ob[sl[1]]),
             jnp.where(lane_lo, ob[sl[2]], oa[sl[2]]), jnp.where(lane_lo, ob[sl[3]], oa[sl[3]]))
    for p in range(4):
        for t in range(DEC_SEQ):
            mix_scr[pl.ds(t * DEC_BATCH + b0, SAMPLE_GROUP), A_WIDTH + p * LANES:A_WIDTH + (p + 1) * LANES] = (
                tiles[p][t * SAMPLE_GROUP:(t + 1) * SAMPLE_GROUP])

    @pl.when(g == N_SAMPLE_GROUPS - 1)
    def _():
        x1 = x_ref[...] + _dot(mix_scr[...].astype(BF16), wout_ref[...])
        x1_ref[...] = x1
        h, ids, gates = _route(x1, nf_ref[...], wr_ref[...], br_ref[...])
        h_ref[...] = h.reshape(h_ref.shape)
        cnt_ref[...] = cnt_in[...]
        ri_ref[...] = _rank_pack(ids, cnt_ref, tc_ref)
        rg_ref[...] = gates


def _mix0_sample(x_all, nm, win, lng, lnb, wcoef, bcoef, ck, cv, bias_sc, bias_sn, wout, nf, wr, br, cnt,
                 x1_all, h_all, ri_all, rg_all, tc_all):
    sample_rows = pl.BlockSpec((TM, D_MODEL), lambda g: (N_PROMPT_BLOCKS, 0))
    sample_rows3 = pl.BlockSpec((TM, ROW_TILE, LANES), lambda g: (N_PROMPT_BLOCKS, 0, 0))
    sample_lanes = pl.BlockSpec((TM, LANES), lambda g: (N_PROMPT_BLOCKS, 0))
    cache_spec = pl.BlockSpec((SAMPLE_GROUP, B_KV_HEADS, B_HEAD_DIM, WINDOW), lambda g: (g, 0, 0, 0))
    anyspec = pl.BlockSpec(memory_space=pl.ANY)
    n_in = 16
    return pl.pallas_call(
        _mix0_sample_kernel,
        grid=(N_SAMPLE_GROUPS,),
        in_specs=[_const_spec((TM, D_MODEL)), _const_spec((1, D_MODEL)), _const_spec((D_MODEL, IN_WIDTH)),
                  _const_spec((1, A_WIDTH)), _const_spec((1, A_WIDTH)),
                  _const_spec((16, A_WIDTH)), _const_spec((8, A_WIDTH)),
                  cache_spec, cache_spec,
                  _const_spec((2, 4 * 32, SAMPLE_GROUP * WINDOW)), _const_spec((2, 4 * 32, LANES)),
                  _const_spec((A_WIDTH + Q_WIDTH, D_MODEL)), _const_spec((1, D_MODEL)),
                  _const_spec((D_MODEL, 2 * LANES)), _const_spec((1, LANES)), _const_spec((1, LANES)),
                  anyspec, anyspec, anyspec, anyspec, anyspec],
        out_specs=[sample_rows, sample_rows3, pl.BlockSpec((8, TM), lambda g: (0, N_PROMPT_BLOCKS)), sample_lanes,
                   pl.BlockSpec((None, 1, LANES), lambda g: (N_PROMPT_BLOCKS, 0, 0)),
                   _const_spec((T_SAMPLE, KV_WIDTH)), _const_spec((T_SAMPLE, KV_WIDTH)),
                   _const_spec((T_SAMPLE, A_WIDTH)), _const_spec((1, LANES))],
        out_shape=[jax.ShapeDtypeStruct((T_ALL, D_MODEL), F32), jax.ShapeDtypeStruct((T_ALL, ROW_TILE, LANES), BF16),
                   jax.ShapeDtypeStruct((8, T_ALL), jnp.int32), jax.ShapeDtypeStruct((T_ALL, LANES), F32),
                   jax.ShapeDtypeStruct((N_ROW_BLOCKS, 1, LANES), F32),
                   jax.ShapeDtypeStruct((T_SAMPLE, KV_WIDTH), F32), jax.ShapeDtypeStruct((T_SAMPLE, KV_WIDTH), F32),
                   jax.ShapeDtypeStruct((T_SAMPLE, A_WIDTH), F32), jax.ShapeDtypeStruct((1, LANES), F32)],
        scratch_shapes=[pltpu.VMEM((T_SAMPLE, Q_WIDTH), F32), pltpu.VMEM((T_SAMPLE, KV_WIDTH), F32),
                        pltpu.VMEM((T_SAMPLE, KV_WIDTH), F32), pltpu.VMEM((T_SAMPLE, D_MODEL), F32)],
        input_output_aliases={n_in: 0, n_in + 1: 1, n_in + 2: 2, n_in + 3: 3, n_in + 4: 4},
        compiler_params=_cparams(("arbitrary",)),
        name="mix0_sample",
    )(x_all, nm, win, lng, lnb, wcoef, bcoef, ck, cv, bias_sc, bias_sn, wout, nf, wr, br, cnt,
      x1_all, h_all, ri_all, rg_all, tc_all)


def _moe_metadata(rt_all, cnt, tcnt):
    counts = cnt[0, :N_EXPERTS].astype(jnp.int32)
    padded = (counts + MOE_BLK - 1) // MOE_BLK * MOE_BLK
    pad_end = jnp.cumsum(padded)
    pad_start = pad_end - padded
    experts = jnp.arange(N_EXPERTS, dtype=jnp.int32)
    n_valid = (pad_end[-1] // MOE_BLK).astype(jnp.int32).reshape(1)
    blk_start = jnp.arange(N_MOE_BLOCKS, dtype=jnp.int32) * MOE_BLK
    block_e = jnp.minimum(jnp.sum((blk_start[:, None] >= pad_end[None, :]).astype(jnp.int32), axis=1),
                          N_EXPERTS - 1).astype(jnp.int32)
    zero_start = (pad_start + counts).astype(jnp.int32)
    zero_len = (padded - counts).astype(jnp.int32)
    first = (blk_start == pad_start[block_e]).astype(jnp.int32)
    used = counts > 0
    parity = ((jnp.cumsum(used.astype(jnp.int32)) - 1) % 2)[block_e].astype(jnp.int32)
    nearest = lax.cummin(jnp.where(used, experts, N_EXPERTS)[::-1])[::-1]
    next_used = jnp.concatenate([nearest[1:], jnp.full((1,), N_EXPERTS, jnp.int32)])
    nxt = jnp.where(next_used < N_EXPERTS, next_used, -1)[block_e].astype(jnp.int32)
    plan = (block_e, first, parity, nxt, n_valid)
    runs = tcnt[:, 0, :N_EXPERTS].astype(jnp.int32)
    run_dst = pad_start[None, :] + jnp.cumsum(runs, axis=0) - runs
    lpos = rt_all[2 * TOP_K:3 * TOP_K].reshape(N_SLOTS).astype(jnp.int32)
    cplan = (lpos, runs.reshape(-1), run_dst.reshape(-1).astype(jnp.int32))
    dplan = cplan + (jnp.concatenate([zero_start, zero_len, n_valid]),)
    return plan, dplan, cplan


RUN_PIECE = 32


def _for_run_pieces(n, start_piece):
    whole = n // RUN_PIECE

    def body(j, carry):
        start_piece(j * RUN_PIECE, RUN_PIECE)
        return carry

    lax.fori_loop(0, whole, body, 0)
    o = whole * RUN_PIECE
    bit = RUN_PIECE // 2
    while bit >= 1:
        take = (n & bit) != 0

        @pl.when(take)
        def _(o=o, bit=bit):
            start_piece(o, bit)

        o = o + jnp.where(take, bit, 0)
        bit //= 2


def _dispatch_kernel(lpos_ref, run_ref, rdst_ref, zs_ref, h_ref, xs_ref, zero_scr, stage, sem, zsem):
    i = pl.program_id(0)

    @pl.when(i == 0)
    def _():
        zero_scr[...] = jnp.zeros_like(zero_scr)

        def pieces(e, do):
            off = zs_ref[e]
            rem = zs_ref[N_EXPERTS + e]
            bit = MOE_BLK // 2
            while bit >= 1:
                take = (rem & bit) != 0

                @pl.when(take)
                def _(off=off, bit=bit):
                    do(pltpu.make_async_copy(zero_scr.at[pl.ds(0, bit)], xs_ref.at[pl.ds(off, bit)], zsem))

                off = off + jnp.where(take, bit, 0)
                bit //= 2

        def start_e(e, c):
            pieces(e, lambda cp: cp.start())
            return c

        def wait_e(e, c):
            pieces(e, lambda cp: cp.wait())
            return c

        def tail(do):
            def step(b, c):
                do(pltpu.make_async_copy(zero_scr, xs_ref.at[pl.ds(b * MOE_BLK, MOE_BLK)], zsem))
                return c
            return step

        n_valid = zs_ref[2 * N_EXPERTS]
        lax.fori_loop(0, N_EXPERTS, start_e, 0)
        lax.fori_loop(n_valid, N_MOE_BLOCKS, tail(lambda cp: cp.start()), 0)
        lax.fori_loop(0, N_EXPERTS, wait_e, 0)
        lax.fori_loop(n_valid, N_MOE_BLOCKS, tail(lambda cp: cp.wait()), 0)

    base = i * TM

    def place(r, carry):
        row = h_ref[r]
        for kk in range(TOP_K):
            stage[lpos_ref[kk * T_ALL + base + r]] = row
        return carry

    lax.fori_loop(0, TM, place, 0, unroll=8)

    def send_run(e, off):
        n = run_ref[i * N_EXPERTS + e]
        dst = rdst_ref[i * N_EXPERTS + e]
        _for_run_pieces(n, lambda o, size: pltpu.make_async_copy(
            stage.at[pl.ds(off + o, size)], xs_ref.at[pl.ds(dst + o, size)], sem).start(
                priority=size.bit_length() % 2))
        return off + n

    lax.fori_loop(0, N_EXPERTS, send_run, 0)
    pltpu.make_async_copy(stage, xs_ref.at[pl.ds(0, TM * TOP_K)], sem).wait()


def _dispatch(dplan, h_all):
    return pl.pallas_call(
        _dispatch_kernel,
        grid_spec=pltpu.PrefetchScalarGridSpec(
            num_scalar_prefetch=4,
            grid=(N_ROW_BLOCKS,),
            in_specs=[pl.BlockSpec((TM, ROW_TILE, LANES), lambda i, lp, rn, rd, z: (i, 0, 0))],
            out_specs=pl.BlockSpec(memory_space=pl.ANY),
            scratch_shapes=[pltpu.VMEM((MOE_BLK, ROW_TILE, LANES), BF16),
                            pltpu.VMEM((TM * TOP_K, ROW_TILE, LANES), BF16),
                            pltpu.SemaphoreType.DMA(()), pltpu.SemaphoreType.DMA(())],
        ),
        out_shape=jax.ShapeDtypeStruct((N_SORT_ROWS, ROW_TILE, LANES), BF16),
        compiler_params=_cparams(("arbitrary",)),
        name="moe_dispatch",
    )(*dplan, h_all)


def _experts_kernel(layer, be_ref, first_ref, par_ref, nxt_ref, nv_ref,
                    x_ref, wg_hbm, wu_hbm, wd_hbm, y_ref,
                    wg_s, wu_s, wd_s, wg_f, wu_f, wd_f, wsem):
    i = pl.program_id(0)

    def fetch(e, slot):
        return (pltpu.make_async_copy(wg_hbm.at[layer, e], wg_f.at[slot], wsem.at[slot]),
                pltpu.make_async_copy(wu_hbm.at[layer, e], wu_f.at[slot], wsem.at[slot]),
                pltpu.make_async_copy(wd_hbm.at[layer, e], wd_f.at[slot], wsem.at[slot]))

    @pl.when(i < nv_ref[0])
    def _():
        e = be_ref[i]
        slot = par_ref[i]

        @pl.when(i == 0)
        def _():
            for cp in fetch(e, slot):
                cp.start()

        @pl.when(first_ref[i] == 1)
        def _():
            for cp in fetch(e, slot):
                cp.wait()
            wg_s[...] = wg_f[slot].astype(BF16)
            wu_s[...] = wu_f[slot].astype(BF16)
            wd_s[...] = wd_f[slot].astype(BF16)
            nxt = nxt_ref[i]

            @pl.when(nxt >= 0)
            def _():
                for cp in fetch(nxt, 1 - slot):
                    cp.start()

        xb = x_ref[...].reshape(MOE_BLK, D_MODEL)
        a = jax.nn.silu(_dot(xb, wg_s[...])) * _dot(xb, wu_s[...])
        y_ref[...] = _dot(a.astype(BF16), wd_s[...]).reshape(y_ref.shape)

    @pl.when(i >= nv_ref[0])
    def _():
        y_ref[...] = jnp.zeros(y_ref.shape, y_ref.dtype)


def _experts(block_e, first, parity, nxt, n_valid, xs, w_gate, w_up, w_down, layer):
    def blk(i, be, fi, pa, nx, nv):
        return (jnp.maximum(jnp.minimum(i, nv[0] - 1), 0), 0, 0)

    anyspec = pl.BlockSpec(memory_space=pl.ANY)
    return pl.pallas_call(
        functools.partial(_experts_kernel, layer),
        grid_spec=pltpu.PrefetchScalarGridSpec(
            num_scalar_prefetch=5,
            grid=(N_MOE_BLOCKS,),
            in_specs=[pl.BlockSpec((MOE_BLK, ROW_TILE, LANES), blk), anyspec, anyspec, anyspec],
            out_specs=pl.BlockSpec((MOE_BLK, ROW_TILE, LANES), lambda i, be, fi, pa, nx, nv: (i, 0, 0)),
            scratch_shapes=[pltpu.VMEM((D_MODEL, D_EXPERT), BF16), pltpu.VMEM((D_MODEL, D_EXPERT), BF16),
                            pltpu.VMEM((D_EXPERT, D_MODEL), BF16),
                            pltpu.VMEM((2, D_MODEL, D_EXPERT), F32), pltpu.VMEM((2, D_MODEL, D_EXPERT), F32),
                            pltpu.VMEM((2, D_EXPERT, D_MODEL), F32), pltpu.SemaphoreType.DMA((2,))],
        ),
        out_shape=jax.ShapeDtypeStruct((N_SORT_ROWS, ROW_TILE, LANES), F32),
        compiler_params=_cparams(("arbitrary",)),
        name="moe_experts",
    )(block_e, first, parity, nxt, n_valid, xs, w_gate, w_up, w_down)


def _gather_rows(lpos_ref, run_ref, rdst_ref, ys_ref, ystage, ybuf, sem, i):
    def fetch(tile, buf):
        def fetch_run(e, off):
            n = run_ref[tile * N_EXPERTS + e]
            src = rdst_ref[tile * N_EXPERTS + e]
            _for_run_pieces(n, lambda o, size: pltpu.make_async_copy(
                ys_ref.at[pl.ds(src + o, size)], ystage.at[buf, pl.ds(off + o, size)], sem.at[buf]).start(
                    priority=size.bit_length() % 2))
            return off + n

        lax.fori_loop(0, N_EXPERTS, fetch_run, 0)

    buf = i % 2

    @pl.when(i == 0)
    def _():
        fetch(i, buf)

    @pl.when(i + 1 < N_ROW_BLOCKS)
    def _():
        fetch(i + 1, 1 - buf)

    pltpu.make_async_copy(ys_ref.at[pl.ds(0, TM * TOP_K)], ystage.at[buf], sem.at[buf]).wait()
    base = i * TM

    def unplace(r, carry):
        for kk in range(TOP_K):
            ybuf[kk, r] = ystage[buf, lpos_ref[kk * T_ALL + base + r]]
        return carry

    lax.fori_loop(0, TM, unplace, 0, unroll=8)


def _combined(x_ref, rg_ref, ybuf):
    rg = rg_ref[...]
    y0 = ybuf[0].reshape(TM, D_MODEL)
    y1 = ybuf[1].reshape(TM, D_MODEL)
    return x_ref[...] + rg[:, 0:1] * y0 + rg[:, 1:2] * y1


_COMBINE_SCRATCH = [pltpu.VMEM((2, TM * TOP_K, ROW_TILE, LANES), F32), pltpu.VMEM((TOP_K, TM, ROW_TILE, LANES), F32),
                    pltpu.SemaphoreType.DMA((2,))]


def _combine_kernel(lpos_ref, run_ref, rdst_ref, x_ref, rg_ref, ys_ref, o_ref, ystage, ybuf, sem):
    _gather_rows(lpos_ref, run_ref, rdst_ref, ys_ref, ystage, ybuf, sem, pl.program_id(0))
    o_ref[...] = _combined(x_ref, rg_ref, ybuf)


def _combine(cplan, x_all, rg_all, ys):
    return pl.pallas_call(
        _combine_kernel,
        grid_spec=pltpu.PrefetchScalarGridSpec(
            num_scalar_prefetch=3,
            grid=(N_ROW_BLOCKS,),
            in_specs=[pl.BlockSpec((TM, D_MODEL), lambda i, a, b, c: (i, 0)),
                      pl.BlockSpec((TM, LANES), lambda i, a, b, c: (i, 0)),
                      pl.BlockSpec(memory_space=pl.ANY)],
            out_specs=pl.BlockSpec((TM, D_MODEL), lambda i, a, b, c: (i, 0)),
            scratch_shapes=_COMBINE_SCRATCH,
        ),
        out_shape=jax.ShapeDtypeStruct((T_ALL, D_MODEL), F32),
        compiler_params=_cparams(("arbitrary",)),
        name="moe_combine",
    )(*cplan, x_all, rg_all, ys)


def _final_kernel(lpos_ref, run_ref, rdst_ref, x_ref, rg_ref, ys_ref, nfin_ref, op_ref, os_ref, ystage, ybuf, sem):
    i = pl.program_id(0)
    _gather_rows(lpos_ref, run_ref, rdst_ref, ys_ref, ystage, ybuf, sem, i)
    y = _rms(_combined(x_ref, rg_ref, ybuf), nfin_ref[...])

    @pl.when(i < N_PROMPT_BLOCKS)
    def _():
        op_ref[...] = y

    @pl.when(i >= N_PROMPT_BLOCKS)
    def _():
        os_ref[...] = y


def _final(cplan, x_all, rg_all, ys, nfin):
    return pl.pallas_call(
        _final_kernel,
        grid_spec=pltpu.PrefetchScalarGridSpec(
            num_scalar_prefetch=3,
            grid=(N_ROW_BLOCKS,),
            in_specs=[pl.BlockSpec((TM, D_MODEL), lambda i, a, b, c: (i, 0)),
                      pl.BlockSpec((TM, LANES), lambda i, a, b, c: (i, 0)),
                      pl.BlockSpec(memory_space=pl.ANY),
                      pl.BlockSpec((1, D_MODEL), lambda i, a, b, c: (0, 0))],
            out_specs=[pl.BlockSpec((TM, D_MODEL), lambda i, a, b, c: (jnp.minimum(i, N_PROMPT_BLOCKS - 1), 0)),
                       pl.BlockSpec((TM, D_MODEL), lambda i, a, b, c: (0, 0))],
            scratch_shapes=_COMBINE_SCRATCH,
        ),
        out_shape=[jax.ShapeDtypeStruct((T_PROMPT, D_MODEL), F32), jax.ShapeDtypeStruct((T_SAMPLE, D_MODEL), F32)],
        compiler_params=_cparams(("arbitrary",)),
        name="moe_combine_final",
    )(*cplan, x_all, rg_all, ys, nfin)


def _moe(h_all, rt_all, cnt, tcnt, w_gate, w_up, w_down, layer):
    plan, dplan, cplan = _moe_metadata(rt_all, cnt, tcnt)
    xs = _dispatch(dplan, h_all)
    ys = _experts(*plan, xs, w_gate, w_up, w_down, layer)
    return cplan, ys


def _pool_project(d_groups, wp_ref, scale):
    outs = [_dot(d_groups[g].astype(BF16), wp_ref[g]) for g in range(len(POOL_SIZES))]
    return jnp.concatenate(outs, axis=1) * scale


def _mix1_prompt_kernel(x_ref, nm_ref, wp_ref, sc_ref, nf_ref, wr_ref, br_ref,
                        x3_ref, h_ref, ri_ref, rg_ref, tc_ref, pl_ref, cnt_ref, ext):
    i = pl.program_id(0)

    @pl.when(i == 0)
    def _():
        cnt_ref[...] = jnp.zeros_like(cnt_ref)

    x = x_ref[...]
    hp = _rms(x, nm_ref[...])

    @pl.when(i % STEPS_PER_BATCH == 0)
    def _():
        ext[0:POOL_MAX, :] = jnp.zeros((POOL_MAX, D_MODEL), F32)

    ext[POOL_MAX:, :] = hp
    pos = (i % STEPS_PER_BATCH) * TM + lax.broadcasted_iota(jnp.int32, (TM, 1), 0)
    d_groups = []
    for g, w in enumerate(POOL_SIZES):
        cols = slice(g * POOL_GROUP_DIM, (g + 1) * POOL_GROUP_DIM)
        acc = ext[:, cols]
        span = 1
        while span < w:
            acc = acc + pltpu.roll(acc, span, 0)
            span *= 2
        cnt = jnp.minimum(pos + 1, w).astype(F32)
        d_groups.append(acc[POOL_MAX:] / cnt - hp[:, cols])
    tail = hp[TM - POOL_MAX:, :]
    ext[0:POOL_MAX, :] = tail
    pl_ref[...] = tail

    x3 = x + _pool_project(d_groups, wp_ref, sc_ref[...])
    x3_ref[...] = x3
    h, ids, gates = _route(x3, nf_ref[...], wr_ref[...], br_ref[...])
    h_ref[...] = h.reshape(h_ref.shape)
    ri_ref[...] = _rank_pack(ids, cnt_ref, tc_ref)
    rg_ref[...] = gates


def _mix1_prompt(x_all, nm, wp, sc, nf, wr, br):
    row_spec = pl.BlockSpec((TM, D_MODEL), lambda i: (i, 0))
    row3_spec = pl.BlockSpec((TM, ROW_TILE, LANES), lambda i: (i, 0, 0))
    lane_spec = pl.BlockSpec((TM, LANES), lambda i: (i, 0))
    return pl.pallas_call(
        _prompt_steps(_mix1_prompt_kernel, 7),
        grid=(N_ROW_BLOCKS,),
        in_specs=[row_spec, _const_spec((1, D_MODEL)),
                  _const_spec((len(POOL_SIZES), POOL_GROUP_DIM, POOL_GROUP_DIM)), _const_spec((1, D_MODEL)),
                  _const_spec((1, D_MODEL)), _const_spec((D_MODEL, 2 * LANES)), _const_spec((1, LANES))],
        out_specs=[row_spec, row3_spec, pl.BlockSpec((8, TM), lambda i: (0, i)), lane_spec,
                   pl.BlockSpec((None, 1, LANES), lambda i: (i, 0, 0)),
                   pl.BlockSpec((None, POOL_MAX, D_MODEL),
                                lambda i: (jnp.minimum(i // STEPS_PER_BATCH, BATCH - 1), 0, 0)),
                   _const_spec((1, LANES))],
        out_shape=[jax.ShapeDtypeStruct((T_ALL, D_MODEL), F32), jax.ShapeDtypeStruct((T_ALL, ROW_TILE, LANES), BF16),
                   jax.ShapeDtypeStruct((8, T_ALL), jnp.int32), jax.ShapeDtypeStruct((T_ALL, LANES), F32),
                   jax.ShapeDtypeStruct((N_ROW_BLOCKS, 1, LANES), F32),
                   jax.ShapeDtypeStruct((BATCH, POOL_MAX, D_MODEL), F32), jax.ShapeDtypeStruct((1, LANES), F32)],
        scratch_shapes=[pltpu.VMEM((POOL_MAX + TM, D_MODEL), F32)],
        compiler_params=_cparams(("arbitrary",)),
        name="mix1_prompt",
    )(x_all, nm, wp, sc, nf, wr, br)


def _mix1_sample_kernel(x_ref, st_ref, nm_ref, wp_ref, sc_ref, nf_ref, wr_ref, br_ref, cnt_in,
                        x3_in, h_in, ri_in, rg_in, tc_in,
                        x3_ref, h_ref, ri_ref, rg_ref, tc_ref, hs_ref, cnt_ref):
    del x3_in, h_in, ri_in, rg_in, tc_in
    x = x_ref[...]
    hs = _rms(x, nm_ref[...])
    hs_ref[...] = hs
    n_ctx = POOL_MAX - 1
    d_groups = []
    for g, w in enumerate(POOL_SIZES):
        cols = slice(g * POOL_GROUP_DIM, (g + 1) * POOL_GROUP_DIM)
        parts = []
        for t in range(DEC_SEQ):
            acc = hs[t * DEC_BATCH:(t + 1) * DEC_BATCH, cols]
            for back in range(1, w):
                src = t - back
                if src >= 0:
                    acc = acc + hs[src * DEC_BATCH:(src + 1) * DEC_BATCH, cols]
                else:
                    acc = acc + st_ref[n_ctx + src, :, cols]
            parts.append(acc / float(w) - hs[t * DEC_BATCH:(t + 1) * DEC_BATCH, cols])
        d_groups.append(jnp.concatenate(parts, axis=0))
    x3 = x + _pool_project(d_groups, wp_ref, sc_ref[...])
    x3_ref[...] = x3
    h, ids, gates = _route(x3, nf_ref[...], wr_ref[...], br_ref[...])
    h_ref[...] = h.reshape(h_ref.shape)
    cnt_ref[...] = cnt_in[...]
    ri_ref[...] = _rank_pack(ids, cnt_ref, tc_ref)
    rg_ref[...] = gates


def _mix1_sample(x_all, state_t, nm, wp, sc, nf, wr, br, cnt, x3_all, h_all, ri_all, rg_all, tc_all):
    sample_rows = pl.BlockSpec((TM, D_MODEL), lambda g: (N_PROMPT_BLOCKS, 0))
    sample_rows3 = pl.BlockSpec((TM, ROW_TILE, LANES), lambda g: (N_PROMPT_BLOCKS, 0, 0))
    sample_lanes = pl.BlockSpec((TM, LANES), lambda g: (N_PROMPT_BLOCKS, 0))
    anyspec = pl.BlockSpec(memory_space=pl.ANY)
    n_in = 9
    return pl.pallas_call(
        _mix1_sample_kernel,
        grid=(1,),
        in_specs=[sample_rows, _const_spec((POOL_MAX - 1, DEC_BATCH, D_MODEL)), _const_spec((1, D_MODEL)),
                  _const_spec((len(POOL_SIZES), POOL_GROUP_DIM, POOL_GROUP_DIM)), _const_spec((1, D_MODEL)),
                  _const_spec((1, D_MODEL)), _const_spec((D_MODEL, 2 * LANES)), _const_spec((1, LANES)),
                  _const_spec((1, LANES)), anyspec, anyspec, anyspec, anyspec, anyspec],
        out_specs=[sample_rows, sample_rows3, pl.BlockSpec((8, TM), lambda g: (0, N_PROMPT_BLOCKS)), sample_lanes,
                   pl.BlockSpec((None, 1, LANES), lambda g: (N_PROMPT_BLOCKS, 0, 0)),
                   _const_spec((T_SAMPLE, D_MODEL)), _const_spec((1, LANES))],
        out_shape=[jax.ShapeDtypeStruct((T_ALL, D_MODEL), F32), jax.ShapeDtypeStruct((T_ALL, ROW_TILE, LANES), BF16),
                   jax.ShapeDtypeStruct((8, T_ALL), jnp.int32), jax.ShapeDtypeStruct((T_ALL, LANES), F32),
                   jax.ShapeDtypeStruct((N_ROW_BLOCKS, 1, LANES), F32),
                   jax.ShapeDtypeStruct((T_SAMPLE, D_MODEL), F32), jax.ShapeDtypeStruct((1, LANES), F32)],
        input_output_aliases={n_in: 0, n_in + 1: 1, n_in + 2: 2, n_in + 3: 3, n_in + 4: 4},
        compiler_params=_cparams(("arbitrary",)),
        name="mix1_sample",
    )(x_all, state_t, nm, wp, sc, nf, wr, br, cnt, x3_all, h_all, ri_all, rg_all, tc_all)


def _router_weights(wg, bg, we, be):
    w = jnp.concatenate([wg, jnp.transpose(we, (1, 0, 2)).reshape(D_MODEL, N_EXPERTS)], axis=1)
    b = jnp.concatenate([bg, be.reshape(N_EXPERTS)])
    pad = LANES - N_GROUPS - N_EXPERTS
    w = jnp.pad(w, ((0, 0), (0, pad)))
    w_hi = w.astype(BF16)
    w_lo = (w - w_hi.astype(F32)).astype(BF16)
    return jnp.concatenate([w_hi, w_lo], axis=1), jnp.pad(b, (0, pad)).reshape(1, LANES)


def _stack(tab):
    return jnp.stack([jnp.concatenate([tab[h] for h in heads], axis=0) for heads in STACK_HEADS])


def kernel(x_prompt, x_sample, cache_k_win, cache_v_win, state_pool, norm_mix, norm_ffn, norm_final, w_in,
           a_ln_g, a_ln_b, a_w_s, a_b_s, b_sinks, rel_bias_table, w_out, c_w_pool, c_scale,
           router_group_w, router_group_b, router_expert_w, router_expert_b, w_gate, w_up, w_down):
    xs_t = jnp.transpose(x_sample, (1, 0, 2)).reshape(T_SAMPLE, D_MODEL)
    xp2 = x_prompt.reshape(T_PROMPT, D_MODEL)
    win =w_in[0].astype(BF16)
    wout = w_out[0].astype(BF16)
    lng = a_ln_g[0].reshape(1, A_WIDTH)
    lnb = a_ln_b[0].reshape(1, A_WIDTH)
    bias_p, bias_sc, bias_sn, ws_tril = _prep(rel_bias_table, a_w_s[0])
    wsp = ws_tril.reshape(A_HEADS // 2, 2, CHUNK, CHUNK).transpose(0, 2, 1, 3).reshape(A_HEADS // 2, CHUNK, 2 * CHUNK)
    bs_full = jnp.repeat(a_b_s[0].T, A_HEAD_DIM, axis=1)
    bias_p = jnp.stack([_stack(bias_p[0]), _stack(bias_p[1])])
    bias_sc = _stack(bias_sc)
    bias_sn = _stack(bias_sn)
    sinks = b_sinks[0]
    sink_p = jnp.stack([jnp.repeat(sinks[jnp.array(hh)], WINDOW) for hh in STACK_HEADS])
    sink_s = jnp.stack([jnp.repeat(sinks[jnp.array(hh)], 32) for hh in STACK_HEADS])
    bias_p = bias_p.at[:, :, :, 0].set(jnp.broadcast_to(sink_p[None], (2, 2, 4 * WINDOW)))
    bias_sc = bias_sc.at[:, :, 0].set(sink_s)
    pairs = [(t, s) for t in range(DEC_SEQ) for s in range(t + 1)]
    wcoef = jnp.stack([jnp.repeat(a_w_s[0][:, t, s], A_HEAD_DIM) for t, s in pairs])
    wcoef = jnp.pad(wcoef, ((0, 16 - len(pairs)), (0, 0)))
    bcoef = jnp.pad(jnp.repeat(a_b_s[0][:, :DEC_SEQ].T, A_HEAD_DIM, axis=1), ((0, 8 - DEC_SEQ), (0, 0)))
    ck = jnp.transpose(cache_k_win[0], (0, 2, 3, 1))
    cv = jnp.transpose(cache_v_win[0], (0, 2, 3, 1))
    routers = [_router_weights(router_group_w[l], router_group_b[l], router_expert_w[l], router_expert_b[l])
               for l in range(2)]
    nm = [norm_mix[l].reshape(1, D_MODEL) for l in range(2)]
    nf = [norm_ffn[l].reshape(1, D_MODEL) for l in range(2)]

    x1_all, h_all, ri_all, rg_all, tc_all, k_last, v_last, va_last, cnt0 = _mix0_prompt(
        xp2, nm[0], win, lng, lnb, wsp, bs_full, bias_p, wout, nf[0], *routers[0])
    x1_all, h_all, ri_all, rg_all, tc_all, k_new, v_new, va_s, cnt0 = _mix0_sample(
        xs_t, nm[0], win, lng, lnb, wcoef, bcoef, ck, cv, bias_sc, bias_sn, wout, nf[0], *routers[0], cnt0,
        x1_all, h_all, ri_all, rg_all, tc_all)
    cplan0, ys0 = _moe(h_all, ri_all, cnt0, tc_all, w_gate, w_up, w_down, 0)
    x2_all = _combine(cplan0, x1_all, rg_all, ys0)

    wp = c_w_pool[0].astype(BF16)
    sc = c_scale[0].reshape(1, D_MODEL)
    x3_all, h2_all, ri2_all, rg2_all, tc2_all, pool_tail, cnt1 = _mix1_prompt(
        x2_all, nm[1], wp, sc, nf[1], *routers[1])
    state_t = jnp.transpose(state_pool[0], (1, 0, 2))
    x3_all, h2_all, ri2_all, rg2_all, tc2_all, hs1, cnt1 = _mix1_sample(
        x2_all, state_t, nm[1], wp, sc, nf[1], *routers[1], cnt1, x3_all, h2_all, ri2_all, rg2_all, tc2_all)
    cplan1, ys1 = _moe(h2_all, ri2_all, cnt1, tc2_all, w_gate, w_up, w_down, 1)
    y_p, y_s = _final(cplan1, x3_all, rg2_all, ys1, norm_final.reshape(1, D_MODEL))

    def from_tmajor(a, width):
        return jnp.transpose(a.reshape(DEC_SEQ, DEC_BATCH, width), (1, 0, 2))

    y_prompt = y_p.reshape(BATCH, SEQ, D_MODEL)
    y_sample = from_tmajor(y_s, D_MODEL)
    win_k_p = k_last.reshape(1, BATCH, WINDOW, B_KV_HEADS, B_HEAD_DIM)
    win_v_p = v_last.reshape(1, BATCH, WINDOW, B_KV_HEADS, B_HEAD_DIM)
    kn = from_tmajor(k_new, KV_WIDTH).reshape(DEC_BATCH, DEC_SEQ, B_KV_HEADS, B_HEAD_DIM)
    vn = from_tmajor(v_new, KV_WIDTH).reshape(DEC_BATCH, DEC_SEQ, B_KV_HEADS, B_HEAD_DIM)
    win_k_s = jnp.concatenate([cache_k_win[0][:, DEC_SEQ:], kn], axis=1)[None]
    win_v_s = jnp.concatenate([cache_v_win[0][:, DEC_SEQ:], vn], axis=1)[None]
    chunk_v_p = va_last.reshape(1, BATCH, CHUNK, A_HEADS, A_HEAD_DIM)
    chunk_v_s = from_tmajor(va_s, A_WIDTH).reshape(1, DEC_BATCH, DEC_SEQ, A_HEADS, A_HEAD_DIM)
    pool_p = pool_tail[:, 1:][None]
    pool_s = jnp.concatenate([state_pool[0][:, DEC_SEQ:], from_tmajor(hs1, D_MODEL)], axis=1)[None]
    return (y_prompt, y_sample, win_k_p, win_v_p, win_k_s, win_v_s, chunk_v_p, chunk_v_s, pool_p, pool_s)
```

```python
import functools
import math

import numpy as np
import jax
import jax.numpy as jnp
from jax import lax
from jax.experimental import pallas as pl
from jax.experimental.pallas import tpu as pltpu

F32 = jnp.float32
BF16 = jnp.bfloat16

D_MODEL = 1024
BATCH = 2
SEQ = 8192
DEC_BATCH = 128
DEC_SEQ = 4
A_WIDTH = 512
A_HEADS = 8
A_HEAD_DIM = 64
CHUNK = 128
B_HEADS = 8
B_KV_HEADS = 2
B_HEAD_DIM = 64
B_GROUP = 4
WINDOW = 128
N_BUCKETS = 32
MAX_DISTANCE = WINDOW
Q_WIDTH = 512
KV_WIDTH = 128
IN_WIDTH = 2 * A_WIDTH + Q_WIDTH + 2 * KV_WIDTH
ATTN_SCALE = B_HEAD_DIM ** -0.5
NEG_INF = -1e30
POOL_SIZES = (2, 4, 8, 16)
POOL_GROUP_DIM = 256
POOL_MAX = 16
N_GROUPS = 4
EXPERTS_PER_GROUP = 8
N_EXPERTS = 32
TOP_K = 2
D_EXPERT = 512
EPS = 1e-6

LANES = 128
ROW_TILE = D_MODEL // LANES
T_PROMPT = BATCH * SEQ
T_SAMPLE = DEC_BATCH * DEC_SEQ
T_ALL = T_PROMPT + T_SAMPLE
TM = 512
N_PROMPT_BLOCKS = T_PROMPT // TM
N_ROW_BLOCKS = T_ALL // TM
STEPS_PER_BATCH = SEQ // TM
SUB = TM // WINDOW
N_SLOTS = T_ALL * TOP_K
MOE_BLK = 512
N_MOE_BLOCKS = N_SLOTS // MOE_BLK + N_EXPERTS
N_SORT_ROWS = N_MOE_BLOCKS * MOE_BLK
SAMPLE_GROUP = 8
N_SAMPLE_GROUPS = DEC_BATCH // SAMPLE_GROUP
VMEM_LIMIT = 56 * 1024 * 1024

STACK_HEADS = ((0, 2, 5, 7), (1, 3, 4, 6))


def _t5_bucket_np(dist):
    n = np.maximum(dist, 0)
    max_exact = N_BUCKETS // 2
    nf = np.maximum(n, 1).astype(np.float32)
    large = max_exact + (np.log(nf / np.float32(max_exact)) / np.float32(math.log(MAX_DISTANCE / max_exact))
                         * np.float32(N_BUCKETS - max_exact)).astype(np.int32)
    large = np.minimum(large, N_BUCKETS - 1)
    return np.where(n < max_exact, n, large).astype(np.int32)


def _bucket_tables():
    qi = np.arange(WINDOW)[:, None]
    ki = np.arange(2 * WINDOW)[None, :]
    dist = qi + WINDOW - ki
    valid = (dist >= 0) & (dist < WINDOW)
    bp = np.where(valid, _t5_bucket_np(dist), -1)
    bp_first = np.where(ki >= WINDOW, bp, -1)
    bkt_p = np.stack([bp_first, bp]).astype(np.int32)

    t = np.repeat(np.arange(DEC_SEQ), SAMPLE_GROUP)[:, None]
    b = np.tile(np.arange(SAMPLE_GROUP), DEC_SEQ)[:, None]
    cb = np.repeat(np.arange(SAMPLE_GROUP), WINDOW)[None, :]
    cj = np.tile(np.arange(WINDOW), SAMPLE_GROUP)[None, :]
    dist_c = t + WINDOW - cj
    valid_c = (cb == b) & (dist_c >= 0) & (dist_c < WINDOW)
    bkt_sc = np.where(valid_c, _t5_bucket_np(dist_c), -1).astype(np.int32)
    nt = np.repeat(np.arange(DEC_SEQ), SAMPLE_GROUP)[None, :]
    nb = np.tile(np.arange(SAMPLE_GROUP), DEC_SEQ)[None, :]
    dist_n = t - nt
    valid_n = (nb == b) & (dist_n >= 0)
    bkt_sn = np.where(valid_n, _t5_bucket_np(dist_n), -1).astype(np.int32)
    bkt_sn = np.concatenate([bkt_sn, np.full((32, LANES - 32), -1, np.int32)], axis=1)
    return bkt_p, bkt_sc, bkt_sn


_BKT_P, _BKT_SC, _BKT_SN = _bucket_tables()


def _cparams(semantics):
    return pltpu.CompilerParams(dimension_semantics=semantics, vmem_limit_bytes=VMEM_LIMIT)


def _rms(x, g):
    return x * lax.rsqrt(jnp.mean(x * x, axis=-1, keepdims=True) + EPS) * g


def _layernorm(x, g, b):
    xc = x - jnp.mean(x, axis=-1, keepdims=True)
    return xc * lax.rsqrt(jnp.mean(xc * xc, axis=-1, keepdims=True) + EPS) * g + b


def _dot(a, b):
    return jnp.dot(a, b, preferred_element_type=F32)


def _dot_nt(a, b):
    return lax.dot_general(a, b, (((1,), (1,)), ((), ())), preferred_element_type=F32)


def _project(x, nm, win, lng, lnb):
    h = _rms(x, nm)
    z = _dot(h.astype(BF16), win)
    u = jax.nn.gelu(z[:, :A_WIDTH])
    va = _layernorm(jax.nn.gelu(z[:, A_WIDTH:2 * A_WIDTH]), lng, lnb)
    q = z[:, 2 * A_WIDTH:2 * A_WIDTH + Q_WIDTH] * ATTN_SCALE
    k = z[:, 2 * A_WIDTH + Q_WIDTH:2 * A_WIDTH + Q_WIDTH + KV_WIDTH]
    v = z[:, 2 * A_WIDTH + Q_WIDTH + KV_WIDTH:]
    return u, va, q, k, v


def _route(x1, nf, wr, br):
    hf = _rms(x1, nf)
    h = hf.astype(BF16)
    h_lo = (hf - h.astype(F32)).astype(BF16)
    part = _dot(h, wr)
    logits = part[:, :LANES] + part[:, LANES:] + _dot(h_lo, wr[:, :LANES]) + br
    rows = logits.shape[0]
    lane = lax.broadcasted_iota(jnp.int32, (rows, LANES), 1)
    lanef = lane.astype(F32)
    big = jnp.float32(1e9)
    is_g = lane < N_GROUPS
    gl = jnp.where(is_g, logits, -jnp.inf)
    gmax = jnp.max(gl, axis=1, keepdims=True)
    gsel = jnp.min(jnp.where(gl == gmax, lanef, big), axis=1, keepdims=True)
    gsum = jnp.sum(jnp.where(is_g, jnp.exp(logits - gmax), 0.0), axis=1, keepdims=True)
    g1 = 1.0 / gsum
    lo = N_GROUPS + EXPERTS_PER_GROUP * gsel
    emask = (lanef >= lo) & (lanef < lo + EXPERTS_PER_GROUP)
    el = jnp.where(emask, logits, -jnp.inf)
    v1 = jnp.max(el, axis=1, keepdims=True)
    i1 = jnp.min(jnp.where(el == v1, lanef, big), axis=1, keepdims=True)
    el2 = jnp.where(lanef == i1, -jnp.inf, el)
    v2 = jnp.max(el2, axis=1, keepdims=True)
    i2 = jnp.min(jnp.where(el2 == v2, lanef, big), axis=1, keepdims=True)
    e2 = jnp.exp(v2 - v1)
    den = 1.0 + e2
    w1 = g1 / den
    w2 = g1 * e2 / den
    ids = jnp.where(lane == 0, i1 - N_GROUPS, jnp.where(lane == 1, i2 - N_GROUPS, 0.0)).astype(jnp.int32)
    gates = jnp.where(lane == 0, w1, jnp.where(lane == 1, w2, 0.0))
    return h, ids, gates


def _rank_pack(ids, cnt_ref, tcnt_ref):
    rows = ids.shape[0]
    lane = lax.broadcasted_iota(jnp.int32, (rows, LANES), 1)
    o0 = (lane == ids[:, 0:1]).astype(F32)
    o1 = (lane == ids[:, 1:2]).astype(F32)
    r = lax.broadcasted_iota(jnp.int32, (rows, rows), 0)
    c = lax.broadcasted_iota(jnp.int32, (rows, rows), 1)
    before = (c < r).astype(BF16)
    p01 = _dot(before, jnp.concatenate([o0, o1], axis=1).astype(BF16))
    p0 = p01[:, :LANES]
    p1 = p01[:, LANES:]
    c0 = jnp.sum(o0, axis=0, keepdims=True)
    c1 = jnp.sum(o1, axis=0, keepdims=True)
    ctile = c0 + c1
    cnt_ref[...] = cnt_ref[...] + ctile
    tcnt_ref[...] = ctile
    inc = jnp.broadcast_to(ctile, (8, LANES))
    lane8 = lax.broadcasted_iota(jnp.int32, (8, LANES), 1)
    for sh in (1, 2, 4, 8, 16, 32, 64):
        inc = inc + jnp.where(lane8 >= sh, pltpu.roll(inc, sh, 1), 0.0)
    start = inc[0:1] - ctile
    lpos0 = jnp.sum(o0 * (start + p0), axis=1, keepdims=True)
    lpos1 = jnp.sum(o1 * (start + c0 + p1), axis=1, keepdims=True)
    idf = ids.astype(F32)
    packed = jnp.where(lane < TOP_K, idf, 0.0)
    for ln, col in ((4, lpos0), (5, lpos1)):
        packed = jnp.where(lane == ln, col, packed)
    return jnp.transpose(packed)[:8].astype(jnp.int32)


def _prep_kernel(tab_ref, sink_ref, bp_ref, bsc_ref, bsn_ref, ws_ref, op_ref, osc_ref, osn_ref, ows_ref):
    def fill(bkt, write, sink_col0):
        col0 = lax.broadcasted_iota(jnp.int32, bkt.shape, 1) == 0
        for st, heads in enumerate(STACK_HEADS):
            for slot, h in enumerate(heads):
                acc = jnp.full(bkt.shape, NEG_INF, F32)
                for b in range(N_BUCKETS):
                    acc = jnp.where(bkt == b, tab_ref[b, h], acc)
                if sink_col0:
                    acc = jnp.where(col0, sink_ref[0, h], acc)
                write(st, slot, acc)

    for var in range(2):
        def wr_p(st, slot, acc, var=var):
            op_ref[var, st, slot * WINDOW:(slot + 1) * WINDOW, :] = acc
        fill(bp_ref[var], wr_p, True)

    rows_s = DEC_SEQ * SAMPLE_GROUP

    def wr_sc(st, slot, acc):
        osc_ref[st, slot * rows_s:(slot + 1) * rows_s, :] = acc
    fill(bsc_ref[...], wr_sc, True)

    def wr_sn(st, slot, acc):
        osn_ref[st, slot * rows_s:(slot + 1) * rows_s, :] = acc
    fill(bsn_ref[...], wr_sn, False)

    r = lax.broadcasted_iota(jnp.int32, (CHUNK, CHUNK), 0)
    c = lax.broadcasted_iota(jnp.int32, (CHUNK, CHUNK), 1)
    for h in range(A_HEADS):
        ows_ref[h // 2, :, (h % 2) * CHUNK:(h % 2 + 1) * CHUNK] = jnp.where(r >= c, ws_ref[h], 0.0).astype(BF16)


def _prep(rel_bias_table, sinks, w_s):
    vm = pl.BlockSpec(memory_space=pltpu.VMEM)
    sm = pl.BlockSpec(memory_space=pltpu.SMEM)
    rows_s = DEC_SEQ * SAMPLE_GROUP
    return pl.pallas_call(
        _prep_kernel,
        in_specs=[sm, sm, vm, vm, vm, vm],
        out_specs=[vm, vm, vm, vm],
        out_shape=[
            jax.ShapeDtypeStruct((2, 2, 4 * WINDOW, 2 * WINDOW), F32),
            jax.ShapeDtypeStruct((2, 4 * rows_s, SAMPLE_GROUP * WINDOW), F32),
            jax.ShapeDtypeStruct((2, 4 * rows_s, LANES), F32),
            jax.ShapeDtypeStruct((A_HEADS // 2, CHUNK, 2 * CHUNK), BF16),
        ],
        name="prep_tables",
    )(rel_bias_table, sinks.reshape(1, B_HEADS), jnp.asarray(_BKT_P), jnp.asarray(_BKT_SC), jnp.asarray(_BKT_SN), w_s)


def _gate_pairs(va_rows, wsp_ref, lane_lo):
    outs = []
    for p in range(A_HEADS // 2):
        vp = va_rows[:, p * LANES:(p + 1) * LANES]
        rhs = jnp.concatenate([jnp.where(lane_lo, vp, 0.0), jnp.where(lane_lo, 0.0, vp)], axis=0).astype(BF16)
        outs.append(_dot(wsp_ref[p], rhs))
    return jnp.concatenate(outs, axis=1)


def _prompt_steps(body, first_row_out):
    def kern(*refs):
        i = pl.program_id(0)

        @pl.when(i < N_PROMPT_BLOCKS)
        def _():
            body(*refs)

        @pl.when(i >= N_PROMPT_BLOCKS)
        def _():
            for r in refs[first_row_out:first_row_out + 5]:
                r[...] = jnp.zeros(r.shape, r.dtype)

    return kern


def _mix0_prompt_kernel(x_ref, nm_ref, win_ref, lng_ref, lnb_ref, wsp_ref, bs_ref, bias_ref,
                        wout_ref, nf_ref, wr_ref, br_ref,
                        x1_ref, h_ref, ri_ref, rg_ref, tc_ref, kl_ref, vl_ref, val_ref, cnt_ref,
                        kprev, vprev, mix_scr):
    @pl.when(pl.program_id(0) == 0)
    def _():
        cnt_ref[...] = jnp.zeros_like(cnt_ref)

    x = x_ref[...]
    u, va, q, k, v = _project(x, nm_ref[...], win_ref[...], lng_ref[...], lnb_ref[...])
    lane_lo = lax.broadcasted_iota(jnp.int32, (WINDOW, LANES), 1) < B_HEAD_DIM
    row0 = lax.broadcasted_iota(jnp.int32, (WINDOW, KV_WIDTH), 0) == 0
    first = pl.program_id(0) % STEPS_PER_BATCH == 0

    @pl.when(first)
    def _():
        kprev[...] = jnp.zeros_like(kprev)
        vprev[...] = jnp.zeros_like(vprev)

    for j in range(SUB):
        rows = slice(j * WINDOW, (j + 1) * WINDOW)
        s_gate = _gate_pairs(va[rows], wsp_ref, lane_lo)
        mix_scr[rows, :A_WIDTH] = u[rows] * (s_gate + bs_ref[...])

        if j == 0:
            kp, vp = kprev[...], vprev[...]
        else:
            prows = slice((j - 1) * WINDOW, j * WINDOW)
            kp, vp = k[prows], v[prows]
        kk = jnp.concatenate([jnp.where(row0, 0.0, kp), k[rows]], axis=0)
        vv = jnp.concatenate([jnp.where(row0, 0.0, vp), v[rows]], axis=0)
        kops = (kk.astype(BF16), pltpu.roll(kk, B_HEAD_DIM, 1).astype(BF16))
        vops = (vv.astype(BF16), pltpu.roll(vv, B_HEAD_DIM, 1).astype(BF16))
        qt = [q[rows, p * LANES:(p + 1) * LANES] for p in range(4)]
        q_even = [jnp.where(lane_lo, t, 0.0) for t in qt]
        q_odd = [jnp.where(lane_lo, 0.0, t) for t in qt]
        stacks = (jnp.concatenate([q_even[0], q_even[1], q_odd[2], q_odd[3]], axis=0),
                  jnp.concatenate([q_odd[0], q_odd[1], q_even[2], q_even[3]], axis=0))
        o = []
        for st in range(2):
            s = _dot_nt(stacks[st].astype(BF16), kops[st])
            if j == 0:
                bias = bias_ref[jnp.where(first, 0, 1), st]
            else:
                bias = bias_ref[1, st]
            s = s + bias
            m = jnp.max(s, axis=-1, keepdims=True)
            p = jnp.exp(s - m)
            den = jnp.sum(p, axis=-1, keepdims=True)
            o.append(_dot(p.astype(BF16), vops[st]) / den)
        oa, ob = o
        sl = [slice(i * WINDOW, (i + 1) * WINDOW) for i in range(4)]
        tiles = (jnp.where(lane_lo, oa[sl[0]], ob[sl[0]]), jnp.where(lane_lo, oa[sl[1]], ob[sl[1]]),
                 jnp.where(lane_lo, ob[sl[2]], oa[sl[2]]), jnp.where(lane_lo, ob[sl[3]], oa[sl[3]]))
        for p in range(4):
            mix_scr[rows, A_WIDTH + p * LANES:A_WIDTH + (p + 1) * LANES] = tiles[p]

    last = slice(TM - WINDOW, TM)
    kprev[...] = k[last]
    vprev[...] = v[last]
    kl_ref[...] = k[last]
    vl_ref[...] = v[last]
    val_ref[...] = va[last]

    x1 = x + _dot(mix_scr[...].astype(BF16), wout_ref[...])
    x1_ref[...] = x1
    h, ids, gates = _route(x1, nf_ref[...], wr_ref[...], br_ref[...])
    h_ref[...] = h.reshape(h_ref.shape)
    ri_ref[...] = _rank_pack(ids, cnt_ref, tc_ref)
    rg_ref[...] = gates


def _const_spec(shape):
    nd = len(shape)
    return pl.BlockSpec(shape, lambda i, _n=nd: (0,) * _n)


def _mix0_prompt(x_all, nm, win, lng, lnb, wsp, bs_full, bias_p, wout, nf, wr, br):
    row_spec = pl.BlockSpec((TM, D_MODEL), lambda i: (i, 0))
    row3_spec = pl.BlockSpec((TM, ROW_TILE, LANES), lambda i: (i, 0, 0))
    lane_spec = pl.BlockSpec((TM, LANES), lambda i: (i, 0))
    last_kv = pl.BlockSpec((None, WINDOW, KV_WIDTH), lambda i: (jnp.minimum(i // STEPS_PER_BATCH, BATCH - 1), 0, 0))
    last_va = pl.BlockSpec((None, WINDOW, A_WIDTH), lambda i: (jnp.minimum(i // STEPS_PER_BATCH, BATCH - 1), 0, 0))
    return pl.pallas_call(
        _prompt_steps(_mix0_prompt_kernel, 12),
        grid=(N_ROW_BLOCKS,),
        in_specs=[pl.BlockSpec((TM, D_MODEL), lambda i: (jnp.minimum(i, N_PROMPT_BLOCKS - 1), 0)),
                  _const_spec((1, D_MODEL)), _const_spec((D_MODEL, IN_WIDTH)),
                  _const_spec((1, A_WIDTH)), _const_spec((1, A_WIDTH)),
                  _const_spec((A_HEADS // 2, CHUNK, 2 * CHUNK)), _const_spec((CHUNK, A_WIDTH)),
                  _const_spec((2, 2, 4 * WINDOW, 2 * WINDOW)),
                  _const_spec((A_WIDTH + Q_WIDTH, D_MODEL)), _const_spec((1, D_MODEL)),
                  _const_spec((D_MODEL, 2 * LANES)), _const_spec((1, LANES))],
        out_specs=[row_spec, row3_spec, pl.BlockSpec((8, TM), lambda i: (0, i)), lane_spec,
                   pl.BlockSpec((None, 1, LANES), lambda i: (i, 0, 0)),
                   last_kv, last_kv, last_va, _const_spec((1, LANES))],
        out_shape=[jax.ShapeDtypeStruct((T_ALL, D_MODEL), F32), jax.ShapeDtypeStruct((T_ALL, ROW_TILE, LANES), BF16),
                   jax.ShapeDtypeStruct((8, T_ALL), jnp.int32), jax.ShapeDtypeStruct((T_ALL, LANES), F32),
                   jax.ShapeDtypeStruct((N_ROW_BLOCKS, 1, LANES), F32),
                   jax.ShapeDtypeStruct((BATCH, WINDOW, KV_WIDTH), F32),
                   jax.ShapeDtypeStruct((BATCH, WINDOW, KV_WIDTH), F32),
                   jax.ShapeDtypeStruct((BATCH, WINDOW, A_WIDTH), F32),
                   jax.ShapeDtypeStruct((1, LANES), F32)],
        scratch_shapes=[pltpu.VMEM((WINDOW, KV_WIDTH), F32), pltpu.VMEM((WINDOW, KV_WIDTH), F32),
                        pltpu.VMEM((TM, D_MODEL), F32)],
        compiler_params=_cparams(("arbitrary",)),
        name="mix0_prompt",
    )(x_all, nm, win, lng, lnb, wsp, bs_full, bias_p, wout, nf, wr, br)


def _mix0_sample_kernel(x_ref, nm_ref, win_ref, lng_ref, lnb_ref, wcoef_ref, bcoef_ref,
                        ck_ref, cv_ref, bsc_ref, bsn_ref,
                        wout_ref, nf_ref, wr_ref, br_ref, cnt_in,
                        x1_in, h_in, ri_in, rg_in, tc_in,
                        x1_ref, h_ref, ri_ref, rg_ref, tc_ref, kn_ref, vn_ref, va_ref, cnt_ref,
                        q_scr, k_scr, v_scr, mix_scr):
    del x1_in, h_in, ri_in, rg_in, tc_in
    g = pl.program_id(0)

    @pl.when(g == 0)
    def _():
        u, va, q, k, v = _project(x_ref[...], nm_ref[...], win_ref[...], lng_ref[...], lnb_ref[...])
        q_scr[...] = q
        k_scr[...] = k
        v_scr[...] = v
        kn_ref[...] = k
        vn_ref[...] = v
        va_ref[...] = va
        for t in range(DEC_SEQ):
            acc = jnp.zeros((DEC_BATCH, A_WIDTH), F32) + bcoef_ref[t:t + 1, :]
            for s in range(t + 1):
                row = t * DEC_SEQ + s
                acc = acc + wcoef_ref[row:row + 1, :] * va[s * DEC_BATCH:(s + 1) * DEC_BATCH]
            mix_scr[t * DEC_BATCH:(t + 1) * DEC_BATCH, :A_WIDTH] = u[t * DEC_BATCH:(t + 1) * DEC_BATCH] * acc

    b0 = pl.multiple_of(g * SAMPLE_GROUP, SAMPLE_GROUP)
    lane_lo = lax.broadcasted_iota(jnp.int32, (DEC_SEQ * SAMPLE_GROUP, LANES), 1) < B_HEAD_DIM

    def grab(ref, width):
        return jnp.concatenate([ref[pl.ds(t * DEC_BATCH + b0, SAMPLE_GROUP), :] for t in range(DEC_SEQ)], axis=0)

    qg = grab(q_scr, Q_WIDTH)
    kn = grab(k_scr, KV_WIDTH)
    vn = grab(v_scr, KV_WIDTH)
    ccol0 = lax.broadcasted_iota(jnp.int32, (KV_WIDTH, SAMPLE_GROUP * WINDOW), 1) == 0

    def cache_t(ref):
        t = jnp.concatenate([ref[b].reshape(KV_WIDTH, WINDOW) for b in range(SAMPLE_GROUP)], axis=1)
        return jnp.where(ccol0, 0.0, t)

    def head_swap(t):
        return jnp.concatenate([t[B_HEAD_DIM:], t[:B_HEAD_DIM]], axis=0)

    kct = cache_t(ck_ref)
    vct = cache_t(cv_ref)
    kc_ops = (kct.astype(BF16), head_swap(kct).astype(BF16))
    vc_ops = (vct.astype(BF16), head_swap(vct).astype(BF16))
    kn_ops = (kn.astype(BF16), pltpu.roll(kn, B_HEAD_DIM, 1).astype(BF16))
    vn_ops = (vn.astype(BF16), pltpu.roll(vn, B_HEAD_DIM, 1).astype(BF16))
    qt = [qg[:, p * LANES:(p + 1) * LANES] for p in range(4)]
    q_even = [jnp.where(lane_lo, t, 0.0) for t in qt]
    q_odd = [jnp.where(lane_lo, 0.0, t) for t in qt]
    stacks = (jnp.concatenate([q_even[0], q_even[1], q_odd[2], q_odd[3]], axis=0),
              jnp.concatenate([q_odd[0], q_odd[1], q_even[2], q_even[3]], axis=0))
    o = []
    for st in range(2):
        qs = stacks[st].astype(BF16)
        sc = _dot(qs, kc_ops[st]) + bsc_ref[st]
        sn = _dot_nt(qs, kn_ops[st]) + bsn_ref[st][:, :DEC_SEQ * SAMPLE_GROUP]
        m = jnp.maximum(jnp.max(sc, axis=-1, keepdims=True), jnp.max(sn, axis=-1, keepdims=True))
        pc = jnp.exp(sc - m)
        pn = jnp.exp(sn - m)
        den = jnp.sum(pc, axis=-1, keepdims=True) + jnp.sum(pn, axis=-1, keepdims=True)
        o.append((_dot_nt(pc.astype(BF16), vc_ops[st]) + _dot(pn.astype(BF16), vn_ops[st])) / den)
    oa, ob = o
    n = DEC_SEQ * SAMPLE_GROUP
    sl = [slice(i * n, (i + 1) * n) for i in range(4)]
    tiles = (jnp.where(lane_lo, oa[sl[0]], ob[sl[0]]), jnp.where(lane_lo, oa[sl[1]], ob[sl[1]]),
             jnp.where(lane_lo, ob[sl[2]], oa[sl[2]]), jnp.where(lane_lo, ob[sl[3]], oa[sl[3]]))
    for p in range(4):
        for t in range(DEC_SEQ):
            mix_scr[pl.ds(t * DEC_BATCH + b0, SAMPLE_GROUP), A_WIDTH + p * LANES:A_WIDTH + (p + 1) * LANES] = (
                tiles[p][t * SAMPLE_GROUP:(t + 1) * SAMPLE_GROUP])

    @pl.when(g == N_SAMPLE_GROUPS - 1)
    def _():
        x1 = x_ref[...] + _dot(mix_scr[...].astype(BF16), wout_ref[...])
        x1_ref[...] = x1
        h, ids, gates = _route(x1, nf_ref[...], wr_ref[...], br_ref[...])
        h_ref[...] = h.reshape(h_ref.shape)
        cnt_ref[...] = cnt_in[...]
        ri_ref[...] = _rank_pack(ids, cnt_ref, tc_ref)
        rg_ref[...] = gates


def _mix0_sample(x_all, nm, win, lng, lnb, wcoef, bcoef, ck, cv, bias_sc, bias_sn, wout, nf, wr, br, cnt,
                 x1_all, h_all, ri_all, rg_all, tc_all):
    sample_rows = pl.BlockSpec((TM, D_MODEL), lambda g: (N_PROMPT_BLOCKS, 0))
    sample_rows3 = pl.BlockSpec((TM, ROW_TILE, LANES), lambda g: (N_PROMPT_BLOCKS, 0, 0))
    sample_lanes = pl.BlockSpec((TM, LANES), lambda g: (N_PROMPT_BLOCKS, 0))
    cache_spec = pl.BlockSpec((SAMPLE_GROUP, B_KV_HEADS, B_HEAD_DIM, WINDOW), lambda g: (g, 0, 0, 0))
    anyspec = pl.BlockSpec(memory_space=pl.ANY)
    n_in = 16
    return pl.pallas_call(
        _mix0_sample_kernel,
        grid=(N_SAMPLE_GROUPS,),
        in_specs=[_const_spec((TM, D_MODEL)), _const_spec((1, D_MODEL)), _const_spec((D_MODEL, IN_WIDTH)),
                  _const_spec((1, A_WIDTH)), _const_spec((1, A_WIDTH)),
                  _const_spec((16, A_WIDTH)), _const_spec((8, A_WIDTH)),
                  cache_spec, cache_spec,
                  _const_spec((2, 4 * 32, SAMPLE_GROUP * WINDOW)), _const_spec((2, 4 * 32, LANES)),
                  _const_spec((A_WIDTH + Q_WIDTH, D_MODEL)), _const_spec((1, D_MODEL)),
                  _const_spec((D_MODEL, 2 * LANES)), _const_spec((1, LANES)), _const_spec((1, LANES)),
                  anyspec, anyspec, anyspec, anyspec, anyspec],
        out_specs=[sample_rows, sample_rows3, pl.BlockSpec((8, TM), lambda g: (0, N_PROMPT_BLOCKS)), sample_lanes,
                   pl.BlockSpec((None, 1, LANES), lambda g: (N_PROMPT_BLOCKS, 0, 0)),
                   _const_spec((T_SAMPLE, KV_WIDTH)), _const_spec((T_SAMPLE, KV_WIDTH)),
                   _const_spec((T_SAMPLE, A_WIDTH)), _const_spec((1, LANES))],
        out_shape=[jax.ShapeDtypeStruct((T_ALL, D_MODEL), F32), jax.ShapeDtypeStruct((T_ALL, ROW_TILE, LANES), BF16),
                   jax.ShapeDtypeStruct((8, T_ALL), jnp.int32), jax.ShapeDtypeStruct((T_ALL, LANES), F32),
                   jax.ShapeDtypeStruct((N_ROW_BLOCKS, 1, LANES), F32),
                   jax.ShapeDtypeStruct((T_SAMPLE, KV_WIDTH), F32), jax.ShapeDtypeStruct((T_SAMPLE, KV_WIDTH), F32),
                   jax.ShapeDtypeStruct((T_SAMPLE, A_WIDTH), F32), jax.ShapeDtypeStruct((1, LANES), F32)],
        scratch_shapes=[pltpu.VMEM((T_SAMPLE, Q_WIDTH), F32), pltpu.VMEM((T_SAMPLE, KV_WIDTH), F32),
                        pltpu.VMEM((T_SAMPLE, KV_WIDTH), F32), pltpu.VMEM((T_SAMPLE, D_MODEL), F32)],
        input_output_aliases={n_in: 0, n_in + 1: 1, n_in + 2: 2, n_in + 3: 3, n_in + 4: 4},
        compiler_params=_cparams(("arbitrary",)),
        name="mix0_sample",
    )(x_all, nm, win, lng, lnb, wcoef, bcoef, ck, cv, bias_sc, bias_sn, wout, nf, wr, br, cnt,
      x1_all, h_all, ri_all, rg_all, tc_all)


def _moe_metadata(rt_all, cnt, tcnt):
    counts = cnt[0, :N_EXPERTS].astype(jnp.int32)
    padded = (counts + MOE_BLK - 1) // MOE_BLK * MOE_BLK
    pad_end = jnp.cumsum(padded)
    pad_start = pad_end - padded
    experts = jnp.arange(N_EXPERTS, dtype=jnp.int32)
    n_valid = (pad_end[-1] // MOE_BLK).astype(jnp.int32).reshape(1)
    blk_start = jnp.arange(N_MOE_BLOCKS, dtype=jnp.int32) * MOE_BLK
    block_e = jnp.minimum(jnp.sum((blk_start[:, None] >= pad_end[None, :]).astype(jnp.int32), axis=1),
                          N_EXPERTS - 1).astype(jnp.int32)
    zero_start = (pad_start + counts).astype(jnp.int32)
    zero_len = (padded - counts).astype(jnp.int32)
    first = (blk_start == pad_start[block_e]).astype(jnp.int32)
    used = counts > 0
    parity = ((jnp.cumsum(used.astype(jnp.int32)) - 1) % 2)[block_e].astype(jnp.int32)
    nearest = lax.cummin(jnp.where(used, experts, N_EXPERTS)[::-1])[::-1]
    next_used = jnp.concatenate([nearest[1:], jnp.full((1,), N_EXPERTS, jnp.int32)])
    nxt = jnp.where(next_used < N_EXPERTS, next_used, -1)[block_e].astype(jnp.int32)
    plan = (block_e, first, parity, nxt, n_valid)
    runs = tcnt[:, 0, :N_EXPERTS].astype(jnp.int32)
    run_dst = pad_start[None, :] + jnp.cumsum(runs, axis=0) - runs
    lpos = rt_all[2 * TOP_K:3 * TOP_K].reshape(N_SLOTS).astype(jnp.int32)
    cplan = (lpos, runs.reshape(-1), run_dst.reshape(-1).astype(jnp.int32))
    dplan = cplan + (jnp.concatenate([zero_start, zero_len, n_valid]),)
    return plan, dplan, cplan


RUN_PIECE = 32


def _for_run_pieces(n, start_piece):
    whole = n // RUN_PIECE

    def body(j, carry):
        start_piece(j * RUN_PIECE, RUN_PIECE)
        return carry

    lax.fori_loop(0, whole, body, 0)
    o = whole * RUN_PIECE
    bit = RUN_PIECE // 2
    while bit >= 1:
        take = (n & bit) != 0

        @pl.when(take)
        def _(o=o, bit=bit):
            start_piece(o, bit)

        o = o + jnp.where(take, bit, 0)
        bit //= 2


def _dispatch_kernel(lpos_ref, run_ref, rdst_ref, zs_ref, h_ref, xs_ref, zero_scr, stage, sem, zsem):
    i = pl.program_id(0)

    @pl.when(i == 0)
    def _():
        zero_scr[...] = jnp.zeros_like(zero_scr)

        def pieces(e, do):
            off = zs_ref[e]
            rem = zs_ref[N_EXPERTS + e]
            bit = MOE_BLK // 2
            while bit >= 1:
                take = (rem & bit) != 0

                @pl.when(take)
                def _(off=off, bit=bit):
                    do(pltpu.make_async_copy(zero_scr.at[pl.ds(0, bit)], xs_ref.at[pl.ds(off, bit)], zsem))

                off = off + jnp.where(take, bit, 0)
                bit //= 2

        def start_e(e, c):
            pieces(e, lambda cp: cp.start())
            return c

        def wait_e(e, c):
            pieces(e, lambda cp: cp.wait())
            return c

        def tail(do):
            def step(b, c):
                do(pltpu.make_async_copy(zero_scr, xs_ref.at[pl.ds(b * MOE_BLK, MOE_BLK)], zsem))
                return c
            return step

        n_valid = zs_ref[2 * N_EXPERTS]
        lax.fori_loop(0, N_EXPERTS, start_e, 0)
        lax.fori_loop(n_valid, N_MOE_BLOCKS, tail(lambda cp: cp.start()), 0)
        lax.fori_loop(0, N_EXPERTS, wait_e, 0)
        lax.fori_loop(n_valid, N_MOE_BLOCKS, tail(lambda cp: cp.wait()), 0)

    base = i * TM

    def place(r, carry):
        row = h_ref[r]
        for kk in range(TOP_K):
            stage[lpos_ref[kk * T_ALL + base + r]] = row
        return carry

    lax.fori_loop(0, TM, place, 0, unroll=8)

    def send_run(e, off):
        n = run_ref[i * N_EXPERTS + e]
        dst = rdst_ref[i * N_EXPERTS + e]
        _for_run_pieces(n, lambda o, size: pltpu.make_async_copy(
            stage.at[pl.ds(off + o, size)], xs_ref.at[pl.ds(dst + o, size)], sem).start(
                priority=size.bit_length() % 2))
        return off + n

    lax.fori_loop(0, N_EXPERTS, send_run, 0)
    pltpu.make_async_copy(stage, xs_ref.at[pl.ds(0, TM * TOP_K)], sem).wait()


def _dispatch(dplan, h_all):
    return pl.pallas_call(
        _dispatch_kernel,
        grid_spec=pltpu.PrefetchScalarGridSpec(
            num_scalar_prefetch=4,
            grid=(N_ROW_BLOCKS,),
            in_specs=[pl.BlockSpec((TM, ROW_TILE, LANES), lambda i, lp, rn, rd, z: (i, 0, 0))],
            out_specs=pl.BlockSpec(memory_space=pl.ANY),
            scratch_shapes=[pltpu.VMEM((MOE_BLK, ROW_TILE, LANES), BF16),
                            pltpu.VMEM((TM * TOP_K, ROW_TILE, LANES), BF16),
                            pltpu.SemaphoreType.DMA(()), pltpu.SemaphoreType.DMA(())],
        ),
        out_shape=jax.ShapeDtypeStruct((N_SORT_ROWS, ROW_TILE, LANES), BF16),
        compiler_params=_cparams(("arbitrary",)),
        name="moe_dispatch",
    )(*dplan, h_all)


def _experts_kernel(layer, be_ref, first_ref, par_ref, nxt_ref, nv_ref,
                    x_ref, wg_hbm, wu_hbm, wd_hbm, y_ref,
                    wg_s, wu_s, wd_s, wg_f, wu_f, wd_f, wsem):
    i = pl.program_id(0)

    def fetch(e, slot):
        return (pltpu.make_async_copy(wg_hbm.at[layer, e], wg_f.at[slot], wsem.at[slot]),
                pltpu.make_async_copy(wu_hbm.at[layer, e], wu_f.at[slot], wsem.at[slot]),
                pltpu.make_async_copy(wd_hbm.at[layer, e], wd_f.at[slot], wsem.at[slot]))

    @pl.when(i < nv_ref[0])
    def _():
        e = be_ref[i]
        slot = par_ref[i]

        @pl.when(i == 0)
        def _():
            for cp in fetch(e, slot):
                cp.start()

        @pl.when(first_ref[i] == 1)
        def _():
            for cp in fetch(e, slot):
                cp.wait()
            wg_s[...] = wg_f[slot].astype(BF16)
            wu_s[...] = wu_f[slot].astype(BF16)
            wd_s[...] = wd_f[slot].astype(BF16)
            nxt = nxt_ref[i]

            @pl.when(nxt >= 0)
            def _():
                for cp in fetch(nxt, 1 - slot):
                    cp.start()

        xb = x_ref[...].reshape(MOE_BLK, D_MODEL)
        a = jax.nn.silu(_dot(xb, wg_s[...])) * _dot(xb, wu_s[...])
        y_ref[...] = _dot(a.astype(BF16), wd_s[...]).reshape(y_ref.shape)

    @pl.when(i >= nv_ref[0])
    def _():
        y_ref[...] = jnp.zeros(y_ref.shape, y_ref.dtype)


def _experts(block_e, first, parity, nxt, n_valid, xs, w_gate, w_up, w_down, layer):
    def blk(i, be, fi, pa, nx, nv):
        return (jnp.maximum(jnp.minimum(i, nv[0] - 1), 0), 0, 0)

    anyspec = pl.BlockSpec(memory_space=pl.ANY)
    return pl.pallas_call(
        functools.partial(_experts_kernel, layer),
        grid_spec=pltpu.PrefetchScalarGridSpec(
            num_scalar_prefetch=5,
            grid=(N_MOE_BLOCKS,),
            in_specs=[pl.BlockSpec((MOE_BLK, ROW_TILE, LANES), blk), anyspec, anyspec, anyspec],
            out_specs=pl.BlockSpec((MOE_BLK, ROW_TILE, LANES), lambda i, be, fi, pa, nx, nv: (i, 0, 0)),
            scratch_shapes=[pltpu.VMEM((D_MODEL, D_EXPERT), BF16), pltpu.VMEM((D_MODEL, D_EXPERT), BF16),
                            pltpu.VMEM((D_EXPERT, D_MODEL), BF16),
                            pltpu.VMEM((2, D_MODEL, D_EXPERT), F32), pltpu.VMEM((2, D_MODEL, D_EXPERT), F32),
                            pltpu.VMEM((2, D_EXPERT, D_MODEL), F32), pltpu.SemaphoreType.DMA((2,))],
        ),
        out_shape=jax.ShapeDtypeStruct((N_SORT_ROWS, ROW_TILE, LANES), F32),
        compiler_params=_cparams(("arbitrary",)),
        name="moe_experts",
    )(block_e, first, parity, nxt, n_valid, xs, w_gate, w_up, w_down)


def _gather_rows(lpos_ref, run_ref, rdst_ref, ys_ref, ystage, ybuf, sem, i):
    def fetch(tile, buf):
        def fetch_run(e, off):
            n = run_ref[tile * N_EXPERTS + e]
            src = rdst_ref[tile * N_EXPERTS + e]
            _for_run_pieces(n, lambda o, size: pltpu.make_async_copy(
                ys_ref.at[pl.ds(src + o, size)], ystage.at[buf, pl.ds(off + o, size)], sem.at[buf]).start(
                    priority=size.bit_length() % 2))
            return off + n

        lax.fori_loop(0, N_EXPERTS, fetch_run, 0)

    buf = i % 2

    @pl.when(i == 0)
    def _():
        fetch(i, buf)

    @pl.when(i + 1 < N_ROW_BLOCKS)
    def _():
        fetch(i + 1, 1 - buf)

    pltpu.make_async_copy(ys_ref.at[pl.ds(0, TM * TOP_K)], ystage.at[buf], sem.at[buf]).wait()
    base = i * TM

    def unplace(r, carry):
        for kk in range(TOP_K):
            ybuf[kk, r] = ystage[buf, lpos_ref[kk * T_ALL + base + r]]
        return carry

    lax.fori_loop(0, TM, unplace, 0, unroll=8)


def _combined(x_ref, rg_ref, ybuf):
    rg = rg_ref[...]
    y0 = ybuf[0].reshape(TM, D_MODEL)
    y1 = ybuf[1].reshape(TM, D_MODEL)
    return x_ref[...] + rg[:, 0:1] * y0 + rg[:, 1:2] * y1


_COMBINE_SCRATCH = [pltpu.VMEM((2, TM * TOP_K, ROW_TILE, LANES), F32), pltpu.VMEM((TOP_K, TM, ROW_TILE, LANES), F32),
                    pltpu.SemaphoreType.DMA((2,))]


def _combine_kernel(lpos_ref, run_ref, rdst_ref, x_ref, rg_ref, ys_ref, o_ref, ystage, ybuf, sem):
    _gather_rows(lpos_ref, run_ref, rdst_ref, ys_ref, ystage, ybuf, sem, pl.program_id(0))
    o_ref[...] = _combined(x_ref, rg_ref, ybuf)


def _combine(cplan, x_all, rg_all, ys):
    return pl.pallas_call(
        _combine_kernel,
        grid_spec=pltpu.PrefetchScalarGridSpec(
            num_scalar_prefetch=3,
            grid=(N_ROW_BLOCKS,),
            in_specs=[pl.BlockSpec((TM, D_MODEL), lambda i, a, b, c: (i, 0)),
                      pl.BlockSpec((TM, LANES), lambda i, a, b, c: (i, 0)),
                      pl.BlockSpec(memory_space=pl.ANY)],
            out_specs=pl.BlockSpec((TM, D_MODEL), lambda i, a, b, c: (i, 0)),
            scratch_shapes=_COMBINE_SCRATCH,
        ),
        out_shape=jax.ShapeDtypeStruct((T_ALL, D_MODEL), F32),
        compiler_params=_cparams(("arbitrary",)),
        name="moe_combine",
    )(*cplan, x_all, rg_all, ys)


def _final_kernel(lpos_ref, run_ref, rdst_ref, x_ref, rg_ref, ys_ref, nfin_ref, op_ref, os_ref, ystage, ybuf, sem):
    i = pl.program_id(0)
    _gather_rows(lpos_ref, run_ref, rdst_ref, ys_ref, ystage, ybuf, sem, i)
    y = _rms(_combined(x_ref, rg_ref, ybuf), nfin_ref[...])

    @pl.when(i < N_PROMPT_BLOCKS)
    def _():
        op_ref[...] = y

    @pl.when(i >= N_PROMPT_BLOCKS)
    def _():
        os_ref[...] = y


def _final(cplan, x_all, rg_all, ys, nfin):
    return pl.pallas_call(
        _final_kernel,
        grid_spec=pltpu.PrefetchScalarGridSpec(
            num_scalar_prefetch=3,
            grid=(N_ROW_BLOCKS,),
            in_specs=[pl.BlockSpec((TM, D_MODEL), lambda i, a, b, c: (i, 0)),
                      pl.BlockSpec((TM, LANES), lambda i, a, b, c: (i, 0)),
                      pl.BlockSpec(memory_space=pl.ANY),
                      pl.BlockSpec((1, D_MODEL), lambda i, a, b, c: (0, 0))],
            out_specs=[pl.BlockSpec((TM, D_MODEL), lambda i, a, b, c: (jnp.minimum(i, N_PROMPT_BLOCKS - 1), 0)),
                       pl.BlockSpec((TM, D_MODEL), lambda i, a, b, c: (0, 0))],
            scratch_shapes=_COMBINE_SCRATCH,
        ),
        out_shape=[jax.ShapeDtypeStruct((T_PROMPT, D_MODEL), F32), jax.ShapeDtypeStruct((T_SAMPLE, D_MODEL), F32)],
        compiler_params=_cparams(("arbitrary",)),
        name="moe_combine_final",
    )(*cplan, x_all, rg_all, ys, nfin)


def _moe(h_all, rt_all, cnt, tcnt, w_gate, w_up, w_down, layer):
    plan, dplan, cplan = _moe_metadata(rt_all, cnt, tcnt)
    xs = _dispatch(dplan, h_all)
    ys = _experts(*plan, xs, w_gate, w_up, w_down, layer)
    return cplan, ys


def _pool_project(d_groups, wp_ref, scale):
    outs = [_dot(d_groups[g].astype(BF16), wp_ref[g]) for g in range(len(POOL_SIZES))]
    return jnp.concatenate(outs, axis=1) * scale


def _mix1_prompt_kernel(x_ref, nm_ref, wp_ref, sc_ref, nf_ref, wr_ref, br_ref,
                        x3_ref, h_ref, ri_ref, rg_ref, tc_ref, pl_ref, cnt_ref, ext):
    i = pl.program_id(0)

    @pl.when(i == 0)
    def _():
        cnt_ref[...] = jnp.zeros_like(cnt_ref)

    x = x_ref[...]
    hp = _rms(x, nm_ref[...])

    @pl.when(i % STEPS_PER_BATCH == 0)
    def _():
        ext[0:POOL_MAX, :] = jnp.zeros((POOL_MAX, D_MODEL), F32)

    ext[POOL_MAX:, :] = hp
    pos = (i % STEPS_PER_BATCH) * TM + lax.broadcasted_iota(jnp.int32, (TM, 1), 0)
    d_groups = []
    for g, w in enumerate(POOL_SIZES):
        cols = slice(g * POOL_GROUP_DIM, (g + 1) * POOL_GROUP_DIM)
        acc = ext[:, cols]
        span = 1
        while span < w:
            acc = acc + pltpu.roll(acc, span, 0)
            span *= 2
        cnt = jnp.minimum(pos + 1, w).astype(F32)
        d_groups.append(acc[POOL_MAX:] / cnt - hp[:, cols])
    tail = hp[TM - POOL_MAX:, :]
    ext[0:POOL_MAX, :] = tail
    pl_ref[...] = tail

    x3 = x + _pool_project(d_groups, wp_ref, sc_ref[...])
    x3_ref[...] = x3
    h, ids, gates = _route(x3, nf_ref[...], wr_ref[...], br_ref[...])
    h_ref[...] = h.reshape(h_ref.shape)
    ri_ref[...] = _rank_pack(ids, cnt_ref, tc_ref)
    rg_ref[...] = gates


def _mix1_prompt(x_all, nm, wp, sc, nf, wr, br):
    row_spec = pl.BlockSpec((TM, D_MODEL), lambda i: (i, 0))
    row3_spec = pl.BlockSpec((TM, ROW_TILE, LANES), lambda i: (i, 0, 0))
    lane_spec = pl.BlockSpec((TM, LANES), lambda i: (i, 0))
    return pl.pallas_call(
        _prompt_steps(_mix1_prompt_kernel, 7),
        grid=(N_ROW_BLOCKS,),
        in_specs=[row_spec, _const_spec((1, D_MODEL)),
                  _const_spec((len(POOL_SIZES), POOL_GROUP_DIM, POOL_GROUP_DIM)), _const_spec((1, D_MODEL)),
                  _const_spec((1, D_MODEL)), _const_spec((D_MODEL, 2 * LANES)), _const_spec((1, LANES))],
        out_specs=[row_spec, row3_spec, pl.BlockSpec((8, TM), lambda i: (0, i)), lane_spec,
                   pl.BlockSpec((None, 1, LANES), lambda i: (i, 0, 0)),
                   pl.BlockSpec((None, POOL_MAX, D_MODEL),
                                lambda i: (jnp.minimum(i // STEPS_PER_BATCH, BATCH - 1), 0, 0)),
                   _const_spec((1, LANES))],
        out_shape=[jax.ShapeDtypeStruct((T_ALL, D_MODEL), F32), jax.ShapeDtypeStruct((T_ALL, ROW_TILE, LANES), BF16),
                   jax.ShapeDtypeStruct((8, T_ALL), jnp.int32), jax.ShapeDtypeStruct((T_ALL, LANES), F32),
                   jax.ShapeDtypeStruct((N_ROW_BLOCKS, 1, LANES), F32),
                   jax.ShapeDtypeStruct((BATCH, POOL_MAX, D_MODEL), F32), jax.ShapeDtypeStruct((1, LANES), F32)],
        scratch_shapes=[pltpu.VMEM((POOL_MAX + TM, D_MODEL), F32)],
        compiler_params=_cparams(("arbitrary",)),
        name="mix1_prompt",
    )(x_all, nm, wp, sc, nf, wr, br)


def _mix1_sample_kernel(x_ref, st_ref, nm_ref, wp_ref, sc_ref, nf_ref, wr_ref, br_ref, cnt_in,
                        x3_in, h_in, ri_in, rg_in, tc_in,
                        x3_ref, h_ref, ri_ref, rg_ref, tc_ref, hs_ref, cnt_ref):
    del x3_in, h_in, ri_in, rg_in, tc_in
    x = x_ref[...]
    hs = _rms(x, nm_ref[...])
    hs_ref[...] = hs
    n_ctx = POOL_MAX - 1
    d_groups = []
    for g, w in enumerate(POOL_SIZES):
        cols = slice(g * POOL_GROUP_DIM, (g + 1) * POOL_GROUP_DIM)
        parts = []
        for t in range(DEC_SEQ):
            acc = hs[t * DEC_BATCH:(t + 1) * DEC_BATCH, cols]
            for back in range(1, w):
                src = t - back
                if src >= 0:
                    acc = acc + hs[src * DEC_BATCH:(src + 1) * DEC_BATCH, cols]
                else:
                    acc = acc + st_ref[n_ctx + src, :, cols]
            parts.append(acc / float(w) - hs[t * DEC_BATCH:(t + 1) * DEC_BATCH, cols])
        d_groups.append(jnp.concatenate(parts, axis=0))
    x3 = x + _pool_project(d_groups, wp_ref, sc_ref[...])
    x3_ref[...] = x3
    h, ids, gates = _route(x3, nf_ref[...], wr_ref[...], br_ref[...])
    h_ref[...] = h.reshape(h_ref.shape)
    cnt_ref[...] = cnt_in[...]
    ri_ref[...] = _rank_pack(ids, cnt_ref, tc_ref)
    rg_ref[...] = gates


def _mix1_sample(x_all, state_t, nm, wp, sc, nf, wr, br, cnt, x3_all, h_all, ri_all, rg_all, tc_all):
    sample_rows = pl.BlockSpec((TM, D_MODEL), lambda g: (N_PROMPT_BLOCKS, 0))
    sample_rows3 = pl.BlockSpec((TM, ROW_TILE, LANES), lambda g: (N_PROMPT_BLOCKS, 0, 0))
    sample_lanes = pl.BlockSpec((TM, LANES), lambda g: (N_PROMPT_BLOCKS, 0))
    anyspec = pl.BlockSpec(memory_space=pl.ANY)
    n_in = 9
    return pl.pallas_call(
        _mix1_sample_kernel,
        grid=(1,),
        in_specs=[sample_rows, _const_spec((POOL_MAX - 1, DEC_BATCH, D_MODEL)), _const_spec((1, D_MODEL)),
                  _const_spec((len(POOL_SIZES), POOL_GROUP_DIM, POOL_GROUP_DIM)), _const_spec((1, D_MODEL)),
                  _const_spec((1, D_MODEL)), _const_spec((D_MODEL, 2 * LANES)), _const_spec((1, LANES)),
                  _const_spec((1, LANES)), anyspec, anyspec, anyspec, anyspec, anyspec],
        out_specs=[sample_rows, sample_rows3, pl.BlockSpec((8, TM), lambda g: (0, N_PROMPT_BLOCKS)), sample_lanes,
                   pl.BlockSpec((None, 1, LANES), lambda g: (N_PROMPT_BLOCKS, 0, 0)),
                   _const_spec((T_SAMPLE, D_MODEL)), _const_spec((1, LANES))],
        out_shape=[jax.ShapeDtypeStruct((T_ALL, D_MODEL), F32), jax.ShapeDtypeStruct((T_ALL, ROW_TILE, LANES), BF16),
                   jax.ShapeDtypeStruct((8, T_ALL), jnp.int32), jax.ShapeDtypeStruct((T_ALL, LANES), F32),
                   jax.ShapeDtypeStruct((N_ROW_BLOCKS, 1, LANES), F32),
                   jax.ShapeDtypeStruct((T_SAMPLE, D_MODEL), F32), jax.ShapeDtypeStruct((1, LANES), F32)],
        input_output_aliases={n_in: 0, n_in + 1: 1, n_in + 2: 2, n_in + 3: 3, n_in + 4: 4},
        compiler_params=_cparams(("arbitrary",)),
        name="mix1_sample",
    )(x_all, state_t, nm, wp, sc, nf, wr, br, cnt, x3_all, h_all, ri_all, rg_all, tc_all)


def _router_weights(wg, bg, we, be):
    w = jnp.concatenate([wg, jnp.transpose(we, (1, 0, 2)).reshape(D_MODEL, N_EXPERTS)], axis=1)
    b = jnp.concatenate([bg, be.reshape(N_EXPERTS)])
    pad = LANES - N_GROUPS - N_EXPERTS
    w = jnp.pad(w, ((0, 0), (0, pad)))
    w_hi = w.astype(BF16)
    w_lo = (w - w_hi.astype(F32)).astype(BF16)
    return jnp.concatenate([w_hi, w_lo], axis=1), jnp.pad(b, (0, pad)).reshape(1, LANES)


def kernel(x_prompt, x_sample, cache_k_win, cache_v_win, state_pool, norm_mix, norm_ffn, norm_final, w_in,
           a_ln_g, a_ln_b, a_w_s, a_b_s, b_sinks, rel_bias_table, w_out, c_w_pool, c_scale,
           router_group_w, router_group_b, router_expert_w, router_expert_b, w_gate, w_up, w_down):
    xs_t = jnp.transpose(x_sample, (1, 0, 2)).reshape(T_SAMPLE, D_MODEL)
    xp2 = x_prompt.reshape(T_PROMPT, D_MODEL)
    win =w_in[0].astype(BF16)
    wout = w_out[0].astype(BF16)
    lng = a_ln_g[0].reshape(1, A_WIDTH)
    lnb = a_ln_b[0].reshape(1, A_WIDTH)
    bias_p, bias_sc, bias_sn, wsp = _prep(rel_bias_table, b_sinks[0], a_w_s[0])
    bs_full = jnp.repeat(a_b_s[0].T, A_HEAD_DIM, axis=1)
    w4 = jnp.transpose(a_w_s[0][:, :DEC_SEQ, :DEC_SEQ], (1, 2, 0)).reshape(DEC_SEQ * DEC_SEQ, A_HEADS)
    wcoef = jnp.repeat(w4, A_HEAD_DIM, axis=1)
    bcoef = jnp.pad(jnp.repeat(a_b_s[0][:, :DEC_SEQ].T, A_HEAD_DIM, axis=1), ((0, 8 - DEC_SEQ), (0, 0)))
    ck = jnp.transpose(cache_k_win[0], (0, 2, 3, 1))
    cv = jnp.transpose(cache_v_win[0], (0, 2, 3, 1))
    routers = [_router_weights(router_group_w[l], router_group_b[l], router_expert_w[l], router_expert_b[l])
               for l in range(2)]
    nm = [norm_mix[l].reshape(1, D_MODEL) for l in range(2)]
    nf = [norm_ffn[l].reshape(1, D_MODEL) for l in range(2)]

    x1_all, h_all, ri_all, rg_all, tc_all, k_last, v_last, va_last, cnt0 = _mix0_prompt(
        xp2, nm[0], win, lng, lnb, wsp, bs_full, bias_p, wout, nf[0], *routers[0])
    x1_all, h_all, ri_all, rg_all, tc_all, k_new, v_new, va_s, cnt0 = _mix0_sample(
        xs_t, nm[0], win, lng, lnb, wcoef, bcoef, ck, cv, bias_sc, bias_sn, wout, nf[0], *routers[0], cnt0,
        x1_all, h_all, ri_all, rg_all, tc_all)
    cplan0, ys0 = _moe(h_all, ri_all, cnt0, tc_all, w_gate, w_up, w_down, 0)
    x2_all = _combine(cplan0, x1_all, rg_all, ys0)

    wp = c_w_pool[0].astype(BF16)
    sc = c_scale[0].reshape(1, D_MODEL)
    x3_all, h2_all, ri2_all, rg2_all, tc2_all, pool_tail, cnt1 = _mix1_prompt(
        x2_all, nm[1], wp, sc, nf[1], *routers[1])
    state_t = jnp.transpose(state_pool[0], (1, 0, 2))
    x3_all, h2_all, ri2_all, rg2_all, tc2_all, hs1, cnt1 = _mix1_sample(
        x2_all, state_t, nm[1], wp, sc, nf[1], *routers[1], cnt1, x3_all, h2_all, ri2_all, rg2_all, tc2_all)
    cplan1, ys1 = _moe(h2_all, ri2_all, cnt1, tc2_all, w_gate, w_up, w_down, 1)
    y_p, y_s = _final(cplan1, x3_all, rg2_all, ys1, norm_final.reshape(1, D_MODEL))

    def from_tmajor(a, width):
        return jnp.transpose(a.reshape(DEC_SEQ, DEC_BATCH, width), (1, 0, 2))

    y_prompt = y_p.reshape(BATCH, SEQ, D_MODEL)
    y_sample = from_tmajor(y_s, D_MODEL)
    win_k_p = k_last.reshape(1, BATCH, WINDOW, B_KV_HEADS, B_HEAD_DIM)
    win_v_p = v_last.reshape(1, BATCH, WINDOW, B_KV_HEADS, B_HEAD_DIM)
    kn = from_tmajor(k_new, KV_WIDTH).reshape(DEC_BATCH, DEC_SEQ, B_KV_HEADS, B_HEAD_DIM)
    vn = from_tmajor(v_new, KV_WIDTH).reshape(DEC_BATCH, DEC_SEQ, B_KV_HEADS, B_HEAD_DIM)
    win_k_s = jnp.concatenate([cache_k_win[0][:, DEC_SEQ:], kn], axis=1)[None]
    win_v_s = jnp.concatenate([cache_v_win[0][:, DEC_SEQ:], vn], axis=1)[None]
    chunk_v_p = va_last.reshape(1, BATCH, CHUNK, A_HEADS, A_HEAD_DIM)
    chunk_v_s = from_tmajor(va_s, A_WIDTH).reshape(1, DEC_BATCH, DEC_SEQ, A_HEADS, A_HEAD_DIM)
    pool_p = pool_tail[:, 1:][None]
    pool_s = jnp.concatenate([state_pool[0][:, DEC_SEQ:], from_tmajor(hs1, D_MODEL)], axis=1)[None]
    return (y_prompt, y_sample, win_k_p, win_v_p, win_k_s, win_v_s, chunk_v_p, chunk_v_s, pool_p, pool_s)
```

```python
import functools
import math

import numpy as np
import jax
import jax.numpy as jnp
from jax import lax
from jax.experimental import pallas as pl
from jax.experimental.pallas import tpu as pltpu

F32 = jnp.float32
BF16 = jnp.bfloat16

D_MODEL = 1024
BATCH = 2
SEQ = 8192
DEC_BATCH = 128
DEC_SEQ = 4
A_WIDTH = 512
A_HEADS = 8
A_HEAD_DIM = 64
CHUNK = 128
B_HEADS = 8
B_KV_HEADS = 2
B_HEAD_DIM = 64
B_GROUP = 4
WINDOW = 128
N_BUCKETS = 32
MAX_DISTANCE = WINDOW
Q_WIDTH = 512
KV_WIDTH = 128
IN_WIDTH = 2 * A_WIDTH + Q_WIDTH + 2 * KV_WIDTH
ATTN_SCALE = B_HEAD_DIM ** -0.5
NEG_INF = -1e30
POOL_SIZES = (2, 4, 8, 16)
POOL_GROUP_DIM = 256
POOL_MAX = 16
N_GROUPS = 4
EXPERTS_PER_GROUP = 8
N_EXPERTS = 32
TOP_K = 2
D_EXPERT = 512
EPS = 1e-6

LANES = 128
ROW_TILE = D_MODEL // LANES
T_PROMPT = BATCH * SEQ
T_SAMPLE = DEC_BATCH * DEC_SEQ
T_ALL = T_PROMPT + T_SAMPLE
TM = 512
N_PROMPT_BLOCKS = T_PROMPT // TM
N_ROW_BLOCKS = T_ALL // TM
STEPS_PER_BATCH = SEQ // TM
SUB = TM // WINDOW
N_SLOTS = T_ALL * TOP_K
MOE_BLK = 512
N_MOE_BLOCKS = N_SLOTS // MOE_BLK + N_EXPERTS
N_SORT_ROWS = N_MOE_BLOCKS * MOE_BLK
SAMPLE_GROUP = 8
N_SAMPLE_GROUPS = DEC_BATCH // SAMPLE_GROUP
VMEM_LIMIT = 56 * 1024 * 1024

STACK_HEADS = ((0, 2, 5, 7), (1, 3, 4, 6))


def _t5_bucket_np(dist):
    n = np.maximum(dist, 0)
    max_exact = N_BUCKETS // 2
    nf = np.maximum(n, 1).astype(np.float32)
    large = max_exact + (np.log(nf / np.float32(max_exact)) / np.float32(math.log(MAX_DISTANCE / max_exact))
                         * np.float32(N_BUCKETS - max_exact)).astype(np.int32)
    large = np.minimum(large, N_BUCKETS - 1)
    return np.where(n < max_exact, n, large).astype(np.int32)


def _bucket_tables():
    qi = np.arange(WINDOW)[:, None]
    ki = np.arange(2 * WINDOW)[None, :]
    dist = qi + WINDOW - ki
    valid = (dist >= 0) & (dist < WINDOW)
    bp = np.where(valid, _t5_bucket_np(dist), -1)
    bp_first = np.where(ki >= WINDOW, bp, -1)
    bkt_p = np.stack([bp_first, bp]).astype(np.int32)

    t = np.repeat(np.arange(DEC_SEQ), SAMPLE_GROUP)[:, None]
    b = np.tile(np.arange(SAMPLE_GROUP), DEC_SEQ)[:, None]
    cb = np.repeat(np.arange(SAMPLE_GROUP), WINDOW)[None, :]
    cj = np.tile(np.arange(WINDOW), SAMPLE_GROUP)[None, :]
    dist_c = t + WINDOW - cj
    valid_c = (cb == b) & (dist_c >= 0) & (dist_c < WINDOW)
    bkt_sc = np.where(valid_c, _t5_bucket_np(dist_c), -1).astype(np.int32)
    nt = np.repeat(np.arange(DEC_SEQ), SAMPLE_GROUP)[None, :]
    nb = np.tile(np.arange(SAMPLE_GROUP), DEC_SEQ)[None, :]
    dist_n = t - nt
    valid_n = (nb == b) & (dist_n >= 0)
    bkt_sn = np.where(valid_n, _t5_bucket_np(dist_n), -1).astype(np.int32)
    bkt_sn = np.concatenate([bkt_sn, np.full((32, LANES - 32), -1, np.int32)], axis=1)
    return bkt_p, bkt_sc, bkt_sn


_BKT_P, _BKT_SC, _BKT_SN = _bucket_tables()


def _cparams(semantics):
    return pltpu.CompilerParams(dimension_semantics=semantics, vmem_limit_bytes=VMEM_LIMIT)


def _rms(x, g):
    return x * lax.rsqrt(jnp.mean(x * x, axis=-1, keepdims=True) + EPS) * g


def _layernorm(x, g, b):
    xc = x - jnp.mean(x, axis=-1, keepdims=True)
    return xc * lax.rsqrt(jnp.mean(xc * xc, axis=-1, keepdims=True) + EPS) * g + b


def _dot(a, b):
    return jnp.dot(a, b, preferred_element_type=F32)


def _dot_nt(a, b):
    return lax.dot_general(a, b, (((1,), (1,)), ((), ())), preferred_element_type=F32)


def _project(x, nm, win, lng, lnb):
    h = _rms(x, nm)
    z = _dot(h.astype(BF16), win)
    u = jax.nn.gelu(z[:, :A_WIDTH])
    va = _layernorm(jax.nn.gelu(z[:, A_WIDTH:2 * A_WIDTH]), lng, lnb)
    q = z[:, 2 * A_WIDTH:2 * A_WIDTH + Q_WIDTH] * ATTN_SCALE
    k = z[:, 2 * A_WIDTH + Q_WIDTH:2 * A_WIDTH + Q_WIDTH + KV_WIDTH]
    v = z[:, 2 * A_WIDTH + Q_WIDTH + KV_WIDTH:]
    return u, va, q, k, v


def _route(x1, nf, wr, br):
    hf = _rms(x1, nf)
    h = hf.astype(BF16)
    h_lo = (hf - h.astype(F32)).astype(BF16)
    part = _dot(h, wr)
    logits = part[:, :LANES] + part[:, LANES:] + _dot(h_lo, wr[:, :LANES]) + br
    rows = logits.shape[0]
    lane = lax.broadcasted_iota(jnp.int32, (rows, LANES), 1)
    lanef = lane.astype(F32)
    big = jnp.float32(1e9)
    is_g = lane < N_GROUPS
    gl = jnp.where(is_g, logits, -jnp.inf)
    gmax = jnp.max(gl, axis=1, keepdims=True)
    gsel = jnp.min(jnp.where(gl == gmax, lanef, big), axis=1, keepdims=True)
    gsum = jnp.sum(jnp.where(is_g, jnp.exp(logits - gmax), 0.0), axis=1, keepdims=True)
    g1 = 1.0 / gsum
    lo = N_GROUPS + EXPERTS_PER_GROUP * gsel
    emask = (lanef >= lo) & (lanef < lo + EXPERTS_PER_GROUP)
    el = jnp.where(emask, logits, -jnp.inf)
    v1 = jnp.max(el, axis=1, keepdims=True)
    i1 = jnp.min(jnp.where(el == v1, lanef, big), axis=1, keepdims=True)
    el2 = jnp.where(lanef == i1, -jnp.inf, el)
    v2 = jnp.max(el2, axis=1, keepdims=True)
    i2 = jnp.min(jnp.where(el2 == v2, lanef, big), axis=1, keepdims=True)
    e2 = jnp.exp(v2 - v1)
    den = 1.0 + e2
    w1 = g1 / den
    w2 = g1 * e2 / den
    ids = jnp.where(lane == 0, i1 - N_GROUPS, jnp.where(lane == 1, i2 - N_GROUPS, 0.0)).astype(jnp.int32)
    gates = jnp.where(lane == 0, w1, jnp.where(lane == 1, w2, 0.0))
    return h, ids, gates


def _rank_pack(ids, cnt_ref, tcnt_ref):
    rows = ids.shape[0]
    lane = lax.broadcasted_iota(jnp.int32, (rows, LANES), 1)
    o0 = (lane == ids[:, 0:1]).astype(F32)
    o1 = (lane == ids[:, 1:2]).astype(F32)
    r = lax.broadcasted_iota(jnp.int32, (rows, rows), 0)
    c = lax.broadcasted_iota(jnp.int32, (rows, rows), 1)
    before = (c < r).astype(BF16)
    p01 = _dot(before, jnp.concatenate([o0, o1], axis=1).astype(BF16))
    p0 = p01[:, :LANES]
    p1 = p01[:, LANES:]
    c0 = jnp.sum(o0, axis=0, keepdims=True)
    c1 = jnp.sum(o1, axis=0, keepdims=True)
    ctile = c0 + c1
    cnt_ref[...] = cnt_ref[...] + ctile
    tcnt_ref[...] = ctile
    inc = jnp.broadcast_to(ctile, (8, LANES))
    lane8 = lax.broadcasted_iota(jnp.int32, (8, LANES), 1)
    for sh in (1, 2, 4, 8, 16, 32, 64):
        inc = inc + jnp.where(lane8 >= sh, pltpu.roll(inc, sh, 1), 0.0)
    start = inc[0:1] - ctile
    lpos0 = jnp.sum(o0 * (start + p0), axis=1, keepdims=True)
    lpos1 = jnp.sum(o1 * (start + c0 + p1), axis=1, keepdims=True)
    idf = ids.astype(F32)
    packed = jnp.where(lane < TOP_K, idf, 0.0)
    for ln, col in ((4, lpos0), (5, lpos1)):
        packed = jnp.where(lane == ln, col, packed)
    return jnp.transpose(packed)[:8].astype(jnp.int32)


def _prep_kernel(tab_ref, sink_ref, bp_ref, bsc_ref, bsn_ref, ws_ref, op_ref, osc_ref, osn_ref, ows_ref):
    def fill(bkt, write, sink_col0):
        col0 = lax.broadcasted_iota(jnp.int32, bkt.shape, 1) == 0
        for st, heads in enumerate(STACK_HEADS):
            for slot, h in enumerate(heads):
                acc = jnp.full(bkt.shape, NEG_INF, F32)
                for b in range(N_BUCKETS):
                    acc = jnp.where(bkt == b, tab_ref[b, h], acc)
                if sink_col0:
                    acc = jnp.where(col0, sink_ref[0, h], acc)
                write(st, slot, acc)

    for var in range(2):
        def wr_p(st, slot, acc, var=var):
            op_ref[var, st, slot * WINDOW:(slot + 1) * WINDOW, :] = acc
        fill(bp_ref[var], wr_p, True)

    rows_s = DEC_SEQ * SAMPLE_GROUP

    def wr_sc(st, slot, acc):
        osc_ref[st, slot * rows_s:(slot + 1) * rows_s, :] = acc
    fill(bsc_ref[...], wr_sc, True)

    def wr_sn(st, slot, acc):
        osn_ref[st, slot * rows_s:(slot + 1) * rows_s, :] = acc
    fill(bsn_ref[...], wr_sn, False)

    r = lax.broadcasted_iota(jnp.int32, (CHUNK, CHUNK), 0)
    c = lax.broadcasted_iota(jnp.int32, (CHUNK, CHUNK), 1)
    for h in range(A_HEADS):
        ows_ref[h // 2, :, (h % 2) * CHUNK:(h % 2 + 1) * CHUNK] = jnp.where(r >= c, ws_ref[h], 0.0).astype(BF16)


def _prep(rel_bias_table, sinks, w_s):
    vm = pl.BlockSpec(memory_space=pltpu.VMEM)
    sm = pl.BlockSpec(memory_space=pltpu.SMEM)
    rows_s = DEC_SEQ * SAMPLE_GROUP
    return pl.pallas_call(
        _prep_kernel,
        in_specs=[sm, sm, vm, vm, vm, vm],
        out_specs=[vm, vm, vm, vm],
        out_shape=[
            jax.ShapeDtypeStruct((2, 2, 4 * WINDOW, 2 * WINDOW), F32),
            jax.ShapeDtypeStruct((2, 4 * rows_s, SAMPLE_GROUP * WINDOW), F32),
            jax.ShapeDtypeStruct((2, 4 * rows_s, LANES), F32),
            jax.ShapeDtypeStruct((A_HEADS // 2, CHUNK, 2 * CHUNK), BF16),
        ],
        name="prep_tables",
    )(rel_bias_table, sinks.reshape(1, B_HEADS), jnp.asarray(_BKT_P), jnp.asarray(_BKT_SC), jnp.asarray(_BKT_SN), w_s)


def _gate_pairs(va_rows, wsp_ref, lane_lo):
    outs = []
    for p in range(A_HEADS // 2):
        vp = va_rows[:, p * LANES:(p + 1) * LANES]
        rhs = jnp.concatenate([jnp.where(lane_lo, vp, 0.0), jnp.where(lane_lo, 0.0, vp)], axis=0).astype(BF16)
        outs.append(_dot(wsp_ref[p], rhs))
    return jnp.concatenate(outs, axis=1)


def _prompt_steps(body, first_row_out):
    def kern(*refs):
        i = pl.program_id(0)

        @pl.when(i < N_PROMPT_BLOCKS)
        def _():
            body(*refs)

        @pl.when(i >= N_PROMPT_BLOCKS)
        def _():
            for r in refs[first_row_out:first_row_out + 5]:
                r[...] = jnp.zeros(r.shape, r.dtype)

    return kern


def _mix0_prompt_kernel(x_ref, nm_ref, win_ref, lng_ref, lnb_ref, wsp_ref, bs_ref, bias_ref,
                        wout_ref, nf_ref, wr_ref, br_ref,
                        x1_ref, h_ref, ri_ref, rg_ref, tc_ref, kl_ref, vl_ref, val_ref, cnt_ref,
                        kprev, vprev, mix_scr):
    @pl.when(pl.program_id(0) == 0)
    def _():
        cnt_ref[...] = jnp.zeros_like(cnt_ref)

    x = x_ref[...]
    u, va, q, k, v = _project(x, nm_ref[...], win_ref[...], lng_ref[...], lnb_ref[...])
    lane_lo = lax.broadcasted_iota(jnp.int32, (WINDOW, LANES), 1) < B_HEAD_DIM
    row0 = lax.broadcasted_iota(jnp.int32, (WINDOW, KV_WIDTH), 0) == 0
    first = pl.program_id(0) % STEPS_PER_BATCH == 0

    @pl.when(first)
    def _():
        kprev[...] = jnp.zeros_like(kprev)
        vprev[...] = jnp.zeros_like(vprev)

    for j in range(SUB):
        rows = slice(j * WINDOW, (j + 1) * WINDOW)
        s_gate = _gate_pairs(va[rows], wsp_ref, lane_lo)
        mix_scr[rows, :A_WIDTH] = u[rows] * (s_gate + bs_ref[...])

        if j == 0:
            kp, vp = kprev[...], vprev[...]
        else:
            prows = slice((j - 1) * WINDOW, j * WINDOW)
            kp, vp = k[prows], v[prows]
        kk = jnp.concatenate([jnp.where(row0, 0.0, kp), k[rows]], axis=0)
        vv = jnp.concatenate([jnp.where(row0, 0.0, vp), v[rows]], axis=0)
        kops = (kk.astype(BF16), pltpu.roll(kk, B_HEAD_DIM, 1).astype(BF16))
        vops = (vv.astype(BF16), pltpu.roll(vv, B_HEAD_DIM, 1).astype(BF16))
        qt = [q[rows, p * LANES:(p + 1) * LANES] for p in range(4)]
        q_even = [jnp.where(lane_lo, t, 0.0) for t in qt]
        q_odd = [jnp.where(lane_lo, 0.0, t) for t in qt]
        stacks = (jnp.concatenate([q_even[0], q_even[1], q_odd[2], q_odd[3]], axis=0),
                  jnp.concatenate([q_odd[0], q_odd[1], q_even[2], q_even[3]], axis=0))
        o = []
        for st in range(2):
            s = _dot_nt(stacks[st].astype(BF16), kops[st])
            if j == 0:
                bias = bias_ref[jnp.where(first, 0, 1), st]
            else:
                bias = bias_ref[1, st]
            s = s + bias
            m = jnp.max(s, axis=-1, keepdims=True)
            p = jnp.exp(s - m)
            den = jnp.sum(p, axis=-1, keepdims=True)
            o.append(_dot(p.astype(BF16), vops[st]) / den)
        oa, ob = o
        sl = [slice(i * WINDOW, (i + 1) * WINDOW) for i in range(4)]
        tiles = (jnp.where(lane_lo, oa[sl[0]], ob[sl[0]]), jnp.where(lane_lo, oa[sl[1]], ob[sl[1]]),
                 jnp.where(lane_lo, ob[sl[2]], oa[sl[2]]), jnp.where(lane_lo, ob[sl[3]], oa[sl[3]]))
        for p in range(4):
            mix_scr[rows, A_WIDTH + p * LANES:A_WIDTH + (p + 1) * LANES] = tiles[p]

    last = slice(TM - WINDOW, TM)
    kprev[...] = k[last]
    vprev[...] = v[last]
    kl_ref[...] = k[last]
    vl_ref[...] = v[last]
    val_ref[...] = va[last]

    x1 = x + _dot(mix_scr[...].astype(BF16), wout_ref[...])
    x1_ref[...] = x1
    h, ids, gates = _route(x1, nf_ref[...], wr_ref[...], br_ref[...])
    h_ref[...] = h.reshape(h_ref.shape)
    ri_ref[...] = _rank_pack(ids, cnt_ref, tc_ref)
    rg_ref[...] = gates


def _const_spec(shape):
    nd = len(shape)
    return pl.BlockSpec(shape, lambda i, _n=nd: (0,) * _n)


def _mix0_prompt(x_all, nm, win, lng, lnb, wsp, bs_full, bias_p, wout, nf, wr, br):
    row_spec = pl.BlockSpec((TM, D_MODEL), lambda i: (i, 0))
    row3_spec = pl.BlockSpec((TM, ROW_TILE, LANES), lambda i: (i, 0, 0))
    lane_spec = pl.BlockSpec((TM, LANES), lambda i: (i, 0))
    last_kv = pl.BlockSpec((None, WINDOW, KV_WIDTH), lambda i: (jnp.minimum(i // STEPS_PER_BATCH, BATCH - 1), 0, 0))
    last_va = pl.BlockSpec((None, WINDOW, A_WIDTH), lambda i: (jnp.minimum(i // STEPS_PER_BATCH, BATCH - 1), 0, 0))
    return pl.pallas_call(
        _prompt_steps(_mix0_prompt_kernel, 12),
        grid=(N_ROW_BLOCKS,),
        in_specs=[pl.BlockSpec((TM, D_MODEL), lambda i: (jnp.minimum(i, N_PROMPT_BLOCKS - 1), 0)),
                  _const_spec((1, D_MODEL)), _const_spec((D_MODEL, IN_WIDTH)),
                  _const_spec((1, A_WIDTH)), _const_spec((1, A_WIDTH)),
                  _const_spec((A_HEADS // 2, CHUNK, 2 * CHUNK)), _const_spec((CHUNK, A_WIDTH)),
                  _const_spec((2, 2, 4 * WINDOW, 2 * WINDOW)),
                  _const_spec((A_WIDTH + Q_WIDTH, D_MODEL)), _const_spec((1, D_MODEL)),
                  _const_spec((D_MODEL, 2 * LANES)), _const_spec((1, LANES))],
        out_specs=[row_spec, row3_spec, pl.BlockSpec((8, TM), lambda i: (0, i)), lane_spec,
                   pl.BlockSpec((None, 1, LANES), lambda i: (i, 0, 0)),
                   last_kv, last_kv, last_va, _const_spec((1, LANES))],
        out_shape=[jax.ShapeDtypeStruct((T_ALL, D_MODEL), F32), jax.ShapeDtypeStruct((T_ALL, ROW_TILE, LANES), BF16),
                   jax.ShapeDtypeStruct((8, T_ALL), jnp.int32), jax.ShapeDtypeStruct((T_ALL, LANES), F32),
                   jax.ShapeDtypeStruct((N_ROW_BLOCKS, 1, LANES), F32),
                   jax.ShapeDtypeStruct((BATCH, WINDOW, KV_WIDTH), F32),
                   jax.ShapeDtypeStruct((BATCH, WINDOW, KV_WIDTH), F32),
                   jax.ShapeDtypeStruct((BATCH, WINDOW, A_WIDTH), F32),
                   jax.ShapeDtypeStruct((1, LANES), F32)],
        scratch_shapes=[pltpu.VMEM((WINDOW, KV_WIDTH), F32), pltpu.VMEM((WINDOW, KV_WIDTH), F32),
                        pltpu.VMEM((TM, D_MODEL), F32)],
        compiler_params=_cparams(("arbitrary",)),
        name="mix0_prompt",
    )(x_all, nm, win, lng, lnb, wsp, bs_full, bias_p, wout, nf, wr, br)


def _mix0_sample_kernel(x_ref, nm_ref, win_ref, lng_ref, lnb_ref, wcoef_ref, bcoef_ref,
                        ck_ref, cv_ref, bsc_ref, bsn_ref,
                        wout_ref, nf_ref, wr_ref, br_ref, cnt_in,
                        x1_in, h_in, ri_in, rg_in, tc_in,
                        x1_ref, h_ref, ri_ref, rg_ref, tc_ref, kn_ref, vn_ref, va_ref, cnt_ref,
                        q_scr, k_scr, v_scr, mix_scr):
    del x1_in, h_in, ri_in, rg_in, tc_in
    g = pl.program_id(0)

    @pl.when(g == 0)
    def _():
        u, va, q, k, v = _project(x_ref[...], nm_ref[...], win_ref[...], lng_ref[...], lnb_ref[...])
        q_scr[...] = q
        k_scr[...] = k
        v_scr[...] = v
        va_ref[...] = va
        for t in range(DEC_SEQ):
            acc = jnp.zeros((DEC_BATCH, A_WIDTH), F32) + bcoef_ref[t:t + 1, :]
            for s in range(t + 1):
                row = t * DEC_SEQ + s
                acc = acc + wcoef_ref[row:row + 1, :] * va[s * DEC_BATCH:(s + 1) * DEC_BATCH]
            mix_scr[t * DEC_BATCH:(t + 1) * DEC_BATCH, :A_WIDTH] = u[t * DEC_BATCH:(t + 1) * DEC_BATCH] * acc

    b0 = pl.multiple_of(g * SAMPLE_GROUP, SAMPLE_GROUP)
    lane_lo = lax.broadcasted_iota(jnp.int32, (DEC_SEQ * SAMPLE_GROUP, LANES), 1) < B_HEAD_DIM

    def grab(ref, width):
        return jnp.concatenate([ref[pl.ds(t * DEC_BATCH + b0, SAMPLE_GROUP), :] for t in range(DEC_SEQ)], axis=0)

    qg = grab(q_scr, Q_WIDTH)
    kn = grab(k_scr, KV_WIDTH)
    vn = grab(v_scr, KV_WIDTH)

    lane_w = lax.broadcasted_iota(jnp.int32, (KV_WIDTH, WINDOW), 1)
    n_new = DEC_SEQ * SAMPLE_GROUP

    def new_window(c_ref, new_rows, w_ref):
        nt = jnp.transpose(jnp.concatenate([new_rows, jnp.zeros((WINDOW - n_new, KV_WIDTH), F32)], axis=0))
        for b in range(SAMPLE_GROUP):
            w = pltpu.roll(c_ref[b].reshape(KV_WIDTH, WINDOW), WINDOW - DEC_SEQ, 1)
            for t in range(DEC_SEQ):
                src = t * SAMPLE_GROUP + b
                dst = WINDOW - DEC_SEQ + t
                w = jnp.where(lane_w == dst, pltpu.roll(nt, (dst - src) % WINDOW, 1), w)
            w_ref[b] = w.reshape(B_KV_HEADS, B_HEAD_DIM, WINDOW)

    new_window(ck_ref, kn, kn_ref)
    new_window(cv_ref, vn, vn_ref)
    ccol0 = lax.broadcasted_iota(jnp.int32, (KV_WIDTH, SAMPLE_GROUP * WINDOW), 1) == 0

    def cache_t(ref):
        t = jnp.concatenate([ref[b].reshape(KV_WIDTH, WINDOW) for b in range(SAMPLE_GROUP)], axis=1)
        return jnp.where(ccol0, 0.0, t)

    def head_swap(t):
        return jnp.concatenate([t[B_HEAD_DIM:], t[:B_HEAD_DIM]], axis=0)

    kct = cache_t(ck_ref)
    vct = cache_t(cv_ref)
    kc_ops = (kct.astype(BF16), head_swap(kct).astype(BF16))
    vc_ops = (vct.astype(BF16), head_swap(vct).astype(BF16))
    kn_ops = (kn.astype(BF16), pltpu.roll(kn, B_HEAD_DIM, 1).astype(BF16))
    vn_ops = (vn.astype(BF16), pltpu.roll(vn, B_HEAD_DIM, 1).astype(BF16))
    qt = [qg[:, p * LANES:(p + 1) * LANES] for p in range(4)]
    q_even = [jnp.where(lane_lo, t, 0.0) for t in qt]
    q_odd = [jnp.where(lane_lo, 0.0, t) for t in qt]
    stacks = (jnp.concatenate([q_even[0], q_even[1], q_odd[2], q_odd[3]], axis=0),
              jnp.concatenate([q_odd[0], q_odd[1], q_even[2], q_even[3]], axis=0))
    o = []
    for st in range(2):
        qs = stacks[st].astype(BF16)
        sc = _dot(qs, kc_ops[st]) + bsc_ref[st]
        sn = _dot_nt(qs, kn_ops[st]) + bsn_ref[st][:, :DEC_SEQ * SAMPLE_GROUP]
        m = jnp.maximum(jnp.max(sc, axis=-1, keepdims=True), jnp.max(sn, axis=-1, keepdims=True))
        pc = jnp.exp(sc - m)
        pn = jnp.exp(sn - m)
        den = jnp.sum(pc, axis=-1, keepdims=True) + jnp.sum(pn, axis=-1, keepdims=True)
        o.append((_dot_nt(pc.astype(BF16), vc_ops[st]) + _dot(pn.astype(BF16), vn_ops[st])) / den)
    oa, ob = o
    n = DEC_SEQ * SAMPLE_GROUP
    sl = [slice(i * n, (i + 1) * n) for i in range(4)]
    tiles = (jnp.where(lane_lo, oa[sl[0]], ob[sl[0]]), jnp.where(lane_lo, oa[sl[1]], ob[sl[1]]),
             jnp.where(lane_lo, ob[sl[2]], oa[sl[2]]), jnp.where(lane_lo, ob[sl[3]], oa[sl[3]]))
    for p in range(4):
        for t in range(DEC_SEQ):
            mix_scr[pl.ds(t * DEC_BATCH + b0, SAMPLE_GROUP), A_WIDTH + p * LANES:A_WIDTH + (p + 1) * LANES] = (
                tiles[p][t * SAMPLE_GROUP:(t + 1) * SAMPLE_GROUP])

    @pl.when(g == N_SAMPLE_GROUPS - 1)
    def _():
        x1 = x_ref[...] + _dot(mix_scr[...].astype(BF16), wout_ref[...])
        x1_ref[...] = x1
        h, ids, gates = _route(x1, nf_ref[...], wr_ref[...], br_ref[...])
        h_ref[...] = h.reshape(h_ref.shape)
        cnt_ref[...] = cnt_in[...]
        ri_ref[...] = _rank_pack(ids, cnt_ref, tc_ref)
        rg_ref[...] = gates


def _mix0_sample(x_all, nm, win, lng, lnb, wcoef, bcoef, ck, cv, bias_sc, bias_sn, wout, nf, wr, br, cnt,
                 x1_all, h_all, ri_all, rg_all, tc_all):
    sample_rows = pl.BlockSpec((TM, D_MODEL), lambda g: (N_PROMPT_BLOCKS, 0))
    sample_rows3 = pl.BlockSpec((TM, ROW_TILE, LANES), lambda g: (N_PROMPT_BLOCKS, 0, 0))
    sample_lanes = pl.BlockSpec((TM, LANES), lambda g: (N_PROMPT_BLOCKS, 0))
    cache_spec = pl.BlockSpec((SAMPLE_GROUP, B_KV_HEADS, B_HEAD_DIM, WINDOW), lambda g: (g, 0, 0, 0))
    anyspec = pl.BlockSpec(memory_space=pl.ANY)
    n_in = 16
    return pl.pallas_call(
        _mix0_sample_kernel,
        grid=(N_SAMPLE_GROUPS,),
        in_specs=[_const_spec((TM, D_MODEL)), _const_spec((1, D_MODEL)), _const_spec((D_MODEL, IN_WIDTH)),
                  _const_spec((1, A_WIDTH)), _const_spec((1, A_WIDTH)),
                  _const_spec((16, A_WIDTH)), _const_spec((8, A_WIDTH)),
                  cache_spec, cache_spec,
                  _const_spec((2, 4 * 32, SAMPLE_GROUP * WINDOW)), _const_spec((2, 4 * 32, LANES)),
                  _const_spec((A_WIDTH + Q_WIDTH, D_MODEL)), _const_spec((1, D_MODEL)),
                  _const_spec((D_MODEL, 2 * LANES)), _const_spec((1, LANES)), _const_spec((1, LANES)),
                  anyspec, anyspec, anyspec, anyspec, anyspec],
        out_specs=[sample_rows, sample_rows3, pl.BlockSpec((8, TM), lambda g: (0, N_PROMPT_BLOCKS)), sample_lanes,
                   pl.BlockSpec((None, 1, LANES), lambda g: (N_PROMPT_BLOCKS, 0, 0)),
                   cache_spec, cache_spec,
                   _const_spec((T_SAMPLE, A_WIDTH)), _const_spec((1, LANES))],
        out_shape=[jax.ShapeDtypeStruct((T_ALL, D_MODEL), F32), jax.ShapeDtypeStruct((T_ALL, ROW_TILE, LANES), BF16),
                   jax.ShapeDtypeStruct((8, T_ALL), jnp.int32), jax.ShapeDtypeStruct((T_ALL, LANES), F32),
                   jax.ShapeDtypeStruct((N_ROW_BLOCKS, 1, LANES), F32),
                   jax.ShapeDtypeStruct((DEC_BATCH, B_KV_HEADS, B_HEAD_DIM, WINDOW), F32),
                   jax.ShapeDtypeStruct((DEC_BATCH, B_KV_HEADS, B_HEAD_DIM, WINDOW), F32),
                   jax.ShapeDtypeStruct((T_SAMPLE, A_WIDTH), F32), jax.ShapeDtypeStruct((1, LANES), F32)],
        scratch_shapes=[pltpu.VMEM((T_SAMPLE, Q_WIDTH), F32), pltpu.VMEM((T_SAMPLE, KV_WIDTH), F32),
                        pltpu.VMEM((T_SAMPLE, KV_WIDTH), F32), pltpu.VMEM((T_SAMPLE, D_MODEL), F32)],
        input_output_aliases={n_in: 0, n_in + 1: 1, n_in + 2: 2, n_in + 3: 3, n_in + 4: 4},
        compiler_params=_cparams(("arbitrary",)),
        name="mix0_sample",
    )(x_all, nm, win, lng, lnb, wcoef, bcoef, ck, cv, bias_sc, bias_sn, wout, nf, wr, br, cnt,
      x1_all, h_all, ri_all, rg_all, tc_all)


def _moe_metadata(rt_all, cnt, tcnt):
    counts = cnt[0, :N_EXPERTS].astype(jnp.int32)
    padded = (counts + MOE_BLK - 1) // MOE_BLK * MOE_BLK
    pad_end = jnp.cumsum(padded)
    pad_start = pad_end - padded
    experts = jnp.arange(N_EXPERTS, dtype=jnp.int32)
    n_valid = (pad_end[-1] // MOE_BLK).astype(jnp.int32).reshape(1)
    blk_start = jnp.arange(N_MOE_BLOCKS, dtype=jnp.int32) * MOE_BLK
    block_e = jnp.minimum(jnp.sum((blk_start[:, None] >= pad_end[None, :]).astype(jnp.int32), axis=1),
                          N_EXPERTS - 1).astype(jnp.int32)
    zero_start = (pad_start + counts).astype(jnp.int32)
    zero_len = (padded - counts).astype(jnp.int32)
    first = (blk_start == pad_start[block_e]).astype(jnp.int32)
    used = counts > 0
    parity = ((jnp.cumsum(used.astype(jnp.int32)) - 1) % 2)[block_e].astype(jnp.int32)
    nearest = lax.cummin(jnp.where(used, experts, N_EXPERTS)[::-1])[::-1]
    next_used = jnp.concatenate([nearest[1:], jnp.full((1,), N_EXPERTS, jnp.int32)])
    nxt = jnp.where(next_used < N_EXPERTS, next_used, -1)[block_e].astype(jnp.int32)
    plan = (block_e, first, parity, nxt, n_valid)
    runs = tcnt[:, 0, :N_EXPERTS].astype(jnp.int32)
    run_dst = pad_start[None, :] + jnp.cumsum(runs, axis=0) - runs
    lpos = rt_all[2 * TOP_K:3 * TOP_K].reshape(N_SLOTS).astype(jnp.int32)
    cplan = (lpos, runs.reshape(-1), run_dst.reshape(-1).astype(jnp.int32))
    dplan = cplan + (jnp.concatenate([zero_start, zero_len, n_valid]),)
    return plan, dplan, cplan


RUN_PIECE = 32


def _for_run_pieces(n, start_piece):
    whole = n // RUN_PIECE

    def body(j, carry):
        start_piece(j * RUN_PIECE, RUN_PIECE)
        return carry

    lax.fori_loop(0, whole, body, 0)
    o = whole * RUN_PIECE
    bit = RUN_PIECE // 2
    while bit >= 1:
        take = (n & bit) != 0

        @pl.when(take)
        def _(o=o, bit=bit):
            start_piece(o, bit)

        o = o + jnp.where(take, bit, 0)
        bit //= 2


def _dispatch_kernel(lpos_ref, run_ref, rdst_ref, zs_ref, h_ref, xs_ref, zero_scr, stage, sem, zsem):
    i = pl.program_id(0)

    @pl.when(i == 0)
    def _():
        zero_scr[...] = jnp.zeros_like(zero_scr)

        def pieces(e, do):
            off = zs_ref[e]
            rem = zs_ref[N_EXPERTS + e]
            bit = MOE_BLK // 2
            while bit >= 1:
                take = (rem & bit) != 0

                @pl.when(take)
                def _(off=off, bit=bit):
                    do(pltpu.make_async_copy(zero_scr.at[pl.ds(0, bit)], xs_ref.at[pl.ds(off, bit)], zsem))

                off = off + jnp.where(take, bit, 0)
                bit //= 2

        def start_e(e, c):
            pieces(e, lambda cp: cp.start())
            return c

        def wait_e(e, c):
            pieces(e, lambda cp: cp.wait())
            return c

        def tail(do):
            def step(b, c):
                do(pltpu.make_async_copy(zero_scr, xs_ref.at[pl.ds(b * MOE_BLK, MOE_BLK)], zsem))
                return c
            return step

        n_valid = zs_ref[2 * N_EXPERTS]
        lax.fori_loop(0, N_EXPERTS, start_e, 0)
        lax.fori_loop(n_valid, N_MOE_BLOCKS, tail(lambda cp: cp.start()), 0)
        lax.fori_loop(0, N_EXPERTS, wait_e, 0)
        lax.fori_loop(n_valid, N_MOE_BLOCKS, tail(lambda cp: cp.wait()), 0)

    base = i * TM

    def place(r, carry):
        row = h_ref[r]
        for kk in range(TOP_K):
            stage[lpos_ref[kk * T_ALL + base + r]] = row
        return carry

    lax.fori_loop(0, TM, place, 0, unroll=8)

    def send_run(e, off):
        n = run_ref[i * N_EXPERTS + e]
        dst = rdst_ref[i * N_EXPERTS + e]
        _for_run_pieces(n, lambda o, size: pltpu.make_async_copy(
            stage.at[pl.ds(off + o, size)], xs_ref.at[pl.ds(dst + o, size)], sem).start(
                priority=size.bit_length() % 2))
        return off + n

    lax.fori_loop(0, N_EXPERTS, send_run, 0)
    pltpu.make_async_copy(stage, xs_ref.at[pl.ds(0, TM * TOP_K)], sem).wait()


def _dispatch(dplan, h_all):
    return pl.pallas_call(
        _dispatch_kernel,
        grid_spec=pltpu.PrefetchScalarGridSpec(
            num_scalar_prefetch=4,
            grid=(N_ROW_BLOCKS,),
            in_specs=[pl.BlockSpec((TM, ROW_TILE, LANES), lambda i, lp, rn, rd, z: (i, 0, 0))],
            out_specs=pl.BlockSpec(memory_space=pl.ANY),
            scratch_shapes=[pltpu.VMEM((MOE_BLK, ROW_TILE, LANES), BF16),
                            pltpu.VMEM((TM * TOP_K, ROW_TILE, LANES), BF16),
                            pltpu.SemaphoreType.DMA(()), pltpu.SemaphoreType.DMA(())],
        ),
        out_shape=jax.ShapeDtypeStruct((N_SORT_ROWS, ROW_TILE, LANES), BF16),
        compiler_params=_cparams(("arbitrary",)),
        name="moe_dispatch",
    )(*dplan, h_all)


def _experts_kernel(layer, be_ref, first_ref, par_ref, nxt_ref, nv_ref,
                    x_ref, wg_hbm, wu_hbm, wd_hbm, y_ref,
                    wg_s, wu_s, wd_s, wg_f, wu_f, wd_f, wsem):
    i = pl.program_id(0)

    def fetch(e, slot):
        return (pltpu.make_async_copy(wg_hbm.at[layer, e], wg_f.at[slot], wsem.at[slot]),
                pltpu.make_async_copy(wu_hbm.at[layer, e], wu_f.at[slot], wsem.at[slot]),
                pltpu.make_async_copy(wd_hbm.at[layer, e], wd_f.at[slot], wsem.at[slot]))

    @pl.when(i < nv_ref[0])
    def _():
        e = be_ref[i]
        slot = par_ref[i]

        @pl.when(i == 0)
        def _():
            for cp in fetch(e, slot):
                cp.start()

        @pl.when(first_ref[i] == 1)
        def _():
            for cp in fetch(e, slot):
                cp.wait()
            wg_s[...] = wg_f[slot].astype(BF16)
            wu_s[...] = wu_f[slot].astype(BF16)
            wd_s[...] = wd_f[slot].astype(BF16)
            nxt = nxt_ref[i]

            @pl.when(nxt >= 0)
            def _():
                for cp in fetch(nxt, 1 - slot):
                    cp.start()

        xb = x_ref[...].reshape(MOE_BLK, D_MODEL)
        a = jax.nn.silu(_dot(xb, wg_s[...])) * _dot(xb, wu_s[...])
        y_ref[...] = _dot(a.astype(BF16), wd_s[...]).reshape(y_ref.shape)

    @pl.when(i >= nv_ref[0])
    def _():
        y_ref[...] = jnp.zeros(y_ref.shape, y_ref.dtype)


def _experts(block_e, first, parity, nxt, n_valid, xs, w_gate, w_up, w_down, layer):
    def blk(i, be, fi, pa, nx, nv):
        return (jnp.maximum(jnp.minimum(i, nv[0] - 1), 0), 0, 0)

    anyspec = pl.BlockSpec(memory_space=pl.ANY)
    return pl.pallas_call(
        functools.partial(_experts_kernel, layer),
        grid_spec=pltpu.PrefetchScalarGridSpec(
            num_scalar_prefetch=5,
            grid=(N_MOE_BLOCKS,),
            in_specs=[pl.BlockSpec((MOE_BLK, ROW_TILE, LANES), blk), anyspec, anyspec, anyspec],
            out_specs=pl.BlockSpec((MOE_BLK, ROW_TILE, LANES), lambda i, be, fi, pa, nx, nv: (i, 0, 0)),
            scratch_shapes=[pltpu.VMEM((D_MODEL, D_EXPERT), BF16), pltpu.VMEM((D_MODEL, D_EXPERT), BF16),
                            pltpu.VMEM((D_EXPERT, D_MODEL), BF16),
                            pltpu.VMEM((2, D_MODEL, D_EXPERT), F32), pltpu.VMEM((2, D_MODEL, D_EXPERT), F32),
                            pltpu.VMEM((2, D_EXPERT, D_MODEL), F32), pltpu.SemaphoreType.DMA((2,))],
        ),
        out_shape=jax.ShapeDtypeStruct((N_SORT_ROWS, ROW_TILE, LANES), F32),
        compiler_params=_cparams(("arbitrary",)),
        name="moe_experts",
    )(block_e, first, parity, nxt, n_valid, xs, w_gate, w_up, w_down)


def _gather_rows(lpos_ref, run_ref, rdst_ref, ys_ref, ystage, ybuf, sem, i):
    def fetch(tile, buf):
        def fetch_run(e, off):
            n = run_ref[tile * N_EXPERTS + e]
            src = rdst_ref[tile * N_EXPERTS + e]
            _for_run_pieces(n, lambda o, size: pltpu.make_async_copy(
                ys_ref.at[pl.ds(src + o, size)], ystage.at[buf, pl.ds(off + o, size)], sem.at[buf]).start(
                    priority=size.bit_length() % 2))
            return off + n

        lax.fori_loop(0, N_EXPERTS, fetch_run, 0)

    buf = i % 2

    @pl.when(i == 0)
    def _():
        fetch(i, buf)

    @pl.when(i + 1 < N_ROW_BLOCKS)
    def _():
        fetch(i + 1, 1 - buf)

    pltpu.make_async_copy(ys_ref.at[pl.ds(0, TM * TOP_K)], ystage.at[buf], sem.at[buf]).wait()
    base = i * TM

    def unplace(r, carry):
        for kk in range(TOP_K):
            ybuf[kk, r] = ystage[buf, lpos_ref[kk * T_ALL + base + r]]
        return carry

    lax.fori_loop(0, TM, unplace, 0, unroll=8)


def _combined(x_ref, rg_ref, ybuf):
    rg = rg_ref[...]
    y0 = ybuf[0].reshape(TM, D_MODEL)
    y1 = ybuf[1].reshape(TM, D_MODEL)
    return x_ref[...] + rg[:, 0:1] * y0 + rg[:, 1:2] * y1


_COMBINE_SCRATCH = [pltpu.VMEM((2, TM * TOP_K, ROW_TILE, LANES), F32), pltpu.VMEM((TOP_K, TM, ROW_TILE, LANES), F32),
                    pltpu.SemaphoreType.DMA((2,))]


def _combine_kernel(lpos_ref, run_ref, rdst_ref, x_ref, rg_ref, ys_ref, o_ref, ystage, ybuf, sem):
    _gather_rows(lpos_ref, run_ref, rdst_ref, ys_ref, ystage, ybuf, sem, pl.program_id(0))
    o_ref[...] = _combined(x_ref, rg_ref, ybuf)


def _combine(cplan, x_all, rg_all, ys):
    return pl.pallas_call(
        _combine_kernel,
        grid_spec=pltpu.PrefetchScalarGridSpec(
            num_scalar_prefetch=3,
            grid=(N_ROW_BLOCKS,),
            in_specs=[pl.BlockSpec((TM, D_MODEL), lambda i, a, b, c: (i, 0)),
                      pl.BlockSpec((TM, LANES), lambda i, a, b, c: (i, 0)),
                      pl.BlockSpec(memory_space=pl.ANY)],
            out_specs=pl.BlockSpec((TM, D_MODEL), lambda i, a, b, c: (i, 0)),
            scratch_shapes=_COMBINE_SCRATCH,
        ),
        out_shape=jax.ShapeDtypeStruct((T_ALL, D_MODEL), F32),
        compiler_params=_cparams(("arbitrary",)),
        name="moe_combine",
    )(*cplan, x_all, rg_all, ys)


def _final_kernel(lpos_ref, run_ref, rdst_ref, x_ref, rg_ref, ys_ref, nfin_ref, op_ref, os_ref, ystage, ybuf, sem):
    i = pl.program_id(0)
    _gather_rows(lpos_ref, run_ref, rdst_ref, ys_ref, ystage, ybuf, sem, i)
    y = _rms(_combined(x_ref, rg_ref, ybuf), nfin_ref[...])

    @pl.when(i < N_PROMPT_BLOCKS)
    def _():
        op_ref[...] = y

    @pl.when(i >= N_PROMPT_BLOCKS)
    def _():
        os_ref[...] = y


def _final(cplan, x_all, rg_all, ys, nfin):
    return pl.pallas_call(
        _final_kernel,
        grid_spec=pltpu.PrefetchScalarGridSpec(
            num_scalar_prefetch=3,
            grid=(N_ROW_BLOCKS,),
            in_specs=[pl.BlockSpec((TM, D_MODEL), lambda i, a, b, c: (i, 0)),
                      pl.BlockSpec((TM, LANES), lambda i, a, b, c: (i, 0)),
                      pl.BlockSpec(memory_space=pl.ANY),
                      pl.BlockSpec((1, D_MODEL), lambda i, a, b, c: (0, 0))],
            out_specs=[pl.BlockSpec((TM, D_MODEL), lambda i, a, b, c: (jnp.minimum(i, N_PROMPT_BLOCKS - 1), 0)),
                       pl.BlockSpec((TM, D_MODEL), lambda i, a, b, c: (0, 0))],
            scratch_shapes=_COMBINE_SCRATCH,
        ),
        out_shape=[jax.ShapeDtypeStruct((T_PROMPT, D_MODEL), F32), jax.ShapeDtypeStruct((T_SAMPLE, D_MODEL), F32)],
        compiler_params=_cparams(("arbitrary",)),
        name="moe_combine_final",
    )(*cplan, x_all, rg_all, ys, nfin)


def _moe(h_all, rt_all, cnt, tcnt, w_gate, w_up, w_down, layer):
    plan, dplan, cplan = _moe_metadata(rt_all, cnt, tcnt)
    xs = _dispatch(dplan, h_all)
    ys = _experts(*plan, xs, w_gate, w_up, w_down, layer)
    return cplan, ys


def _pool_project(d_groups, wp_ref, scale):
    outs = [_dot(d_groups[g].astype(BF16), wp_ref[g]) for g in range(len(POOL_SIZES))]
    return jnp.concatenate(outs, axis=1) * scale


def _mix1_prompt_kernel(x_ref, nm_ref, wp_ref, sc_ref, nf_ref, wr_ref, br_ref,
                        x3_ref, h_ref, ri_ref, rg_ref, tc_ref, pl_ref, cnt_ref, ext):
    i = pl.program_id(0)

    @pl.when(i == 0)
    def _():
        cnt_ref[...] = jnp.zeros_like(cnt_ref)

    x = x_ref[...]
    hp = _rms(x, nm_ref[...])

    @pl.when(i % STEPS_PER_BATCH == 0)
    def _():
        ext[0:POOL_MAX, :] = jnp.zeros((POOL_MAX, D_MODEL), F32)

    ext[POOL_MAX:, :] = hp
    pos = (i % STEPS_PER_BATCH) * TM + lax.broadcasted_iota(jnp.int32, (TM, 1), 0)
    d_groups = []
    for g, w in enumerate(POOL_SIZES):
        cols = slice(g * POOL_GROUP_DIM, (g + 1) * POOL_GROUP_DIM)
        acc = ext[:, cols]
        span = 1
        while span < w:
            acc = acc + pltpu.roll(acc, span, 0)
            span *= 2
        cnt = jnp.minimum(pos + 1, w).astype(F32)
        d_groups.append(acc[POOL_MAX:] / cnt - hp[:, cols])
    tail = hp[TM - POOL_MAX:, :]
    ext[0:POOL_MAX, :] = tail
    pl_ref[...] = tail

    x3 = x + _pool_project(d_groups, wp_ref, sc_ref[...])
    x3_ref[...] = x3
    h, ids, gates = _route(x3, nf_ref[...], wr_ref[...], br_ref[...])
    h_ref[...] = h.reshape(h_ref.shape)
    ri_ref[...] = _rank_pack(ids, cnt_ref, tc_ref)
    rg_ref[...] = gates


def _mix1_prompt(x_all, nm, wp, sc, nf, wr, br):
    row_spec = pl.BlockSpec((TM, D_MODEL), lambda i: (i, 0))
    row3_spec = pl.BlockSpec((TM, ROW_TILE, LANES), lambda i: (i, 0, 0))
    lane_spec = pl.BlockSpec((TM, LANES), lambda i: (i, 0))
    return pl.pallas_call(
        _prompt_steps(_mix1_prompt_kernel, 7),
        grid=(N_ROW_BLOCKS,),
        in_specs=[row_spec, _const_spec((1, D_MODEL)),
                  _const_spec((len(POOL_SIZES), POOL_GROUP_DIM, POOL_GROUP_DIM)), _const_spec((1, D_MODEL)),
                  _const_spec((1, D_MODEL)), _const_spec((D_MODEL, 2 * LANES)), _const_spec((1, LANES))],
        out_specs=[row_spec, row3_spec, pl.BlockSpec((8, TM), lambda i: (0, i)), lane_spec,
                   pl.BlockSpec((None, 1, LANES), lambda i: (i, 0, 0)),
                   pl.BlockSpec((None, POOL_MAX, D_MODEL),
                                lambda i: (jnp.minimum(i // STEPS_PER_BATCH, BATCH - 1), 0, 0)),
                   _const_spec((1, LANES))],
        out_shape=[jax.ShapeDtypeStruct((T_ALL, D_MODEL), F32), jax.ShapeDtypeStruct((T_ALL, ROW_TILE, LANES), BF16),
                   jax.ShapeDtypeStruct((8, T_ALL), jnp.int32), jax.ShapeDtypeStruct((T_ALL, LANES), F32),
                   jax.ShapeDtypeStruct((N_ROW_BLOCKS, 1, LANES), F32),
                   jax.ShapeDtypeStruct((BATCH, POOL_MAX, D_MODEL), F32), jax.ShapeDtypeStruct((1, LANES), F32)],
        scratch_shapes=[pltpu.VMEM((POOL_MAX + TM, D_MODEL), F32)],
        compiler_params=_cparams(("arbitrary",)),
        name="mix1_prompt",
    )(x_all, nm, wp, sc, nf, wr, br)


def _mix1_sample_kernel(x_ref, st_ref, nm_ref, wp_ref, sc_ref, nf_ref, wr_ref, br_ref, cnt_in,
                        x3_in, h_in, ri_in, rg_in, tc_in,
                        x3_ref, h_ref, ri_ref, rg_ref, tc_ref, hs_ref, cnt_ref):
    del x3_in, h_in, ri_in, rg_in, tc_in
    x = x_ref[...]
    hs = _rms(x, nm_ref[...])
    hs_ref[...] = hs
    n_ctx = POOL_MAX - 1
    d_groups = []
    for g, w in enumerate(POOL_SIZES):
        cols = slice(g * POOL_GROUP_DIM, (g + 1) * POOL_GROUP_DIM)
        parts = []
        for t in range(DEC_SEQ):
            acc = hs[t * DEC_BATCH:(t + 1) * DEC_BATCH, cols]
            for back in range(1, w):
                src = t - back
                if src >= 0:
                    acc = acc + hs[src * DEC_BATCH:(src + 1) * DEC_BATCH, cols]
                else:
                    acc = acc + st_ref[n_ctx + src, :, cols]
            parts.append(acc / float(w) - hs[t * DEC_BATCH:(t + 1) * DEC_BATCH, cols])
        d_groups.append(jnp.concatenate(parts, axis=0))
    x3 = x + _pool_project(d_groups, wp_ref, sc_ref[...])
    x3_ref[...] = x3
    h, ids, gates = _route(x3, nf_ref[...], wr_ref[...], br_ref[...])
    h_ref[...] = h.reshape(h_ref.shape)
    cnt_ref[...] = cnt_in[...]
    ri_ref[...] = _rank_pack(ids, cnt_ref, tc_ref)
    rg_ref[...] = gates


def _mix1_sample(x_all, state_t, nm, wp, sc, nf, wr, br, cnt, x3_all, h_all, ri_all, rg_all, tc_all):
    sample_rows = pl.BlockSpec((TM, D_MODEL), lambda g: (N_PROMPT_BLOCKS, 0))
    sample_rows3 = pl.BlockSpec((TM, ROW_TILE, LANES), lambda g: (N_PROMPT_BLOCKS, 0, 0))
    sample_lanes = pl.BlockSpec((TM, LANES), lambda g: (N_PROMPT_BLOCKS, 0))
    anyspec = pl.BlockSpec(memory_space=pl.ANY)
    n_in = 9
    return pl.pallas_call(
        _mix1_sample_kernel,
        grid=(1,),
        in_specs=[sample_rows, _const_spec((POOL_MAX - 1, DEC_BATCH, D_MODEL)), _const_spec((1, D_MODEL)),
                  _const_spec((len(POOL_SIZES), POOL_GROUP_DIM, POOL_GROUP_DIM)), _const_spec((1, D_MODEL)),
                  _const_spec((1, D_MODEL)), _const_spec((D_MODEL, 2 * LANES)), _const_spec((1, LANES)),
                  _const_spec((1, LANES)), anyspec, anyspec, anyspec, anyspec, anyspec],
        out_specs=[sample_rows, sample_rows3, pl.BlockSpec((8, TM), lambda g: (0, N_PROMPT_BLOCKS)), sample_lanes,
                   pl.BlockSpec((None, 1, LANES), lambda g: (N_PROMPT_BLOCKS, 0, 0)),
                   _const_spec((T_SAMPLE, D_MODEL)), _const_spec((1, LANES))],
        out_shape=[jax.ShapeDtypeStruct((T_ALL, D_MODEL), F32), jax.ShapeDtypeStruct((T_ALL, ROW_TILE, LANES), BF16),
                   jax.ShapeDtypeStruct((8, T_ALL), jnp.int32), jax.ShapeDtypeStruct((T_ALL, LANES), F32),
                   jax.ShapeDtypeStruct((N_ROW_BLOCKS, 1, LANES), F32),
                   jax.ShapeDtypeStruct((T_SAMPLE, D_MODEL), F32), jax.ShapeDtypeStruct((1, LANES), F32)],
        input_output_aliases={n_in: 0, n_in + 1: 1, n_in + 2: 2, n_in + 3: 3, n_in + 4: 4},
        compiler_params=_cparams(("arbitrary",)),
        name="mix1_sample",
    )(x_all, state_t, nm, wp, sc, nf, wr, br, cnt, x3_all, h_all, ri_all, rg_all, tc_all)


def _router_weights(wg, bg, we, be):
    w = jnp.concatenate([wg, jnp.transpose(we, (1, 0, 2)).reshape(D_MODEL, N_EXPERTS)], axis=1)
    b = jnp.concatenate([bg, be.reshape(N_EXPERTS)])
    pad = LANES - N_GROUPS - N_EXPERTS
    w = jnp.pad(w, ((0, 0), (0, pad)))
    w_hi = w.astype(BF16)
    w_lo = (w - w_hi.astype(F32)).astype(BF16)
    return jnp.concatenate([w_hi, w_lo], axis=1), jnp.pad(b, (0, pad)).reshape(1, LANES)


def kernel(x_prompt, x_sample, cache_k_win, cache_v_win, state_pool, norm_mix, norm_ffn, norm_final, w_in,
           a_ln_g, a_ln_b, a_w_s, a_b_s, b_sinks, rel_bias_table, w_out, c_w_pool, c_scale,
           router_group_w, router_group_b, router_expert_w, router_expert_b, w_gate, w_up, w_down):
    xs_t = jnp.transpose(x_sample, (1, 0, 2)).reshape(T_SAMPLE, D_MODEL)
    xp2 = x_prompt.reshape(T_PROMPT, D_MODEL)
    win =w_in[0].astype(BF16)
    wout = w_out[0].astype(BF16)
    lng = a_ln_g[0].reshape(1, A_WIDTH)
    lnb = a_ln_b[0].reshape(1, A_WIDTH)
    bias_p, bias_sc, bias_sn, wsp = _prep(rel_bias_table, b_sinks[0], a_w_s[0])
    bs_full = jnp.repeat(a_b_s[0].T, A_HEAD_DIM, axis=1)
    w4 = jnp.transpose(a_w_s[0][:, :DEC_SEQ, :DEC_SEQ], (1, 2, 0)).reshape(DEC_SEQ * DEC_SEQ, A_HEADS)
    wcoef = jnp.repeat(w4, A_HEAD_DIM, axis=1)
    bcoef = jnp.pad(jnp.repeat(a_b_s[0][:, :DEC_SEQ].T, A_HEAD_DIM, axis=1), ((0, 8 - DEC_SEQ), (0, 0)))
    ck = jnp.transpose(cache_k_win[0], (0, 2, 3, 1))
    cv = jnp.transpose(cache_v_win[0], (0, 2, 3, 1))
    routers = [_router_weights(router_group_w[l], router_group_b[l], router_expert_w[l], router_expert_b[l])
               for l in range(2)]
    nm = [norm_mix[l].reshape(1, D_MODEL) for l in range(2)]
    nf = [norm_ffn[l].reshape(1, D_MODEL) for l in range(2)]

    x1_all, h_all, ri_all, rg_all, tc_all, k_last, v_last, va_last, cnt0 = _mix0_prompt(
        xp2, nm[0], win, lng, lnb, wsp, bs_full, bias_p, wout, nf[0], *routers[0])
    x1_all, h_all, ri_all, rg_all, tc_all, k_new, v_new, va_s, cnt0 = _mix0_sample(
        xs_t, nm[0], win, lng, lnb, wcoef, bcoef, ck, cv, bias_sc, bias_sn, wout, nf[0], *routers[0], cnt0,
        x1_all, h_all, ri_all, rg_all, tc_all)
    cplan0, ys0 = _moe(h_all, ri_all, cnt0, tc_all, w_gate, w_up, w_down, 0)
    x2_all = _combine(cplan0, x1_all, rg_all, ys0)

    wp = c_w_pool[0].astype(BF16)
    sc = c_scale[0].reshape(1, D_MODEL)
    x3_all, h2_all, ri2_all, rg2_all, tc2_all, pool_tail, cnt1 = _mix1_prompt(
        x2_all, nm[1], wp, sc, nf[1], *routers[1])
    state_t = jnp.transpose(state_pool[0], (1, 0, 2))
    x3_all, h2_all, ri2_all, rg2_all, tc2_all, hs1, cnt1 = _mix1_sample(
        x2_all, state_t, nm[1], wp, sc, nf[1], *routers[1], cnt1, x3_all, h2_all, ri2_all, rg2_all, tc2_all)
    cplan1, ys1 = _moe(h2_all, ri2_all, cnt1, tc2_all, w_gate, w_up, w_down, 1)
    y_p, y_s = _final(cplan1, x3_all, rg2_all, ys1, norm_final.reshape(1, D_MODEL))

    def from_tmajor(a, width):
        return jnp.transpose(a.reshape(DEC_SEQ, DEC_BATCH, width), (1, 0, 2))

    y_prompt = y_p.reshape(BATCH, SEQ, D_MODEL)
    y_sample = from_tmajor(y_s, D_MODEL)
    win_k_p = k_last.reshape(1, BATCH, WINDOW, B_KV_HEADS, B_HEAD_DIM)
    win_v_p = v_last.reshape(1, BATCH, WINDOW, B_KV_HEADS, B_HEAD_DIM)
    win_k_s = jnp.transpose(k_new, (0, 3, 1, 2))[None]
    win_v_s = jnp.transpose(v_new, (0, 3, 1, 2))[None]
    chunk_v_p = va_last.reshape(1, BATCH, CHUNK, A_HEADS, A_HEAD_DIM)
    chunk_v_s = from_tmajor(va_s, A_WIDTH).reshape(1, DEC_BATCH, DEC_SEQ, A_HEADS, A_HEAD_DIM)
    pool_p = pool_tail[:, 1:][None]
    pool_s = jnp.concatenate([state_pool[0][:, DEC_SEQ:], from_tmajor(hs1, D_MODEL)], axis=1)[None]
    return (y_prompt, y_sample, win_k_p, win_v_p, win_k_s, win_v_s, chunk_v_p, chunk_v_s, pool_p, pool_s)
```

```python
import functools
import math

import numpy as np
import jax
import jax.numpy as jnp
from jax import lax
from jax.experimental import pallas as pl
from jax.experimental.pallas import tpu as pltpu

F32 = jnp.float32
BF16 = jnp.bfloat16

D_MODEL = 1024
BATCH = 2
SEQ = 8192
DEC_BATCH = 128
DEC_SEQ = 4
A_WIDTH = 512
A_HEADS = 8
A_HEAD_DIM = 64
CHUNK = 128
B_HEADS = 8
B_KV_HEADS = 2
B_HEAD_DIM = 64
B_GROUP = 4
WINDOW = 128
N_BUCKETS = 32
MAX_DISTANCE = WINDOW
Q_WIDTH = 512
KV_WIDTH = 128
IN_WIDTH = 2 * A_WIDTH + Q_WIDTH + 2 * KV_WIDTH
ATTN_SCALE = B_HEAD_DIM ** -0.5
NEG_INF = -1e30
POOL_SIZES = (2, 4, 8, 16)
POOL_GROUP_DIM = 256
POOL_MAX = 16
N_GROUPS = 4
EXPERTS_PER_GROUP = 8
N_EXPERTS = 32
TOP_K = 2
D_EXPERT = 512
EPS = 1e-6

LANES = 128
ROW_TILE = D_MODEL // LANES
T_PROMPT = BATCH * SEQ
T_SAMPLE = DEC_BATCH * DEC_SEQ
T_ALL = T_PROMPT + T_SAMPLE
TM = 512
N_PROMPT_BLOCKS = T_PROMPT // TM
N_ROW_BLOCKS = T_ALL // TM
STEPS_PER_BATCH = SEQ // TM
SUB = TM // WINDOW
N_SLOTS = T_ALL * TOP_K
MOE_BLK = 512
N_MOE_BLOCKS = N_SLOTS // MOE_BLK + N_EXPERTS
N_SORT_ROWS = N_MOE_BLOCKS * MOE_BLK
SAMPLE_GROUP = 8
N_SAMPLE_GROUPS = DEC_BATCH // SAMPLE_GROUP
VMEM_LIMIT = 56 * 1024 * 1024

STACK_HEADS = ((0, 2, 5, 7), (1, 3, 4, 6))


def _t5_bucket_np(dist):
    n = np.maximum(dist, 0)
    max_exact = N_BUCKETS // 2
    nf = np.maximum(n, 1).astype(np.float32)
    large = max_exact + (np.log(nf / np.float32(max_exact)) / np.float32(math.log(MAX_DISTANCE / max_exact))
                         * np.float32(N_BUCKETS - max_exact)).astype(np.int32)
    large = np.minimum(large, N_BUCKETS - 1)
    return np.where(n < max_exact, n, large).astype(np.int32)


def _bucket_tables():
    qi = np.arange(WINDOW)[:, None]
    ki = np.arange(2 * WINDOW)[None, :]
    dist = qi + WINDOW - ki
    valid = (dist >= 0) & (dist < WINDOW)
    bp = np.where(valid, _t5_bucket_np(dist), -1)
    bp_first = np.where(ki >= WINDOW, bp, -1)
    bkt_p = np.stack([bp_first, bp]).astype(np.int32)

    t = np.repeat(np.arange(DEC_SEQ), SAMPLE_GROUP)[:, None]
    b = np.tile(np.arange(SAMPLE_GROUP), DEC_SEQ)[:, None]
    cb = np.repeat(np.arange(SAMPLE_GROUP), WINDOW)[None, :]
    cj = np.tile(np.arange(WINDOW), SAMPLE_GROUP)[None, :]
    dist_c = t + WINDOW - cj
    valid_c = (cb == b) & (dist_c >= 0) & (dist_c < WINDOW)
    bkt_sc = np.where(valid_c, _t5_bucket_np(dist_c), -1).astype(np.int32)
    nt = np.repeat(np.arange(DEC_SEQ), SAMPLE_GROUP)[None, :]
    nb = np.tile(np.arange(SAMPLE_GROUP), DEC_SEQ)[None, :]
    dist_n = t - nt
    valid_n = (nb == b) & (dist_n >= 0)
    bkt_sn = np.where(valid_n, _t5_bucket_np(dist_n), -1).astype(np.int32)
    bkt_sn = np.concatenate([bkt_sn, np.full((32, LANES - 32), -1, np.int32)], axis=1)
    return bkt_p, bkt_sc, bkt_sn


_BKT_P, _BKT_SC, _BKT_SN = _bucket_tables()


def _cparams(semantics):
    return pltpu.CompilerParams(dimension_semantics=semantics, vmem_limit_bytes=VMEM_LIMIT)


def _rms(x, g):
    return x * lax.rsqrt(jnp.mean(x * x, axis=-1, keepdims=True) + EPS) * g


def _layernorm(x, g, b):
    xc = x - jnp.mean(x, axis=-1, keepdims=True)
    return xc * lax.rsqrt(jnp.mean(xc * xc, axis=-1, keepdims=True) + EPS) * g + b


def _dot(a, b):
    return jnp.dot(a, b, preferred_element_type=F32)


def _dot_nt(a, b):
    return lax.dot_general(a, b, (((1,), (1,)), ((), ())), preferred_element_type=F32)


def _project(x, nm, win, lng, lnb):
    h = _rms(x, nm)
    z = _dot(h.astype(BF16), win)
    u = jax.nn.gelu(z[:, :A_WIDTH])
    va = _layernorm(jax.nn.gelu(z[:, A_WIDTH:2 * A_WIDTH]), lng, lnb)
    q = z[:, 2 * A_WIDTH:2 * A_WIDTH + Q_WIDTH] * ATTN_SCALE
    k = z[:, 2 * A_WIDTH + Q_WIDTH:2 * A_WIDTH + Q_WIDTH + KV_WIDTH]
    v = z[:, 2 * A_WIDTH + Q_WIDTH + KV_WIDTH:]
    return u, va, q, k, v


def _route(x1, nf, wr, br):
    hf = _rms(x1, nf)
    h = hf.astype(BF16)
    h_lo = (hf - h.astype(F32)).astype(BF16)
    part = _dot(h, wr)
    logits = part[:, :LANES] + part[:, LANES:] + _dot(h_lo, wr[:, :LANES]) + br
    rows = logits.shape[0]
    lane = lax.broadcasted_iota(jnp.int32, (rows, LANES), 1)
    lanef = lane.astype(F32)
    big = jnp.float32(1e9)
    is_g = lane < N_GROUPS
    gl = jnp.where(is_g, logits, -jnp.inf)
    gmax = jnp.max(gl, axis=1, keepdims=True)
    gsel = jnp.min(jnp.where(gl == gmax, lanef, big), axis=1, keepdims=True)
    gsum = jnp.sum(jnp.where(is_g, jnp.exp(logits - gmax), 0.0), axis=1, keepdims=True)
    g1 = 1.0 / gsum
    lo = N_GROUPS + EXPERTS_PER_GROUP * gsel
    emask = (lanef >= lo) & (lanef < lo + EXPERTS_PER_GROUP)
    el = jnp.where(emask, logits, -jnp.inf)
    v1 = jnp.max(el, axis=1, keepdims=True)
    i1 = jnp.min(jnp.where(el == v1, lanef, big), axis=1, keepdims=True)
    el2 = jnp.where(lanef == i1, -jnp.inf, el)
    v2 = jnp.max(el2, axis=1, keepdims=True)
    i2 = jnp.min(jnp.where(el2 == v2, lanef, big), axis=1, keepdims=True)
    e2 = jnp.exp(v2 - v1)
    den = 1.0 + e2
    w1 = g1 / den
    w2 = g1 * e2 / den
    ids = jnp.where(lane == 0, i1 - N_GROUPS, jnp.where(lane == 1, i2 - N_GROUPS, 0.0)).astype(jnp.int32)
    gates = jnp.where(lane == 0, w1, jnp.where(lane == 1, w2, 0.0))
    return h, ids, gates


def _rank_pack(ids, cnt_ref, tcnt_ref):
    rows = ids.shape[0]
    lane = lax.broadcasted_iota(jnp.int32, (rows, LANES), 1)
    o0 = (lane == ids[:, 0:1]).astype(F32)
    o1 = (lane == ids[:, 1:2]).astype(F32)
    r = lax.broadcasted_iota(jnp.int32, (rows, rows), 0)
    c = lax.broadcasted_iota(jnp.int32, (rows, rows), 1)
    before = (c < r).astype(BF16)
    p01 = _dot(before, jnp.concatenate([o0, o1], axis=1).astype(BF16))
    p0 = p01[:, :LANES]
    p1 = p01[:, LANES:]
    c0 = jnp.sum(o0, axis=0, keepdims=True)
    c1 = jnp.sum(o1, axis=0, keepdims=True)
    ctile = c0 + c1
    cnt_ref[...] = cnt_ref[...] + ctile
    tcnt_ref[...] = ctile
    inc = jnp.broadcast_to(ctile, (8, LANES))
    lane8 = lax.broadcasted_iota(jnp.int32, (8, LANES), 1)
    for sh in (1, 2, 4, 8, 16, 32, 64):
        inc = inc + jnp.where(lane8 >= sh, pltpu.roll(inc, sh, 1), 0.0)
    start = inc[0:1] - ctile
    lpos0 = jnp.sum(o0 * (start + p0), axis=1, keepdims=True)
    lpos1 = jnp.sum(o1 * (start + c0 + p1), axis=1, keepdims=True)
    idf = ids.astype(F32)
    packed = jnp.where(lane < TOP_K, idf, 0.0)
    for ln, col in ((4, lpos0), (5, lpos1)):
        packed = jnp.where(lane == ln, col, packed)
    return jnp.transpose(packed)[:8].astype(jnp.int32)


def _prep_kernel(tab_ref, sink_ref, bp_ref, bsc_ref, bsn_ref, ws_ref, op_ref, osc_ref, osn_ref, ows_ref):
    def fill(bkt, write, sink_col0):
        col0 = lax.broadcasted_iota(jnp.int32, bkt.shape, 1) == 0
        for st, heads in enumerate(STACK_HEADS):
            for slot, h in enumerate(heads):
                acc = jnp.full(bkt.shape, NEG_INF, F32)
                for b in range(N_BUCKETS):
                    acc = jnp.where(bkt == b, tab_ref[b, h], acc)
                if sink_col0:
                    acc = jnp.where(col0, sink_ref[0, h], acc)
                write(st, slot, acc)

    for var in range(2):
        def wr_p(st, slot, acc, var=var):
            op_ref[var, st, slot * WINDOW:(slot + 1) * WINDOW, :] = acc
        fill(bp_ref[var], wr_p, True)

    rows_s = DEC_SEQ * SAMPLE_GROUP

    def wr_sc(st, slot, acc):
        osc_ref[st, slot * rows_s:(slot + 1) * rows_s, :] = acc
    fill(bsc_ref[...], wr_sc, True)

    def wr_sn(st, slot, acc):
        osn_ref[st, slot * rows_s:(slot + 1) * rows_s, :] = acc
    fill(bsn_ref[...], wr_sn, False)

    r = lax.broadcasted_iota(jnp.int32, (CHUNK, CHUNK), 0)
    c = lax.broadcasted_iota(jnp.int32, (CHUNK, CHUNK), 1)
    for h in range(A_HEADS):
        ows_ref[h // 2, :, (h % 2) * CHUNK:(h % 2 + 1) * CHUNK] = jnp.where(r >= c, ws_ref[h], 0.0).astype(BF16)


def _prep(rel_bias_table, sinks, w_s):
    vm = pl.BlockSpec(memory_space=pltpu.VMEM)
    sm = pl.BlockSpec(memory_space=pltpu.SMEM)
    rows_s = DEC_SEQ * SAMPLE_GROUP
    return pl.pallas_call(
        _prep_kernel,
        in_specs=[sm, sm, vm, vm, vm, vm],
        out_specs=[vm, vm, vm, vm],
        out_shape=[
            jax.ShapeDtypeStruct((2, 2, 4 * WINDOW, 2 * WINDOW), F32),
            jax.ShapeDtypeStruct((2, 4 * rows_s, SAMPLE_GROUP * WINDOW), F32),
            jax.ShapeDtypeStruct((2, 4 * rows_s, LANES), F32),
            jax.ShapeDtypeStruct((A_HEADS // 2, CHUNK, 2 * CHUNK), BF16),
        ],
        name="prep_tables",
    )(rel_bias_table, sinks.reshape(1, B_HEADS), jnp.asarray(_BKT_P), jnp.asarray(_BKT_SC), jnp.asarray(_BKT_SN), w_s)


def _gate_pairs(va_rows, wsp_ref, lane_lo):
    outs = []
    for p in range(A_HEADS // 2):
        vp = va_rows[:, p * LANES:(p + 1) * LANES]
        rhs = jnp.concatenate([jnp.where(lane_lo, vp, 0.0), jnp.where(lane_lo, 0.0, vp)], axis=0).astype(BF16)
        outs.append(_dot(wsp_ref[p], rhs))
    return jnp.concatenate(outs, axis=1)


def _prompt_steps(body, first_row_out):
    def kern(*refs):
        i = pl.program_id(0)

        @pl.when(i < N_PROMPT_BLOCKS)
        def _():
            body(*refs)

        @pl.when(i >= N_PROMPT_BLOCKS)
        def _():
            for r in refs[first_row_out:first_row_out + 5]:
                r[...] = jnp.zeros(r.shape, r.dtype)

    return kern


def _mix0_prompt_kernel(x_ref, nm_ref, win_ref, lng_ref, lnb_ref, wsp_ref, bs_ref, bias_ref,
                        wout_ref, nf_ref, wr_ref, br_ref,
                        x1_ref, h_ref, ri_ref, rg_ref, tc_ref, kl_ref, vl_ref, val_ref, cnt_ref,
                        kprev, vprev, mix_scr):
    @pl.when(pl.program_id(0) == 0)
    def _():
        cnt_ref[...] = jnp.zeros_like(cnt_ref)

    x = x_ref[...]
    u, va, q, k, v = _project(x, nm_ref[...], win_ref[...], lng_ref[...], lnb_ref[...])
    lane_lo = lax.broadcasted_iota(jnp.int32, (WINDOW, LANES), 1) < B_HEAD_DIM
    row0 = lax.broadcasted_iota(jnp.int32, (WINDOW, KV_WIDTH), 0) == 0
    first = pl.program_id(0) % STEPS_PER_BATCH == 0

    @pl.when(first)
    def _():
        kprev[...] = jnp.zeros_like(kprev)
        vprev[...] = jnp.zeros_like(vprev)

    for j in range(SUB):
        rows = slice(j * WINDOW, (j + 1) * WINDOW)
        s_gate = _gate_pairs(va[rows], wsp_ref, lane_lo)
        mix_scr[rows, :A_WIDTH] = u[rows] * (s_gate + bs_ref[...])

        if j == 0:
            kp, vp = kprev[...], vprev[...]
        else:
            prows = slice((j - 1) * WINDOW, j * WINDOW)
            kp, vp = k[prows], v[prows]
        kk = jnp.concatenate([jnp.where(row0, 0.0, kp), k[rows]], axis=0)
        vv = jnp.concatenate([jnp.where(row0, 0.0, vp), v[rows]], axis=0)
        kops = (kk.astype(BF16), pltpu.roll(kk, B_HEAD_DIM, 1).astype(BF16))
        vops = (vv.astype(BF16), pltpu.roll(vv, B_HEAD_DIM, 1).astype(BF16))
        qt = [q[rows, p * LANES:(p + 1) * LANES] for p in range(4)]
        q_even = [jnp.where(lane_lo, t, 0.0) for t in qt]
        q_odd = [jnp.where(lane_lo, 0.0, t) for t in qt]
        stacks = (jnp.concatenate([q_even[0], q_even[1], q_odd[2], q_odd[3]], axis=0),
                  jnp.concatenate([q_odd[0], q_odd[1], q_even[2], q_even[3]], axis=0))
        o = []
        for st in range(2):
            s = _dot_nt(stacks[st].astype(BF16), kops[st])
            if j == 0:
                bias = bias_ref[jnp.where(first, 0, 1), st]
            else:
                bias = bias_ref[1, st]
            s = s + bias
            m = jnp.max(s, axis=-1, keepdims=True)
            p = jnp.exp(s - m)
            den = jnp.sum(p, axis=-1, keepdims=True)
            o.append(_dot(p.astype(BF16), vops[st]) / den)
        oa, ob = o
        sl = [slice(i * WINDOW, (i + 1) * WINDOW) for i in range(4)]
        tiles = (jnp.where(lane_lo, oa[sl[0]], ob[sl[0]]), jnp.where(lane_lo, oa[sl[1]], ob[sl[1]]),
                 jnp.where(lane_lo, ob[sl[2]], oa[sl[2]]), jnp.where(lane_lo, ob[sl[3]], oa[sl[3]]))
        for p in range(4):
            mix_scr[rows, A_WIDTH + p * LANES:A_WIDTH + (p + 1) * LANES] = tiles[p]

    last = slice(TM - WINDOW, TM)
    kprev[...] = k[last]
    vprev[...] = v[last]
    kl_ref[...] = k[last]
    vl_ref[...] = v[last]
    val_ref[...] = va[last]

    x1 = x + _dot(mix_scr[...].astype(BF16), wout_ref[...])
    x1_ref[...] = x1
    h, ids, gates = _route(x1, nf_ref[...], wr_ref[...], br_ref[...])
    h_ref[...] = h.reshape(h_ref.shape)
    ri_ref[...] = _rank_pack(ids, cnt_ref, tc_ref)
    rg_ref[...] = gates


def _const_spec(shape):
    nd = len(shape)
    return pl.BlockSpec(shape, lambda i, _n=nd: (0,) * _n)


def _mix0_prompt(x_all, nm, win, lng, lnb, wsp, bs_full, bias_p, wout, nf, wr, br):
    row_spec = pl.BlockSpec((TM, D_MODEL), lambda i: (i, 0))
    row3_spec = pl.BlockSpec((TM, ROW_TILE, LANES), lambda i: (i, 0, 0))
    lane_spec = pl.BlockSpec((TM, LANES), lambda i: (i, 0))
    last_kv = pl.BlockSpec((None, WINDOW, KV_WIDTH), lambda i: (jnp.minimum(i // STEPS_PER_BATCH, BATCH - 1), 0, 0))
    last_va = pl.BlockSpec((None, WINDOW, A_WIDTH), lambda i: (jnp.minimum(i // STEPS_PER_BATCH, BATCH - 1), 0, 0))
    return pl.pallas_call(
        _prompt_steps(_mix0_prompt_kernel, 12),
        grid=(N_ROW_BLOCKS,),
        in_specs=[pl.BlockSpec((TM, D_MODEL), lambda i: (jnp.minimum(i, N_PROMPT_BLOCKS - 1), 0)),
                  _const_spec((1, D_MODEL)), _const_spec((D_MODEL, IN_WIDTH)),
                  _const_spec((1, A_WIDTH)), _const_spec((1, A_WIDTH)),
                  _const_spec((A_HEADS // 2, CHUNK, 2 * CHUNK)), _const_spec((CHUNK, A_WIDTH)),
                  _const_spec((2, 2, 4 * WINDOW, 2 * WINDOW)),
                  _const_spec((A_WIDTH + Q_WIDTH, D_MODEL)), _const_spec((1, D_MODEL)),
                  _const_spec((D_MODEL, 2 * LANES)), _const_spec((1, LANES))],
        out_specs=[row_spec, row3_spec, pl.BlockSpec((8, TM), lambda i: (0, i)), lane_spec,
                   pl.BlockSpec((None, 1, LANES), lambda i: (i, 0, 0)),
                   last_kv, last_kv, last_va, _const_spec((1, LANES))],
        out_shape=[jax.ShapeDtypeStruct((T_ALL, D_MODEL), F32), jax.ShapeDtypeStruct((T_ALL, ROW_TILE, LANES), BF16),
                   jax.ShapeDtypeStruct((8, T_ALL), jnp.int32), jax.ShapeDtypeStruct((T_ALL, LANES), F32),
                   jax.ShapeDtypeStruct((N_ROW_BLOCKS, 1, LANES), F32),
                   jax.ShapeDtypeStruct((BATCH, WINDOW, KV_WIDTH), F32),
                   jax.ShapeDtypeStruct((BATCH, WINDOW, KV_WIDTH), F32),
                   jax.ShapeDtypeStruct((BATCH, WINDOW, A_WIDTH), F32),
                   jax.ShapeDtypeStruct((1, LANES), F32)],
        scratch_shapes=[pltpu.VMEM((WINDOW, KV_WIDTH), F32), pltpu.VMEM((WINDOW, KV_WIDTH), F32),
                        pltpu.VMEM((TM, D_MODEL), F32)],
        compiler_params=_cparams(("arbitrary",)),
        name="mix0_prompt",
    )(x_all, nm, win, lng, lnb, wsp, bs_full, bias_p, wout, nf, wr, br)


def _mix0_sample_kernel(x_ref, nm_ref, win_ref, lng_ref, lnb_ref, wcoef_ref, bcoef_ref,
                        ck_ref, cv_ref, bsc_ref, bsn_ref,
                        wout_ref, nf_ref, wr_ref, br_ref, cnt_in,
                        x1_in, h_in, ri_in, rg_in, tc_in,
                        x1_ref, h_ref, ri_ref, rg_ref, tc_ref, kn_ref, vn_ref, va_ref, cnt_ref,
                        q_scr, k_scr, v_scr, mix_scr):
    del x1_in, h_in, ri_in, rg_in, tc_in
    g = pl.program_id(0)

    @pl.when(g == 0)
    def _():
        u, va, q, k, v = _project(x_ref[...], nm_ref[...], win_ref[...], lng_ref[...], lnb_ref[...])
        q_scr[...] = q
        k_scr[...] = k
        v_scr[...] = v
        va_ref[...] = va
        for t in range(DEC_SEQ):
            acc = jnp.zeros((DEC_BATCH, A_WIDTH), F32) + bcoef_ref[t:t + 1, :]
            for s in range(t + 1):
                row = t * DEC_SEQ + s
                acc = acc + wcoef_ref[row:row + 1, :] * va[s * DEC_BATCH:(s + 1) * DEC_BATCH]
            mix_scr[t * DEC_BATCH:(t + 1) * DEC_BATCH, :A_WIDTH] = u[t * DEC_BATCH:(t + 1) * DEC_BATCH] * acc

    b0 = pl.multiple_of(g * SAMPLE_GROUP, SAMPLE_GROUP)
    lane_lo = lax.broadcasted_iota(jnp.int32, (DEC_SEQ * SAMPLE_GROUP, LANES), 1) < B_HEAD_DIM

    def grab(ref, width):
        return jnp.concatenate([ref[pl.ds(t * DEC_BATCH + b0, SAMPLE_GROUP), :] for t in range(DEC_SEQ)], axis=0)

    qg = grab(q_scr, Q_WIDTH)
    kn = grab(k_scr, KV_WIDTH)
    vn = grab(v_scr, KV_WIDTH)

    lane_w = lax.broadcasted_iota(jnp.int32, (KV_WIDTH, WINDOW), 1)
    n_new = DEC_SEQ * SAMPLE_GROUP

    def new_window(c_ref, new_rows, w_ref):
        nt = jnp.transpose(jnp.concatenate([new_rows, jnp.zeros((WINDOW - n_new, KV_WIDTH), F32)], axis=0))
        for b in range(SAMPLE_GROUP):
            w = pltpu.roll(c_ref[b].reshape(KV_WIDTH, WINDOW), WINDOW - DEC_SEQ, 1)
            for t in range(DEC_SEQ):
                src = t * SAMPLE_GROUP + b
                dst = WINDOW - DEC_SEQ + t
                w = jnp.where(lane_w == dst, pltpu.roll(nt, (dst - src) % WINDOW, 1), w)
            w_ref[b] = w.reshape(B_KV_HEADS, B_HEAD_DIM, WINDOW)

    new_window(ck_ref, kn, kn_ref)
    new_window(cv_ref, vn, vn_ref)
    ccol0 = lax.broadcasted_iota(jnp.int32, (KV_WIDTH, SAMPLE_GROUP * WINDOW), 1) == 0

    def cache_t(ref):
        t = jnp.concatenate([ref[b].reshape(KV_WIDTH, WINDOW) for b in range(SAMPLE_GROUP)], axis=1)
        return jnp.where(ccol0, 0.0, t)

    def head_swap(t):
        return jnp.concatenate([t[B_HEAD_DIM:], t[:B_HEAD_DIM]], axis=0)

    kct = cache_t(ck_ref)
    vct = cache_t(cv_ref)
    kc_ops = (kct.astype(BF16), head_swap(kct).astype(BF16))
    vc_ops = (vct.astype(BF16), head_swap(vct).astype(BF16))
    kn_ops = (kn.astype(BF16), pltpu.roll(kn, B_HEAD_DIM, 1).astype(BF16))
    vn_ops = (vn.astype(BF16), pltpu.roll(vn, B_HEAD_DIM, 1).astype(BF16))
    qt = [qg[:, p * LANES:(p + 1) * LANES] for p in range(4)]
    q_even = [jnp.where(lane_lo, t, 0.0) for t in qt]
    q_odd = [jnp.where(lane_lo, 0.0, t) for t in qt]
    stacks = (jnp.concatenate([q_even[0], q_even[1], q_odd[2], q_odd[3]], axis=0),
              jnp.concatenate([q_odd[0], q_odd[1], q_even[2], q_even[3]], axis=0))
    o = []
    for st in range(2):
        qs = stacks[st].astype(BF16)
        sc = _dot(qs, kc_ops[st]) + bsc_ref[st]
        sn = _dot_nt(qs, kn_ops[st]) + bsn_ref[st][:, :DEC_SEQ * SAMPLE_GROUP]
        m = jnp.maximum(jnp.max(sc, axis=-1, keepdims=True), jnp.max(sn, axis=-1, keepdims=True))
        pc = jnp.exp(sc - m)
        pn = jnp.exp(sn - m)
        den = jnp.sum(pc, axis=-1, keepdims=True) + jnp.sum(pn, axis=-1, keepdims=True)
        o.append((_dot_nt(pc.astype(BF16), vc_ops[st]) + _dot(pn.astype(BF16), vn_ops[st])) / den)
    oa, ob = o
    n = DEC_SEQ * SAMPLE_GROUP
    sl = [slice(i * n, (i + 1) * n) for i in range(4)]
    tiles = (jnp.where(lane_lo, oa[sl[0]], ob[sl[0]]), jnp.where(lane_lo, oa[sl[1]], ob[sl[1]]),
             jnp.where(lane_lo, ob[sl[2]], oa[sl[2]]), jnp.where(lane_lo, ob[sl[3]], oa[sl[3]]))
    for p in range(4):
        for t in range(DEC_SEQ):
            mix_scr[pl.ds(t * DEC_BATCH + b0, SAMPLE_GROUP), A_WIDTH + p * LANES:A_WIDTH + (p + 1) * LANES] = (
                tiles[p][t * SAMPLE_GROUP:(t + 1) * SAMPLE_GROUP])

    @pl.when(g == N_SAMPLE_GROUPS - 1)
    def _():
        x1 = x_ref[...] + _dot(mix_scr[...].astype(BF16), wout_ref[...])
        x1_ref[...] = x1
        h, ids, gates = _route(x1, nf_ref[...], wr_ref[...], br_ref[...])
        h_ref[...] = h.reshape(h_ref.shape)
        cnt_ref[...] = cnt_in[...]
        ri_ref[...] = _rank_pack(ids, cnt_ref, tc_ref)
        rg_ref[...] = gates


def _mix0_sample(x_all, nm, win, lng, lnb, wcoef, bcoef, ck, cv, bias_sc, bias_sn, wout, nf, wr, br, cnt,
                 x1_all, h_all, ri_all, rg_all, tc_all):
    sample_rows = pl.BlockSpec((TM, D_MODEL), lambda g: (N_PROMPT_BLOCKS, 0))
    sample_rows3 = pl.BlockSpec((TM, ROW_TILE, LANES), lambda g: (N_PROMPT_BLOCKS, 0, 0))
    sample_lanes = pl.BlockSpec((TM, LANES), lambda g: (N_PROMPT_BLOCKS, 0))
    cache_spec = pl.BlockSpec((SAMPLE_GROUP, B_KV_HEADS, B_HEAD_DIM, WINDOW), lambda g: (g, 0, 0, 0))
    anyspec = pl.BlockSpec(memory_space=pl.ANY)
    n_in = 16
    return pl.pallas_call(
        _mix0_sample_kernel,
        grid=(N_SAMPLE_GROUPS,),
        in_specs=[_const_spec((TM, D_MODEL)), _const_spec((1, D_MODEL)), _const_spec((D_MODEL, IN_WIDTH)),
                  _const_spec((1, A_WIDTH)), _const_spec((1, A_WIDTH)),
                  _const_spec((16, A_WIDTH)), _const_spec((8, A_WIDTH)),
                  cache_spec, cache_spec,
                  _const_spec((2, 4 * 32, SAMPLE_GROUP * WINDOW)), _const_spec((2, 4 * 32, LANES)),
                  _const_spec((A_WIDTH + Q_WIDTH, D_MODEL)), _const_spec((1, D_MODEL)),
                  _const_spec((D_MODEL, 2 * LANES)), _const_spec((1, LANES)), _const_spec((1, LANES)),
                  anyspec, anyspec, anyspec, anyspec, anyspec],
        out_specs=[sample_rows, sample_rows3, pl.BlockSpec((8, TM), lambda g: (0, N_PROMPT_BLOCKS)), sample_lanes,
                   pl.BlockSpec((None, 1, LANES), lambda g: (N_PROMPT_BLOCKS, 0, 0)),
                   cache_spec, cache_spec,
                   _const_spec((T_SAMPLE, A_WIDTH)), _const_spec((1, LANES))],
        out_shape=[jax.ShapeDtypeStruct((T_ALL, D_MODEL), F32), jax.ShapeDtypeStruct((T_ALL, ROW_TILE, LANES), BF16),
                   jax.ShapeDtypeStruct((8, T_ALL), jnp.int32), jax.ShapeDtypeStruct((T_ALL, LANES), F32),
                   jax.ShapeDtypeStruct((N_ROW_BLOCKS, 1, LANES), F32),
                   jax.ShapeDtypeStruct((DEC_BATCH, B_KV_HEADS, B_HEAD_DIM, WINDOW), F32),
                   jax.ShapeDtypeStruct((DEC_BATCH, B_KV_HEADS, B_HEAD_DIM, WINDOW), F32),
                   jax.ShapeDtypeStruct((T_SAMPLE, A_WIDTH), F32), jax.ShapeDtypeStruct((1, LANES), F32)],
        scratch_shapes=[pltpu.VMEM((T_SAMPLE, Q_WIDTH), F32), pltpu.VMEM((T_SAMPLE, KV_WIDTH), F32),
                        pltpu.VMEM((T_SAMPLE, KV_WIDTH), F32), pltpu.VMEM((T_SAMPLE, D_MODEL), F32)],
        input_output_aliases={n_in: 0, n_in + 1: 1, n_in + 2: 2, n_in + 3: 3, n_in + 4: 4},
        compiler_params=_cparams(("arbitrary",)),
        name="mix0_sample",
    )(x_all, nm, win, lng, lnb, wcoef, bcoef, ck, cv, bias_sc, bias_sn, wout, nf, wr, br, cnt,
      x1_all, h_all, ri_all, rg_all, tc_all)


def _moe_metadata(rt_all, cnt, tcnt):
    counts = cnt[0, :N_EXPERTS].astype(jnp.int32)
    padded = (counts + MOE_BLK - 1) // MOE_BLK * MOE_BLK
    pad_end = jnp.cumsum(padded)
    pad_start = pad_end - padded
    experts = jnp.arange(N_EXPERTS, dtype=jnp.int32)
    n_valid = (pad_end[-1] // MOE_BLK).astype(jnp.int32).reshape(1)
    blk_start = jnp.arange(N_MOE_BLOCKS, dtype=jnp.int32) * MOE_BLK
    block_e = jnp.minimum(jnp.sum((blk_start[:, None] >= pad_end[None, :]).astype(jnp.int32), axis=1),
                          N_EXPERTS - 1).astype(jnp.int32)
    zero_start = (pad_start + counts).astype(jnp.int32)
    zero_len = (padded - counts).astype(jnp.int32)
    first = (blk_start == pad_start[block_e]).astype(jnp.int32)
    used = counts > 0
    parity = ((jnp.cumsum(used.astype(jnp.int32)) - 1) % 2)[block_e].astype(jnp.int32)
    nearest = lax.cummin(jnp.where(used, experts, N_EXPERTS)[::-1])[::-1]
    next_used = jnp.concatenate([nearest[1:], jnp.full((1,), N_EXPERTS, jnp.int32)])
    nxt = jnp.where(next_used < N_EXPERTS, next_used, -1)[block_e].astype(jnp.int32)
    plan = (block_e, first, parity, nxt, n_valid)
    runs = tcnt[:, 0, :N_EXPERTS].astype(jnp.int32)
    run_dst = pad_start[None, :] + jnp.cumsum(runs, axis=0) - runs
    lpos = rt_all[2 * TOP_K:3 * TOP_K].reshape(N_SLOTS).astype(jnp.int32)
    cplan = (lpos, runs.reshape(-1), run_dst.reshape(-1).astype(jnp.int32))
    dplan = cplan + (jnp.concatenate([zero_start, zero_len, n_valid]),)
    return plan, dplan, cplan


RUN_PIECE = 32


def _for_run_pieces(n, start_piece):
    whole = n // RUN_PIECE

    def body(j, carry):
        start_piece(j * RUN_PIECE, RUN_PIECE)
        return carry

    lax.fori_loop(0, whole, body, 0)
    o = whole * RUN_PIECE
    bit = RUN_PIECE // 2
    while bit >= 1:
        take = (n & bit) != 0

        @pl.when(take)
        def _(o=o, bit=bit):
            start_piece(o, bit)

        o = o + jnp.where(take, bit, 0)
        bit //= 2


def _dispatch_kernel(lpos_ref, run_ref, rdst_ref, zs_ref, h_ref, xs_ref, zero_scr, stage, sem, zsem):
    i = pl.program_id(0)

    @pl.when(i == 0)
    def _():
        zero_scr[...] = jnp.zeros_like(zero_scr)

        def pieces(e, do):
            off = zs_ref[e]
            rem = zs_ref[N_EXPERTS + e]
            bit = MOE_BLK // 2
            while bit >= 1:
                take = (rem & bit) != 0

                @pl.when(take)
                def _(off=off, bit=bit):
                    do(pltpu.make_async_copy(zero_scr.at[pl.ds(0, bit)], xs_ref.at[pl.ds(off, bit)], zsem))

                off = off + jnp.where(take, bit, 0)
                bit //= 2

        def start_e(e, c):
            pieces(e, lambda cp: cp.start())
            return c

        def wait_e(e, c):
            pieces(e, lambda cp: cp.wait())
            return c

        def tail(do):
            def step(b, c):
                do(pltpu.make_async_copy(zero_scr, xs_ref.at[pl.ds(b * MOE_BLK, MOE_BLK)], zsem))
                return c
            return step

        n_valid = zs_ref[2 * N_EXPERTS]
        lax.fori_loop(0, N_EXPERTS, start_e, 0)
        lax.fori_loop(n_valid, N_MOE_BLOCKS, tail(lambda cp: cp.start()), 0)
        lax.fori_loop(0, N_EXPERTS, wait_e, 0)
        lax.fori_loop(n_valid, N_MOE_BLOCKS, tail(lambda cp: cp.wait()), 0)

    base = i * TM

    def place(r, carry):
        row = h_ref[r]
        for kk in range(TOP_K):
            stage[lpos_ref[kk * T_ALL + base + r]] = row
        return carry

    lax.fori_loop(0, TM, place, 0, unroll=8)

    def send_run(e, off):
        n = run_ref[i * N_EXPERTS + e]
        dst = rdst_ref[i * N_EXPERTS + e]
        _for_run_pieces(n, lambda o, size: pltpu.make_async_copy(
            stage.at[pl.ds(off + o, size)], xs_ref.at[pl.ds(dst + o, size)], sem).start(
                priority=size.bit_length() % 2))
        return off + n

    lax.fori_loop(0, N_EXPERTS, send_run, 0)
    pltpu.make_async_copy(stage, xs_ref.at[pl.ds(0, TM * TOP_K)], sem).wait()


def _dispatch(dplan, h_all):
    return pl.pallas_call(
        _dispatch_kernel,
        grid_spec=pltpu.PrefetchScalarGridSpec(
            num_scalar_prefetch=4,
            grid=(N_ROW_BLOCKS,),
            in_specs=[pl.BlockSpec((TM, ROW_TILE, LANES), lambda i, lp, rn, rd, z: (i, 0, 0))],
            out_specs=pl.BlockSpec(memory_space=pl.ANY),
            scratch_shapes=[pltpu.VMEM((MOE_BLK, ROW_TILE, LANES), BF16),
                            pltpu.VMEM((TM * TOP_K, ROW_TILE, LANES), BF16),
                            pltpu.SemaphoreType.DMA(()), pltpu.SemaphoreType.DMA(())],
        ),
        out_shape=jax.ShapeDtypeStruct((N_SORT_ROWS, ROW_TILE, LANES), BF16),
        compiler_params=_cparams(("arbitrary",)),
        name="moe_dispatch",
    )(*dplan, h_all)


def _experts_kernel(layer, be_ref, first_ref, par_ref, nxt_ref, nv_ref,
                    x_ref, wg_hbm, wu_hbm, wd_hbm, y_ref,
                    wg_s, wu_s, wd_s, wg_f, wu_f, wd_f, wsem):
    i = pl.program_id(0)

    def fetch(e, slot):
        return (pltpu.make_async_copy(wg_hbm.at[layer, e], wg_f.at[slot], wsem.at[slot]),
                pltpu.make_async_copy(wu_hbm.at[layer, e], wu_f.at[slot], wsem.at[slot]),
                pltpu.make_async_copy(wd_hbm.at[layer, e], wd_f.at[slot], wsem.at[slot]))

    @pl.when(i < nv_ref[0])
    def _():
        e = be_ref[i]
        slot = par_ref[i]

        @pl.when(i == 0)
        def _():
            for cp in fetch(e, slot):
                cp.start()

        @pl.when(first_ref[i] == 1)
        def _():
            for cp in fetch(e, slot):
                cp.wait()
            wg_s[...] = wg_f[slot].astype(BF16)
            wu_s[...] = wu_f[slot].astype(BF16)
            wd_s[...] = wd_f[slot].astype(BF16)
            nxt = nxt_ref[i]

            @pl.when(nxt >= 0)
            def _():
                for cp in fetch(nxt, 1 - slot):
                    cp.start()

        xb = x_ref[...].reshape(MOE_BLK, D_MODEL)
        a = jax.nn.silu(_dot(xb, wg_s[...])) * _dot(xb, wu_s[...])
        y_ref[...] = _dot(a.astype(BF16), wd_s[...]).reshape(y_ref.shape)

    @pl.when(i >= nv_ref[0])
    def _():
        y_ref[...] = jnp.zeros(y_ref.shape, y_ref.dtype)


def _experts(block_e, first, parity, nxt, n_valid, xs, w_gate, w_up, w_down, layer):
    def blk(i, be, fi, pa, nx, nv):
        return (jnp.maximum(jnp.minimum(i, nv[0] - 1), 0), 0, 0)

    anyspec = pl.BlockSpec(memory_space=pl.ANY)
    return pl.pallas_call(
        functools.partial(_experts_kernel, layer),
        grid_spec=pltpu.PrefetchScalarGridSpec(
            num_scalar_prefetch=5,
            grid=(N_MOE_BLOCKS,),
            in_specs=[pl.BlockSpec((MOE_BLK, ROW_TILE, LANES), blk), anyspec, anyspec, anyspec],
            out_specs=pl.BlockSpec((MOE_BLK, ROW_TILE, LANES), lambda i, be, fi, pa, nx, nv: (i, 0, 0)),
            scratch_shapes=[pltpu.VMEM((D_MODEL, D_EXPERT), BF16), pltpu.VMEM((D_MODEL, D_EXPERT), BF16),
                            pltpu.VMEM((D_EXPERT, D_MODEL), BF16),
                            pltpu.VMEM((2, D_MODEL, D_EXPERT), F32), pltpu.VMEM((2, D_MODEL, D_EXPERT), F32),
                            pltpu.VMEM((2, D_EXPERT, D_MODEL), F32), pltpu.SemaphoreType.DMA((2,))],
        ),
        out_shape=jax.ShapeDtypeStruct((N_SORT_ROWS, ROW_TILE, LANES), F32),
        compiler_params=_cparams(("arbitrary",)),
        name="moe_experts",
    )(block_e, first, parity, nxt, n_valid, xs, w_gate, w_up, w_down)


def _gather_rows(lpos_ref, run_ref, rdst_ref, ys_ref, ystage, ybufs, sem, i):
    def fetch(tile, buf):
        def fetch_run(e, off):
            n = run_ref[tile * N_EXPERTS + e]
            src = rdst_ref[tile * N_EXPERTS + e]
            _for_run_pieces(n, lambda o, size: pltpu.make_async_copy(
                ys_ref.at[pl.ds(src + o, size)], ystage.at[buf, pl.ds(off + o, size)], sem.at[buf]).start(
                    priority=size.bit_length() % 2))
            return off + n

        lax.fori_loop(0, N_EXPERTS, fetch_run, 0)

    def wait(buf):
        pltpu.make_async_copy(ys_ref.at[pl.ds(0, TM * TOP_K)], ystage.at[buf], sem.at[buf]).wait()

    cur = i % 2

    @pl.when(i == 0)
    def _():
        fetch(i, 0)
        wait(0)

        def unplace(r, carry):
            for kk in range(TOP_K):
                ybufs[0][kk, r] = ystage[0, lpos_ref[kk * T_ALL + r]]
            return carry

        lax.fori_loop(0, TM, unplace, 0, unroll=8)
        fetch(i + 1, 1)

    @pl.when(i + 1 < N_ROW_BLOCKS)
    def _():
        wait(1 - cur)

    @pl.when(i + 2 < N_ROW_BLOCKS)
    def _():
        fetch(i + 2, cur)

    def pieces(compute, store):
        nxt = jnp.minimum(i + 1, N_ROW_BLOCKS - 1)

        def variant(par):
            ycur, ynext = ybufs[par], ybufs[1 - par]

            def piece(j, carry):
                rows = pl.ds(pl.multiple_of(j * COMBINE_ROWS, COMBINE_ROWS), COMBINE_ROWS)
                out = compute(rows, ycur[0, rows].reshape(COMBINE_ROWS, D_MODEL),
                              ycur[1, rows].reshape(COMBINE_ROWS, D_MODEL))
                base = nxt * TM + j * COMBINE_ROWS
                for r in range(COMBINE_ROWS):
                    for kk in range(TOP_K):
                        ynext[kk, j * COMBINE_ROWS + r] = ystage[1 - par, lpos_ref[kk * T_ALL + base + r]]
                store(rows, out)
                return carry

            lax.fori_loop(0, TM // COMBINE_ROWS, piece, 0)

        for par in range(2):
            @pl.when(cur == par)
            def _(par=par):
                variant(par)

    return pieces


COMBINE_ROWS = 64


def _combined(x_ref, rg_ref, rows, y0, y1):
    rg = rg_ref[rows, :]
    return x_ref[rows, :] + rg[:, 0:1] * y0 + rg[:, 1:2] * y1


_COMBINE_SCRATCH = [pltpu.VMEM((2, TM * TOP_K, ROW_TILE, LANES), F32),
                    pltpu.VMEM((TOP_K, TM, ROW_TILE, LANES), F32), pltpu.VMEM((TOP_K, TM, ROW_TILE, LANES), F32),
                    pltpu.SemaphoreType.DMA((2,))]


def _combine_kernel(lpos_ref, run_ref, rdst_ref, x_ref, rg_ref, ys_ref, o_ref, ystage, ybuf0, ybuf1, sem):
    pieces = _gather_rows(lpos_ref, run_ref, rdst_ref, ys_ref, ystage, (ybuf0, ybuf1), sem, pl.program_id(0))

    def store(rows, out):
        o_ref[rows, :] = out

    pieces(functools.partial(_combined, x_ref, rg_ref), store)


def _combine(cplan, x_all, rg_all, ys):
    return pl.pallas_call(
        _combine_kernel,
        grid_spec=pltpu.PrefetchScalarGridSpec(
            num_scalar_prefetch=3,
            grid=(N_ROW_BLOCKS,),
            in_specs=[pl.BlockSpec((TM, D_MODEL), lambda i, a, b, c: (i, 0)),
                      pl.BlockSpec((TM, LANES), lambda i, a, b, c: (i, 0)),
                      pl.BlockSpec(memory_space=pl.ANY)],
            out_specs=pl.BlockSpec((TM, D_MODEL), lambda i, a, b, c: (i, 0)),
            scratch_shapes=_COMBINE_SCRATCH,
        ),
        out_shape=jax.ShapeDtypeStruct((T_ALL, D_MODEL), F32),
        compiler_params=_cparams(("arbitrary",)),
        name="moe_combine",
    )(*cplan, x_all, rg_all, ys)


def _final_kernel(lpos_ref, run_ref, rdst_ref, x_ref, rg_ref, ys_ref, nfin_ref, op_ref, os_ref,
                  ystage, ybuf0, ybuf1, sem):
    i = pl.program_id(0)
    pieces = _gather_rows(lpos_ref, run_ref, rdst_ref, ys_ref, ystage, (ybuf0, ybuf1), sem, i)

    def compute(rows, y0, y1):
        return _rms(_combined(x_ref, rg_ref, rows, y0, y1), nfin_ref[...])

    def store(rows, y):
        @pl.when(i < N_PROMPT_BLOCKS)
        def _():
            op_ref[rows, :] = y

        @pl.when(i >= N_PROMPT_BLOCKS)
        def _():
            os_ref[rows, :] = y

    pieces(compute, store)


def _final(cplan, x_all, rg_all, ys, nfin):
    return pl.pallas_call(
        _final_kernel,
        grid_spec=pltpu.PrefetchScalarGridSpec(
            num_scalar_prefetch=3,
            grid=(N_ROW_BLOCKS,),
            in_specs=[pl.BlockSpec((TM, D_MODEL), lambda i, a, b, c: (i, 0)),
                      pl.BlockSpec((TM, LANES), lambda i, a, b, c: (i, 0)),
                      pl.BlockSpec(memory_space=pl.ANY),
                      pl.BlockSpec((1, D_MODEL), lambda i, a, b, c: (0, 0))],
            out_specs=[pl.BlockSpec((TM, D_MODEL), lambda i, a, b, c: (jnp.minimum(i, N_PROMPT_BLOCKS - 1), 0)),
                       pl.BlockSpec((TM, D_MODEL), lambda i, a, b, c: (0, 0))],
            scratch_shapes=_COMBINE_SCRATCH,
        ),
        out_shape=[jax.ShapeDtypeStruct((T_PROMPT, D_MODEL), F32), jax.ShapeDtypeStruct((T_SAMPLE, D_MODEL), F32)],
        compiler_params=_cparams(("arbitrary",)),
        name="moe_combine_final",
    )(*cplan, x_all, rg_all, ys, nfin)


def _moe(h_all, rt_all, cnt, tcnt, w_gate, w_up, w_down, layer):
    plan, dplan, cplan = _moe_metadata(rt_all, cnt, tcnt)
    xs = _dispatch(dplan, h_all)
    ys = _experts(*plan, xs, w_gate, w_up, w_down, layer)
    return cplan, ys


def _pool_project(d_groups, wp_ref, scale):
    outs = [_dot(d_groups[g].astype(BF16), wp_ref[g]) for g in range(len(POOL_SIZES))]
    return jnp.concatenate(outs, axis=1) * scale


def _mix1_prompt_kernel(x_ref, nm_ref, wp_ref, sc_ref, nf_ref, wr_ref, br_ref,
                        x3_ref, h_ref, ri_ref, rg_ref, tc_ref, pl_ref, cnt_ref, ext):
    i = pl.program_id(0)

    @pl.when(i == 0)
    def _():
        cnt_ref[...] = jnp.zeros_like(cnt_ref)

    x = x_ref[...]
    hp = _rms(x, nm_ref[...])

    @pl.when(i % STEPS_PER_BATCH == 0)
    def _():
        ext[0:POOL_MAX, :] = jnp.zeros((POOL_MAX, D_MODEL), F32)

    ext[POOL_MAX:, :] = hp
    pos = (i % STEPS_PER_BATCH) * TM + lax.broadcasted_iota(jnp.int32, (TM, 1), 0)
    d_groups = []
    for g, w in enumerate(POOL_SIZES):
        cols = slice(g * POOL_GROUP_DIM, (g + 1) * POOL_GROUP_DIM)
        acc = ext[:, cols]
        span = 1
        while span < w:
            acc = acc + pltpu.roll(acc, span, 0)
            span *= 2
        cnt = jnp.minimum(pos + 1, w).astype(F32)
        d_groups.append(acc[POOL_MAX:] / cnt - hp[:, cols])
    tail = hp[TM - POOL_MAX:, :]
    ext[0:POOL_MAX, :] = tail
    pl_ref[...] = tail

    x3 = x + _pool_project(d_groups, wp_ref, sc_ref[...])
    x3_ref[...] = x3
    h, ids, gates = _route(x3, nf_ref[...], wr_ref[...], br_ref[...])
    h_ref[...] = h.reshape(h_ref.shape)
    ri_ref[...] = _rank_pack(ids, cnt_ref, tc_ref)
    rg_ref[...] = gates


def _mix1_prompt(x_all, nm, wp, sc, nf, wr, br):
    row_spec = pl.BlockSpec((TM, D_MODEL), lambda i: (i, 0))
    row3_spec = pl.BlockSpec((TM, ROW_TILE, LANES), lambda i: (i, 0, 0))
    lane_spec = pl.BlockSpec((TM, LANES), lambda i: (i, 0))
    return pl.pallas_call(
        _prompt_steps(_mix1_prompt_kernel, 7),
        grid=(N_ROW_BLOCKS,),
        in_specs=[row_spec, _const_spec((1, D_MODEL)),
                  _const_spec((len(POOL_SIZES), POOL_GROUP_DIM, POOL_GROUP_DIM)), _const_spec((1, D_MODEL)),
                  _const_spec((1, D_MODEL)), _const_spec((D_MODEL, 2 * LANES)), _const_spec((1, LANES))],
        out_specs=[row_spec, row3_spec, pl.BlockSpec((8, TM), lambda i: (0, i)), lane_spec,
                   pl.BlockSpec((None, 1, LANES), lambda i: (i, 0, 0)),
                   pl.BlockSpec((None, POOL_MAX, D_MODEL),
                                lambda i: (jnp.minimum(i // STEPS_PER_BATCH, BATCH - 1), 0, 0)),
                   _const_spec((1, LANES))],
        out_shape=[jax.ShapeDtypeStruct((T_ALL, D_MODEL), F32), jax.ShapeDtypeStruct((T_ALL, ROW_TILE, LANES), BF16),
                   jax.ShapeDtypeStruct((8, T_ALL), jnp.int32), jax.ShapeDtypeStruct((T_ALL, LANES), F32),
                   jax.ShapeDtypeStruct((N_ROW_BLOCKS, 1, LANES), F32),
                   jax.ShapeDtypeStruct((BATCH, POOL_MAX, D_MODEL), F32), jax.ShapeDtypeStruct((1, LANES), F32)],
        scratch_shapes=[pltpu.VMEM((POOL_MAX + TM, D_MODEL), F32)],
        compiler_params=_cparams(("arbitrary",)),
        name="mix1_prompt",
    )(x_all, nm, wp, sc, nf, wr, br)


def _mix1_sample_kernel(x_ref, st_ref, nm_ref, wp_ref, sc_ref, nf_ref, wr_ref, br_ref, cnt_in,
                        x3_in, h_in, ri_in, rg_in, tc_in,
                        x3_ref, h_ref, ri_ref, rg_ref, tc_ref, hs_ref, cnt_ref):
    del x3_in, h_in, ri_in, rg_in, tc_in
    x = x_ref[...]
    hs = _rms(x, nm_ref[...])
    hs_ref[...] = hs
    n_ctx = POOL_MAX - 1
    d_groups = []
    for g, w in enumerate(POOL_SIZES):
        cols = slice(g * POOL_GROUP_DIM, (g + 1) * POOL_GROUP_DIM)
        parts = []
        for t in range(DEC_SEQ):
            acc = hs[t * DEC_BATCH:(t + 1) * DEC_BATCH, cols]
            for back in range(1, w):
                src = t - back
                if src >= 0:
                    acc = acc + hs[src * DEC_BATCH:(src + 1) * DEC_BATCH, cols]
                else:
                    acc = acc + st_ref[n_ctx + src, :, cols]
            parts.append(acc / float(w) - hs[t * DEC_BATCH:(t + 1) * DEC_BATCH, cols])
        d_groups.append(jnp.concatenate(parts, axis=0))
    x3 = x + _pool_project(d_groups, wp_ref, sc_ref[...])
    x3_ref[...] = x3
    h, ids, gates = _route(x3, nf_ref[...], wr_ref[...], br_ref[...])
    h_ref[...] = h.reshape(h_ref.shape)
    cnt_ref[...] = cnt_in[...]
    ri_ref[...] = _rank_pack(ids, cnt_ref, tc_ref)
    rg_ref[...] = gates


def _mix1_sample(x_all, state_t, nm, wp, sc, nf, wr, br, cnt, x3_all, h_all, ri_all, rg_all, tc_all):
    sample_rows = pl.BlockSpec((TM, D_MODEL), lambda g: (N_PROMPT_BLOCKS, 0))
    sample_rows3 = pl.BlockSpec((TM, ROW_TILE, LANES), lambda g: (N_PROMPT_BLOCKS, 0, 0))
    sample_lanes = pl.BlockSpec((TM, LANES), lambda g: (N_PROMPT_BLOCKS, 0))
    anyspec = pl.BlockSpec(memory_space=pl.ANY)
    n_in = 9
    return pl.pallas_call(
        _mix1_sample_kernel,
        grid=(1,),
        in_specs=[sample_rows, _const_spec((POOL_MAX - 1, DEC_BATCH, D_MODEL)), _const_spec((1, D_MODEL)),
                  _const_spec((len(POOL_SIZES), POOL_GROUP_DIM, POOL_GROUP_DIM)), _const_spec((1, D_MODEL)),
                  _const_spec((1, D_MODEL)), _const_spec((D_MODEL, 2 * LANES)), _const_spec((1, LANES)),
                  _const_spec((1, LANES)), anyspec, anyspec, anyspec, anyspec, anyspec],
        out_specs=[sample_rows, sample_rows3, pl.BlockSpec((8, TM), lambda g: (0, N_PROMPT_BLOCKS)), sample_lanes,
                   pl.BlockSpec((None, 1, LANES), lambda g: (N_PROMPT_BLOCKS, 0, 0)),
                   _const_spec((T_SAMPLE, D_MODEL)), _const_spec((1, LANES))],
        out_shape=[jax.ShapeDtypeStruct((T_ALL, D_MODEL), F32), jax.ShapeDtypeStruct((T_ALL, ROW_TILE, LANES), BF16),
                   jax.ShapeDtypeStruct((8, T_ALL), jnp.int32), jax.ShapeDtypeStruct((T_ALL, LANES), F32),
                   jax.ShapeDtypeStruct((N_ROW_BLOCKS, 1, LANES), F32),
                   jax.ShapeDtypeStruct((T_SAMPLE, D_MODEL), F32), jax.ShapeDtypeStruct((1, LANES), F32)],
        input_output_aliases={n_in: 0, n_in + 1: 1, n_in + 2: 2, n_in + 3: 3, n_in + 4: 4},
        compiler_params=_cparams(("arbitrary",)),
        name="mix1_sample",
    )(x_all, state_t, nm, wp, sc, nf, wr, br, cnt, x3_all, h_all, ri_all, rg_all, tc_all)


def _router_weights(wg, bg, we, be):
    w = jnp.concatenate([wg, jnp.transpose(we, (1, 0, 2)).reshape(D_MODEL, N_EXPERTS)], axis=1)
    b = jnp.concatenate([bg, be.reshape(N_EXPERTS)])
    pad = LANES - N_GROUPS - N_EXPERTS
    w = jnp.pad(w, ((0, 0), (0, pad)))
    w_hi = w.astype(BF16)
    w_lo = (w - w_hi.astype(F32)).astype(BF16)
    return jnp.concatenate([w_hi, w_lo], axis=1), jnp.pad(b, (0, pad)).reshape(1, LANES)


def kernel(x_prompt, x_sample, cache_k_win, cache_v_win, state_pool, norm_mix, norm_ffn, norm_final, w_in,
           a_ln_g, a_ln_b, a_w_s, a_b_s, b_sinks, rel_bias_table, w_out, c_w_pool, c_scale,
           router_group_w, router_group_b, router_expert_w, router_expert_b, w_gate, w_up, w_down):
    xs_t = jnp.transpose(x_sample, (1, 0, 2)).reshape(T_SAMPLE, D_MODEL)
    xp2 = x_prompt.reshape(T_PROMPT, D_MODEL)
    win =w_in[0].astype(BF16)
    wout = w_out[0].astype(BF16)
    lng = a_ln_g[0].reshape(1, A_WIDTH)
    lnb = a_ln_b[0].reshape(1, A_WIDTH)
    bias_p, bias_sc, bias_sn, wsp = _prep(rel_bias_table, b_sinks[0], a_w_s[0])
    bs_full = jnp.repeat(a_b_s[0].T, A_HEAD_DIM, axis=1)
    w4 = jnp.transpose(a_w_s[0][:, :DEC_SEQ, :DEC_SEQ], (1, 2, 0)).reshape(DEC_SEQ * DEC_SEQ, A_HEADS)
    wcoef = jnp.repeat(w4, A_HEAD_DIM, axis=1)
    bcoef = jnp.pad(jnp.repeat(a_b_s[0][:, :DEC_SEQ].T, A_HEAD_DIM, axis=1), ((0, 8 - DEC_SEQ), (0, 0)))
    ck = jnp.transpose(cache_k_win[0], (0, 2, 3, 1))
    cv = jnp.transpose(cache_v_win[0], (0, 2, 3, 1))
    routers = [_router_weights(router_group_w[l], router_group_b[l], router_expert_w[l], router_expert_b[l])
               for l in range(2)]
    nm = [norm_mix[l].reshape(1, D_MODEL) for l in range(2)]
    nf = [norm_ffn[l].reshape(1, D_MODEL) for l in range(2)]

    x1_all, h_all, ri_all, rg_all, tc_all, k_last, v_last, va_last, cnt0 = _mix0_prompt(
        xp2, nm[0], win, lng, lnb, wsp, bs_full, bias_p, wout, nf[0], *routers[0])
    x1_all, h_all, ri_all, rg_all, tc_all, k_new, v_new, va_s, cnt0 = _mix0_sample(
        xs_t, nm[0], win, lng, lnb, wcoef, bcoef, ck, cv, bias_sc, bias_sn, wout, nf[0], *routers[0], cnt0,
        x1_all, h_all, ri_all, rg_all, tc_all)
    cplan0, ys0 = _moe(h_all, ri_all, cnt0, tc_all, w_gate, w_up, w_down, 0)
    x2_all = _combine(cplan0, x1_all, rg_all, ys0)

    wp = c_w_pool[0].astype(BF16)
    sc = c_scale[0].reshape(1, D_MODEL)
    x3_all, h2_all, ri2_all, rg2_all, tc2_all, pool_tail, cnt1 = _mix1_prompt(
        x2_all, nm[1], wp, sc, nf[1], *routers[1])
    state_t = jnp.transpose(state_pool[0], (1, 0, 2))
    x3_all, h2_all, ri2_all, rg2_all, tc2_all, hs1, cnt1 = _mix1_sample(
        x2_all, state_t, nm[1], wp, sc, nf[1], *routers[1], cnt1, x3_all, h2_all, ri2_all, rg2_all, tc2_all)
    cplan1, ys1 = _moe(h2_all, ri2_all, cnt1, tc2_all, w_gate, w_up, w_down, 1)
    y_p, y_s = _final(cplan1, x3_all, rg2_all, ys1, norm_final.reshape(1, D_MODEL))

    def from_tmajor(a, width):
        return jnp.transpose(a.reshape(DEC_SEQ, DEC_BATCH, width), (1, 0, 2))

    y_prompt = y_p.reshape(BATCH, SEQ, D_MODEL)
    y_sample = from_tmajor(y_s, D_MODEL)
    win_k_p = k_last.reshape(1, BATCH, WINDOW, B_KV_HEADS, B_HEAD_DIM)
    win_v_p = v_last.reshape(1, BATCH, WINDOW, B_KV_HEADS, B_HEAD_DIM)
    win_k_s = jnp.transpose(k_new, (0, 3, 1, 2))[None]
    win_v_s = jnp.transpose(v_new, (0, 3, 1, 2))[None]
    chunk_v_p = va_last.reshape(1, BATCH, CHUNK, A_HEADS, A_HEAD_DIM)
    chunk_v_s = from_tmajor(va_s, A_WIDTH).reshape(1, DEC_BATCH, DEC_SEQ, A_HEADS, A_HEAD_DIM)
    pool_p = pool_tail[:, 1:][None]
    pool_s = jnp.concatenate([state_pool[0][:, DEC_SEQ:], from_tmajor(hs1, D_MODEL)], axis=1)[None]
    return (y_prompt, y_sample, win_k_p, win_v_p, win_k_s, win_v_s, chunk_v_p, chunk_v_s, pool_p, pool_s)
```

```python
import functools
import math

import numpy as np
import jax
import jax.numpy as jnp
from jax import lax
from jax.experimental import pallas as pl
from jax.experimental.pallas import tpu as pltpu

F32 = jnp.float32
BF16 = jnp.bfloat16

D_MODEL = 1024
BATCH = 2
SEQ = 8192
DEC_BATCH = 128
DEC_SEQ = 4
A_WIDTH = 512
A_HEADS = 8
A_HEAD_DIM = 64
CHUNK = 128
B_HEADS = 8
B_KV_HEADS = 2
B_HEAD_DIM = 64
B_GROUP = 4
WINDOW = 128
N_BUCKETS = 32
MAX_DISTANCE = WINDOW
Q_WIDTH = 512
KV_WIDTH = 128
IN_WIDTH = 2 * A_WIDTH + Q_WIDTH + 2 * KV_WIDTH
ATTN_SCALE = B_HEAD_DIM ** -0.5
NEG_INF = -1e30
POOL_SIZES = (2, 4, 8, 16)
POOL_GROUP_DIM = 256
POOL_MAX = 16
N_GROUPS = 4
EXPERTS_PER_GROUP = 8
N_EXPERTS = 32
TOP_K = 2
D_EXPERT = 512
EPS = 1e-6

LANES = 128
ROW_TILE = D_MODEL // LANES
T_PROMPT = BATCH * SEQ
T_SAMPLE = DEC_BATCH * DEC_SEQ
T_ALL = T_PROMPT + T_SAMPLE
TM = 512
N_PROMPT_BLOCKS = T_PROMPT // TM
N_ROW_BLOCKS = T_ALL // TM
STEPS_PER_BATCH = SEQ // TM
SUB = TM // WINDOW
N_SLOTS = T_ALL * TOP_K
MOE_BLK = 512
N_MOE_BLOCKS = N_SLOTS // MOE_BLK + N_EXPERTS
N_SORT_ROWS = N_MOE_BLOCKS * MOE_BLK
SAMPLE_GROUP = 8
N_SAMPLE_GROUPS = DEC_BATCH // SAMPLE_GROUP
VMEM_LIMIT = 56 * 1024 * 1024

STACK_HEADS = ((0, 2, 5, 7), (1, 3, 4, 6))


def _t5_bucket_np(dist):
    n = np.maximum(dist, 0)
    max_exact = N_BUCKETS // 2
    nf = np.maximum(n, 1).astype(np.float32)
    large = max_exact + (np.log(nf / np.float32(max_exact)) / np.float32(math.log(MAX_DISTANCE / max_exact))
                         * np.float32(N_BUCKETS - max_exact)).astype(np.int32)
    large = np.minimum(large, N_BUCKETS - 1)
    return np.where(n < max_exact, n, large).astype(np.int32)


def _bucket_tables():
    qi = np.arange(WINDOW)[:, None]
    ki = np.arange(2 * WINDOW)[None, :]
    dist = qi + WINDOW - ki
    valid = (dist >= 0) & (dist < WINDOW)
    bp = np.where(valid, _t5_bucket_np(dist), -1)
    bp_first = np.where(ki >= WINDOW, bp, -1)
    bkt_p = np.stack([bp_first, bp]).astype(np.int32)

    t = np.repeat(np.arange(DEC_SEQ), SAMPLE_GROUP)[:, None]
    b = np.tile(np.arange(SAMPLE_GROUP), DEC_SEQ)[:, None]
    cb = np.repeat(np.arange(SAMPLE_GROUP), WINDOW)[None, :]
    cj = np.tile(np.arange(WINDOW), SAMPLE_GROUP)[None, :]
    dist_c = t + WINDOW - cj
    valid_c = (cb == b) & (dist_c >= 0) & (dist_c < WINDOW)
    bkt_sc = np.where(valid_c, _t5_bucket_np(dist_c), -1).astype(np.int32)
    nt = np.repeat(np.arange(DEC_SEQ), SAMPLE_GROUP)[None, :]
    nb = np.tile(np.arange(SAMPLE_GROUP), DEC_SEQ)[None, :]
    dist_n = t - nt
    valid_n = (nb == b) & (dist_n >= 0)
    bkt_sn = np.where(valid_n, _t5_bucket_np(dist_n), -1).astype(np.int32)
    bkt_sn = np.concatenate([bkt_sn, np.full((32, LANES - 32), -1, np.int32)], axis=1)
    return bkt_p, bkt_sc, bkt_sn


_BKT_P, _BKT_SC, _BKT_SN = _bucket_tables()


def _cparams(semantics):
    return pltpu.CompilerParams(dimension_semantics=semantics, vmem_limit_bytes=VMEM_LIMIT)


def _rms(x, g):
    return x * lax.rsqrt(jnp.mean(x * x, axis=-1, keepdims=True) + EPS) * g


def _layernorm(x, g, b):
    xc = x - jnp.mean(x, axis=-1, keepdims=True)
    return xc * lax.rsqrt(jnp.mean(xc * xc, axis=-1, keepdims=True) + EPS) * g + b


def _dot(a, b):
    return jnp.dot(a, b, preferred_element_type=F32)


def _dot_nt(a, b):
    return lax.dot_general(a, b, (((1,), (1,)), ((), ())), preferred_element_type=F32)


def _project(x, nm, win, lng, lnb):
    h = _rms(x, nm)
    z = _dot(h.astype(BF16), win)
    u = jax.nn.gelu(z[:, :A_WIDTH])
    va = _layernorm(jax.nn.gelu(z[:, A_WIDTH:2 * A_WIDTH]), lng, lnb)
    q = z[:, 2 * A_WIDTH:2 * A_WIDTH + Q_WIDTH] * ATTN_SCALE
    k = z[:, 2 * A_WIDTH + Q_WIDTH:2 * A_WIDTH + Q_WIDTH + KV_WIDTH]
    v = z[:, 2 * A_WIDTH + Q_WIDTH + KV_WIDTH:]
    return u, va, q, k, v


def _route(x1, nf, wr, br):
    hf = _rms(x1, nf)
    h = hf.astype(BF16)
    h_lo = (hf - h.astype(F32)).astype(BF16)
    part = _dot(h, wr)
    logits = part[:, :LANES] + part[:, LANES:] + _dot(h_lo, wr[:, :LANES]) + br
    rows = logits.shape[0]
    lane = lax.broadcasted_iota(jnp.int32, (rows, LANES), 1)
    lanef = lane.astype(F32)
    big = jnp.float32(1e9)
    is_g = lane < N_GROUPS
    gl = jnp.where(is_g, logits, -jnp.inf)
    gmax = jnp.max(gl, axis=1, keepdims=True)
    gsel = jnp.min(jnp.where(gl == gmax, lanef, big), axis=1, keepdims=True)
    gsum = jnp.sum(jnp.where(is_g, jnp.exp(logits - gmax), 0.0), axis=1, keepdims=True)
    g1 = 1.0 / gsum
    lo = N_GROUPS + EXPERTS_PER_GROUP * gsel
    emask = (lanef >= lo) & (lanef < lo + EXPERTS_PER_GROUP)
    el = jnp.where(emask, logits, -jnp.inf)
    v1 = jnp.max(el, axis=1, keepdims=True)
    i1 = jnp.min(jnp.where(el == v1, lanef, big), axis=1, keepdims=True)
    el2 = jnp.where(lanef == i1, -jnp.inf, el)
    v2 = jnp.max(el2, axis=1, keepdims=True)
    i2 = jnp.min(jnp.where(el2 == v2, lanef, big), axis=1, keepdims=True)
    e2 = jnp.exp(v2 - v1)
    den = 1.0 + e2
    w1 = g1 / den
    w2 = g1 * e2 / den
    ids = jnp.where(lane == 0, i1 - N_GROUPS, jnp.where(lane == 1, i2 - N_GROUPS, 0.0)).astype(jnp.int32)
    gates = jnp.where(lane == 0, w1, jnp.where(lane == 1, w2, 0.0))
    return h, ids, gates


def _rank_pack(ids, cnt_ref, tcnt_ref):
    rows = ids.shape[0]
    lane = lax.broadcasted_iota(jnp.int32, (rows, LANES), 1)
    o0 = (lane == ids[:, 0:1]).astype(F32)
    o1 = (lane == ids[:, 1:2]).astype(F32)
    r = lax.broadcasted_iota(jnp.int32, (rows, rows), 0)
    c = lax.broadcasted_iota(jnp.int32, (rows, rows), 1)
    before = (c < r).astype(BF16)
    p01 = _dot(before, jnp.concatenate([o0, o1], axis=1).astype(BF16))
    p0 = p01[:, :LANES]
    p1 = p01[:, LANES:]
    c0 = jnp.sum(o0, axis=0, keepdims=True)
    c1 = jnp.sum(o1, axis=0, keepdims=True)
    ctile = c0 + c1
    cnt_ref[...] = cnt_ref[...] + ctile
    tcnt_ref[...] = ctile
    inc = jnp.broadcast_to(ctile, (8, LANES))
    lane8 = lax.broadcasted_iota(jnp.int32, (8, LANES), 1)
    for sh in (1, 2, 4, 8, 16, 32, 64):
        inc = inc + jnp.where(lane8 >= sh, pltpu.roll(inc, sh, 1), 0.0)
    start = inc[0:1] - ctile
    lpos0 = jnp.sum(o0 * (start + p0), axis=1, keepdims=True)
    lpos1 = jnp.sum(o1 * (start + c0 + p1), axis=1, keepdims=True)
    idf = ids.astype(F32)
    packed = jnp.where(lane < TOP_K, idf, 0.0)
    for ln, col in ((4, lpos0), (5, lpos1)):
        packed = jnp.where(lane == ln, col, packed)
    return jnp.transpose(packed)[:8].astype(jnp.int32)


def _prep_kernel(tab_ref, sink_ref, bp_ref, bsc_ref, bsn_ref, ws_ref, op_ref, osc_ref, osn_ref, ows_ref):
    def fill(bkt, write, sink_col0):
        col0 = lax.broadcasted_iota(jnp.int32, bkt.shape, 1) == 0
        for st, heads in enumerate(STACK_HEADS):
            for slot, h in enumerate(heads):
                acc = jnp.full(bkt.shape, NEG_INF, F32)
                for b in range(N_BUCKETS):
                    acc = jnp.where(bkt == b, tab_ref[b, h], acc)
                if sink_col0:
                    acc = jnp.where(col0, sink_ref[0, h], acc)
                write(st, slot, acc)

    for var in range(2):
        def wr_p(st, slot, acc, var=var):
            op_ref[var, st, slot * WINDOW:(slot + 1) * WINDOW, :] = acc
        fill(bp_ref[var], wr_p, True)

    rows_s = DEC_SEQ * SAMPLE_GROUP

    def wr_sc(st, slot, acc):
        osc_ref[st, slot * rows_s:(slot + 1) * rows_s, :] = acc
    fill(bsc_ref[...], wr_sc, True)

    def wr_sn(st, slot, acc):
        osn_ref[st, slot * rows_s:(slot + 1) * rows_s, :] = acc
    fill(bsn_ref[...], wr_sn, False)

    r = lax.broadcasted_iota(jnp.int32, (CHUNK, CHUNK), 0)
    c = lax.broadcasted_iota(jnp.int32, (CHUNK, CHUNK), 1)
    for h in range(A_HEADS):
        ows_ref[h // 2, :, (h % 2) * CHUNK:(h % 2 + 1) * CHUNK] = jnp.where(r >= c, ws_ref[h], 0.0).astype(BF16)


def _prep(rel_bias_table, sinks, w_s):
    vm = pl.BlockSpec(memory_space=pltpu.VMEM)
    sm = pl.BlockSpec(memory_space=pltpu.SMEM)
    rows_s = DEC_SEQ * SAMPLE_GROUP
    return pl.pallas_call(
        _prep_kernel,
        in_specs=[sm, sm, vm, vm, vm, vm],
        out_specs=[vm, vm, vm, vm],
        out_shape=[
            jax.ShapeDtypeStruct((2, 2, 4 * WINDOW, 2 * WINDOW), F32),
            jax.ShapeDtypeStruct((2, 4 * rows_s, SAMPLE_GROUP * WINDOW), F32),
            jax.ShapeDtypeStruct((2, 4 * rows_s, LANES), F32),
            jax.ShapeDtypeStruct((A_HEADS // 2, CHUNK, 2 * CHUNK), BF16),
        ],
        name="prep_tables",
    )(rel_bias_table, sinks.reshape(1, B_HEADS), jnp.asarray(_BKT_P), jnp.asarray(_BKT_SC), jnp.asarray(_BKT_SN), w_s)


def _gate_pairs(va_rows, wsp_ref, lane_lo):
    outs = []
    for p in range(A_HEADS // 2):
        vp = va_rows[:, p * LANES:(p + 1) * LANES]
        rhs = jnp.concatenate([jnp.where(lane_lo, vp, 0.0), jnp.where(lane_lo, 0.0, vp)], axis=0).astype(BF16)
        outs.append(_dot(wsp_ref[p], rhs))
    return jnp.concatenate(outs, axis=1)


def _stage_rows(idx_s, rows_ref, stg_ref, unrolled):
    def copy(r):
        row = rows_ref[r]
        for kk in range(TOP_K):
            stg_ref[idx_s[2 * TOP_K + kk, r]] = row

    if unrolled:
        for r in range(TM):
            copy(r)
    else:
        def body(r, carry):
            copy(r)
            return carry
        lax.fori_loop(0, TM, body, 0, unroll=8)


def _prompt_steps(body, first_row_out, stg_index):
    def kern(*refs):
        i = pl.program_id(0)
        stg_ref, idx_v, idx_s, hprev, psem = refs[stg_index:stg_index + 5]
        rest = refs[:stg_index] + refs[stg_index + 5:]
        to_smem = pltpu.make_async_copy(idx_v, idx_s, psem)

        @pl.when(i == 0)
        def _():
            hprev[...] = jnp.zeros_like(hprev)

            def init(r, carry):
                for kk in range(TOP_K):
                    idx_s[2 * TOP_K + kk, r] = kk * TM + r
                return carry

            lax.fori_loop(0, TM, init, 0)

        @pl.when(i >= 1)
        def _():
            to_smem.wait()

        @pl.when(i < N_PROMPT_BLOCKS)
        def _():
            _stage_rows(idx_s, hprev, stg_ref, unrolled=True)
            body(*rest, idx_v, hprev)
            to_smem.start()

        @pl.when(i >= N_PROMPT_BLOCKS)
        def _():
            _stage_rows(idx_s, hprev, stg_ref, unrolled=False)
            for r in refs[first_row_out:first_row_out + 5]:
                r[...] = jnp.zeros(r.shape, r.dtype)

    return kern


def _stage_scratch():
    return [pltpu.VMEM((8, TM), jnp.int32), pltpu.SMEM((8, TM), jnp.int32),
            pltpu.VMEM((TM, ROW_TILE, LANES), BF16), pltpu.SemaphoreType.DMA(())]


_STAGED_SPEC = pl.BlockSpec((TM * TOP_K, ROW_TILE, LANES),
                            lambda i: (jnp.clip(i - 1, 0, N_PROMPT_BLOCKS - 1), 0, 0))
_STAGED_SHAPE = jax.ShapeDtypeStruct((T_PROMPT * TOP_K, ROW_TILE, LANES), BF16)


def _mix0_prompt_kernel(x_ref, nm_ref, win_ref, lng_ref, lnb_ref, wsp_ref, bs_ref, bias_ref,
                        wout_ref, nf_ref, wr_ref, br_ref,
                        x1_ref, h_ref, ri_ref, rg_ref, tc_ref, kl_ref, vl_ref, val_ref, cnt_ref,
                        kprev, vprev, mix_scr, idx_v, hprev):
    @pl.when(pl.program_id(0) == 0)
    def _():
        cnt_ref[...] = jnp.zeros_like(cnt_ref)

    x = x_ref[...]
    u, va, q, k, v = _project(x, nm_ref[...], win_ref[...], lng_ref[...], lnb_ref[...])
    lane_lo = lax.broadcasted_iota(jnp.int32, (WINDOW, LANES), 1) < B_HEAD_DIM
    row0 = lax.broadcasted_iota(jnp.int32, (WINDOW, KV_WIDTH), 0) == 0
    first = pl.program_id(0) % STEPS_PER_BATCH == 0

    @pl.when(first)
    def _():
        kprev[...] = jnp.zeros_like(kprev)
        vprev[...] = jnp.zeros_like(vprev)

    for j in range(SUB):
        rows = slice(j * WINDOW, (j + 1) * WINDOW)
        s_gate = _gate_pairs(va[rows], wsp_ref, lane_lo)
        mix_scr[rows, :A_WIDTH] = u[rows] * (s_gate + bs_ref[...])

        if j == 0:
            kp, vp = kprev[...], vprev[...]
        else:
            prows = slice((j - 1) * WINDOW, j * WINDOW)
            kp, vp = k[prows], v[prows]
        kk = jnp.concatenate([jnp.where(row0, 0.0, kp), k[rows]], axis=0)
        vv = jnp.concatenate([jnp.where(row0, 0.0, vp), v[rows]], axis=0)
        kops = (kk.astype(BF16), pltpu.roll(kk, B_HEAD_DIM, 1).astype(BF16))
        vops = (vv.astype(BF16), pltpu.roll(vv, B_HEAD_DIM, 1).astype(BF16))
        qt = [q[rows, p * LANES:(p + 1) * LANES] for p in range(4)]
        q_even = [jnp.where(lane_lo, t, 0.0) for t in qt]
        q_odd = [jnp.where(lane_lo, 0.0, t) for t in qt]
        stacks = (jnp.concatenate([q_even[0], q_even[1], q_odd[2], q_odd[3]], axis=0),
                  jnp.concatenate([q_odd[0], q_odd[1], q_even[2], q_even[3]], axis=0))
        o = []
        for st in range(2):
            s = _dot_nt(stacks[st].astype(BF16), kops[st])
            if j == 0:
                bias = bias_ref[jnp.where(first, 0, 1), st]
            else:
                bias = bias_ref[1, st]
            s = s + bias
            m = jnp.max(s, axis=-1, keepdims=True)
            p = jnp.exp(s - m)
            den = jnp.sum(p, axis=-1, keepdims=True)
            o.append(_dot(p.astype(BF16), vops[st]) / den)
        oa, ob = o
        sl = [slice(i * WINDOW, (i + 1) * WINDOW) for i in range(4)]
        tiles = (jnp.where(lane_lo, oa[sl[0]], ob[sl[0]]), jnp.where(lane_lo, oa[sl[1]], ob[sl[1]]),
                 jnp.where(lane_lo, ob[sl[2]], oa[sl[2]]), jnp.where(lane_lo, ob[sl[3]], oa[sl[3]]))
        for p in range(4):
            mix_scr[rows, A_WIDTH + p * LANES:A_WIDTH + (p + 1) * LANES] = tiles[p]

    last = slice(TM - WINDOW, TM)
    kprev[...] = k[last]
    vprev[...] = v[last]
    kl_ref[...] = k[last]
    vl_ref[...] = v[last]
    val_ref[...] = va[last]

    x1 = x + _dot(mix_scr[...].astype(BF16), wout_ref[...])
    x1_ref[...] = x1
    h, ids, gates = _route(x1, nf_ref[...], wr_ref[...], br_ref[...])
    _emit_routing(h, ids, gates, h_ref, ri_ref, rg_ref, tc_ref, cnt_ref, idx_v, hprev)


def _emit_routing(h, ids, gates, h_ref, ri_ref, rg_ref, tc_ref, cnt_ref, idx_v, hprev):
    h3 = h.reshape(h_ref.shape)
    packed = _rank_pack(ids, cnt_ref, tc_ref)
    h_ref[...] = h3
    hprev[...] = h3
    ri_ref[...] = packed
    idx_v[...] = packed
    rg_ref[...] = gates


def _emit_and_stage(h, ids, gates, h_ref, ri_ref, rg_ref, tc_ref, cnt_ref, stg_ref, idx_v, idx_s, psem):
    _emit_routing(h, ids, gates, h_ref, ri_ref, rg_ref, tc_ref, cnt_ref, idx_v, h_ref)
    to_smem = pltpu.make_async_copy(idx_v, idx_s, psem)
    to_smem.start()
    to_smem.wait()
    _stage_rows(idx_s, h_ref, stg_ref, unrolled=False)


def _const_spec(shape):
    nd = len(shape)
    return pl.BlockSpec(shape, lambda i, _n=nd: (0,) * _n)


def _mix0_prompt(x_all, nm, win, lng, lnb, wsp, bs_full, bias_p, wout, nf, wr, br):
    row_spec = pl.BlockSpec((TM, D_MODEL), lambda i: (i, 0))
    row3_spec = pl.BlockSpec((TM, ROW_TILE, LANES), lambda i: (i, 0, 0))
    lane_spec = pl.BlockSpec((TM, LANES), lambda i: (i, 0))
    last_kv = pl.BlockSpec((None, WINDOW, KV_WIDTH), lambda i: (jnp.minimum(i // STEPS_PER_BATCH, BATCH - 1), 0, 0))
    last_va = pl.BlockSpec((None, WINDOW, A_WIDTH), lambda i: (jnp.minimum(i // STEPS_PER_BATCH, BATCH - 1), 0, 0))
    return pl.pallas_call(
        _prompt_steps(_mix0_prompt_kernel, 12, 12 + 9),
        grid=(N_ROW_BLOCKS,),
        in_specs=[pl.BlockSpec((TM, D_MODEL), lambda i: (jnp.minimum(i, N_PROMPT_BLOCKS - 1), 0)),
                  _const_spec((1, D_MODEL)), _const_spec((D_MODEL, IN_WIDTH)),
                  _const_spec((1, A_WIDTH)), _const_spec((1, A_WIDTH)),
                  _const_spec((A_HEADS // 2, CHUNK, 2 * CHUNK)), _const_spec((CHUNK, A_WIDTH)),
                  _const_spec((2, 2, 4 * WINDOW, 2 * WINDOW)),
                  _const_spec((A_WIDTH + Q_WIDTH, D_MODEL)), _const_spec((1, D_MODEL)),
                  _const_spec((D_MODEL, 2 * LANES)), _const_spec((1, LANES))],
        out_specs=[row_spec, row3_spec, pl.BlockSpec((8, TM), lambda i: (0, i)), lane_spec,
                   pl.BlockSpec((None, 1, LANES), lambda i: (i, 0, 0)),
                   last_kv, last_kv, last_va, _const_spec((1, LANES)), _STAGED_SPEC],
        out_shape=[jax.ShapeDtypeStruct((T_ALL, D_MODEL), F32), jax.ShapeDtypeStruct((T_ALL, ROW_TILE, LANES), BF16),
                   jax.ShapeDtypeStruct((8, T_ALL), jnp.int32), jax.ShapeDtypeStruct((T_ALL, LANES), F32),
                   jax.ShapeDtypeStruct((N_ROW_BLOCKS, 1, LANES), F32),
                   jax.ShapeDtypeStruct((BATCH, WINDOW, KV_WIDTH), F32),
                   jax.ShapeDtypeStruct((BATCH, WINDOW, KV_WIDTH), F32),
                   jax.ShapeDtypeStruct((BATCH, WINDOW, A_WIDTH), F32),
                   jax.ShapeDtypeStruct((1, LANES), F32), _STAGED_SHAPE],
        scratch_shapes=_stage_scratch() + [pltpu.VMEM((WINDOW, KV_WIDTH), F32), pltpu.VMEM((WINDOW, KV_WIDTH), F32),
                                           pltpu.VMEM((TM, D_MODEL), F32)],
        compiler_params=_cparams(("arbitrary",)),
        name="mix0_prompt",
    )(x_all, nm, win, lng, lnb, wsp, bs_full, bias_p, wout, nf, wr, br)


def _mix0_sample_kernel(x_ref, nm_ref, win_ref, lng_ref, lnb_ref, wcoef_ref, bcoef_ref,
                        ck_ref, cv_ref, bsc_ref, bsn_ref,
                        wout_ref, nf_ref, wr_ref, br_ref, cnt_in,
                        x1_in, h_in, ri_in, rg_in, tc_in,
                        x1_ref, h_ref, ri_ref, rg_ref, tc_ref, kn_ref, vn_ref, va_ref, cnt_ref, stg_ref,
                        q_scr, k_scr, v_scr, mix_scr, idx_v, idx_s, psem):
    del x1_in, h_in, ri_in, rg_in, tc_in
    g = pl.program_id(0)

    @pl.when(g == 0)
    def _():
        u, va, q, k, v = _project(x_ref[...], nm_ref[...], win_ref[...], lng_ref[...], lnb_ref[...])
        q_scr[...] = q
        k_scr[...] = k
        v_scr[...] = v
        va_ref[...] = va
        for t in range(DEC_SEQ):
            acc = jnp.zeros((DEC_BATCH, A_WIDTH), F32) + bcoef_ref[t:t + 1, :]
            for s in range(t + 1):
                row = t * DEC_SEQ + s
                acc = acc + wcoef_ref[row:row + 1, :] * va[s * DEC_BATCH:(s + 1) * DEC_BATCH]
            mix_scr[t * DEC_BATCH:(t + 1) * DEC_BATCH, :A_WIDTH] = u[t * DEC_BATCH:(t + 1) * DEC_BATCH] * acc

    b0 = pl.multiple_of(g * SAMPLE_GROUP, SAMPLE_GROUP)
    lane_lo = lax.broadcasted_iota(jnp.int32, (DEC_SEQ * SAMPLE_GROUP, LANES), 1) < B_HEAD_DIM

    def grab(ref, width):
        return jnp.concatenate([ref[pl.ds(t * DEC_BATCH + b0, SAMPLE_GROUP), :] for t in range(DEC_SEQ)], axis=0)

    qg = grab(q_scr, Q_WIDTH)
    kn = grab(k_scr, KV_WIDTH)
    vn = grab(v_scr, KV_WIDTH)

    lane_w = lax.broadcasted_iota(jnp.int32, (KV_WIDTH, WINDOW), 1)
    n_new = DEC_SEQ * SAMPLE_GROUP

    def new_window(c_ref, new_rows, w_ref):
        nt = jnp.transpose(jnp.concatenate([new_rows, jnp.zeros((WINDOW - n_new, KV_WIDTH), F32)], axis=0))
        for b in range(SAMPLE_GROUP):
            w = pltpu.roll(c_ref[b].reshape(KV_WIDTH, WINDOW), WINDOW - DEC_SEQ, 1)
            for t in range(DEC_SEQ):
                src = t * SAMPLE_GROUP + b
                dst = WINDOW - DEC_SEQ + t
                w = jnp.where(lane_w == dst, pltpu.roll(nt, (dst - src) % WINDOW, 1), w)
            w_ref[b] = w.reshape(B_KV_HEADS, B_HEAD_DIM, WINDOW)

    new_window(ck_ref, kn, kn_ref)
    new_window(cv_ref, vn, vn_ref)
    ccol0 = lax.broadcasted_iota(jnp.int32, (KV_WIDTH, SAMPLE_GROUP * WINDOW), 1) == 0

    def cache_t(ref):
        t = jnp.concatenate([ref[b].reshape(KV_WIDTH, WINDOW) for b in range(SAMPLE_GROUP)], axis=1)
        return jnp.where(ccol0, 0.0, t)

    def head_swap(t):
        return jnp.concatenate([t[B_HEAD_DIM:], t[:B_HEAD_DIM]], axis=0)

    kct = cache_t(ck_ref)
    vct = cache_t(cv_ref)
    kc_ops = (kct.astype(BF16), head_swap(kct).astype(BF16))
    vc_ops = (vct.astype(BF16), head_swap(vct).astype(BF16))
    kn_ops = (kn.astype(BF16), pltpu.roll(kn, B_HEAD_DIM, 1).astype(BF16))
    vn_ops = (vn.astype(BF16), pltpu.roll(vn, B_HEAD_DIM, 1).astype(BF16))
    qt = [qg[:, p * LANES:(p + 1) * LANES] for p in range(4)]
    q_even = [jnp.where(lane_lo, t, 0.0) for t in qt]
    q_odd = [jnp.where(lane_lo, 0.0, t) for t in qt]
    stacks = (jnp.concatenate([q_even[0], q_even[1], q_odd[2], q_odd[3]], axis=0),
              jnp.concatenate([q_odd[0], q_odd[1], q_even[2], q_even[3]], axis=0))
    o = []
    for st in range(2):
        qs = stacks[st].astype(BF16)
        sc = _dot(qs, kc_ops[st]) + bsc_ref[st]
        sn = _dot_nt(qs, kn_ops[st]) + bsn_ref[st][:, :DEC_SEQ * SAMPLE_GROUP]
        m = jnp.maximum(jnp.max(sc, axis=-1, keepdims=True), jnp.max(sn, axis=-1, keepdims=True))
        pc = jnp.exp(sc - m)
        pn = jnp.exp(sn - m)
        den = jnp.sum(pc, axis=-1, keepdims=True) + jnp.sum(pn, axis=-1, keepdims=True)
        o.append((_dot_nt(pc.astype(BF16), vc_ops[st]) + _dot(pn.astype(BF16), vn_ops[st])) / den)
    oa, ob = o
    n = DEC_SEQ * SAMPLE_GROUP
    sl = [slice(i * n, (i + 1) * n) for i in range(4)]
    tiles = (jnp.where(lane_lo, oa[sl[0]], ob[sl[0]]), jnp.where(lane_lo, oa[sl[1]], ob[sl[1]]),
             jnp.where(lane_lo, ob[sl[2]], oa[sl[2]]), jnp.where(lane_lo, ob[sl[3]], oa[sl[3]]))
    for p in range(4):
        for t in range(DEC_SEQ):
            mix_scr[pl.ds(t * DEC_BATCH + b0, SAMPLE_GROUP), A_WIDTH + p * LANES:A_WIDTH + (p + 1) * LANES] = (
                tiles[p][t * SAMPLE_GROUP:(t + 1) * SAMPLE_GROUP])

    @pl.when(g == N_SAMPLE_GROUPS - 1)
    def _():
        x1 = x_ref[...] + _dot(mix_scr[...].astype(BF16), wout_ref[...])
        x1_ref[...] = x1
        h, ids, gates = _route(x1, nf_ref[...], wr_ref[...], br_ref[...])
        cnt_ref[...] = cnt_in[...]
        _emit_and_stage(h, ids, gates, h_ref, ri_ref, rg_ref, tc_ref, cnt_ref, stg_ref, idx_v, idx_s, psem)


def _mix0_sample(x_all, nm, win, lng, lnb, wcoef, bcoef, ck, cv, bias_sc, bias_sn, wout, nf, wr, br, cnt,
                 x1_all, h_all, ri_all, rg_all, tc_all):
    sample_rows = pl.BlockSpec((TM, D_MODEL), lambda g: (N_PROMPT_BLOCKS, 0))
    sample_rows3 = pl.BlockSpec((TM, ROW_TILE, LANES), lambda g: (N_PROMPT_BLOCKS, 0, 0))
    sample_lanes = pl.BlockSpec((TM, LANES), lambda g: (N_PROMPT_BLOCKS, 0))
    cache_spec = pl.BlockSpec((SAMPLE_GROUP, B_KV_HEADS, B_HEAD_DIM, WINDOW), lambda g: (g, 0, 0, 0))
    anyspec = pl.BlockSpec(memory_space=pl.ANY)
    n_in = 16
    return pl.pallas_call(
        _mix0_sample_kernel,
        grid=(N_SAMPLE_GROUPS,),
        in_specs=[_const_spec((TM, D_MODEL)), _const_spec((1, D_MODEL)), _const_spec((D_MODEL, IN_WIDTH)),
                  _const_spec((1, A_WIDTH)), _const_spec((1, A_WIDTH)),
                  _const_spec((16, A_WIDTH)), _const_spec((8, A_WIDTH)),
                  cache_spec, cache_spec,
                  _const_spec((2, 4 * 32, SAMPLE_GROUP * WINDOW)), _const_spec((2, 4 * 32, LANES)),
                  _const_spec((A_WIDTH + Q_WIDTH, D_MODEL)), _const_spec((1, D_MODEL)),
                  _const_spec((D_MODEL, 2 * LANES)), _const_spec((1, LANES)), _const_spec((1, LANES)),
                  anyspec, anyspec, anyspec, anyspec, anyspec],
        out_specs=[sample_rows, sample_rows3, pl.BlockSpec((8, TM), lambda g: (0, N_PROMPT_BLOCKS)), sample_lanes,
                   pl.BlockSpec((None, 1, LANES), lambda g: (N_PROMPT_BLOCKS, 0, 0)),
                   cache_spec, cache_spec,
                   _const_spec((T_SAMPLE, A_WIDTH)), _const_spec((1, LANES)),
                   _const_spec((TM * TOP_K, ROW_TILE, LANES))],
        out_shape=[jax.ShapeDtypeStruct((T_ALL, D_MODEL), F32), jax.ShapeDtypeStruct((T_ALL, ROW_TILE, LANES), BF16),
                   jax.ShapeDtypeStruct((8, T_ALL), jnp.int32), jax.ShapeDtypeStruct((T_ALL, LANES), F32),
                   jax.ShapeDtypeStruct((N_ROW_BLOCKS, 1, LANES), F32),
                   jax.ShapeDtypeStruct((DEC_BATCH, B_KV_HEADS, B_HEAD_DIM, WINDOW), F32),
                   jax.ShapeDtypeStruct((DEC_BATCH, B_KV_HEADS, B_HEAD_DIM, WINDOW), F32),
                   jax.ShapeDtypeStruct((T_SAMPLE, A_WIDTH), F32), jax.ShapeDtypeStruct((1, LANES), F32),
                   jax.ShapeDtypeStruct((TM * TOP_K, ROW_TILE, LANES), BF16)],
        scratch_shapes=[pltpu.VMEM((T_SAMPLE, Q_WIDTH), F32), pltpu.VMEM((T_SAMPLE, KV_WIDTH), F32),
                        pltpu.VMEM((T_SAMPLE, KV_WIDTH), F32), pltpu.VMEM((T_SAMPLE, D_MODEL), F32),
                        pltpu.VMEM((8, TM), jnp.int32), pltpu.SMEM((8, TM), jnp.int32), pltpu.SemaphoreType.DMA(())],
        input_output_aliases={n_in: 0, n_in + 1: 1, n_in + 2: 2, n_in + 3: 3, n_in + 4: 4},
        compiler_params=_cparams(("arbitrary",)),
        name="mix0_sample",
    )(x_all, nm, win, lng, lnb, wcoef, bcoef, ck, cv, bias_sc, bias_sn, wout, nf, wr, br, cnt,
      x1_all, h_all, ri_all, rg_all, tc_all)


def _moe_metadata(rt_all, cnt, tcnt):
    counts = cnt[0, :N_EXPERTS].astype(jnp.int32)
    padded = (counts + MOE_BLK - 1) // MOE_BLK * MOE_BLK
    pad_end = jnp.cumsum(padded)
    pad_start = pad_end - padded
    experts = jnp.arange(N_EXPERTS, dtype=jnp.int32)
    n_valid = (pad_end[-1] // MOE_BLK).astype(jnp.int32).reshape(1)
    blk_start = jnp.arange(N_MOE_BLOCKS, dtype=jnp.int32) * MOE_BLK
    block_e = jnp.minimum(jnp.sum((blk_start[:, None] >= pad_end[None, :]).astype(jnp.int32), axis=1),
                          N_EXPERTS - 1).astype(jnp.int32)
    zero_start = (pad_start + counts).astype(jnp.int32)
    zero_len = (padded - counts).astype(jnp.int32)
    first = (blk_start == pad_start[block_e]).astype(jnp.int32)
    used = counts > 0
    parity = ((jnp.cumsum(used.astype(jnp.int32)) - 1) % 2)[block_e].astype(jnp.int32)
    nearest = lax.cummin(jnp.where(used, experts, N_EXPERTS)[::-1])[::-1]
    next_used = jnp.concatenate([nearest[1:], jnp.full((1,), N_EXPERTS, jnp.int32)])
    nxt = jnp.where(next_used < N_EXPERTS, next_used, -1)[block_e].astype(jnp.int32)
    plan = (block_e, first, parity, nxt, n_valid)
    runs = tcnt[:, 0, :N_EXPERTS].astype(jnp.int32)
    run_dst = pad_start[None, :] + jnp.cumsum(runs, axis=0) - runs
    lpos = rt_all[2 * TOP_K:3 * TOP_K].reshape(N_SLOTS).astype(jnp.int32)
    cplan = (lpos, runs.reshape(-1), run_dst.reshape(-1).astype(jnp.int32))
    dplan = cplan[1:] + (jnp.concatenate([zero_start, zero_len, n_valid]),)
    return plan, dplan, cplan


RUN_PIECE = 32


def _for_run_pieces(n, start_piece):
    whole = n // RUN_PIECE

    def body(j, carry):
        start_piece(j * RUN_PIECE, RUN_PIECE)
        return carry

    lax.fori_loop(0, whole, body, 0)
    o = whole * RUN_PIECE
    bit = RUN_PIECE // 2
    while bit >= 1:
        take = (n & bit) != 0

        @pl.when(take)
        def _(o=o, bit=bit):
            start_piece(o, bit)

        o = o + jnp.where(take, bit, 0)
        bit //= 2


def _dispatch_kernel(run_ref, rdst_ref, zs_ref, stp_ref, sts_ref, xs_ref, zero_scr, sem, zsem):
    i = pl.program_id(0)

    @pl.when(i == 0)
    def _():
        zero_scr[...] = jnp.zeros_like(zero_scr)

        def pieces(e, do):
            off = zs_ref[e]
            rem = zs_ref[N_EXPERTS + e]
            bit = MOE_BLK // 2
            while bit >= 1:
                take = (rem & bit) != 0

                @pl.when(take)
                def _(off=off, bit=bit):
                    do(pltpu.make_async_copy(zero_scr.at[pl.ds(0, bit)], xs_ref.at[pl.ds(off, bit)], zsem))

                off = off + jnp.where(take, bit, 0)
                bit //= 2

        def start_e(e, c):
            pieces(e, lambda cp: cp.start())
            return c

        def wait_e(e, c):
            pieces(e, lambda cp: cp.wait())
            return c

        def tail(do):
            def step(b, c):
                do(pltpu.make_async_copy(zero_scr, xs_ref.at[pl.ds(b * MOE_BLK, MOE_BLK)], zsem))
                return c
            return step

        n_valid = zs_ref[2 * N_EXPERTS]
        lax.fori_loop(0, N_EXPERTS, start_e, 0)
        lax.fori_loop(n_valid, N_MOE_BLOCKS, tail(lambda cp: cp.start()), 0)
        lax.fori_loop(0, N_EXPERTS, wait_e, 0)
        lax.fori_loop(n_valid, N_MOE_BLOCKS, tail(lambda cp: cp.wait()), 0)

    def send(src_ref, row0):
        def send_run(e, off):
            n = run_ref[i * N_EXPERTS + e]
            dst = rdst_ref[i * N_EXPERTS + e]
            _for_run_pieces(n, lambda o, size: pltpu.make_async_copy(
                src_ref.at[pl.ds(row0 + off + o, size)], xs_ref.at[pl.ds(dst + o, size)], sem).start(
                    priority=size.bit_length() % 2))
            return off + n

        lax.fori_loop(0, N_EXPERTS, send_run, 0)

    @pl.when(i < N_PROMPT_BLOCKS)
    def _():
        send(stp_ref, i * (TM * TOP_K))

    @pl.when(i >= N_PROMPT_BLOCKS)
    def _():
        send(sts_ref, 0)

    pltpu.make_async_copy(sts_ref, xs_ref.at[pl.ds(0, TM * TOP_K)], sem).wait()


def _dispatch(dplan, staged_prompt, staged_sample):
    anyspec = pl.BlockSpec(memory_space=pl.ANY)
    return pl.pallas_call(
        _dispatch_kernel,
        grid_spec=pltpu.PrefetchScalarGridSpec(
            num_scalar_prefetch=3,
            grid=(N_ROW_BLOCKS,),
            in_specs=[anyspec, anyspec],
            out_specs=anyspec,
            scratch_shapes=[pltpu.VMEM((MOE_BLK, ROW_TILE, LANES), BF16),
                            pltpu.SemaphoreType.DMA(()), pltpu.SemaphoreType.DMA(())],
        ),
        out_shape=jax.ShapeDtypeStruct((N_SORT_ROWS, ROW_TILE, LANES), BF16),
        compiler_params=_cparams(("arbitrary",)),
        name="moe_dispatch",
    )(*dplan, staged_prompt, staged_sample)


def _experts_kernel(layer, be_ref, first_ref, par_ref, nxt_ref, nv_ref,
                    x_ref, wg_hbm, wu_hbm, wd_hbm, y_ref,
                    wg_s, wu_s, wd_s, wg_f, wu_f, wd_f, wsem):
    i = pl.program_id(0)

    def fetch(e, slot):
        return (pltpu.make_async_copy(wg_hbm.at[layer, e], wg_f.at[slot], wsem.at[slot]),
                pltpu.make_async_copy(wu_hbm.at[layer, e], wu_f.at[slot], wsem.at[slot]),
                pltpu.make_async_copy(wd_hbm.at[layer, e], wd_f.at[slot], wsem.at[slot]))

    @pl.when(i < nv_ref[0])
    def _():
        e = be_ref[i]
        slot = par_ref[i]

        @pl.when(i == 0)
        def _():
            for cp in fetch(e, slot):
                cp.start()

        @pl.when(first_ref[i] == 1)
        def _():
            for cp in fetch(e, slot):
                cp.wait()
            wg_s[...] = wg_f[slot].astype(BF16)
            wu_s[...] = wu_f[slot].astype(BF16)
            wd_s[...] = wd_f[slot].astype(BF16)
            nxt = nxt_ref[i]

            @pl.when(nxt >= 0)
            def _():
                for cp in fetch(nxt, 1 - slot):
                    cp.start()

        xb = x_ref[...].reshape(MOE_BLK, D_MODEL)
        a = jax.nn.silu(_dot(xb, wg_s[...])) * _dot(xb, wu_s[...])
        y_ref[...] = _dot(a.astype(BF16), wd_s[...]).reshape(y_ref.shape)

    @pl.when(i >= nv_ref[0])
    def _():
        y_ref[...] = jnp.zeros(y_ref.shape, y_ref.dtype)


def _experts(block_e, first, parity, nxt, n_valid, xs, w_gate, w_up, w_down, layer):
    def blk(i, be, fi, pa, nx, nv):
        return (jnp.maximum(jnp.minimum(i, nv[0] - 1), 0), 0, 0)

    anyspec = pl.BlockSpec(memory_space=pl.ANY)
    return pl.pallas_call(
        functools.partial(_experts_kernel, layer),
        grid_spec=pltpu.PrefetchScalarGridSpec(
            num_scalar_prefetch=5,
            grid=(N_MOE_BLOCKS,),
            in_specs=[pl.BlockSpec((MOE_BLK, ROW_TILE, LANES), blk), anyspec, anyspec, anyspec],
            out_specs=pl.BlockSpec((MOE_BLK, ROW_TILE, LANES), lambda i, be, fi, pa, nx, nv: (i, 0, 0)),
            scratch_shapes=[pltpu.VMEM((D_MODEL, D_EXPERT), BF16), pltpu.VMEM((D_MODEL, D_EXPERT), BF16),
                            pltpu.VMEM((D_EXPERT, D_MODEL), BF16),
                            pltpu.VMEM((2, D_MODEL, D_EXPERT), F32), pltpu.VMEM((2, D_MODEL, D_EXPERT), F32),
                            pltpu.VMEM((2, D_EXPERT, D_MODEL), F32), pltpu.SemaphoreType.DMA((2,))],
        ),
        out_shape=jax.ShapeDtypeStruct((N_SORT_ROWS, ROW_TILE, LANES), F32),
        compiler_params=_cparams(("arbitrary",)),
        name="moe_experts",
    )(block_e, first, parity, nxt, n_valid, xs, w_gate, w_up, w_down)


def _gather_rows(lpos_ref, run_ref, rdst_ref, ys_ref, ystage, ybufs, sem, i):
    def fetch(tile, buf):
        def fetch_run(e, off):
            n = run_ref[tile * N_EXPERTS + e]
            src = rdst_ref[tile * N_EXPERTS + e]
            _for_run_pieces(n, lambda o, size: pltpu.make_async_copy(
                ys_ref.at[pl.ds(src + o, size)], ystage.at[buf, pl.ds(off + o, size)], sem.at[buf]).start(
                    priority=size.bit_length() % 2))
            return off + n

        lax.fori_loop(0, N_EXPERTS, fetch_run, 0)

    def wait(buf):
        pltpu.make_async_copy(ys_ref.at[pl.ds(0, TM * TOP_K)], ystage.at[buf], sem.at[buf]).wait()

    cur = i % 2

    @pl.when(i == 0)
    def _():
        fetch(i, 0)
        wait(0)

        def unplace(r, carry):
            for kk in range(TOP_K):
                ybufs[0][kk, r] = ystage[0, lpos_ref[kk * T_ALL + r]]
            return carry

        lax.fori_loop(0, TM, unplace, 0, unroll=8)
        fetch(i + 1, 1)

    @pl.when(i + 1 < N_ROW_BLOCKS)
    def _():
        wait(1 - cur)

    @pl.when(i + 2 < N_ROW_BLOCKS)
    def _():
        fetch(i + 2, cur)

    def pieces(compute, store):
        nxt = jnp.minimum(i + 1, N_ROW_BLOCKS - 1)

        def variant(par):
            ycur, ynext = ybufs[par], ybufs[1 - par]

            def piece(j, carry):
                rows = pl.ds(pl.multiple_of(j * COMBINE_ROWS, COMBINE_ROWS), COMBINE_ROWS)
                out = compute(rows, ycur[0, rows].reshape(COMBINE_ROWS, D_MODEL),
                              ycur[1, rows].reshape(COMBINE_ROWS, D_MODEL))
                base = nxt * TM + j * COMBINE_ROWS
                for r in range(COMBINE_ROWS):
                    for kk in range(TOP_K):
                        ynext[kk, j * COMBINE_ROWS + r] = ystage[1 - par, lpos_ref[kk * T_ALL + base + r]]
                store(rows, out)
                return carry

            lax.fori_loop(0, TM // COMBINE_ROWS, piece, 0)

        for par in range(2):
            @pl.when(cur == par)
            def _(par=par):
                variant(par)

    return pieces


COMBINE_ROWS = 64


def _combined(x_ref, rg_ref, rows, y0, y1):
    rg = rg_ref[rows, :]
    return x_ref[rows, :] + rg[:, 0:1] * y0 + rg[:, 1:2] * y1


_COMBINE_SCRATCH = [pltpu.VMEM((2, TM * TOP_K, ROW_TILE, LANES), F32),
                    pltpu.VMEM((TOP_K, TM, ROW_TILE, LANES), F32), pltpu.VMEM((TOP_K, TM, ROW_TILE, LANES), F32),
                    pltpu.SemaphoreType.DMA((2,))]


def _combine_kernel(lpos_ref, run_ref, rdst_ref, x_ref, rg_ref, ys_ref, o_ref, ystage, ybuf0, ybuf1, sem):
    pieces = _gather_rows(lpos_ref, run_ref, rdst_ref, ys_ref, ystage, (ybuf0, ybuf1), sem, pl.program_id(0))

    def store(rows, out):
        o_ref[rows, :] = out

    pieces(functools.partial(_combined, x_ref, rg_ref), store)


def _combine(cplan, x_all, rg_all, ys):
    return pl.pallas_call(
        _combine_kernel,
        grid_spec=pltpu.PrefetchScalarGridSpec(
            num_scalar_prefetch=3,
            grid=(N_ROW_BLOCKS,),
            in_specs=[pl.BlockSpec((TM, D_MODEL), lambda i, a, b, c: (i, 0)),
                      pl.BlockSpec((TM, LANES), lambda i, a, b, c: (i, 0)),
                      pl.BlockSpec(memory_space=pl.ANY)],
            out_specs=pl.BlockSpec((TM, D_MODEL), lambda i, a, b, c: (i, 0)),
            scratch_shapes=_COMBINE_SCRATCH,
        ),
        out_shape=jax.ShapeDtypeStruct((T_ALL, D_MODEL), F32),
        compiler_params=_cparams(("arbitrary",)),
        name="moe_combine",
    )(*cplan, x_all, rg_all, ys)


def _final_kernel(lpos_ref, run_ref, rdst_ref, x_ref, rg_ref, ys_ref, nfin_ref, op_ref, os_ref,
                  ystage, ybuf0, ybuf1, sem):
    i = pl.program_id(0)
    pieces = _gather_rows(lpos_ref, run_ref, rdst_ref, ys_ref, ystage, (ybuf0, ybuf1), sem, i)

    def compute(rows, y0, y1):
        return _rms(_combined(x_ref, rg_ref, rows, y0, y1), nfin_ref[...])

    def store(rows, y):
        @pl.when(i < N_PROMPT_BLOCKS)
        def _():
            op_ref[rows, :] = y

        @pl.when(i >= N_PROMPT_BLOCKS)
        def _():
            os_ref[rows, :] = y

    pieces(compute, store)


def _final(cplan, x_all, rg_all, ys, nfin):
    return pl.pallas_call(
        _final_kernel,
        grid_spec=pltpu.PrefetchScalarGridSpec(
            num_scalar_prefetch=3,
            grid=(N_ROW_BLOCKS,),
            in_specs=[pl.BlockSpec((TM, D_MODEL), lambda i, a, b, c: (i, 0)),
                      pl.BlockSpec((TM, LANES), lambda i, a, b, c: (i, 0)),
                      pl.BlockSpec(memory_space=pl.ANY),
                      pl.BlockSpec((1, D_MODEL), lambda i, a, b, c: (0, 0))],
            out_specs=[pl.BlockSpec((TM, D_MODEL), lambda i, a, b, c: (jnp.minimum(i, N_PROMPT_BLOCKS - 1), 0)),
                       pl.BlockSpec((TM, D_MODEL), lambda i, a, b, c: (0, 0))],
            scratch_shapes=_COMBINE_SCRATCH,
        ),
        out_shape=[jax.ShapeDtypeStruct((T_PROMPT, D_MODEL), F32), jax.ShapeDtypeStruct((T_SAMPLE, D_MODEL), F32)],
        compiler_params=_cparams(("arbitrary",)),
        name="moe_combine_final",
    )(*cplan, x_all, rg_all, ys, nfin)


def _moe(staged_prompt, staged_sample, rt_all, cnt, tcnt, w_gate, w_up, w_down, layer):
    plan, dplan, cplan = _moe_metadata(rt_all, cnt, tcnt)
    xs = _dispatch(dplan, staged_prompt, staged_sample)
    ys = _experts(*plan, xs, w_gate, w_up, w_down, layer)
    return cplan, ys


def _pool_project(d_groups, wp_ref, scale):
    outs = [_dot(d_groups[g].astype(BF16), wp_ref[g]) for g in range(len(POOL_SIZES))]
    return jnp.concatenate(outs, axis=1) * scale


def _mix1_prompt_kernel(x_ref, nm_ref, wp_ref, sc_ref, nf_ref, wr_ref, br_ref,
                        x3_ref, h_ref, ri_ref, rg_ref, tc_ref, pl_ref, cnt_ref, ext, idx_v, hprev):
    i = pl.program_id(0)

    @pl.when(i == 0)
    def _():
        cnt_ref[...] = jnp.zeros_like(cnt_ref)

    x = x_ref[...]
    hp = _rms(x, nm_ref[...])

    @pl.when(i % STEPS_PER_BATCH == 0)
    def _():
        ext[0:POOL_MAX, :] = jnp.zeros((POOL_MAX, D_MODEL), F32)

    ext[POOL_MAX:, :] = hp
    pos = (i % STEPS_PER_BATCH) * TM + lax.broadcasted_iota(jnp.int32, (TM, 1), 0)
    d_groups = []
    for g, w in enumerate(POOL_SIZES):
        cols = slice(g * POOL_GROUP_DIM, (g + 1) * POOL_GROUP_DIM)
        acc = ext[:, cols]
        span = 1
        while span < w:
            acc = acc + pltpu.roll(acc, span, 0)
            span *= 2
        cnt = jnp.minimum(pos + 1, w).astype(F32)
        d_groups.append(acc[POOL_MAX:] / cnt - hp[:, cols])
    tail = hp[TM - POOL_MAX:, :]
    ext[0:POOL_MAX, :] = tail
    pl_ref[...] = tail

    x3 = x + _pool_project(d_groups, wp_ref, sc_ref[...])
    x3_ref[...] = x3
    h, ids, gates = _route(x3, nf_ref[...], wr_ref[...], br_ref[...])
    _emit_routing(h, ids, gates, h_ref, ri_ref, rg_ref, tc_ref, cnt_ref, idx_v, hprev)


def _mix1_prompt(x_all, nm, wp, sc, nf, wr, br):
    row_spec = pl.BlockSpec((TM, D_MODEL), lambda i: (i, 0))
    row3_spec = pl.BlockSpec((TM, ROW_TILE, LANES), lambda i: (i, 0, 0))
    lane_spec = pl.BlockSpec((TM, LANES), lambda i: (i, 0))
    return pl.pallas_call(
        _prompt_steps(_mix1_prompt_kernel, 7, 7 + 7),
        grid=(N_ROW_BLOCKS,),
        in_specs=[row_spec, _const_spec((1, D_MODEL)),
                  _const_spec((len(POOL_SIZES), POOL_GROUP_DIM, POOL_GROUP_DIM)), _const_spec((1, D_MODEL)),
                  _const_spec((1, D_MODEL)), _const_spec((D_MODEL, 2 * LANES)), _const_spec((1, LANES))],
        out_specs=[row_spec, row3_spec, pl.BlockSpec((8, TM), lambda i: (0, i)), lane_spec,
                   pl.BlockSpec((None, 1, LANES), lambda i: (i, 0, 0)),
                   pl.BlockSpec((None, POOL_MAX, D_MODEL),
                                lambda i: (jnp.minimum(i // STEPS_PER_BATCH, BATCH - 1), 0, 0)),
                   _const_spec((1, LANES)), _STAGED_SPEC],
        out_shape=[jax.ShapeDtypeStruct((T_ALL, D_MODEL), F32), jax.ShapeDtypeStruct((T_ALL, ROW_TILE, LANES), BF16),
                   jax.ShapeDtypeStruct((8, T_ALL), jnp.int32), jax.ShapeDtypeStruct((T_ALL, LANES), F32),
                   jax.ShapeDtypeStruct((N_ROW_BLOCKS, 1, LANES), F32),
                   jax.ShapeDtypeStruct((BATCH, POOL_MAX, D_MODEL), F32), jax.ShapeDtypeStruct((1, LANES), F32),
                   _STAGED_SHAPE],
        scratch_shapes=_stage_scratch() + [pltpu.VMEM((POOL_MAX + TM, D_MODEL), F32)],
        compiler_params=_cparams(("arbitrary",)),
        name="mix1_prompt",
    )(x_all, nm, wp, sc, nf, wr, br)


def _mix1_sample_kernel(x_ref, st_ref, nm_ref, wp_ref, sc_ref, nf_ref, wr_ref, br_ref, cnt_in,
                        x3_in, h_in, ri_in, rg_in, tc_in,
                        x3_ref, h_ref, ri_ref, rg_ref, tc_ref, hs_ref, cnt_ref, stg_ref, idx_v, idx_s, psem):
    del x3_in, h_in, ri_in, rg_in, tc_in
    x = x_ref[...]
    hs = _rms(x, nm_ref[...])
    hs_ref[...] = hs
    n_ctx = POOL_MAX - 1
    d_groups = []
    for g, w in enumerate(POOL_SIZES):
        cols = slice(g * POOL_GROUP_DIM, (g + 1) * POOL_GROUP_DIM)
        parts = []
        for t in range(DEC_SEQ):
            acc = hs[t * DEC_BATCH:(t + 1) * DEC_BATCH, cols]
            for back in range(1, w):
                src = t - back
                if src >= 0:
                    acc = acc + hs[src * DEC_BATCH:(src + 1) * DEC_BATCH, cols]
                else:
                    acc = acc + st_ref[n_ctx + src, :, cols]
            parts.append(acc / float(w) - hs[t * DEC_BATCH:(t + 1) * DEC_BATCH, cols])
        d_groups.append(jnp.concatenate(parts, axis=0))
    x3 = x + _pool_project(d_groups, wp_ref, sc_ref[...])
    x3_ref[...] = x3
    h, ids, gates = _route(x3, nf_ref[...], wr_ref[...], br_ref[...])
    cnt_ref[...] = cnt_in[...]
    _emit_and_stage(h, ids, gates, h_ref, ri_ref, rg_ref, tc_ref, cnt_ref, stg_ref, idx_v, idx_s, psem)


def _mix1_sample(x_all, state_t, nm, wp, sc, nf, wr, br, cnt, x3_all, h_all, ri_all, rg_all, tc_all):
    sample_rows = pl.BlockSpec((TM, D_MODEL), lambda g: (N_PROMPT_BLOCKS, 0))
    sample_rows3 = pl.BlockSpec((TM, ROW_TILE, LANES), lambda g: (N_PROMPT_BLOCKS, 0, 0))
    sample_lanes = pl.BlockSpec((TM, LANES), lambda g: (N_PROMPT_BLOCKS, 0))
    anyspec = pl.BlockSpec(memory_space=pl.ANY)
    n_in = 9
    return pl.pallas_call(
        _mix1_sample_kernel,
        grid=(1,),
        in_specs=[sample_rows, _const_spec((POOL_MAX - 1, DEC_BATCH, D_MODEL)), _const_spec((1, D_MODEL)),
                  _const_spec((len(POOL_SIZES), POOL_GROUP_DIM, POOL_GROUP_DIM)), _const_spec((1, D_MODEL)),
                  _const_spec((1, D_MODEL)), _const_spec((D_MODEL, 2 * LANES)), _const_spec((1, LANES)),
                  _const_spec((1, LANES)), anyspec, anyspec, anyspec, anyspec, anyspec],
        out_specs=[sample_rows, sample_rows3, pl.BlockSpec((8, TM), lambda g: (0, N_PROMPT_BLOCKS)), sample_lanes,
                   pl.BlockSpec((None, 1, LANES), lambda g: (N_PROMPT_BLOCKS, 0, 0)),
                   _const_spec((T_SAMPLE, D_MODEL)), _const_spec((1, LANES)),
                   _const_spec((TM * TOP_K, ROW_TILE, LANES))],
        out_shape=[jax.ShapeDtypeStruct((T_ALL, D_MODEL), F32), jax.ShapeDtypeStruct((T_ALL, ROW_TILE, LANES), BF16),
                   jax.ShapeDtypeStruct((8, T_ALL), jnp.int32), jax.ShapeDtypeStruct((T_ALL, LANES), F32),
                   jax.ShapeDtypeStruct((N_ROW_BLOCKS, 1, LANES), F32),
                   jax.ShapeDtypeStruct((T_SAMPLE, D_MODEL), F32), jax.ShapeDtypeStruct((1, LANES), F32),
                   jax.ShapeDtypeStruct((TM * TOP_K, ROW_TILE, LANES), BF16)],
        scratch_shapes=[pltpu.VMEM((8, TM), jnp.int32), pltpu.SMEM((8, TM), jnp.int32), pltpu.SemaphoreType.DMA(())],
        input_output_aliases={n_in: 0, n_in + 1: 1, n_in + 2: 2, n_in + 3: 3, n_in + 4: 4},
        compiler_params=_cparams(("arbitrary",)),
        name="mix1_sample",
    )(x_all, state_t, nm, wp, sc, nf, wr, br, cnt, x3_all, h_all, ri_all, rg_all, tc_all)


def _router_weights(wg, bg, we, be):
    w = jnp.concatenate([wg, jnp.transpose(we, (1, 0, 2)).reshape(D_MODEL, N_EXPERTS)], axis=1)
    b = jnp.concatenate([bg, be.reshape(N_EXPERTS)])
    pad = LANES - N_GROUPS - N_EXPERTS
    w = jnp.pad(w, ((0, 0), (0, pad)))
    w_hi = w.astype(BF16)
    w_lo = (w - w_hi.astype(F32)).astype(BF16)
    return jnp.concatenate([w_hi, w_lo], axis=1), jnp.pad(b, (0, pad)).reshape(1, LANES)


def kernel(x_prompt, x_sample, cache_k_win, cache_v_win, state_pool, norm_mix, norm_ffn, norm_final, w_in,
           a_ln_g, a_ln_b, a_w_s, a_b_s, b_sinks, rel_bias_table, w_out, c_w_pool, c_scale,
           router_group_w, router_group_b, router_expert_w, router_expert_b, w_gate, w_up, w_down):
    xs_t = jnp.transpose(x_sample, (1, 0, 2)).reshape(T_SAMPLE, D_MODEL)
    xp2 = x_prompt.reshape(T_PROMPT, D_MODEL)
    win =w_in[0].astype(BF16)
    wout = w_out[0].astype(BF16)
    lng = a_ln_g[0].reshape(1, A_WIDTH)
    lnb = a_ln_b[0].reshape(1, A_WIDTH)
    bias_p, bias_sc, bias_sn, wsp = _prep(rel_bias_table, b_sinks[0], a_w_s[0])
    bs_full = jnp.repeat(a_b_s[0].T, A_HEAD_DIM, axis=1)
    w4 = jnp.transpose(a_w_s[0][:, :DEC_SEQ, :DEC_SEQ], (1, 2, 0)).reshape(DEC_SEQ * DEC_SEQ, A_HEADS)
    wcoef = jnp.repeat(w4, A_HEAD_DIM, axis=1)
    bcoef = jnp.pad(jnp.repeat(a_b_s[0][:, :DEC_SEQ].T, A_HEAD_DIM, axis=1), ((0, 8 - DEC_SEQ), (0, 0)))
    ck = jnp.transpose(cache_k_win[0], (0, 2, 3, 1))
    cv = jnp.transpose(cache_v_win[0], (0, 2, 3, 1))
    routers = [_router_weights(router_group_w[l], router_group_b[l], router_expert_w[l], router_expert_b[l])
               for l in range(2)]
    nm = [norm_mix[l].reshape(1, D_MODEL) for l in range(2)]
    nf = [norm_ffn[l].reshape(1, D_MODEL) for l in range(2)]

    x1_all, h_all, ri_all, rg_all, tc_all, k_last, v_last, va_last, cnt0, stg_p = _mix0_prompt(
        xp2, nm[0], win, lng, lnb, wsp, bs_full, bias_p, wout, nf[0], *routers[0])
    x1_all, h_all, ri_all, rg_all, tc_all, k_new, v_new, va_s, cnt0, stg_s = _mix0_sample(
        xs_t, nm[0], win, lng, lnb, wcoef, bcoef, ck, cv, bias_sc, bias_sn, wout, nf[0], *routers[0], cnt0,
        x1_all, h_all, ri_all, rg_all, tc_all)
    cplan0, ys0 = _moe(stg_p, stg_s, ri_all, cnt0, tc_all, w_gate, w_up, w_down, 0)
    x2_all = _combine(cplan0, x1_all, rg_all, ys0)

    wp = c_w_pool[0].astype(BF16)
    sc = c_scale[0].reshape(1, D_MODEL)
    x3_all, h2_all, ri2_all, rg2_all, tc2_all, pool_tail, cnt1, stg2_p = _mix1_prompt(
        x2_all, nm[1], wp, sc, nf[1], *routers[1])
    state_t = jnp.transpose(state_pool[0], (1, 0, 2))
    x3_all, h2_all, ri2_all, rg2_all, tc2_all, hs1, cnt1, stg2_s = _mix1_sample(
        x2_all, state_t, nm[1], wp, sc, nf[1], *routers[1], cnt1, x3_all, h2_all, ri2_all, rg2_all, tc2_all)
    cplan1, ys1 = _moe(stg2_p, stg2_s, ri2_all, cnt1, tc2_all, w_gate, w_up, w_down, 1)
    y_p, y_s = _final(cplan1, x3_all, rg2_all, ys1, norm_final.reshape(1, D_MODEL))

    def from_tmajor(a, width):
        return jnp.transpose(a.reshape(DEC_SEQ, DEC_BATCH, width), (1, 0, 2))

    y_prompt = y_p.reshape(BATCH, SEQ, D_MODEL)
    y_sample = from_tmajor(y_s, D_MODEL)
    win_k_p = k_last.reshape(1, BATCH, WINDOW, B_KV_HEADS, B_HEAD_DIM)
    win_v_p = v_last.reshape(1, BATCH, WINDOW, B_KV_HEADS, B_HEAD_DIM)
    win_k_s = jnp.transpose(k_new, (0, 3, 1, 2))[None]
    win_v_s = jnp.transpose(v_new, (0, 3, 1, 2))[None]
    chunk_v_p = va_last.reshape(1, BATCH, CHUNK, A_HEADS, A_HEAD_DIM)
    chunk_v_s = from_tmajor(va_s, A_WIDTH).reshape(1, DEC_BATCH, DEC_SEQ, A_HEADS, A_HEAD_DIM)
    pool_p = pool_tail[:, 1:][None]
    pool_s = jnp.concatenate([state_pool[0][:, DEC_SEQ:], from_tmajor(hs1, D_MODEL)], axis=1)[None]
    return (y_prompt, y_sample, win_k_p, win_v_p, win_k_s, win_v_s, chunk_v_p, chunk_v_s, pool_p, pool_s)
```

```python
import functools
import math

import numpy as np
import jax
import jax.numpy as jnp
from jax import lax
from jax.experimental import pallas as pl
from jax.experimental.pallas import tpu as pltpu

F32 = jnp.float32
BF16 = jnp.bfloat16

D_MODEL = 1024
BATCH = 2
SEQ = 8192
DEC_BATCH = 128
DEC_SEQ = 4
A_WIDTH = 512
A_HEADS = 8
A_HEAD_DIM = 64
CHUNK = 128
B_HEADS = 8
B_KV_HEADS = 2
B_HEAD_DIM = 64
B_GROUP = 4
WINDOW = 128
N_BUCKETS = 32
MAX_DISTANCE = WINDOW
Q_WIDTH = 512
KV_WIDTH = 128
IN_WIDTH = 2 * A_WIDTH + Q_WIDTH + 2 * KV_WIDTH
ATTN_SCALE = B_HEAD_DIM ** -0.5
NEG_INF = -1e30
POOL_SIZES = (2, 4, 8, 16)
POOL_GROUP_DIM = 256
POOL_MAX = 16
N_GROUPS = 4
EXPERTS_PER_GROUP = 8
N_EXPERTS = 32
TOP_K = 2
D_EXPERT = 512
EPS = 1e-6

LANES = 128
ROW_TILE = D_MODEL // LANES
T_PROMPT = BATCH * SEQ
T_SAMPLE = DEC_BATCH * DEC_SEQ
T_ALL = T_PROMPT + T_SAMPLE
TM = 512
N_PROMPT_BLOCKS = T_PROMPT // TM
N_ROW_BLOCKS = T_ALL // TM
STEPS_PER_BATCH = SEQ // TM
SUB = TM // WINDOW
N_SLOTS = T_ALL * TOP_K
MOE_BLK = 512
N_MOE_BLOCKS = N_SLOTS // MOE_BLK + N_EXPERTS
N_SORT_ROWS = N_MOE_BLOCKS * MOE_BLK
SAMPLE_GROUP = 8
N_SAMPLE_GROUPS = DEC_BATCH // SAMPLE_GROUP
VMEM_LIMIT = 56 * 1024 * 1024

STACK_HEADS = ((0, 2, 5, 7), (1, 3, 4, 6))


def _t5_bucket_np(dist):
    n = np.maximum(dist, 0)
    max_exact = N_BUCKETS // 2
    nf = np.maximum(n, 1).astype(np.float32)
    large = max_exact + (np.log(nf / np.float32(max_exact)) / np.float32(math.log(MAX_DISTANCE / max_exact))
                         * np.float32(N_BUCKETS - max_exact)).astype(np.int32)
    large = np.minimum(large, N_BUCKETS - 1)
    return np.where(n < max_exact, n, large).astype(np.int32)


def _bucket_tables():
    qi = np.arange(WINDOW)[:, None]
    ki = np.arange(2 * WINDOW)[None, :]
    dist = qi + WINDOW - ki
    valid = (dist >= 0) & (dist < WINDOW)
    bp = np.where(valid, _t5_bucket_np(dist), -1)
    bp_first = np.where(ki >= WINDOW, bp, -1)
    bkt_p = np.stack([bp_first, bp]).astype(np.int32)

    t = np.repeat(np.arange(DEC_SEQ), SAMPLE_GROUP)[:, None]
    b = np.tile(np.arange(SAMPLE_GROUP), DEC_SEQ)[:, None]
    cb = np.repeat(np.arange(SAMPLE_GROUP), WINDOW)[None, :]
    cj = np.tile(np.arange(WINDOW), SAMPLE_GROUP)[None, :]
    dist_c = t + WINDOW - cj
    valid_c = (cb == b) & (dist_c >= 0) & (dist_c < WINDOW)
    bkt_sc = np.where(valid_c, _t5_bucket_np(dist_c), -1).astype(np.int32)
    nt = np.repeat(np.arange(DEC_SEQ), SAMPLE_GROUP)[None, :]
    nb = np.tile(np.arange(SAMPLE_GROUP), DEC_SEQ)[None, :]
    dist_n = t - nt
    valid_n = (nb == b) & (dist_n >= 0)
    bkt_sn = np.where(valid_n, _t5_bucket_np(dist_n), -1).astype(np.int32)
    bkt_sn = np.concatenate([bkt_sn, np.full((32, LANES - 32), -1, np.int32)], axis=1)
    return bkt_p, bkt_sc, bkt_sn


_BKT_P, _BKT_SC, _BKT_SN = _bucket_tables()


def _cparams(semantics):
    return pltpu.CompilerParams(dimension_semantics=semantics, vmem_limit_bytes=VMEM_LIMIT)


def _rms(x, g):
    return x * lax.rsqrt(jnp.mean(x * x, axis=-1, keepdims=True) + EPS) * g


def _layernorm(x, g, b):
    xc = x - jnp.mean(x, axis=-1, keepdims=True)
    return xc * lax.rsqrt(jnp.mean(xc * xc, axis=-1, keepdims=True) + EPS) * g + b


def _dot(a, b):
    return jnp.dot(a, b, preferred_element_type=F32)


def _dot_nt(a, b):
    return lax.dot_general(a, b, (((1,), (1,)), ((), ())), preferred_element_type=F32)


def _project(x, nm, win, lng, lnb):
    h = _rms(x, nm)
    z = _dot(h.astype(BF16), win)
    u = jax.nn.gelu(z[:, :A_WIDTH])
    va = _layernorm(jax.nn.gelu(z[:, A_WIDTH:2 * A_WIDTH]), lng, lnb)
    q = z[:, 2 * A_WIDTH:2 * A_WIDTH + Q_WIDTH] * ATTN_SCALE
    k = z[:, 2 * A_WIDTH + Q_WIDTH:2 * A_WIDTH + Q_WIDTH + KV_WIDTH]
    v = z[:, 2 * A_WIDTH + Q_WIDTH + KV_WIDTH:]
    return u, va, q, k, v


def _route(x1, nf, wr, br):
    hf = _rms(x1, nf)
    h = hf.astype(BF16)
    h_lo = (hf - h.astype(F32)).astype(BF16)
    part = _dot(h, wr)
    logits = part[:, :LANES] + part[:, LANES:] + _dot(h_lo, wr[:, :LANES]) + br
    rows = logits.shape[0]
    lane = lax.broadcasted_iota(jnp.int32, (rows, LANES), 1)
    lanef = lane.astype(F32)
    big = jnp.float32(1e9)
    is_g = lane < N_GROUPS
    gl = jnp.where(is_g, logits, -jnp.inf)
    gmax = jnp.max(gl, axis=1, keepdims=True)
    gsel = jnp.min(jnp.where(gl == gmax, lanef, big), axis=1, keepdims=True)
    gsum = jnp.sum(jnp.where(is_g, jnp.exp(logits - gmax), 0.0), axis=1, keepdims=True)
    g1 = 1.0 / gsum
    lo = N_GROUPS + EXPERTS_PER_GROUP * gsel
    emask = (lanef >= lo) & (lanef < lo + EXPERTS_PER_GROUP)
    el = jnp.where(emask, logits, -jnp.inf)
    v1 = jnp.max(el, axis=1, keepdims=True)
    i1 = jnp.min(jnp.where(el == v1, lanef, big), axis=1, keepdims=True)
    el2 = jnp.where(lanef == i1, -jnp.inf, el)
    v2 = jnp.max(el2, axis=1, keepdims=True)
    i2 = jnp.min(jnp.where(el2 == v2, lanef, big), axis=1, keepdims=True)
    e2 = jnp.exp(v2 - v1)
    den = 1.0 + e2
    w1 = g1 / den
    w2 = g1 * e2 / den
    ids = jnp.where(lane == 0, i1 - N_GROUPS, jnp.where(lane == 1, i2 - N_GROUPS, 0.0)).astype(jnp.int32)
    gates = jnp.where(lane == 0, w1, jnp.where(lane == 1, w2, 0.0))
    return h, ids, gates


def _rank_pack(ids, cnt_ref, tcnt_ref):
    rows = ids.shape[0]
    lane = lax.broadcasted_iota(jnp.int32, (rows, LANES), 1)
    o0 = (lane == ids[:, 0:1]).astype(F32)
    o1 = (lane == ids[:, 1:2]).astype(F32)
    r = lax.broadcasted_iota(jnp.int32, (rows, rows), 0)
    c = lax.broadcasted_iota(jnp.int32, (rows, rows), 1)
    before = (c < r).astype(BF16)
    p01 = _dot(before, jnp.concatenate([o0, o1], axis=1).astype(BF16))
    p0 = p01[:, :LANES]
    p1 = p01[:, LANES:]
    c0 = jnp.sum(o0, axis=0, keepdims=True)
    c1 = jnp.sum(o1, axis=0, keepdims=True)
    ctile = c0 + c1
    cnt_ref[...] = cnt_ref[...] + ctile
    tcnt_ref[...] = ctile
    inc = jnp.broadcast_to(ctile, (8, LANES))
    lane8 = lax.broadcasted_iota(jnp.int32, (8, LANES), 1)
    for sh in (1, 2, 4, 8, 16, 32, 64):
        inc = inc + jnp.where(lane8 >= sh, pltpu.roll(inc, sh, 1), 0.0)
    start = inc[0:1] - ctile
    lpos0 = jnp.sum(o0 * (start + p0), axis=1, keepdims=True)
    lpos1 = jnp.sum(o1 * (start + c0 + p1), axis=1, keepdims=True)
    idf = ids.astype(F32)
    packed = jnp.where(lane < TOP_K, idf, 0.0)
    for ln, col in ((4, lpos0), (5, lpos1)):
        packed = jnp.where(lane == ln, col, packed)
    return jnp.transpose(packed)[:8].astype(jnp.int32)


def _prep_kernel(tab_ref, sink_ref, bp_ref, bsc_ref, bsn_ref, ws_ref, op_ref, osc_ref, osn_ref, ows_ref):
    def fill(bkt, write, sink_col0):
        col0 = lax.broadcasted_iota(jnp.int32, bkt.shape, 1) == 0
        for st, heads in enumerate(STACK_HEADS):
            for slot, h in enumerate(heads):
                acc = jnp.full(bkt.shape, NEG_INF, F32)
                for b in range(N_BUCKETS):
                    acc = jnp.where(bkt == b, tab_ref[b, h], acc)
                if sink_col0:
                    acc = jnp.where(col0, sink_ref[0, h], acc)
                write(st, slot, acc)

    for var in range(2):
        def wr_p(st, slot, acc, var=var):
            op_ref[var, st, slot * WINDOW:(slot + 1) * WINDOW, :] = acc
        fill(bp_ref[var], wr_p, True)

    rows_s = DEC_SEQ * SAMPLE_GROUP

    def wr_sc(st, slot, acc):
        osc_ref[st, slot * rows_s:(slot + 1) * rows_s, :] = acc
    fill(bsc_ref[...], wr_sc, True)

    def wr_sn(st, slot, acc):
        osn_ref[st, slot * rows_s:(slot + 1) * rows_s, :] = acc
    fill(bsn_ref[...], wr_sn, False)

    r = lax.broadcasted_iota(jnp.int32, (CHUNK, CHUNK), 0)
    c = lax.broadcasted_iota(jnp.int32, (CHUNK, CHUNK), 1)
    for h in range(A_HEADS):
        ows_ref[h // 2, :, (h % 2) * CHUNK:(h % 2 + 1) * CHUNK] = jnp.where(r >= c, ws_ref[h], 0.0).astype(BF16)


def _prep(rel_bias_table, sinks, w_s):
    vm = pl.BlockSpec(memory_space=pltpu.VMEM)
    sm = pl.BlockSpec(memory_space=pltpu.SMEM)
    rows_s = DEC_SEQ * SAMPLE_GROUP
    return pl.pallas_call(
        _prep_kernel,
        in_specs=[sm, sm, vm, vm, vm, vm],
        out_specs=[vm, vm, vm, vm],
        out_shape=[
            jax.ShapeDtypeStruct((2, 2, 4 * WINDOW, 2 * WINDOW), F32),
            jax.ShapeDtypeStruct((2, 4 * rows_s, SAMPLE_GROUP * WINDOW), F32),
            jax.ShapeDtypeStruct((2, 4 * rows_s, LANES), F32),
            jax.ShapeDtypeStruct((A_HEADS // 2, CHUNK, 2 * CHUNK), BF16),
        ],
        name="prep_tables",
    )(rel_bias_table, sinks.reshape(1, B_HEADS), jnp.asarray(_BKT_P), jnp.asarray(_BKT_SC), jnp.asarray(_BKT_SN), w_s)


def _gate_pairs(va_rows, wsp_ref, lane_lo):
    outs = []
    for p in range(A_HEADS // 2):
        vp = va_rows[:, p * LANES:(p + 1) * LANES]
        rhs = jnp.concatenate([jnp.where(lane_lo, vp, 0.0), jnp.where(lane_lo, 0.0, vp)], axis=0).astype(BF16)
        outs.append(_dot(wsp_ref[p], rhs))
    return jnp.concatenate(outs, axis=1)


def _stage_rows(idx_s, rows_ref, stg_ref, unrolled):
    def copy(r):
        row = rows_ref[r]
        for kk in range(TOP_K):
            stg_ref[idx_s[2 * TOP_K + kk, r]] = row

    if unrolled:
        for r in range(TM):
            copy(r)
    else:
        def body(r, carry):
            copy(r)
            return carry
        lax.fori_loop(0, TM, body, 0, unroll=8)


def _prompt_steps(body, first_row_out, stg_index):
    def kern(*refs):
        i = pl.program_id(0)
        stg_ref, idx_v, idx_s, hprev, psem = refs[stg_index:stg_index + 5]
        rest = refs[:stg_index] + refs[stg_index + 5:]
        to_smem = pltpu.make_async_copy(idx_v, idx_s, psem)

        @pl.when(i == 0)
        def _():
            hprev[...] = jnp.zeros_like(hprev)

            def init(r, carry):
                for kk in range(TOP_K):
                    idx_s[2 * TOP_K + kk, r] = kk * TM + r
                return carry

            lax.fori_loop(0, TM, init, 0)

        @pl.when(i >= 1)
        def _():
            to_smem.wait()

        @pl.when(i < N_PROMPT_BLOCKS)
        def _():
            _stage_rows(idx_s, hprev, stg_ref, unrolled=True)
            body(*rest, idx_v, hprev)
            to_smem.start()

        @pl.when(i >= N_PROMPT_BLOCKS)
        def _():
            _stage_rows(idx_s, hprev, stg_ref, unrolled=False)
            for r in refs[first_row_out:first_row_out + 5]:
                r[...] = jnp.zeros(r.shape, r.dtype)

    return kern


def _stage_scratch():
    return [pltpu.VMEM((8, TM), jnp.int32), pltpu.SMEM((8, TM), jnp.int32),
            pltpu.VMEM((TM, ROW_TILE, LANES), BF16), pltpu.SemaphoreType.DMA(())]


_STAGED_SPEC = pl.BlockSpec((TM * TOP_K, ROW_TILE, LANES),
                            lambda i: (jnp.clip(i - 1, 0, N_PROMPT_BLOCKS - 1), 0, 0))
_STAGED_SHAPE = jax.ShapeDtypeStruct((T_PROMPT * TOP_K, ROW_TILE, LANES), BF16)


def _mix0_prompt_kernel(x_ref, nm_ref, win_ref, lng_ref, lnb_ref, wsp_ref, bs_ref, bias_ref,
                        wout_ref, nf_ref, wr_ref, br_ref,
                        x1_ref, h_ref, ri_ref, rg_ref, tc_ref, kl_ref, vl_ref, val_ref, cnt_ref,
                        kprev, vprev, mix_scr, idx_v, hprev):
    @pl.when(pl.program_id(0) == 0)
    def _():
        cnt_ref[...] = jnp.zeros_like(cnt_ref)

    x = x_ref[...]
    u, va, q, k, v = _project(x, nm_ref[...], win_ref[...], lng_ref[...], lnb_ref[...])
    lane_lo = lax.broadcasted_iota(jnp.int32, (WINDOW, LANES), 1) < B_HEAD_DIM
    row0 = lax.broadcasted_iota(jnp.int32, (WINDOW, KV_WIDTH), 0) == 0
    first = pl.program_id(0) % STEPS_PER_BATCH == 0

    @pl.when(first)
    def _():
        kprev[...] = jnp.zeros_like(kprev)
        vprev[...] = jnp.zeros_like(vprev)

    for j in range(SUB):
        rows = slice(j * WINDOW, (j + 1) * WINDOW)
        s_gate = _gate_pairs(va[rows], wsp_ref, lane_lo)
        mix_scr[rows, :A_WIDTH] = u[rows] * (s_gate + bs_ref[...])

        if j == 0:
            kp, vp = kprev[...], vprev[...]
        else:
            prows = slice((j - 1) * WINDOW, j * WINDOW)
            kp, vp = k[prows], v[prows]
        kk = jnp.concatenate([jnp.where(row0, 0.0, kp), k[rows]], axis=0)
        vv = jnp.concatenate([jnp.where(row0, 0.0, vp), v[rows]], axis=0)
        kops = (kk.astype(BF16), pltpu.roll(kk, B_HEAD_DIM, 1).astype(BF16))
        vops = (vv.astype(BF16), pltpu.roll(vv, B_HEAD_DIM, 1).astype(BF16))
        qt = [q[rows, p * LANES:(p + 1) * LANES] for p in range(4)]
        q_even = [jnp.where(lane_lo, t, 0.0) for t in qt]
        q_odd = [jnp.where(lane_lo, 0.0, t) for t in qt]
        stacks = (jnp.concatenate([q_even[0], q_even[1], q_odd[2], q_odd[3]], axis=0),
                  jnp.concatenate([q_odd[0], q_odd[1], q_even[2], q_even[3]], axis=0))
        o = []
        for st in range(2):
            s = _dot_nt(stacks[st].astype(BF16), kops[st])
            if j == 0:
                bias = bias_ref[jnp.where(first, 0, 1), st]
            else:
                bias = bias_ref[1, st]
            s = s + bias
            m = jnp.max(s, axis=-1, keepdims=True)
            p = jnp.exp(s - m)
            den = jnp.sum(p, axis=-1, keepdims=True)
            o.append(_dot(p.astype(BF16), vops[st]) / den)
        oa, ob = o
        sl = [slice(i * WINDOW, (i + 1) * WINDOW) for i in range(4)]
        tiles = (jnp.where(lane_lo, oa[sl[0]], ob[sl[0]]), jnp.where(lane_lo, oa[sl[1]], ob[sl[1]]),
                 jnp.where(lane_lo, ob[sl[2]], oa[sl[2]]), jnp.where(lane_lo, ob[sl[3]], oa[sl[3]]))
        for p in range(4):
            mix_scr[rows, A_WIDTH + p * LANES:A_WIDTH + (p + 1) * LANES] = tiles[p]

    last = slice(TM - WINDOW, TM)
    kprev[...] = k[last]
    vprev[...] = v[last]
    kl_ref[...] = k[last]
    vl_ref[...] = v[last]
    val_ref[...] = va[last]

    x1 = x + _dot(mix_scr[...].astype(BF16), wout_ref[...])
    x1_ref[...] = x1
    h, ids, gates = _route(x1, nf_ref[...], wr_ref[...], br_ref[...])
    _emit_routing(h, ids, gates, h_ref, ri_ref, rg_ref, tc_ref, cnt_ref, idx_v, hprev)


def _emit_routing(h, ids, gates, h_ref, ri_ref, rg_ref, tc_ref, cnt_ref, idx_v, hprev):
    h3 = h.reshape(h_ref.shape)
    packed = _rank_pack(ids, cnt_ref, tc_ref)
    h_ref[...] = h3
    hprev[...] = h3
    ri_ref[...] = packed
    idx_v[...] = packed
    rg_ref[...] = gates


def _emit_and_stage(h, ids, gates, h_ref, ri_ref, rg_ref, tc_ref, cnt_ref, stg_ref, idx_v, idx_s, psem):
    _emit_routing(h, ids, gates, h_ref, ri_ref, rg_ref, tc_ref, cnt_ref, idx_v, h_ref)
    to_smem = pltpu.make_async_copy(idx_v, idx_s, psem)
    to_smem.start()
    to_smem.wait()
    _stage_rows(idx_s, h_ref, stg_ref, unrolled=False)


def _const_spec(shape):
    nd = len(shape)
    return pl.BlockSpec(shape, lambda i, _n=nd: (0,) * _n)


def _mix0_prompt(x_all, nm, win, lng, lnb, wsp, bs_full, bias_p, wout, nf, wr, br):
    row_spec = pl.BlockSpec((TM, D_MODEL), lambda i: (i, 0))
    row3_spec = pl.BlockSpec((TM, ROW_TILE, LANES), lambda i: (i, 0, 0))
    lane_spec = pl.BlockSpec((TM, LANES), lambda i: (i, 0))
    last_kv = pl.BlockSpec((None, WINDOW, KV_WIDTH), lambda i: (jnp.minimum(i // STEPS_PER_BATCH, BATCH - 1), 0, 0))
    last_va = pl.BlockSpec((None, WINDOW, A_WIDTH), lambda i: (jnp.minimum(i // STEPS_PER_BATCH, BATCH - 1), 0, 0))
    return pl.pallas_call(
        _prompt_steps(_mix0_prompt_kernel, 12, 12 + 9),
        grid=(N_ROW_BLOCKS,),
        in_specs=[pl.BlockSpec((TM, D_MODEL), lambda i: (jnp.minimum(i, N_PROMPT_BLOCKS - 1), 0)),
                  _const_spec((1, D_MODEL)), _const_spec((D_MODEL, IN_WIDTH)),
                  _const_spec((1, A_WIDTH)), _const_spec((1, A_WIDTH)),
                  _const_spec((A_HEADS // 2, CHUNK, 2 * CHUNK)), _const_spec((CHUNK, A_WIDTH)),
                  _const_spec((2, 2, 4 * WINDOW, 2 * WINDOW)),
                  _const_spec((A_WIDTH + Q_WIDTH, D_MODEL)), _const_spec((1, D_MODEL)),
                  _const_spec((D_MODEL, 2 * LANES)), _const_spec((1, LANES))],
        out_specs=[row_spec, row3_spec, pl.BlockSpec((8, TM), lambda i: (0, i)), lane_spec,
                   pl.BlockSpec((None, 1, LANES), lambda i: (i, 0, 0)),
                   last_kv, last_kv, last_va, _const_spec((1, LANES)), _STAGED_SPEC],
        out_shape=[jax.ShapeDtypeStruct((T_ALL, D_MODEL), F32), jax.ShapeDtypeStruct((T_ALL, ROW_TILE, LANES), BF16),
                   jax.ShapeDtypeStruct((8, T_ALL), jnp.int32), jax.ShapeDtypeStruct((T_ALL, LANES), F32),
                   jax.ShapeDtypeStruct((N_ROW_BLOCKS, 1, LANES), F32),
                   jax.ShapeDtypeStruct((BATCH, WINDOW, KV_WIDTH), F32),
                   jax.ShapeDtypeStruct((BATCH, WINDOW, KV_WIDTH), F32),
                   jax.ShapeDtypeStruct((BATCH, WINDOW, A_WIDTH), F32),
                   jax.ShapeDtypeStruct((1, LANES), F32), _STAGED_SHAPE],
        scratch_shapes=_stage_scratch() + [pltpu.VMEM((WINDOW, KV_WIDTH), F32), pltpu.VMEM((WINDOW, KV_WIDTH), F32),
                                           pltpu.VMEM((TM, D_MODEL), F32)],
        compiler_params=_cparams(("arbitrary",)),
        name="mix0_prompt",
    )(x_all, nm, win, lng, lnb, wsp, bs_full, bias_p, wout, nf, wr, br)


def _mix0_sample_kernel(x_ref, nm_ref, win_ref, lng_ref, lnb_ref, wcoef_ref, bcoef_ref,
                        ck_ref, cv_ref, bsc_ref, bsn_ref,
                        wout_ref, nf_ref, wr_ref, br_ref, cnt_in,
                        x1_in, h_in, ri_in, rg_in, tc_in,
                        x1_ref, h_ref, ri_ref, rg_ref, tc_ref, kn_ref, vn_ref, va_ref, cnt_ref, stg_ref,
                        q_scr, k_scr, v_scr, mix_scr, idx_v, idx_s, psem):
    del x1_in, h_in, ri_in, rg_in, tc_in
    g = pl.program_id(0)

    @pl.when(g == 0)
    def _():
        u, va, q, k, v = _project(x_ref[...], nm_ref[...], win_ref[...], lng_ref[...], lnb_ref[...])
        q_scr[...] = q
        k_scr[...] = k
        v_scr[...] = v
        va_ref[...] = va
        for t in range(DEC_SEQ):
            acc = jnp.zeros((DEC_BATCH, A_WIDTH), F32) + bcoef_ref[t:t + 1, :]
            for s in range(t + 1):
                row = t * DEC_SEQ + s
                acc = acc + wcoef_ref[row:row + 1, :] * va[s * DEC_BATCH:(s + 1) * DEC_BATCH]
            mix_scr[t * DEC_BATCH:(t + 1) * DEC_BATCH, :A_WIDTH] = u[t * DEC_BATCH:(t + 1) * DEC_BATCH] * acc

    b0 = pl.multiple_of(g * SAMPLE_GROUP, SAMPLE_GROUP)
    lane_lo = lax.broadcasted_iota(jnp.int32, (DEC_SEQ * SAMPLE_GROUP, LANES), 1) < B_HEAD_DIM

    def grab(ref, width):
        return jnp.concatenate([ref[pl.ds(t * DEC_BATCH + b0, SAMPLE_GROUP), :] for t in range(DEC_SEQ)], axis=0)

    qg = grab(q_scr, Q_WIDTH)
    kn = grab(k_scr, KV_WIDTH)
    vn = grab(v_scr, KV_WIDTH)

    lane_w = lax.broadcasted_iota(jnp.int32, (KV_WIDTH, WINDOW), 1)
    n_new = DEC_SEQ * SAMPLE_GROUP

    def new_window(c_ref, new_rows, w_ref):
        nt = jnp.transpose(jnp.concatenate([new_rows, jnp.zeros((WINDOW - n_new, KV_WIDTH), F32)], axis=0))
        for b in range(SAMPLE_GROUP):
            w = pltpu.roll(c_ref[b].reshape(KV_WIDTH, WINDOW), WINDOW - DEC_SEQ, 1)
            for t in range(DEC_SEQ):
                src = t * SAMPLE_GROUP + b
                dst = WINDOW - DEC_SEQ + t
                w = jnp.where(lane_w == dst, pltpu.roll(nt, (dst - src) % WINDOW, 1), w)
            w_ref[b] = w.reshape(B_KV_HEADS, B_HEAD_DIM, WINDOW)

    new_window(ck_ref, kn, kn_ref)
    new_window(cv_ref, vn, vn_ref)
    ccol0 = lax.broadcasted_iota(jnp.int32, (KV_WIDTH, SAMPLE_GROUP * WINDOW), 1) == 0

    def cache_t(ref):
        t = jnp.concatenate([ref[b].reshape(KV_WIDTH, WINDOW) for b in range(SAMPLE_GROUP)], axis=1)
        return jnp.where(ccol0, 0.0, t)

    def head_swap(t):
        return jnp.concatenate([t[B_HEAD_DIM:], t[:B_HEAD_DIM]], axis=0)

    kct = cache_t(ck_ref)
    vct = cache_t(cv_ref)
    kc_ops = (kct.astype(BF16), head_swap(kct).astype(BF16))
    vc_ops = (vct.astype(BF16), head_swap(vct).astype(BF16))
    kn_ops = (kn.astype(BF16), pltpu.roll(kn, B_HEAD_DIM, 1).astype(BF16))
    vn_ops = (vn.astype(BF16), pltpu.roll(vn, B_HEAD_DIM, 1).astype(BF16))
    qt = [qg[:, p * LANES:(p + 1) * LANES] for p in range(4)]
    q_even = [jnp.where(lane_lo, t, 0.0) for t in qt]
    q_odd = [jnp.where(lane_lo, 0.0, t) for t in qt]
    stacks = (jnp.concatenate([q_even[0], q_even[1], q_odd[2], q_odd[3]], axis=0),
              jnp.concatenate([q_odd[0], q_odd[1], q_even[2], q_even[3]], axis=0))
    o = []
    for st in range(2):
        qs = stacks[st].astype(BF16)
        sc = _dot(qs, kc_ops[st]) + bsc_ref[st]
        sn = _dot_nt(qs, kn_ops[st]) + bsn_ref[st][:, :DEC_SEQ * SAMPLE_GROUP]
        m = jnp.maximum(jnp.max(sc, axis=-1, keepdims=True), jnp.max(sn, axis=-1, keepdims=True))
        pc = jnp.exp(sc - m)
        pn = jnp.exp(sn - m)
        den = jnp.sum(pc, axis=-1, keepdims=True) + jnp.sum(pn, axis=-1, keepdims=True)
        o.append((_dot_nt(pc.astype(BF16), vc_ops[st]) + _dot(pn.astype(BF16), vn_ops[st])) / den)
    oa, ob = o
    n = DEC_SEQ * SAMPLE_GROUP
    sl = [slice(i * n, (i + 1) * n) for i in range(4)]
    tiles = (jnp.where(lane_lo, oa[sl[0]], ob[sl[0]]), jnp.where(lane_lo, oa[sl[1]], ob[sl[1]]),
             jnp.where(lane_lo, ob[sl[2]], oa[sl[2]]), jnp.where(lane_lo, ob[sl[3]], oa[sl[3]]))
    for p in range(4):
        for t in range(DEC_SEQ):
            mix_scr[pl.ds(t * DEC_BATCH + b0, SAMPLE_GROUP), A_WIDTH + p * LANES:A_WIDTH + (p + 1) * LANES] = (
                tiles[p][t * SAMPLE_GROUP:(t + 1) * SAMPLE_GROUP])

    @pl.when(g == N_SAMPLE_GROUPS - 1)
    def _():
        x1 = x_ref[...] + _dot(mix_scr[...].astype(BF16), wout_ref[...])
        x1_ref[...] = x1
        h, ids, gates = _route(x1, nf_ref[...], wr_ref[...], br_ref[...])
        cnt_ref[...] = cnt_in[...]
        _emit_and_stage(h, ids, gates, h_ref, ri_ref, rg_ref, tc_ref, cnt_ref, stg_ref, idx_v, idx_s, psem)


def _mix0_sample(x_all, nm, win, lng, lnb, wcoef, bcoef, ck, cv, bias_sc, bias_sn, wout, nf, wr, br, cnt,
                 x1_all, h_all, ri_all, rg_all, tc_all):
    sample_rows = pl.BlockSpec((TM, D_MODEL), lambda g: (N_PROMPT_BLOCKS, 0))
    sample_rows3 = pl.BlockSpec((TM, ROW_TILE, LANES), lambda g: (N_PROMPT_BLOCKS, 0, 0))
    sample_lanes = pl.BlockSpec((TM, LANES), lambda g: (N_PROMPT_BLOCKS, 0))
    cache_spec = pl.BlockSpec((SAMPLE_GROUP, B_KV_HEADS, B_HEAD_DIM, WINDOW), lambda g: (g, 0, 0, 0))
    anyspec = pl.BlockSpec(memory_space=pl.ANY)
    n_in = 16
    return pl.pallas_call(
        _mix0_sample_kernel,
        grid=(N_SAMPLE_GROUPS,),
        in_specs=[_const_spec((TM, D_MODEL)), _const_spec((1, D_MODEL)), _const_spec((D_MODEL, IN_WIDTH)),
                  _const_spec((1, A_WIDTH)), _const_spec((1, A_WIDTH)),
                  _const_spec((16, A_WIDTH)), _const_spec((8, A_WIDTH)),
                  cache_spec, cache_spec,
                  _const_spec((2, 4 * 32, SAMPLE_GROUP * WINDOW)), _const_spec((2, 4 * 32, LANES)),
                  _const_spec((A_WIDTH + Q_WIDTH, D_MODEL)), _const_spec((1, D_MODEL)),
                  _const_spec((D_MODEL, 2 * LANES)), _const_spec((1, LANES)), _const_spec((1, LANES)),
                  anyspec, anyspec, anyspec, anyspec, anyspec],
        out_specs=[sample_rows, sample_rows3, pl.BlockSpec((8, TM), lambda g: (0, N_PROMPT_BLOCKS)), sample_lanes,
                   pl.BlockSpec((None, 1, LANES), lambda g: (N_PROMPT_BLOCKS, 0, 0)),
                   cache_spec, cache_spec,
                   _const_spec((T_SAMPLE, A_WIDTH)), _const_spec((1, LANES)),
                   _const_spec((TM * TOP_K, ROW_TILE, LANES))],
        out_shape=[jax.ShapeDtypeStruct((T_ALL, D_MODEL), F32), jax.ShapeDtypeStruct((T_ALL, ROW_TILE, LANES), BF16),
                   jax.ShapeDtypeStruct((8, T_ALL), jnp.int32), jax.ShapeDtypeStruct((T_ALL, LANES), F32),
                   jax.ShapeDtypeStruct((N_ROW_BLOCKS, 1, LANES), F32),
                   jax.ShapeDtypeStruct((DEC_BATCH, B_KV_HEADS, B_HEAD_DIM, WINDOW), F32),
                   jax.ShapeDtypeStruct((DEC_BATCH, B_KV_HEADS, B_HEAD_DIM, WINDOW), F32),
                   jax.ShapeDtypeStruct((T_SAMPLE, A_WIDTH), F32), jax.ShapeDtypeStruct((1, LANES), F32),
                   jax.ShapeDtypeStruct((TM * TOP_K, ROW_TILE, LANES), BF16)],
        scratch_shapes=[pltpu.VMEM((T_SAMPLE, Q_WIDTH), F32), pltpu.VMEM((T_SAMPLE, KV_WIDTH), F32),
                        pltpu.VMEM((T_SAMPLE, KV_WIDTH), F32), pltpu.VMEM((T_SAMPLE, D_MODEL), F32),
                        pltpu.VMEM((8, TM), jnp.int32), pltpu.SMEM((8, TM), jnp.int32), pltpu.SemaphoreType.DMA(())],
        input_output_aliases={n_in: 0, n_in + 1: 1, n_in + 2: 2, n_in + 3: 3, n_in + 4: 4},
        compiler_params=_cparams(("arbitrary",)),
        name="mix0_sample",
    )(x_all, nm, win, lng, lnb, wcoef, bcoef, ck, cv, bias_sc, bias_sn, wout, nf, wr, br, cnt,
      x1_all, h_all, ri_all, rg_all, tc_all)


def _moe_metadata(rt_all, cnt, tcnt):
    counts = cnt[0, :N_EXPERTS].astype(jnp.int32)
    padded = (counts + MOE_BLK - 1) // MOE_BLK * MOE_BLK
    pad_end = jnp.cumsum(padded)
    pad_start = pad_end - padded
    experts = jnp.arange(N_EXPERTS, dtype=jnp.int32)
    n_valid = (pad_end[-1] // MOE_BLK).astype(jnp.int32).reshape(1)
    blk_start = jnp.arange(N_MOE_BLOCKS, dtype=jnp.int32) * MOE_BLK
    block_e = jnp.minimum(jnp.sum((blk_start[:, None] >= pad_end[None, :]).astype(jnp.int32), axis=1),
                          N_EXPERTS - 1).astype(jnp.int32)
    zero_start = (pad_start + counts).astype(jnp.int32)
    zero_len = (padded - counts).astype(jnp.int32)
    first = (blk_start == pad_start[block_e]).astype(jnp.int32)
    used = counts > 0
    parity = ((jnp.cumsum(used.astype(jnp.int32)) - 1) % 2)[block_e].astype(jnp.int32)
    nearest = lax.cummin(jnp.where(used, experts, N_EXPERTS)[::-1])[::-1]
    next_used = jnp.concatenate([nearest[1:], jnp.full((1,), N_EXPERTS, jnp.int32)])
    nxt = jnp.where(next_used < N_EXPERTS, next_used, -1)[block_e].astype(jnp.int32)
    plan = (block_e, first, parity, nxt, n_valid)
    runs = tcnt[:, 0, :N_EXPERTS].astype(jnp.int32)
    run_dst = pad_start[None, :] + jnp.cumsum(runs, axis=0) - runs
    lpos = rt_all[2 * TOP_K:3 * TOP_K].reshape(N_SLOTS).astype(jnp.int32)
    cplan = (lpos, runs.reshape(-1), run_dst.reshape(-1).astype(jnp.int32))
    dplan = cplan[1:] + (jnp.concatenate([zero_start, zero_len, n_valid]),)
    return plan, dplan, cplan


RUN_PIECE = 32


def _for_run_pieces(n, start_piece):
    whole = n // RUN_PIECE

    def body(j, carry):
        start_piece(j * RUN_PIECE, RUN_PIECE)
        return carry

    lax.fori_loop(0, whole, body, 0)
    o = whole * RUN_PIECE
    bit = RUN_PIECE // 2
    while bit >= 1:
        take = (n & bit) != 0

        @pl.when(take)
        def _(o=o, bit=bit):
            start_piece(o, bit)

        o = o + jnp.where(take, bit, 0)
        bit //= 2


def _dispatch_kernel(run_ref, rdst_ref, zs_ref, stp_ref, sts_ref, xs_ref, zero_scr, sem, zsem):
    i = pl.program_id(0)

    @pl.when(i == 0)
    def _():
        zero_scr[...] = jnp.zeros_like(zero_scr)

        def pieces(e, do):
            off = zs_ref[e]
            rem = zs_ref[N_EXPERTS + e]
            bit = MOE_BLK // 2
            while bit >= 1:
                take = (rem & bit) != 0

                @pl.when(take)
                def _(off=off, bit=bit):
                    do(pltpu.make_async_copy(zero_scr.at[pl.ds(0, bit)], xs_ref.at[pl.ds(off, bit)], zsem))

                off = off + jnp.where(take, bit, 0)
                bit //= 2

        def start_e(e, c):
            pieces(e, lambda cp: cp.start())
            return c

        def wait_e(e, c):
            pieces(e, lambda cp: cp.wait())
            return c

        def tail(do):
            def step(b, c):
                do(pltpu.make_async_copy(zero_scr, xs_ref.at[pl.ds(b * MOE_BLK, MOE_BLK)], zsem))
                return c
            return step

        n_valid = zs_ref[2 * N_EXPERTS]
        lax.fori_loop(0, N_EXPERTS, start_e, 0)
        lax.fori_loop(n_valid, N_MOE_BLOCKS, tail(lambda cp: cp.start()), 0)
        lax.fori_loop(0, N_EXPERTS, wait_e, 0)
        lax.fori_loop(n_valid, N_MOE_BLOCKS, tail(lambda cp: cp.wait()), 0)

    def send(src_ref):
        def send_run(e, off):
            n = run_ref[i * N_EXPERTS + e]
            dst = rdst_ref[i * N_EXPERTS + e]
            _for_run_pieces(n, lambda o, size: pltpu.make_async_copy(
                src_ref.at[pl.ds(off + o, size)], xs_ref.at[pl.ds(dst + o, size)], sem).start(
                    priority=size.bit_length() % 2))
            return off + n

        lax.fori_loop(0, N_EXPERTS, send_run, 0)

    @pl.when(i < N_PROMPT_BLOCKS)
    def _():
        send(stp_ref)

    @pl.when(i >= N_PROMPT_BLOCKS)
    def _():
        send(sts_ref)

    pltpu.make_async_copy(sts_ref, xs_ref.at[pl.ds(0, TM * TOP_K)], sem).wait()


def _dispatch(dplan, staged_prompt, staged_sample):
    tile = (TM * TOP_K, ROW_TILE, LANES)
    return pl.pallas_call(
        _dispatch_kernel,
        grid_spec=pltpu.PrefetchScalarGridSpec(
            num_scalar_prefetch=3,
            grid=(N_ROW_BLOCKS,),
            in_specs=[pl.BlockSpec(tile, lambda i, *_: (jnp.minimum(i, N_PROMPT_BLOCKS - 1), 0, 0)),
                      pl.BlockSpec(tile, lambda i, *_: (0, 0, 0))],
            out_specs=pl.BlockSpec(memory_space=pl.ANY),
            scratch_shapes=[pltpu.VMEM((MOE_BLK, ROW_TILE, LANES), BF16),
                            pltpu.SemaphoreType.DMA(()), pltpu.SemaphoreType.DMA(())],
        ),
        out_shape=jax.ShapeDtypeStruct((N_SORT_ROWS, ROW_TILE, LANES), BF16),
        compiler_params=_cparams(("arbitrary",)),
        name="moe_dispatch",
    )(*dplan, staged_prompt, staged_sample)


def _experts_kernel(layer, be_ref, first_ref, par_ref, nxt_ref, nv_ref,
                    x_ref, wg_hbm, wu_hbm, wd_hbm, y_ref,
                    wg_s, wu_s, wd_s, wg_f, wu_f, wd_f, wsem):
    i = pl.program_id(0)

    def fetch(e, slot):
        return (pltpu.make_async_copy(wg_hbm.at[layer, e], wg_f.at[slot], wsem.at[slot]),
                pltpu.make_async_copy(wu_hbm.at[layer, e], wu_f.at[slot], wsem.at[slot]),
                pltpu.make_async_copy(wd_hbm.at[layer, e], wd_f.at[slot], wsem.at[slot]))

    @pl.when(i < nv_ref[0])
    def _():
        e = be_ref[i]
        slot = par_ref[i]

        @pl.when(i == 0)
        def _():
            for cp in fetch(e, slot):
                cp.start()

        @pl.when(first_ref[i] == 1)
        def _():
            for cp in fetch(e, slot):
                cp.wait()
            wg_s[...] = wg_f[slot].astype(BF16)
            wu_s[...] = wu_f[slot].astype(BF16)
            wd_s[...] = wd_f[slot].astype(BF16)
            nxt = nxt_ref[i]

            @pl.when(nxt >= 0)
            def _():
                for cp in fetch(nxt, 1 - slot):
                    cp.start()

        xb = x_ref[...].reshape(MOE_BLK, D_MODEL)
        a = jax.nn.silu(_dot(xb, wg_s[...])) * _dot(xb, wu_s[...])
        y_ref[...] = _dot(a.astype(BF16), wd_s[...]).reshape(y_ref.shape)

    @pl.when(i >= nv_ref[0])
    def _():
        y_ref[...] = jnp.zeros(y_ref.shape, y_ref.dtype)


def _experts(block_e, first, parity, nxt, n_valid, xs, w_gate, w_up, w_down, layer):
    def blk(i, be, fi, pa, nx, nv):
        return (jnp.maximum(jnp.minimum(i, nv[0] - 1), 0), 0, 0)

    anyspec = pl.BlockSpec(memory_space=pl.ANY)
    return pl.pallas_call(
        functools.partial(_experts_kernel, layer),
        grid_spec=pltpu.PrefetchScalarGridSpec(
            num_scalar_prefetch=5,
            grid=(N_MOE_BLOCKS,),
            in_specs=[pl.BlockSpec((MOE_BLK, ROW_TILE, LANES), blk), anyspec, anyspec, anyspec],
            out_specs=pl.BlockSpec((MOE_BLK, ROW_TILE, LANES), lambda i, be, fi, pa, nx, nv: (i, 0, 0)),
            scratch_shapes=[pltpu.VMEM((D_MODEL, D_EXPERT), BF16), pltpu.VMEM((D_MODEL, D_EXPERT), BF16),
                            pltpu.VMEM((D_EXPERT, D_MODEL), BF16),
                            pltpu.VMEM((2, D_MODEL, D_EXPERT), F32), pltpu.VMEM((2, D_MODEL, D_EXPERT), F32),
                            pltpu.VMEM((2, D_EXPERT, D_MODEL), F32), pltpu.SemaphoreType.DMA((2,))],
        ),
        out_shape=jax.ShapeDtypeStruct((N_SORT_ROWS, ROW_TILE, LANES), F32),
        compiler_params=_cparams(("arbitrary",)),
        name="moe_experts",
    )(block_e, first, parity, nxt, n_valid, xs, w_gate, w_up, w_down)


def _gather_rows(lpos_ref, run_ref, rdst_ref, ys_ref, ystage, ybufs, sem, i):
    def fetch(tile, buf):
        def fetch_run(e, off):
            n = run_ref[tile * N_EXPERTS + e]
            src = rdst_ref[tile * N_EXPERTS + e]
            _for_run_pieces(n, lambda o, size: pltpu.make_async_copy(
                ys_ref.at[pl.ds(src + o, size)], ystage.at[buf, pl.ds(off + o, size)], sem.at[buf]).start(
                    priority=size.bit_length() % 2))
            return off + n

        lax.fori_loop(0, N_EXPERTS, fetch_run, 0)

    def wait(buf):
        pltpu.make_async_copy(ys_ref.at[pl.ds(0, TM * TOP_K)], ystage.at[buf], sem.at[buf]).wait()

    cur = i % 2

    @pl.when(i == 0)
    def _():
        fetch(i, 0)
        wait(0)

        def unplace(r, carry):
            for kk in range(TOP_K):
                ybufs[0][kk, r] = ystage[0, lpos_ref[kk * T_ALL + r]]
            return carry

        lax.fori_loop(0, TM, unplace, 0, unroll=8)
        fetch(i + 1, 1)

    @pl.when(i + 1 < N_ROW_BLOCKS)
    def _():
        wait(1 - cur)

    @pl.when(i + 2 < N_ROW_BLOCKS)
    def _():
        fetch(i + 2, cur)

    def pieces(compute, store):
        nxt = jnp.minimum(i + 1, N_ROW_BLOCKS - 1)

        def variant(par):
            ycur, ynext = ybufs[par], ybufs[1 - par]

            def piece(j, carry):
                rows = pl.ds(pl.multiple_of(j * COMBINE_ROWS, COMBINE_ROWS), COMBINE_ROWS)
                out = compute(rows, ycur[0, rows].reshape(COMBINE_ROWS, D_MODEL),
                              ycur[1, rows].reshape(COMBINE_ROWS, D_MODEL))
                base = nxt * TM + j * COMBINE_ROWS
                for r in range(COMBINE_ROWS):
                    for kk in range(TOP_K):
                        ynext[kk, j * COMBINE_ROWS + r] = ystage[1 - par, lpos_ref[kk * T_ALL + base + r]]
                store(rows, out)
                return carry

            lax.fori_loop(0, TM // COMBINE_ROWS, piece, 0)

        for par in range(2):
            @pl.when(cur == par)
            def _(par=par):
                variant(par)

    return pieces


COMBINE_ROWS = 64


def _combined(x_ref, rg_ref, rows, y0, y1):
    rg = rg_ref[rows, :]
    return x_ref[rows, :] + rg[:, 0:1] * y0 + rg[:, 1:2] * y1


_COMBINE_SCRATCH = [pltpu.VMEM((2, TM * TOP_K, ROW_TILE, LANES), F32),
                    pltpu.VMEM((TOP_K, TM, ROW_TILE, LANES), F32), pltpu.VMEM((TOP_K, TM, ROW_TILE, LANES), F32),
                    pltpu.SemaphoreType.DMA((2,))]


def _combine_kernel(lpos_ref, run_ref, rdst_ref, x_ref, rg_ref, ys_ref, o_ref, ystage, ybuf0, ybuf1, sem):
    pieces = _gather_rows(lpos_ref, run_ref, rdst_ref, ys_ref, ystage, (ybuf0, ybuf1), sem, pl.program_id(0))

    def store(rows, out):
        o_ref[rows, :] = out

    pieces(functools.partial(_combined, x_ref, rg_ref), store)


def _combine(cplan, x_all, rg_all, ys):
    return pl.pallas_call(
        _combine_kernel,
        grid_spec=pltpu.PrefetchScalarGridSpec(
            num_scalar_prefetch=3,
            grid=(N_ROW_BLOCKS,),
            in_specs=[pl.BlockSpec((TM, D_MODEL), lambda i, a, b, c: (i, 0)),
                      pl.BlockSpec((TM, LANES), lambda i, a, b, c: (i, 0)),
                      pl.BlockSpec(memory_space=pl.ANY)],
            out_specs=pl.BlockSpec((TM, D_MODEL), lambda i, a, b, c: (i, 0)),
            scratch_shapes=_COMBINE_SCRATCH,
        ),
        out_shape=jax.ShapeDtypeStruct((T_ALL, D_MODEL), F32),
        compiler_params=_cparams(("arbitrary",)),
        name="moe_combine",
    )(*cplan, x_all, rg_all, ys)


def _final_kernel(lpos_ref, run_ref, rdst_ref, x_ref, rg_ref, ys_ref, nfin_ref, op_ref, os_ref,
                  ystage, ybuf0, ybuf1, sem):
    i = pl.program_id(0)
    pieces = _gather_rows(lpos_ref, run_ref, rdst_ref, ys_ref, ystage, (ybuf0, ybuf1), sem, i)

    def compute(rows, y0, y1):
        return _rms(_combined(x_ref, rg_ref, rows, y0, y1), nfin_ref[...])

    def store(rows, y):
        @pl.when(i < N_PROMPT_BLOCKS)
        def _():
            op_ref[rows, :] = y

        @pl.when(i >= N_PROMPT_BLOCKS)
        def _():
            os_ref[rows, :] = y

    pieces(compute, store)


def _final(cplan, x_all, rg_all, ys, nfin):
    return pl.pallas_call(
        _final_kernel,
        grid_spec=pltpu.PrefetchScalarGridSpec(
            num_scalar_prefetch=3,
            grid=(N_ROW_BLOCKS,),
            in_specs=[pl.BlockSpec((TM, D_MODEL), lambda i, a, b, c: (i, 0)),
                      pl.BlockSpec((TM, LANES), lambda i, a, b, c: (i, 0)),
                      pl.BlockSpec(memory_space=pl.ANY),
                      pl.BlockSpec((1, D_MODEL), lambda i, a, b, c: (0, 0))],
            out_specs=[pl.BlockSpec((TM, D_MODEL), lambda i, a, b, c: (jnp.minimum(i, N_PROMPT_BLOCKS - 1), 0)),
                       pl.BlockSpec((TM, D_MODEL), lambda i, a, b, c: (0, 0))],
            scratch_shapes=_COMBINE_SCRATCH,
        ),
        out_shape=[jax.ShapeDtypeStruct((T_PROMPT, D_MODEL), F32), jax.ShapeDtypeStruct((T_SAMPLE, D_MODEL), F32)],
        compiler_params=_cparams(("arbitrary",)),
        name="moe_combine_final",
    )(*cplan, x_all, rg_all, ys, nfin)


def _moe(staged_prompt, staged_sample, rt_all, cnt, tcnt, w_gate, w_up, w_down, layer):
    plan, dplan, cplan = _moe_metadata(rt_all, cnt, tcnt)
    xs = _dispatch(dplan, staged_prompt, staged_sample)
    ys = _experts(*plan, xs, w_gate, w_up, w_down, layer)
    return cplan, ys


def _pool_project(d_groups, wp_ref, scale):
    outs = [_dot(d_groups[g].astype(BF16), wp_ref[g]) for g in range(len(POOL_SIZES))]
    return jnp.concatenate(outs, axis=1) * scale


def _mix1_prompt_kernel(x_ref, nm_ref, wp_ref, sc_ref, nf_ref, wr_ref, br_ref,
                        x3_ref, h_ref, ri_ref, rg_ref, tc_ref, pl_ref, cnt_ref, ext, idx_v, hprev):
    i = pl.program_id(0)

    @pl.when(i == 0)
    def _():
        cnt_ref[...] = jnp.zeros_like(cnt_ref)

    x = x_ref[...]
    hp = _rms(x, nm_ref[...])

    @pl.when(i % STEPS_PER_BATCH == 0)
    def _():
        ext[0:POOL_MAX, :] = jnp.zeros((POOL_MAX, D_MODEL), F32)

    ext[POOL_MAX:, :] = hp
    pos = (i % STEPS_PER_BATCH) * TM + lax.broadcasted_iota(jnp.int32, (TM, 1), 0)
    d_groups = []
    for g, w in enumerate(POOL_SIZES):
        cols = slice(g * POOL_GROUP_DIM, (g + 1) * POOL_GROUP_DIM)
        acc = ext[:, cols]
        span = 1
        while span < w:
            acc = acc + pltpu.roll(acc, span, 0)
            span *= 2
        cnt = jnp.minimum(pos + 1, w).astype(F32)
        d_groups.append(acc[POOL_MAX:] / cnt - hp[:, cols])
    tail = hp[TM - POOL_MAX:, :]
    ext[0:POOL_MAX, :] = tail
    pl_ref[...] = tail

    x3 = x + _pool_project(d_groups, wp_ref, sc_ref[...])
    x3_ref[...] = x3
    h, ids, gates = _route(x3, nf_ref[...], wr_ref[...], br_ref[...])
    _emit_routing(h, ids, gates, h_ref, ri_ref, rg_ref, tc_ref, cnt_ref, idx_v, hprev)


def _mix1_prompt(x_all, nm, wp, sc, nf, wr, br):
    row_spec = pl.BlockSpec((TM, D_MODEL), lambda i: (i, 0))
    row3_spec = pl.BlockSpec((TM, ROW_TILE, LANES), lambda i: (i, 0, 0))
    lane_spec = pl.BlockSpec((TM, LANES), lambda i: (i, 0))
    return pl.pallas_call(
        _prompt_steps(_mix1_prompt_kernel, 7, 7 + 7),
        grid=(N_ROW_BLOCKS,),
        in_specs=[row_spec, _const_spec((1, D_MODEL)),
                  _const_spec((len(POOL_SIZES), POOL_GROUP_DIM, POOL_GROUP_DIM)), _const_spec((1, D_MODEL)),
                  _const_spec((1, D_MODEL)), _const_spec((D_MODEL, 2 * LANES)), _const_spec((1, LANES))],
        out_specs=[row_spec, row3_spec, pl.BlockSpec((8, TM), lambda i: (0, i)), lane_spec,
                   pl.BlockSpec((None, 1, LANES), lambda i: (i, 0, 0)),
                   pl.BlockSpec((None, POOL_MAX, D_MODEL),
                                lambda i: (jnp.minimum(i // STEPS_PER_BATCH, BATCH - 1), 0, 0)),
                   _const_spec((1, LANES)), _STAGED_SPEC],
        out_shape=[jax.ShapeDtypeStruct((T_ALL, D_MODEL), F32), jax.ShapeDtypeStruct((T_ALL, ROW_TILE, LANES), BF16),
                   jax.ShapeDtypeStruct((8, T_ALL), jnp.int32), jax.ShapeDtypeStruct((T_ALL, LANES), F32),
                   jax.ShapeDtypeStruct((N_ROW_BLOCKS, 1, LANES), F32),
                   jax.ShapeDtypeStruct((BATCH, POOL_MAX, D_MODEL), F32), jax.ShapeDtypeStruct((1, LANES), F32),
                   _STAGED_SHAPE],
        scratch_shapes=_stage_scratch() + [pltpu.VMEM((POOL_MAX + TM, D_MODEL), F32)],
        compiler_params=_cparams(("arbitrary",)),
        name="mix1_prompt",
    )(x_all, nm, wp, sc, nf, wr, br)


def _mix1_sample_kernel(x_ref, st_ref, nm_ref, wp_ref, sc_ref, nf_ref, wr_ref, br_ref, cnt_in,
                        x3_in, h_in, ri_in, rg_in, tc_in,
                        x3_ref, h_ref, ri_ref, rg_ref, tc_ref, hs_ref, cnt_ref, stg_ref, idx_v, idx_s, psem):
    del x3_in, h_in, ri_in, rg_in, tc_in
    x = x_ref[...]
    hs = _rms(x, nm_ref[...])
    hs_ref[...] = hs
    n_ctx = POOL_MAX - 1
    d_groups = []
    for g, w in enumerate(POOL_SIZES):
        cols = slice(g * POOL_GROUP_DIM, (g + 1) * POOL_GROUP_DIM)
        parts = []
        for t in range(DEC_SEQ):
            acc = hs[t * DEC_BATCH:(t + 1) * DEC_BATCH, cols]
            for back in range(1, w):
                src = t - back
                if src >= 0:
                    acc = acc + hs[src * DEC_BATCH:(src + 1) * DEC_BATCH, cols]
                else:
                    acc = acc + st_ref[n_ctx + src, :, cols]
            parts.append(acc / float(w) - hs[t * DEC_BATCH:(t + 1) * DEC_BATCH, cols])
        d_groups.append(jnp.concatenate(parts, axis=0))
    x3 = x + _pool_project(d_groups, wp_ref, sc_ref[...])
    x3_ref[...] = x3
    h, ids, gates = _route(x3, nf_ref[...], wr_ref[...], br_ref[...])
    cnt_ref[...] = cnt_in[...]
    _emit_and_stage(h, ids, gates, h_ref, ri_ref, rg_ref, tc_ref, cnt_ref, stg_ref, idx_v, idx_s, psem)


def _mix1_sample(x_all, state_t, nm, wp, sc, nf, wr, br, cnt, x3_all, h_all, ri_all, rg_all, tc_all):
    sample_rows = pl.BlockSpec((TM, D_MODEL), lambda g: (N_PROMPT_BLOCKS, 0))
    sample_rows3 = pl.BlockSpec((TM, ROW_TILE, LANES), lambda g: (N_PROMPT_BLOCKS, 0, 0))
    sample_lanes = pl.BlockSpec((TM, LANES), lambda g: (N_PROMPT_BLOCKS, 0))
    anyspec = pl.BlockSpec(memory_space=pl.ANY)
    n_in = 9
    return pl.pallas_call(
        _mix1_sample_kernel,
        grid=(1,),
        in_specs=[sample_rows, _const_spec((POOL_MAX - 1, DEC_BATCH, D_MODEL)), _const_spec((1, D_MODEL)),
                  _const_spec((len(POOL_SIZES), POOL_GROUP_DIM, POOL_GROUP_DIM)), _const_spec((1, D_MODEL)),
                  _const_spec((1, D_MODEL)), _const_spec((D_MODEL, 2 * LANES)), _const_spec((1, LANES)),
                  _const_spec((1, LANES)), anyspec, anyspec, anyspec, anyspec, anyspec],
        out_specs=[sample_rows, sample_rows3, pl.BlockSpec((8, TM), lambda g: (0, N_PROMPT_BLOCKS)), sample_lanes,
                   pl.BlockSpec((None, 1, LANES), lambda g: (N_PROMPT_BLOCKS, 0, 0)),
                   _const_spec((T_SAMPLE, D_MODEL)), _const_spec((1, LANES)),
                   _const_spec((TM * TOP_K, ROW_TILE, LANES))],
        out_shape=[jax.ShapeDtypeStruct((T_ALL, D_MODEL), F32), jax.ShapeDtypeStruct((T_ALL, ROW_TILE, LANES), BF16),
                   jax.ShapeDtypeStruct((8, T_ALL), jnp.int32), jax.ShapeDtypeStruct((T_ALL, LANES), F32),
                   jax.ShapeDtypeStruct((N_ROW_BLOCKS, 1, LANES), F32),
                   jax.ShapeDtypeStruct((T_SAMPLE, D_MODEL), F32), jax.ShapeDtypeStruct((1, LANES), F32),
                   jax.ShapeDtypeStruct((TM * TOP_K, ROW_TILE, LANES), BF16)],
        scratch_shapes=[pltpu.VMEM((8, TM), jnp.int32), pltpu.SMEM((8, TM), jnp.int32), pltpu.SemaphoreType.DMA(())],
        input_output_aliases={n_in: 0, n_in + 1: 1, n_in + 2: 2, n_in + 3: 3, n_in + 4: 4},
        compiler_params=_cparams(("arbitrary",)),
        name="mix1_sample",
    )(x_all, state_t, nm, wp, sc, nf, wr, br, cnt, x3_all, h_all, ri_all, rg_all, tc_all)


def _router_weights(wg, bg, we, be):
    w = jnp.concatenate([wg, jnp.transpose(we, (1, 0, 2)).reshape(D_MODEL, N_EXPERTS)], axis=1)
    b = jnp.concatenate([bg, be.reshape(N_EXPERTS)])
    pad = LANES - N_GROUPS - N_EXPERTS
    w = jnp.pad(w, ((0, 0), (0, pad)))
    w_hi = w.astype(BF16)
    w_lo = (w - w_hi.astype(F32)).astype(BF16)
    return jnp.concatenate([w_hi, w_lo], axis=1), jnp.pad(b, (0, pad)).reshape(1, LANES)


def kernel(x_prompt, x_sample, cache_k_win, cache_v_win, state_pool, norm_mix, norm_ffn, norm_final, w_in,
           a_ln_g, a_ln_b, a_w_s, a_b_s, b_sinks, rel_bias_table, w_out, c_w_pool, c_scale,
           router_group_w, router_group_b, router_expert_w, router_expert_b, w_gate, w_up, w_down):
    xs_t = jnp.transpose(x_sample, (1, 0, 2)).reshape(T_SAMPLE, D_MODEL)
    xp2 = x_prompt.reshape(T_PROMPT, D_MODEL)
    win =w_in[0].astype(BF16)
    wout = w_out[0].astype(BF16)
    lng = a_ln_g[0].reshape(1, A_WIDTH)
    lnb = a_ln_b[0].reshape(1, A_WIDTH)
    bias_p, bias_sc, bias_sn, wsp = _prep(rel_bias_table, b_sinks[0], a_w_s[0])
    bs_full = jnp.repeat(a_b_s[0].T, A_HEAD_DIM, axis=1)
    w4 = jnp.transpose(a_w_s[0][:, :DEC_SEQ, :DEC_SEQ], (1, 2, 0)).reshape(DEC_SEQ * DEC_SEQ, A_HEADS)
    wcoef = jnp.repeat(w4, A_HEAD_DIM, axis=1)
    bcoef = jnp.pad(jnp.repeat(a_b_s[0][:, :DEC_SEQ].T, A_HEAD_DIM, axis=1), ((0, 8 - DEC_SEQ), (0, 0)))
    ck = jnp.transpose(cache_k_win[0], (0, 2, 3, 1))
    cv = jnp.transpose(cache_v_win[0], (0, 2, 3, 1))
    routers = [_router_weights(router_group_w[l], router_group_b[l], router_expert_w[l], router_expert_b[l])
               for l in range(2)]
    nm = [norm_mix[l].reshape(1, D_MODEL) for l in range(2)]
    nf = [norm_ffn[l].reshape(1, D_MODEL) for l in range(2)]

    x1_all, h_all, ri_all, rg_all, tc_all, k_last, v_last, va_last, cnt0, stg_p = _mix0_prompt(
        xp2, nm[0], win, lng, lnb, wsp, bs_full, bias_p, wout, nf[0], *routers[0])
    x1_all, h_all, ri_all, rg_all, tc_all, k_new, v_new, va_s, cnt0, stg_s = _mix0_sample(
        xs_t, nm[0], win, lng, lnb, wcoef, bcoef, ck, cv, bias_sc, bias_sn, wout, nf[0], *routers[0], cnt0,
        x1_all, h_all, ri_all, rg_all, tc_all)
    cplan0, ys0 = _moe(stg_p, stg_s, ri_all, cnt0, tc_all, w_gate, w_up, w_down, 0)
    x2_all = _combine(cplan0, x1_all, rg_all, ys0)

    wp = c_w_pool[0].astype(BF16)
    sc = c_scale[0].reshape(1, D_MODEL)
    x3_all, h2_all, ri2_all, rg2_all, tc2_all, pool_tail, cnt1, stg2_p = _mix1_prompt(
        x2_all, nm[1], wp, sc, nf[1], *routers[1])
    state_t = jnp.transpose(state_pool[0], (1, 0, 2))
    x3_all, h2_all, ri2_all, rg2_all, tc2_all, hs1, cnt1, stg2_s = _mix1_sample(
        x2_all, state_t, nm[1], wp, sc, nf[1], *routers[1], cnt1, x3_all, h2_all, ri2_all, rg2_all, tc2_all)
    cplan1, ys1 = _moe(stg2_p, stg2_s, ri2_all, cnt1, tc2_all, w_gate, w_up, w_down, 1)
    y_p, y_s = _final(cplan1, x3_all, rg2_all, ys1, norm_final.reshape(1, D_MODEL))

    def from_tmajor(a, width):
        return jnp.transpose(a.reshape(DEC_SEQ, DEC_BATCH, width), (1, 0, 2))

    y_prompt = y_p.reshape(BATCH, SEQ, D_MODEL)
    y_sample = from_tmajor(y_s, D_MODEL)
    win_k_p = k_last.reshape(1, BATCH, WINDOW, B_KV_HEADS, B_HEAD_DIM)
    win_v_p = v_last.reshape(1, BATCH, WINDOW, B_KV_HEADS, B_HEAD_DIM)
    win_k_s = jnp.transpose(k_new, (0, 3, 1, 2))[None]
    win_v_s = jnp.transpose(v_new, (0, 3, 1, 2))[None]
    chunk_v_p = va_last.reshape(1, BATCH, CHUNK, A_HEADS, A_HEAD_DIM)
    chunk_v_s = from_tmajor(va_s, A_WIDTH).reshape(1, DEC_BATCH, DEC_SEQ, A_HEADS, A_HEAD_DIM)
    pool_p = pool_tail[:, 1:][None]
    pool_s = jnp.concatenate([state_pool[0][:, DEC_SEQ:], from_tmajor(hs1, D_MODEL)], axis=1)[None]
    return (y_prompt, y_sample, win_k_p, win_v_p, win_k_s, win_v_s, chunk_v_p, chunk_v_s, pool_p, pool_s)
```

```python
import functools
import math

import numpy as np
import jax
import jax.numpy as jnp
from jax import lax
from jax.experimental import pallas as pl
from jax.experimental.pallas import tpu as pltpu

F32 = jnp.float32
BF16 = jnp.bfloat16

D_MODEL = 1024
BATCH = 2
SEQ = 8192
DEC_BATCH = 128
DEC_SEQ = 4
A_WIDTH = 512
A_HEADS = 8
A_HEAD_DIM = 64
CHUNK = 128
B_HEADS = 8
B_KV_HEADS = 2
B_HEAD_DIM = 64
B_GROUP = 4
WINDOW = 128
N_BUCKETS = 32
MAX_DISTANCE = WINDOW
Q_WIDTH = 512
KV_WIDTH = 128
IN_WIDTH = 2 * A_WIDTH + Q_WIDTH + 2 * KV_WIDTH
ATTN_SCALE = B_HEAD_DIM ** -0.5
NEG_INF = -1e30
POOL_SIZES = (2, 4, 8, 16)
POOL_GROUP_DIM = 256
POOL_MAX = 16
N_GROUPS = 4
EXPERTS_PER_GROUP = 8
N_EXPERTS = 32
TOP_K = 2
D_EXPERT = 512
EPS = 1e-6

LANES = 128
ROW_TILE = D_MODEL // LANES
T_PROMPT = BATCH * SEQ
T_SAMPLE = DEC_BATCH * DEC_SEQ
T_ALL = T_PROMPT + T_SAMPLE
TM = 512
N_PROMPT_BLOCKS = T_PROMPT // TM
N_ROW_BLOCKS = T_ALL // TM
STEPS_PER_BATCH = SEQ // TM
SUB = TM // WINDOW
N_SLOTS = T_ALL * TOP_K
MOE_BLK = 512
N_MOE_BLOCKS = N_SLOTS // MOE_BLK + N_EXPERTS
N_SORT_ROWS = N_MOE_BLOCKS * MOE_BLK
SAMPLE_GROUP = 8
N_SAMPLE_GROUPS = DEC_BATCH // SAMPLE_GROUP
VMEM_LIMIT = 56 * 1024 * 1024

STACK_HEADS = ((0, 2, 5, 7), (1, 3, 4, 6))


def _t5_bucket_np(dist):
    n = np.maximum(dist, 0)
    max_exact = N_BUCKETS // 2
    nf = np.maximum(n, 1).astype(np.float32)
    large = max_exact + (np.log(nf / np.float32(max_exact)) / np.float32(math.log(MAX_DISTANCE / max_exact))
                         * np.float32(N_BUCKETS - max_exact)).astype(np.int32)
    large = np.minimum(large, N_BUCKETS - 1)
    return np.where(n < max_exact, n, large).astype(np.int32)


def _bucket_tables():
    qi = np.arange(WINDOW)[:, None]
    ki = np.arange(2 * WINDOW)[None, :]
    dist = qi + WINDOW - ki
    valid = (dist >= 0) & (dist < WINDOW)
    bp = np.where(valid, _t5_bucket_np(dist), -1)
    bp_first = np.where(ki >= WINDOW, bp, -1)
    bkt_p = np.stack([bp_first, bp]).astype(np.int32)

    t = np.repeat(np.arange(DEC_SEQ), SAMPLE_GROUP)[:, None]
    b = np.tile(np.arange(SAMPLE_GROUP), DEC_SEQ)[:, None]
    cb = np.repeat(np.arange(SAMPLE_GROUP), WINDOW)[None, :]
    cj = np.tile(np.arange(WINDOW), SAMPLE_GROUP)[None, :]
    dist_c = t + WINDOW - cj
    valid_c = (cb == b) & (dist_c >= 0) & (dist_c < WINDOW)
    bkt_sc = np.where(valid_c, _t5_bucket_np(dist_c), -1).astype(np.int32)
    nt = np.repeat(np.arange(DEC_SEQ), SAMPLE_GROUP)[None, :]
    nb = np.tile(np.arange(SAMPLE_GROUP), DEC_SEQ)[None, :]
    dist_n = t - nt
    valid_n = (nb == b) & (dist_n >= 0)
    bkt_sn = np.where(valid_n, _t5_bucket_np(dist_n), -1).astype(np.int32)
    bkt_sn = np.concatenate([bkt_sn, np.full((32, LANES - 32), -1, np.int32)], axis=1)
    return bkt_p, bkt_sc, bkt_sn


_BKT_P, _BKT_SC, _BKT_SN = _bucket_tables()


def _cparams(semantics):
    return pltpu.CompilerParams(dimension_semantics=semantics, vmem_limit_bytes=VMEM_LIMIT)


def _rms(x, g):
    return x * lax.rsqrt(jnp.mean(x * x, axis=-1, keepdims=True) + EPS) * g


def _layernorm(x, g, b):
    xc = x - jnp.mean(x, axis=-1, keepdims=True)
    return xc * lax.rsqrt(jnp.mean(xc * xc, axis=-1, keepdims=True) + EPS) * g + b


def _dot(a, b):
    return jnp.dot(a, b, preferred_element_type=F32)


def _dot_nt(a, b):
    return lax.dot_general(a, b, (((1,), (1,)), ((), ())), preferred_element_type=F32)


def _project(x, nm, win, lng, lnb):
    h = _rms(x, nm)
    z = _dot(h.astype(BF16), win)
    u = jax.nn.gelu(z[:, :A_WIDTH])
    va = _layernorm(jax.nn.gelu(z[:, A_WIDTH:2 * A_WIDTH]), lng, lnb)
    q = z[:, 2 * A_WIDTH:2 * A_WIDTH + Q_WIDTH] * ATTN_SCALE
    k = z[:, 2 * A_WIDTH + Q_WIDTH:2 * A_WIDTH + Q_WIDTH + KV_WIDTH]
    v = z[:, 2 * A_WIDTH + Q_WIDTH + KV_WIDTH:]
    return u, va, q, k, v


def _route(x1, nf, wr, br):
    hf = _rms(x1, nf)
    h = hf.astype(BF16)
    h_lo = (hf - h.astype(F32)).astype(BF16)
    part = _dot(h, wr)
    logits = part[:, :LANES] + part[:, LANES:] + _dot(h_lo, wr[:, :LANES]) + br
    rows = logits.shape[0]
    lane = lax.broadcasted_iota(jnp.int32, (rows, LANES), 1)
    lanef = lane.astype(F32)
    big = jnp.float32(1e9)
    is_g = lane < N_GROUPS
    gl = jnp.where(is_g, logits, -jnp.inf)
    gmax = jnp.max(gl, axis=1, keepdims=True)
    gsel = jnp.min(jnp.where(gl == gmax, lanef, big), axis=1, keepdims=True)
    gsum = jnp.sum(jnp.where(is_g, jnp.exp(logits - gmax), 0.0), axis=1, keepdims=True)
    g1 = 1.0 / gsum
    lo = N_GROUPS + EXPERTS_PER_GROUP * gsel
    emask = (lanef >= lo) & (lanef < lo + EXPERTS_PER_GROUP)
    el = jnp.where(emask, logits, -jnp.inf)
    v1 = jnp.max(el, axis=1, keepdims=True)
    i1 = jnp.min(jnp.where(el == v1, lanef, big), axis=1, keepdims=True)
    el2 = jnp.where(lanef == i1, -jnp.inf, el)
    v2 = jnp.max(el2, axis=1, keepdims=True)
    i2 = jnp.min(jnp.where(el2 == v2, lanef, big), axis=1, keepdims=True)
    e2 = jnp.exp(v2 - v1)
    den = 1.0 + e2
    w1 = g1 / den
    w2 = g1 * e2 / den
    ids = jnp.where(lane == 0, i1 - N_GROUPS, jnp.where(lane == 1, i2 - N_GROUPS, 0.0)).astype(jnp.int32)
    gates = jnp.where(lane == 0, w1, jnp.where(lane == 1, w2, 0.0))
    return h, ids, gates


def _rank_pack(ids, cnt_ref, tcnt_ref):
    rows = ids.shape[0]
    lane = lax.broadcasted_iota(jnp.int32, (rows, LANES), 1)
    o0 = (lane == ids[:, 0:1]).astype(F32)
    o1 = (lane == ids[:, 1:2]).astype(F32)
    r = lax.broadcasted_iota(jnp.int32, (rows, rows), 0)
    c = lax.broadcasted_iota(jnp.int32, (rows, rows), 1)
    before = (c < r).astype(BF16)
    p01 = _dot(before, jnp.concatenate([o0, o1], axis=1).astype(BF16))
    p0 = p01[:, :LANES]
    p1 = p01[:, LANES:]
    c0 = jnp.sum(o0, axis=0, keepdims=True)
    c1 = jnp.sum(o1, axis=0, keepdims=True)
    ctile = c0 + c1
    cnt_ref[...] = cnt_ref[...] + ctile
    tcnt_ref[...] = ctile
    inc = jnp.broadcast_to(ctile, (8, LANES))
    lane8 = lax.broadcasted_iota(jnp.int32, (8, LANES), 1)
    for sh in (1, 2, 4, 8, 16, 32, 64):
        inc = inc + jnp.where(lane8 >= sh, pltpu.roll(inc, sh, 1), 0.0)
    start = inc[0:1] - ctile
    lpos0 = jnp.sum(o0 * (start + p0), axis=1, keepdims=True)
    lpos1 = jnp.sum(o1 * (start + c0 + p1), axis=1, keepdims=True)
    idf = ids.astype(F32)
    packed = jnp.where(lane < TOP_K, idf, 0.0)
    for ln, col in ((4, lpos0), (5, lpos1)):
        packed = jnp.where(lane == ln, col, packed)
    return jnp.transpose(packed)[:8].astype(jnp.int32)


def _prep_kernel(tab_ref, sink_ref, bp_ref, bsc_ref, bsn_ref, ws_ref, op_ref, osc_ref, osn_ref, ows_ref):
    def fill(bkt, write, sink_col0):
        col0 = lax.broadcasted_iota(jnp.int32, bkt.shape, 1) == 0
        for st, heads in enumerate(STACK_HEADS):
            for slot, h in enumerate(heads):
                acc = jnp.full(bkt.shape, NEG_INF, F32)
                for b in range(N_BUCKETS):
                    acc = jnp.where(bkt == b, tab_ref[b, h], acc)
                if sink_col0:
                    acc = jnp.where(col0, sink_ref[0, h], acc)
                write(st, slot, acc)

    for var in range(2):
        def wr_p(st, slot, acc, var=var):
            op_ref[var, st, slot * WINDOW:(slot + 1) * WINDOW, :] = acc
        fill(bp_ref[var], wr_p, True)

    rows_s = DEC_SEQ * SAMPLE_GROUP

    def wr_sc(st, slot, acc):
        osc_ref[st, slot * rows_s:(slot + 1) * rows_s, :] = acc
    fill(bsc_ref[...], wr_sc, True)

    def wr_sn(st, slot, acc):
        osn_ref[st, slot * rows_s:(slot + 1) * rows_s, :] = acc
    fill(bsn_ref[...], wr_sn, False)

    r = lax.broadcasted_iota(jnp.int32, (CHUNK, CHUNK), 0)
    c = lax.broadcasted_iota(jnp.int32, (CHUNK, CHUNK), 1)
    for h in range(A_HEADS):
        ows_ref[h // 2, :, (h % 2) * CHUNK:(h % 2 + 1) * CHUNK] = jnp.where(r >= c, ws_ref[h], 0.0).astype(BF16)


def _prep(rel_bias_table, sinks, w_s):
    vm = pl.BlockSpec(memory_space=pltpu.VMEM)
    sm = pl.BlockSpec(memory_space=pltpu.SMEM)
    rows_s = DEC_SEQ * SAMPLE_GROUP
    return pl.pallas_call(
        _prep_kernel,
        in_specs=[sm, sm, vm, vm, vm, vm],
        out_specs=[vm, vm, vm, vm],
        out_shape=[
            jax.ShapeDtypeStruct((2, 2, 4 * WINDOW, 2 * WINDOW), F32),
            jax.ShapeDtypeStruct((2, 4 * rows_s, SAMPLE_GROUP * WINDOW), F32),
            jax.ShapeDtypeStruct((2, 4 * rows_s, LANES), F32),
            jax.ShapeDtypeStruct((A_HEADS // 2, CHUNK, 2 * CHUNK), BF16),
        ],
        name="prep_tables",
    )(rel_bias_table, sinks.reshape(1, B_HEADS), jnp.asarray(_BKT_P), jnp.asarray(_BKT_SC), jnp.asarray(_BKT_SN), w_s)


def _gate_pairs(va_rows, wsp_ref, lane_lo):
    outs = []
    for p in range(A_HEADS // 2):
        vp = va_rows[:, p * LANES:(p + 1) * LANES]
        rhs = jnp.concatenate([jnp.where(lane_lo, vp, 0.0), jnp.where(lane_lo, 0.0, vp)], axis=0).astype(BF16)
        outs.append(_dot(wsp_ref[p], rhs))
    return jnp.concatenate(outs, axis=1)


def _prompt_steps(body, first_row_out):
    def kern(*refs):
        i = pl.program_id(0)

        @pl.when(i < N_PROMPT_BLOCKS)
        def _():
            body(*refs)

        @pl.when(i >= N_PROMPT_BLOCKS)
        def _():
            for r in refs[first_row_out:first_row_out + 5]:
                r[...] = jnp.zeros(r.shape, r.dtype)

    return kern


def _mix0_prompt_kernel(x_ref, nm_ref, win_ref, lng_ref, lnb_ref, wsp_ref, bs_ref, bias_ref,
                        wout_ref, nf_ref, wr_ref, br_ref,
                        x1_ref, h_ref, ri_ref, rg_ref, tc_ref, kl_ref, vl_ref, val_ref, cnt_ref,
                        kprev, vprev, mix_scr):
    @pl.when(pl.program_id(0) == 0)
    def _():
        cnt_ref[...] = jnp.zeros_like(cnt_ref)

    x = x_ref[...]
    u, va, q, k, v = _project(x, nm_ref[...], win_ref[...], lng_ref[...], lnb_ref[...])
    lane_lo = lax.broadcasted_iota(jnp.int32, (WINDOW, LANES), 1) < B_HEAD_DIM
    row0 = lax.broadcasted_iota(jnp.int32, (WINDOW, KV_WIDTH), 0) == 0
    first = pl.program_id(0) % STEPS_PER_BATCH == 0

    @pl.when(first)
    def _():
        kprev[...] = jnp.zeros_like(kprev)
        vprev[...] = jnp.zeros_like(vprev)

    for j in range(SUB):
        rows = slice(j * WINDOW, (j + 1) * WINDOW)
        s_gate = _gate_pairs(va[rows], wsp_ref, lane_lo)
        mix_scr[rows, :A_WIDTH] = u[rows] * (s_gate + bs_ref[...])

        if j == 0:
            kp, vp = kprev[...], vprev[...]
        else:
            prows = slice((j - 1) * WINDOW, j * WINDOW)
            kp, vp = k[prows], v[prows]
        kk = jnp.concatenate([jnp.where(row0, 0.0, kp), k[rows]], axis=0)
        vv = jnp.concatenate([jnp.where(row0, 0.0, vp), v[rows]], axis=0)
        kops = (kk.astype(BF16), pltpu.roll(kk, B_HEAD_DIM, 1).astype(BF16))
        vops = (vv.astype(BF16), pltpu.roll(vv, B_HEAD_DIM, 1).astype(BF16))
        qt = [q[rows, p * LANES:(p + 1) * LANES] for p in range(4)]
        q_even = [jnp.where(lane_lo, t, 0.0) for t in qt]
        q_odd = [jnp.where(lane_lo, 0.0, t) for t in qt]
        stacks = (jnp.concatenate([q_even[0], q_even[1], q_odd[2], q_odd[3]], axis=0),
                  jnp.concatenate([q_odd[0], q_odd[1], q_even[2], q_even[3]], axis=0))
        o = []
        for st in range(2):
            s = _dot_nt(stacks[st].astype(BF16), kops[st])
            if j == 0:
                bias = bias_ref[jnp.where(first, 0, 1), st]
            else:
                bias = bias_ref[1, st]
            s = s + bias
            m = jnp.max(s, axis=-1, keepdims=True)
            p = jnp.exp(s - m)
            den = jnp.sum(p, axis=-1, keepdims=True)
            o.append(_dot(p.astype(BF16), vops[st]) / den)
        oa, ob = o
        sl = [slice(i * WINDOW, (i + 1) * WINDOW) for i in range(4)]
        tiles = (jnp.where(lane_lo, oa[sl[0]], ob[sl[0]]), jnp.where(lane_lo, oa[sl[1]], ob[sl[1]]),
                 jnp.where(lane_lo, ob[sl[2]], oa[sl[2]]), jnp.where(lane_lo, ob[sl[3]], oa[sl[3]]))
        for p in range(4):
            mix_scr[rows, A_WIDTH + p * LANES:A_WIDTH + (p + 1) * LANES] = tiles[p]

    last = slice(TM - WINDOW, TM)
    kprev[...] = k[last]
    vprev[...] = v[last]
    kl_ref[...] = k[last]
    vl_ref[...] = v[last]
    val_ref[...] = va[last]

    x1 = x + _dot(mix_scr[...].astype(BF16), wout_ref[...])
    x1_ref[...] = x1
    h, ids, gates = _route(x1, nf_ref[...], wr_ref[...], br_ref[...])
    h_ref[...] = h.reshape(h_ref.shape)
    ri_ref[...] = _rank_pack(ids, cnt_ref, tc_ref)
    rg_ref[...] = gates


def _const_spec(shape):
    nd = len(shape)
    return pl.BlockSpec(shape, lambda i, _n=nd: (0,) * _n)


def _mix0_prompt(x_all, nm, win, lng, lnb, wsp, bs_full, bias_p, wout, nf, wr, br):
    row_spec = pl.BlockSpec((TM, D_MODEL), lambda i: (i, 0))
    row3_spec = pl.BlockSpec((TM, ROW_TILE, LANES), lambda i: (i, 0, 0))
    lane_spec = pl.BlockSpec((TM, LANES), lambda i: (i, 0))
    last_kv = pl.BlockSpec((None, WINDOW, KV_WIDTH), lambda i: (jnp.minimum(i // STEPS_PER_BATCH, BATCH - 1), 0, 0))
    last_va = pl.BlockSpec((None, WINDOW, A_WIDTH), lambda i: (jnp.minimum(i // STEPS_PER_BATCH, BATCH - 1), 0, 0))
    return pl.pallas_call(
        _prompt_steps(_mix0_prompt_kernel, 12),
        grid=(N_ROW_BLOCKS,),
        in_specs=[pl.BlockSpec((TM, D_MODEL), lambda i: (jnp.minimum(i, N_PROMPT_BLOCKS - 1), 0)),
                  _const_spec((1, D_MODEL)), _const_spec((D_MODEL, IN_WIDTH)),
                  _const_spec((1, A_WIDTH)), _const_spec((1, A_WIDTH)),
                  _const_spec((A_HEADS // 2, CHUNK, 2 * CHUNK)), _const_spec((CHUNK, A_WIDTH)),
                  _const_spec((2, 2, 4 * WINDOW, 2 * WINDOW)),
                  _const_spec((A_WIDTH + Q_WIDTH, D_MODEL)), _const_spec((1, D_MODEL)),
                  _const_spec((D_MODEL, 2 * LANES)), _const_spec((1, LANES))],
        out_specs=[row_spec, row3_spec, pl.BlockSpec((8, TM), lambda i: (0, i)), lane_spec,
                   pl.BlockSpec((None, 1, LANES), lambda i: (i, 0, 0)),
                   last_kv, last_kv, last_va, _const_spec((1, LANES))],
        out_shape=[jax.ShapeDtypeStruct((T_ALL, D_MODEL), F32), jax.ShapeDtypeStruct((T_ALL, ROW_TILE, LANES), BF16),
                   jax.ShapeDtypeStruct((8, T_ALL), jnp.int32), jax.ShapeDtypeStruct((T_ALL, LANES), F32),
                   jax.ShapeDtypeStruct((N_ROW_BLOCKS, 1, LANES), F32),
                   jax.ShapeDtypeStruct((BATCH, WINDOW, KV_WIDTH), F32),
                   jax.ShapeDtypeStruct((BATCH, WINDOW, KV_WIDTH), F32),
                   jax.ShapeDtypeStruct((BATCH, WINDOW, A_WIDTH), F32),
                   jax.ShapeDtypeStruct((1, LANES), F32)],
        scratch_shapes=[pltpu.VMEM((WINDOW, KV_WIDTH), F32), pltpu.VMEM((WINDOW, KV_WIDTH), F32),
                        pltpu.VMEM((TM, D_MODEL), F32)],
        compiler_params=_cparams(("arbitrary",)),
        name="mix0_prompt",
    )(x_all, nm, win, lng, lnb, wsp, bs_full, bias_p, wout, nf, wr, br)


def _mix0_sample_kernel(x_ref, nm_ref, win_ref, lng_ref, lnb_ref, wcoef_ref, bcoef_ref,
                        ck_ref, cv_ref, bsc_ref, bsn_ref,
                        wout_ref, nf_ref, wr_ref, br_ref, cnt_in,
                        x1_in, h_in, ri_in, rg_in, tc_in,
                        x1_ref, h_ref, ri_ref, rg_ref, tc_ref, kn_ref, vn_ref, va_ref, cnt_ref,
                        q_scr, k_scr, v_scr, mix_scr):
    del x1_in, h_in, ri_in, rg_in, tc_in
    g = pl.program_id(0)

    @pl.when(g == 0)
    def _():
        u, va, q, k, v = _project(x_ref[...], nm_ref[...], win_ref[...], lng_ref[...], lnb_ref[...])
        q_scr[...] = q
        k_scr[...] = k
        v_scr[...] = v
        va_ref[...] = va
        for t in range(DEC_SEQ):
            acc = jnp.zeros((DEC_BATCH, A_WIDTH), F32) + bcoef_ref[t:t + 1, :]
            for s in range(t + 1):
                row = t * DEC_SEQ + s
                acc = acc + wcoef_ref[row:row + 1, :] * va[s * DEC_BATCH:(s + 1) * DEC_BATCH]
            mix_scr[t * DEC_BATCH:(t + 1) * DEC_BATCH, :A_WIDTH] = u[t * DEC_BATCH:(t + 1) * DEC_BATCH] * acc

    b0 = pl.multiple_of(g * SAMPLE_GROUP, SAMPLE_GROUP)
    lane_lo = lax.broadcasted_iota(jnp.int32, (DEC_SEQ * SAMPLE_GROUP, LANES), 1) < B_HEAD_DIM

    def grab(ref, width):
        return jnp.concatenate([ref[pl.ds(t * DEC_BATCH + b0, SAMPLE_GROUP), :] for t in range(DEC_SEQ)], axis=0)

    qg = grab(q_scr, Q_WIDTH)
    kn = grab(k_scr, KV_WIDTH)
    vn = grab(v_scr, KV_WIDTH)

    lane_w = lax.broadcasted_iota(jnp.int32, (KV_WIDTH, WINDOW), 1)
    n_new = DEC_SEQ * SAMPLE_GROUP

    def new_window(c_ref, new_rows, w_ref):
        nt = jnp.transpose(jnp.concatenate([new_rows, jnp.zeros((WINDOW - n_new, KV_WIDTH), F32)], axis=0))
        for b in range(SAMPLE_GROUP):
            w = pltpu.roll(c_ref[b].reshape(KV_WIDTH, WINDOW), WINDOW - DEC_SEQ, 1)
            for t in range(DEC_SEQ):
                src = t * SAMPLE_GROUP + b
                dst = WINDOW - DEC_SEQ + t
                w = jnp.where(lane_w == dst, pltpu.roll(nt, (dst - src) % WINDOW, 1), w)
            w_ref[b] = w.reshape(B_KV_HEADS, B_HEAD_DIM, WINDOW)

    new_window(ck_ref, kn, kn_ref)
    new_window(cv_ref, vn, vn_ref)
    ccol0 = lax.broadcasted_iota(jnp.int32, (KV_WIDTH, SAMPLE_GROUP * WINDOW), 1) == 0

    def cache_t(ref):
        t = jnp.concatenate([ref[b].reshape(KV_WIDTH, WINDOW) for b in range(SAMPLE_GROUP)], axis=1)
        return jnp.where(ccol0, 0.0, t)

    def head_swap(t):
        return jnp.concatenate([t[B_HEAD_DIM:], t[:B_HEAD_DIM]], axis=0)

    kct = cache_t(ck_ref)
    vct = cache_t(cv_ref)
    kc_ops = (kct.astype(BF16), head_swap(kct).astype(BF16))
    vc_ops = (vct.astype(BF16), head_swap(vct).astype(BF16))
    kn_ops = (kn.astype(BF16), pltpu.roll(kn, B_HEAD_DIM, 1).astype(BF16))
    vn_ops = (vn.astype(BF16), pltpu.roll(vn, B_HEAD_DIM, 1).astype(BF16))
    qt = [qg[:, p * LANES:(p + 1) * LANES] for p in range(4)]
    q_even = [jnp.where(lane_lo, t, 0.0) for t in qt]
    q_odd = [jnp.where(lane_lo, 0.0, t) for t in qt]
    stacks = (jnp.concatenate([q_even[0], q_even[1], q_odd[2], q_odd[3]], axis=0),
              jnp.concatenate([q_odd[0], q_odd[1], q_even[2], q_even[3]], axis=0))
    o = []
    for st in range(2):
        qs = stacks[st].astype(BF16)
        sc = _dot(qs, kc_ops[st]) + bsc_ref[st]
        sn = _dot_nt(qs, kn_ops[st]) + bsn_ref[st][:, :DEC_SEQ * SAMPLE_GROUP]
        m = jnp.maximum(jnp.max(sc, axis=-1, keepdims=True), jnp.max(sn, axis=-1, keepdims=True))
        pc = jnp.exp(sc - m)
        pn = jnp.exp(sn - m)
        den = jnp.sum(pc, axis=-1, keepdims=True) + jnp.sum(pn, axis=-1, keepdims=True)
        o.append((_dot_nt(pc.astype(BF16), vc_ops[st]) + _dot(pn.astype(BF16), vn_ops[st])) / den)
    oa, ob = o
    n = DEC_SEQ * SAMPLE_GROUP
    sl = [slice(i * n, (i + 1) * n) for i in range(4)]
    tiles = (jnp.where(lane_lo, oa[sl[0]], ob[sl[0]]), jnp.where(lane_lo, oa[sl[1]], ob[sl[1]]),
             jnp.where(lane_lo, ob[sl[2]], oa[sl[2]]), jnp.where(lane_lo, ob[sl[3]], oa[sl[3]]))
    for p in range(4):
        for t in range(DEC_SEQ):
            mix_scr[pl.ds(t * DEC_BATCH + b0, SAMPLE_GROUP), A_WIDTH + p * LANES:A_WIDTH + (p + 1) * LANES] = (
                tiles[p][t * SAMPLE_GROUP:(t + 1) * SAMPLE_GROUP])

    @pl.when(g == N_SAMPLE_GROUPS - 1)
    def _():
        x1 = x_ref[...] + _dot(mix_scr[...].astype(BF16), wout_ref[...])
        x1_ref[...] = x1
        h, ids, gates = _route(x1, nf_ref[...], wr_ref[...], br_ref[...])
        h_ref[...] = h.reshape(h_ref.shape)
        cnt_ref[...] = cnt_in[...]
        ri_ref[...] = _rank_pack(ids, cnt_ref, tc_ref)
        rg_ref[...] = gates


def _mix0_sample(x_all, nm, win, lng, lnb, wcoef, bcoef, ck, cv, bias_sc, bias_sn, wout, nf, wr, br, cnt,
                 x1_all, h_all, ri_all, rg_all, tc_all):
    sample_rows = pl.BlockSpec((TM, D_MODEL), lambda g: (N_PROMPT_BLOCKS, 0))
    sample_rows3 = pl.BlockSpec((TM, ROW_TILE, LANES), lambda g: (N_PROMPT_BLOCKS, 0, 0))
    sample_lanes = pl.BlockSpec((TM, LANES), lambda g: (N_PROMPT_BLOCKS, 0))
    cache_spec = pl.BlockSpec((SAMPLE_GROUP, B_KV_HEADS, B_HEAD_DIM, WINDOW), lambda g: (g, 0, 0, 0))
    anyspec = pl.BlockSpec(memory_space=pl.ANY)
    n_in = 16
    return pl.pallas_call(
        _mix0_sample_kernel,
        grid=(N_SAMPLE_GROUPS,),
        in_specs=[_const_spec((TM, D_MODEL)), _const_spec((1, D_MODEL)), _const_spec((D_MODEL, IN_WIDTH)),
                  _const_spec((1, A_WIDTH)), _const_spec((1, A_WIDTH)),
                  _const_spec((16, A_WIDTH)), _const_spec((8, A_WIDTH)),
                  cache_spec, cache_spec,
                  _const_spec((2, 4 * 32, SAMPLE_GROUP * WINDOW)), _const_spec((2, 4 * 32, LANES)),
                  _const_spec((A_WIDTH + Q_WIDTH, D_MODEL)), _const_spec((1, D_MODEL)),
                  _const_spec((D_MODEL, 2 * LANES)), _const_spec((1, LANES)), _const_spec((1, LANES)),
                  anyspec, anyspec, anyspec, anyspec, anyspec],
        out_specs=[sample_rows, sample_rows3, pl.BlockSpec((8, TM), lambda g: (0, N_PROMPT_BLOCKS)), sample_lanes,
                   pl.BlockSpec((None, 1, LANES), lambda g: (N_PROMPT_BLOCKS, 0, 0)),
                   cache_spec, cache_spec,
                   _const_spec((T_SAMPLE, A_WIDTH)), _const_spec((1, LANES))],
        out_shape=[jax.ShapeDtypeStruct((T_ALL, D_MODEL), F32), jax.ShapeDtypeStruct((T_ALL, ROW_TILE, LANES), BF16),
                   jax.ShapeDtypeStruct((8, T_ALL), jnp.int32), jax.ShapeDtypeStruct((T_ALL, LANES), F32),
                   jax.ShapeDtypeStruct((N_ROW_BLOCKS, 1, LANES), F32),
                   jax.ShapeDtypeStruct((DEC_BATCH, B_KV_HEADS, B_HEAD_DIM, WINDOW), F32),
                   jax.ShapeDtypeStruct((DEC_BATCH, B_KV_HEADS, B_HEAD_DIM, WINDOW), F32),
                   jax.ShapeDtypeStruct((T_SAMPLE, A_WIDTH), F32), jax.ShapeDtypeStruct((1, LANES), F32)],
        scratch_shapes=[pltpu.VMEM((T_SAMPLE, Q_WIDTH), F32), pltpu.VMEM((T_SAMPLE, KV_WIDTH), F32),
                        pltpu.VMEM((T_SAMPLE, KV_WIDTH), F32), pltpu.VMEM((T_SAMPLE, D_MODEL), F32)],
        input_output_aliases={n_in: 0, n_in + 1: 1, n_in + 2: 2, n_in + 3: 3, n_in + 4: 4},
        compiler_params=_cparams(("arbitrary",)),
        name="mix0_sample",
    )(x_all, nm, win, lng, lnb, wcoef, bcoef, ck, cv, bias_sc, bias_sn, wout, nf, wr, br, cnt,
      x1_all, h_all, ri_all, rg_all, tc_all)


def _moe_metadata(rt_all, cnt, tcnt):
    counts = cnt[0, :N_EXPERTS].astype(jnp.int32)
    padded = (counts + MOE_BLK - 1) // MOE_BLK * MOE_BLK
    pad_end = jnp.cumsum(padded)
    pad_start = pad_end - padded
    experts = jnp.arange(N_EXPERTS, dtype=jnp.int32)
    n_valid = (pad_end[-1] // MOE_BLK).astype(jnp.int32).reshape(1)
    blk_start = jnp.arange(N_MOE_BLOCKS, dtype=jnp.int32) * MOE_BLK
    block_e = jnp.minimum(jnp.sum((blk_start[:, None] >= pad_end[None, :]).astype(jnp.int32), axis=1),
                          N_EXPERTS - 1).astype(jnp.int32)
    zero_start = (pad_start + counts).astype(jnp.int32)
    zero_len = (padded - counts).astype(jnp.int32)
    first = (blk_start == pad_start[block_e]).astype(jnp.int32)
    used = counts > 0
    parity = ((jnp.cumsum(used.astype(jnp.int32)) - 1) % 2)[block_e].astype(jnp.int32)
    nearest = lax.cummin(jnp.where(used, experts, N_EXPERTS)[::-1])[::-1]
    next_used = jnp.concatenate([nearest[1:], jnp.full((1,), N_EXPERTS, jnp.int32)])
    nxt = jnp.where(next_used < N_EXPERTS, next_used, -1)[block_e].astype(jnp.int32)
    plan = (block_e, first, parity, nxt, n_valid)
    runs = tcnt[:, 0, :N_EXPERTS].astype(jnp.int32)
    run_dst = pad_start[None, :] + jnp.cumsum(runs, axis=0) - runs
    lpos = rt_all[2 * TOP_K:3 * TOP_K].reshape(N_SLOTS).astype(jnp.int32)
    cplan = (lpos, runs.reshape(-1), run_dst.reshape(-1).astype(jnp.int32))
    gruns = runs.reshape(N_DISPATCH_STEPS, DISPATCH_TILES, N_EXPERTS)
    gtot = jnp.sum(gruns, axis=1)
    gstart = jnp.cumsum(gtot, axis=1) - gtot
    shift = ((gstart[:, None, :] + jnp.cumsum(gruns, axis=1) - gruns).reshape(N_ROW_BLOCKS, N_EXPERTS)
             - (jnp.cumsum(runs, axis=1) - runs))
    shift_rows = jnp.repeat(jnp.transpose(shift), TM, axis=1)
    hit = rt_all[:TOP_K, None, :] == experts[None, :, None]
    gpos = (rt_all[2 * TOP_K:3 * TOP_K] + jnp.sum(jnp.where(hit, shift_rows[None], 0), axis=1)).reshape(N_SLOTS)
    grun_dst = pad_start[None, :] + jnp.cumsum(gtot, axis=0) - gtot
    dplan = (gpos.astype(jnp.int32), gtot.reshape(-1), grun_dst.reshape(-1).astype(jnp.int32),
             jnp.concatenate([zero_start, zero_len, n_valid]))
    return plan, dplan, cplan


RUN_PIECE = 32
DISPATCH_TILES = 3
DISPATCH_ROWS = DISPATCH_TILES * TM
N_DISPATCH_STEPS = N_ROW_BLOCKS // DISPATCH_TILES


def _for_run_pieces(n, start_piece):
    whole = n // RUN_PIECE

    def body(j, carry):
        start_piece(j * RUN_PIECE, RUN_PIECE)
        return carry

    lax.fori_loop(0, whole, body, 0)
    o = whole * RUN_PIECE
    bit = RUN_PIECE // 2
    while bit >= 1:
        take = (n & bit) != 0

        @pl.when(take)
        def _(o=o, bit=bit):
            start_piece(o, bit)

        o = o + jnp.where(take, bit, 0)
        bit //= 2


def _dispatch_kernel(lpos_ref, run_ref, rdst_ref, zs_ref, h_ref, xs_ref, zero_scr, stage, sem, zsem):
    i = pl.program_id(0)

    @pl.when(i == 0)
    def _():
        zero_scr[...] = jnp.zeros_like(zero_scr)

        def pieces(e, do):
            off = zs_ref[e]
            rem = zs_ref[N_EXPERTS + e]
            bit = MOE_BLK // 2
            while bit >= 1:
                take = (rem & bit) != 0

                @pl.when(take)
                def _(off=off, bit=bit):
                    do(pltpu.make_async_copy(zero_scr.at[pl.ds(0, bit)], xs_ref.at[pl.ds(off, bit)], zsem))

                off = off + jnp.where(take, bit, 0)
                bit //= 2

        def start_e(e, c):
            pieces(e, lambda cp: cp.start())
            return c

        def wait_e(e, c):
            pieces(e, lambda cp: cp.wait())
            return c

        def tail(do):
            def step(b, c):
                do(pltpu.make_async_copy(zero_scr, xs_ref.at[pl.ds(b * MOE_BLK, MOE_BLK)], zsem))
                return c
            return step

        n_valid = zs_ref[2 * N_EXPERTS]
        lax.fori_loop(0, N_EXPERTS, start_e, 0)
        lax.fori_loop(n_valid, N_MOE_BLOCKS, tail(lambda cp: cp.start()), 0)
        lax.fori_loop(0, N_EXPERTS, wait_e, 0)
        lax.fori_loop(n_valid, N_MOE_BLOCKS, tail(lambda cp: cp.wait()), 0)

    base = i * DISPATCH_ROWS

    def place(r, carry):
        row = h_ref[r]
        for kk in range(TOP_K):
            stage[lpos_ref[kk * T_ALL + base + r]] = row
        return carry

    lax.fori_loop(0, DISPATCH_ROWS, place, 0, unroll=8)

    def send_run(e, off):
        n = run_ref[i * N_EXPERTS + e]
        dst = rdst_ref[i * N_EXPERTS + e]
        _for_run_pieces(n, lambda o, size: pltpu.make_async_copy(
            stage.at[pl.ds(off + o, size)], xs_ref.at[pl.ds(dst + o, size)], sem).start(
                priority=size.bit_length() % 2))
        return off + n

    lax.fori_loop(0, N_EXPERTS, send_run, 0)
    pltpu.make_async_copy(stage, xs_ref.at[pl.ds(0, DISPATCH_ROWS * TOP_K)], sem).wait()


def _dispatch(dplan, h_all):
    return pl.pallas_call(
        _dispatch_kernel,
        grid_spec=pltpu.PrefetchScalarGridSpec(
            num_scalar_prefetch=4,
            grid=(N_DISPATCH_STEPS,),
            in_specs=[pl.BlockSpec((DISPATCH_ROWS, ROW_TILE, LANES), lambda i, lp, rn, rd, z: (i, 0, 0))],
            out_specs=pl.BlockSpec(memory_space=pl.ANY),
            scratch_shapes=[pltpu.VMEM((MOE_BLK, ROW_TILE, LANES), BF16),
                            pltpu.VMEM((DISPATCH_ROWS * TOP_K, ROW_TILE, LANES), BF16),
                            pltpu.SemaphoreType.DMA(()), pltpu.SemaphoreType.DMA(())],
        ),
        out_shape=jax.ShapeDtypeStruct((N_SORT_ROWS, ROW_TILE, LANES), BF16),
        compiler_params=_cparams(("arbitrary",)),
        name="moe_dispatch",
    )(*dplan, h_all)


def _experts_kernel(layer, be_ref, first_ref, par_ref, nxt_ref, nv_ref,
                    x_ref, wg_hbm, wu_hbm, wd_hbm, y_ref,
                    wg_s, wu_s, wd_s, wg_f, wu_f, wd_f, wsem):
    i = pl.program_id(0)

    def fetch(e, slot):
        return (pltpu.make_async_copy(wg_hbm.at[layer, e], wg_f.at[slot], wsem.at[slot]),
                pltpu.make_async_copy(wu_hbm.at[layer, e], wu_f.at[slot], wsem.at[slot]),
                pltpu.make_async_copy(wd_hbm.at[layer, e], wd_f.at[slot], wsem.at[slot]))

    @pl.when(i < nv_ref[0])
    def _():
        e = be_ref[i]
        slot = par_ref[i]

        @pl.when(i == 0)
        def _():
            for cp in fetch(e, slot):
                cp.start()

        @pl.when(first_ref[i] == 1)
        def _():
            for cp in fetch(e, slot):
                cp.wait()
            wg_s[...] = wg_f[slot].astype(BF16)
            wu_s[...] = wu_f[slot].astype(BF16)
            wd_s[...] = wd_f[slot].astype(BF16)
            nxt = nxt_ref[i]

            @pl.when(nxt >= 0)
            def _():
                for cp in fetch(nxt, 1 - slot):
                    cp.start()

        xb = x_ref[...].reshape(MOE_BLK, D_MODEL)
        a = jax.nn.silu(_dot(xb, wg_s[...])) * _dot(xb, wu_s[...])
        y_ref[...] = _dot(a.astype(BF16), wd_s[...]).reshape(y_ref.shape)

    @pl.when(i >= nv_ref[0])
    def _():
        y_ref[...] = jnp.zeros(y_ref.shape, y_ref.dtype)


def _experts(block_e, first, parity, nxt, n_valid, xs, w_gate, w_up, w_down, layer):
    def blk(i, be, fi, pa, nx, nv):
        return (jnp.maximum(jnp.minimum(i, nv[0] - 1), 0), 0, 0)

    anyspec = pl.BlockSpec(memory_space=pl.ANY)
    return pl.pallas_call(
        functools.partial(_experts_kernel, layer),
        grid_spec=pltpu.PrefetchScalarGridSpec(
            num_scalar_prefetch=5,
            grid=(N_MOE_BLOCKS,),
            in_specs=[pl.BlockSpec((MOE_BLK, ROW_TILE, LANES), blk), anyspec, anyspec, anyspec],
            out_specs=pl.BlockSpec((MOE_BLK, ROW_TILE, LANES), lambda i, be, fi, pa, nx, nv: (i, 0, 0)),
            scratch_shapes=[pltpu.VMEM((D_MODEL, D_EXPERT), BF16), pltpu.VMEM((D_MODEL, D_EXPERT), BF16),
                            pltpu.VMEM((D_EXPERT, D_MODEL), BF16),
                            pltpu.VMEM((2, D_MODEL, D_EXPERT), F32), pltpu.VMEM((2, D_MODEL, D_EXPERT), F32),
                            pltpu.VMEM((2, D_EXPERT, D_MODEL), F32), pltpu.SemaphoreType.DMA((2,))],
        ),
        out_shape=jax.ShapeDtypeStruct((N_SORT_ROWS, ROW_TILE, LANES), F32),
        compiler_params=_cparams(("arbitrary",)),
        name="moe_experts",
    )(block_e, first, parity, nxt, n_valid, xs, w_gate, w_up, w_down)


def _gather_rows(lpos_ref, run_ref, rdst_ref, ys_ref, ystage, ybufs, sem, i):
    def fetch(tile, buf):
        def fetch_run(e, off):
            n = run_ref[tile * N_EXPERTS + e]
            src = rdst_ref[tile * N_EXPERTS + e]
            _for_run_pieces(n, lambda o, size: pltpu.make_async_copy(
                ys_ref.at[pl.ds(src + o, size)], ystage.at[buf, pl.ds(off + o, size)], sem.at[buf]).start(
                    priority=size.bit_length() % 2))
            return off + n

        lax.fori_loop(0, N_EXPERTS, fetch_run, 0)

    def wait(buf):
        pltpu.make_async_copy(ys_ref.at[pl.ds(0, TM * TOP_K)], ystage.at[buf], sem.at[buf]).wait()

    cur = i % 2

    @pl.when(i == 0)
    def _():
        fetch(i, 0)
        wait(0)

        def unplace(r, carry):
            for kk in range(TOP_K):
                ybufs[0][kk, r] = ystage[0, lpos_ref[kk * T_ALL + r]]
            return carry

        lax.fori_loop(0, TM, unplace, 0, unroll=8)
        fetch(i + 1, 1)

    @pl.when(i + 1 < N_ROW_BLOCKS)
    def _():
        wait(1 - cur)

    @pl.when(i + 2 < N_ROW_BLOCKS)
    def _():
        fetch(i + 2, cur)

    def pieces(compute, store):
        nxt = jnp.minimum(i + 1, N_ROW_BLOCKS - 1)

        def variant(par):
            ycur, ynext = ybufs[par], ybufs[1 - par]

            def piece(j, carry):
                rows = pl.ds(pl.multiple_of(j * COMBINE_ROWS, COMBINE_ROWS), COMBINE_ROWS)
                out = compute(rows, ycur[0, rows].reshape(COMBINE_ROWS, D_MODEL),
                              ycur[1, rows].reshape(COMBINE_ROWS, D_MODEL))
                base = nxt * TM + j * COMBINE_ROWS
                for r in range(COMBINE_ROWS):
                    for kk in range(TOP_K):
                        ynext[kk, j * COMBINE_ROWS + r] = ystage[1 - par, lpos_ref[kk * T_ALL + base + r]]
                store(rows, out)
                return carry

            lax.fori_loop(0, TM // COMBINE_ROWS, piece, 0)

        for par in range(2):
            @pl.when(cur == par)
            def _(par=par):
                variant(par)

    return pieces


COMBINE_ROWS = 64


def _combined(x_ref, rg_ref, rows, y0, y1):
    rg = rg_ref[rows, :]
    return x_ref[rows, :] + rg[:, 0:1] * y0 + rg[:, 1:2] * y1


_COMBINE_SCRATCH = [pltpu.VMEM((2, TM * TOP_K, ROW_TILE, LANES), F32),
                    pltpu.VMEM((TOP_K, TM, ROW_TILE, LANES), F32), pltpu.VMEM((TOP_K, TM, ROW_TILE, LANES), F32),
                    pltpu.SemaphoreType.DMA((2,))]


def _combine_kernel(lpos_ref, run_ref, rdst_ref, x_ref, rg_ref, ys_ref, o_ref, ystage, ybuf0, ybuf1, sem):
    pieces = _gather_rows(lpos_ref, run_ref, rdst_ref, ys_ref, ystage, (ybuf0, ybuf1), sem, pl.program_id(0))

    def store(rows, out):
        o_ref[rows, :] = out

    pieces(functools.partial(_combined, x_ref, rg_ref), store)


def _combine(cplan, x_all, rg_all, ys):
    return pl.pallas_call(
        _combine_kernel,
        grid_spec=pltpu.PrefetchScalarGridSpec(
            num_scalar_prefetch=3,
            grid=(N_ROW_BLOCKS,),
            in_specs=[pl.BlockSpec((TM, D_MODEL), lambda i, a, b, c: (i, 0)),
                      pl.BlockSpec((TM, LANES), lambda i, a, b, c: (i, 0)),
                      pl.BlockSpec(memory_space=pl.ANY)],
            out_specs=pl.BlockSpec((TM, D_MODEL), lambda i, a, b, c: (i, 0)),
            scratch_shapes=_COMBINE_SCRATCH,
        ),
        out_shape=jax.ShapeDtypeStruct((T_ALL, D_MODEL), F32),
        compiler_params=_cparams(("arbitrary",)),
        name="moe_combine",
    )(*cplan, x_all, rg_all, ys)


def _final_kernel(lpos_ref, run_ref, rdst_ref, x_ref, rg_ref, ys_ref, nfin_ref, op_ref, os_ref,
                  ystage, ybuf0, ybuf1, sem):
    i = pl.program_id(0)
    pieces = _gather_rows(lpos_ref, run_ref, rdst_ref, ys_ref, ystage, (ybuf0, ybuf1), sem, i)

    def compute(rows, y0, y1):
        return _rms(_combined(x_ref, rg_ref, rows, y0, y1), nfin_ref[...])

    def store(rows, y):
        @pl.when(i < N_PROMPT_BLOCKS)
        def _():
            op_ref[rows, :] = y

        @pl.when(i >= N_PROMPT_BLOCKS)
        def _():
            os_ref[rows, :] = y

    pieces(compute, store)


def _final(cplan, x_all, rg_all, ys, nfin):
    return pl.pallas_call(
        _final_kernel,
        grid_spec=pltpu.PrefetchScalarGridSpec(
            num_scalar_prefetch=3,
            grid=(N_ROW_BLOCKS,),
            in_specs=[pl.BlockSpec((TM, D_MODEL), lambda i, a, b, c: (i, 0)),
                      pl.BlockSpec((TM, LANES), lambda i, a, b, c: (i, 0)),
                      pl.BlockSpec(memory_space=pl.ANY),
                      pl.BlockSpec((1, D_MODEL), lambda i, a, b, c: (0, 0))],
            out_specs=[pl.BlockSpec((TM, D_MODEL), lambda i, a, b, c: (jnp.minimum(i, N_PROMPT_BLOCKS - 1), 0)),
                       pl.BlockSpec((TM, D_MODEL), lambda i, a, b, c: (0, 0))],
            scratch_shapes=_COMBINE_SCRATCH,
        ),
        out_shape=[jax.ShapeDtypeStruct((T_PROMPT, D_MODEL), F32), jax.ShapeDtypeStruct((T_SAMPLE, D_MODEL), F32)],
        compiler_params=_cparams(("arbitrary",)),
        name="moe_combine_final",
    )(*cplan, x_all, rg_all, ys, nfin)


def _moe(h_all, rt_all, cnt, tcnt, w_gate, w_up, w_down, layer):
    plan, dplan, cplan = _moe_metadata(rt_all, cnt, tcnt)
    xs = _dispatch(dplan, h_all)
    ys = _experts(*plan, xs, w_gate, w_up, w_down, layer)
    return cplan, ys


def _pool_project(d_groups, wp_ref, scale):
    outs = [_dot(d_groups[g].astype(BF16), wp_ref[g]) for g in range(len(POOL_SIZES))]
    return jnp.concatenate(outs, axis=1) * scale


def _mix1_prompt_kernel(x_ref, nm_ref, wp_ref, sc_ref, nf_ref, wr_ref, br_ref,
                        x3_ref, h_ref, ri_ref, rg_ref, tc_ref, pl_ref, cnt_ref, ext):
    i = pl.program_id(0)

    @pl.when(i == 0)
    def _():
        cnt_ref[...] = jnp.zeros_like(cnt_ref)

    x = x_ref[...]
    hp = _rms(x, nm_ref[...])

    @pl.when(i % STEPS_PER_BATCH == 0)
    def _():
        ext[0:POOL_MAX, :] = jnp.zeros((POOL_MAX, D_MODEL), F32)

    ext[POOL_MAX:, :] = hp
    pos = (i % STEPS_PER_BATCH) * TM + lax.broadcasted_iota(jnp.int32, (TM, 1), 0)
    d_groups = []
    for g, w in enumerate(POOL_SIZES):
        cols = slice(g * POOL_GROUP_DIM, (g + 1) * POOL_GROUP_DIM)
        acc = ext[:, cols]
        span = 1
        while span < w:
            acc = acc + pltpu.roll(acc, span, 0)
            span *= 2
        cnt = jnp.minimum(pos + 1, w).astype(F32)
        d_groups.append(acc[POOL_MAX:] / cnt - hp[:, cols])
    tail = hp[TM - POOL_MAX:, :]
    ext[0:POOL_MAX, :] = tail
    pl_ref[...] = tail

    x3 = x + _pool_project(d_groups, wp_ref, sc_ref[...])
    x3_ref[...] = x3
    h, ids, gates = _route(x3, nf_ref[...], wr_ref[...], br_ref[...])
    h_ref[...] = h.reshape(h_ref.shape)
    ri_ref[...] = _rank_pack(ids, cnt_ref, tc_ref)
    rg_ref[...] = gates


def _mix1_prompt(x_all, nm, wp, sc, nf, wr, br):
    row_spec = pl.BlockSpec((TM, D_MODEL), lambda i: (i, 0))
    row3_spec = pl.BlockSpec((TM, ROW_TILE, LANES), lambda i: (i, 0, 0))
    lane_spec = pl.BlockSpec((TM, LANES), lambda i: (i, 0))
    return pl.pallas_call(
        _prompt_steps(_mix1_prompt_kernel, 7),
        grid=(N_ROW_BLOCKS,),
        in_specs=[row_spec, _const_spec((1, D_MODEL)),
                  _const_spec((len(POOL_SIZES), POOL_GROUP_DIM, POOL_GROUP_DIM)), _const_spec((1, D_MODEL)),
                  _const_spec((1, D_MODEL)), _const_spec((D_MODEL, 2 * LANES)), _const_spec((1, LANES))],
        out_specs=[row_spec, row3_spec, pl.BlockSpec((8, TM), lambda i: (0, i)), lane_spec,
                   pl.BlockSpec((None, 1, LANES), lambda i: (i, 0, 0)),
                   pl.BlockSpec((None, POOL_MAX, D_MODEL),
                                lambda i: (jnp.minimum(i // STEPS_PER_BATCH, BATCH - 1), 0, 0)),
                   _const_spec((1, LANES))],
        out_shape=[jax.ShapeDtypeStruct((T_ALL, D_MODEL), F32), jax.ShapeDtypeStruct((T_ALL, ROW_TILE, LANES), BF16),
                   jax.ShapeDtypeStruct((8, T_ALL), jnp.int32), jax.ShapeDtypeStruct((T_ALL, LANES), F32),
                   jax.ShapeDtypeStruct((N_ROW_BLOCKS, 1, LANES), F32),
                   jax.ShapeDtypeStruct((BATCH, POOL_MAX, D_MODEL), F32), jax.ShapeDtypeStruct((1, LANES), F32)],
        scratch_shapes=[pltpu.VMEM((POOL_MAX + TM, D_MODEL), F32)],
        compiler_params=_cparams(("arbitrary",)),
        name="mix1_prompt",
    )(x_all, nm, wp, sc, nf, wr, br)


def _mix1_sample_kernel(x_ref, st_ref, nm_ref, wp_ref, sc_ref, nf_ref, wr_ref, br_ref, cnt_in,
                        x3_in, h_in, ri_in, rg_in, tc_in,
                        x3_ref, h_ref, ri_ref, rg_ref, tc_ref, hs_ref, cnt_ref):
    del x3_in, h_in, ri_in, rg_in, tc_in
    x = x_ref[...]
    hs = _rms(x, nm_ref[...])
    hs_ref[...] = hs
    n_ctx = POOL_MAX - 1
    d_groups = []
    for g, w in enumerate(POOL_SIZES):
        cols = slice(g * POOL_GROUP_DIM, (g + 1) * POOL_GROUP_DIM)
        parts = []
        for t in range(DEC_SEQ):
            acc = hs[t * DEC_BATCH:(t + 1) * DEC_BATCH, cols]
            for back in range(1, w):
                src = t - back
                if src >= 0:
                    acc = acc + hs[src * DEC_BATCH:(src + 1) * DEC_BATCH, cols]
                else:
                    acc = acc + st_ref[n_ctx + src, :, cols]
            parts.append(acc / float(w) - hs[t * DEC_BATCH:(t + 1) * DEC_BATCH, cols])
        d_groups.append(jnp.concatenate(parts, axis=0))
    x3 = x + _pool_project(d_groups, wp_ref, sc_ref[...])
    x3_ref[...] = x3
    h, ids, gates = _route(x3, nf_ref[...], wr_ref[...], br_ref[...])
    h_ref[...] = h.reshape(h_ref.shape)
    cnt_ref[...] = cnt_in[...]
    ri_ref[...] = _rank_pack(ids, cnt_ref, tc_ref)
    rg_ref[...] = gates


def _mix1_sample(x_all, state_t, nm, wp, sc, nf, wr, br, cnt, x3_all, h_all, ri_all, rg_all, tc_all):
    sample_rows = pl.BlockSpec((TM, D_MODEL), lambda g: (N_PROMPT_BLOCKS, 0))
    sample_rows3 = pl.BlockSpec((TM, ROW_TILE, LANES), lambda g: (N_PROMPT_BLOCKS, 0, 0))
    sample_lanes = pl.BlockSpec((TM, LANES), lambda g: (N_PROMPT_BLOCKS, 0))
    anyspec = pl.BlockSpec(memory_space=pl.ANY)
    n_in = 9
    return pl.pallas_call(
        _mix1_sample_kernel,
        grid=(1,),
        in_specs=[sample_rows, _const_spec((POOL_MAX - 1, DEC_BATCH, D_MODEL)), _const_spec((1, D_MODEL)),
                  _const_spec((len(POOL_SIZES), POOL_GROUP_DIM, POOL_GROUP_DIM)), _const_spec((1, D_MODEL)),
                  _const_spec((1, D_MODEL)), _const_spec((D_MODEL, 2 * LANES)), _const_spec((1, LANES)),
                  _const_spec((1, LANES)), anyspec, anyspec, anyspec, anyspec, anyspec],
        out_specs=[sample_rows, sample_rows3, pl.BlockSpec((8, TM), lambda g: (0, N_PROMPT_BLOCKS)), sample_lanes,
                   pl.BlockSpec((None, 1, LANES), lambda g: (N_PROMPT_BLOCKS, 0, 0)),
                   _const_spec((T_SAMPLE, D_MODEL)), _const_spec((1, LANES))],
        out_shape=[jax.ShapeDtypeStruct((T_ALL, D_MODEL), F32), jax.ShapeDtypeStruct((T_ALL, ROW_TILE, LANES), BF16),
                   jax.ShapeDtypeStruct((8, T_ALL), jnp.int32), jax.ShapeDtypeStruct((T_ALL, LANES), F32),
                   jax.ShapeDtypeStruct((N_ROW_BLOCKS, 1, LANES), F32),
                   jax.ShapeDtypeStruct((T_SAMPLE, D_MODEL), F32), jax.ShapeDtypeStruct((1, LANES), F32)],
        input_output_aliases={n_in: 0, n_in + 1: 1, n_in + 2: 2, n_in + 3: 3, n_in + 4: 4},
        compiler_params=_cparams(("arbitrary",)),
        name="mix1_sample",
    )(x_all, state_t, nm, wp, sc, nf, wr, br, cnt, x3_all, h_all, ri_all, rg_all, tc_all)


def _router_weights(wg, bg, we, be):
    w = jnp.concatenate([wg, jnp.transpose(we, (1, 0, 2)).reshape(D_MODEL, N_EXPERTS)], axis=1)
    b = jnp.concatenate([bg, be.reshape(N_EXPERTS)])
    pad = LANES - N_GROUPS - N_EXPERTS
    w = jnp.pad(w, ((0, 0), (0, pad)))
    w_hi = w.astype(BF16)
    w_lo = (w - w_hi.astype(F32)).astype(BF16)
    return jnp.concatenate([w_hi, w_lo], axis=1), jnp.pad(b, (0, pad)).reshape(1, LANES)


def kernel(x_prompt, x_sample, cache_k_win, cache_v_win, state_pool, norm_mix, norm_ffn, norm_final, w_in,
           a_ln_g, a_ln_b, a_w_s, a_b_s, b_sinks, rel_bias_table, w_out, c_w_pool, c_scale,
           router_group_w, router_group_b, router_expert_w, router_expert_b, w_gate, w_up, w_down):
    xs_t = jnp.transpose(x_sample, (1, 0, 2)).reshape(T_SAMPLE, D_MODEL)
    xp2 = x_prompt.reshape(T_PROMPT, D_MODEL)
    win =w_in[0].astype(BF16)
    wout = w_out[0].astype(BF16)
    lng = a_ln_g[0].reshape(1, A_WIDTH)
    lnb = a_ln_b[0].reshape(1, A_WIDTH)
    bias_p, bias_sc, bias_sn, wsp = _prep(rel_bias_table, b_sinks[0], a_w_s[0])
    bs_full = jnp.repeat(a_b_s[0].T, A_HEAD_DIM, axis=1)
    w4 = jnp.transpose(a_w_s[0][:, :DEC_SEQ, :DEC_SEQ], (1, 2, 0)).reshape(DEC_SEQ * DEC_SEQ, A_HEADS)
    wcoef = jnp.repeat(w4, A_HEAD_DIM, axis=1)
    bcoef = jnp.pad(jnp.repeat(a_b_s[0][:, :DEC_SEQ].T, A_HEAD_DIM, axis=1), ((0, 8 - DEC_SEQ), (0, 0)))
    ck = jnp.transpose(cache_k_win[0], (0, 2, 3, 1))
    cv = jnp.transpose(cache_v_win[0], (0, 2, 3, 1))
    routers = [_router_weights(router_group_w[l], router_group_b[l], router_expert_w[l], router_expert_b[l])
               for l in range(2)]
    nm = [norm_mix[l].reshape(1, D_MODEL) for l in range(2)]
    nf = [norm_ffn[l].reshape(1, D_MODEL) for l in range(2)]

    x1_all, h_all, ri_all, rg_all, tc_all, k_last, v_last, va_last, cnt0 = _mix0_prompt(
        xp2, nm[0], win, lng, lnb, wsp, bs_full, bias_p, wout, nf[0], *routers[0])
    x1_all, h_all, ri_all, rg_all, tc_all, k_new, v_new, va_s, cnt0 = _mix0_sample(
        xs_t, nm[0], win, lng, lnb, wcoef, bcoef, ck, cv, bias_sc, bias_sn, wout, nf[0], *routers[0], cnt0,
        x1_all, h_all, ri_all, rg_all, tc_all)
    cplan0, ys0 = _moe(h_all, ri_all, cnt0, tc_all, w_gate, w_up, w_down, 0)
    x2_all = _combine(cplan0, x1_all, rg_all, ys0)

    wp = c_w_pool[0].astype(BF16)
    sc = c_scale[0].reshape(1, D_MODEL)
    x3_all, h2_all, ri2_all, rg2_all, tc2_all, pool_tail, cnt1 = _mix1_prompt(
        x2_all, nm[1], wp, sc, nf[1], *routers[1])
    state_t = jnp.transpose(state_pool[0], (1, 0, 2))
    x3_all, h2_all, ri2_all, rg2_all, tc2_all, hs1, cnt1 = _mix1_sample(
        x2_all, state_t, nm[1], wp, sc, nf[1], *routers[1], cnt1, x3_all, h2_all, ri2_all, rg2_all, tc2_all)
    cplan1, ys1 = _moe(h2_all, ri2_all, cnt1, tc2_all, w_gate, w_up, w_down, 1)
    y_p, y_s = _final(cplan1, x3_all, rg2_all, ys1, norm_final.reshape(1, D_MODEL))

    def from_tmajor(a, width):
        return jnp.transpose(a.reshape(DEC_SEQ, DEC_BATCH, width), (1, 0, 2))

    y_prompt = y_p.reshape(BATCH, SEQ, D_MODEL)
    y_sample = from_tmajor(y_s, D_MODEL)
    win_k_p = k_last.reshape(1, BATCH, WINDOW, B_KV_HEADS, B_HEAD_DIM)
    win_v_p = v_last.reshape(1, BATCH, WINDOW, B_KV_HEADS, B_HEAD_DIM)
    win_k_s = jnp.transpose(k_new, (0, 3, 1, 2))[None]
    win_v_s = jnp.transpose(v_new, (0, 3, 1, 2))[None]
    chunk_v_p = va_last.reshape(1, BATCH, CHUNK, A_HEADS, A_HEAD_DIM)
    chunk_v_s = from_tmajor(va_s, A_WIDTH).reshape(1, DEC_BATCH, DEC_SEQ, A_HEADS, A_HEAD_DIM)
    pool_p = pool_tail[:, 1:][None]
    pool_s = jnp.concatenate([state_pool[0][:, DEC_SEQ:], from_tmajor(hs1, D_MODEL)], axis=1)[None]
    return (y_prompt, y_sample, win_k_p, win_v_p, win_k_s, win_v_s, chunk_v_p, chunk_v_s, pool_p, pool_s)
```

```python
import functools
import math

import numpy as np
import jax
import jax.numpy as jnp
from jax import lax
from jax.experimental import pallas as pl
from jax.experimental.pallas import tpu as pltpu

F32 = jnp.float32
BF16 = jnp.bfloat16

D_MODEL = 1024
BATCH = 2
SEQ = 8192
DEC_BATCH = 128
DEC_SEQ = 4
A_WIDTH = 512
A_HEADS = 8
A_HEAD_DIM = 64
CHUNK = 128
B_HEADS = 8
B_KV_HEADS = 2
B_HEAD_DIM = 64
B_GROUP = 4
WINDOW = 128
N_BUCKETS = 32
MAX_DISTANCE = WINDOW
Q_WIDTH = 512
KV_WIDTH = 128
IN_WIDTH = 2 * A_WIDTH + Q_WIDTH + 2 * KV_WIDTH
ATTN_SCALE = B_HEAD_DIM ** -0.5
NEG_INF = -1e30
POOL_SIZES = (2, 4, 8, 16)
POOL_GROUP_DIM = 256
POOL_MAX = 16
N_GROUPS = 4
EXPERTS_PER_GROUP = 8
N_EXPERTS = 32
TOP_K = 2
D_EXPERT = 512
EPS = 1e-6

LANES = 128
ROW_TILE = D_MODEL // LANES
T_PROMPT = BATCH * SEQ
T_SAMPLE = DEC_BATCH * DEC_SEQ
T_ALL = T_PROMPT + T_SAMPLE
TM = 512
N_PROMPT_BLOCKS = T_PROMPT // TM
N_ROW_BLOCKS = T_ALL // TM
STEPS_PER_BATCH = SEQ // TM
SUB = TM // WINDOW
N_SLOTS = T_ALL * TOP_K
MOE_BLK = 512
N_MOE_BLOCKS = N_SLOTS // MOE_BLK + N_EXPERTS
N_SORT_ROWS = N_MOE_BLOCKS * MOE_BLK
SAMPLE_GROUP = 8
N_SAMPLE_GROUPS = DEC_BATCH // SAMPLE_GROUP
VMEM_LIMIT = 56 * 1024 * 1024

STACK_HEADS = ((0, 2, 5, 7), (1, 3, 4, 6))


def _t5_bucket_np(dist):
    n = np.maximum(dist, 0)
    max_exact = N_BUCKETS // 2
    nf = np.maximum(n, 1).astype(np.float32)
    large = max_exact + (np.log(nf / np.float32(max_exact)) / np.float32(math.log(MAX_DISTANCE / max_exact))
                         * np.float32(N_BUCKETS - max_exact)).astype(np.int32)
    large = np.minimum(large, N_BUCKETS - 1)
    return np.where(n < max_exact, n, large).astype(np.int32)


def _bucket_tables():
    qi = np.arange(WINDOW)[:, None]
    ki = np.arange(2 * WINDOW)[None, :]
    dist = qi + WINDOW - ki
    valid = (dist >= 0) & (dist < WINDOW)
    bp = np.where(valid, _t5_bucket_np(dist), -1)
    bp_first = np.where(ki >= WINDOW, bp, -1)
    bkt_p = np.stack([bp_first, bp]).astype(np.int32)

    t = np.repeat(np.arange(DEC_SEQ), SAMPLE_GROUP)[:, None]
    b = np.tile(np.arange(SAMPLE_GROUP), DEC_SEQ)[:, None]
    cb = np.repeat(np.arange(SAMPLE_GROUP), WINDOW)[None, :]
    cj = np.tile(np.arange(WINDOW), SAMPLE_GROUP)[None, :]
    dist_c = t + WINDOW - cj
    valid_c = (cb == b) & (dist_c >= 0) & (dist_c < WINDOW)
    bkt_sc = np.where(valid_c, _t5_bucket_np(dist_c), -1).astype(np.int32)
    nt = np.repeat(np.arange(DEC_SEQ), SAMPLE_GROUP)[None, :]
    nb = np.tile(np.arange(SAMPLE_GROUP), DEC_SEQ)[None, :]
    dist_n = t - nt
    valid_n = (nb == b) & (dist_n >= 0)
    bkt_sn = np.where(valid_n, _t5_bucket_np(dist_n), -1).astype(np.int32)
    bkt_sn = np.concatenate([bkt_sn, np.full((32, LANES - 32), -1, np.int32)], axis=1)
    return bkt_p, bkt_sc, bkt_sn


_BKT_P, _BKT_SC, _BKT_SN = _bucket_tables()


def _cparams(semantics):
    return pltpu.CompilerParams(dimension_semantics=semantics, vmem_limit_bytes=VMEM_LIMIT)


def _rms(x, g):
    return x * lax.rsqrt(jnp.mean(x * x, axis=-1, keepdims=True) + EPS) * g


def _layernorm(x, g, b):
    xc = x - jnp.mean(x, axis=-1, keepdims=True)
    return xc * lax.rsqrt(jnp.mean(xc * xc, axis=-1, keepdims=True) + EPS) * g + b


def _dot(a, b):
    return jnp.dot(a, b, preferred_element_type=F32)


def _dot_nt(a, b):
    return lax.dot_general(a, b, (((1,), (1,)), ((), ())), preferred_element_type=F32)


def _project(x, nm, win, lng, lnb):
    h = _rms(x, nm)
    z = _dot(h.astype(BF16), win)
    u = jax.nn.gelu(z[:, :A_WIDTH])
    va = _layernorm(jax.nn.gelu(z[:, A_WIDTH:2 * A_WIDTH]), lng, lnb)
    q = z[:, 2 * A_WIDTH:2 * A_WIDTH + Q_WIDTH] * ATTN_SCALE
    k = z[:, 2 * A_WIDTH + Q_WIDTH:2 * A_WIDTH + Q_WIDTH + KV_WIDTH]
    v = z[:, 2 * A_WIDTH + Q_WIDTH + KV_WIDTH:]
    return u, va, q, k, v


def _route(x1, nf, wr, br):
    hf = _rms(x1, nf)
    h = hf.astype(BF16)
    h_lo = (hf - h.astype(F32)).astype(BF16)
    part = _dot(h, wr)
    logits = part[:, :LANES] + part[:, LANES:] + _dot(h_lo, wr[:, :LANES]) + br
    rows = logits.shape[0]
    lane = lax.broadcasted_iota(jnp.int32, (rows, LANES), 1)
    lanef = lane.astype(F32)
    big = jnp.float32(1e9)
    is_g = lane < N_GROUPS
    gl = jnp.where(is_g, logits, -jnp.inf)
    gmax = jnp.max(gl, axis=1, keepdims=True)
    gsel = jnp.min(jnp.where(gl == gmax, lanef, big), axis=1, keepdims=True)
    gsum = jnp.sum(jnp.where(is_g, jnp.exp(logits - gmax), 0.0), axis=1, keepdims=True)
    g1 = 1.0 / gsum
    lo = N_GROUPS + EXPERTS_PER_GROUP * gsel
    emask = (lanef >= lo) & (lanef < lo + EXPERTS_PER_GROUP)
    el = jnp.where(emask, logits, -jnp.inf)
    v1 = jnp.max(el, axis=1, keepdims=True)
    i1 = jnp.min(jnp.where(el == v1, lanef, big), axis=1, keepdims=True)
    el2 = jnp.where(lanef == i1, -jnp.inf, el)
    v2 = jnp.max(el2, axis=1, keepdims=True)
    i2 = jnp.min(jnp.where(el2 == v2, lanef, big), axis=1, keepdims=True)
    e2 = jnp.exp(v2 - v1)
    den = 1.0 + e2
    w1 = g1 / den
    w2 = g1 * e2 / den
    ids = jnp.where(lane == 0, i1 - N_GROUPS, jnp.where(lane == 1, i2 - N_GROUPS, 0.0)).astype(jnp.int32)
    gates = jnp.where(lane == 0, w1, jnp.where(lane == 1, w2, 0.0))
    return h, ids, gates


def _rank_pack(ids, cnt_ref, tcnt_ref):
    rows = ids.shape[0]
    lane = lax.broadcasted_iota(jnp.int32, (rows, LANES), 1)
    o0 = (lane == ids[:, 0:1]).astype(F32)
    o1 = (lane == ids[:, 1:2]).astype(F32)
    r = lax.broadcasted_iota(jnp.int32, (rows, rows), 0)
    c = lax.broadcasted_iota(jnp.int32, (rows, rows), 1)
    before = (c < r).astype(BF16)
    p01 = _dot(before, jnp.concatenate([o0, o1], axis=1).astype(BF16))
    p0 = p01[:, :LANES]
    p1 = p01[:, LANES:]
    c0 = jnp.sum(o0, axis=0, keepdims=True)
    c1 = jnp.sum(o1, axis=0, keepdims=True)
    ctile = c0 + c1
    cnt_ref[...] = cnt_ref[...] + ctile
    tcnt_ref[...] = ctile
    inc = jnp.broadcast_to(ctile, (8, LANES))
    lane8 = lax.broadcasted_iota(jnp.int32, (8, LANES), 1)
    for sh in (1, 2, 4, 8, 16, 32, 64):
        inc = inc + jnp.where(lane8 >= sh, pltpu.roll(inc, sh, 1), 0.0)
    start = inc[0:1] - ctile
    lpos0 = jnp.sum(o0 * (start + p0), axis=1, keepdims=True)
    lpos1 = jnp.sum(o1 * (start + c0 + p1), axis=1, keepdims=True)
    idf = ids.astype(F32)
    packed = jnp.where(lane < TOP_K, idf, 0.0)
    for ln, col in ((4, lpos0), (5, lpos1)):
        packed = jnp.where(lane == ln, col, packed)
    return jnp.transpose(packed)[:8].astype(jnp.int32)


def _prep_kernel(tab_ref, sink_ref, bp_ref, bsc_ref, bsn_ref, ws_ref, op_ref, osc_ref, osn_ref, ows_ref):
    def fill(bkt, write, sink_col0):
        col0 = lax.broadcasted_iota(jnp.int32, bkt.shape, 1) == 0
        for st, heads in enumerate(STACK_HEADS):
            for slot, h in enumerate(heads):
                acc = jnp.full(bkt.shape, NEG_INF, F32)
                for b in range(N_BUCKETS):
                    acc = jnp.where(bkt == b, tab_ref[b, h], acc)
                if sink_col0:
                    acc = jnp.where(col0, sink_ref[0, h], acc)
                write(st, slot, acc)

    for var in range(2):
        def wr_p(st, slot, acc, var=var):
            op_ref[var, st, slot * WINDOW:(slot + 1) * WINDOW, :] = acc
        fill(bp_ref[var], wr_p, True)

    rows_s = DEC_SEQ * SAMPLE_GROUP

    def wr_sc(st, slot, acc):
        osc_ref[st, slot * rows_s:(slot + 1) * rows_s, :] = acc
    fill(bsc_ref[...], wr_sc, True)

    def wr_sn(st, slot, acc):
        osn_ref[st, slot * rows_s:(slot + 1) * rows_s, :] = acc
    fill(bsn_ref[...], wr_sn, False)

    r = lax.broadcasted_iota(jnp.int32, (CHUNK, CHUNK), 0)
    c = lax.broadcasted_iota(jnp.int32, (CHUNK, CHUNK), 1)
    for h in range(A_HEADS):
        ows_ref[h // 2, :, (h % 2) * CHUNK:(h % 2 + 1) * CHUNK] = jnp.where(r >= c, ws_ref[h], 0.0).astype(BF16)


def _prep(rel_bias_table, sinks, w_s):
    vm = pl.BlockSpec(memory_space=pltpu.VMEM)
    sm = pl.BlockSpec(memory_space=pltpu.SMEM)
    rows_s = DEC_SEQ * SAMPLE_GROUP
    return pl.pallas_call(
        _prep_kernel,
        in_specs=[sm, sm, vm, vm, vm, vm],
        out_specs=[vm, vm, vm, vm],
        out_shape=[
            jax.ShapeDtypeStruct((2, 2, 4 * WINDOW, 2 * WINDOW), F32),
            jax.ShapeDtypeStruct((2, 4 * rows_s, SAMPLE_GROUP * WINDOW), F32),
            jax.ShapeDtypeStruct((2, 4 * rows_s, LANES), F32),
            jax.ShapeDtypeStruct((A_HEADS // 2, CHUNK, 2 * CHUNK), BF16),
        ],
        name="prep_tables",
    )(rel_bias_table, sinks.reshape(1, B_HEADS), jnp.asarray(_BKT_P), jnp.asarray(_BKT_SC), jnp.asarray(_BKT_SN), w_s)


def _gate_pairs(va_rows, wsp_ref, lane_lo):
    outs = []
    for p in range(A_HEADS // 2):
        vp = va_rows[:, p * LANES:(p + 1) * LANES]
        rhs = jnp.concatenate([jnp.where(lane_lo, vp, 0.0), jnp.where(lane_lo, 0.0, vp)], axis=0).astype(BF16)
        outs.append(_dot(wsp_ref[p], rhs))
    return jnp.concatenate(outs, axis=1)


def _prompt_steps(body, first_row_out):
    def kern(*refs):
        i = pl.program_id(0)

        @pl.when(i < N_PROMPT_BLOCKS)
        def _():
            body(*refs)

        @pl.when(i >= N_PROMPT_BLOCKS)
        def _():
            for r in refs[first_row_out:first_row_out + 5]:
                r[...] = jnp.zeros(r.shape, r.dtype)

    return kern


def _mix0_prompt_kernel(x_ref, nm_ref, win_ref, lng_ref, lnb_ref, wsp_ref, bs_ref, bias_ref,
                        wout_ref, nf_ref, wr_ref, br_ref,
                        x1_ref, h_ref, ri_ref, rg_ref, tc_ref, kl_ref, vl_ref, val_ref, cnt_ref,
                        kprev, vprev, mix_scr):
    @pl.when(pl.program_id(0) == 0)
    def _():
        cnt_ref[...] = jnp.zeros_like(cnt_ref)

    x = x_ref[...]
    u, va, q, k, v = _project(x, nm_ref[...], win_ref[...], lng_ref[...], lnb_ref[...])
    lane_lo = lax.broadcasted_iota(jnp.int32, (WINDOW, LANES), 1) < B_HEAD_DIM
    row0 = lax.broadcasted_iota(jnp.int32, (WINDOW, KV_WIDTH), 0) == 0
    first = pl.program_id(0) % STEPS_PER_BATCH == 0

    @pl.when(first)
    def _():
        kprev[...] = jnp.zeros_like(kprev)
        vprev[...] = jnp.zeros_like(vprev)

    for j in range(SUB):
        rows = slice(j * WINDOW, (j + 1) * WINDOW)
        s_gate = _gate_pairs(va[rows], wsp_ref, lane_lo)
        mix_scr[rows, :A_WIDTH] = u[rows] * (s_gate + bs_ref[...])

        if j == 0:
            kp, vp = kprev[...], vprev[...]
        else:
            prows = slice((j - 1) * WINDOW, j * WINDOW)
            kp, vp = k[prows], v[prows]
        kk = jnp.concatenate([jnp.where(row0, 0.0, kp), k[rows]], axis=0)
        vv = jnp.concatenate([jnp.where(row0, 0.0, vp), v[rows]], axis=0)
        kops = (kk.astype(BF16), pltpu.roll(kk, B_HEAD_DIM, 1).astype(BF16))
        vops = (vv.astype(BF16), pltpu.roll(vv, B_HEAD_DIM, 1).astype(BF16))
        qt = [q[rows, p * LANES:(p + 1) * LANES] for p in range(4)]
        q_even = [jnp.where(lane_lo, t, 0.0) for t in qt]
        q_odd = [jnp.where(lane_lo, 0.0, t) for t in qt]
        stacks = (jnp.concatenate([q_even[0], q_even[1], q_odd[2], q_odd[3]], axis=0),
                  jnp.concatenate([q_odd[0], q_odd[1], q_even[2], q_even[3]], axis=0))
        o = []
        for st in range(2):
            s = _dot_nt(stacks[st].astype(BF16), kops[st])
            if j == 0:
                bias = bias_ref[jnp.where(first, 0, 1), st]
            else:
                bias = bias_ref[1, st]
            s = s + bias
            m = jnp.max(s, axis=-1, keepdims=True)
            p = jnp.exp(s - m)
            den = jnp.sum(p, axis=-1, keepdims=True)
            o.append(_dot(p.astype(BF16), vops[st]) / den)
        oa, ob = o
        sl = [slice(i * WINDOW, (i + 1) * WINDOW) for i in range(4)]
        tiles = (jnp.where(lane_lo, oa[sl[0]], ob[sl[0]]), jnp.where(lane_lo, oa[sl[1]], ob[sl[1]]),
                 jnp.where(lane_lo, ob[sl[2]], oa[sl[2]]), jnp.where(lane_lo, ob[sl[3]], oa[sl[3]]))
        for p in range(4):
            mix_scr[rows, A_WIDTH + p * LANES:A_WIDTH + (p + 1) * LANES] = tiles[p]

    last = slice(TM - WINDOW, TM)
    kprev[...] = k[last]
    vprev[...] = v[last]
    kl_ref[...] = k[last]
    vl_ref[...] = v[last]
    val_ref[...] = va[last]

    x1 = x + _dot(mix_scr[...].astype(BF16), wout_ref[...])
    x1_ref[...] = x1
    h, ids, gates = _route(x1, nf_ref[...], wr_ref[...], br_ref[...])
    h_ref[...] = h.reshape(h_ref.shape)
    ri_ref[...] = _rank_pack(ids, cnt_ref, tc_ref)
    rg_ref[...] = gates


def _const_spec(shape):
    nd = len(shape)
    return pl.BlockSpec(shape, lambda i, _n=nd: (0,) * _n)


def _mix0_prompt(x_all, nm, win, lng, lnb, wsp, bs_full, bias_p, wout, nf, wr, br):
    row_spec = pl.BlockSpec((TM, D_MODEL), lambda i: (i, 0))
    row3_spec = pl.BlockSpec((TM, ROW_TILE, LANES), lambda i: (i, 0, 0))
    lane_spec = pl.BlockSpec((TM, LANES), lambda i: (i, 0))
    last_kv = pl.BlockSpec((None, WINDOW, KV_WIDTH), lambda i: (jnp.minimum(i // STEPS_PER_BATCH, BATCH - 1), 0, 0))
    last_va = pl.BlockSpec((None, WINDOW, A_WIDTH), lambda i: (jnp.minimum(i // STEPS_PER_BATCH, BATCH - 1), 0, 0))
    return pl.pallas_call(
        _prompt_steps(_mix0_prompt_kernel, 12),
        grid=(N_ROW_BLOCKS,),
        in_specs=[pl.BlockSpec((TM, D_MODEL), lambda i: (jnp.minimum(i, N_PROMPT_BLOCKS - 1), 0)),
                  _const_spec((1, D_MODEL)), _const_spec((D_MODEL, IN_WIDTH)),
                  _const_spec((1, A_WIDTH)), _const_spec((1, A_WIDTH)),
                  _const_spec((A_HEADS // 2, CHUNK, 2 * CHUNK)), _const_spec((CHUNK, A_WIDTH)),
                  _const_spec((2, 2, 4 * WINDOW, 2 * WINDOW)),
                  _const_spec((A_WIDTH + Q_WIDTH, D_MODEL)), _const_spec((1, D_MODEL)),
                  _const_spec((D_MODEL, 2 * LANES)), _const_spec((1, LANES))],
        out_specs=[row_spec, row3_spec, pl.BlockSpec((8, TM), lambda i: (0, i)), lane_spec,
                   pl.BlockSpec((None, 1, LANES), lambda i: (i, 0, 0)),
                   last_kv, last_kv, last_va, _const_spec((1, LANES))],
        out_shape=[jax.ShapeDtypeStruct((T_ALL, D_MODEL), F32), jax.ShapeDtypeStruct((T_ALL, ROW_TILE, LANES), BF16),
                   jax.ShapeDtypeStruct((8, T_ALL), jnp.int32), jax.ShapeDtypeStruct((T_ALL, LANES), F32),
                   jax.ShapeDtypeStruct((N_ROW_BLOCKS, 1, LANES), F32),
                   jax.ShapeDtypeStruct((BATCH, WINDOW, KV_WIDTH), F32),
                   jax.ShapeDtypeStruct((BATCH, WINDOW, KV_WIDTH), F32),
                   jax.ShapeDtypeStruct((BATCH, WINDOW, A_WIDTH), F32),
                   jax.ShapeDtypeStruct((1, LANES), F32)],
        scratch_shapes=[pltpu.VMEM((WINDOW, KV_WIDTH), F32), pltpu.VMEM((WINDOW, KV_WIDTH), F32),
                        pltpu.VMEM((TM, D_MODEL), F32)],
        compiler_params=_cparams(("arbitrary",)),
        name="mix0_prompt",
    )(x_all, nm, win, lng, lnb, wsp, bs_full, bias_p, wout, nf, wr, br)


def _mix0_sample_kernel(x_ref, nm_ref, win_ref, lng_ref, lnb_ref, wcoef_ref, bcoef_ref,
                        ck_ref, cv_ref, bsc_ref, bsn_ref,
                        wout_ref, nf_ref, wr_ref, br_ref, cnt_in,
                        x1_in, h_in, ri_in, rg_in, tc_in,
                        x1_ref, h_ref, ri_ref, rg_ref, tc_ref, kn_ref, vn_ref, va_ref, cnt_ref,
                        q_scr, k_scr, v_scr, mix_scr):
    del x1_in, h_in, ri_in, rg_in, tc_in
    g = pl.program_id(0)

    @pl.when(g == 0)
    def _():
        u, va, q, k, v = _project(x_ref[...], nm_ref[...], win_ref[...], lng_ref[...], lnb_ref[...])
        q_scr[...] = q
        k_scr[...] = k
        v_scr[...] = v
        va_ref[...] = va
        for t in range(DEC_SEQ):
            acc = jnp.zeros((DEC_BATCH, A_WIDTH), F32) + bcoef_ref[t:t + 1, :]
            for s in range(t + 1):
                row = t * DEC_SEQ + s
                acc = acc + wcoef_ref[row:row + 1, :] * va[s * DEC_BATCH:(s + 1) * DEC_BATCH]
            mix_scr[t * DEC_BATCH:(t + 1) * DEC_BATCH, :A_WIDTH] = u[t * DEC_BATCH:(t + 1) * DEC_BATCH] * acc

    b0 = pl.multiple_of(g * SAMPLE_GROUP, SAMPLE_GROUP)
    lane_lo = lax.broadcasted_iota(jnp.int32, (DEC_SEQ * SAMPLE_GROUP, LANES), 1) < B_HEAD_DIM

    def grab(ref, width):
        return jnp.concatenate([ref[pl.ds(t * DEC_BATCH + b0, SAMPLE_GROUP), :] for t in range(DEC_SEQ)], axis=0)

    qg = grab(q_scr, Q_WIDTH)
    kn = grab(k_scr, KV_WIDTH)
    vn = grab(v_scr, KV_WIDTH)

    lane_w = lax.broadcasted_iota(jnp.int32, (KV_WIDTH, WINDOW), 1)
    n_new = DEC_SEQ * SAMPLE_GROUP

    def new_window(c_ref, new_rows, w_ref):
        nt = jnp.transpose(jnp.concatenate([new_rows, jnp.zeros((WINDOW - n_new, KV_WIDTH), F32)], axis=0))
        for b in range(SAMPLE_GROUP):
            w = pltpu.roll(c_ref[b].reshape(KV_WIDTH, WINDOW), WINDOW - DEC_SEQ, 1)
            for t in range(DEC_SEQ):
                src = t * SAMPLE_GROUP + b
                dst = WINDOW - DEC_SEQ + t
                w = jnp.where(lane_w == dst, pltpu.roll(nt, (dst - src) % WINDOW, 1), w)
            w_ref[b] = w.reshape(B_KV_HEADS, B_HEAD_DIM, WINDOW)

    new_window(ck_ref, kn, kn_ref)
    new_window(cv_ref, vn, vn_ref)
    ccol0 = lax.broadcasted_iota(jnp.int32, (KV_WIDTH, SAMPLE_GROUP * WINDOW), 1) == 0

    def cache_t(ref):
        t = jnp.concatenate([ref[b].reshape(KV_WIDTH, WINDOW) for b in range(SAMPLE_GROUP)], axis=1)
        return jnp.where(ccol0, 0.0, t)

    def head_swap(t):
        return jnp.concatenate([t[B_HEAD_DIM:], t[:B_HEAD_DIM]], axis=0)

    kct = cache_t(ck_ref)
    vct = cache_t(cv_ref)
    kc_ops = (kct.astype(BF16), head_swap(kct).astype(BF16))
    vc_ops = (vct.astype(BF16), head_swap(vct).astype(BF16))
    kn_ops = (kn.astype(BF16), pltpu.roll(kn, B_HEAD_DIM, 1).astype(BF16))
    vn_ops = (vn.astype(BF16), pltpu.roll(vn, B_HEAD_DIM, 1).astype(BF16))
    qt = [qg[:, p * LANES:(p + 1) * LANES] for p in range(4)]
    q_even = [jnp.where(lane_lo, t, 0.0) for t in qt]
    q_odd = [jnp.where(lane_lo, 0.0, t) for t in qt]
    stacks = (jnp.concatenate([q_even[0], q_even[1], q_odd[2], q_odd[3]], axis=0),
              jnp.concatenate([q_odd[0], q_odd[1], q_even[2], q_even[3]], axis=0))
    o = []
    for st in range(2):
        qs = stacks[st].astype(BF16)
        sc = _dot(qs, kc_ops[st]) + bsc_ref[st]
        sn = _dot_nt(qs, kn_ops[st]) + bsn_ref[st][:, :DEC_SEQ * SAMPLE_GROUP]
        m = jnp.maximum(jnp.max(sc, axis=-1, keepdims=True), jnp.max(sn, axis=-1, keepdims=True))
        pc = jnp.exp(sc - m)
        pn = jnp.exp(sn - m)
        den = jnp.sum(pc, axis=-1, keepdims=True) + jnp.sum(pn, axis=-1, keepdims=True)
        o.append((_dot_nt(pc.astype(BF16), vc_ops[st]) + _dot(pn.astype(BF16), vn_ops[st])) / den)
    oa, ob = o
    n = DEC_SEQ * SAMPLE_GROUP
    sl = [slice(i * n, (i + 1) * n) for i in range(4)]
    tiles = (jnp.where(lane_lo, oa[sl[0]], ob[sl[0]]), jnp.where(lane_lo, oa[sl[1]], ob[sl[1]]),
             jnp.where(lane_lo, ob[sl[2]], oa[sl[2]]), jnp.where(lane_lo, ob[sl[3]], oa[sl[3]]))
    for p in range(4):
        for t in range(DEC_SEQ):
            mix_scr[pl.ds(t * DEC_BATCH + b0, SAMPLE_GROUP), A_WIDTH + p * LANES:A_WIDTH + (p + 1) * LANES] = (
                tiles[p][t * SAMPLE_GROUP:(t + 1) * SAMPLE_GROUP])

    @pl.when(g == N_SAMPLE_GROUPS - 1)
    def _():
        x1 = x_ref[...] + _dot(mix_scr[...].astype(BF16), wout_ref[...])
        x1_ref[...] = x1
        h, ids, gates = _route(x1, nf_ref[...], wr_ref[...], br_ref[...])
        h_ref[...] = h.reshape(h_ref.shape)
        cnt_ref[...] = cnt_in[...]
        ri_ref[...] = _rank_pack(ids, cnt_ref, tc_ref)
        rg_ref[...] = gates


def _mix0_sample(x_all, nm, win, lng, lnb, wcoef, bcoef, ck, cv, bias_sc, bias_sn, wout, nf, wr, br, cnt,
                 x1_all, h_all, ri_all, rg_all, tc_all):
    sample_rows = pl.BlockSpec((TM, D_MODEL), lambda g: (N_PROMPT_BLOCKS, 0))
    sample_rows3 = pl.BlockSpec((TM, ROW_TILE, LANES), lambda g: (N_PROMPT_BLOCKS, 0, 0))
    sample_lanes = pl.BlockSpec((TM, LANES), lambda g: (N_PROMPT_BLOCKS, 0))
    cache_spec = pl.BlockSpec((SAMPLE_GROUP, B_KV_HEADS, B_HEAD_DIM, WINDOW), lambda g: (g, 0, 0, 0))
    anyspec = pl.BlockSpec(memory_space=pl.ANY)
    n_in = 16
    return pl.pallas_call(
        _mix0_sample_kernel,
        grid=(N_SAMPLE_GROUPS,),
        in_specs=[_const_spec((TM, D_MODEL)), _const_spec((1, D_MODEL)), _const_spec((D_MODEL, IN_WIDTH)),
                  _const_spec((1, A_WIDTH)), _const_spec((1, A_WIDTH)),
                  _const_spec((16, A_WIDTH)), _const_spec((8, A_WIDTH)),
                  cache_spec, cache_spec,
                  _const_spec((2, 4 * 32, SAMPLE_GROUP * WINDOW)), _const_spec((2, 4 * 32, LANES)),
                  _const_spec((A_WIDTH + Q_WIDTH, D_MODEL)), _const_spec((1, D_MODEL)),
                  _const_spec((D_MODEL, 2 * LANES)), _const_spec((1, LANES)), _const_spec((1, LANES)),
                  anyspec, anyspec, anyspec, anyspec, anyspec],
        out_specs=[sample_rows, sample_rows3, pl.BlockSpec((8, TM), lambda g: (0, N_PROMPT_BLOCKS)), sample_lanes,
                   pl.BlockSpec((None, 1, LANES), lambda g: (N_PROMPT_BLOCKS, 0, 0)),
                   cache_spec, cache_spec,
                   _const_spec((T_SAMPLE, A_WIDTH)), _const_spec((1, LANES))],
        out_shape=[jax.ShapeDtypeStruct((T_ALL, D_MODEL), F32), jax.ShapeDtypeStruct((T_ALL, ROW_TILE, LANES), BF16),
                   jax.ShapeDtypeStruct((8, T_ALL), jnp.int32), jax.ShapeDtypeStruct((T_ALL, LANES), F32),
                   jax.ShapeDtypeStruct((N_ROW_BLOCKS, 1, LANES), F32),
                   jax.ShapeDtypeStruct((DEC_BATCH, B_KV_HEADS, B_HEAD_DIM, WINDOW), F32),
                   jax.ShapeDtypeStruct((DEC_BATCH, B_KV_HEADS, B_HEAD_DIM, WINDOW), F32),
                   jax.ShapeDtypeStruct((T_SAMPLE, A_WIDTH), F32), jax.ShapeDtypeStruct((1, LANES), F32)],
        scratch_shapes=[pltpu.VMEM((T_SAMPLE, Q_WIDTH), F32), pltpu.VMEM((T_SAMPLE, KV_WIDTH), F32),
                        pltpu.VMEM((T_SAMPLE, KV_WIDTH), F32), pltpu.VMEM((T_SAMPLE, D_MODEL), F32)],
        input_output_aliases={n_in: 0, n_in + 1: 1, n_in + 2: 2, n_in + 3: 3, n_in + 4: 4},
        compiler_params=_cparams(("arbitrary",)),
        name="mix0_sample",
    )(x_all, nm, win, lng, lnb, wcoef, bcoef, ck, cv, bias_sc, bias_sn, wout, nf, wr, br, cnt,
      x1_all, h_all, ri_all, rg_all, tc_all)


def _moe_metadata(rt_all, cnt, tcnt):
    counts = cnt[0, :N_EXPERTS].astype(jnp.int32)
    padded = (counts + MOE_BLK - 1) // MOE_BLK * MOE_BLK
    pad_end = jnp.cumsum(padded)
    pad_start = pad_end - padded
    experts = jnp.arange(N_EXPERTS, dtype=jnp.int32)
    n_valid = (pad_end[-1] // MOE_BLK).astype(jnp.int32).reshape(1)
    blk_start = jnp.arange(N_MOE_BLOCKS, dtype=jnp.int32) * MOE_BLK
    block_e = jnp.minimum(jnp.sum((blk_start[:, None] >= pad_end[None, :]).astype(jnp.int32), axis=1),
                          N_EXPERTS - 1).astype(jnp.int32)
    zero_start = (pad_start + counts).astype(jnp.int32)
    zero_len = (padded - counts).astype(jnp.int32)
    first = (blk_start == pad_start[block_e]).astype(jnp.int32)
    used = counts > 0
    parity = ((jnp.cumsum(used.astype(jnp.int32)) - 1) % 2)[block_e].astype(jnp.int32)
    nearest = lax.cummin(jnp.where(used, experts, N_EXPERTS)[::-1])[::-1]
    next_used = jnp.concatenate([nearest[1:], jnp.full((1,), N_EXPERTS, jnp.int32)])
    nxt = jnp.where(next_used < N_EXPERTS, next_used, -1)[block_e].astype(jnp.int32)
    plan = (block_e, first, parity, nxt, n_valid)
    runs = tcnt[:, 0, :N_EXPERTS].astype(jnp.int32)
    run_dst = pad_start[None, :] + jnp.cumsum(runs, axis=0) - runs
    lpos = rt_all[2 * TOP_K:3 * TOP_K].reshape(N_SLOTS).astype(jnp.int32)
    cplan = (lpos, runs.reshape(-1), run_dst.reshape(-1).astype(jnp.int32))
    gruns = runs.reshape(N_DISPATCH_STEPS, DISPATCH_TILES, N_EXPERTS)
    gtot = jnp.sum(gruns, axis=1)
    gstart = jnp.cumsum(gtot, axis=1) - gtot
    shift = ((gstart[:, None, :] + jnp.cumsum(gruns, axis=1) - gruns).reshape(N_ROW_BLOCKS, N_EXPERTS)
             - (jnp.cumsum(runs, axis=1) - runs))
    shift_rows = jnp.repeat(jnp.transpose(shift), TM, axis=1)
    hit = rt_all[:TOP_K, None, :] == experts[None, :, None]
    gpos = (rt_all[2 * TOP_K:3 * TOP_K] + jnp.sum(jnp.where(hit, shift_rows[None], 0), axis=1)).reshape(N_SLOTS)
    grun_dst = pad_start[None, :] + jnp.cumsum(gtot, axis=0) - gtot
    dplan = (gpos.astype(jnp.int32), gtot.reshape(-1), grun_dst.reshape(-1).astype(jnp.int32),
             jnp.concatenate([zero_start, zero_len, n_valid]))
    return plan, dplan, cplan


RUN_PIECE = 32
DISPATCH_TILES = 3
DISPATCH_ROWS = DISPATCH_TILES * TM
N_DISPATCH_STEPS = N_ROW_BLOCKS // DISPATCH_TILES


def _for_run_pieces(n, start_piece):
    whole = n // RUN_PIECE

    def body(j, carry):
        start_piece(j * RUN_PIECE, RUN_PIECE)
        return carry

    lax.fori_loop(0, whole, body, 0)
    o = whole * RUN_PIECE
    bit = RUN_PIECE // 2
    while bit >= 1:
        take = (n & bit) != 0

        @pl.when(take)
        def _(o=o, bit=bit):
            start_piece(o, bit)

        o = o + jnp.where(take, bit, 0)
        bit //= 2


def _dispatch_kernel(lpos_ref, run_ref, rdst_ref, zs_ref, h_ref, xs_ref, zero_scr, stage, sem, zsem):
    i = pl.program_id(0)

    @pl.when(i == 0)
    def _():
        zero_scr[...] = jnp.zeros_like(zero_scr)

        def pieces(e, do):
            off = zs_ref[e]
            rem = zs_ref[N_EXPERTS + e]
            bit = MOE_BLK // 2
            while bit >= 1:
                take = (rem & bit) != 0

                @pl.when(take)
                def _(off=off, bit=bit):
                    do(pltpu.make_async_copy(zero_scr.at[pl.ds(0, bit)], xs_ref.at[pl.ds(off, bit)], zsem))

                off = off + jnp.where(take, bit, 0)
                bit //= 2

        def start_e(e, c):
            pieces(e, lambda cp: cp.start())
            return c

        def wait_e(e, c):
            pieces(e, lambda cp: cp.wait())
            return c

        def tail(do):
            def step(b, c):
                do(pltpu.make_async_copy(zero_scr, xs_ref.at[pl.ds(b * MOE_BLK, MOE_BLK)], zsem))
                return c
            return step

        n_valid = zs_ref[2 * N_EXPERTS]
        lax.fori_loop(0, N_EXPERTS, start_e, 0)
        lax.fori_loop(n_valid, N_MOE_BLOCKS, tail(lambda cp: cp.start()), 0)
        lax.fori_loop(0, N_EXPERTS, wait_e, 0)
        lax.fori_loop(n_valid, N_MOE_BLOCKS, tail(lambda cp: cp.wait()), 0)

    base = i * DISPATCH_ROWS
    slot = i % 2

    def place(r, carry):
        row = h_ref[r]
        for kk in range(TOP_K):
            stage[slot, lpos_ref[kk * T_ALL + base + r]] = row
        return carry

    lax.fori_loop(0, DISPATCH_ROWS, place, 0, unroll=8)

    def send_run(e, off):
        n = run_ref[i * N_EXPERTS + e]
        dst = rdst_ref[i * N_EXPERTS + e]
        _for_run_pieces(n, lambda o, size: pltpu.make_async_copy(
            stage.at[slot, pl.ds(off + o, size)], xs_ref.at[pl.ds(dst + o, size)], sem.at[slot]).start(
                priority=size.bit_length() % 2))
        return off + n

    lax.fori_loop(0, N_EXPERTS, send_run, 0)

    def drain(s):
        pltpu.make_async_copy(stage.at[s], xs_ref.at[pl.ds(0, DISPATCH_ROWS * TOP_K)], sem.at[s]).wait()

    @pl.when(i >= 1)
    def _():
        drain(1 - slot)

    @pl.when(i == N_DISPATCH_STEPS - 1)
    def _():
        drain(slot)


def _dispatch(dplan, h_all):
    return pl.pallas_call(
        _dispatch_kernel,
        grid_spec=pltpu.PrefetchScalarGridSpec(
            num_scalar_prefetch=4,
            grid=(N_DISPATCH_STEPS,),
            in_specs=[pl.BlockSpec((DISPATCH_ROWS, ROW_TILE, LANES), lambda i, lp, rn, rd, z: (i, 0, 0))],
            out_specs=pl.BlockSpec(memory_space=pl.ANY),
            scratch_shapes=[pltpu.VMEM((MOE_BLK, ROW_TILE, LANES), BF16),
                            pltpu.VMEM((2, DISPATCH_ROWS * TOP_K, ROW_TILE, LANES), BF16),
                            pltpu.SemaphoreType.DMA((2,)), pltpu.SemaphoreType.DMA(())],
        ),
        out_shape=jax.ShapeDtypeStruct((N_SORT_ROWS, ROW_TILE, LANES), BF16),
        compiler_params=_cparams(("arbitrary",)),
        name="moe_dispatch",
    )(*dplan, h_all)


def _experts_kernel(layer, be_ref, first_ref, par_ref, nxt_ref, nv_ref,
                    x_ref, wg_hbm, wu_hbm, wd_hbm, y_ref,
                    wg_s, wu_s, wd_s, wg_f, wu_f, wd_f, wsem):
    i = pl.program_id(0)

    def fetch(e, slot):
        return (pltpu.make_async_copy(wg_hbm.at[layer, e], wg_f.at[slot], wsem.at[slot]),
                pltpu.make_async_copy(wu_hbm.at[layer, e], wu_f.at[slot], wsem.at[slot]),
                pltpu.make_async_copy(wd_hbm.at[layer, e], wd_f.at[slot], wsem.at[slot]))

    @pl.when(i < nv_ref[0])
    def _():
        e = be_ref[i]
        slot = par_ref[i]

        @pl.when(i == 0)
        def _():
            for cp in fetch(e, slot):
                cp.start()

        @pl.when(first_ref[i] == 1)
        def _():
            for cp in fetch(e, slot):
                cp.wait()
            wg_s[...] = wg_f[slot].astype(BF16)
            wu_s[...] = wu_f[slot].astype(BF16)
            wd_s[...] = wd_f[slot].astype(BF16)
            nxt = nxt_ref[i]

            @pl.when(nxt >= 0)
            def _():
                for cp in fetch(nxt, 1 - slot):
                    cp.start()

        xb = x_ref[...].reshape(MOE_BLK, D_MODEL)
        a = jax.nn.silu(_dot(xb, wg_s[...])) * _dot(xb, wu_s[...])
        y_ref[...] = _dot(a.astype(BF16), wd_s[...]).reshape(y_ref.shape)

    @pl.when(i >= nv_ref[0])
    def _():
        y_ref[...] = jnp.zeros(y_ref.shape, y_ref.dtype)


def _experts(block_e, first, parity, nxt, n_valid, xs, w_gate, w_up, w_down, layer):
    def blk(i, be, fi, pa, nx, nv):
        return (jnp.maximum(jnp.minimum(i, nv[0] - 1), 0), 0, 0)

    anyspec = pl.BlockSpec(memory_space=pl.ANY)
    return pl.pallas_call(
        functools.partial(_experts_kernel, layer),
        grid_spec=pltpu.PrefetchScalarGridSpec(
            num_scalar_prefetch=5,
            grid=(N_MOE_BLOCKS,),
            in_specs=[pl.BlockSpec((MOE_BLK, ROW_TILE, LANES), blk), anyspec, anyspec, anyspec],
            out_specs=pl.BlockSpec((MOE_BLK, ROW_TILE, LANES), lambda i, be, fi, pa, nx, nv: (i, 0, 0)),
            scratch_shapes=[pltpu.VMEM((D_MODEL, D_EXPERT), BF16), pltpu.VMEM((D_MODEL, D_EXPERT), BF16),
                            pltpu.VMEM((D_EXPERT, D_MODEL), BF16),
                            pltpu.VMEM((2, D_MODEL, D_EXPERT), F32), pltpu.VMEM((2, D_MODEL, D_EXPERT), F32),
                            pltpu.VMEM((2, D_EXPERT, D_MODEL), F32), pltpu.SemaphoreType.DMA((2,))],
        ),
        out_shape=jax.ShapeDtypeStruct((N_SORT_ROWS, ROW_TILE, LANES), F32),
        compiler_params=_cparams(("arbitrary",)),
        name="moe_experts",
    )(block_e, first, parity, nxt, n_valid, xs, w_gate, w_up, w_down)


def _gather_rows(lpos_ref, run_ref, rdst_ref, ys_ref, ystage, ybufs, sem, i):
    def fetch(tile, buf):
        def fetch_run(e, off):
            n = run_ref[tile * N_EXPERTS + e]
            src = rdst_ref[tile * N_EXPERTS + e]
            _for_run_pieces(n, lambda o, size: pltpu.make_async_copy(
                ys_ref.at[pl.ds(src + o, size)], ystage.at[buf, pl.ds(off + o, size)], sem.at[buf]).start(
                    priority=size.bit_length() % 2))
            return off + n

        lax.fori_loop(0, N_EXPERTS, fetch_run, 0)

    def wait(buf):
        pltpu.make_async_copy(ys_ref.at[pl.ds(0, TM * TOP_K)], ystage.at[buf], sem.at[buf]).wait()

    cur = i % 2

    @pl.when(i == 0)
    def _():
        fetch(i, 0)
        wait(0)

        def unplace(r, carry):
            for kk in range(TOP_K):
                ybufs[0][kk, r] = ystage[0, lpos_ref[kk * T_ALL + r]]
            return carry

        lax.fori_loop(0, TM, unplace, 0, unroll=8)
        fetch(i + 1, 1)

    @pl.when(i + 1 < N_ROW_BLOCKS)
    def _():
        wait(1 - cur)

    @pl.when(i + 2 < N_ROW_BLOCKS)
    def _():
        fetch(i + 2, cur)

    def pieces(compute, store):
        nxt = jnp.minimum(i + 1, N_ROW_BLOCKS - 1)

        def variant(par):
            ycur, ynext = ybufs[par], ybufs[1 - par]

            def piece(j, carry):
                rows = pl.ds(pl.multiple_of(j * COMBINE_ROWS, COMBINE_ROWS), COMBINE_ROWS)
                out = compute(rows, ycur[0, rows].reshape(COMBINE_ROWS, D_MODEL),
                              ycur[1, rows].reshape(COMBINE_ROWS, D_MODEL))
                base = nxt * TM + j * COMBINE_ROWS
                for r in range(COMBINE_ROWS):
                    for kk in range(TOP_K):
                        ynext[kk, j * COMBINE_ROWS + r] = ystage[1 - par, lpos_ref[kk * T_ALL + base + r]]
                store(rows, out)
                return carry

            lax.fori_loop(0, TM // COMBINE_ROWS, piece, 0)

        for par in range(2):
            @pl.when(cur == par)
            def _(par=par):
                variant(par)

    return pieces


COMBINE_ROWS = 64


def _combined(x_ref, rg_ref, rows, y0, y1):
    rg = rg_ref[rows, :]
    return x_ref[rows, :] + rg[:, 0:1] * y0 + rg[:, 1:2] * y1


_COMBINE_SCRATCH = [pltpu.VMEM((2, TM * TOP_K, ROW_TILE, LANES), F32),
                    pltpu.VMEM((TOP_K, TM, ROW_TILE, LANES), F32), pltpu.VMEM((TOP_K, TM, ROW_TILE, LANES), F32),
                    pltpu.SemaphoreType.DMA((2,))]


def _combine_kernel(lpos_ref, run_ref, rdst_ref, x_ref, rg_ref, ys_ref, o_ref, ystage, ybuf0, ybuf1, sem):
    pieces = _gather_rows(lpos_ref, run_ref, rdst_ref, ys_ref, ystage, (ybuf0, ybuf1), sem, pl.program_id(0))

    def store(rows, out):
        o_ref[rows, :] = out

    pieces(functools.partial(_combined, x_ref, rg_ref), store)


def _combine(cplan, x_all, rg_all, ys):
    return pl.pallas_call(
        _combine_kernel,
        grid_spec=pltpu.PrefetchScalarGridSpec(
            num_scalar_prefetch=3,
            grid=(N_ROW_BLOCKS,),
            in_specs=[pl.BlockSpec((TM, D_MODEL), lambda i, a, b, c: (i, 0)),
                      pl.BlockSpec((TM, LANES), lambda i, a, b, c: (i, 0)),
                      pl.BlockSpec(memory_space=pl.ANY)],
            out_specs=pl.BlockSpec((TM, D_MODEL), lambda i, a, b, c: (i, 0)),
            scratch_shapes=_COMBINE_SCRATCH,
        ),
        out_shape=jax.ShapeDtypeStruct((T_ALL, D_MODEL), F32),
        compiler_params=_cparams(("arbitrary",)),
        name="moe_combine",
    )(*cplan, x_all, rg_all, ys)


def _final_kernel(lpos_ref, run_ref, rdst_ref, x_ref, rg_ref, ys_ref, nfin_ref, op_ref, os_ref,
                  ystage, ybuf0, ybuf1, sem):
    i = pl.program_id(0)
    pieces = _gather_rows(lpos_ref, run_ref, rdst_ref, ys_ref, ystage, (ybuf0, ybuf1), sem, i)

    def compute(rows, y0, y1):
        return _rms(_combined(x_ref, rg_ref, rows, y0, y1), nfin_ref[...])

    def store(rows, y):
        @pl.when(i < N_PROMPT_BLOCKS)
        def _():
            op_ref[rows, :] = y

        @pl.when(i >= N_PROMPT_BLOCKS)
        def _():
            os_ref[rows, :] = y

    pieces(compute, store)


def _final(cplan, x_all, rg_all, ys, nfin):
    return pl.pallas_call(
        _final_kernel,
        grid_spec=pltpu.PrefetchScalarGridSpec(
            num_scalar_prefetch=3,
            grid=(N_ROW_BLOCKS,),
            in_specs=[pl.BlockSpec((TM, D_MODEL), lambda i, a, b, c: (i, 0)),
                      pl.BlockSpec((TM, LANES), lambda i, a, b, c: (i, 0)),
                      pl.BlockSpec(memory_space=pl.ANY),
                      pl.BlockSpec((1, D_MODEL), lambda i, a, b, c: (0, 0))],
            out_specs=[pl.BlockSpec((TM, D_MODEL), lambda i, a, b, c: (jnp.minimum(i, N_PROMPT_BLOCKS - 1), 0)),
                       pl.BlockSpec((TM, D_MODEL), lambda i, a, b, c: (0, 0))],
            scratch_shapes=_COMBINE_SCRATCH,
        ),
        out_shape=[jax.ShapeDtypeStruct((T_PROMPT, D_MODEL), F32), jax.ShapeDtypeStruct((T_SAMPLE, D_MODEL), F32)],
        compiler_params=_cparams(("arbitrary",)),
        name="moe_combine_final",
    )(*cplan, x_all, rg_all, ys, nfin)


def _moe(h_all, rt_all, cnt, tcnt, w_gate, w_up, w_down, layer):
    plan, dplan, cplan = _moe_metadata(rt_all, cnt, tcnt)
    xs = _dispatch(dplan, h_all)
    ys = _experts(*plan, xs, w_gate, w_up, w_down, layer)
    return cplan, ys


def _pool_project(d_groups, wp_ref, scale):
    outs = [_dot(d_groups[g].astype(BF16), wp_ref[g]) for g in range(len(POOL_SIZES))]
    return jnp.concatenate(outs, axis=1) * scale


def _mix1_prompt_kernel(x_ref, nm_ref, wp_ref, sc_ref, nf_ref, wr_ref, br_ref,
                        x3_ref, h_ref, ri_ref, rg_ref, tc_ref, pl_ref, cnt_ref, ext):
    i = pl.program_id(0)

    @pl.when(i == 0)
    def _():
        cnt_ref[...] = jnp.zeros_like(cnt_ref)

    x = x_ref[...]
    hp = _rms(x, nm_ref[...])

    @pl.when(i % STEPS_PER_BATCH == 0)
    def _():
        ext[0:POOL_MAX, :] = jnp.zeros((POOL_MAX, D_MODEL), F32)

    ext[POOL_MAX:, :] = hp
    pos = (i % STEPS_PER_BATCH) * TM + lax.broadcasted_iota(jnp.int32, (TM, 1), 0)
    d_groups = []
    for g, w in enumerate(POOL_SIZES):
        cols = slice(g * POOL_GROUP_DIM, (g + 1) * POOL_GROUP_DIM)
        acc = ext[:, cols]
        span = 1
        while span < w:
            acc = acc + pltpu.roll(acc, span, 0)
            span *= 2
        cnt = jnp.minimum(pos + 1, w).astype(F32)
        d_groups.append(acc[POOL_MAX:] / cnt - hp[:, cols])
    tail = hp[TM - POOL_MAX:, :]
    ext[0:POOL_MAX, :] = tail
    pl_ref[...] = tail

    x3 = x + _pool_project(d_groups, wp_ref, sc_ref[...])
    x3_ref[...] = x3
    h, ids, gates = _route(x3, nf_ref[...], wr_ref[...], br_ref[...])
    h_ref[...] = h.reshape(h_ref.shape)
    ri_ref[...] = _rank_pack(ids, cnt_ref, tc_ref)
    rg_ref[...] = gates


def _mix1_prompt(x_all, nm, wp, sc, nf, wr, br):
    row_spec = pl.BlockSpec((TM, D_MODEL), lambda i: (i, 0))
    row3_spec = pl.BlockSpec((TM, ROW_TILE, LANES), lambda i: (i, 0, 0))
    lane_spec = pl.BlockSpec((TM, LANES), lambda i: (i, 0))
    return pl.pallas_call(
        _prompt_steps(_mix1_prompt_kernel, 7),
        grid=(N_ROW_BLOCKS,),
        in_specs=[row_spec, _const_spec((1, D_MODEL)),
                  _const_spec((len(POOL_SIZES), POOL_GROUP_DIM, POOL_GROUP_DIM)), _const_spec((1, D_MODEL)),
                  _const_spec((1, D_MODEL)), _const_spec((D_MODEL, 2 * LANES)), _const_spec((1, LANES))],
        out_specs=[row_spec, row3_spec, pl.BlockSpec((8, TM), lambda i: (0, i)), lane_spec,
                   pl.BlockSpec((None, 1, LANES), lambda i: (i, 0, 0)),
                   pl.BlockSpec((None, POOL_MAX, D_MODEL),
                                lambda i: (jnp.minimum(i // STEPS_PER_BATCH, BATCH - 1), 0, 0)),
                   _const_spec((1, LANES))],
        out_shape=[jax.ShapeDtypeStruct((T_ALL, D_MODEL), F32), jax.ShapeDtypeStruct((T_ALL, ROW_TILE, LANES), BF16),
                   jax.ShapeDtypeStruct((8, T_ALL), jnp.int32), jax.ShapeDtypeStruct((T_ALL, LANES), F32),
                   jax.ShapeDtypeStruct((N_ROW_BLOCKS, 1, LANES), F32),
                   jax.ShapeDtypeStruct((BATCH, POOL_MAX, D_MODEL), F32), jax.ShapeDtypeStruct((1, LANES), F32)],
        scratch_shapes=[pltpu.VMEM((POOL_MAX + TM, D_MODEL), F32)],
        compiler_params=_cparams(("arbitrary",)),
        name="mix1_prompt",
    )(x_all, nm, wp, sc, nf, wr, br)


def _mix1_sample_kernel(x_ref, st_ref, nm_ref, wp_ref, sc_ref, nf_ref, wr_ref, br_ref, cnt_in,
                        x3_in, h_in, ri_in, rg_in, tc_in,
                        x3_ref, h_ref, ri_ref, rg_ref, tc_ref, hs_ref, cnt_ref):
    del x3_in, h_in, ri_in, rg_in, tc_in
    x = x_ref[...]
    hs = _rms(x, nm_ref[...])
    hs_ref[...] = hs
    n_ctx = POOL_MAX - 1
    d_groups = []
    for g, w in enumerate(POOL_SIZES):
        cols = slice(g * POOL_GROUP_DIM, (g + 1) * POOL_GROUP_DIM)
        parts = []
        for t in range(DEC_SEQ):
            acc = hs[t * DEC_BATCH:(t + 1) * DEC_BATCH, cols]
            for back in range(1, w):
                src = t - back
                if src >= 0:
                    acc = acc + hs[src * DEC_BATCH:(src + 1) * DEC_BATCH, cols]
                else:
                    acc = acc + st_ref[n_ctx + src, :, cols]
            parts.append(acc / float(w) - hs[t * DEC_BATCH:(t + 1) * DEC_BATCH, cols])
        d_groups.append(jnp.concatenate(parts, axis=0))
    x3 = x + _pool_project(d_groups, wp_ref, sc_ref[...])
    x3_ref[...] = x3
    h, ids, gates = _route(x3, nf_ref[...], wr_ref[...], br_ref[...])
    h_ref[...] = h.reshape(h_ref.shape)
    cnt_ref[...] = cnt_in[...]
    ri_ref[...] = _rank_pack(ids, cnt_ref, tc_ref)
    rg_ref[...] = gates


def _mix1_sample(x_all, state_t, nm, wp, sc, nf, wr, br, cnt, x3_all, h_all, ri_all, rg_all, tc_all):
    sample_rows = pl.BlockSpec((TM, D_MODEL), lambda g: (N_PROMPT_BLOCKS, 0))
    sample_rows3 = pl.BlockSpec((TM, ROW_TILE, LANES), lambda g: (N_PROMPT_BLOCKS, 0, 0))
    sample_lanes = pl.BlockSpec((TM, LANES), lambda g: (N_PROMPT_BLOCKS, 0))
    anyspec = pl.BlockSpec(memory_space=pl.ANY)
    n_in = 9
    return pl.pallas_call(
        _mix1_sample_kernel,
        grid=(1,),
        in_specs=[sample_rows, _const_spec((POOL_MAX - 1, DEC_BATCH, D_MODEL)), _const_spec((1, D_MODEL)),
                  _const_spec((len(POOL_SIZES), POOL_GROUP_DIM, POOL_GROUP_DIM)), _const_spec((1, D_MODEL)),
                  _const_spec((1, D_MODEL)), _const_spec((D_MODEL, 2 * LANES)), _const_spec((1, LANES)),
                  _const_spec((1, LANES)), anyspec, anyspec, anyspec, anyspec, anyspec],
        out_specs=[sample_rows, sample_rows3, pl.BlockSpec((8, TM), lambda g: (0, N_PROMPT_BLOCKS)), sample_lanes,
                   pl.BlockSpec((None, 1, LANES), lambda g: (N_PROMPT_BLOCKS, 0, 0)),
                   _const_spec((T_SAMPLE, D_MODEL)), _const_spec((1, LANES))],
        out_shape=[jax.ShapeDtypeStruct((T_ALL, D_MODEL), F32), jax.ShapeDtypeStruct((T_ALL, ROW_TILE, LANES), BF16),
                   jax.ShapeDtypeStruct((8, T_ALL), jnp.int32), jax.ShapeDtypeStruct((T_ALL, LANES), F32),
                   jax.ShapeDtypeStruct((N_ROW_BLOCKS, 1, LANES), F32),
                   jax.ShapeDtypeStruct((T_SAMPLE, D_MODEL), F32), jax.ShapeDtypeStruct((1, LANES), F32)],
        input_output_aliases={n_in: 0, n_in + 1: 1, n_in + 2: 2, n_in + 3: 3, n_in + 4: 4},
        compiler_params=_cparams(("arbitrary",)),
        name="mix1_sample",
    )(x_all, state_t, nm, wp, sc, nf, wr, br, cnt, x3_all, h_all, ri_all, rg_all, tc_all)


def _router_weights(wg, bg, we, be):
    w = jnp.concatenate([wg, jnp.transpose(we, (1, 0, 2)).reshape(D_MODEL, N_EXPERTS)], axis=1)
    b = jnp.concatenate([bg, be.reshape(N_EXPERTS)])
    pad = LANES - N_GROUPS - N_EXPERTS
    w = jnp.pad(w, ((0, 0), (0, pad)))
    w_hi = w.astype(BF16)
    w_lo = (w - w_hi.astype(F32)).astype(BF16)
    return jnp.concatenate([w_hi, w_lo], axis=1), jnp.pad(b, (0, pad)).reshape(1, LANES)


def kernel(x_prompt, x_sample, cache_k_win, cache_v_win, state_pool, norm_mix, norm_ffn, norm_final, w_in,
           a_ln_g, a_ln_b, a_w_s, a_b_s, b_sinks, rel_bias_table, w_out, c_w_pool, c_scale,
           router_group_w, router_group_b, router_expert_w, router_expert_b, w_gate, w_up, w_down):
    xs_t = jnp.transpose(x_sample, (1, 0, 2)).reshape(T_SAMPLE, D_MODEL)
    xp2 = x_prompt.reshape(T_PROMPT, D_MODEL)
    win =w_in[0].astype(BF16)
    wout = w_out[0].astype(BF16)
    lng = a_ln_g[0].reshape(1, A_WIDTH)
    lnb = a_ln_b[0].reshape(1, A_WIDTH)
    bias_p, bias_sc, bias_sn, wsp = _prep(rel_bias_table, b_sinks[0], a_w_s[0])
    bs_full = jnp.repeat(a_b_s[0].T, A_HEAD_DIM, axis=1)
    w4 = jnp.transpose(a_w_s[0][:, :DEC_SEQ, :DEC_SEQ], (1, 2, 0)).reshape(DEC_SEQ * DEC_SEQ, A_HEADS)
    wcoef = jnp.repeat(w4, A_HEAD_DIM, axis=1)
    bcoef = jnp.pad(jnp.repeat(a_b_s[0][:, :DEC_SEQ].T, A_HEAD_DIM, axis=1), ((0, 8 - DEC_SEQ), (0, 0)))
    ck = jnp.transpose(cache_k_win[0], (0, 2, 3, 1))
    cv = jnp.transpose(cache_v_win[0], (0, 2, 3, 1))
    routers = [_router_weights(router_group_w[l], router_group_b[l], router_expert_w[l], router_expert_b[l])
               for l in range(2)]
    nm = [norm_mix[l].reshape(1, D_MODEL) for l in range(2)]
    nf = [norm_ffn[l].reshape(1, D_MODEL) for l in range(2)]

    x1_all, h_all, ri_all, rg_all, tc_all, k_last, v_last, va_last, cnt0 = _mix0_prompt(
        xp2, nm[0], win, lng, lnb, wsp, bs_full, bias_p, wout, nf[0], *routers[0])
    x1_all, h_all, ri_all, rg_all, tc_all, k_new, v_new, va_s, cnt0 = _mix0_sample(
        xs_t, nm[0], win, lng, lnb, wcoef, bcoef, ck, cv, bias_sc, bias_sn, wout, nf[0], *routers[0], cnt0,
        x1_all, h_all, ri_all, rg_all, tc_all)
    cplan0, ys0 = _moe(h_all, ri_all, cnt0, tc_all, w_gate, w_up, w_down, 0)
    x2_all = _combine(cplan0, x1_all, rg_all, ys0)

    wp = c_w_pool[0].astype(BF16)
    sc = c_scale[0].reshape(1, D_MODEL)
    x3_all, h2_all, ri2_all, rg2_all, tc2_all, pool_tail, cnt1 = _mix1_prompt(
        x2_all, nm[1], wp, sc, nf[1], *routers[1])
    state_t = jnp.transpose(state_pool[0], (1, 0, 2))
    x3_all, h2_all, ri2_all, rg2_all, tc2_all, hs1, cnt1 = _mix1_sample(
        x2_all, state_t, nm[1], wp, sc, nf[1], *routers[1], cnt1, x3_all, h2_all, ri2_all, rg2_all, tc2_all)
    cplan1, ys1 = _moe(h2_all, ri2_all, cnt1, tc2_all, w_gate, w_up, w_down, 1)
    y_p, y_s = _final(cplan1, x3_all, rg2_all, ys1, norm_final.reshape(1, D_MODEL))

    def from_tmajor(a, width):
        return jnp.transpose(a.reshape(DEC_SEQ, DEC_BATCH, width), (1, 0, 2))

    y_prompt = y_p.reshape(BATCH, SEQ, D_MODEL)
    y_sample = from_tmajor(y_s, D_MODEL)
    win_k_p = k_last.reshape(1, BATCH, WINDOW, B_KV_HEADS, B_HEAD_DIM)
    win_v_p = v_last.reshape(1, BATCH, WINDOW, B_KV_HEADS, B_HEAD_DIM)
    win_k_s = jnp.transpose(k_new, (0, 3, 1, 2))[None]
    win_v_s = jnp.transpose(v_new, (0, 3, 1, 2))[None]
    chunk_v_p = va_last.reshape(1, BATCH, CHUNK, A_HEADS, A_HEAD_DIM)
    chunk_v_s = from_tmajor(va_s, A_WIDTH).reshape(1, DEC_BATCH, DEC_SEQ, A_HEADS, A_HEAD_DIM)
    pool_p = pool_tail[:, 1:][None]
    pool_s = jnp.concatenate([state_pool[0][:, DEC_SEQ:], from_tmajor(hs1, D_MODEL)], axis=1)[None]
    return (y_prompt, y_sample, win_k_p, win_v_p, win_k_s, win_v_s, chunk_v_p, chunk_v_s, pool_p, pool_s)
```

```python
import functools
import math

import numpy as np
import jax
import jax.numpy as jnp
from jax import lax
from jax.experimental import pallas as pl
from jax.experimental.pallas import tpu as pltpu

F32 = jnp.float32
BF16 = jnp.bfloat16

D_MODEL = 1024
BATCH = 2
SEQ = 8192
DEC_BATCH = 128
DEC_SEQ = 4
A_WIDTH = 512
A_HEADS = 8
A_HEAD_DIM = 64
CHUNK = 128
B_HEADS = 8
B_KV_HEADS = 2
B_HEAD_DIM = 64
B_GROUP = 4
WINDOW = 128
N_BUCKETS = 32
MAX_DISTANCE = WINDOW
Q_WIDTH = 512
KV_WIDTH = 128
IN_WIDTH = 2 * A_WIDTH + Q_WIDTH + 2 * KV_WIDTH
ATTN_SCALE = B_HEAD_DIM ** -0.5
NEG_INF = -1e30
POOL_SIZES = (2, 4, 8, 16)
POOL_GROUP_DIM = 256
POOL_MAX = 16
N_GROUPS = 4
EXPERTS_PER_GROUP = 8
N_EXPERTS = 32
TOP_K = 2
D_EXPERT = 512
EPS = 1e-6

LANES = 128
ROW_TILE = D_MODEL // LANES
T_PROMPT = BATCH * SEQ
T_SAMPLE = DEC_BATCH * DEC_SEQ
T_ALL = T_PROMPT + T_SAMPLE
TM = 512
N_PROMPT_BLOCKS = T_PROMPT // TM
N_ROW_BLOCKS = T_ALL // TM
STEPS_PER_BATCH = SEQ // TM
SUB = TM // WINDOW
N_SLOTS = T_ALL * TOP_K
MOE_BLK = 512
N_MOE_BLOCKS = N_SLOTS // MOE_BLK + N_EXPERTS
N_SORT_ROWS = N_MOE_BLOCKS * MOE_BLK
SAMPLE_GROUP = 8
N_SAMPLE_GROUPS = DEC_BATCH // SAMPLE_GROUP
VMEM_LIMIT = 56 * 1024 * 1024

STACK_HEADS = ((0, 2, 5, 7), (1, 3, 4, 6))


def _t5_bucket_np(dist):
    n = np.maximum(dist, 0)
    max_exact = N_BUCKETS // 2
    nf = np.maximum(n, 1).astype(np.float32)
    large = max_exact + (np.log(nf / np.float32(max_exact)) / np.float32(math.log(MAX_DISTANCE / max_exact))
                         * np.float32(N_BUCKETS - max_exact)).astype(np.int32)
    large = np.minimum(large, N_BUCKETS - 1)
    return np.where(n < max_exact, n, large).astype(np.int32)


def _bucket_tables():
    qi = np.arange(WINDOW)[:, None]
    ki = np.arange(2 * WINDOW)[None, :]
    dist = qi + WINDOW - ki
    valid = (dist >= 0) & (dist < WINDOW)
    bp = np.where(valid, _t5_bucket_np(dist), -1)
    bp_first = np.where(ki >= WINDOW, bp, -1)
    bkt_p = np.stack([bp_first, bp]).astype(np.int32)

    t = np.repeat(np.arange(DEC_SEQ), SAMPLE_GROUP)[:, None]
    b = np.tile(np.arange(SAMPLE_GROUP), DEC_SEQ)[:, None]
    cb = np.repeat(np.arange(SAMPLE_GROUP), WINDOW)[None, :]
    cj = np.tile(np.arange(WINDOW), SAMPLE_GROUP)[None, :]
    dist_c = t + WINDOW - cj
    valid_c = (cb == b) & (dist_c >= 0) & (dist_c < WINDOW)
    bkt_sc = np.where(valid_c, _t5_bucket_np(dist_c), -1).astype(np.int32)
    nt = np.repeat(np.arange(DEC_SEQ), SAMPLE_GROUP)[None, :]
    nb = np.tile(np.arange(SAMPLE_GROUP), DEC_SEQ)[None, :]
    dist_n = t - nt
    valid_n = (nb == b) & (dist_n >= 0)
    bkt_sn = np.where(valid_n, _t5_bucket_np(dist_n), -1).astype(np.int32)
    bkt_sn = np.concatenate([bkt_sn, np.full((32, LANES - 32), -1, np.int32)], axis=1)
    return bkt_p, bkt_sc, bkt_sn


_BKT_P, _BKT_SC, _BKT_SN = _bucket_tables()


def _cparams(semantics):
    return pltpu.CompilerParams(dimension_semantics=semantics, vmem_limit_bytes=VMEM_LIMIT)


def _rms(x, g):
    return x * lax.rsqrt(jnp.mean(x * x, axis=-1, keepdims=True) + EPS) * g


def _layernorm(x, g, b):
    xc = x - jnp.mean(x, axis=-1, keepdims=True)
    return xc * lax.rsqrt(jnp.mean(xc * xc, axis=-1, keepdims=True) + EPS) * g + b


def _dot(a, b):
    return jnp.dot(a, b, preferred_element_type=F32)


def _dot_nt(a, b):
    return lax.dot_general(a, b, (((1,), (1,)), ((), ())), preferred_element_type=F32)


def _project(x, nm, win, lng, lnb):
    h = _rms(x, nm)
    z = _dot(h.astype(BF16), win)
    u = jax.nn.gelu(z[:, :A_WIDTH])
    va = _layernorm(jax.nn.gelu(z[:, A_WIDTH:2 * A_WIDTH]), lng, lnb)
    q = z[:, 2 * A_WIDTH:2 * A_WIDTH + Q_WIDTH] * ATTN_SCALE
    k = z[:, 2 * A_WIDTH + Q_WIDTH:2 * A_WIDTH + Q_WIDTH + KV_WIDTH]
    v = z[:, 2 * A_WIDTH + Q_WIDTH + KV_WIDTH:]
    return u, va, q, k, v


def _route(x1, nf, wr, br):
    hf = _rms(x1, nf)
    h = hf.astype(BF16)
    h_lo = (hf - h.astype(F32)).astype(BF16)
    part = _dot(h, wr)
    logits = part[:, :LANES] + part[:, LANES:] + _dot(h_lo, wr[:, :LANES]) + br
    rows = logits.shape[0]
    lane = lax.broadcasted_iota(jnp.int32, (rows, LANES), 1)
    lanef = lane.astype(F32)
    big = jnp.float32(1e9)
    is_g = lane < N_GROUPS
    gl = jnp.where(is_g, logits, -jnp.inf)
    gmax = jnp.max(gl, axis=1, keepdims=True)
    gsel = jnp.min(jnp.where(gl == gmax, lanef, big), axis=1, keepdims=True)
    gsum = jnp.sum(jnp.where(is_g, jnp.exp(logits - gmax), 0.0), axis=1, keepdims=True)
    g1 = 1.0 / gsum
    lo = N_GROUPS + EXPERTS_PER_GROUP * gsel
    emask = (lanef >= lo) & (lanef < lo + EXPERTS_PER_GROUP)
    el = jnp.where(emask, logits, -jnp.inf)
    v1 = jnp.max(el, axis=1, keepdims=True)
    i1 = jnp.min(jnp.where(el == v1, lanef, big), axis=1, keepdims=True)
    el2 = jnp.where(lanef == i1, -jnp.inf, el)
    v2 = jnp.max(el2, axis=1, keepdims=True)
    i2 = jnp.min(jnp.where(el2 == v2, lanef, big), axis=1, keepdims=True)
    e2 = jnp.exp(v2 - v1)
    den = 1.0 + e2
    w1 = g1 / den
    w2 = g1 * e2 / den
    ids = jnp.where(lane == 0, i1 - N_GROUPS, jnp.where(lane == 1, i2 - N_GROUPS, 0.0)).astype(jnp.int32)
    gates = jnp.where(lane == 0, w1, jnp.where(lane == 1, w2, 0.0))
    return h, ids, gates


def _rank_pack(ids, cnt_ref, tcnt_ref):
    rows = ids.shape[0]
    lane = lax.broadcasted_iota(jnp.int32, (rows, LANES), 1)
    o0 = (lane == ids[:, 0:1]).astype(F32)
    o1 = (lane == ids[:, 1:2]).astype(F32)
    r = lax.broadcasted_iota(jnp.int32, (rows, rows), 0)
    c = lax.broadcasted_iota(jnp.int32, (rows, rows), 1)
    before = (c < r).astype(BF16)
    p01 = _dot(before, jnp.concatenate([o0, o1], axis=1).astype(BF16))
    p0 = p01[:, :LANES]
    p1 = p01[:, LANES:]
    c0 = jnp.sum(o0, axis=0, keepdims=True)
    c1 = jnp.sum(o1, axis=0, keepdims=True)
    ctile = c0 + c1
    cnt_ref[...] = cnt_ref[...] + ctile
    tcnt_ref[...] = ctile
    inc = jnp.broadcast_to(ctile, (8, LANES))
    lane8 = lax.broadcasted_iota(jnp.int32, (8, LANES), 1)
    for sh in (1, 2, 4, 8, 16, 32, 64):
        inc = inc + jnp.where(lane8 >= sh, pltpu.roll(inc, sh, 1), 0.0)
    start = inc[0:1] - ctile
    lpos0 = jnp.sum(o0 * (start + p0), axis=1, keepdims=True)
    lpos1 = jnp.sum(o1 * (start + c0 + p1), axis=1, keepdims=True)
    idf = ids.astype(F32)
    packed = jnp.where(lane < TOP_K, idf, 0.0)
    for ln, col in ((4, lpos0), (5, lpos1)):
        packed = jnp.where(lane == ln, col, packed)
    return jnp.transpose(packed)[:8].astype(jnp.int32)


def _prep_kernel(tab_ref, sink_ref, bp_ref, bsc_ref, bsn_ref, ws_ref, op_ref, osc_ref, osn_ref, ows_ref):
    def fill(bkt, write, sink_col0):
        col0 = lax.broadcasted_iota(jnp.int32, bkt.shape, 1) == 0
        for st, heads in enumerate(STACK_HEADS):
            for slot, h in enumerate(heads):
                acc = jnp.full(bkt.shape, NEG_INF, F32)
                for b in range(N_BUCKETS):
                    acc = jnp.where(bkt == b, tab_ref[b, h], acc)
                if sink_col0:
                    acc = jnp.where(col0, sink_ref[0, h], acc)
                write(st, slot, acc)

    for var in range(2):
        def wr_p(st, slot, acc, var=var):
            op_ref[var, st, slot * WINDOW:(slot + 1) * WINDOW, :] = acc
        fill(bp_ref[var], wr_p, True)

    rows_s = DEC_SEQ * SAMPLE_GROUP

    def wr_sc(st, slot, acc):
        osc_ref[st, slot * rows_s:(slot + 1) * rows_s, :] = acc
    fill(bsc_ref[...], wr_sc, True)

    def wr_sn(st, slot, acc):
        osn_ref[st, slot * rows_s:(slot + 1) * rows_s, :] = acc
    fill(bsn_ref[...], wr_sn, False)

    r = lax.broadcasted_iota(jnp.int32, (CHUNK, CHUNK), 0)
    c = lax.broadcasted_iota(jnp.int32, (CHUNK, CHUNK), 1)
    for h in range(A_HEADS):
        ows_ref[h // 2, :, (h % 2) * CHUNK:(h % 2 + 1) * CHUNK] = jnp.where(r >= c, ws_ref[h], 0.0).astype(BF16)


def _prep(rel_bias_table, sinks, w_s):
    vm = pl.BlockSpec(memory_space=pltpu.VMEM)
    sm = pl.BlockSpec(memory_space=pltpu.SMEM)
    rows_s = DEC_SEQ * SAMPLE_GROUP
    return pl.pallas_call(
        _prep_kernel,
        in_specs=[sm, sm, vm, vm, vm, vm],
        out_specs=[vm, vm, vm, vm],
        out_shape=[
            jax.ShapeDtypeStruct((2, 2, 4 * WINDOW, 2 * WINDOW), F32),
            jax.ShapeDtypeStruct((2, 4 * rows_s, SAMPLE_GROUP * WINDOW), F32),
            jax.ShapeDtypeStruct((2, 4 * rows_s, LANES), F32),
            jax.ShapeDtypeStruct((A_HEADS // 2, CHUNK, 2 * CHUNK), BF16),
        ],
        name="prep_tables",
    )(rel_bias_table, sinks.reshape(1, B_HEADS), jnp.asarray(_BKT_P), jnp.asarray(_BKT_SC), jnp.asarray(_BKT_SN), w_s)


def _gate_pairs(va_rows, wsp_ref, lane_lo):
    outs = []
    for p in range(A_HEADS // 2):
        vp = va_rows[:, p * LANES:(p + 1) * LANES]
        rhs = jnp.concatenate([jnp.where(lane_lo, vp, 0.0), jnp.where(lane_lo, 0.0, vp)], axis=0).astype(BF16)
        outs.append(_dot(wsp_ref[p], rhs))
    return jnp.concatenate(outs, axis=1)


def _prompt_steps(body, first_row_out):
    def kern(*refs):
        i = pl.program_id(0)

        @pl.when(i < N_PROMPT_BLOCKS)
        def _():
            body(*refs)

        @pl.when(i >= N_PROMPT_BLOCKS)
        def _():
            for r in refs[first_row_out:first_row_out + 5]:
                r[...] = jnp.zeros(r.shape, r.dtype)

    return kern


def _mix0_prompt_kernel(x_ref, nm_ref, win_ref, lng_ref, lnb_ref, wsp_ref, bs_ref, bias_ref,
                        wout_ref, nf_ref, wr_ref, br_ref,
                        x1_ref, h_ref, ri_ref, rg_ref, tc_ref, kl_ref, vl_ref, val_ref, cnt_ref,
                        kprev, vprev, mix_scr):
    @pl.when(pl.program_id(0) == 0)
    def _():
        cnt_ref[...] = jnp.zeros_like(cnt_ref)

    x = x_ref[...]
    u, va, q, k, v = _project(x, nm_ref[...], win_ref[...], lng_ref[...], lnb_ref[...])
    lane_lo = lax.broadcasted_iota(jnp.int32, (WINDOW, LANES), 1) < B_HEAD_DIM
    row0 = lax.broadcasted_iota(jnp.int32, (WINDOW, KV_WIDTH), 0) == 0
    first = pl.program_id(0) % STEPS_PER_BATCH == 0

    @pl.when(first)
    def _():
        kprev[...] = jnp.zeros_like(kprev)
        vprev[...] = jnp.zeros_like(vprev)

    for j in range(SUB):
        rows = slice(j * WINDOW, (j + 1) * WINDOW)
        s_gate = _gate_pairs(va[rows], wsp_ref, lane_lo)
        mix_scr[rows, :A_WIDTH] = u[rows] * (s_gate + bs_ref[...])

        if j == 0:
            kp, vp = kprev[...], vprev[...]
        else:
            prows = slice((j - 1) * WINDOW, j * WINDOW)
            kp, vp = k[prows], v[prows]
        kk = jnp.concatenate([jnp.where(row0, 0.0, kp), k[rows]], axis=0)
        vv = jnp.concatenate([jnp.where(row0, 0.0, vp), v[rows]], axis=0)
        kops = (kk.astype(BF16), pltpu.roll(kk, B_HEAD_DIM, 1).astype(BF16))
        vops = (vv.astype(BF16), pltpu.roll(vv, B_HEAD_DIM, 1).astype(BF16))
        qt = [q[rows, p * LANES:(p + 1) * LANES] for p in range(4)]
        q_even = [jnp.where(lane_lo, t, 0.0) for t in qt]
        q_odd = [jnp.where(lane_lo, 0.0, t) for t in qt]
        stacks = (jnp.concatenate([q_even[0], q_even[1], q_odd[2], q_odd[3]], axis=0),
                  jnp.concatenate([q_odd[0], q_odd[1], q_even[2], q_even[3]], axis=0))
        o = []
        for st in range(2):
            s = _dot_nt(stacks[st].astype(BF16), kops[st])
            if j == 0:
                bias = bias_ref[jnp.where(first, 0, 1), st]
            else:
                bias = bias_ref[1, st]
            s = s + bias
            m = jnp.max(s, axis=-1, keepdims=True)
            p = jnp.exp(s - m)
            den = jnp.sum(p, axis=-1, keepdims=True)
            o.append(_dot(p.astype(BF16), vops[st]) / den)
        oa, ob = o
        sl = [slice(i * WINDOW, (i + 1) * WINDOW) for i in range(4)]
        tiles = (jnp.where(lane_lo, oa[sl[0]], ob[sl[0]]), jnp.where(lane_lo, oa[sl[1]], ob[sl[1]]),
                 jnp.where(lane_lo, ob[sl[2]], oa[sl[2]]), jnp.where(lane_lo, ob[sl[3]], oa[sl[3]]))
        for p in range(4):
            mix_scr[rows, A_WIDTH + p * LANES:A_WIDTH + (p + 1) * LANES] = tiles[p]

    last = slice(TM - WINDOW, TM)
    kprev[...] = k[last]
    vprev[...] = v[last]
    kl_ref[...] = k[last]
    vl_ref[...] = v[last]
    val_ref[...] = va[last]

    x1 = x + _dot(mix_scr[...].astype(BF16), wout_ref[...])
    x1_ref[...] = x1
    h, ids, gates = _route(x1, nf_ref[...], wr_ref[...], br_ref[...])
    h_ref[...] = h.reshape(h_ref.shape)
    ri_ref[...] = _rank_pack(ids, cnt_ref, tc_ref)
    rg_ref[...] = gates


def _const_spec(shape):
    nd = len(shape)
    return pl.BlockSpec(shape, lambda i, _n=nd: (0,) * _n)


def _mix0_prompt(x_all, nm, win, lng, lnb, wsp, bs_full, bias_p, wout, nf, wr, br):
    row_spec = pl.BlockSpec((TM, D_MODEL), lambda i: (i, 0))
    row3_spec = pl.BlockSpec((TM, ROW_TILE, LANES), lambda i: (i, 0, 0))
    lane_spec = pl.BlockSpec((TM, LANES), lambda i: (i, 0))
    last_kv = pl.BlockSpec((None, WINDOW, KV_WIDTH), lambda i: (jnp.minimum(i // STEPS_PER_BATCH, BATCH - 1), 0, 0))
    last_va = pl.BlockSpec((None, WINDOW, A_WIDTH), lambda i: (jnp.minimum(i // STEPS_PER_BATCH, BATCH - 1), 0, 0))
    return pl.pallas_call(
        _prompt_steps(_mix0_prompt_kernel, 12),
        grid=(N_ROW_BLOCKS,),
        in_specs=[pl.BlockSpec((TM, D_MODEL), lambda i: (jnp.minimum(i, N_PROMPT_BLOCKS - 1), 0)),
                  _const_spec((1, D_MODEL)), _const_spec((D_MODEL, IN_WIDTH)),
                  _const_spec((1, A_WIDTH)), _const_spec((1, A_WIDTH)),
                  _const_spec((A_HEADS // 2, CHUNK, 2 * CHUNK)), _const_spec((CHUNK, A_WIDTH)),
                  _const_spec((2, 2, 4 * WINDOW, 2 * WINDOW)),
                  _const_spec((A_WIDTH + Q_WIDTH, D_MODEL)), _const_spec((1, D_MODEL)),
                  _const_spec((D_MODEL, 2 * LANES)), _const_spec((1, LANES))],
        out_specs=[row_spec, row3_spec, pl.BlockSpec((8, TM), lambda i: (0, i)), lane_spec,
                   pl.BlockSpec((None, 1, LANES), lambda i: (i, 0, 0)),
                   last_kv, last_kv, last_va, _const_spec((1, LANES))],
        out_shape=[jax.ShapeDtypeStruct((T_ALL, D_MODEL), F32), jax.ShapeDtypeStruct((T_ALL, ROW_TILE, LANES), BF16),
                   jax.ShapeDtypeStruct((8, T_ALL), jnp.int32), jax.ShapeDtypeStruct((T_ALL, LANES), F32),
                   jax.ShapeDtypeStruct((N_ROW_BLOCKS, 1, LANES), F32),
                   jax.ShapeDtypeStruct((BATCH, WINDOW, KV_WIDTH), F32),
                   jax.ShapeDtypeStruct((BATCH, WINDOW, KV_WIDTH), F32),
                   jax.ShapeDtypeStruct((BATCH, WINDOW, A_WIDTH), F32),
                   jax.ShapeDtypeStruct((1, LANES), F32)],
        scratch_shapes=[pltpu.VMEM((WINDOW, KV_WIDTH), F32), pltpu.VMEM((WINDOW, KV_WIDTH), F32),
                        pltpu.VMEM((TM, D_MODEL), F32)],
        compiler_params=_cparams(("arbitrary",)),
        name="mix0_prompt",
    )(x_all, nm, win, lng, lnb, wsp, bs_full, bias_p, wout, nf, wr, br)


def _mix0_sample_kernel(x_ref, nm_ref, win_ref, lng_ref, lnb_ref, wcoef_ref, bcoef_ref,
                        ck_ref, cv_ref, bsc_ref, bsn_ref,
                        wout_ref, nf_ref, wr_ref, br_ref, cnt_in,
                        x1_in, h_in, ri_in, rg_in, tc_in,
                        x1_ref, h_ref, ri_ref, rg_ref, tc_ref, kn_ref, vn_ref, va_ref, cnt_ref,
                        q_scr, k_scr, v_scr, mix_scr):
    del x1_in, h_in, ri_in, rg_in, tc_in
    g = pl.program_id(0)

    @pl.when(g == 0)
    def _():
        u, va, q, k, v = _project(x_ref[...], nm_ref[...], win_ref[...], lng_ref[...], lnb_ref[...])
        q_scr[...] = q
        k_scr[...] = k
        v_scr[...] = v
        va_ref[...] = va
        for t in range(DEC_SEQ):
            acc = jnp.zeros((DEC_BATCH, A_WIDTH), F32) + bcoef_ref[t:t + 1, :]
            for s in range(t + 1):
                row = t * DEC_SEQ + s
                acc = acc + wcoef_ref[row:row + 1, :] * va[s * DEC_BATCH:(s + 1) * DEC_BATCH]
            mix_scr[t * DEC_BATCH:(t + 1) * DEC_BATCH, :A_WIDTH] = u[t * DEC_BATCH:(t + 1) * DEC_BATCH] * acc

    b0 = pl.multiple_of(g * SAMPLE_GROUP, SAMPLE_GROUP)
    lane_lo = lax.broadcasted_iota(jnp.int32, (DEC_SEQ * SAMPLE_GROUP, LANES), 1) < B_HEAD_DIM

    def grab(ref, width):
        return jnp.concatenate([ref[pl.ds(t * DEC_BATCH + b0, SAMPLE_GROUP), :] for t in range(DEC_SEQ)], axis=0)

    qg = grab(q_scr, Q_WIDTH)
    kn = grab(k_scr, KV_WIDTH)
    vn = grab(v_scr, KV_WIDTH)

    lane_w = lax.broadcasted_iota(jnp.int32, (KV_WIDTH, WINDOW), 1)
    n_new = DEC_SEQ * SAMPLE_GROUP

    def new_window(c_ref, new_rows, w_ref):
        nt = jnp.transpose(jnp.concatenate([new_rows, jnp.zeros((WINDOW - n_new, KV_WIDTH), F32)], axis=0))
        for b in range(SAMPLE_GROUP):
            w = pltpu.roll(c_ref[b].reshape(KV_WIDTH, WINDOW), WINDOW - DEC_SEQ, 1)
            for t in range(DEC_SEQ):
                src = t * SAMPLE_GROUP + b
                dst = WINDOW - DEC_SEQ + t
                w = jnp.where(lane_w == dst, pltpu.roll(nt, (dst - src) % WINDOW, 1), w)
            w_ref[b] = w.reshape(B_KV_HEADS, B_HEAD_DIM, WINDOW)

    new_window(ck_ref, kn, kn_ref)
    new_window(cv_ref, vn, vn_ref)
    ccol0 = lax.broadcasted_iota(jnp.int32, (KV_WIDTH, SAMPLE_GROUP * WINDOW), 1) == 0

    def cache_t(ref):
        t = jnp.concatenate([ref[b].reshape(KV_WIDTH, WINDOW) for b in range(SAMPLE_GROUP)], axis=1)
        return jnp.where(ccol0, 0.0, t)

    def head_swap(t):
        return jnp.concatenate([t[B_HEAD_DIM:], t[:B_HEAD_DIM]], axis=0)

    kct = cache_t(ck_ref)
    vct = cache_t(cv_ref)
    kc_ops = (kct.astype(BF16), head_swap(kct).astype(BF16))
    vc_ops = (vct.astype(BF16), head_swap(vct).astype(BF16))
    kn_ops = (kn.astype(BF16), pltpu.roll(kn, B_HEAD_DIM, 1).astype(BF16))
    vn_ops = (vn.astype(BF16), pltpu.roll(vn, B_HEAD_DIM, 1).astype(BF16))
    qt = [qg[:, p * LANES:(p + 1) * LANES] for p in range(4)]
    q_even = [jnp.where(lane_lo, t, 0.0) for t in qt]
    q_odd = [jnp.where(lane_lo, 0.0, t) for t in qt]
    stacks = (jnp.concatenate([q_even[0], q_even[1], q_odd[2], q_odd[3]], axis=0),
              jnp.concatenate([q_odd[0], q_odd[1], q_even[2], q_even[3]], axis=0))
    o = []
    for st in range(2):
        qs = stacks[st].astype(BF16)
        sc = _dot(qs, kc_ops[st]) + bsc_ref[st]
        sn = _dot_nt(qs, kn_ops[st]) + bsn_ref[st][:, :DEC_SEQ * SAMPLE_GROUP]
        m = jnp.maximum(jnp.max(sc, axis=-1, keepdims=True), jnp.max(sn, axis=-1, keepdims=True))
        pc = jnp.exp(sc - m)
        pn = jnp.exp(sn - m)
        den = jnp.sum(pc, axis=-1, keepdims=True) + jnp.sum(pn, axis=-1, keepdims=True)
        o.append((_dot_nt(pc.astype(BF16), vc_ops[st]) + _dot(pn.astype(BF16), vn_ops[st])) / den)
    oa, ob = o
    n = DEC_SEQ * SAMPLE_GROUP
    sl = [slice(i * n, (i + 1) * n) for i in range(4)]
    tiles = (jnp.where(lane_lo, oa[sl[0]], ob[sl[0]]), jnp.where(lane_lo, oa[sl[1]], ob[sl[1]]),
             jnp.where(lane_lo, ob[sl[2]], oa[sl[2]]), jnp.where(lane_lo, ob[sl[3]], oa[sl[3]]))
    for p in range(4):
        for t in range(DEC_SEQ):
            mix_scr[pl.ds(t * DEC_BATCH + b0, SAMPLE_GROUP), A_WIDTH + p * LANES:A_WIDTH + (p + 1) * LANES] = (
                tiles[p][t * SAMPLE_GROUP:(t + 1) * SAMPLE_GROUP])

    @pl.when(g == N_SAMPLE_GROUPS - 1)
    def _():
        x1 = x_ref[...] + _dot(mix_scr[...].astype(BF16), wout_ref[...])
        x1_ref[...] = x1
        h, ids, gates = _route(x1, nf_ref[...], wr_ref[...], br_ref[...])
        h_ref[...] = h.reshape(h_ref.shape)
        cnt_ref[...] = cnt_in[...]
        ri_ref[...] = _rank_pack(ids, cnt_ref, tc_ref)
        rg_ref[...] = gates


def _mix0_sample(x_all, nm, win, lng, lnb, wcoef, bcoef, ck, cv, bias_sc, bias_sn, wout, nf, wr, br, cnt,
                 x1_all, h_all, ri_all, rg_all, tc_all):
    sample_rows = pl.BlockSpec((TM, D_MODEL), lambda g: (N_PROMPT_BLOCKS, 0))
    sample_rows3 = pl.BlockSpec((TM, ROW_TILE, LANES), lambda g: (N_PROMPT_BLOCKS, 0, 0))
    sample_lanes = pl.BlockSpec((TM, LANES), lambda g: (N_PROMPT_BLOCKS, 0))
    cache_spec = pl.BlockSpec((SAMPLE_GROUP, B_KV_HEADS, B_HEAD_DIM, WINDOW), lambda g: (g, 0, 0, 0))
    anyspec = pl.BlockSpec(memory_space=pl.ANY)
    n_in = 16
    return pl.pallas_call(
        _mix0_sample_kernel,
        grid=(N_SAMPLE_GROUPS,),
        in_specs=[_const_spec((TM, D_MODEL)), _const_spec((1, D_MODEL)), _const_spec((D_MODEL, IN_WIDTH)),
                  _const_spec((1, A_WIDTH)), _const_spec((1, A_WIDTH)),
                  _const_spec((16, A_WIDTH)), _const_spec((8, A_WIDTH)),
                  cache_spec, cache_spec,
                  _const_spec((2, 4 * 32, SAMPLE_GROUP * WINDOW)), _const_spec((2, 4 * 32, LANES)),
                  _const_spec((A_WIDTH + Q_WIDTH, D_MODEL)), _const_spec((1, D_MODEL)),
                  _const_spec((D_MODEL, 2 * LANES)), _const_spec((1, LANES)), _const_spec((1, LANES)),
                  anyspec, anyspec, anyspec, anyspec, anyspec],
        out_specs=[sample_rows, sample_rows3, pl.BlockSpec((8, TM), lambda g: (0, N_PROMPT_BLOCKS)), sample_lanes,
                   pl.BlockSpec((None, 1, LANES), lambda g: (N_PROMPT_BLOCKS, 0, 0)),
                   cache_spec, cache_spec,
                   _const_spec((T_SAMPLE, A_WIDTH)), _const_spec((1, LANES))],
        out_shape=[jax.ShapeDtypeStruct((T_ALL, D_MODEL), F32), jax.ShapeDtypeStruct((T_ALL, ROW_TILE, LANES), BF16),
                   jax.ShapeDtypeStruct((8, T_ALL), jnp.int32), jax.ShapeDtypeStruct((T_ALL, LANES), F32),
                   jax.ShapeDtypeStruct((N_ROW_BLOCKS, 1, LANES), F32),
                   jax.ShapeDtypeStruct((DEC_BATCH, B_KV_HEADS, B_HEAD_DIM, WINDOW), F32),
                   jax.ShapeDtypeStruct((DEC_BATCH, B_KV_HEADS, B_HEAD_DIM, WINDOW), F32),
                   jax.ShapeDtypeStruct((T_SAMPLE, A_WIDTH), F32), jax.ShapeDtypeStruct((1, LANES), F32)],
        scratch_shapes=[pltpu.VMEM((T_SAMPLE, Q_WIDTH), F32), pltpu.VMEM((T_SAMPLE, KV_WIDTH), F32),
                        pltpu.VMEM((T_SAMPLE, KV_WIDTH), F32), pltpu.VMEM((T_SAMPLE, D_MODEL), F32)],
        input_output_aliases={n_in: 0, n_in + 1: 1, n_in + 2: 2, n_in + 3: 3, n_in + 4: 4},
        compiler_params=_cparams(("arbitrary",)),
        name="mix0_sample",
    )(x_all, nm, win, lng, lnb, wcoef, bcoef, ck, cv, bias_sc, bias_sn, wout, nf, wr, br, cnt,
      x1_all, h_all, ri_all, rg_all, tc_all)


def _moe_metadata(rt_all, cnt, tcnt):
    counts = cnt[0, :N_EXPERTS].astype(jnp.int32)
    padded = (counts + MOE_BLK - 1) // MOE_BLK * MOE_BLK
    pad_end = jnp.cumsum(padded)
    pad_start = pad_end - padded
    experts = jnp.arange(N_EXPERTS, dtype=jnp.int32)
    n_valid = (pad_end[-1] // MOE_BLK).astype(jnp.int32).reshape(1)
    blk_start = jnp.arange(N_MOE_BLOCKS, dtype=jnp.int32) * MOE_BLK
    block_e = jnp.minimum(jnp.sum((blk_start[:, None] >= pad_end[None, :]).astype(jnp.int32), axis=1),
                          N_EXPERTS - 1).astype(jnp.int32)
    zero_start = (pad_start + counts).astype(jnp.int32)
    zero_len = (padded - counts).astype(jnp.int32)
    first = (blk_start == pad_start[block_e]).astype(jnp.int32)
    used = counts > 0
    parity = ((jnp.cumsum(used.astype(jnp.int32)) - 1) % 2)[block_e].astype(jnp.int32)
    nearest = lax.cummin(jnp.where(used, experts, N_EXPERTS)[::-1])[::-1]
    next_used = jnp.concatenate([nearest[1:], jnp.full((1,), N_EXPERTS, jnp.int32)])
    nxt = jnp.where(next_used < N_EXPERTS, next_used, -1)[block_e].astype(jnp.int32)
    plan = (block_e, first, parity, nxt, n_valid)
    runs = tcnt[:, 0, :N_EXPERTS].astype(jnp.int32)
    gruns = runs.reshape(N_DISPATCH_STEPS, DISPATCH_TILES, N_EXPERTS)
    gtot = jnp.sum(gruns, axis=1)
    gstart = jnp.cumsum(gtot, axis=1) - gtot
    shift = ((gstart[:, None, :] + jnp.cumsum(gruns, axis=1) - gruns).reshape(N_ROW_BLOCKS, N_EXPERTS)
             - (jnp.cumsum(runs, axis=1) - runs))
    shift_rows = jnp.repeat(jnp.transpose(shift), TM, axis=1)
    hit = rt_all[:TOP_K, None, :] == experts[None, :, None]
    gpos = (rt_all[2 * TOP_K:3 * TOP_K] + jnp.sum(jnp.where(hit, shift_rows[None], 0), axis=1)).reshape(N_SLOTS)
    grun_dst = pad_start[None, :] + jnp.cumsum(gtot, axis=0) - gtot
    cplan = (gpos.astype(jnp.int32), gtot.reshape(-1), grun_dst.reshape(-1).astype(jnp.int32))
    dplan = cplan + (jnp.concatenate([zero_start, zero_len, n_valid]),)
    return plan, dplan, cplan


RUN_PIECE = 32
DISPATCH_TILES = 3
DISPATCH_ROWS = DISPATCH_TILES * TM
N_DISPATCH_STEPS = N_ROW_BLOCKS // DISPATCH_TILES


def _for_run_pieces(n, start_piece):
    whole = n // RUN_PIECE

    def body(j, carry):
        start_piece(j * RUN_PIECE, RUN_PIECE)
        return carry

    lax.fori_loop(0, whole, body, 0)
    o = whole * RUN_PIECE
    bit = RUN_PIECE // 2
    while bit >= 1:
        take = (n & bit) != 0

        @pl.when(take)
        def _(o=o, bit=bit):
            start_piece(o, bit)

        o = o + jnp.where(take, bit, 0)
        bit //= 2


def _dispatch_kernel(lpos_ref, run_ref, rdst_ref, zs_ref, h_ref, xs_ref, zero_scr, stage, sem, zsem):
    i = pl.program_id(0)

    @pl.when(i == 0)
    def _():
        zero_scr[...] = jnp.zeros_like(zero_scr)

        def pieces(e, do):
            off = zs_ref[e]
            rem = zs_ref[N_EXPERTS + e]
            bit = MOE_BLK // 2
            while bit >= 1:
                take = (rem & bit) != 0

                @pl.when(take)
                def _(off=off, bit=bit):
                    do(pltpu.make_async_copy(zero_scr.at[pl.ds(0, bit)], xs_ref.at[pl.ds(off, bit)], zsem))

                off = off + jnp.where(take, bit, 0)
                bit //= 2

        def start_e(e, c):
            pieces(e, lambda cp: cp.start())
            return c

        def wait_e(e, c):
            pieces(e, lambda cp: cp.wait())
            return c

        def tail(do):
            def step(b, c):
                do(pltpu.make_async_copy(zero_scr, xs_ref.at[pl.ds(b * MOE_BLK, MOE_BLK)], zsem))
                return c
            return step

        n_valid = zs_ref[2 * N_EXPERTS]
        lax.fori_loop(0, N_EXPERTS, start_e, 0)
        lax.fori_loop(n_valid, N_MOE_BLOCKS, tail(lambda cp: cp.start()), 0)
        lax.fori_loop(0, N_EXPERTS, wait_e, 0)
        lax.fori_loop(n_valid, N_MOE_BLOCKS, tail(lambda cp: cp.wait()), 0)

    base = i * DISPATCH_ROWS
    slot = i % 2

    def place(r, carry):
        row = h_ref[r]
        for kk in range(TOP_K):
            stage[slot, lpos_ref[kk * T_ALL + base + r]] = row
        return carry

    lax.fori_loop(0, DISPATCH_ROWS, place, 0, unroll=8)

    def send_run(e, off):
        n = run_ref[i * N_EXPERTS + e]
        dst = rdst_ref[i * N_EXPERTS + e]
        _for_run_pieces(n, lambda o, size: pltpu.make_async_copy(
            stage.at[slot, pl.ds(off + o, size)], xs_ref.at[pl.ds(dst + o, size)], sem.at[slot]).start(
                priority=size.bit_length() % 2))
        return off + n

    lax.fori_loop(0, N_EXPERTS, send_run, 0)

    def drain(s):
        pltpu.make_async_copy(stage.at[s], xs_ref.at[pl.ds(0, DISPATCH_ROWS * TOP_K)], sem.at[s]).wait()

    @pl.when(i >= 1)
    def _():
        drain(1 - slot)

    @pl.when(i == N_DISPATCH_STEPS - 1)
    def _():
        drain(slot)


def _dispatch(dplan, h_all):
    return pl.pallas_call(
        _dispatch_kernel,
        grid_spec=pltpu.PrefetchScalarGridSpec(
            num_scalar_prefetch=4,
            grid=(N_DISPATCH_STEPS,),
            in_specs=[pl.BlockSpec((DISPATCH_ROWS, ROW_TILE, LANES), lambda i, lp, rn, rd, z: (i, 0, 0))],
            out_specs=pl.BlockSpec(memory_space=pl.ANY),
            scratch_shapes=[pltpu.VMEM((MOE_BLK, ROW_TILE, LANES), BF16),
                            pltpu.VMEM((2, DISPATCH_ROWS * TOP_K, ROW_TILE, LANES), BF16),
                            pltpu.SemaphoreType.DMA((2,)), pltpu.SemaphoreType.DMA(())],
        ),
        out_shape=jax.ShapeDtypeStruct((N_SORT_ROWS, ROW_TILE, LANES), BF16),
        compiler_params=_cparams(("arbitrary",)),
        name="moe_dispatch",
    )(*dplan, h_all)


def _experts_kernel(layer, be_ref, first_ref, par_ref, nxt_ref, nv_ref,
                    x_ref, wg_hbm, wu_hbm, wd_hbm, y_ref,
                    wg_s, wu_s, wd_s, wg_f, wu_f, wd_f, wsem):
    i = pl.program_id(0)

    def fetch(e, slot):
        return (pltpu.make_async_copy(wg_hbm.at[layer, e], wg_f.at[slot], wsem.at[slot]),
                pltpu.make_async_copy(wu_hbm.at[layer, e], wu_f.at[slot], wsem.at[slot]),
                pltpu.make_async_copy(wd_hbm.at[layer, e], wd_f.at[slot], wsem.at[slot]))

    @pl.when(i < nv_ref[0])
    def _():
        e = be_ref[i]
        slot = par_ref[i]

        @pl.when(i == 0)
        def _():
            for cp in fetch(e, slot):
                cp.start()

        @pl.when(first_ref[i] == 1)
        def _():
            for cp in fetch(e, slot):
                cp.wait()
            wg_s[...] = wg_f[slot].astype(BF16)
            wu_s[...] = wu_f[slot].astype(BF16)
            wd_s[...] = wd_f[slot].astype(BF16)
            nxt = nxt_ref[i]

            @pl.when(nxt >= 0)
            def _():
                for cp in fetch(nxt, 1 - slot):
                    cp.start()

        xb = x_ref[...].reshape(MOE_BLK, D_MODEL)
        a = jax.nn.silu(_dot(xb, wg_s[...])) * _dot(xb, wu_s[...])
        y_ref[...] = _dot(a.astype(BF16), wd_s[...]).reshape(y_ref.shape)

    @pl.when(i >= nv_ref[0])
    def _():
        y_ref[...] = jnp.zeros(y_ref.shape, y_ref.dtype)


def _experts(block_e, first, parity, nxt, n_valid, xs, w_gate, w_up, w_down, layer):
    def blk(i, be, fi, pa, nx, nv):
        return (jnp.maximum(jnp.minimum(i, nv[0] - 1), 0), 0, 0)

    anyspec = pl.BlockSpec(memory_space=pl.ANY)
    return pl.pallas_call(
        functools.partial(_experts_kernel, layer),
        grid_spec=pltpu.PrefetchScalarGridSpec(
            num_scalar_prefetch=5,
            grid=(N_MOE_BLOCKS,),
            in_specs=[pl.BlockSpec((MOE_BLK, ROW_TILE, LANES), blk), anyspec, anyspec, anyspec],
            out_specs=pl.BlockSpec((MOE_BLK, ROW_TILE, LANES), lambda i, be, fi, pa, nx, nv: (i, 0, 0)),
            scratch_shapes=[pltpu.VMEM((D_MODEL, D_EXPERT), BF16), pltpu.VMEM((D_MODEL, D_EXPERT), BF16),
                            pltpu.VMEM((D_EXPERT, D_MODEL), BF16),
                            pltpu.VMEM((2, D_MODEL, D_EXPERT), F32), pltpu.VMEM((2, D_MODEL, D_EXPERT), F32),
                            pltpu.VMEM((2, D_EXPERT, D_MODEL), F32), pltpu.SemaphoreType.DMA((2,))],
        ),
        out_shape=jax.ShapeDtypeStruct((N_SORT_ROWS, ROW_TILE, LANES), F32),
        compiler_params=_cparams(("arbitrary",)),
        name="moe_experts",
    )(block_e, first, parity, nxt, n_valid, xs, w_gate, w_up, w_down)


def _gather_rows(lpos_ref, run_ref, rdst_ref, ys_ref, ystage, ybufs, sem, i):
    def fetch(group, buf):
        def fetch_run(e, off):
            n = run_ref[group * N_EXPERTS + e]
            src = rdst_ref[group * N_EXPERTS + e]
            _for_run_pieces(n, lambda o, size: pltpu.make_async_copy(
                ys_ref.at[pl.ds(src + o, size)], ystage.at[buf, pl.ds(off + o, size)], sem.at[buf]).start(
                    priority=size.bit_length() % 2))
            return off + n

        lax.fori_loop(0, N_EXPERTS, fetch_run, 0)

    def wait(buf):
        pltpu.make_async_copy(ys_ref.at[pl.ds(0, DISPATCH_ROWS * TOP_K)], ystage.at[buf], sem.at[buf]).wait()

    cur = i % 2
    group = i // DISPATCH_TILES
    phase = i % DISPATCH_TILES
    last_of_group = phase == DISPATCH_TILES - 1

    @pl.when(i == 0)
    def _():
        fetch(0, 0)
        wait(0)

        def unplace(r, carry):
            for kk in range(TOP_K):
                ybufs[0][kk, r] = ystage[0, lpos_ref[kk * T_ALL + r]]
            return carry

        lax.fori_loop(0, TM, unplace, 0, unroll=8)
        fetch(1, 1)

    @pl.when(last_of_group & (group + 1 < N_DISPATCH_STEPS))
    def _():
        wait((group + 1) % 2)

    @pl.when(last_of_group & (group + 2 < N_DISPATCH_STEPS))
    def _():
        fetch(group + 2, group % 2)

    def pieces(compute, store):
        nxt = jnp.minimum(i + 1, N_ROW_BLOCKS - 1)
        nslot = (nxt // DISPATCH_TILES) % 2

        def variant(par):
            ycur, ynext = ybufs[par], ybufs[1 - par]

            def piece(j, carry):
                rows = pl.ds(pl.multiple_of(j * COMBINE_ROWS, COMBINE_ROWS), COMBINE_ROWS)
                out = compute(rows, ycur[0, rows].reshape(COMBINE_ROWS, D_MODEL),
                              ycur[1, rows].reshape(COMBINE_ROWS, D_MODEL))
                base = nxt * TM + j * COMBINE_ROWS
                for r in range(COMBINE_ROWS):
                    for kk in range(TOP_K):
                        ynext[kk, j * COMBINE_ROWS + r] = ystage[nslot, lpos_ref[kk * T_ALL + base + r]]
                store(rows, out)
                return carry

            lax.fori_loop(0, TM // COMBINE_ROWS, piece, 0)

        for par in range(2):
            @pl.when(cur == par)
            def _(par=par):
                variant(par)

    return pieces


COMBINE_ROWS = 64


def _combined(x_ref, rg_ref, rows, y0, y1):
    rg = rg_ref[rows, :]
    return x_ref[rows, :] + rg[:, 0:1] * y0 + rg[:, 1:2] * y1


_COMBINE_SCRATCH = [pltpu.VMEM((2, DISPATCH_ROWS * TOP_K, ROW_TILE, LANES), F32),
                    pltpu.VMEM((TOP_K, TM, ROW_TILE, LANES), F32), pltpu.VMEM((TOP_K, TM, ROW_TILE, LANES), F32),
                    pltpu.SemaphoreType.DMA((2,))]


def _combine_kernel(lpos_ref, run_ref, rdst_ref, x_ref, rg_ref, ys_ref, o_ref, ystage, ybuf0, ybuf1, sem):
    pieces = _gather_rows(lpos_ref, run_ref, rdst_ref, ys_ref, ystage, (ybuf0, ybuf1), sem, pl.program_id(0))

    def store(rows, out):
        o_ref[rows, :] = out

    pieces(functools.partial(_combined, x_ref, rg_ref), store)


def _combine(cplan, x_all, rg_all, ys):
    return pl.pallas_call(
        _combine_kernel,
        grid_spec=pltpu.PrefetchScalarGridSpec(
            num_scalar_prefetch=3,
            grid=(N_ROW_BLOCKS,),
            in_specs=[pl.BlockSpec((TM, D_MODEL), lambda i, a, b, c: (i, 0)),
                      pl.BlockSpec((TM, LANES), lambda i, a, b, c: (i, 0)),
                      pl.BlockSpec(memory_space=pl.ANY)],
            out_specs=pl.BlockSpec((TM, D_MODEL), lambda i, a, b, c: (i, 0)),
            scratch_shapes=_COMBINE_SCRATCH,
        ),
        out_shape=jax.ShapeDtypeStruct((T_ALL, D_MODEL), F32),
        compiler_params=_cparams(("arbitrary",)),
        name="moe_combine",
    )(*cplan, x_all, rg_all, ys)


def _final_kernel(lpos_ref, run_ref, rdst_ref, x_ref, rg_ref, ys_ref, nfin_ref, op_ref, os_ref,
                  ystage, ybuf0, ybuf1, sem):
    i = pl.program_id(0)
    pieces = _gather_rows(lpos_ref, run_ref, rdst_ref, ys_ref, ystage, (ybuf0, ybuf1), sem, i)

    def compute(rows, y0, y1):
        return _rms(_combined(x_ref, rg_ref, rows, y0, y1), nfin_ref[...])

    def store(rows, y):
        @pl.when(i < N_PROMPT_BLOCKS)
        def _():
            op_ref[rows, :] = y

        @pl.when(i >= N_PROMPT_BLOCKS)
        def _():
            os_ref[rows, :] = y

    pieces(compute, store)


def _final(cplan, x_all, rg_all, ys, nfin):
    return pl.pallas_call(
        _final_kernel,
        grid_spec=pltpu.PrefetchScalarGridSpec(
            num_scalar_prefetch=3,
            grid=(N_ROW_BLOCKS,),
            in_specs=[pl.BlockSpec((TM, D_MODEL), lambda i, a, b, c: (i, 0)),
                      pl.BlockSpec((TM, LANES), lambda i, a, b, c: (i, 0)),
                      pl.BlockSpec(memory_space=pl.ANY),
                      pl.BlockSpec((1, D_MODEL), lambda i, a, b, c: (0, 0))],
            out_specs=[pl.BlockSpec((TM, D_MODEL), lambda i, a, b, c: (jnp.minimum(i, N_PROMPT_BLOCKS - 1), 0)),
                       pl.BlockSpec((TM, D_MODEL), lambda i, a, b, c: (0, 0))],
            scratch_shapes=_COMBINE_SCRATCH,
        ),
        out_shape=[jax.ShapeDtypeStruct((T_PROMPT, D_MODEL), F32), jax.ShapeDtypeStruct((T_SAMPLE, D_MODEL), F32)],
        compiler_params=_cparams(("arbitrary",)),
        name="moe_combine_final",
    )(*cplan, x_all, rg_all, ys, nfin)


def _moe(h_all, rt_all, cnt, tcnt, w_gate, w_up, w_down, layer):
    plan, dplan, cplan = _moe_metadata(rt_all, cnt, tcnt)
    xs = _dispatch(dplan, h_all)
    ys = _experts(*plan, xs, w_gate, w_up, w_down, layer)
    return cplan, ys


def _pool_project(d_groups, wp_ref, scale):
    outs = [_dot(d_groups[g].astype(BF16), wp_ref[g]) for g in range(len(POOL_SIZES))]
    return jnp.concatenate(outs, axis=1) * scale


def _mix1_prompt_kernel(x_ref, nm_ref, wp_ref, sc_ref, nf_ref, wr_ref, br_ref,
                        x3_ref, h_ref, ri_ref, rg_ref, tc_ref, pl_ref, cnt_ref, ext):
    i = pl.program_id(0)

    @pl.when(i == 0)
    def _():
        cnt_ref[...] = jnp.zeros_like(cnt_ref)

    x = x_ref[...]
    hp = _rms(x, nm_ref[...])

    @pl.when(i % STEPS_PER_BATCH == 0)
    def _():
        ext[0:POOL_MAX, :] = jnp.zeros((POOL_MAX, D_MODEL), F32)

    ext[POOL_MAX:, :] = hp
    pos = (i % STEPS_PER_BATCH) * TM + lax.broadcasted_iota(jnp.int32, (TM, 1), 0)
    d_groups = []
    for g, w in enumerate(POOL_SIZES):
        cols = slice(g * POOL_GROUP_DIM, (g + 1) * POOL_GROUP_DIM)
        acc = ext[:, cols]
        span = 1
        while span < w:
            acc = acc + pltpu.roll(acc, span, 0)
            span *= 2
        cnt = jnp.minimum(pos + 1, w).astype(F32)
        d_groups.append(acc[POOL_MAX:] / cnt - hp[:, cols])
    tail = hp[TM - POOL_MAX:, :]
    ext[0:POOL_MAX, :] = tail
    pl_ref[...] = tail

    x3 = x + _pool_project(d_groups, wp_ref, sc_ref[...])
    x3_ref[...] = x3
    h, ids, gates = _route(x3, nf_ref[...], wr_ref[...], br_ref[...])
    h_ref[...] = h.reshape(h_ref.shape)
    ri_ref[...] = _rank_pack(ids, cnt_ref, tc_ref)
    rg_ref[...] = gates


def _mix1_prompt(x_all, nm, wp, sc, nf, wr, br):
    row_spec = pl.BlockSpec((TM, D_MODEL), lambda i: (i, 0))
    row3_spec = pl.BlockSpec((TM, ROW_TILE, LANES), lambda i: (i, 0, 0))
    lane_spec = pl.BlockSpec((TM, LANES), lambda i: (i, 0))
    return pl.pallas_call(
        _prompt_steps(_mix1_prompt_kernel, 7),
        grid=(N_ROW_BLOCKS,),
        in_specs=[row_spec, _const_spec((1, D_MODEL)),
                  _const_spec((len(POOL_SIZES), POOL_GROUP_DIM, POOL_GROUP_DIM)), _const_spec((1, D_MODEL)),
                  _const_spec((1, D_MODEL)), _const_spec((D_MODEL, 2 * LANES)), _const_spec((1, LANES))],
        out_specs=[row_spec, row3_spec, pl.BlockSpec((8, TM), lambda i: (0, i)), lane_spec,
                   pl.BlockSpec((None, 1, LANES), lambda i: (i, 0, 0)),
                   pl.BlockSpec((None, POOL_MAX, D_MODEL),
                                lambda i: (jnp.minimum(i // STEPS_PER_BATCH, BATCH - 1), 0, 0)),
                   _const_spec((1, LANES))],
        out_shape=[jax.ShapeDtypeStruct((T_ALL, D_MODEL), F32), jax.ShapeDtypeStruct((T_ALL, ROW_TILE, LANES), BF16),
                   jax.ShapeDtypeStruct((8, T_ALL), jnp.int32), jax.ShapeDtypeStruct((T_ALL, LANES), F32),
                   jax.ShapeDtypeStruct((N_ROW_BLOCKS, 1, LANES), F32),
                   jax.ShapeDtypeStruct((BATCH, POOL_MAX, D_MODEL), F32), jax.ShapeDtypeStruct((1, LANES), F32)],
        scratch_shapes=[pltpu.VMEM((POOL_MAX + TM, D_MODEL), F32)],
        compiler_params=_cparams(("arbitrary",)),
        name="mix1_prompt",
    )(x_all, nm, wp, sc, nf, wr, br)


def _mix1_sample_kernel(x_ref, st_ref, nm_ref, wp_ref, sc_ref, nf_ref, wr_ref, br_ref, cnt_in,
                        x3_in, h_in, ri_in, rg_in, tc_in,
                        x3_ref, h_ref, ri_ref, rg_ref, tc_ref, hs_ref, cnt_ref):
    del x3_in, h_in, ri_in, rg_in, tc_in
    x = x_ref[...]
    hs = _rms(x, nm_ref[...])
    hs_ref[...] = hs
    n_ctx = POOL_MAX - 1
    d_groups = []
    for g, w in enumerate(POOL_SIZES):
        cols = slice(g * POOL_GROUP_DIM, (g + 1) * POOL_GROUP_DIM)
        parts = []
        for t in range(DEC_SEQ):
            acc = hs[t * DEC_BATCH:(t + 1) * DEC_BATCH, cols]
            for back in range(1, w):
                src = t - back
                if src >= 0:
                    acc = acc + hs[src * DEC_BATCH:(src + 1) * DEC_BATCH, cols]
                else:
                    acc = acc + st_ref[n_ctx + src, :, cols]
            parts.append(acc / float(w) - hs[t * DEC_BATCH:(t + 1) * DEC_BATCH, cols])
        d_groups.append(jnp.concatenate(parts, axis=0))
    x3 = x + _pool_project(d_groups, wp_ref, sc_ref[...])
    x3_ref[...] = x3
    h, ids, gates = _route(x3, nf_ref[...], wr_ref[...], br_ref[...])
    h_ref[...] = h.reshape(h_ref.shape)
    cnt_ref[...] = cnt_in[...]
    ri_ref[...] = _rank_pack(ids, cnt_ref, tc_ref)
    rg_ref[...] = gates


def _mix1_sample(x_all, state_t, nm, wp, sc, nf, wr, br, cnt, x3_all, h_all, ri_all, rg_all, tc_all):
    sample_rows = pl.BlockSpec((TM, D_MODEL), lambda g: (N_PROMPT_BLOCKS, 0))
    sample_rows3 = pl.BlockSpec((TM, ROW_TILE, LANES), lambda g: (N_PROMPT_BLOCKS, 0, 0))
    sample_lanes = pl.BlockSpec((TM, LANES), lambda g: (N_PROMPT_BLOCKS, 0))
    anyspec = pl.BlockSpec(memory_space=pl.ANY)
    n_in = 9
    return pl.pallas_call(
        _mix1_sample_kernel,
        grid=(1,),
        in_specs=[sample_rows, _const_spec((POOL_MAX - 1, DEC_BATCH, D_MODEL)), _const_spec((1, D_MODEL)),
                  _const_spec((len(POOL_SIZES), POOL_GROUP_DIM, POOL_GROUP_DIM)), _const_spec((1, D_MODEL)),
                  _const_spec((1, D_MODEL)), _const_spec((D_MODEL, 2 * LANES)), _const_spec((1, LANES)),
                  _const_spec((1, LANES)), anyspec, anyspec, anyspec, anyspec, anyspec],
        out_specs=[sample_rows, sample_rows3, pl.BlockSpec((8, TM), lambda g: (0, N_PROMPT_BLOCKS)), sample_lanes,
                   pl.BlockSpec((None, 1, LANES), lambda g: (N_PROMPT_BLOCKS, 0, 0)),
                   _const_spec((T_SAMPLE, D_MODEL)), _const_spec((1, LANES))],
        out_shape=[jax.ShapeDtypeStruct((T_ALL, D_MODEL), F32), jax.ShapeDtypeStruct((T_ALL, ROW_TILE, LANES), BF16),
                   jax.ShapeDtypeStruct((8, T_ALL), jnp.int32), jax.ShapeDtypeStruct((T_ALL, LANES), F32),
                   jax.ShapeDtypeStruct((N_ROW_BLOCKS, 1, LANES), F32),
                   jax.ShapeDtypeStruct((T_SAMPLE, D_MODEL), F32), jax.ShapeDtypeStruct((1, LANES), F32)],
        input_output_aliases={n_in: 0, n_in + 1: 1, n_in + 2: 2, n_in + 3: 3, n_in + 4: 4},
        compiler_params=_cparams(("arbitrary",)),
        name="mix1_sample",
    )(x_all, state_t, nm, wp, sc, nf, wr, br, cnt, x3_all, h_all, ri_all, rg_all, tc_all)


def _router_weights(wg, bg, we, be):
    w = jnp.concatenate([wg, jnp.transpose(we, (1, 0, 2)).reshape(D_MODEL, N_EXPERTS)], axis=1)
    b = jnp.concatenate([bg, be.reshape(N_EXPERTS)])
    pad = LANES - N_GROUPS - N_EXPERTS
    w = jnp.pad(w, ((0, 0), (0, pad)))
    w_hi = w.astype(BF16)
    w_lo = (w - w_hi.astype(F32)).astype(BF16)
    return jnp.concatenate([w_hi, w_lo], axis=1), jnp.pad(b, (0, pad)).reshape(1, LANES)


def kernel(x_prompt, x_sample, cache_k_win, cache_v_win, state_pool, norm_mix, norm_ffn, norm_final, w_in,
           a_ln_g, a_ln_b, a_w_s, a_b_s, b_sinks, rel_bias_table, w_out, c_w_pool, c_scale,
           router_group_w, router_group_b, router_expert_w, router_expert_b, w_gate, w_up, w_down):
    xs_t = jnp.transpose(x_sample, (1, 0, 2)).reshape(T_SAMPLE, D_MODEL)
    xp2 = x_prompt.reshape(T_PROMPT, D_MODEL)
    win =w_in[0].astype(BF16)
    wout = w_out[0].astype(BF16)
    lng = a_ln_g[0].reshape(1, A_WIDTH)
    lnb = a_ln_b[0].reshape(1, A_WIDTH)
    bias_p, bias_sc, bias_sn, wsp = _prep(rel_bias_table, b_sinks[0], a_w_s[0])
    bs_full = jnp.repeat(a_b_s[0].T, A_HEAD_DIM, axis=1)
    w4 = jnp.transpose(a_w_s[0][:, :DEC_SEQ, :DEC_SEQ], (1, 2, 0)).reshape(DEC_SEQ * DEC_SEQ, A_HEADS)
    wcoef = jnp.repeat(w4, A_HEAD_DIM, axis=1)
    bcoef = jnp.pad(jnp.repeat(a_b_s[0][:, :DEC_SEQ].T, A_HEAD_DIM, axis=1), ((0, 8 - DEC_SEQ), (0, 0)))
    ck = jnp.transpose(cache_k_win[0], (0, 2, 3, 1))
    cv = jnp.transpose(cache_v_win[0], (0, 2, 3, 1))
    routers = [_router_weights(router_group_w[l], router_group_b[l], router_expert_w[l], router_expert_b[l])
               for l in range(2)]
    nm = [norm_mix[l].reshape(1, D_MODEL) for l in range(2)]
    nf = [norm_ffn[l].reshape(1, D_MODEL) for l in range(2)]

    x1_all, h_all, ri_all, rg_all, tc_all, k_last, v_last, va_last, cnt0 = _mix0_prompt(
        xp2, nm[0], win, lng, lnb, wsp, bs_full, bias_p, wout, nf[0], *routers[0])
    x1_all, h_all, ri_all, rg_all, tc_all, k_new, v_new, va_s, cnt0 = _mix0_sample(
        xs_t, nm[0], win, lng, lnb, wcoef, bcoef, ck, cv, bias_sc, bias_sn, wout, nf[0], *routers[0], cnt0,
        x1_all, h_all, ri_all, rg_all, tc_all)
    cplan0, ys0 = _moe(h_all, ri_all, cnt0, tc_all, w_gate, w_up, w_down, 0)
    x2_all = _combine(cplan0, x1_all, rg_all, ys0)

    wp = c_w_pool[0].astype(BF16)
    sc = c_scale[0].reshape(1, D_MODEL)
    x3_all, h2_all, ri2_all, rg2_all, tc2_all, pool_tail, cnt1 = _mix1_prompt(
        x2_all, nm[1], wp, sc, nf[1], *routers[1])
    state_t = jnp.transpose(state_pool[0], (1, 0, 2))
    x3_all, h2_all, ri2_all, rg2_all, tc2_all, hs1, cnt1 = _mix1_sample(
        x2_all, state_t, nm[1], wp, sc, nf[1], *routers[1], cnt1, x3_all, h2_all, ri2_all, rg2_all, tc2_all)
    cplan1, ys1 = _moe(h2_all, ri2_all, cnt1, tc2_all, w_gate, w_up, w_down, 1)
    y_p, y_s = _final(cplan1, x3_all, rg2_all, ys1, norm_final.reshape(1, D_MODEL))

    def from_tmajor(a, width):
        return jnp.transpose(a.reshape(DEC_SEQ, DEC_BATCH, width), (1, 0, 2))

    y_prompt = y_p.reshape(BATCH, SEQ, D_MODEL)
    y_sample = from_tmajor(y_s, D_MODEL)
    win_k_p = k_last.reshape(1, BATCH, WINDOW, B_KV_HEADS, B_HEAD_DIM)
    win_v_p = v_last.reshape(1, BATCH, WINDOW, B_KV_HEADS, B_HEAD_DIM)
    win_k_s = jnp.transpose(k_new, (0, 3, 1, 2))[None]
    win_v_s = jnp.transpose(v_new, (0, 3, 1, 2))[None]
    chunk_v_p = va_last.reshape(1, BATCH, CHUNK, A_HEADS, A_HEAD_DIM)
    chunk_v_s = from_tmajor(va_s, A_WIDTH).reshape(1, DEC_BATCH, DEC_SEQ, A_HEADS, A_HEAD_DIM)
    pool_p = pool_tail[:, 1:][None]
    pool_s = jnp.concatenate([state_pool[0][:, DEC_SEQ:], from_tmajor(hs1, D_MODEL)], axis=1)[None]
    return (y_prompt, y_sample, win_k_p, win_v_p, win_k_s, win_v_s, chunk_v_p, chunk_v_s, pool_p, pool_s)
```

```python
import functools
import math

import numpy as np
import jax
import jax.numpy as jnp
from jax import lax
from jax.experimental import pallas as pl
from jax.experimental.pallas import tpu as pltpu

F32 = jnp.float32
BF16 = jnp.bfloat16

D_MODEL = 1024
BATCH = 2
SEQ = 8192
DEC_BATCH = 128
DEC_SEQ = 4
A_WIDTH = 512
A_HEADS = 8
A_HEAD_DIM = 64
CHUNK = 128
B_HEADS = 8
B_KV_HEADS = 2
B_HEAD_DIM = 64
B_GROUP = 4
WINDOW = 128
N_BUCKETS = 32
MAX_DISTANCE = WINDOW
Q_WIDTH = 512
KV_WIDTH = 128
IN_WIDTH = 2 * A_WIDTH + Q_WIDTH + 2 * KV_WIDTH
ATTN_SCALE = B_HEAD_DIM ** -0.5
NEG_INF = -1e30
POOL_SIZES = (2, 4, 8, 16)
POOL_GROUP_DIM = 256
POOL_MAX = 16
N_GROUPS = 4
EXPERTS_PER_GROUP = 8
N_EXPERTS = 32
TOP_K = 2
D_EXPERT = 512
EPS = 1e-6

LANES = 128
ROW_TILE = D_MODEL // LANES
T_PROMPT = BATCH * SEQ
T_SAMPLE = DEC_BATCH * DEC_SEQ
T_ALL = T_PROMPT + T_SAMPLE
TM = 512
N_PROMPT_BLOCKS = T_PROMPT // TM
N_ROW_BLOCKS = T_ALL // TM
STEPS_PER_BATCH = SEQ // TM
SUB = TM // WINDOW
N_SLOTS = T_ALL * TOP_K
MOE_BLK = 512
N_MOE_BLOCKS = N_SLOTS // MOE_BLK + N_EXPERTS
N_SORT_ROWS = N_MOE_BLOCKS * MOE_BLK
SAMPLE_GROUP = 8
N_SAMPLE_GROUPS = DEC_BATCH // SAMPLE_GROUP
VMEM_LIMIT = 56 * 1024 * 1024

STACK_HEADS = ((0, 2, 5, 7), (1, 3, 4, 6))


def _t5_bucket_np(dist):
    n = np.maximum(dist, 0)
    max_exact = N_BUCKETS // 2
    nf = np.maximum(n, 1).astype(np.float32)
    large = max_exact + (np.log(nf / np.float32(max_exact)) / np.float32(math.log(MAX_DISTANCE / max_exact))
                         * np.float32(N_BUCKETS - max_exact)).astype(np.int32)
    large = np.minimum(large, N_BUCKETS - 1)
    return np.where(n < max_exact, n, large).astype(np.int32)


def _bucket_tables():
    qi = np.arange(WINDOW)[:, None]
    ki = np.arange(2 * WINDOW)[None, :]
    dist = qi + WINDOW - ki
    valid = (dist >= 0) & (dist < WINDOW)
    bp = np.where(valid, _t5_bucket_np(dist), -1)
    bp_first = np.where(ki >= WINDOW, bp, -1)
    bkt_p = np.stack([bp_first, bp]).astype(np.int32)

    t = np.repeat(np.arange(DEC_SEQ), SAMPLE_GROUP)[:, None]
    b = np.tile(np.arange(SAMPLE_GROUP), DEC_SEQ)[:, None]
    cb = np.repeat(np.arange(SAMPLE_GROUP), WINDOW)[None, :]
    cj = np.tile(np.arange(WINDOW), SAMPLE_GROUP)[None, :]
    dist_c = t + WINDOW - cj
    valid_c = (cb == b) & (dist_c >= 0) & (dist_c < WINDOW)
    bkt_sc = np.where(valid_c, _t5_bucket_np(dist_c), -1).astype(np.int32)
    nt = np.repeat(np.arange(DEC_SEQ), SAMPLE_GROUP)[None, :]
    nb = np.tile(np.arange(SAMPLE_GROUP), DEC_SEQ)[None, :]
    dist_n = t - nt
    valid_n = (nb == b) & (dist_n >= 0)
    bkt_sn = np.where(valid_n, _t5_bucket_np(dist_n), -1).astype(np.int32)
    bkt_sn = np.concatenate([bkt_sn, np.full((32, LANES - 32), -1, np.int32)], axis=1)
    return bkt_p, bkt_sc, bkt_sn


_BKT_P, _BKT_SC, _BKT_SN = _bucket_tables()


def _cparams(semantics):
    return pltpu.CompilerParams(dimension_semantics=semantics, vmem_limit_bytes=VMEM_LIMIT)


def _rms(x, g):
    return x * lax.rsqrt(jnp.mean(x * x, axis=-1, keepdims=True) + EPS) * g


def _layernorm(x, g, b):
    xc = x - jnp.mean(x, axis=-1, keepdims=True)
    return xc * lax.rsqrt(jnp.mean(xc * xc, axis=-1, keepdims=True) + EPS) * g + b


def _dot(a, b):
    return jnp.dot(a, b, preferred_element_type=F32)


def _dot_nt(a, b):
    return lax.dot_general(a, b, (((1,), (1,)), ((), ())), preferred_element_type=F32)


def _project(x, nm, win, lng, lnb):
    h = _rms(x, nm)
    z = _dot(h.astype(BF16), win)
    u = jax.nn.gelu(z[:, :A_WIDTH])
    va = _layernorm(jax.nn.gelu(z[:, A_WIDTH:2 * A_WIDTH]), lng, lnb)
    q = z[:, 2 * A_WIDTH:2 * A_WIDTH + Q_WIDTH] * ATTN_SCALE
    k = z[:, 2 * A_WIDTH + Q_WIDTH:2 * A_WIDTH + Q_WIDTH + KV_WIDTH]
    v = z[:, 2 * A_WIDTH + Q_WIDTH + KV_WIDTH:]
    return u, va, q, k, v


def _route(x1, nf, wr, br):
    hf = _rms(x1, nf)
    h = hf.astype(BF16)
    h_lo = (hf - h.astype(F32)).astype(BF16)
    part = _dot(h, wr)
    logits = part[:, :LANES] + part[:, LANES:] + _dot(h_lo, wr[:, :LANES]) + br
    rows = logits.shape[0]
    lane = lax.broadcasted_iota(jnp.int32, (rows, LANES), 1)
    lanef = lane.astype(F32)
    big = jnp.float32(1e9)
    is_g = lane < N_GROUPS
    gl = jnp.where(is_g, logits, -jnp.inf)
    gmax = jnp.max(gl, axis=1, keepdims=True)
    gsel = jnp.min(jnp.where(gl == gmax, lanef, big), axis=1, keepdims=True)
    gsum = jnp.sum(jnp.where(is_g, jnp.exp(logits - gmax), 0.0), axis=1, keepdims=True)
    g1 = 1.0 / gsum
    lo = N_GROUPS + EXPERTS_PER_GROUP * gsel
    emask = (lanef >= lo) & (lanef < lo + EXPERTS_PER_GROUP)
    el = jnp.where(emask, logits, -jnp.inf)
    v1 = jnp.max(el, axis=1, keepdims=True)
    i1 = jnp.min(jnp.where(el == v1, lanef, big), axis=1, keepdims=True)
    el2 = jnp.where(lanef == i1, -jnp.inf, el)
    v2 = jnp.max(el2, axis=1, keepdims=True)
    i2 = jnp.min(jnp.where(el2 == v2, lanef, big), axis=1, keepdims=True)
    e2 = jnp.exp(v2 - v1)
    den = 1.0 + e2
    w1 = g1 / den
    w2 = g1 * e2 / den
    ids = jnp.where(lane == 0, i1 - N_GROUPS, jnp.where(lane == 1, i2 - N_GROUPS, 0.0)).astype(jnp.int32)
    gates = jnp.where(lane == 0, w1, jnp.where(lane == 1, w2, 0.0))
    return h, ids, gates


def _rank_pack(ids, cnt_ref, tcnt_ref):
    rows = ids.shape[0]
    lane = lax.broadcasted_iota(jnp.int32, (rows, LANES), 1)
    o0 = (lane == ids[:, 0:1]).astype(F32)
    o1 = (lane == ids[:, 1:2]).astype(F32)
    r = lax.broadcasted_iota(jnp.int32, (rows, rows), 0)
    c = lax.broadcasted_iota(jnp.int32, (rows, rows), 1)
    before = (c < r).astype(BF16)
    p01 = _dot(before, jnp.concatenate([o0, o1], axis=1).astype(BF16))
    p0 = p01[:, :LANES]
    p1 = p01[:, LANES:]
    c0 = jnp.sum(o0, axis=0, keepdims=True)
    c1 = jnp.sum(o1, axis=0, keepdims=True)
    ctile = c0 + c1
    cnt_ref[...] = cnt_ref[...] + ctile
    tcnt_ref[...] = ctile
    inc = jnp.broadcast_to(ctile, (8, LANES))
    lane8 = lax.broadcasted_iota(jnp.int32, (8, LANES), 1)
    for sh in (1, 2, 4, 8, 16, 32, 64):
        inc = inc + jnp.where(lane8 >= sh, pltpu.roll(inc, sh, 1), 0.0)
    start = inc[0:1] - ctile
    lpos0 = jnp.sum(o0 * (start + p0), axis=1, keepdims=True)
    lpos1 = jnp.sum(o1 * (start + c0 + p1), axis=1, keepdims=True)
    idf = ids.astype(F32)
    packed = jnp.where(lane < TOP_K, idf, 0.0)
    for ln, col in ((4, lpos0), (5, lpos1)):
        packed = jnp.where(lane == ln, col, packed)
    return jnp.transpose(packed)[:8].astype(jnp.int32)


def _prep_kernel(tab_ref, sink_ref, bp_ref, bsc_ref, bsn_ref, ws_ref, op_ref, osc_ref, osn_ref, ows_ref):
    def fill(bkt, write, sink_col0):
        col0 = lax.broadcasted_iota(jnp.int32, bkt.shape, 1) == 0
        for st, heads in enumerate(STACK_HEADS):
            for slot, h in enumerate(heads):
                acc = jnp.full(bkt.shape, NEG_INF, F32)
                for b in range(N_BUCKETS):
                    acc = jnp.where(bkt == b, tab_ref[b, h], acc)
                if sink_col0:
                    acc = jnp.where(col0, sink_ref[0, h], acc)
                write(st, slot, acc)

    for var in range(2):
        def wr_p(st, slot, acc, var=var):
            op_ref[var, st, slot * WINDOW:(slot + 1) * WINDOW, :] = acc
        fill(bp_ref[var], wr_p, True)

    rows_s = DEC_SEQ * SAMPLE_GROUP

    def wr_sc(st, slot, acc):
        osc_ref[st, slot * rows_s:(slot + 1) * rows_s, :] = acc
    fill(bsc_ref[...], wr_sc, True)

    def wr_sn(st, slot, acc):
        osn_ref[st, slot * rows_s:(slot + 1) * rows_s, :] = acc
    fill(bsn_ref[...], wr_sn, False)

    r = lax.broadcasted_iota(jnp.int32, (CHUNK, CHUNK), 0)
    c = lax.broadcasted_iota(jnp.int32, (CHUNK, CHUNK), 1)
    for h in range(A_HEADS):
        ows_ref[h // 2, :, (h % 2) * CHUNK:(h % 2 + 1) * CHUNK] = jnp.where(r >= c, ws_ref[h], 0.0).astype(BF16)


def _prep(rel_bias_table, sinks, w_s):
    vm = pl.BlockSpec(memory_space=pltpu.VMEM)
    sm = pl.BlockSpec(memory_space=pltpu.SMEM)
    rows_s = DEC_SEQ * SAMPLE_GROUP
    return pl.pallas_call(
        _prep_kernel,
        in_specs=[sm, sm, vm, vm, vm, vm],
        out_specs=[vm, vm, vm, vm],
        out_shape=[
            jax.ShapeDtypeStruct((2, 2, 4 * WINDOW, 2 * WINDOW), F32),
            jax.ShapeDtypeStruct((2, 4 * rows_s, SAMPLE_GROUP * WINDOW), F32),
            jax.ShapeDtypeStruct((2, 4 * rows_s, LANES), F32),
            jax.ShapeDtypeStruct((A_HEADS // 2, CHUNK, 2 * CHUNK), BF16),
        ],
        name="prep_tables",
    )(rel_bias_table, sinks.reshape(1, B_HEADS), jnp.asarray(_BKT_P), jnp.asarray(_BKT_SC), jnp.asarray(_BKT_SN), w_s)


def _gate_pairs(va_rows, wsp_ref, lane_lo):
    outs = []
    for p in range(A_HEADS // 2):
        vp = va_rows[:, p * LANES:(p + 1) * LANES]
        rhs = jnp.concatenate([jnp.where(lane_lo, vp, 0.0), jnp.where(lane_lo, 0.0, vp)], axis=0).astype(BF16)
        outs.append(_dot(wsp_ref[p], rhs))
    return jnp.concatenate(outs, axis=1)


def _prompt_steps(body, first_row_out):
    def kern(*refs):
        i = pl.program_id(0)

        @pl.when(i < N_PROMPT_BLOCKS)
        def _():
            body(*refs)

        @pl.when(i >= N_PROMPT_BLOCKS)
        def _():
            for r in refs[first_row_out:first_row_out + 5]:
                r[...] = jnp.zeros(r.shape, r.dtype)

    return kern


def _mix0_prompt_kernel(x_ref, nm_ref, win_ref, lng_ref, lnb_ref, wsp_ref, bs_ref, bias_ref,
                        wout_ref, nf_ref, wr_ref, br_ref,
                        x1_ref, h_ref, ri_ref, rg_ref, tc_ref, kl_ref, vl_ref, val_ref, cnt_ref,
                        kprev, vprev, mix_scr):
    @pl.when(pl.program_id(0) == 0)
    def _():
        cnt_ref[...] = jnp.zeros_like(cnt_ref)

    x = x_ref[...]
    u, va, q, k, v = _project(x, nm_ref[...], win_ref[...], lng_ref[...], lnb_ref[...])
    lane_lo = lax.broadcasted_iota(jnp.int32, (WINDOW, LANES), 1) < B_HEAD_DIM
    row0 = lax.broadcasted_iota(jnp.int32, (WINDOW, KV_WIDTH), 0) == 0
    first = pl.program_id(0) % STEPS_PER_BATCH == 0

    @pl.when(first)
    def _():
        kprev[...] = jnp.zeros_like(kprev)
        vprev[...] = jnp.zeros_like(vprev)

    for j in range(SUB):
        rows = slice(j * WINDOW, (j + 1) * WINDOW)
        s_gate = _gate_pairs(va[rows], wsp_ref, lane_lo)
        mix_scr[rows, :A_WIDTH] = u[rows] * (s_gate + bs_ref[...])

        if j == 0:
            kp, vp = kprev[...], vprev[...]
        else:
            prows = slice((j - 1) * WINDOW, j * WINDOW)
            kp, vp = k[prows], v[prows]
        kk = jnp.concatenate([jnp.where(row0, 0.0, kp), k[rows]], axis=0)
        vv = jnp.concatenate([jnp.where(row0, 0.0, vp), v[rows]], axis=0)
        kops = (kk.astype(BF16), pltpu.roll(kk, B_HEAD_DIM, 1).astype(BF16))
        vops = (vv.astype(BF16), pltpu.roll(vv, B_HEAD_DIM, 1).astype(BF16))
        qt = [q[rows, p * LANES:(p + 1) * LANES] for p in range(4)]
        q_even = [jnp.where(lane_lo, t, 0.0) for t in qt]
        q_odd = [jnp.where(lane_lo, 0.0, t) for t in qt]
        stacks = (jnp.concatenate([q_even[0], q_even[1], q_odd[2], q_odd[3]], axis=0),
                  jnp.concatenate([q_odd[0], q_odd[1], q_even[2], q_even[3]], axis=0))
        o = []
        for st in range(2):
            s = _dot_nt(stacks[st].astype(BF16), kops[st])
            if j == 0:
                bias = bias_ref[jnp.where(first, 0, 1), st]
            else:
                bias = bias_ref[1, st]
            s = s + bias
            m = jnp.max(s, axis=-1, keepdims=True)
            p = jnp.exp(s - m)
            den = jnp.sum(p, axis=-1, keepdims=True)
            o.append(_dot(p.astype(BF16), vops[st]) / den)
        oa, ob = o
        sl = [slice(i * WINDOW, (i + 1) * WINDOW) for i in range(4)]
        tiles = (jnp.where(lane_lo, oa[sl[0]], ob[sl[0]]), jnp.where(lane_lo, oa[sl[1]], ob[sl[1]]),
                 jnp.where(lane_lo, ob[sl[2]], oa[sl[2]]), jnp.where(lane_lo, ob[sl[3]], oa[sl[3]]))
        for p in range(4):
            mix_scr[rows, A_WIDTH + p * LANES:A_WIDTH + (p + 1) * LANES] = tiles[p]

    last = slice(TM - WINDOW, TM)
    kprev[...] = k[last]
    vprev[...] = v[last]
    kl_ref[...] = k[last]
    vl_ref[...] = v[last]
    val_ref[...] = va[last]

    x1 = x + _dot(mix_scr[...].astype(BF16), wout_ref[...])
    x1_ref[...] = x1
    h, ids, gates = _route(x1, nf_ref[...], wr_ref[...], br_ref[...])
    h_ref[...] = h.reshape(h_ref.shape)
    ri_ref[...] = _rank_pack(ids, cnt_ref, tc_ref)
    rg_ref[...] = gates


def _const_spec(shape):
    nd = len(shape)
    return pl.BlockSpec(shape, lambda i, _n=nd: (0,) * _n)


def _mix0_prompt(x_all, nm, win, lng, lnb, wsp, bs_full, bias_p, wout, nf, wr, br):
    row_spec = pl.BlockSpec((TM, D_MODEL), lambda i: (i, 0))
    row3_spec = pl.BlockSpec((TM, ROW_TILE, LANES), lambda i: (i, 0, 0))
    lane_spec = pl.BlockSpec((TM, LANES), lambda i: (i, 0))
    last_kv = pl.BlockSpec((None, WINDOW, KV_WIDTH), lambda i: (jnp.minimum(i // STEPS_PER_BATCH, BATCH - 1), 0, 0))
    last_va = pl.BlockSpec((None, WINDOW, A_WIDTH), lambda i: (jnp.minimum(i // STEPS_PER_BATCH, BATCH - 1), 0, 0))
    return pl.pallas_call(
        _prompt_steps(_mix0_prompt_kernel, 12),
        grid=(N_ROW_BLOCKS,),
        in_specs=[pl.BlockSpec((TM, D_MODEL), lambda i: (jnp.minimum(i, N_PROMPT_BLOCKS - 1), 0)),
                  _const_spec((1, D_MODEL)), _const_spec((D_MODEL, IN_WIDTH)),
                  _const_spec((1, A_WIDTH)), _const_spec((1, A_WIDTH)),
                  _const_spec((A_HEADS // 2, CHUNK, 2 * CHUNK)), _const_spec((CHUNK, A_WIDTH)),
                  _const_spec((2, 2, 4 * WINDOW, 2 * WINDOW)),
                  _const_spec((A_WIDTH + Q_WIDTH, D_MODEL)), _const_spec((1, D_MODEL)),
                  _const_spec((D_MODEL, 2 * LANES)), _const_spec((1, LANES))],
        out_specs=[row_spec, row3_spec, pl.BlockSpec((8, TM), lambda i: (0, i)), lane_spec,
                   pl.BlockSpec((None, 1, LANES), lambda i: (i, 0, 0)),
                   last_kv, last_kv, last_va, _const_spec((1, LANES))],
        out_shape=[jax.ShapeDtypeStruct((T_ALL, D_MODEL), F32), jax.ShapeDtypeStruct((T_ALL, ROW_TILE, LANES), BF16),
                   jax.ShapeDtypeStruct((8, T_ALL), jnp.int32), jax.ShapeDtypeStruct((T_ALL, LANES), F32),
                   jax.ShapeDtypeStruct((N_ROW_BLOCKS, 1, LANES), F32),
                   jax.ShapeDtypeStruct((BATCH, WINDOW, KV_WIDTH), F32),
                   jax.ShapeDtypeStruct((BATCH, WINDOW, KV_WIDTH), F32),
                   jax.ShapeDtypeStruct((BATCH, WINDOW, A_WIDTH), F32),
                   jax.ShapeDtypeStruct((1, LANES), F32)],
        scratch_shapes=[pltpu.VMEM((WINDOW, KV_WIDTH), F32), pltpu.VMEM((WINDOW, KV_WIDTH), F32),
                        pltpu.VMEM((TM, D_MODEL), F32)],
        compiler_params=_cparams(("arbitrary",)),
        name="mix0_prompt",
    )(x_all, nm, win, lng, lnb, wsp, bs_full, bias_p, wout, nf, wr, br)


def _mix0_sample_kernel(x_ref, nm_ref, win_ref, lng_ref, lnb_ref, wcoef_ref, bcoef_ref,
                        ck_ref, cv_ref, bsc_ref, bsn_ref,
                        wout_ref, nf_ref, wr_ref, br_ref, cnt_in,
                        x1_in, h_in, ri_in, rg_in, tc_in,
                        x1_ref, h_ref, ri_ref, rg_ref, tc_ref, kn_ref, vn_ref, va_ref, cnt_ref,
                        q_scr, k_scr, v_scr, mix_scr):
    del x1_in, h_in, ri_in, rg_in, tc_in
    g = pl.program_id(0)

    @pl.when(g == 0)
    def _():
        u, va, q, k, v = _project(x_ref[...], nm_ref[...], win_ref[...], lng_ref[...], lnb_ref[...])
        q_scr[...] = q
        k_scr[...] = k
        v_scr[...] = v
        va_ref[...] = va
        for t in range(DEC_SEQ):
            acc = jnp.zeros((DEC_BATCH, A_WIDTH), F32) + bcoef_ref[t:t + 1, :]
            for s in range(t + 1):
                row = t * DEC_SEQ + s
                acc = acc + wcoef_ref[row:row + 1, :] * va[s * DEC_BATCH:(s + 1) * DEC_BATCH]
            mix_scr[t * DEC_BATCH:(t + 1) * DEC_BATCH, :A_WIDTH] = u[t * DEC_BATCH:(t + 1) * DEC_BATCH] * acc

    b0 = pl.multiple_of(g * SAMPLE_GROUP, SAMPLE_GROUP)
    lane_lo = lax.broadcasted_iota(jnp.int32, (DEC_SEQ * SAMPLE_GROUP, LANES), 1) < B_HEAD_DIM

    def grab(ref, width):
        return jnp.concatenate([ref[pl.ds(t * DEC_BATCH + b0, SAMPLE_GROUP), :] for t in range(DEC_SEQ)], axis=0)

    qg = grab(q_scr, Q_WIDTH)
    kn = grab(k_scr, KV_WIDTH)
    vn = grab(v_scr, KV_WIDTH)

    lane_w = lax.broadcasted_iota(jnp.int32, (KV_WIDTH, WINDOW), 1)
    n_new = DEC_SEQ * SAMPLE_GROUP

    def new_window(c_ref, new_rows, w_ref):
        nt = jnp.transpose(jnp.concatenate([new_rows, jnp.zeros((WINDOW - n_new, KV_WIDTH), F32)], axis=0))
        for b in range(SAMPLE_GROUP):
            w = pltpu.roll(c_ref[b].reshape(KV_WIDTH, WINDOW), WINDOW - DEC_SEQ, 1)
            for t in range(DEC_SEQ):
                src = t * SAMPLE_GROUP + b
                dst = WINDOW - DEC_SEQ + t
                w = jnp.where(lane_w == dst, pltpu.roll(nt, (dst - src) % WINDOW, 1), w)
            w_ref[b] = w.reshape(B_KV_HEADS, B_HEAD_DIM, WINDOW)

    new_window(ck_ref, kn, kn_ref)
    new_window(cv_ref, vn, vn_ref)
    ccol0 = lax.broadcasted_iota(jnp.int32, (KV_WIDTH, SAMPLE_GROUP * WINDOW), 1) == 0

    def cache_t(ref):
        t = jnp.concatenate([ref[b].reshape(KV_WIDTH, WINDOW) for b in range(SAMPLE_GROUP)], axis=1)
        return jnp.where(ccol0, 0.0, t)

    def head_swap(t):
        return jnp.concatenate([t[B_HEAD_DIM:], t[:B_HEAD_DIM]], axis=0)

    kct = cache_t(ck_ref)
    vct = cache_t(cv_ref)
    kc_ops = (kct.astype(BF16), head_swap(kct).astype(BF16))
    vc_ops = (vct.astype(BF16), head_swap(vct).astype(BF16))
    kn_ops = (kn.astype(BF16), pltpu.roll(kn, B_HEAD_DIM, 1).astype(BF16))
    vn_ops = (vn.astype(BF16), pltpu.roll(vn, B_HEAD_DIM, 1).astype(BF16))
    qt = [qg[:, p * LANES:(p + 1) * LANES] for p in range(4)]
    q_even = [jnp.where(lane_lo, t, 0.0) for t in qt]
    q_odd = [jnp.where(lane_lo, 0.0, t) for t in qt]
    stacks = (jnp.concatenate([q_even[0], q_even[1], q_odd[2], q_odd[3]], axis=0),
              jnp.concatenate([q_odd[0], q_odd[1], q_even[2], q_even[3]], axis=0))
    o = []
    for st in range(2):
        qs = stacks[st].astype(BF16)
        sc = _dot(qs, kc_ops[st]) + bsc_ref[st]
        sn = _dot_nt(qs, kn_ops[st]) + bsn_ref[st][:, :DEC_SEQ * SAMPLE_GROUP]
        m = jnp.maximum(jnp.max(sc, axis=-1, keepdims=True), jnp.max(sn, axis=-1, keepdims=True))
        pc = jnp.exp(sc - m)
        pn = jnp.exp(sn - m)
        den = jnp.sum(pc, axis=-1, keepdims=True) + jnp.sum(pn, axis=-1, keepdims=True)
        o.append((_dot_nt(pc.astype(BF16), vc_ops[st]) + _dot(pn.astype(BF16), vn_ops[st])) / den)
    oa, ob = o
    n = DEC_SEQ * SAMPLE_GROUP
    sl = [slice(i * n, (i + 1) * n) for i in range(4)]
    tiles = (jnp.where(lane_lo, oa[sl[0]], ob[sl[0]]), jnp.where(lane_lo, oa[sl[1]], ob[sl[1]]),
             jnp.where(lane_lo, ob[sl[2]], oa[sl[2]]), jnp.where(lane_lo, ob[sl[3]], oa[sl[3]]))
    for p in range(4):
        for t in range(DEC_SEQ):
            mix_scr[pl.ds(t * DEC_BATCH + b0, SAMPLE_GROUP), A_WIDTH + p * LANES:A_WIDTH + (p + 1) * LANES] = (
                tiles[p][t * SAMPLE_GROUP:(t + 1) * SAMPLE_GROUP])

    @pl.when(g == N_SAMPLE_GROUPS - 1)
    def _():
        x1 = x_ref[...] + _dot(mix_scr[...].astype(BF16), wout_ref[...])
        x1_ref[...] = x1
        h, ids, gates = _route(x1, nf_ref[...], wr_ref[...], br_ref[...])
        h_ref[...] = h.reshape(h_ref.shape)
        cnt_ref[...] = cnt_in[...]
        ri_ref[...] = _rank_pack(ids, cnt_ref, tc_ref)
        rg_ref[...] = gates


def _mix0_sample(x_all, nm, win, lng, lnb, wcoef, bcoef, ck, cv, bias_sc, bias_sn, wout, nf, wr, br, cnt,
                 x1_all, h_all, ri_all, rg_all, tc_all):
    sample_rows = pl.BlockSpec((TM, D_MODEL), lambda g: (N_PROMPT_BLOCKS, 0))
    sample_rows3 = pl.BlockSpec((TM, ROW_TILE, LANES), lambda g: (N_PROMPT_BLOCKS, 0, 0))
    sample_lanes = pl.BlockSpec((TM, LANES), lambda g: (N_PROMPT_BLOCKS, 0))
    cache_spec = pl.BlockSpec((SAMPLE_GROUP, B_KV_HEADS, B_HEAD_DIM, WINDOW), lambda g: (g, 0, 0, 0))
    anyspec = pl.BlockSpec(memory_space=pl.ANY)
    n_in = 16
    return pl.pallas_call(
        _mix0_sample_kernel,
        grid=(N_SAMPLE_GROUPS,),
        in_specs=[_const_spec((TM, D_MODEL)), _const_spec((1, D_MODEL)), _const_spec((D_MODEL, IN_WIDTH)),
                  _const_spec((1, A_WIDTH)), _const_spec((1, A_WIDTH)),
                  _const_spec((16, A_WIDTH)), _const_spec((8, A_WIDTH)),
                  cache_spec, cache_spec,
                  _const_spec((2, 4 * 32, SAMPLE_GROUP * WINDOW)), _const_spec((2, 4 * 32, LANES)),
                  _const_spec((A_WIDTH + Q_WIDTH, D_MODEL)), _const_spec((1, D_MODEL)),
                  _const_spec((D_MODEL, 2 * LANES)), _const_spec((1, LANES)), _const_spec((1, LANES)),
                  anyspec, anyspec, anyspec, anyspec, anyspec],
        out_specs=[sample_rows, sample_rows3, pl.BlockSpec((8, TM), lambda g: (0, N_PROMPT_BLOCKS)), sample_lanes,
                   pl.BlockSpec((None, 1, LANES), lambda g: (N_PROMPT_BLOCKS, 0, 0)),
                   cache_spec, cache_spec,
                   _const_spec((T_SAMPLE, A_WIDTH)), _const_spec((1, LANES))],
        out_shape=[jax.ShapeDtypeStruct((T_ALL, D_MODEL), F32), jax.ShapeDtypeStruct((T_ALL, ROW_TILE, LANES), BF16),
                   jax.ShapeDtypeStruct((8, T_ALL), jnp.int32), jax.ShapeDtypeStruct((T_ALL, LANES), F32),
                   jax.ShapeDtypeStruct((N_ROW_BLOCKS, 1, LANES), F32),
                   jax.ShapeDtypeStruct((DEC_BATCH, B_KV_HEADS, B_HEAD_DIM, WINDOW), F32),
                   jax.ShapeDtypeStruct((DEC_BATCH, B_KV_HEADS, B_HEAD_DIM, WINDOW), F32),
                   jax.ShapeDtypeStruct((T_SAMPLE, A_WIDTH), F32), jax.ShapeDtypeStruct((1, LANES), F32)],
        scratch_shapes=[pltpu.VMEM((T_SAMPLE, Q_WIDTH), F32), pltpu.VMEM((T_SAMPLE, KV_WIDTH), F32),
                        pltpu.VMEM((T_SAMPLE, KV_WIDTH), F32), pltpu.VMEM((T_SAMPLE, D_MODEL), F32)],
        input_output_aliases={n_in: 0, n_in + 1: 1, n_in + 2: 2, n_in + 3: 3, n_in + 4: 4},
        compiler_params=_cparams(("arbitrary",)),
        name="mix0_sample",
    )(x_all, nm, win, lng, lnb, wcoef, bcoef, ck, cv, bias_sc, bias_sn, wout, nf, wr, br, cnt,
      x1_all, h_all, ri_all, rg_all, tc_all)


def _moe_metadata(rt_all, cnt, tcnt):
    counts = cnt[0, :N_EXPERTS].astype(jnp.int32)
    padded = (counts + MOE_BLK - 1) // MOE_BLK * MOE_BLK
    pad_end = jnp.cumsum(padded)
    pad_start = pad_end - padded
    experts = jnp.arange(N_EXPERTS, dtype=jnp.int32)
    n_valid = (pad_end[-1] // MOE_BLK).astype(jnp.int32).reshape(1)
    blk_start = jnp.arange(N_MOE_BLOCKS, dtype=jnp.int32) * MOE_BLK
    block_e = jnp.minimum(jnp.sum((blk_start[:, None] >= pad_end[None, :]).astype(jnp.int32), axis=1),
                          N_EXPERTS - 1).astype(jnp.int32)
    zero_start = (pad_start + counts).astype(jnp.int32)
    zero_len = (padded - counts).astype(jnp.int32)
    first = (blk_start == pad_start[block_e]).astype(jnp.int32)
    used = counts > 0
    parity = ((jnp.cumsum(used.astype(jnp.int32)) - 1) % 2)[block_e].astype(jnp.int32)
    nearest = lax.cummin(jnp.where(used, experts, N_EXPERTS)[::-1])[::-1]
    next_used = jnp.concatenate([nearest[1:], jnp.full((1,), N_EXPERTS, jnp.int32)])
    nxt = jnp.where(next_used < N_EXPERTS, next_used, -1)[block_e].astype(jnp.int32)
    plan = (block_e, first, parity, nxt, n_valid)
    runs = tcnt[:, 0, :N_EXPERTS].astype(jnp.int32)
    gruns = runs.reshape(N_DISPATCH_STEPS, DISPATCH_TILES, N_EXPERTS)
    gtot = jnp.sum(gruns, axis=1)
    gstart = jnp.cumsum(gtot, axis=1) - gtot
    shift = ((gstart[:, None, :] + jnp.cumsum(gruns, axis=1) - gruns).reshape(N_ROW_BLOCKS, N_EXPERTS)
             - (jnp.cumsum(runs, axis=1) - runs))
    shift_rows = jnp.repeat(jnp.transpose(shift), TM, axis=1)
    hit = rt_all[:TOP_K, None, :] == experts[None, :, None]
    gpos = (rt_all[2 * TOP_K:3 * TOP_K] + jnp.sum(jnp.where(hit, shift_rows[None], 0), axis=1)).reshape(N_SLOTS)
    grun_dst = pad_start[None, :] + jnp.cumsum(gtot, axis=0) - gtot
    cplan = (gpos.astype(jnp.int32), gtot.reshape(-1), grun_dst.reshape(-1).astype(jnp.int32))
    dplan = cplan + (jnp.concatenate([zero_start, zero_len, n_valid]),)
    return plan, dplan, cplan


RUN_PIECE = 32
DISPATCH_TILES = 3
DISPATCH_ROWS = DISPATCH_TILES * TM
N_DISPATCH_STEPS = N_ROW_BLOCKS // DISPATCH_TILES


def _for_run_pieces(n, start_piece):
    whole = n // RUN_PIECE

    def body(j, carry):
        start_piece(j * RUN_PIECE, RUN_PIECE)
        return carry

    lax.fori_loop(0, whole, body, 0)
    o = whole * RUN_PIECE
    bit = RUN_PIECE // 2
    while bit >= 1:
        take = (n & bit) != 0

        @pl.when(take)
        def _(o=o, bit=bit):
            start_piece(o, bit)

        o = o + jnp.where(take, bit, 0)
        bit //= 2


def _dispatch_kernel(run_ref, rdst_ref, zs_ref, h_ref, gpos_hbm, xs_ref, zero_scr, stage, pos_s, sem, zsem, psem):
    i = pl.program_id(0)

    def pos_copy(step, do):
        for kk in range(TOP_K):
            do(pltpu.make_async_copy(gpos_hbm.at[pl.ds(kk * T_ALL + step * DISPATCH_ROWS, DISPATCH_ROWS)],
                                     pos_s.at[pl.ds(kk * DISPATCH_ROWS, DISPATCH_ROWS)], psem))

    @pl.when(i == 0)
    def _():
        pos_copy(0, lambda cp: cp.start())
        zero_scr[...] = jnp.zeros_like(zero_scr)

        def pieces(e, do):
            off = zs_ref[e]
            rem = zs_ref[N_EXPERTS + e]
            bit = MOE_BLK // 2
            while bit >= 1:
                take = (rem & bit) != 0

                @pl.when(take)
                def _(off=off, bit=bit):
                    do(pltpu.make_async_copy(zero_scr.at[pl.ds(0, bit)], xs_ref.at[pl.ds(off, bit)], zsem))

                off = off + jnp.where(take, bit, 0)
                bit //= 2

        def start_e(e, c):
            pieces(e, lambda cp: cp.start())
            return c

        def wait_e(e, c):
            pieces(e, lambda cp: cp.wait())
            return c

        def tail(do):
            def step(b, c):
                do(pltpu.make_async_copy(zero_scr, xs_ref.at[pl.ds(b * MOE_BLK, MOE_BLK)], zsem))
                return c
            return step

        n_valid = zs_ref[2 * N_EXPERTS]
        lax.fori_loop(0, N_EXPERTS, start_e, 0)
        lax.fori_loop(n_valid, N_MOE_BLOCKS, tail(lambda cp: cp.start()), 0)
        lax.fori_loop(0, N_EXPERTS, wait_e, 0)
        lax.fori_loop(n_valid, N_MOE_BLOCKS, tail(lambda cp: cp.wait()), 0)

    slot = i % 2
    pos_copy(i, lambda cp: cp.wait())

    for s in range(2):
        @pl.when(slot == s)
        def _(s=s):
            def place(r, carry):
                row = h_ref[r]
                for kk in range(TOP_K):
                    stage[s, pos_s[kk * DISPATCH_ROWS + r]] = row
                return carry

            lax.fori_loop(0, DISPATCH_ROWS, place, 0, unroll=8)

    @pl.when(i + 1 < N_DISPATCH_STEPS)
    def _():
        pos_copy(i + 1, lambda cp: cp.start())

    def send_run(e, off):
        n = run_ref[i * N_EXPERTS + e]
        dst = rdst_ref[i * N_EXPERTS + e]
        _for_run_pieces(n, lambda o, size: pltpu.make_async_copy(
            stage.at[slot, pl.ds(off + o, size)], xs_ref.at[pl.ds(dst + o, size)], sem.at[slot]).start(
                priority=size.bit_length() % 2))
        return off + n

    lax.fori_loop(0, N_EXPERTS, send_run, 0)

    def drain(s):
        pltpu.make_async_copy(stage.at[s], xs_ref.at[pl.ds(0, DISPATCH_ROWS * TOP_K)], sem.at[s]).wait()

    @pl.when(i >= 1)
    def _():
        drain(1 - slot)

    @pl.when(i == N_DISPATCH_STEPS - 1)
    def _():
        drain(slot)


def _dispatch(dplan, h_all):
    return pl.pallas_call(
        _dispatch_kernel,
        grid_spec=pltpu.PrefetchScalarGridSpec(
            num_scalar_prefetch=3,
            grid=(N_DISPATCH_STEPS,),
            in_specs=[pl.BlockSpec((DISPATCH_ROWS, ROW_TILE, LANES), lambda i, rn, rd, z: (i, 0, 0)),
                      pl.BlockSpec(memory_space=pl.ANY)],
            out_specs=pl.BlockSpec(memory_space=pl.ANY),
            scratch_shapes=[pltpu.VMEM((MOE_BLK, ROW_TILE, LANES), BF16),
                            pltpu.VMEM((2, DISPATCH_ROWS * TOP_K, ROW_TILE, LANES), BF16),
                            pltpu.SMEM((TOP_K * DISPATCH_ROWS,), jnp.int32),
                            pltpu.SemaphoreType.DMA((2,)), pltpu.SemaphoreType.DMA(()),
                            pltpu.SemaphoreType.DMA(())],
        ),
        out_shape=jax.ShapeDtypeStruct((N_SORT_ROWS, ROW_TILE, LANES), BF16),
        compiler_params=_cparams(("arbitrary",)),
        name="moe_dispatch",
    )(*dplan[1:], h_all, dplan[0])


def _experts_kernel(layer, be_ref, first_ref, par_ref, nxt_ref, nv_ref,
                    x_ref, wg_hbm, wu_hbm, wd_hbm, y_ref,
                    wg_s, wu_s, wd_s, wg_f, wu_f, wd_f, wsem):
    i = pl.program_id(0)

    def fetch(e, slot):
        return (pltpu.make_async_copy(wg_hbm.at[layer, e], wg_f.at[slot], wsem.at[slot]),
                pltpu.make_async_copy(wu_hbm.at[layer, e], wu_f.at[slot], wsem.at[slot]),
                pltpu.make_async_copy(wd_hbm.at[layer, e], wd_f.at[slot], wsem.at[slot]))

    @pl.when(i < nv_ref[0])
    def _():
        e = be_ref[i]
        slot = par_ref[i]

        @pl.when(i == 0)
        def _():
            for cp in fetch(e, slot):
                cp.start()

        @pl.when(first_ref[i] == 1)
        def _():
            for cp in fetch(e, slot):
                cp.wait()
            wg_s[...] = wg_f[slot].astype(BF16)
            wu_s[...] = wu_f[slot].astype(BF16)
            wd_s[...] = wd_f[slot].astype(BF16)
            nxt = nxt_ref[i]

            @pl.when(nxt >= 0)
            def _():
                for cp in fetch(nxt, 1 - slot):
                    cp.start()

        xb = x_ref[...].reshape(MOE_BLK, D_MODEL)
        a = jax.nn.silu(_dot(xb, wg_s[...])) * _dot(xb, wu_s[...])
        y_ref[...] = _dot(a.astype(BF16), wd_s[...]).reshape(y_ref.shape)

    @pl.when(i >= nv_ref[0])
    def _():
        y_ref[...] = jnp.zeros(y_ref.shape, y_ref.dtype)


def _experts(block_e, first, parity, nxt, n_valid, xs, w_gate, w_up, w_down, layer):
    def blk(i, be, fi, pa, nx, nv):
        return (jnp.maximum(jnp.minimum(i, nv[0] - 1), 0), 0, 0)

    anyspec = pl.BlockSpec(memory_space=pl.ANY)
    return pl.pallas_call(
        functools.partial(_experts_kernel, layer),
        grid_spec=pltpu.PrefetchScalarGridSpec(
            num_scalar_prefetch=5,
            grid=(N_MOE_BLOCKS,),
            in_specs=[pl.BlockSpec((MOE_BLK, ROW_TILE, LANES), blk), anyspec, anyspec, anyspec],
            out_specs=pl.BlockSpec((MOE_BLK, ROW_TILE, LANES), lambda i, be, fi, pa, nx, nv: (i, 0, 0)),
            scratch_shapes=[pltpu.VMEM((D_MODEL, D_EXPERT), BF16), pltpu.VMEM((D_MODEL, D_EXPERT), BF16),
                            pltpu.VMEM((D_EXPERT, D_MODEL), BF16),
                            pltpu.VMEM((2, D_MODEL, D_EXPERT), F32), pltpu.VMEM((2, D_MODEL, D_EXPERT), F32),
                            pltpu.VMEM((2, D_EXPERT, D_MODEL), F32), pltpu.SemaphoreType.DMA((2,))],
        ),
        out_shape=jax.ShapeDtypeStruct((N_SORT_ROWS, ROW_TILE, LANES), F32),
        compiler_params=_cparams(("arbitrary",)),
        name="moe_experts",
    )(block_e, first, parity, nxt, n_valid, xs, w_gate, w_up, w_down)


def _gather_rows(lpos_ref, run_ref, rdst_ref, ys_ref, ystage, ybufs, sem, i):
    def fetch(group, buf):
        def fetch_run(e, off):
            n = run_ref[group * N_EXPERTS + e]
            src = rdst_ref[group * N_EXPERTS + e]
            _for_run_pieces(n, lambda o, size: pltpu.make_async_copy(
                ys_ref.at[pl.ds(src + o, size)], ystage.at[buf, pl.ds(off + o, size)], sem.at[buf]).start(
                    priority=size.bit_length() % 2))
            return off + n

        lax.fori_loop(0, N_EXPERTS, fetch_run, 0)

    def wait(buf):
        pltpu.make_async_copy(ys_ref.at[pl.ds(0, DISPATCH_ROWS * TOP_K)], ystage.at[buf], sem.at[buf]).wait()

    cur = i % 2
    group = i // DISPATCH_TILES
    phase = i % DISPATCH_TILES
    last_of_group = phase == DISPATCH_TILES - 1

    @pl.when(i == 0)
    def _():
        fetch(0, 0)
        wait(0)

        def unplace(r, carry):
            for kk in range(TOP_K):
                ybufs[0][kk, r] = ystage[0, lpos_ref[kk * T_ALL + r]]
            return carry

        lax.fori_loop(0, TM, unplace, 0, unroll=8)
        fetch(1, 1)

    @pl.when(last_of_group & (group + 1 < N_DISPATCH_STEPS))
    def _():
        wait((group + 1) % 2)

    @pl.when(last_of_group & (group + 2 < N_DISPATCH_STEPS))
    def _():
        fetch(group + 2, group % 2)

    def pieces(compute, store):
        nxt = jnp.minimum(i + 1, N_ROW_BLOCKS - 1)
        nslot = (nxt // DISPATCH_TILES) % 2

        def variant(par):
            ycur, ynext = ybufs[par], ybufs[1 - par]

            def piece(j, carry):
                rows = pl.ds(pl.multiple_of(j * COMBINE_ROWS, COMBINE_ROWS), COMBINE_ROWS)
                out = compute(rows, ycur[0, rows].reshape(COMBINE_ROWS, D_MODEL),
                              ycur[1, rows].reshape(COMBINE_ROWS, D_MODEL))
                base = nxt * TM + j * COMBINE_ROWS
                for r in range(COMBINE_ROWS):
                    for kk in range(TOP_K):
                        ynext[kk, j * COMBINE_ROWS + r] = ystage[nslot, lpos_ref[kk * T_ALL + base + r]]
                store(rows, out)
                return carry

            lax.fori_loop(0, TM // COMBINE_ROWS, piece, 0)

        for par in range(2):
            @pl.when(cur == par)
            def _(par=par):
                variant(par)

    return pieces


COMBINE_ROWS = 64


def _combined(x_ref, rg_ref, rows, y0, y1):
    rg = rg_ref[rows, :]
    return x_ref[rows, :] + rg[:, 0:1] * y0 + rg[:, 1:2] * y1


_COMBINE_SCRATCH = [pltpu.VMEM((2, DISPATCH_ROWS * TOP_K, ROW_TILE, LANES), F32),
                    pltpu.VMEM((TOP_K, TM, ROW_TILE, LANES), F32), pltpu.VMEM((TOP_K, TM, ROW_TILE, LANES), F32),
                    pltpu.SemaphoreType.DMA((2,))]


def _combine_kernel(lpos_ref, run_ref, rdst_ref, x_ref, rg_ref, ys_ref, o_ref, ystage, ybuf0, ybuf1, sem):
    pieces = _gather_rows(lpos_ref, run_ref, rdst_ref, ys_ref, ystage, (ybuf0, ybuf1), sem, pl.program_id(0))

    def store(rows, out):
        o_ref[rows, :] = out

    pieces(functools.partial(_combined, x_ref, rg_ref), store)


def _combine(cplan, x_all, rg_all, ys):
    return pl.pallas_call(
        _combine_kernel,
        grid_spec=pltpu.PrefetchScalarGridSpec(
            num_scalar_prefetch=3,
            grid=(N_ROW_BLOCKS,),
            in_specs=[pl.BlockSpec((TM, D_MODEL), lambda i, a, b, c: (i, 0)),
                      pl.BlockSpec((TM, LANES), lambda i, a, b, c: (i, 0)),
                      pl.BlockSpec(memory_space=pl.ANY)],
            out_specs=pl.BlockSpec((TM, D_MODEL), lambda i, a, b, c: (i, 0)),
            scratch_shapes=_COMBINE_SCRATCH,
        ),
        out_shape=jax.ShapeDtypeStruct((T_ALL, D_MODEL), F32),
        compiler_params=_cparams(("arbitrary",)),
        name="moe_combine",
    )(*cplan, x_all, rg_all, ys)


def _final_kernel(lpos_ref, run_ref, rdst_ref, x_ref, rg_ref, ys_ref, nfin_ref, op_ref, os_ref,
                  ystage, ybuf0, ybuf1, sem):
    i = pl.program_id(0)
    pieces = _gather_rows(lpos_ref, run_ref, rdst_ref, ys_ref, ystage, (ybuf0, ybuf1), sem, i)

    def compute(rows, y0, y1):
        return _rms(_combined(x_ref, rg_ref, rows, y0, y1), nfin_ref[...])

    def store(rows, y):
        @pl.when(i < N_PROMPT_BLOCKS)
        def _():
            op_ref[rows, :] = y

        @pl.when(i >= N_PROMPT_BLOCKS)
        def _():
            os_ref[rows, :] = y

    pieces(compute, store)


def _final(cplan, x_all, rg_all, ys, nfin):
    return pl.pallas_call(
        _final_kernel,
        grid_spec=pltpu.PrefetchScalarGridSpec(
            num_scalar_prefetch=3,
            grid=(N_ROW_BLOCKS,),
            in_specs=[pl.BlockSpec((TM, D_MODEL), lambda i, a, b, c: (i, 0)),
                      pl.BlockSpec((TM, LANES), lambda i, a, b, c: (i, 0)),
                      pl.BlockSpec(memory_space=pl.ANY),
                      pl.BlockSpec((1, D_MODEL), lambda i, a, b, c: (0, 0))],
            out_specs=[pl.BlockSpec((TM, D_MODEL), lambda i, a, b, c: (jnp.minimum(i, N_PROMPT_BLOCKS - 1), 0)),
                       pl.BlockSpec((TM, D_MODEL), lambda i, a, b, c: (0, 0))],
            scratch_shapes=_COMBINE_SCRATCH,
        ),
        out_shape=[jax.ShapeDtypeStruct((T_PROMPT, D_MODEL), F32), jax.ShapeDtypeStruct((T_SAMPLE, D_MODEL), F32)],
        compiler_params=_cparams(("arbitrary",)),
        name="moe_combine_final",
    )(*cplan, x_all, rg_all, ys, nfin)


def _moe(h_all, rt_all, cnt, tcnt, w_gate, w_up, w_down, layer):
    plan, dplan, cplan = _moe_metadata(rt_all, cnt, tcnt)
    xs = _dispatch(dplan, h_all)
    ys = _experts(*plan, xs, w_gate, w_up, w_down, layer)
    return cplan, ys


def _pool_project(d_groups, wp_ref, scale):
    outs = [_dot(d_groups[g].astype(BF16), wp_ref[g]) for g in range(len(POOL_SIZES))]
    return jnp.concatenate(outs, axis=1) * scale


def _mix1_prompt_kernel(x_ref, nm_ref, wp_ref, sc_ref, nf_ref, wr_ref, br_ref,
                        x3_ref, h_ref, ri_ref, rg_ref, tc_ref, pl_ref, cnt_ref, ext):
    i = pl.program_id(0)

    @pl.when(i == 0)
    def _():
        cnt_ref[...] = jnp.zeros_like(cnt_ref)

    x = x_ref[...]
    hp = _rms(x, nm_ref[...])

    @pl.when(i % STEPS_PER_BATCH == 0)
    def _():
        ext[0:POOL_MAX, :] = jnp.zeros((POOL_MAX, D_MODEL), F32)

    ext[POOL_MAX:, :] = hp
    pos = (i % STEPS_PER_BATCH) * TM + lax.broadcasted_iota(jnp.int32, (TM, 1), 0)
    d_groups = []
    for g, w in enumerate(POOL_SIZES):
        cols = slice(g * POOL_GROUP_DIM, (g + 1) * POOL_GROUP_DIM)
        acc = ext[:, cols]
        span = 1
        while span < w:
            acc = acc + pltpu.roll(acc, span, 0)
            span *= 2
        cnt = jnp.minimum(pos + 1, w).astype(F32)
        d_groups.append(acc[POOL_MAX:] / cnt - hp[:, cols])
    tail = hp[TM - POOL_MAX:, :]
    ext[0:POOL_MAX, :] = tail
    pl_ref[...] = tail

    x3 = x + _pool_project(d_groups, wp_ref, sc_ref[...])
    x3_ref[...] = x3
    h, ids, gates = _route(x3, nf_ref[...], wr_ref[...], br_ref[...])
    h_ref[...] = h.reshape(h_ref.shape)
    ri_ref[...] = _rank_pack(ids, cnt_ref, tc_ref)
    rg_ref[...] = gates


def _mix1_prompt(x_all, nm, wp, sc, nf, wr, br):
    row_spec = pl.BlockSpec((TM, D_MODEL), lambda i: (i, 0))
    row3_spec = pl.BlockSpec((TM, ROW_TILE, LANES), lambda i: (i, 0, 0))
    lane_spec = pl.BlockSpec((TM, LANES), lambda i: (i, 0))
    return pl.pallas_call(
        _prompt_steps(_mix1_prompt_kernel, 7),
        grid=(N_ROW_BLOCKS,),
        in_specs=[row_spec, _const_spec((1, D_MODEL)),
                  _const_spec((len(POOL_SIZES), POOL_GROUP_DIM, POOL_GROUP_DIM)), _const_spec((1, D_MODEL)),
                  _const_spec((1, D_MODEL)), _const_spec((D_MODEL, 2 * LANES)), _const_spec((1, LANES))],
        out_specs=[row_spec, row3_spec, pl.BlockSpec((8, TM), lambda i: (0, i)), lane_spec,
                   pl.BlockSpec((None, 1, LANES), lambda i: (i, 0, 0)),
                   pl.BlockSpec((None, POOL_MAX, D_MODEL),
                                lambda i: (jnp.minimum(i // STEPS_PER_BATCH, BATCH - 1), 0, 0)),
                   _const_spec((1, LANES))],
        out_shape=[jax.ShapeDtypeStruct((T_ALL, D_MODEL), F32), jax.ShapeDtypeStruct((T_ALL, ROW_TILE, LANES), BF16),
                   jax.ShapeDtypeStruct((8, T_ALL), jnp.int32), jax.ShapeDtypeStruct((T_ALL, LANES), F32),
                   jax.ShapeDtypeStruct((N_ROW_BLOCKS, 1, LANES), F32),
                   jax.ShapeDtypeStruct((BATCH, POOL_MAX, D_MODEL), F32), jax.ShapeDtypeStruct((1, LANES), F32)],
        scratch_shapes=[pltpu.VMEM((POOL_MAX + TM, D_MODEL), F32)],
        compiler_params=_cparams(("arbitrary",)),
        name="mix1_prompt",
    )(x_all, nm, wp, sc, nf, wr, br)


def _mix1_sample_kernel(x_ref, st_ref, nm_ref, wp_ref, sc_ref, nf_ref, wr_ref, br_ref, cnt_in,
                        x3_in, h_in, ri_in, rg_in, tc_in,
                        x3_ref, h_ref, ri_ref, rg_ref, tc_ref, hs_ref, cnt_ref):
    del x3_in, h_in, ri_in, rg_in, tc_in
    x = x_ref[...]
    hs = _rms(x, nm_ref[...])
    hs_ref[...] = hs
    n_ctx = POOL_MAX - 1
    d_groups = []
    for g, w in enumerate(POOL_SIZES):
        cols = slice(g * POOL_GROUP_DIM, (g + 1) * POOL_GROUP_DIM)
        parts = []
        for t in range(DEC_SEQ):
            acc = hs[t * DEC_BATCH:(t + 1) * DEC_BATCH, cols]
            for back in range(1, w):
                src = t - back
                if src >= 0:
                    acc = acc + hs[src * DEC_BATCH:(src + 1) * DEC_BATCH, cols]
                else:
                    acc = acc + st_ref[n_ctx + src, :, cols]
            parts.append(acc / float(w) - hs[t * DEC_BATCH:(t + 1) * DEC_BATCH, cols])
        d_groups.append(jnp.concatenate(parts, axis=0))
    x3 = x + _pool_project(d_groups, wp_ref, sc_ref[...])
    x3_ref[...] = x3
    h, ids, gates = _route(x3, nf_ref[...], wr_ref[...], br_ref[...])
    h_ref[...] = h.reshape(h_ref.shape)
    cnt_ref[...] = cnt_in[...]
    ri_ref[...] = _rank_pack(ids, cnt_ref, tc_ref)
    rg_ref[...] = gates


def _mix1_sample(x_all, state_t, nm, wp, sc, nf, wr, br, cnt, x3_all, h_all, ri_all, rg_all, tc_all):
    sample_rows = pl.BlockSpec((TM, D_MODEL), lambda g: (N_PROMPT_BLOCKS, 0))
    sample_rows3 = pl.BlockSpec((TM, ROW_TILE, LANES), lambda g: (N_PROMPT_BLOCKS, 0, 0))
    sample_lanes = pl.BlockSpec((TM, LANES), lambda g: (N_PROMPT_BLOCKS, 0))
    anyspec = pl.BlockSpec(memory_space=pl.ANY)
    n_in = 9
    return pl.pallas_call(
        _mix1_sample_kernel,
        grid=(1,),
        in_specs=[sample_rows, _const_spec((POOL_MAX - 1, DEC_BATCH, D_MODEL)), _const_spec((1, D_MODEL)),
                  _const_spec((len(POOL_SIZES), POOL_GROUP_DIM, POOL_GROUP_DIM)), _const_spec((1, D_MODEL)),
                  _const_spec((1, D_MODEL)), _const_spec((D_MODEL, 2 * LANES)), _const_spec((1, LANES)),
                  _const_spec((1, LANES)), anyspec, anyspec, anyspec, anyspec, anyspec],
        out_specs=[sample_rows, sample_rows3, pl.BlockSpec((8, TM), lambda g: (0, N_PROMPT_BLOCKS)), sample_lanes,
                   pl.BlockSpec((None, 1, LANES), lambda g: (N_PROMPT_BLOCKS, 0, 0)),
                   _const_spec((T_SAMPLE, D_MODEL)), _const_spec((1, LANES))],
        out_shape=[jax.ShapeDtypeStruct((T_ALL, D_MODEL), F32), jax.ShapeDtypeStruct((T_ALL, ROW_TILE, LANES), BF16),
                   jax.ShapeDtypeStruct((8, T_ALL), jnp.int32), jax.ShapeDtypeStruct((T_ALL, LANES), F32),
                   jax.ShapeDtypeStruct((N_ROW_BLOCKS, 1, LANES), F32),
                   jax.ShapeDtypeStruct((T_SAMPLE, D_MODEL), F32), jax.ShapeDtypeStruct((1, LANES), F32)],
        input_output_aliases={n_in: 0, n_in + 1: 1, n_in + 2: 2, n_in + 3: 3, n_in + 4: 4},
        compiler_params=_cparams(("arbitrary",)),
        name="mix1_sample",
    )(x_all, state_t, nm, wp, sc, nf, wr, br, cnt, x3_all, h_all, ri_all, rg_all, tc_all)


def _router_weights(wg, bg, we, be):
    w = jnp.concatenate([wg, jnp.transpose(we, (1, 0, 2)).reshape(D_MODEL, N_EXPERTS)], axis=1)
    b = jnp.concatenate([bg, be.reshape(N_EXPERTS)])
    pad = LANES - N_GROUPS - N_EXPERTS
    w = jnp.pad(w, ((0, 0), (0, pad)))
    w_hi = w.astype(BF16)
    w_lo = (w - w_hi.astype(F32)).astype(BF16)
    return jnp.concatenate([w_hi, w_lo], axis=1), jnp.pad(b, (0, pad)).reshape(1, LANES)


def kernel(x_prompt, x_sample, cache_k_win, cache_v_win, state_pool, norm_mix, norm_ffn, norm_final, w_in,
           a_ln_g, a_ln_b, a_w_s, a_b_s, b_sinks, rel_bias_table, w_out, c_w_pool, c_scale,
           router_group_w, router_group_b, router_expert_w, router_expert_b, w_gate, w_up, w_down):
    xs_t = jnp.transpose(x_sample, (1, 0, 2)).reshape(T_SAMPLE, D_MODEL)
    xp2 = x_prompt.reshape(T_PROMPT, D_MODEL)
    win =w_in[0].astype(BF16)
    wout = w_out[0].astype(BF16)
    lng = a_ln_g[0].reshape(1, A_WIDTH)
    lnb = a_ln_b[0].reshape(1, A_WIDTH)
    bias_p, bias_sc, bias_sn, wsp = _prep(rel_bias_table, b_sinks[0], a_w_s[0])
    bs_full = jnp.repeat(a_b_s[0].T, A_HEAD_DIM, axis=1)
    w4 = jnp.transpose(a_w_s[0][:, :DEC_SEQ, :DEC_SEQ], (1, 2, 0)).reshape(DEC_SEQ * DEC_SEQ, A_HEADS)
    wcoef = jnp.repeat(w4, A_HEAD_DIM, axis=1)
    bcoef = jnp.pad(jnp.repeat(a_b_s[0][:, :DEC_SEQ].T, A_HEAD_DIM, axis=1), ((0, 8 - DEC_SEQ), (0, 0)))
    ck = jnp.transpose(cache_k_win[0], (0, 2, 3, 1))
    cv = jnp.transpose(cache_v_win[0], (0, 2, 3, 1))
    routers = [_router_weights(router_group_w[l], router_group_b[l], router_expert_w[l], router_expert_b[l])
               for l in range(2)]
    nm = [norm_mix[l].reshape(1, D_MODEL) for l in range(2)]
    nf = [norm_ffn[l].reshape(1, D_MODEL) for l in range(2)]

    x1_all, h_all, ri_all, rg_all, tc_all, k_last, v_last, va_last, cnt0 = _mix0_prompt(
        xp2, nm[0], win, lng, lnb, wsp, bs_full, bias_p, wout, nf[0], *routers[0])
    x1_all, h_all, ri_all, rg_all, tc_all, k_new, v_new, va_s, cnt0 = _mix0_sample(
        xs_t, nm[0], win, lng, lnb, wcoef, bcoef, ck, cv, bias_sc, bias_sn, wout, nf[0], *routers[0], cnt0,
        x1_all, h_all, ri_all, rg_all, tc_all)
    cplan0, ys0 = _moe(h_all, ri_all, cnt0, tc_all, w_gate, w_up, w_down, 0)
    x2_all = _combine(cplan0, x1_all, rg_all, ys0)

    wp = c_w_pool[0].astype(BF16)
    sc = c_scale[0].reshape(1, D_MODEL)
    x3_all, h2_all, ri2_all, rg2_all, tc2_all, pool_tail, cnt1 = _mix1_prompt(
        x2_all, nm[1], wp, sc, nf[1], *routers[1])
    state_t = jnp.transpose(state_pool[0], (1, 0, 2))
    x3_all, h2_all, ri2_all, rg2_all, tc2_all, hs1, cnt1 = _mix1_sample(
        x2_all, state_t, nm[1], wp, sc, nf[1], *routers[1], cnt1, x3_all, h2_all, ri2_all, rg2_all, tc2_all)
    cplan1, ys1 = _moe(h2_all, ri2_all, cnt1, tc2_all, w_gate, w_up, w_down, 1)
    y_p, y_s = _final(cplan1, x3_all, rg2_all, ys1, norm_final.reshape(1, D_MODEL))

    def from_tmajor(a, width):
        return jnp.transpose(a.reshape(DEC_SEQ, DEC_BATCH, width), (1, 0, 2))

    y_prompt = y_p.reshape(BATCH, SEQ, D_MODEL)
    y_sample = from_tmajor(y_s, D_MODEL)
    win_k_p = k_last.reshape(1, BATCH, WINDOW, B_KV_HEADS, B_HEAD_DIM)
    win_v_p = v_last.reshape(1, BATCH, WINDOW, B_KV_HEADS, B_HEAD_DIM)
    win_k_s = jnp.transpose(k_new, (0, 3, 1, 2))[None]
    win_v_s = jnp.transpose(v_new, (0, 3, 1, 2))[None]
    chunk_v_p = va_last.reshape(1, BATCH, CHUNK, A_HEADS, A_HEAD_DIM)
    chunk_v_s = from_tmajor(va_s, A_WIDTH).reshape(1, DEC_BATCH, DEC_SEQ, A_HEADS, A_HEAD_DIM)
    pool_p = pool_tail[:, 1:][None]
    pool_s = jnp.concatenate([state_pool[0][:, DEC_SEQ:], from_tmajor(hs1, D_MODEL)], axis=1)[None]
    return (y_prompt, y_sample, win_k_p, win_v_p, win_k_s, win_v_s, chunk_v_p, chunk_v_s, pool_p, pool_s)
```

```python
import functools
import math

import numpy as np
import jax
import jax.numpy as jnp
from jax import lax
from jax.experimental import pallas as pl
from jax.experimental.pallas import tpu as pltpu

F32 = jnp.float32
BF16 = jnp.bfloat16

D_MODEL = 1024
BATCH = 2
SEQ = 8192
DEC_BATCH = 128
DEC_SEQ = 4
A_WIDTH = 512
A_HEADS = 8
A_HEAD_DIM = 64
CHUNK = 128
B_HEADS = 8
B_KV_HEADS = 2
B_HEAD_DIM = 64
B_GROUP = 4
WINDOW = 128
N_BUCKETS = 32
MAX_DISTANCE = WINDOW
Q_WIDTH = 512
KV_WIDTH = 128
IN_WIDTH = 2 * A_WIDTH + Q_WIDTH + 2 * KV_WIDTH
ATTN_SCALE = B_HEAD_DIM ** -0.5
NEG_INF = -1e30
POOL_SIZES = (2, 4, 8, 16)
POOL_GROUP_DIM = 256
POOL_MAX = 16
N_GROUPS = 4
EXPERTS_PER_GROUP = 8
N_EXPERTS = 32
TOP_K = 2
D_EXPERT = 512
EPS = 1e-6

LANES = 128
ROW_TILE = D_MODEL // LANES
T_PROMPT = BATCH * SEQ
T_SAMPLE = DEC_BATCH * DEC_SEQ
T_ALL = T_PROMPT + T_SAMPLE
TM = 512
N_PROMPT_BLOCKS = T_PROMPT // TM
N_ROW_BLOCKS = T_ALL // TM
STEPS_PER_BATCH = SEQ // TM
SUB = TM // WINDOW
N_SLOTS = T_ALL * TOP_K
MOE_BLK = 512
N_MOE_BLOCKS = N_SLOTS // MOE_BLK + N_EXPERTS
N_SORT_ROWS = N_MOE_BLOCKS * MOE_BLK
SAMPLE_GROUP = 8
N_SAMPLE_GROUPS = DEC_BATCH // SAMPLE_GROUP
VMEM_LIMIT = 56 * 1024 * 1024

STACK_HEADS = ((0, 2, 5, 7), (1, 3, 4, 6))


def _t5_bucket_np(dist):
    n = np.maximum(dist, 0)
    max_exact = N_BUCKETS // 2
    nf = np.maximum(n, 1).astype(np.float32)
    large = max_exact + (np.log(nf / np.float32(max_exact)) / np.float32(math.log(MAX_DISTANCE / max_exact))
                         * np.float32(N_BUCKETS - max_exact)).astype(np.int32)
    large = np.minimum(large, N_BUCKETS - 1)
    return np.where(n < max_exact, n, large).astype(np.int32)


def _bucket_tables():
    qi = np.arange(WINDOW)[:, None]
    ki = np.arange(2 * WINDOW)[None, :]
    dist = qi + WINDOW - ki
    valid = (dist >= 0) & (dist < WINDOW)
    bp = np.where(valid, _t5_bucket_np(dist), -1)
    bp_first = np.where(ki >= WINDOW, bp, -1)
    bkt_p = np.stack([bp_first, bp]).astype(np.int32)

    t = np.repeat(np.arange(DEC_SEQ), SAMPLE_GROUP)[:, None]
    b = np.tile(np.arange(SAMPLE_GROUP), DEC_SEQ)[:, None]
    cb = np.repeat(np.arange(SAMPLE_GROUP), WINDOW)[None, :]
    cj = np.tile(np.arange(WINDOW), SAMPLE_GROUP)[None, :]
    dist_c = t + WINDOW - cj
    valid_c = (cb == b) & (dist_c >= 0) & (dist_c < WINDOW)
    bkt_sc = np.where(valid_c, _t5_bucket_np(dist_c), -1).astype(np.int32)
    nt = np.repeat(np.arange(DEC_SEQ), SAMPLE_GROUP)[None, :]
    nb = np.tile(np.arange(SAMPLE_GROUP), DEC_SEQ)[None, :]
    dist_n = t - nt
    valid_n = (nb == b) & (dist_n >= 0)
    bkt_sn = np.where(valid_n, _t5_bucket_np(dist_n), -1).astype(np.int32)
    bkt_sn = np.concatenate([bkt_sn, np.full((32, LANES - 32), -1, np.int32)], axis=1)
    return bkt_p, bkt_sc, bkt_sn


_BKT_P, _BKT_SC, _BKT_SN = _bucket_tables()


def _cparams(semantics):
    return pltpu.CompilerParams(dimension_semantics=semantics, vmem_limit_bytes=VMEM_LIMIT)


def _rms(x, g):
    return x * lax.rsqrt(jnp.mean(x * x, axis=-1, keepdims=True) + EPS) * g


def _layernorm(x, g, b):
    xc = x - jnp.mean(x, axis=-1, keepdims=True)
    return xc * lax.rsqrt(jnp.mean(xc * xc, axis=-1, keepdims=True) + EPS) * g + b


def _dot(a, b):
    return jnp.dot(a, b, preferred_element_type=F32)


def _dot_nt(a, b):
    return lax.dot_general(a, b, (((1,), (1,)), ((), ())), preferred_element_type=F32)


def _project(x, nm, win, lng, lnb):
    h = _rms(x, nm)
    z = _dot(h.astype(BF16), win)
    u = jax.nn.gelu(z[:, :A_WIDTH])
    va = _layernorm(jax.nn.gelu(z[:, A_WIDTH:2 * A_WIDTH]), lng, lnb)
    q = z[:, 2 * A_WIDTH:2 * A_WIDTH + Q_WIDTH] * ATTN_SCALE
    k = z[:, 2 * A_WIDTH + Q_WIDTH:2 * A_WIDTH + Q_WIDTH + KV_WIDTH]
    v = z[:, 2 * A_WIDTH + Q_WIDTH + KV_WIDTH:]
    return u, va, q, k, v


def _route(x1, nf, wr, br):
    hf = _rms(x1, nf)
    h = hf.astype(BF16)
    h_lo = (hf - h.astype(F32)).astype(BF16)
    part = _dot(h, wr)
    logits = part[:, :LANES] + part[:, LANES:] + _dot(h_lo, wr[:, :LANES]) + br
    rows = logits.shape[0]
    lane = lax.broadcasted_iota(jnp.int32, (rows, LANES), 1)
    lanef = lane.astype(F32)
    big = jnp.float32(1e9)
    is_g = lane < N_GROUPS
    gl = jnp.where(is_g, logits, -jnp.inf)
    gmax = jnp.max(gl, axis=1, keepdims=True)
    gsel = jnp.min(jnp.where(gl == gmax, lanef, big), axis=1, keepdims=True)
    gsum = jnp.sum(jnp.where(is_g, jnp.exp(logits - gmax), 0.0), axis=1, keepdims=True)
    g1 = 1.0 / gsum
    lo = N_GROUPS + EXPERTS_PER_GROUP * gsel
    emask = (lanef >= lo) & (lanef < lo + EXPERTS_PER_GROUP)
    el = jnp.where(emask, logits, -jnp.inf)
    v1 = jnp.max(el, axis=1, keepdims=True)
    i1 = jnp.min(jnp.where(el == v1, lanef, big), axis=1, keepdims=True)
    el2 = jnp.where(lanef == i1, -jnp.inf, el)
    v2 = jnp.max(el2, axis=1, keepdims=True)
    i2 = jnp.min(jnp.where(el2 == v2, lanef, big), axis=1, keepdims=True)
    e2 = jnp.exp(v2 - v1)
    den = 1.0 + e2
    w1 = g1 / den
    w2 = g1 * e2 / den
    ids = jnp.where(lane == 0, i1 - N_GROUPS, jnp.where(lane == 1, i2 - N_GROUPS, 0.0)).astype(jnp.int32)
    gates = jnp.where(lane == 0, w1, jnp.where(lane == 1, w2, 0.0))
    return h, ids, gates


def _rank_pack(ids, cnt_ref, tcnt_ref):
    rows = ids.shape[0]
    lane = lax.broadcasted_iota(jnp.int32, (rows, LANES), 1)
    o0 = (lane == ids[:, 0:1]).astype(F32)
    o1 = (lane == ids[:, 1:2]).astype(F32)
    r = lax.broadcasted_iota(jnp.int32, (rows, rows), 0)
    c = lax.broadcasted_iota(jnp.int32, (rows, rows), 1)
    before = (c < r).astype(BF16)
    p01 = _dot(before, jnp.concatenate([o0, o1], axis=1).astype(BF16))
    p0 = p01[:, :LANES]
    p1 = p01[:, LANES:]
    c0 = jnp.sum(o0, axis=0, keepdims=True)
    c1 = jnp.sum(o1, axis=0, keepdims=True)
    ctile = c0 + c1
    cnt_ref[...] = cnt_ref[...] + ctile
    tcnt_ref[...] = ctile
    inc = jnp.broadcast_to(ctile, (8, LANES))
    lane8 = lax.broadcasted_iota(jnp.int32, (8, LANES), 1)
    for sh in (1, 2, 4, 8, 16, 32, 64):
        inc = inc + jnp.where(lane8 >= sh, pltpu.roll(inc, sh, 1), 0.0)
    start = inc[0:1] - ctile
    lpos0 = jnp.sum(o0 * (start + p0), axis=1, keepdims=True)
    lpos1 = jnp.sum(o1 * (start + c0 + p1), axis=1, keepdims=True)
    idf = ids.astype(F32)
    packed = jnp.where(lane < TOP_K, idf, 0.0)
    for ln, col in ((4, lpos0), (5, lpos1)):
        packed = jnp.where(lane == ln, col, packed)
    return jnp.transpose(packed)[:8].astype(jnp.int32)


def _prep_kernel(tab_ref, sink_ref, bp_ref, bsc_ref, bsn_ref, ws_ref, op_ref, osc_ref, osn_ref, ows_ref):
    def fill(bkt, write, sink_col0):
        col0 = lax.broadcasted_iota(jnp.int32, bkt.shape, 1) == 0
        for st, heads in enumerate(STACK_HEADS):
            for slot, h in enumerate(heads):
                acc = jnp.full(bkt.shape, NEG_INF, F32)
                for b in range(N_BUCKETS):
                    acc = jnp.where(bkt == b, tab_ref[b, h], acc)
                if sink_col0:
                    acc = jnp.where(col0, sink_ref[0, h], acc)
                write(st, slot, acc)

    for var in range(2):
        def wr_p(st, slot, acc, var=var):
            op_ref[var, st, slot * WINDOW:(slot + 1) * WINDOW, :] = acc
        fill(bp_ref[var], wr_p, True)

    rows_s = DEC_SEQ * SAMPLE_GROUP

    def wr_sc(st, slot, acc):
        osc_ref[st, slot * rows_s:(slot + 1) * rows_s, :] = acc
    fill(bsc_ref[...], wr_sc, True)

    def wr_sn(st, slot, acc):
        osn_ref[st, slot * rows_s:(slot + 1) * rows_s, :] = acc
    fill(bsn_ref[...], wr_sn, False)

    r = lax.broadcasted_iota(jnp.int32, (CHUNK, CHUNK), 0)
    c = lax.broadcasted_iota(jnp.int32, (CHUNK, CHUNK), 1)
    for h in range(A_HEADS):
        ows_ref[h // 2, :, (h % 2) * CHUNK:(h % 2 + 1) * CHUNK] = jnp.where(r >= c, ws_ref[h], 0.0).astype(BF16)


def _prep(rel_bias_table, sinks, w_s):
    vm = pl.BlockSpec(memory_space=pltpu.VMEM)
    sm = pl.BlockSpec(memory_space=pltpu.SMEM)
    rows_s = DEC_SEQ * SAMPLE_GROUP
    return pl.pallas_call(
        _prep_kernel,
        in_specs=[sm, sm, vm, vm, vm, vm],
        out_specs=[vm, vm, vm, vm],
        out_shape=[
            jax.ShapeDtypeStruct((2, 2, 4 * WINDOW, 2 * WINDOW), F32),
            jax.ShapeDtypeStruct((2, 4 * rows_s, SAMPLE_GROUP * WINDOW), F32),
            jax.ShapeDtypeStruct((2, 4 * rows_s, LANES), F32),
            jax.ShapeDtypeStruct((A_HEADS // 2, CHUNK, 2 * CHUNK), BF16),
        ],
        name="prep_tables",
    )(rel_bias_table, sinks.reshape(1, B_HEADS), jnp.asarray(_BKT_P), jnp.asarray(_BKT_SC), jnp.asarray(_BKT_SN), w_s)


def _gate_pairs(va_rows, wsp_ref, lane_lo):
    outs = []
    for p in range(A_HEADS // 2):
        vp = va_rows[:, p * LANES:(p + 1) * LANES]
        rhs = jnp.concatenate([jnp.where(lane_lo, vp, 0.0), jnp.where(lane_lo, 0.0, vp)], axis=0).astype(BF16)
        outs.append(_dot(wsp_ref[p], rhs))
    return jnp.concatenate(outs, axis=1)


def _prompt_steps(body, first_row_out):
    def kern(*refs):
        i = pl.program_id(0)

        @pl.when(i < N_PROMPT_BLOCKS)
        def _():
            body(*refs)

        @pl.when(i >= N_PROMPT_BLOCKS)
        def _():
            for r in refs[first_row_out:first_row_out + 5]:
                r[...] = jnp.zeros(r.shape, r.dtype)

    return kern


def _mix0_prompt_kernel(x_ref, nm_ref, win_ref, lng_ref, lnb_ref, wsp_ref, bs_ref, bias_ref,
                        wout_ref, nf_ref, wr_ref, br_ref,
                        x1_ref, h_ref, ri_ref, rg_ref, tc_ref, kl_ref, vl_ref, val_ref, cnt_ref,
                        kprev, vprev, mix_scr):
    @pl.when(pl.program_id(0) == 0)
    def _():
        cnt_ref[...] = jnp.zeros_like(cnt_ref)

    x = x_ref[...]
    u, va, q, k, v = _project(x, nm_ref[...], win_ref[...], lng_ref[...], lnb_ref[...])
    lane_lo = lax.broadcasted_iota(jnp.int32, (WINDOW, LANES), 1) < B_HEAD_DIM
    row0 = lax.broadcasted_iota(jnp.int32, (WINDOW, KV_WIDTH), 0) == 0
    first = pl.program_id(0) % STEPS_PER_BATCH == 0

    @pl.when(first)
    def _():
        kprev[...] = jnp.zeros_like(kprev)
        vprev[...] = jnp.zeros_like(vprev)

    for j in range(SUB):
        rows = slice(j * WINDOW, (j + 1) * WINDOW)
        s_gate = _gate_pairs(va[rows], wsp_ref, lane_lo)
        mix_scr[rows, :A_WIDTH] = u[rows] * (s_gate + bs_ref[...])

        if j == 0:
            kp, vp = kprev[...], vprev[...]
        else:
            prows = slice((j - 1) * WINDOW, j * WINDOW)
            kp, vp = k[prows], v[prows]
        kk = jnp.concatenate([jnp.where(row0, 0.0, kp), k[rows]], axis=0)
        vv = jnp.concatenate([jnp.where(row0, 0.0, vp), v[rows]], axis=0)
        kops = (kk.astype(BF16), pltpu.roll(kk, B_HEAD_DIM, 1).astype(BF16))
        vops = (vv.astype(BF16), pltpu.roll(vv, B_HEAD_DIM, 1).astype(BF16))
        qt = [q[rows, p * LANES:(p + 1) * LANES] for p in range(4)]
        q_even = [jnp.where(lane_lo, t, 0.0) for t in qt]
        q_odd = [jnp.where(lane_lo, 0.0, t) for t in qt]
        stacks = (jnp.concatenate([q_even[0], q_even[1], q_odd[2], q_odd[3]], axis=0),
                  jnp.concatenate([q_odd[0], q_odd[1], q_even[2], q_even[3]], axis=0))
        o = []
        for st in range(2):
            s = _dot_nt(stacks[st].astype(BF16), kops[st])
            if j == 0:
                bias = bias_ref[jnp.where(first, 0, 1), st]
            else:
                bias = bias_ref[1, st]
            s = s + bias
            m = jnp.max(s, axis=-1, keepdims=True)
            p = jnp.exp(s - m)
            den = jnp.sum(p, axis=-1, keepdims=True)
            o.append(_dot(p.astype(BF16), vops[st]) / den)
        oa, ob = o
        sl = [slice(i * WINDOW, (i + 1) * WINDOW) for i in range(4)]
        tiles = (jnp.where(lane_lo, oa[sl[0]], ob[sl[0]]), jnp.where(lane_lo, oa[sl[1]], ob[sl[1]]),
                 jnp.where(lane_lo, ob[sl[2]], oa[sl[2]]), jnp.where(lane_lo, ob[sl[3]], oa[sl[3]]))
        for p in range(4):
            mix_scr[rows, A_WIDTH + p * LANES:A_WIDTH + (p + 1) * LANES] = tiles[p]

    last = slice(TM - WINDOW, TM)
    kprev[...] = k[last]
    vprev[...] = v[last]
    kl_ref[...] = k[last]
    vl_ref[...] = v[last]
    val_ref[...] = va[last]

    x1 = x + _dot(mix_scr[...].astype(BF16), wout_ref[...])
    x1_ref[...] = x1
    h, ids, gates = _route(x1, nf_ref[...], wr_ref[...], br_ref[...])
    h_ref[...] = h.reshape(h_ref.shape)
    ri_ref[...] = _rank_pack(ids, cnt_ref, tc_ref)
    rg_ref[...] = gates


def _const_spec(shape):
    nd = len(shape)
    return pl.BlockSpec(shape, lambda i, _n=nd: (0,) * _n)


def _mix0_prompt(x_all, nm, win, lng, lnb, wsp, bs_full, bias_p, wout, nf, wr, br):
    row_spec = pl.BlockSpec((TM, D_MODEL), lambda i: (i, 0))
    row3_spec = pl.BlockSpec((TM, ROW_TILE, LANES), lambda i: (i, 0, 0))
    lane_spec = pl.BlockSpec((TM, LANES), lambda i: (i, 0))
    last_kv = pl.BlockSpec((None, WINDOW, KV_WIDTH), lambda i: (jnp.minimum(i // STEPS_PER_BATCH, BATCH - 1), 0, 0))
    last_va = pl.BlockSpec((None, WINDOW, A_WIDTH), lambda i: (jnp.minimum(i // STEPS_PER_BATCH, BATCH - 1), 0, 0))
    return pl.pallas_call(
        _prompt_steps(_mix0_prompt_kernel, 12),
        grid=(N_ROW_BLOCKS,),
        in_specs=[pl.BlockSpec((TM, D_MODEL), lambda i: (jnp.minimum(i, N_PROMPT_BLOCKS - 1), 0)),
                  _const_spec((1, D_MODEL)), _const_spec((D_MODEL, IN_WIDTH)),
                  _const_spec((1, A_WIDTH)), _const_spec((1, A_WIDTH)),
                  _const_spec((A_HEADS // 2, CHUNK, 2 * CHUNK)), _const_spec((CHUNK, A_WIDTH)),
                  _const_spec((2, 2, 4 * WINDOW, 2 * WINDOW)),
                  _const_spec((A_WIDTH + Q_WIDTH, D_MODEL)), _const_spec((1, D_MODEL)),
                  _const_spec((D_MODEL, 2 * LANES)), _const_spec((1, LANES))],
        out_specs=[row_spec, row3_spec, pl.BlockSpec((8, TM), lambda i: (0, i)), lane_spec,
                   pl.BlockSpec((None, 1, LANES), lambda i: (i, 0, 0)),
                   last_kv, last_kv, last_va, _const_spec((1, LANES))],
        out_shape=[jax.ShapeDtypeStruct((T_ALL, D_MODEL), F32), jax.ShapeDtypeStruct((T_ALL, ROW_TILE, LANES), BF16),
                   jax.ShapeDtypeStruct((8, T_ALL), jnp.int32), jax.ShapeDtypeStruct((T_ALL, LANES), F32),
                   jax.ShapeDtypeStruct((N_ROW_BLOCKS, 1, LANES), F32),
                   jax.ShapeDtypeStruct((BATCH, WINDOW, KV_WIDTH), F32),
                   jax.ShapeDtypeStruct((BATCH, WINDOW, KV_WIDTH), F32),
                   jax.ShapeDtypeStruct((BATCH, WINDOW, A_WIDTH), F32),
                   jax.ShapeDtypeStruct((1, LANES), F32)],
        scratch_shapes=[pltpu.VMEM((WINDOW, KV_WIDTH), F32), pltpu.VMEM((WINDOW, KV_WIDTH), F32),
                        pltpu.VMEM((TM, D_MODEL), F32)],
        compiler_params=_cparams(("arbitrary",)),
        name="mix0_prompt",
    )(x_all, nm, win, lng, lnb, wsp, bs_full, bias_p, wout, nf, wr, br)


def _mix0_sample_kernel(x_ref, nm_ref, win_ref, lng_ref, lnb_ref, wcoef_ref, bcoef_ref,
                        ck_ref, cv_ref, bsc_ref, bsn_ref,
                        wout_ref, nf_ref, wr_ref, br_ref, cnt_in,
                        x1_in, h_in, ri_in, rg_in, tc_in,
                        x1_ref, h_ref, ri_ref, rg_ref, tc_ref, kn_ref, vn_ref, va_ref, cnt_ref,
                        q_scr, k_scr, v_scr, mix_scr):
    del x1_in, h_in, ri_in, rg_in, tc_in
    g = pl.program_id(0)

    @pl.when(g == 0)
    def _():
        u, va, q, k, v = _project(x_ref[...], nm_ref[...], win_ref[...], lng_ref[...], lnb_ref[...])
        q_scr[...] = q
        k_scr[...] = k
        v_scr[...] = v
        va_ref[...] = va
        for t in range(DEC_SEQ):
            acc = jnp.zeros((DEC_BATCH, A_WIDTH), F32) + bcoef_ref[t:t + 1, :]
            for s in range(t + 1):
                row = t * DEC_SEQ + s
                acc = acc + wcoef_ref[row:row + 1, :] * va[s * DEC_BATCH:(s + 1) * DEC_BATCH]
            mix_scr[t * DEC_BATCH:(t + 1) * DEC_BATCH, :A_WIDTH] = u[t * DEC_BATCH:(t + 1) * DEC_BATCH] * acc

    b0 = pl.multiple_of(g * SAMPLE_GROUP, SAMPLE_GROUP)
    lane_lo = lax.broadcasted_iota(jnp.int32, (DEC_SEQ * SAMPLE_GROUP, LANES), 1) < B_HEAD_DIM

    def grab(ref, width):
        return jnp.concatenate([ref[pl.ds(t * DEC_BATCH + b0, SAMPLE_GROUP), :] for t in range(DEC_SEQ)], axis=0)

    qg = grab(q_scr, Q_WIDTH)
    kn = grab(k_scr, KV_WIDTH)
    vn = grab(v_scr, KV_WIDTH)

    lane_w = lax.broadcasted_iota(jnp.int32, (KV_WIDTH, WINDOW), 1)
    n_new = DEC_SEQ * SAMPLE_GROUP

    def new_window(c_ref, new_rows, w_ref):
        nt = jnp.transpose(jnp.concatenate([new_rows, jnp.zeros((WINDOW - n_new, KV_WIDTH), F32)], axis=0))
        for b in range(SAMPLE_GROUP):
            w = pltpu.roll(c_ref[b].reshape(KV_WIDTH, WINDOW), WINDOW - DEC_SEQ, 1)
            for t in range(DEC_SEQ):
                src = t * SAMPLE_GROUP + b
                dst = WINDOW - DEC_SEQ + t
                w = jnp.where(lane_w == dst, pltpu.roll(nt, (dst - src) % WINDOW, 1), w)
            w_ref[b] = w.reshape(B_KV_HEADS, B_HEAD_DIM, WINDOW)

    new_window(ck_ref, kn, kn_ref)
    new_window(cv_ref, vn, vn_ref)
    ccol0 = lax.broadcasted_iota(jnp.int32, (KV_WIDTH, SAMPLE_GROUP * WINDOW), 1) == 0

    def cache_t(ref):
        t = jnp.concatenate([ref[b].reshape(KV_WIDTH, WINDOW) for b in range(SAMPLE_GROUP)], axis=1)
        return jnp.where(ccol0, 0.0, t)

    def head_swap(t):
        return jnp.concatenate([t[B_HEAD_DIM:], t[:B_HEAD_DIM]], axis=0)

    kct = cache_t(ck_ref)
    vct = cache_t(cv_ref)
    kc_ops = (kct.astype(BF16), head_swap(kct).astype(BF16))
    vc_ops = (vct.astype(BF16), head_swap(vct).astype(BF16))
    kn_ops = (kn.astype(BF16), pltpu.roll(kn, B_HEAD_DIM, 1).astype(BF16))
    vn_ops = (vn.astype(BF16), pltpu.roll(vn, B_HEAD_DIM, 1).astype(BF16))
    qt = [qg[:, p * LANES:(p + 1) * LANES] for p in range(4)]
    q_even = [jnp.where(lane_lo, t, 0.0) for t in qt]
    q_odd = [jnp.where(lane_lo, 0.0, t) for t in qt]
    stacks = (jnp.concatenate([q_even[0], q_even[1], q_odd[2], q_odd[3]], axis=0),
              jnp.concatenate([q_odd[0], q_odd[1], q_even[2], q_even[3]], axis=0))
    o = []
    for st in range(2):
        qs = stacks[st].astype(BF16)
        sc = _dot(qs, kc_ops[st]) + bsc_ref[st]
        sn = _dot_nt(qs, kn_ops[st]) + bsn_ref[st][:, :DEC_SEQ * SAMPLE_GROUP]
        m = jnp.maximum(jnp.max(sc, axis=-1, keepdims=True), jnp.max(sn, axis=-1, keepdims=True))
        pc = jnp.exp(sc - m)
        pn = jnp.exp(sn - m)
        den = jnp.sum(pc, axis=-1, keepdims=True) + jnp.sum(pn, axis=-1, keepdims=True)
        o.append((_dot_nt(pc.astype(BF16), vc_ops[st]) + _dot(pn.astype(BF16), vn_ops[st])) / den)
    oa, ob = o
    n = DEC_SEQ * SAMPLE_GROUP
    sl = [slice(i * n, (i + 1) * n) for i in range(4)]
    tiles = (jnp.where(lane_lo, oa[sl[0]], ob[sl[0]]), jnp.where(lane_lo, oa[sl[1]], ob[sl[1]]),
             jnp.where(lane_lo, ob[sl[2]], oa[sl[2]]), jnp.where(lane_lo, ob[sl[3]], oa[sl[3]]))
    for p in range(4):
        for t in range(DEC_SEQ):
            mix_scr[pl.ds(t * DEC_BATCH + b0, SAMPLE_GROUP), A_WIDTH + p * LANES:A_WIDTH + (p + 1) * LANES] = (
                tiles[p][t * SAMPLE_GROUP:(t + 1) * SAMPLE_GROUP])

    @pl.when(g == N_SAMPLE_GROUPS - 1)
    def _():
        x1 = x_ref[...] + _dot(mix_scr[...].astype(BF16), wout_ref[...])
        x1_ref[...] = x1
        h, ids, gates = _route(x1, nf_ref[...], wr_ref[...], br_ref[...])
        h_ref[...] = h.reshape(h_ref.shape)
        cnt_ref[...] = cnt_in[...]
        ri_ref[...] = _rank_pack(ids, cnt_ref, tc_ref)
        rg_ref[...] = gates


def _mix0_sample(x_all, nm, win, lng, lnb, wcoef, bcoef, ck, cv, bias_sc, bias_sn, wout, nf, wr, br, cnt,
                 x1_all, h_all, ri_all, rg_all, tc_all):
    sample_rows = pl.BlockSpec((TM, D_MODEL), lambda g: (N_PROMPT_BLOCKS, 0))
    sample_rows3 = pl.BlockSpec((TM, ROW_TILE, LANES), lambda g: (N_PROMPT_BLOCKS, 0, 0))
    sample_lanes = pl.BlockSpec((TM, LANES), lambda g: (N_PROMPT_BLOCKS, 0))
    cache_spec = pl.BlockSpec((SAMPLE_GROUP, B_KV_HEADS, B_HEAD_DIM, WINDOW), lambda g: (g, 0, 0, 0))
    anyspec = pl.BlockSpec(memory_space=pl.ANY)
    n_in = 16
    return pl.pallas_call(
        _mix0_sample_kernel,
        grid=(N_SAMPLE_GROUPS,),
        in_specs=[_const_spec((TM, D_MODEL)), _const_spec((1, D_MODEL)), _const_spec((D_MODEL, IN_WIDTH)),
                  _const_spec((1, A_WIDTH)), _const_spec((1, A_WIDTH)),
                  _const_spec((16, A_WIDTH)), _const_spec((8, A_WIDTH)),
                  cache_spec, cache_spec,
                  _const_spec((2, 4 * 32, SAMPLE_GROUP * WINDOW)), _const_spec((2, 4 * 32, LANES)),
                  _const_spec((A_WIDTH + Q_WIDTH, D_MODEL)), _const_spec((1, D_MODEL)),
                  _const_spec((D_MODEL, 2 * LANES)), _const_spec((1, LANES)), _const_spec((1, LANES)),
                  anyspec, anyspec, anyspec, anyspec, anyspec],
        out_specs=[sample_rows, sample_rows3, pl.BlockSpec((8, TM), lambda g: (0, N_PROMPT_BLOCKS)), sample_lanes,
                   pl.BlockSpec((None, 1, LANES), lambda g: (N_PROMPT_BLOCKS, 0, 0)),
                   cache_spec, cache_spec,
                   _const_spec((T_SAMPLE, A_WIDTH)), _const_spec((1, LANES))],
        out_shape=[jax.ShapeDtypeStruct((T_ALL, D_MODEL), F32), jax.ShapeDtypeStruct((T_ALL, ROW_TILE, LANES), BF16),
                   jax.ShapeDtypeStruct((8, T_ALL), jnp.int32), jax.ShapeDtypeStruct((T_ALL, LANES), F32),
                   jax.ShapeDtypeStruct((N_ROW_BLOCKS, 1, LANES), F32),
                   jax.ShapeDtypeStruct((DEC_BATCH, B_KV_HEADS, B_HEAD_DIM, WINDOW), F32),
                   jax.ShapeDtypeStruct((DEC_BATCH, B_KV_HEADS, B_HEAD_DIM, WINDOW), F32),
                   jax.ShapeDtypeStruct((T_SAMPLE, A_WIDTH), F32), jax.ShapeDtypeStruct((1, LANES), F32)],
        scratch_shapes=[pltpu.VMEM((T_SAMPLE, Q_WIDTH), F32), pltpu.VMEM((T_SAMPLE, KV_WIDTH), F32),
                        pltpu.VMEM((T_SAMPLE, KV_WIDTH), F32), pltpu.VMEM((T_SAMPLE, D_MODEL), F32)],
        input_output_aliases={n_in: 0, n_in + 1: 1, n_in + 2: 2, n_in + 3: 3, n_in + 4: 4},
        compiler_params=_cparams(("arbitrary",)),
        name="mix0_sample",
    )(x_all, nm, win, lng, lnb, wcoef, bcoef, ck, cv, bias_sc, bias_sn, wout, nf, wr, br, cnt,
      x1_all, h_all, ri_all, rg_all, tc_all)


def _moe_metadata(rt_all, cnt, tcnt):
    counts = cnt[0, :N_EXPERTS].astype(jnp.int32)
    padded = (counts + MOE_BLK - 1) // MOE_BLK * MOE_BLK
    pad_end = jnp.cumsum(padded)
    pad_start = pad_end - padded
    experts = jnp.arange(N_EXPERTS, dtype=jnp.int32)
    n_valid = (pad_end[-1] // MOE_BLK).astype(jnp.int32).reshape(1)
    blk_start = jnp.arange(N_MOE_BLOCKS, dtype=jnp.int32) * MOE_BLK
    block_e = jnp.minimum(jnp.sum((blk_start[:, None] >= pad_end[None, :]).astype(jnp.int32), axis=1),
                          N_EXPERTS - 1).astype(jnp.int32)
    zero_start = (pad_start + counts).astype(jnp.int32)
    zero_len = (padded - counts).astype(jnp.int32)
    first = (blk_start == pad_start[block_e]).astype(jnp.int32)
    used = counts > 0
    parity = ((jnp.cumsum(used.astype(jnp.int32)) - 1) % 2)[block_e].astype(jnp.int32)
    nearest = lax.cummin(jnp.where(used, experts, N_EXPERTS)[::-1])[::-1]
    next_used = jnp.concatenate([nearest[1:], jnp.full((1,), N_EXPERTS, jnp.int32)])
    nxt = jnp.where(next_used < N_EXPERTS, next_used, -1)[block_e].astype(jnp.int32)
    plan = (block_e, first, parity, nxt, n_valid)
    runs = tcnt[:, 0, :N_EXPERTS].astype(jnp.int32)
    gruns = runs.reshape(N_DISPATCH_STEPS, DISPATCH_TILES, N_EXPERTS)
    gtot = jnp.sum(gruns, axis=1)
    gstart = jnp.cumsum(gtot, axis=1) - gtot
    shift = ((gstart[:, None, :] + jnp.cumsum(gruns, axis=1) - gruns).reshape(N_ROW_BLOCKS, N_EXPERTS)
             - (jnp.cumsum(runs, axis=1) - runs))
    shift_rows = jnp.repeat(jnp.transpose(shift), TM, axis=1)
    hit = rt_all[:TOP_K, None, :] == experts[None, :, None]
    gpos = (rt_all[2 * TOP_K:3 * TOP_K] + jnp.sum(jnp.where(hit, shift_rows[None], 0), axis=1)).reshape(N_SLOTS)
    grun_dst = pad_start[None, :] + jnp.cumsum(gtot, axis=0) - gtot
    cplan = (gpos.astype(jnp.int32), gtot.reshape(-1), grun_dst.reshape(-1).astype(jnp.int32))
    dplan = cplan + (jnp.concatenate([zero_start, zero_len, n_valid]),)
    return plan, dplan, cplan


RUN_PIECE = 32
DISPATCH_TILES = 3
DISPATCH_ROWS = DISPATCH_TILES * TM
N_DISPATCH_STEPS = N_ROW_BLOCKS // DISPATCH_TILES


def _for_run_pieces(n, start_piece):
    whole = n // RUN_PIECE

    def body(j, carry):
        start_piece(j * RUN_PIECE, RUN_PIECE)
        return carry

    lax.fori_loop(0, whole, body, 0)
    o = whole * RUN_PIECE
    bit = RUN_PIECE // 2
    while bit >= 1:
        take = (n & bit) != 0

        @pl.when(take)
        def _(o=o, bit=bit):
            start_piece(o, bit)

        o = o + jnp.where(take, bit, 0)
        bit //= 2


def _dispatch_kernel(run_ref, rdst_ref, zs_ref, h_ref, gpos_hbm, xs_ref, zero_scr, stage, pos_s, sem, zsem, psem):
    i = pl.program_id(0)

    def pos_copy(step, do):
        for kk in range(TOP_K):
            do(pltpu.make_async_copy(gpos_hbm.at[pl.ds(kk * T_ALL + step * DISPATCH_ROWS, DISPATCH_ROWS)],
                                     pos_s.at[pl.ds(kk * DISPATCH_ROWS, DISPATCH_ROWS)], psem))

    @pl.when(i == 0)
    def _():
        pos_copy(0, lambda cp: cp.start())
        zero_scr[...] = jnp.zeros_like(zero_scr)

        def pieces(e, do):
            off = zs_ref[e]
            rem = zs_ref[N_EXPERTS + e]
            bit = MOE_BLK // 2
            while bit >= 1:
                take = (rem & bit) != 0

                @pl.when(take)
                def _(off=off, bit=bit):
                    do(pltpu.make_async_copy(zero_scr.at[pl.ds(0, bit)], xs_ref.at[pl.ds(off, bit)], zsem))

                off = off + jnp.where(take, bit, 0)
                bit //= 2

        def start_e(e, c):
            pieces(e, lambda cp: cp.start())
            return c

        def wait_e(e, c):
            pieces(e, lambda cp: cp.wait())
            return c

        def tail(do):
            def step(b, c):
                do(pltpu.make_async_copy(zero_scr, xs_ref.at[pl.ds(b * MOE_BLK, MOE_BLK)], zsem))
                return c
            return step

        n_valid = zs_ref[2 * N_EXPERTS]
        lax.fori_loop(0, N_EXPERTS, start_e, 0)
        lax.fori_loop(n_valid, N_MOE_BLOCKS, tail(lambda cp: cp.start()), 0)
        lax.fori_loop(0, N_EXPERTS, wait_e, 0)
        lax.fori_loop(n_valid, N_MOE_BLOCKS, tail(lambda cp: cp.wait()), 0)

    slot = i % 2
    pos_copy(i, lambda cp: cp.wait())

    for s in range(2):
        @pl.when(slot == s)
        def _(s=s):
            def place(r, carry):
                row = h_ref[r]
                for kk in range(TOP_K):
                    stage[s, pos_s[kk * DISPATCH_ROWS + r]] = row
                return carry

            lax.fori_loop(0, DISPATCH_ROWS, place, 0, unroll=32)

    @pl.when(i + 1 < N_DISPATCH_STEPS)
    def _():
        pos_copy(i + 1, lambda cp: cp.start())

    def send_run(e, off):
        n = run_ref[i * N_EXPERTS + e]
        dst = rdst_ref[i * N_EXPERTS + e]
        _for_run_pieces(n, lambda o, size: pltpu.make_async_copy(
            stage.at[slot, pl.ds(off + o, size)], xs_ref.at[pl.ds(dst + o, size)], sem.at[slot]).start(
                priority=size.bit_length() % 2))
        return off + n

    lax.fori_loop(0, N_EXPERTS, send_run, 0)

    def drain(s):
        pltpu.make_async_copy(stage.at[s], xs_ref.at[pl.ds(0, DISPATCH_ROWS * TOP_K)], sem.at[s]).wait()

    @pl.when(i >= 1)
    def _():
        drain(1 - slot)

    @pl.when(i == N_DISPATCH_STEPS - 1)
    def _():
        drain(slot)


def _dispatch(dplan, h_all):
    return pl.pallas_call(
        _dispatch_kernel,
        grid_spec=pltpu.PrefetchScalarGridSpec(
            num_scalar_prefetch=3,
            grid=(N_DISPATCH_STEPS,),
            in_specs=[pl.BlockSpec((DISPATCH_ROWS, ROW_TILE, LANES), lambda i, rn, rd, z: (i, 0, 0)),
                      pl.BlockSpec(memory_space=pl.ANY)],
            out_specs=pl.BlockSpec(memory_space=pl.ANY),
            scratch_shapes=[pltpu.VMEM((MOE_BLK, ROW_TILE, LANES), BF16),
                            pltpu.VMEM((2, DISPATCH_ROWS * TOP_K, ROW_TILE, LANES), BF16),
                            pltpu.SMEM((TOP_K * DISPATCH_ROWS,), jnp.int32),
                            pltpu.SemaphoreType.DMA((2,)), pltpu.SemaphoreType.DMA(()),
                            pltpu.SemaphoreType.DMA(())],
        ),
        out_shape=jax.ShapeDtypeStruct((N_SORT_ROWS, ROW_TILE, LANES), BF16),
        compiler_params=_cparams(("arbitrary",)),
        name="moe_dispatch",
    )(*dplan[1:], h_all, dplan[0])


def _experts_kernel(layer, be_ref, first_ref, par_ref, nxt_ref, nv_ref,
                    x_ref, wg_hbm, wu_hbm, wd_hbm, y_ref,
                    wg_s, wu_s, wd_s, wg_f, wu_f, wd_f, wsem):
    i = pl.program_id(0)

    def fetch(e, slot):
        return (pltpu.make_async_copy(wg_hbm.at[layer, e], wg_f.at[slot], wsem.at[slot]),
                pltpu.make_async_copy(wu_hbm.at[layer, e], wu_f.at[slot], wsem.at[slot]),
                pltpu.make_async_copy(wd_hbm.at[layer, e], wd_f.at[slot], wsem.at[slot]))

    @pl.when(i < nv_ref[0])
    def _():
        e = be_ref[i]
        slot = par_ref[i]

        @pl.when(i == 0)
        def _():
            for cp in fetch(e, slot):
                cp.start()

        @pl.when(first_ref[i] == 1)
        def _():
            for cp in fetch(e, slot):
                cp.wait()
            wg_s[...] = wg_f[slot].astype(BF16)
            wu_s[...] = wu_f[slot].astype(BF16)
            wd_s[...] = wd_f[slot].astype(BF16)
            nxt = nxt_ref[i]

            @pl.when(nxt >= 0)
            def _():
                for cp in fetch(nxt, 1 - slot):
                    cp.start()

        xb = x_ref[...].reshape(MOE_BLK, D_MODEL)
        a = jax.nn.silu(_dot(xb, wg_s[...])) * _dot(xb, wu_s[...])
        y_ref[...] = _dot(a.astype(BF16), wd_s[...]).reshape(y_ref.shape)

    @pl.when(i >= nv_ref[0])
    def _():
        y_ref[...] = jnp.zeros(y_ref.shape, y_ref.dtype)


def _experts(block_e, first, parity, nxt, n_valid, xs, w_gate, w_up, w_down, layer):
    def blk(i, be, fi, pa, nx, nv):
        return (jnp.maximum(jnp.minimum(i, nv[0] - 1), 0), 0, 0)

    anyspec = pl.BlockSpec(memory_space=pl.ANY)
    return pl.pallas_call(
        functools.partial(_experts_kernel, layer),
        grid_spec=pltpu.PrefetchScalarGridSpec(
            num_scalar_prefetch=5,
            grid=(N_MOE_BLOCKS,),
            in_specs=[pl.BlockSpec((MOE_BLK, ROW_TILE, LANES), blk), anyspec, anyspec, anyspec],
            out_specs=pl.BlockSpec((MOE_BLK, ROW_TILE, LANES), lambda i, be, fi, pa, nx, nv: (i, 0, 0)),
            scratch_shapes=[pltpu.VMEM((D_MODEL, D_EXPERT), BF16), pltpu.VMEM((D_MODEL, D_EXPERT), BF16),
                            pltpu.VMEM((D_EXPERT, D_MODEL), BF16),
                            pltpu.VMEM((2, D_MODEL, D_EXPERT), F32), pltpu.VMEM((2, D_MODEL, D_EXPERT), F32),
                            pltpu.VMEM((2, D_EXPERT, D_MODEL), F32), pltpu.SemaphoreType.DMA((2,))],
        ),
        out_shape=jax.ShapeDtypeStruct((N_SORT_ROWS, ROW_TILE, LANES), F32),
        compiler_params=_cparams(("arbitrary",)),
        name="moe_experts",
    )(block_e, first, parity, nxt, n_valid, xs, w_gate, w_up, w_down)


def _gather_rows(lpos_ref, run_ref, rdst_ref, ys_ref, ystage, ybufs, sem, i):
    def fetch(group, buf):
        def fetch_run(e, off):
            n = run_ref[group * N_EXPERTS + e]
            src = rdst_ref[group * N_EXPERTS + e]
            _for_run_pieces(n, lambda o, size: pltpu.make_async_copy(
                ys_ref.at[pl.ds(src + o, size)], ystage.at[buf, pl.ds(off + o, size)], sem.at[buf]).start(
                    priority=size.bit_length() % 2))
            return off + n

        lax.fori_loop(0, N_EXPERTS, fetch_run, 0)

    def wait(buf):
        pltpu.make_async_copy(ys_ref.at[pl.ds(0, DISPATCH_ROWS * TOP_K)], ystage.at[buf], sem.at[buf]).wait()

    cur = i % 2
    group = i // DISPATCH_TILES
    phase = i % DISPATCH_TILES
    last_of_group = phase == DISPATCH_TILES - 1

    @pl.when(i == 0)
    def _():
        fetch(0, 0)
        wait(0)

        def unplace(r, carry):
            for kk in range(TOP_K):
                ybufs[0][kk, r] = ystage[0, lpos_ref[kk * T_ALL + r]]
            return carry

        lax.fori_loop(0, TM, unplace, 0, unroll=8)
        fetch(1, 1)

    @pl.when(last_of_group & (group + 1 < N_DISPATCH_STEPS))
    def _():
        wait((group + 1) % 2)

    @pl.when(last_of_group & (group + 2 < N_DISPATCH_STEPS))
    def _():
        fetch(group + 2, group % 2)

    def pieces(compute, store):
        nxt = jnp.minimum(i + 1, N_ROW_BLOCKS - 1)
        nslot = (nxt // DISPATCH_TILES) % 2

        def variant(par):
            ycur, ynext = ybufs[par], ybufs[1 - par]

            def piece(j, carry):
                rows = pl.ds(pl.multiple_of(j * COMBINE_ROWS, COMBINE_ROWS), COMBINE_ROWS)
                out = compute(rows, ycur[0, rows].reshape(COMBINE_ROWS, D_MODEL),
                              ycur[1, rows].reshape(COMBINE_ROWS, D_MODEL))
                base = nxt * TM + j * COMBINE_ROWS
                for r in range(COMBINE_ROWS):
                    for kk in range(TOP_K):
                        ynext[kk, j * COMBINE_ROWS + r] = ystage[nslot, lpos_ref[kk * T_ALL + base + r]]
                store(rows, out)
                return carry

            lax.fori_loop(0, TM // COMBINE_ROWS, piece, 0)

        for par in range(2):
            @pl.when(cur == par)
            def _(par=par):
                variant(par)

    return pieces


COMBINE_ROWS = 64


def _combined(x_ref, rg_ref, rows, y0, y1):
    rg = rg_ref[rows, :]
    return x_ref[rows, :] + rg[:, 0:1] * y0 + rg[:, 1:2] * y1


_COMBINE_SCRATCH = [pltpu.VMEM((2, DISPATCH_ROWS * TOP_K, ROW_TILE, LANES), F32),
                    pltpu.VMEM((TOP_K, TM, ROW_TILE, LANES), F32), pltpu.VMEM((TOP_K, TM, ROW_TILE, LANES), F32),
                    pltpu.SemaphoreType.DMA((2,))]


def _combine_kernel(lpos_ref, run_ref, rdst_ref, x_ref, rg_ref, ys_ref, o_ref, ystage, ybuf0, ybuf1, sem):
    pieces = _gather_rows(lpos_ref, run_ref, rdst_ref, ys_ref, ystage, (ybuf0, ybuf1), sem, pl.program_id(0))

    def store(rows, out):
        o_ref[rows, :] = out

    pieces(functools.partial(_combined, x_ref, rg_ref), store)


def _combine(cplan, x_all, rg_all, ys):
    return pl.pallas_call(
        _combine_kernel,
        grid_spec=pltpu.PrefetchScalarGridSpec(
            num_scalar_prefetch=3,
            grid=(N_ROW_BLOCKS,),
            in_specs=[pl.BlockSpec((TM, D_MODEL), lambda i, a, b, c: (i, 0)),
                      pl.BlockSpec((TM, LANES), lambda i, a, b, c: (i, 0)),
                      pl.BlockSpec(memory_space=pl.ANY)],
            out_specs=pl.BlockSpec((TM, D_MODEL), lambda i, a, b, c: (i, 0)),
            scratch_shapes=_COMBINE_SCRATCH,
        ),
        out_shape=jax.ShapeDtypeStruct((T_ALL, D_MODEL), F32),
        compiler_params=_cparams(("arbitrary",)),
        name="moe_combine",
    )(*cplan, x_all, rg_all, ys)


def _final_kernel(lpos_ref, run_ref, rdst_ref, x_ref, rg_ref, ys_ref, nfin_ref, op_ref, os_ref,
                  ystage, ybuf0, ybuf1, sem):
    i = pl.program_id(0)
    pieces = _gather_rows(lpos_ref, run_ref, rdst_ref, ys_ref, ystage, (ybuf0, ybuf1), sem, i)

    def compute(rows, y0, y1):
        return _rms(_combined(x_ref, rg_ref, rows, y0, y1), nfin_ref[...])

    def store(rows, y):
        @pl.when(i < N_PROMPT_BLOCKS)
        def _():
            op_ref[rows, :] = y

        @pl.when(i >= N_PROMPT_BLOCKS)
        def _():
            os_ref[rows, :] = y

    pieces(compute, store)


def _final(cplan, x_all, rg_all, ys, nfin):
    return pl.pallas_call(
        _final_kernel,
        grid_spec=pltpu.PrefetchScalarGridSpec(
            num_scalar_prefetch=3,
            grid=(N_ROW_BLOCKS,),
            in_specs=[pl.BlockSpec((TM, D_MODEL), lambda i, a, b, c: (i, 0)),
                      pl.BlockSpec((TM, LANES), lambda i, a, b, c: (i, 0)),
                      pl.BlockSpec(memory_space=pl.ANY),
                      pl.BlockSpec((1, D_MODEL), lambda i, a, b, c: (0, 0))],
            out_specs=[pl.BlockSpec((TM, D_MODEL), lambda i, a, b, c: (jnp.minimum(i, N_PROMPT_BLOCKS - 1), 0)),
                       pl.BlockSpec((TM, D_MODEL), lambda i, a, b, c: (0, 0))],
            scratch_shapes=_COMBINE_SCRATCH,
        ),
        out_shape=[jax.ShapeDtypeStruct((T_PROMPT, D_MODEL), F32), jax.ShapeDtypeStruct((T_SAMPLE, D_MODEL), F32)],
        compiler_params=_cparams(("arbitrary",)),
        name="moe_combine_final",
    )(*cplan, x_all, rg_all, ys, nfin)


def _moe(h_all, rt_all, cnt, tcnt, w_gate, w_up, w_down, layer):
    plan, dplan, cplan = _moe_metadata(rt_all, cnt, tcnt)
    xs = _dispatch(dplan, h_all)
    ys = _experts(*plan, xs, w_gate, w_up, w_down, layer)
    return cplan, ys


def _pool_project(d_groups, wp_ref, scale):
    outs = [_dot(d_groups[g].astype(BF16), wp_ref[g]) for g in range(len(POOL_SIZES))]
    return jnp.concatenate(outs, axis=1) * scale


def _mix1_prompt_kernel(x_ref, nm_ref, wp_ref, sc_ref, nf_ref, wr_ref, br_ref,
                        x3_ref, h_ref, ri_ref, rg_ref, tc_ref, pl_ref, cnt_ref, ext):
    i = pl.program_id(0)

    @pl.when(i == 0)
    def _():
        cnt_ref[...] = jnp.zeros_like(cnt_ref)

    x = x_ref[...]
    hp = _rms(x, nm_ref[...])

    @pl.when(i % STEPS_PER_BATCH == 0)
    def _():
        ext[0:POOL_MAX, :] = jnp.zeros((POOL_MAX, D_MODEL), F32)

    ext[POOL_MAX:, :] = hp
    pos = (i % STEPS_PER_BATCH) * TM + lax.broadcasted_iota(jnp.int32, (TM, 1), 0)
    d_groups = []
    for g, w in enumerate(POOL_SIZES):
        cols = slice(g * POOL_GROUP_DIM, (g + 1) * POOL_GROUP_DIM)
        acc = ext[:, cols]
        span = 1
        while span < w:
            acc = acc + pltpu.roll(acc, span, 0)
            span *= 2
        cnt = jnp.minimum(pos + 1, w).astype(F32)
        d_groups.append(acc[POOL_MAX:] / cnt - hp[:, cols])
    tail = hp[TM - POOL_MAX:, :]
    ext[0:POOL_MAX, :] = tail
    pl_ref[...] = tail

    x3 = x + _pool_project(d_groups, wp_ref, sc_ref[...])
    x3_ref[...] = x3
    h, ids, gates = _route(x3, nf_ref[...], wr_ref[...], br_ref[...])
    h_ref[...] = h.reshape(h_ref.shape)
    ri_ref[...] = _rank_pack(ids, cnt_ref, tc_ref)
    rg_ref[...] = gates


def _mix1_prompt(x_all, nm, wp, sc, nf, wr, br):
    row_spec = pl.BlockSpec((TM, D_MODEL), lambda i: (i, 0))
    row3_spec = pl.BlockSpec((TM, ROW_TILE, LANES), lambda i: (i, 0, 0))
    lane_spec = pl.BlockSpec((TM, LANES), lambda i: (i, 0))
    return pl.pallas_call(
        _prompt_steps(_mix1_prompt_kernel, 7),
        grid=(N_ROW_BLOCKS,),
        in_specs=[row_spec, _const_spec((1, D_MODEL)),
                  _const_spec((len(POOL_SIZES), POOL_GROUP_DIM, POOL_GROUP_DIM)), _const_spec((1, D_MODEL)),
                  _const_spec((1, D_MODEL)), _const_spec((D_MODEL, 2 * LANES)), _const_spec((1, LANES))],
        out_specs=[row_spec, row3_spec, pl.BlockSpec((8, TM), lambda i: (0, i)), lane_spec,
                   pl.BlockSpec((None, 1, LANES), lambda i: (i, 0, 0)),
                   pl.BlockSpec((None, POOL_MAX, D_MODEL),
                                lambda i: (jnp.minimum(i // STEPS_PER_BATCH, BATCH - 1), 0, 0)),
                   _const_spec((1, LANES))],
        out_shape=[jax.ShapeDtypeStruct((T_ALL, D_MODEL), F32), jax.ShapeDtypeStruct((T_ALL, ROW_TILE, LANES), BF16),
                   jax.ShapeDtypeStruct((8, T_ALL), jnp.int32), jax.ShapeDtypeStruct((T_ALL, LANES), F32),
                   jax.ShapeDtypeStruct((N_ROW_BLOCKS, 1, LANES), F32),
                   jax.ShapeDtypeStruct((BATCH, POOL_MAX, D_MODEL), F32), jax.ShapeDtypeStruct((1, LANES), F32)],
        scratch_shapes=[pltpu.VMEM((POOL_MAX + TM, D_MODEL), F32)],
        compiler_params=_cparams(("arbitrary",)),
        name="mix1_prompt",
    )(x_all, nm, wp, sc, nf, wr, br)


def _mix1_sample_kernel(x_ref, st_ref, nm_ref, wp_ref, sc_ref, nf_ref, wr_ref, br_ref, cnt_in,
                        x3_in, h_in, ri_in, rg_in, tc_in,
                        x3_ref, h_ref, ri_ref, rg_ref, tc_ref, hs_ref, cnt_ref):
    del x3_in, h_in, ri_in, rg_in, tc_in
    x = x_ref[...]
    hs = _rms(x, nm_ref[...])
    hs_ref[...] = hs
    n_ctx = POOL_MAX - 1
    d_groups = []
    for g, w in enumerate(POOL_SIZES):
        cols = slice(g * POOL_GROUP_DIM, (g + 1) * POOL_GROUP_DIM)
        parts = []
        for t in range(DEC_SEQ):
            acc = hs[t * DEC_BATCH:(t + 1) * DEC_BATCH, cols]
            for back in range(1, w):
                src = t - back
                if src >= 0:
                    acc = acc + hs[src * DEC_BATCH:(src + 1) * DEC_BATCH, cols]
                else:
                    acc = acc + st_ref[n_ctx + src, :, cols]
            parts.append(acc / float(w) - hs[t * DEC_BATCH:(t + 1) * DEC_BATCH, cols])
        d_groups.append(jnp.concatenate(parts, axis=0))
    x3 = x + _pool_project(d_groups, wp_ref, sc_ref[...])
    x3_ref[...] = x3
    h, ids, gates = _route(x3, nf_ref[...], wr_ref[...], br_ref[...])
    h_ref[...] = h.reshape(h_ref.shape)
    cnt_ref[...] = cnt_in[...]
    ri_ref[...] = _rank_pack(ids, cnt_ref, tc_ref)
    rg_ref[...] = gates


def _mix1_sample(x_all, state_t, nm, wp, sc, nf, wr, br, cnt, x3_all, h_all, ri_all, rg_all, tc_all):
    sample_rows = pl.BlockSpec((TM, D_MODEL), lambda g: (N_PROMPT_BLOCKS, 0))
    sample_rows3 = pl.BlockSpec((TM, ROW_TILE, LANES), lambda g: (N_PROMPT_BLOCKS, 0, 0))
    sample_lanes = pl.BlockSpec((TM, LANES), lambda g: (N_PROMPT_BLOCKS, 0))
    anyspec = pl.BlockSpec(memory_space=pl.ANY)
    n_in = 9
    return pl.pallas_call(
        _mix1_sample_kernel,
        grid=(1,),
        in_specs=[sample_rows, _const_spec((POOL_MAX - 1, DEC_BATCH, D_MODEL)), _const_spec((1, D_MODEL)),
                  _const_spec((len(POOL_SIZES), POOL_GROUP_DIM, POOL_GROUP_DIM)), _const_spec((1, D_MODEL)),
                  _const_spec((1, D_MODEL)), _const_spec((D_MODEL, 2 * LANES)), _const_spec((1, LANES)),
                  _const_spec((1, LANES)), anyspec, anyspec, anyspec, anyspec, anyspec],
        out_specs=[sample_rows, sample_rows3, pl.BlockSpec((8, TM), lambda g: (0, N_PROMPT_BLOCKS)), sample_lanes,
                   pl.BlockSpec((None, 1, LANES), lambda g: (N_PROMPT_BLOCKS, 0, 0)),
                   _const_spec((T_SAMPLE, D_MODEL)), _const_spec((1, LANES))],
        out_shape=[jax.ShapeDtypeStruct((T_ALL, D_MODEL), F32), jax.ShapeDtypeStruct((T_ALL, ROW_TILE, LANES), BF16),
                   jax.ShapeDtypeStruct((8, T_ALL), jnp.int32), jax.ShapeDtypeStruct((T_ALL, LANES), F32),
                   jax.ShapeDtypeStruct((N_ROW_BLOCKS, 1, LANES), F32),
                   jax.ShapeDtypeStruct((T_SAMPLE, D_MODEL), F32), jax.ShapeDtypeStruct((1, LANES), F32)],
        input_output_aliases={n_in: 0, n_in + 1: 1, n_in + 2: 2, n_in + 3: 3, n_in + 4: 4},
        compiler_params=_cparams(("arbitrary",)),
        name="mix1_sample",
    )(x_all, state_t, nm, wp, sc, nf, wr, br, cnt, x3_all, h_all, ri_all, rg_all, tc_all)


def _router_weights(wg, bg, we, be):
    w = jnp.concatenate([wg, jnp.transpose(we, (1, 0, 2)).reshape(D_MODEL, N_EXPERTS)], axis=1)
    b = jnp.concatenate([bg, be.reshape(N_EXPERTS)])
    pad = LANES - N_GROUPS - N_EXPERTS
    w = jnp.pad(w, ((0, 0), (0, pad)))
    w_hi = w.astype(BF16)
    w_lo = (w - w_hi.astype(F32)).astype(BF16)
    return jnp.concatenate([w_hi, w_lo], axis=1), jnp.pad(b, (0, pad)).reshape(1, LANES)


def kernel(x_prompt, x_sample, cache_k_win, cache_v_win, state_pool, norm_mix, norm_ffn, norm_final, w_in,
           a_ln_g, a_ln_b, a_w_s, a_b_s, b_sinks, rel_bias_table, w_out, c_w_pool, c_scale,
           router_group_w, router_group_b, router_expert_w, router_expert_b, w_gate, w_up, w_down):
    xs_t = jnp.transpose(x_sample, (1, 0, 2)).reshape(T_SAMPLE, D_MODEL)
    xp2 = x_prompt.reshape(T_PROMPT, D_MODEL)
    win =w_in[0].astype(BF16)
    wout = w_out[0].astype(BF16)
    lng = a_ln_g[0].reshape(1, A_WIDTH)
    lnb = a_ln_b[0].reshape(1, A_WIDTH)
    bias_p, bias_sc, bias_sn, wsp = _prep(rel_bias_table, b_sinks[0], a_w_s[0])
    bs_full = jnp.repeat(a_b_s[0].T, A_HEAD_DIM, axis=1)
    w4 = jnp.transpose(a_w_s[0][:, :DEC_SEQ, :DEC_SEQ], (1, 2, 0)).reshape(DEC_SEQ * DEC_SEQ, A_HEADS)
    wcoef = jnp.repeat(w4, A_HEAD_DIM, axis=1)
    bcoef = jnp.pad(jnp.repeat(a_b_s[0][:, :DEC_SEQ].T, A_HEAD_DIM, axis=1), ((0, 8 - DEC_SEQ), (0, 0)))
    ck = jnp.transpose(cache_k_win[0], (0, 2, 3, 1))
    cv = jnp.transpose(cache_v_win[0], (0, 2, 3, 1))
    routers = [_router_weights(router_group_w[l], router_group_b[l], router_expert_w[l], router_expert_b[l])
               for l in range(2)]
    nm = [norm_mix[l].reshape(1, D_MODEL) for l in range(2)]
    nf = [norm_ffn[l].reshape(1, D_MODEL) for l in range(2)]

    x1_all, h_all, ri_all, rg_all, tc_all, k_last, v_last, va_last, cnt0 = _mix0_prompt(
        xp2, nm[0], win, lng, lnb, wsp, bs_full, bias_p, wout, nf[0], *routers[0])
    x1_all, h_all, ri_all, rg_all, tc_all, k_new, v_new, va_s, cnt0 = _mix0_sample(
        xs_t, nm[0], win, lng, lnb, wcoef, bcoef, ck, cv, bias_sc, bias_sn, wout, nf[0], *routers[0], cnt0,
        x1_all, h_all, ri_all, rg_all, tc_all)
    cplan0, ys0 = _moe(h_all, ri_all, cnt0, tc_all, w_gate, w_up, w_down, 0)
    x2_all = _combine(cplan0, x1_all, rg_all, ys0)

    wp = c_w_pool[0].astype(BF16)
    sc = c_scale[0].reshape(1, D_MODEL)
    x3_all, h2_all, ri2_all, rg2_all, tc2_all, pool_tail, cnt1 = _mix1_prompt(
        x2_all, nm[1], wp, sc, nf[1], *routers[1])
    state_t = jnp.transpose(state_pool[0], (1, 0, 2))
    x3_all, h2_all, ri2_all, rg2_all, tc2_all, hs1, cnt1 = _mix1_sample(
        x2_all, state_t, nm[1], wp, sc, nf[1], *routers[1], cnt1, x3_all, h2_all, ri2_all, rg2_all, tc2_all)
    cplan1, ys1 = _moe(h2_all, ri2_all, cnt1, tc2_all, w_gate, w_up, w_down, 1)
    y_p, y_s = _final(cplan1, x3_all, rg2_all, ys1, norm_final.reshape(1, D_MODEL))

    def from_tmajor(a, width):
        return jnp.transpose(a.reshape(DEC_SEQ, DEC_BATCH, width), (1, 0, 2))

    y_prompt = y_p.reshape(BATCH, SEQ, D_MODEL)
    y_sample = from_tmajor(y_s, D_MODEL)
    win_k_p = k_last.reshape(1, BATCH, WINDOW, B_KV_HEADS, B_HEAD_DIM)
    win_v_p = v_last.reshape(1, BATCH, WINDOW, B_KV_HEADS, B_HEAD_DIM)
    win_k_s = jnp.transpose(k_new, (0, 3, 1, 2))[None]
    win_v_s = jnp.transpose(v_new, (0, 3, 1, 2))[None]
    chunk_v_p = va_last.reshape(1, BATCH, CHUNK, A_HEADS, A_HEAD_DIM)
    chunk_v_s = from_tmajor(va_s, A_WIDTH).reshape(1, DEC_BATCH, DEC_SEQ, A_HEADS, A_HEAD_DIM)
    pool_p = pool_tail[:, 1:][None]
    pool_s = jnp.concatenate([state_pool[0][:, DEC_SEQ:], from_tmajor(hs1, D_MODEL)], axis=1)[None]
    return (y_prompt, y_sample, win_k_p, win_v_p, win_k_s, win_v_s, chunk_v_p, chunk_v_s, pool_p, pool_s)
```

```python
import functools
import math

import numpy as np
import jax
import jax.numpy as jnp
from jax import lax
from jax.experimental import pallas as pl
from jax.experimental.pallas import tpu as pltpu

F32 = jnp.float32
BF16 = jnp.bfloat16

D_MODEL = 1024
BATCH = 2
SEQ = 8192
DEC_BATCH = 128
DEC_SEQ = 4
A_WIDTH = 512
A_HEADS = 8
A_HEAD_DIM = 64
CHUNK = 128
B_HEADS = 8
B_KV_HEADS = 2
B_HEAD_DIM = 64
B_GROUP = 4
WINDOW = 128
N_BUCKETS = 32
MAX_DISTANCE = WINDOW
Q_WIDTH = 512
KV_WIDTH = 128
IN_WIDTH = 2 * A_WIDTH + Q_WIDTH + 2 * KV_WIDTH
ATTN_SCALE = B_HEAD_DIM ** -0.5
NEG_INF = -1e30
POOL_SIZES = (2, 4, 8, 16)
POOL_GROUP_DIM = 256
POOL_MAX = 16
N_GROUPS = 4
EXPERTS_PER_GROUP = 8
N_EXPERTS = 32
TOP_K = 2
D_EXPERT = 512
EPS = 1e-6

LANES = 128
ROW_TILE = D_MODEL // LANES
T_PROMPT = BATCH * SEQ
T_SAMPLE = DEC_BATCH * DEC_SEQ
T_ALL = T_PROMPT + T_SAMPLE
TM = 512
N_PROMPT_BLOCKS = T_PROMPT // TM
N_ROW_BLOCKS = T_ALL // TM
STEPS_PER_BATCH = SEQ // TM
SUB = TM // WINDOW
N_SLOTS = T_ALL * TOP_K
MOE_BLK = 512
N_MOE_BLOCKS = N_SLOTS // MOE_BLK + N_EXPERTS
N_SORT_ROWS = N_MOE_BLOCKS * MOE_BLK
SAMPLE_GROUP = 8
N_SAMPLE_GROUPS = DEC_BATCH // SAMPLE_GROUP
VMEM_LIMIT = 56 * 1024 * 1024

STACK_HEADS = ((0, 2, 5, 7), (1, 3, 4, 6))


def _t5_bucket_np(dist):
    n = np.maximum(dist, 0)
    max_exact = N_BUCKETS // 2
    nf = np.maximum(n, 1).astype(np.float32)
    large = max_exact + (np.log(nf / np.float32(max_exact)) / np.float32(math.log(MAX_DISTANCE / max_exact))
                         * np.float32(N_BUCKETS - max_exact)).astype(np.int32)
    large = np.minimum(large, N_BUCKETS - 1)
    return np.where(n < max_exact, n, large).astype(np.int32)


def _bucket_tables():
    qi = np.arange(WINDOW)[:, None]
    ki = np.arange(2 * WINDOW)[None, :]
    dist = qi + WINDOW - ki
    valid = (dist >= 0) & (dist < WINDOW)
    bp = np.where(valid, _t5_bucket_np(dist), -1)
    bp_first = np.where(ki >= WINDOW, bp, -1)
    bkt_p = np.stack([bp_first, bp]).astype(np.int32)

    t = np.repeat(np.arange(DEC_SEQ), SAMPLE_GROUP)[:, None]
    b = np.tile(np.arange(SAMPLE_GROUP), DEC_SEQ)[:, None]
    cb = np.repeat(np.arange(SAMPLE_GROUP), WINDOW)[None, :]
    cj = np.tile(np.arange(WINDOW), SAMPLE_GROUP)[None, :]
    dist_c = t + WINDOW - cj
    valid_c = (cb == b) & (dist_c >= 0) & (dist_c < WINDOW)
    bkt_sc = np.where(valid_c, _t5_bucket_np(dist_c), -1).astype(np.int32)
    nt = np.repeat(np.arange(DEC_SEQ), SAMPLE_GROUP)[None, :]
    nb = np.tile(np.arange(SAMPLE_GROUP), DEC_SEQ)[None, :]
    dist_n = t - nt
    valid_n = (nb == b) & (dist_n >= 0)
    bkt_sn = np.where(valid_n, _t5_bucket_np(dist_n), -1).astype(np.int32)
    bkt_sn = np.concatenate([bkt_sn, np.full((32, LANES - 32), -1, np.int32)], axis=1)
    return bkt_p, bkt_sc, bkt_sn


_BKT_P, _BKT_SC, _BKT_SN = _bucket_tables()


def _cparams(semantics):
    return pltpu.CompilerParams(dimension_semantics=semantics, vmem_limit_bytes=VMEM_LIMIT)


def _rms(x, g):
    return x * lax.rsqrt(jnp.mean(x * x, axis=-1, keepdims=True) + EPS) * g


def _layernorm(x, g, b):
    xc = x - jnp.mean(x, axis=-1, keepdims=True)
    return xc * lax.rsqrt(jnp.mean(xc * xc, axis=-1, keepdims=True) + EPS) * g + b


def _dot(a, b):
    return jnp.dot(a, b, preferred_element_type=F32)


def _dot_nt(a, b):
    return lax.dot_general(a, b, (((1,), (1,)), ((), ())), preferred_element_type=F32)


def _project(x, nm, win, lng, lnb):
    h = _rms(x, nm)
    z = _dot(h.astype(BF16), win)
    u = jax.nn.gelu(z[:, :A_WIDTH])
    va = _layernorm(jax.nn.gelu(z[:, A_WIDTH:2 * A_WIDTH]), lng, lnb)
    q = z[:, 2 * A_WIDTH:2 * A_WIDTH + Q_WIDTH] * ATTN_SCALE
    k = z[:, 2 * A_WIDTH + Q_WIDTH:2 * A_WIDTH + Q_WIDTH + KV_WIDTH]
    v = z[:, 2 * A_WIDTH + Q_WIDTH + KV_WIDTH:]
    return u, va, q, k, v


def _route(x1, nf, wr, br):
    hf = _rms(x1, nf)
    h = hf.astype(BF16)
    h_lo = (hf - h.astype(F32)).astype(BF16)
    part = _dot(h, wr)
    logits = part[:, :LANES] + part[:, LANES:] + _dot(h_lo, wr[:, :LANES]) + br
    rows = logits.shape[0]
    lane = lax.broadcasted_iota(jnp.int32, (rows, LANES), 1)
    lanef = lane.astype(F32)
    big = jnp.float32(1e9)
    is_g = lane < N_GROUPS
    gl = jnp.where(is_g, logits, -jnp.inf)
    gmax = jnp.max(gl, axis=1, keepdims=True)
    gsel = jnp.min(jnp.where(gl == gmax, lanef, big), axis=1, keepdims=True)
    gsum = jnp.sum(jnp.where(is_g, jnp.exp(logits - gmax), 0.0), axis=1, keepdims=True)
    g1 = 1.0 / gsum
    lo = N_GROUPS + EXPERTS_PER_GROUP * gsel
    emask = (lanef >= lo) & (lanef < lo + EXPERTS_PER_GROUP)
    el = jnp.where(emask, logits, -jnp.inf)
    v1 = jnp.max(el, axis=1, keepdims=True)
    i1 = jnp.min(jnp.where(el == v1, lanef, big), axis=1, keepdims=True)
    el2 = jnp.where(lanef == i1, -jnp.inf, el)
    v2 = jnp.max(el2, axis=1, keepdims=True)
    i2 = jnp.min(jnp.where(el2 == v2, lanef, big), axis=1, keepdims=True)
    e2 = jnp.exp(v2 - v1)
    den = 1.0 + e2
    w1 = g1 / den
    w2 = g1 * e2 / den
    ids = jnp.where(lane == 0, i1 - N_GROUPS, jnp.where(lane == 1, i2 - N_GROUPS, 0.0)).astype(jnp.int32)
    gates = jnp.where(lane == 0, w1, jnp.where(lane == 1, w2, 0.0))
    return h, ids, gates


def _rank_pack(ids, cnt_ref, tcnt_ref):
    rows = ids.shape[0]
    lane = lax.broadcasted_iota(jnp.int32, (rows, LANES), 1)
    o0 = (lane == ids[:, 0:1]).astype(F32)
    o1 = (lane == ids[:, 1:2]).astype(F32)
    r = lax.broadcasted_iota(jnp.int32, (rows, rows), 0)
    c = lax.broadcasted_iota(jnp.int32, (rows, rows), 1)
    before = (c < r).astype(BF16)
    p01 = _dot(before, jnp.concatenate([o0, o1], axis=1).astype(BF16))
    p0 = p01[:, :LANES]
    p1 = p01[:, LANES:]
    c0 = jnp.sum(o0, axis=0, keepdims=True)
    c1 = jnp.sum(o1, axis=0, keepdims=True)
    ctile = c0 + c1
    cnt_ref[...] = cnt_ref[...] + ctile
    tcnt_ref[...] = ctile
    inc = jnp.broadcast_to(ctile, (8, LANES))
    lane8 = lax.broadcasted_iota(jnp.int32, (8, LANES), 1)
    for sh in (1, 2, 4, 8, 16, 32, 64):
        inc = inc + jnp.where(lane8 >= sh, pltpu.roll(inc, sh, 1), 0.0)
    start = inc[0:1] - ctile
    lpos0 = jnp.sum(o0 * (start + p0), axis=1, keepdims=True)
    lpos1 = jnp.sum(o1 * (start + c0 + p1), axis=1, keepdims=True)
    idf = ids.astype(F32)
    packed = jnp.where(lane < TOP_K, idf, 0.0)
    for ln, col in ((4, lpos0), (5, lpos1)):
        packed = jnp.where(lane == ln, col, packed)
    return jnp.transpose(packed)[:8].astype(jnp.int32)


def _prep_kernel(tab_ref, sink_ref, bp_ref, bsc_ref, bsn_ref, ws_ref, op_ref, osc_ref, osn_ref, ows_ref):
    def fill(bkt, write, sink_col0):
        col0 = lax.broadcasted_iota(jnp.int32, bkt.shape, 1) == 0
        for st, heads in enumerate(STACK_HEADS):
            for slot, h in enumerate(heads):
                acc = jnp.full(bkt.shape, NEG_INF, F32)
                for b in range(N_BUCKETS):
                    acc = jnp.where(bkt == b, tab_ref[b, h], acc)
                if sink_col0:
                    acc = jnp.where(col0, sink_ref[0, h], acc)
                write(st, slot, acc)

    for var in range(2):
        def wr_p(st, slot, acc, var=var):
            op_ref[var, st, slot * WINDOW:(slot + 1) * WINDOW, :] = acc
        fill(bp_ref[var], wr_p, True)

    rows_s = DEC_SEQ * SAMPLE_GROUP

    def wr_sc(st, slot, acc):
        osc_ref[st, slot * rows_s:(slot + 1) * rows_s, :] = acc
    fill(bsc_ref[...], wr_sc, True)

    def wr_sn(st, slot, acc):
        osn_ref[st, slot * rows_s:(slot + 1) * rows_s, :] = acc
    fill(bsn_ref[...], wr_sn, False)

    r = lax.broadcasted_iota(jnp.int32, (CHUNK, CHUNK), 0)
    c = lax.broadcasted_iota(jnp.int32, (CHUNK, CHUNK), 1)
    for h in range(A_HEADS):
        ows_ref[h // 2, :, (h % 2) * CHUNK:(h % 2 + 1) * CHUNK] = jnp.where(r >= c, ws_ref[h], 0.0).astype(BF16)


def _prep(rel_bias_table, sinks, w_s):
    vm = pl.BlockSpec(memory_space=pltpu.VMEM)
    sm = pl.BlockSpec(memory_space=pltpu.SMEM)
    rows_s = DEC_SEQ * SAMPLE_GROUP
    return pl.pallas_call(
        _prep_kernel,
        in_specs=[sm, sm, vm, vm, vm, vm],
        out_specs=[vm, vm, vm, vm],
        out_shape=[
            jax.ShapeDtypeStruct((2, 2, 4 * WINDOW, 2 * WINDOW), F32),
            jax.ShapeDtypeStruct((2, 4 * rows_s, SAMPLE_GROUP * WINDOW), F32),
            jax.ShapeDtypeStruct((2, 4 * rows_s, LANES), F32),
            jax.ShapeDtypeStruct((A_HEADS // 2, CHUNK, 2 * CHUNK), BF16),
        ],
        name="prep_tables",
    )(rel_bias_table, sinks.reshape(1, B_HEADS), jnp.asarray(_BKT_P), jnp.asarray(_BKT_SC), jnp.asarray(_BKT_SN), w_s)


def _gate_pairs(va_rows, wsp_ref, lane_lo):
    outs = []
    for p in range(A_HEADS // 2):
        vp = va_rows[:, p * LANES:(p + 1) * LANES]
        rhs = jnp.concatenate([jnp.where(lane_lo, vp, 0.0), jnp.where(lane_lo, 0.0, vp)], axis=0).astype(BF16)
        outs.append(_dot(wsp_ref[p], rhs))
    return jnp.concatenate(outs, axis=1)


def _prompt_steps(body, first_row_out):
    def kern(*refs):
        i = pl.program_id(0)

        @pl.when(i < N_PROMPT_BLOCKS)
        def _():
            body(*refs)

        @pl.when(i >= N_PROMPT_BLOCKS)
        def _():
            for r in refs[first_row_out:first_row_out + 5]:
                r[...] = jnp.zeros(r.shape, r.dtype)

    return kern


def _mix0_prompt_kernel(x_ref, nm_ref, win_ref, lng_ref, lnb_ref, wsp_ref, bs_ref, bias_ref,
                        wout_ref, nf_ref, wr_ref, br_ref,
                        x1_ref, h_ref, ri_ref, rg_ref, tc_ref, kl_ref, vl_ref, val_ref, cnt_ref,
                        kprev, vprev, mix_scr):
    @pl.when(pl.program_id(0) == 0)
    def _():
        cnt_ref[...] = jnp.zeros_like(cnt_ref)

    x = x_ref[...]
    u, va, q, k, v = _project(x, nm_ref[...], win_ref[...], lng_ref[...], lnb_ref[...])
    lane_lo = lax.broadcasted_iota(jnp.int32, (WINDOW, LANES), 1) < B_HEAD_DIM
    row0 = lax.broadcasted_iota(jnp.int32, (WINDOW, KV_WIDTH), 0) == 0
    first = pl.program_id(0) % STEPS_PER_BATCH == 0

    @pl.when(first)
    def _():
        kprev[...] = jnp.zeros_like(kprev)
        vprev[...] = jnp.zeros_like(vprev)

    for j in range(SUB):
        rows = slice(j * WINDOW, (j + 1) * WINDOW)
        s_gate = _gate_pairs(va[rows], wsp_ref, lane_lo)
        mix_scr[rows, :A_WIDTH] = u[rows] * (s_gate + bs_ref[...])

        if j == 0:
            kp, vp = kprev[...], vprev[...]
        else:
            prows = slice((j - 1) * WINDOW, j * WINDOW)
            kp, vp = k[prows], v[prows]
        kk = jnp.concatenate([jnp.where(row0, 0.0, kp), k[rows]], axis=0)
        vv = jnp.concatenate([jnp.where(row0, 0.0, vp), v[rows]], axis=0)
        kops = (kk.astype(BF16), pltpu.roll(kk, B_HEAD_DIM, 1).astype(BF16))
        vops = (vv.astype(BF16), pltpu.roll(vv, B_HEAD_DIM, 1).astype(BF16))
        qt = [q[rows, p * LANES:(p + 1) * LANES] for p in range(4)]
        q_even = [jnp.where(lane_lo, t, 0.0) for t in qt]
        q_odd = [jnp.where(lane_lo, 0.0, t) for t in qt]
        stacks = (jnp.concatenate([q_even[0], q_even[1], q_odd[2], q_odd[3]], axis=0),
                  jnp.concatenate([q_odd[0], q_odd[1], q_even[2], q_even[3]], axis=0))
        o = []
        for st in range(2):
            s = _dot_nt(stacks[st].astype(BF16), kops[st])
            if j == 0:
                bias = bias_ref[jnp.where(first, 0, 1), st]
            else:
                bias = bias_ref[1, st]
            s = s + bias
            m = jnp.max(s, axis=-1, keepdims=True)
            p = jnp.exp(s - m)
            den = jnp.sum(p, axis=-1, keepdims=True)
            o.append(_dot(p.astype(BF16), vops[st]) / den)
        oa, ob = o
        sl = [slice(i * WINDOW, (i + 1) * WINDOW) for i in range(4)]
        tiles = (jnp.where(lane_lo, oa[sl[0]], ob[sl[0]]), jnp.where(lane_lo, oa[sl[1]], ob[sl[1]]),
                 jnp.where(lane_lo, ob[sl[2]], oa[sl[2]]), jnp.where(lane_lo, ob[sl[3]], oa[sl[3]]))
        for p in range(4):
            mix_scr[rows, A_WIDTH + p * LANES:A_WIDTH + (p + 1) * LANES] = tiles[p]

    last = slice(TM - WINDOW, TM)
    kprev[...] = k[last]
    vprev[...] = v[last]
    kl_ref[...] = k[last]
    vl_ref[...] = v[last]
    val_ref[...] = va[last]

    x1 = x + _dot(mix_scr[...].astype(BF16), wout_ref[...])
    x1_ref[...] = x1
    h, ids, gates = _route(x1, nf_ref[...], wr_ref[...], br_ref[...])
    h_ref[...] = h.reshape(h_ref.shape)
    ri_ref[...] = _rank_pack(ids, cnt_ref, tc_ref)
    rg_ref[...] = gates


def _const_spec(shape):
    nd = len(shape)
    return pl.BlockSpec(shape, lambda i, _n=nd: (0,) * _n)


def _mix0_prompt(x_all, nm, win, lng, lnb, wsp, bs_full, bias_p, wout, nf, wr, br):
    row_spec = pl.BlockSpec((TM, D_MODEL), lambda i: (i, 0))
    row3_spec = pl.BlockSpec((TM, ROW_TILE, LANES), lambda i: (i, 0, 0))
    lane_spec = pl.BlockSpec((TM, LANES), lambda i: (i, 0))
    last_kv = pl.BlockSpec((None, WINDOW, KV_WIDTH), lambda i: (jnp.minimum(i // STEPS_PER_BATCH, BATCH - 1), 0, 0))
    last_va = pl.BlockSpec((None, WINDOW, A_WIDTH), lambda i: (jnp.minimum(i // STEPS_PER_BATCH, BATCH - 1), 0, 0))
    return pl.pallas_call(
        _prompt_steps(_mix0_prompt_kernel, 12),
        grid=(N_ROW_BLOCKS,),
        in_specs=[pl.BlockSpec((TM, D_MODEL), lambda i: (jnp.minimum(i, N_PROMPT_BLOCKS - 1), 0)),
                  _const_spec((1, D_MODEL)), _const_spec((D_MODEL, IN_WIDTH)),
                  _const_spec((1, A_WIDTH)), _const_spec((1, A_WIDTH)),
                  _const_spec((A_HEADS // 2, CHUNK, 2 * CHUNK)), _const_spec((CHUNK, A_WIDTH)),
                  _const_spec((2, 2, 4 * WINDOW, 2 * WINDOW)),
                  _const_spec((A_WIDTH + Q_WIDTH, D_MODEL)), _const_spec((1, D_MODEL)),
                  _const_spec((D_MODEL, 2 * LANES)), _const_spec((1, LANES))],
        out_specs=[row_spec, row3_spec, pl.BlockSpec((8, TM), lambda i: (0, i)), lane_spec,
                   pl.BlockSpec((None, 1, LANES), lambda i: (i, 0, 0)),
                   last_kv, last_kv, last_va, _const_spec((1, LANES))],
        out_shape=[jax.ShapeDtypeStruct((T_ALL, D_MODEL), F32), jax.ShapeDtypeStruct((T_ALL, ROW_TILE, LANES), BF16),
                   jax.ShapeDtypeStruct((8, T_ALL), jnp.int32), jax.ShapeDtypeStruct((T_ALL, LANES), F32),
                   jax.ShapeDtypeStruct((N_ROW_BLOCKS, 1, LANES), F32),
                   jax.ShapeDtypeStruct((BATCH, WINDOW, KV_WIDTH), F32),
                   jax.ShapeDtypeStruct((BATCH, WINDOW, KV_WIDTH), F32),
                   jax.ShapeDtypeStruct((BATCH, WINDOW, A_WIDTH), F32),
                   jax.ShapeDtypeStruct((1, LANES), F32)],
        scratch_shapes=[pltpu.VMEM((WINDOW, KV_WIDTH), F32), pltpu.VMEM((WINDOW, KV_WIDTH), F32),
                        pltpu.VMEM((TM, D_MODEL), F32)],
        compiler_params=_cparams(("arbitrary",)),
        name="mix0_prompt",
    )(x_all, nm, win, lng, lnb, wsp, bs_full, bias_p, wout, nf, wr, br)


def _mix0_sample_kernel(x_ref, nm_ref, win_ref, lng_ref, lnb_ref, wcoef_ref, bcoef_ref,
                        ck_ref, cv_ref, bsc_ref, bsn_ref,
                        wout_ref, nf_ref, wr_ref, br_ref, cnt_in,
                        x1_in, h_in, ri_in, rg_in, tc_in,
                        x1_ref, h_ref, ri_ref, rg_ref, tc_ref, kn_ref, vn_ref, va_ref, cnt_ref,
                        q_scr, k_scr, v_scr, mix_scr):
    del x1_in, h_in, ri_in, rg_in, tc_in
    g = pl.program_id(0)

    @pl.when(g == 0)
    def _():
        u, va, q, k, v = _project(x_ref[...], nm_ref[...], win_ref[...], lng_ref[...], lnb_ref[...])
        q_scr[...] = q
        k_scr[...] = k
        v_scr[...] = v
        va_ref[...] = va
        for t in range(DEC_SEQ):
            acc = jnp.zeros((DEC_BATCH, A_WIDTH), F32) + bcoef_ref[t:t + 1, :]
            for s in range(t + 1):
                row = t * DEC_SEQ + s
                acc = acc + wcoef_ref[row:row + 1, :] * va[s * DEC_BATCH:(s + 1) * DEC_BATCH]
            mix_scr[t * DEC_BATCH:(t + 1) * DEC_BATCH, :A_WIDTH] = u[t * DEC_BATCH:(t + 1) * DEC_BATCH] * acc

    b0 = pl.multiple_of(g * SAMPLE_GROUP, SAMPLE_GROUP)
    lane_lo = lax.broadcasted_iota(jnp.int32, (DEC_SEQ * SAMPLE_GROUP, LANES), 1) < B_HEAD_DIM

    def grab(ref, width):
        return jnp.concatenate([ref[pl.ds(t * DEC_BATCH + b0, SAMPLE_GROUP), :] for t in range(DEC_SEQ)], axis=0)

    qg = grab(q_scr, Q_WIDTH)
    kn = grab(k_scr, KV_WIDTH)
    vn = grab(v_scr, KV_WIDTH)

    lane_w = lax.broadcasted_iota(jnp.int32, (KV_WIDTH, WINDOW), 1)
    n_new = DEC_SEQ * SAMPLE_GROUP

    def new_window(c_ref, new_rows, w_ref):
        nt = jnp.transpose(jnp.concatenate([new_rows, jnp.zeros((WINDOW - n_new, KV_WIDTH), F32)], axis=0))
        for b in range(SAMPLE_GROUP):
            w = pltpu.roll(c_ref[b].reshape(KV_WIDTH, WINDOW), WINDOW - DEC_SEQ, 1)
            for t in range(DEC_SEQ):
                src = t * SAMPLE_GROUP + b
                dst = WINDOW - DEC_SEQ + t
                w = jnp.where(lane_w == dst, pltpu.roll(nt, (dst - src) % WINDOW, 1), w)
            w_ref[b] = w.reshape(B_KV_HEADS, B_HEAD_DIM, WINDOW)

    new_window(ck_ref, kn, kn_ref)
    new_window(cv_ref, vn, vn_ref)
    ccol0 = lax.broadcasted_iota(jnp.int32, (KV_WIDTH, SAMPLE_GROUP * WINDOW), 1) == 0

    def cache_t(ref):
        t = jnp.concatenate([ref[b].reshape(KV_WIDTH, WINDOW) for b in range(SAMPLE_GROUP)], axis=1)
        return jnp.where(ccol0, 0.0, t)

    def head_swap(t):
        return jnp.concatenate([t[B_HEAD_DIM:], t[:B_HEAD_DIM]], axis=0)

    kct = cache_t(ck_ref)
    vct = cache_t(cv_ref)
    kc_ops = (kct.astype(BF16), head_swap(kct).astype(BF16))
    vc_ops = (vct.astype(BF16), head_swap(vct).astype(BF16))
    kn_ops = (kn.astype(BF16), pltpu.roll(kn, B_HEAD_DIM, 1).astype(BF16))
    vn_ops = (vn.astype(BF16), pltpu.roll(vn, B_HEAD_DIM, 1).astype(BF16))
    qt = [qg[:, p * LANES:(p + 1) * LANES] for p in range(4)]
    q_even = [jnp.where(lane_lo, t, 0.0) for t in qt]
    q_odd = [jnp.where(lane_lo, 0.0, t) for t in qt]
    stacks = (jnp.concatenate([q_even[0], q_even[1], q_odd[2], q_odd[3]], axis=0),
              jnp.concatenate([q_odd[0], q_odd[1], q_even[2], q_even[3]], axis=0))
    o = []
    for st in range(2):
        qs = stacks[st].astype(BF16)
        sc = _dot(qs, kc_ops[st]) + bsc_ref[st]
        sn = _dot_nt(qs, kn_ops[st]) + bsn_ref[st][:, :DEC_SEQ * SAMPLE_GROUP]
        m = jnp.maximum(jnp.max(sc, axis=-1, keepdims=True), jnp.max(sn, axis=-1, keepdims=True))
        pc = jnp.exp(sc - m)
        pn = jnp.exp(sn - m)
        den = jnp.sum(pc, axis=-1, keepdims=True) + jnp.sum(pn, axis=-1, keepdims=True)
        o.append((_dot_nt(pc.astype(BF16), vc_ops[st]) + _dot(pn.astype(BF16), vn_ops[st])) / den)
    oa, ob = o
    n = DEC_SEQ * SAMPLE_GROUP
    sl = [slice(i * n, (i + 1) * n) for i in range(4)]
    tiles = (jnp.where(lane_lo, oa[sl[0]], ob[sl[0]]), jnp.where(lane_lo, oa[sl[1]], ob[sl[1]]),
             jnp.where(lane_lo, ob[sl[2]], oa[sl[2]]), jnp.where(lane_lo, ob[sl[3]], oa[sl[3]]))
    for p in range(4):
        for t in range(DEC_SEQ):
            mix_scr[pl.ds(t * DEC_BATCH + b0, SAMPLE_GROUP), A_WIDTH + p * LANES:A_WIDTH + (p + 1) * LANES] = (
                tiles[p][t * SAMPLE_GROUP:(t + 1) * SAMPLE_GROUP])

    @pl.when(g == N_SAMPLE_GROUPS - 1)
    def _():
        x1 = x_ref[...] + _dot(mix_scr[...].astype(BF16), wout_ref[...])
        x1_ref[...] = x1
        h, ids, gates = _route(x1, nf_ref[...], wr_ref[...], br_ref[...])
        h_ref[...] = h.reshape(h_ref.shape)
        cnt_ref[...] = cnt_in[...]
        ri_ref[...] = _rank_pack(ids, cnt_ref, tc_ref)
        rg_ref[...] = gates


def _mix0_sample(x_all, nm, win, lng, lnb, wcoef, bcoef, ck, cv, bias_sc, bias_sn, wout, nf, wr, br, cnt,
                 x1_all, h_all, ri_all, rg_all, tc_all):
    sample_rows = pl.BlockSpec((TM, D_MODEL), lambda g: (N_PROMPT_BLOCKS, 0))
    sample_rows3 = pl.BlockSpec((TM, ROW_TILE, LANES), lambda g: (N_PROMPT_BLOCKS, 0, 0))
    sample_lanes = pl.BlockSpec((TM, LANES), lambda g: (N_PROMPT_BLOCKS, 0))
    cache_spec = pl.BlockSpec((SAMPLE_GROUP, B_KV_HEADS, B_HEAD_DIM, WINDOW), lambda g: (g, 0, 0, 0))
    anyspec = pl.BlockSpec(memory_space=pl.ANY)
    n_in = 16
    return pl.pallas_call(
        _mix0_sample_kernel,
        grid=(N_SAMPLE_GROUPS,),
        in_specs=[_const_spec((TM, D_MODEL)), _const_spec((1, D_MODEL)), _const_spec((D_MODEL, IN_WIDTH)),
                  _const_spec((1, A_WIDTH)), _const_spec((1, A_WIDTH)),
                  _const_spec((16, A_WIDTH)), _const_spec((8, A_WIDTH)),
                  cache_spec, cache_spec,
                  _const_spec((2, 4 * 32, SAMPLE_GROUP * WINDOW)), _const_spec((2, 4 * 32, LANES)),
                  _const_spec((A_WIDTH + Q_WIDTH, D_MODEL)), _const_spec((1, D_MODEL)),
                  _const_spec((D_MODEL, 2 * LANES)), _const_spec((1, LANES)), _const_spec((1, LANES)),
                  anyspec, anyspec, anyspec, anyspec, anyspec],
        out_specs=[sample_rows, sample_rows3, pl.BlockSpec((8, TM), lambda g: (0, N_PROMPT_BLOCKS)), sample_lanes,
                   pl.BlockSpec((None, 1, LANES), lambda g: (N_PROMPT_BLOCKS, 0, 0)),
                   cache_spec, cache_spec,
                   _const_spec((T_SAMPLE, A_WIDTH)), _const_spec((1, LANES))],
        out_shape=[jax.ShapeDtypeStruct((T_ALL, D_MODEL), F32), jax.ShapeDtypeStruct((T_ALL, ROW_TILE, LANES), BF16),
                   jax.ShapeDtypeStruct((8, T_ALL), jnp.int32), jax.ShapeDtypeStruct((T_ALL, LANES), F32),
                   jax.ShapeDtypeStruct((N_ROW_BLOCKS, 1, LANES), F32),
                   jax.ShapeDtypeStruct((DEC_BATCH, B_KV_HEADS, B_HEAD_DIM, WINDOW), F32),
                   jax.ShapeDtypeStruct((DEC_BATCH, B_KV_HEADS, B_HEAD_DIM, WINDOW), F32),
                   jax.ShapeDtypeStruct((T_SAMPLE, A_WIDTH), F32), jax.ShapeDtypeStruct((1, LANES), F32)],
        scratch_shapes=[pltpu.VMEM((T_SAMPLE, Q_WIDTH), F32), pltpu.VMEM((T_SAMPLE, KV_WIDTH), F32),
                        pltpu.VMEM((T_SAMPLE, KV_WIDTH), F32), pltpu.VMEM((T_SAMPLE, D_MODEL), F32)],
        input_output_aliases={n_in: 0, n_in + 1: 1, n_in + 2: 2, n_in + 3: 3, n_in + 4: 4},
        compiler_params=_cparams(("arbitrary",)),
        name="mix0_sample",
    )(x_all, nm, win, lng, lnb, wcoef, bcoef, ck, cv, bias_sc, bias_sn, wout, nf, wr, br, cnt,
      x1_all, h_all, ri_all, rg_all, tc_all)


def _moe_metadata(rt_all, cnt, tcnt):
    counts = cnt[0, :N_EXPERTS].astype(jnp.int32)
    padded = (counts + MOE_BLK - 1) // MOE_BLK * MOE_BLK
    pad_end = jnp.cumsum(padded)
    pad_start = pad_end - padded
    experts = jnp.arange(N_EXPERTS, dtype=jnp.int32)
    n_valid = (pad_end[-1] // MOE_BLK).astype(jnp.int32).reshape(1)
    blk_start = jnp.arange(N_MOE_BLOCKS, dtype=jnp.int32) * MOE_BLK
    block_e = jnp.minimum(jnp.sum((blk_start[:, None] >= pad_end[None, :]).astype(jnp.int32), axis=1),
                          N_EXPERTS - 1).astype(jnp.int32)
    zero_start = (pad_start + counts).astype(jnp.int32)
    zero_len = (padded - counts).astype(jnp.int32)
    first = (blk_start == pad_start[block_e]).astype(jnp.int32)
    used = counts > 0
    parity = ((jnp.cumsum(used.astype(jnp.int32)) - 1) % 2)[block_e].astype(jnp.int32)
    nearest = lax.cummin(jnp.where(used, experts, N_EXPERTS)[::-1])[::-1]
    next_used = jnp.concatenate([nearest[1:], jnp.full((1,), N_EXPERTS, jnp.int32)])
    nxt = jnp.where(next_used < N_EXPERTS, next_used, -1)[block_e].astype(jnp.int32)
    plan = (block_e, first, parity, nxt, n_valid)
    runs = tcnt[:, 0, :N_EXPERTS].astype(jnp.int32)
    gruns = runs.reshape(N_DISPATCH_STEPS, DISPATCH_TILES, N_EXPERTS)
    gtot = jnp.sum(gruns, axis=1)
    gstart = jnp.cumsum(gtot, axis=1) - gtot
    shift = ((gstart[:, None, :] + jnp.cumsum(gruns, axis=1) - gruns).reshape(N_ROW_BLOCKS, N_EXPERTS)
             - (jnp.cumsum(runs, axis=1) - runs))
    shift_rows = jnp.repeat(jnp.transpose(shift), TM, axis=1)
    hit = rt_all[:TOP_K, None, :] == experts[None, :, None]
    gpos = (rt_all[2 * TOP_K:3 * TOP_K] + jnp.sum(jnp.where(hit, shift_rows[None], 0), axis=1)).reshape(N_SLOTS)
    grun_dst = pad_start[None, :] + jnp.cumsum(gtot, axis=0) - gtot
    cplan = (gpos.astype(jnp.int32), gtot.reshape(-1), grun_dst.reshape(-1).astype(jnp.int32))
    dplan = cplan + (jnp.concatenate([zero_start, zero_len, n_valid]),)
    return plan, dplan, cplan


RUN_PIECE = 32
DISPATCH_TILES = 3
DISPATCH_ROWS = DISPATCH_TILES * TM
N_DISPATCH_STEPS = N_ROW_BLOCKS // DISPATCH_TILES


def _for_run_pieces(n, start_piece):
    whole = n // RUN_PIECE

    def body(j, carry):
        start_piece(j * RUN_PIECE, RUN_PIECE)
        return carry

    lax.fori_loop(0, whole, body, 0)
    o = whole * RUN_PIECE
    bit = RUN_PIECE // 2
    while bit >= 1:
        take = (n & bit) != 0

        @pl.when(take)
        def _(o=o, bit=bit):
            start_piece(o, bit)

        o = o + jnp.where(take, bit, 0)
        bit //= 2


def _dispatch_kernel(run_ref, rdst_ref, zs_ref, h_ref, gpos_hbm, xs_ref, zero_scr, stage, pos_s, sem, zsem, psem):
    i = pl.program_id(0)

    def pos_copy(step, do):
        for kk in range(TOP_K):
            do(pltpu.make_async_copy(gpos_hbm.at[pl.ds(kk * T_ALL + step * DISPATCH_ROWS, DISPATCH_ROWS)],
                                     pos_s.at[pl.ds(kk * DISPATCH_ROWS, DISPATCH_ROWS)], psem))

    @pl.when(i == 0)
    def _():
        pos_copy(0, lambda cp: cp.start())
        zero_scr[...] = jnp.zeros_like(zero_scr)

        def pieces(e, do):
            off = zs_ref[e]
            rem = zs_ref[N_EXPERTS + e]
            bit = MOE_BLK // 2
            while bit >= 1:
                take = (rem & bit) != 0

                @pl.when(take)
                def _(off=off, bit=bit):
                    do(pltpu.make_async_copy(zero_scr.at[pl.ds(0, bit)], xs_ref.at[pl.ds(off, bit)], zsem))

                off = off + jnp.where(take, bit, 0)
                bit //= 2

        def start_e(e, c):
            pieces(e, lambda cp: cp.start())
            return c

        def wait_e(e, c):
            pieces(e, lambda cp: cp.wait())
            return c

        def tail(do):
            def step(b, c):
                do(pltpu.make_async_copy(zero_scr, xs_ref.at[pl.ds(b * MOE_BLK, MOE_BLK)], zsem))
                return c
            return step

        n_valid = zs_ref[2 * N_EXPERTS]
        lax.fori_loop(0, N_EXPERTS, start_e, 0)
        lax.fori_loop(n_valid, N_MOE_BLOCKS, tail(lambda cp: cp.start()), 0)
        lax.fori_loop(0, N_EXPERTS, wait_e, 0)
        lax.fori_loop(n_valid, N_MOE_BLOCKS, tail(lambda cp: cp.wait()), 0)

    slot = i % 2
    pos_copy(i, lambda cp: cp.wait())

    for s in range(2):
        @pl.when(slot == s)
        def _(s=s):
            def place(r, carry):
                row = h_ref[r]
                for kk in range(TOP_K):
                    stage[s, pos_s[kk * DISPATCH_ROWS + r]] = row
                return carry

            lax.fori_loop(0, DISPATCH_ROWS, place, 0, unroll=32)

    @pl.when(i + 1 < N_DISPATCH_STEPS)
    def _():
        pos_copy(i + 1, lambda cp: cp.start())

    def send_run(e, off):
        n = run_ref[i * N_EXPERTS + e]
        dst = rdst_ref[i * N_EXPERTS + e]
        _for_run_pieces(n, lambda o, size: pltpu.make_async_copy(
            stage.at[slot, pl.ds(off + o, size)], xs_ref.at[pl.ds(dst + o, size)], sem.at[slot]).start(
                priority=size.bit_length() % 2))
        return off + n

    lax.fori_loop(0, N_EXPERTS, send_run, 0)

    def drain(s):
        pltpu.make_async_copy(stage.at[s], xs_ref.at[pl.ds(0, DISPATCH_ROWS * TOP_K)], sem.at[s]).wait()

    @pl.when(i >= 1)
    def _():
        drain(1 - slot)

    @pl.when(i == N_DISPATCH_STEPS - 1)
    def _():
        drain(slot)


def _dispatch(dplan, h_all):
    return pl.pallas_call(
        _dispatch_kernel,
        grid_spec=pltpu.PrefetchScalarGridSpec(
            num_scalar_prefetch=3,
            grid=(N_DISPATCH_STEPS,),
            in_specs=[pl.BlockSpec((DISPATCH_ROWS, ROW_TILE, LANES), lambda i, rn, rd, z: (i, 0, 0)),
                      pl.BlockSpec(memory_space=pl.ANY)],
            out_specs=pl.BlockSpec(memory_space=pl.ANY),
            scratch_shapes=[pltpu.VMEM((MOE_BLK, ROW_TILE, LANES), BF16),
                            pltpu.VMEM((2, DISPATCH_ROWS * TOP_K, ROW_TILE, LANES), BF16),
                            pltpu.SMEM((TOP_K * DISPATCH_ROWS,), jnp.int32),
                            pltpu.SemaphoreType.DMA((2,)), pltpu.SemaphoreType.DMA(()),
                            pltpu.SemaphoreType.DMA(())],
        ),
        out_shape=jax.ShapeDtypeStruct((N_SORT_ROWS, ROW_TILE, LANES), BF16),
        compiler_params=_cparams(("arbitrary",)),
        name="moe_dispatch",
    )(*dplan[1:], h_all, dplan[0])


def _experts_kernel(layer, be_ref, first_ref, par_ref, nxt_ref, nv_ref,
                    x_ref, wg_hbm, wu_hbm, wd_hbm, y_ref,
                    wg_s, wu_s, wd_s, wg_f, wu_f, wd_f, wsem):
    i = pl.program_id(0)

    def fetch(e, slot):
        return (pltpu.make_async_copy(wg_hbm.at[layer, e], wg_f.at[slot], wsem.at[slot]),
                pltpu.make_async_copy(wu_hbm.at[layer, e], wu_f.at[slot], wsem.at[slot]),
                pltpu.make_async_copy(wd_hbm.at[layer, e], wd_f.at[slot], wsem.at[slot]))

    @pl.when(i < nv_ref[0])
    def _():
        e = be_ref[i]
        slot = par_ref[i]

        @pl.when(i == 0)
        def _():
            for cp in fetch(e, slot):
                cp.start()

        def mlp(wg, wu, wd):
            xb = x_ref[...].reshape(MOE_BLK, D_MODEL)
            a = jax.nn.silu(_dot(xb, wg)) * _dot(xb, wu)
            y_ref[...] = _dot(a.astype(BF16), wd).reshape(y_ref.shape)

        @pl.when(first_ref[i] == 1)
        def _():
            for cp in fetch(e, slot):
                cp.wait()
            nxt = nxt_ref[i]

            @pl.when(nxt >= 0)
            def _():
                for cp in fetch(nxt, 1 - slot):
                    cp.start()

        @pl.when(first_ref[i] == 1)
        def _():
            wg = wg_f[slot].astype(BF16)
            wu = wu_f[slot].astype(BF16)
            wd = wd_f[slot].astype(BF16)
            wg_s[...] = wg
            wu_s[...] = wu
            wd_s[...] = wd
            mlp(wg, wu, wd)

        @pl.when(first_ref[i] != 1)
        def _():
            mlp(wg_s[...], wu_s[...], wd_s[...])

    @pl.when(i >= nv_ref[0])
    def _():
        y_ref[...] = jnp.zeros(y_ref.shape, y_ref.dtype)


def _experts(block_e, first, parity, nxt, n_valid, xs, w_gate, w_up, w_down, layer):
    def blk(i, be, fi, pa, nx, nv):
        return (jnp.maximum(jnp.minimum(i, nv[0] - 1), 0), 0, 0)

    anyspec = pl.BlockSpec(memory_space=pl.ANY)
    return pl.pallas_call(
        functools.partial(_experts_kernel, layer),
        grid_spec=pltpu.PrefetchScalarGridSpec(
            num_scalar_prefetch=5,
            grid=(N_MOE_BLOCKS,),
            in_specs=[pl.BlockSpec((MOE_BLK, ROW_TILE, LANES), blk), anyspec, anyspec, anyspec],
            out_specs=pl.BlockSpec((MOE_BLK, ROW_TILE, LANES), lambda i, be, fi, pa, nx, nv: (i, 0, 0)),
            scratch_shapes=[pltpu.VMEM((D_MODEL, D_EXPERT), BF16), pltpu.VMEM((D_MODEL, D_EXPERT), BF16),
                            pltpu.VMEM((D_EXPERT, D_MODEL), BF16),
                            pltpu.VMEM((2, D_MODEL, D_EXPERT), F32), pltpu.VMEM((2, D_MODEL, D_EXPERT), F32),
                            pltpu.VMEM((2, D_EXPERT, D_MODEL), F32), pltpu.SemaphoreType.DMA((2,))],
        ),
        out_shape=jax.ShapeDtypeStruct((N_SORT_ROWS, ROW_TILE, LANES), F32),
        compiler_params=_cparams(("arbitrary",)),
        name="moe_experts",
    )(block_e, first, parity, nxt, n_valid, xs, w_gate, w_up, w_down)


def _gather_rows(lpos_ref, run_ref, rdst_ref, ys_ref, ystage, ybufs, sem, i):
    def fetch(group, buf):
        def fetch_run(e, off):
            n = run_ref[group * N_EXPERTS + e]
            src = rdst_ref[group * N_EXPERTS + e]
            _for_run_pieces(n, lambda o, size: pltpu.make_async_copy(
                ys_ref.at[pl.ds(src + o, size)], ystage.at[buf, pl.ds(off + o, size)], sem.at[buf]).start(
                    priority=size.bit_length() % 2))
            return off + n

        lax.fori_loop(0, N_EXPERTS, fetch_run, 0)

    def wait(buf):
        pltpu.make_async_copy(ys_ref.at[pl.ds(0, DISPATCH_ROWS * TOP_K)], ystage.at[buf], sem.at[buf]).wait()

    cur = i % 2
    group = i // DISPATCH_TILES
    phase = i % DISPATCH_TILES
    last_of_group = phase == DISPATCH_TILES - 1

    @pl.when(i == 0)
    def _():
        fetch(0, 0)
        wait(0)

        def unplace(r, carry):
            for kk in range(TOP_K):
                ybufs[0][kk, r] = ystage[0, lpos_ref[kk * T_ALL + r]]
            return carry

        lax.fori_loop(0, TM, unplace, 0, unroll=8)
        fetch(1, 1)

    @pl.when(last_of_group & (group + 1 < N_DISPATCH_STEPS))
    def _():
        wait((group + 1) % 2)

    @pl.when(last_of_group & (group + 2 < N_DISPATCH_STEPS))
    def _():
        fetch(group + 2, group % 2)

    def pieces(compute, store):
        nxt = jnp.minimum(i + 1, N_ROW_BLOCKS - 1)
        nslot = (nxt // DISPATCH_TILES) % 2

        def variant(par):
            ycur, ynext = ybufs[par], ybufs[1 - par]

            def piece(j, carry):
                rows = pl.ds(pl.multiple_of(j * COMBINE_ROWS, COMBINE_ROWS), COMBINE_ROWS)
                out = compute(rows, ycur[0, rows].reshape(COMBINE_ROWS, D_MODEL),
                              ycur[1, rows].reshape(COMBINE_ROWS, D_MODEL))
                base = nxt * TM + j * COMBINE_ROWS
                for r in range(COMBINE_ROWS):
                    for kk in range(TOP_K):
                        ynext[kk, j * COMBINE_ROWS + r] = ystage[nslot, lpos_ref[kk * T_ALL + base + r]]
                store(rows, out)
                return carry

            lax.fori_loop(0, TM // COMBINE_ROWS, piece, 0)

        for par in range(2):
            @pl.when(cur == par)
            def _(par=par):
                variant(par)

    return pieces


COMBINE_ROWS = 64


def _combined(x_ref, rg_ref, rows, y0, y1):
    rg = rg_ref[rows, :]
    return x_ref[rows, :] + rg[:, 0:1] * y0 + rg[:, 1:2] * y1


_COMBINE_SCRATCH = [pltpu.VMEM((2, DISPATCH_ROWS * TOP_K, ROW_TILE, LANES), F32),
                    pltpu.VMEM((TOP_K, TM, ROW_TILE, LANES), F32), pltpu.VMEM((TOP_K, TM, ROW_TILE, LANES), F32),
                    pltpu.SemaphoreType.DMA((2,))]


def _combine_kernel(lpos_ref, run_ref, rdst_ref, x_ref, rg_ref, ys_ref, o_ref, ystage, ybuf0, ybuf1, sem):
    pieces = _gather_rows(lpos_ref, run_ref, rdst_ref, ys_ref, ystage, (ybuf0, ybuf1), sem, pl.program_id(0))

    def store(rows, out):
        o_ref[rows, :] = out

    pieces(functools.partial(_combined, x_ref, rg_ref), store)


def _combine(cplan, x_all, rg_all, ys):
    return pl.pallas_call(
        _combine_kernel,
        grid_spec=pltpu.PrefetchScalarGridSpec(
            num_scalar_prefetch=3,
            grid=(N_ROW_BLOCKS,),
            in_specs=[pl.BlockSpec((TM, D_MODEL), lambda i, a, b, c: (i, 0)),
                      pl.BlockSpec((TM, LANES), lambda i, a, b, c: (i, 0)),
                      pl.BlockSpec(memory_space=pl.ANY)],
            out_specs=pl.BlockSpec((TM, D_MODEL), lambda i, a, b, c: (i, 0)),
            scratch_shapes=_COMBINE_SCRATCH,
        ),
        out_shape=jax.ShapeDtypeStruct((T_ALL, D_MODEL), F32),
        compiler_params=_cparams(("arbitrary",)),
        name="moe_combine",
    )(*cplan, x_all, rg_all, ys)


def _final_kernel(lpos_ref, run_ref, rdst_ref, x_ref, rg_ref, ys_ref, nfin_ref, op_ref, os_ref,
                  ystage, ybuf0, ybuf1, sem):
    i = pl.program_id(0)
    pieces = _gather_rows(lpos_ref, run_ref, rdst_ref, ys_ref, ystage, (ybuf0, ybuf1), sem, i)

    def compute(rows, y0, y1):
        return _rms(_combined(x_ref, rg_ref, rows, y0, y1), nfin_ref[...])

    def store(rows, y):
        @pl.when(i < N_PROMPT_BLOCKS)
        def _():
            op_ref[rows, :] = y

        @pl.when(i >= N_PROMPT_BLOCKS)
        def _():
            os_ref[rows, :] = y

    pieces(compute, store)


def _final(cplan, x_all, rg_all, ys, nfin):
    return pl.pallas_call(
        _final_kernel,
        grid_spec=pltpu.PrefetchScalarGridSpec(
            num_scalar_prefetch=3,
            grid=(N_ROW_BLOCKS,),
            in_specs=[pl.BlockSpec((TM, D_MODEL), lambda i, a, b, c: (i, 0)),
                      pl.BlockSpec((TM, LANES), lambda i, a, b, c: (i, 0)),
                      pl.BlockSpec(memory_space=pl.ANY),
                      pl.BlockSpec((1, D_MODEL), lambda i, a, b, c: (0, 0))],
            out_specs=[pl.BlockSpec((TM, D_MODEL), lambda i, a, b, c: (jnp.minimum(i, N_PROMPT_BLOCKS - 1), 0)),
                       pl.BlockSpec((TM, D_MODEL), lambda i, a, b, c: (0, 0))],
            scratch_shapes=_COMBINE_SCRATCH,
        ),
        out_shape=[jax.ShapeDtypeStruct((T_PROMPT, D_MODEL), F32), jax.ShapeDtypeStruct((T_SAMPLE, D_MODEL), F32)],
        compiler_params=_cparams(("arbitrary",)),
        name="moe_combine_final",
    )(*cplan, x_all, rg_all, ys, nfin)


def _moe(h_all, rt_all, cnt, tcnt, w_gate, w_up, w_down, layer):
    plan, dplan, cplan = _moe_metadata(rt_all, cnt, tcnt)
    xs = _dispatch(dplan, h_all)
    ys = _experts(*plan, xs, w_gate, w_up, w_down, layer)
    return cplan, ys


def _pool_project(d_groups, wp_ref, scale):
    outs = [_dot(d_groups[g].astype(BF16), wp_ref[g]) for g in range(len(POOL_SIZES))]
    return jnp.concatenate(outs, axis=1) * scale


def _mix1_prompt_kernel(x_ref, nm_ref, wp_ref, sc_ref, nf_ref, wr_ref, br_ref,
                        x3_ref, h_ref, ri_ref, rg_ref, tc_ref, pl_ref, cnt_ref, ext):
    i = pl.program_id(0)

    @pl.when(i == 0)
    def _():
        cnt_ref[...] = jnp.zeros_like(cnt_ref)

    x = x_ref[...]
    hp = _rms(x, nm_ref[...])

    @pl.when(i % STEPS_PER_BATCH == 0)
    def _():
        ext[0:POOL_MAX, :] = jnp.zeros((POOL_MAX, D_MODEL), F32)

    ext[POOL_MAX:, :] = hp
    pos = (i % STEPS_PER_BATCH) * TM + lax.broadcasted_iota(jnp.int32, (TM, 1), 0)
    d_groups = []
    for g, w in enumerate(POOL_SIZES):
        cols = slice(g * POOL_GROUP_DIM, (g + 1) * POOL_GROUP_DIM)
        acc = ext[:, cols]
        span = 1
        while span < w:
            acc = acc + pltpu.roll(acc, span, 0)
            span *= 2
        cnt = jnp.minimum(pos + 1, w).astype(F32)
        d_groups.append(acc[POOL_MAX:] / cnt - hp[:, cols])
    tail = hp[TM - POOL_MAX:, :]
    ext[0:POOL_MAX, :] = tail
    pl_ref[...] = tail

    x3 = x + _pool_project(d_groups, wp_ref, sc_ref[...])
    x3_ref[...] = x3
    h, ids, gates = _route(x3, nf_ref[...], wr_ref[...], br_ref[...])
    h_ref[...] = h.reshape(h_ref.shape)
    ri_ref[...] = _rank_pack(ids, cnt_ref, tc_ref)
    rg_ref[...] = gates


def _mix1_prompt(x_all, nm, wp, sc, nf, wr, br):
    row_spec = pl.BlockSpec((TM, D_MODEL), lambda i: (i, 0))
    row3_spec = pl.BlockSpec((TM, ROW_TILE, LANES), lambda i: (i, 0, 0))
    lane_spec = pl.BlockSpec((TM, LANES), lambda i: (i, 0))
    return pl.pallas_call(
        _prompt_steps(_mix1_prompt_kernel, 7),
        grid=(N_ROW_BLOCKS,),
        in_specs=[row_spec, _const_spec((1, D_MODEL)),
                  _const_spec((len(POOL_SIZES), POOL_GROUP_DIM, POOL_GROUP_DIM)), _const_spec((1, D_MODEL)),
                  _const_spec((1, D_MODEL)), _const_spec((D_MODEL, 2 * LANES)), _const_spec((1, LANES))],
        out_specs=[row_spec, row3_spec, pl.BlockSpec((8, TM), lambda i: (0, i)), lane_spec,
                   pl.BlockSpec((None, 1, LANES), lambda i: (i, 0, 0)),
                   pl.BlockSpec((None, POOL_MAX, D_MODEL),
                                lambda i: (jnp.minimum(i // STEPS_PER_BATCH, BATCH - 1), 0, 0)),
                   _const_spec((1, LANES))],
        out_shape=[jax.ShapeDtypeStruct((T_ALL, D_MODEL), F32), jax.ShapeDtypeStruct((T_ALL, ROW_TILE, LANES), BF16),
                   jax.ShapeDtypeStruct((8, T_ALL), jnp.int32), jax.ShapeDtypeStruct((T_ALL, LANES), F32),
                   jax.ShapeDtypeStruct((N_ROW_BLOCKS, 1, LANES), F32),
                   jax.ShapeDtypeStruct((BATCH, POOL_MAX, D_MODEL), F32), jax.ShapeDtypeStruct((1, LANES), F32)],
        scratch_shapes=[pltpu.VMEM((POOL_MAX + TM, D_MODEL), F32)],
        compiler_params=_cparams(("arbitrary",)),
        name="mix1_prompt",
    )(x_all, nm, wp, sc, nf, wr, br)


def _mix1_sample_kernel(x_ref, st_ref, nm_ref, wp_ref, sc_ref, nf_ref, wr_ref, br_ref, cnt_in,
                        x3_in, h_in, ri_in, rg_in, tc_in,
                        x3_ref, h_ref, ri_ref, rg_ref, tc_ref, hs_ref, cnt_ref):
    del x3_in, h_in, ri_in, rg_in, tc_in
    x = x_ref[...]
    hs = _rms(x, nm_ref[...])
    hs_ref[...] = hs
    n_ctx = POOL_MAX - 1
    d_groups = []
    for g, w in enumerate(POOL_SIZES):
        cols = slice(g * POOL_GROUP_DIM, (g + 1) * POOL_GROUP_DIM)
        parts = []
        for t in range(DEC_SEQ):
            acc = hs[t * DEC_BATCH:(t + 1) * DEC_BATCH, cols]
            for back in range(1, w):
                src = t - back
                if src >= 0:
                    acc = acc + hs[src * DEC_BATCH:(src + 1) * DEC_BATCH, cols]
                else:
                    acc = acc + st_ref[n_ctx + src, :, cols]
            parts.append(acc / float(w) - hs[t * DEC_BATCH:(t + 1) * DEC_BATCH, cols])
        d_groups.append(jnp.concatenate(parts, axis=0))
    x3 = x + _pool_project(d_groups, wp_ref, sc_ref[...])
    x3_ref[...] = x3
    h, ids, gates = _route(x3, nf_ref[...], wr_ref[...], br_ref[...])
    h_ref[...] = h.reshape(h_ref.shape)
    cnt_ref[...] = cnt_in[...]
    ri_ref[...] = _rank_pack(ids, cnt_ref, tc_ref)
    rg_ref[...] = gates


def _mix1_sample(x_all, state_t, nm, wp, sc, nf, wr, br, cnt, x3_all, h_all, ri_all, rg_all, tc_all):
    sample_rows = pl.BlockSpec((TM, D_MODEL), lambda g: (N_PROMPT_BLOCKS, 0))
    sample_rows3 = pl.BlockSpec((TM, ROW_TILE, LANES), lambda g: (N_PROMPT_BLOCKS, 0, 0))
    sample_lanes = pl.BlockSpec((TM, LANES), lambda g: (N_PROMPT_BLOCKS, 0))
    anyspec = pl.BlockSpec(memory_space=pl.ANY)
    n_in = 9
    return pl.pallas_call(
        _mix1_sample_kernel,
        grid=(1,),
        in_specs=[sample_rows, _const_spec((POOL_MAX - 1, DEC_BATCH, D_MODEL)), _const_spec((1, D_MODEL)),
                  _const_spec((len(POOL_SIZES), POOL_GROUP_DIM, POOL_GROUP_DIM)), _const_spec((1, D_MODEL)),
                  _const_spec((1, D_MODEL)), _const_spec((D_MODEL, 2 * LANES)), _const_spec((1, LANES)),
                  _const_spec((1, LANES)), anyspec, anyspec, anyspec, anyspec, anyspec],
        out_specs=[sample_rows, sample_rows3, pl.BlockSpec((8, TM), lambda g: (0, N_PROMPT_BLOCKS)), sample_lanes,
                   pl.BlockSpec((None, 1, LANES), lambda g: (N_PROMPT_BLOCKS, 0, 0)),
                   _const_spec((T_SAMPLE, D_MODEL)), _const_spec((1, LANES))],
        out_shape=[jax.ShapeDtypeStruct((T_ALL, D_MODEL), F32), jax.ShapeDtypeStruct((T_ALL, ROW_TILE, LANES), BF16),
                   jax.ShapeDtypeStruct((8, T_ALL), jnp.int32), jax.ShapeDtypeStruct((T_ALL, LANES), F32),
                   jax.ShapeDtypeStruct((N_ROW_BLOCKS, 1, LANES), F32),
                   jax.ShapeDtypeStruct((T_SAMPLE, D_MODEL), F32), jax.ShapeDtypeStruct((1, LANES), F32)],
        input_output_aliases={n_in: 0, n_in + 1: 1, n_in + 2: 2, n_in + 3: 3, n_in + 4: 4},
        compiler_params=_cparams(("arbitrary",)),
        name="mix1_sample",
    )(x_all, state_t, nm, wp, sc, nf, wr, br, cnt, x3_all, h_all, ri_all, rg_all, tc_all)


def _router_weights(wg, bg, we, be):
    w = jnp.concatenate([wg, jnp.transpose(we, (1, 0, 2)).reshape(D_MODEL, N_EXPERTS)], axis=1)
    b = jnp.concatenate([bg, be.reshape(N_EXPERTS)])
    pad = LANES - N_GROUPS - N_EXPERTS
    w = jnp.pad(w, ((0, 0), (0, pad)))
    w_hi = w.astype(BF16)
    w_lo = (w - w_hi.astype(F32)).astype(BF16)
    return jnp.concatenate([w_hi, w_lo], axis=1), jnp.pad(b, (0, pad)).reshape(1, LANES)


def kernel(x_prompt, x_sample, cache_k_win, cache_v_win, state_pool, norm_mix, norm_ffn, norm_final, w_in,
           a_ln_g, a_ln_b, a_w_s, a_b_s, b_sinks, rel_bias_table, w_out, c_w_pool, c_scale,
           router_group_w, router_group_b, router_expert_w, router_expert_b, w_gate, w_up, w_down):
    xs_t = jnp.transpose(x_sample, (1, 0, 2)).reshape(T_SAMPLE, D_MODEL)
    xp2 = x_prompt.reshape(T_PROMPT, D_MODEL)
    win =w_in[0].astype(BF16)
    wout = w_out[0].astype(BF16)
    lng = a_ln_g[0].reshape(1, A_WIDTH)
    lnb = a_ln_b[0].reshape(1, A_WIDTH)
    bias_p, bias_sc, bias_sn, wsp = _prep(rel_bias_table, b_sinks[0], a_w_s[0])
    bs_full = jnp.repeat(a_b_s[0].T, A_HEAD_DIM, axis=1)
    w4 = jnp.transpose(a_w_s[0][:, :DEC_SEQ, :DEC_SEQ], (1, 2, 0)).reshape(DEC_SEQ * DEC_SEQ, A_HEADS)
    wcoef = jnp.repeat(w4, A_HEAD_DIM, axis=1)
    bcoef = jnp.pad(jnp.repeat(a_b_s[0][:, :DEC_SEQ].T, A_HEAD_DIM, axis=1), ((0, 8 - DEC_SEQ), (0, 0)))
    ck = jnp.transpose(cache_k_win[0], (0, 2, 3, 1))
    cv = jnp.transpose(cache_v_win[0], (0, 2, 3, 1))
    routers = [_router_weights(router_group_w[l], router_group_b[l], router_expert_w[l], router_expert_b[l])
               for l in range(2)]
    nm = [norm_mix[l].reshape(1, D_MODEL) for l in range(2)]
    nf = [norm_ffn[l].reshape(1, D_MODEL) for l in range(2)]

    x1_all, h_all, ri_all, rg_all, tc_all, k_last, v_last, va_last, cnt0 = _mix0_prompt(
        xp2, nm[0], win, lng, lnb, wsp, bs_full, bias_p, wout, nf[0], *routers[0])
    x1_all, h_all, ri_all, rg_all, tc_all, k_new, v_new, va_s, cnt0 = _mix0_sample(
        xs_t, nm[0], win, lng, lnb, wcoef, bcoef, ck, cv, bias_sc, bias_sn, wout, nf[0], *routers[0], cnt0,
        x1_all, h_all, ri_all, rg_all, tc_all)
    cplan0, ys0 = _moe(h_all, ri_all, cnt0, tc_all, w_gate, w_up, w_down, 0)
    x2_all = _combine(cplan0, x1_all, rg_all, ys0)

    wp = c_w_pool[0].astype(BF16)
    sc = c_scale[0].reshape(1, D_MODEL)
    x3_all, h2_all, ri2_all, rg2_all, tc2_all, pool_tail, cnt1 = _mix1_prompt(
        x2_all, nm[1], wp, sc, nf[1], *routers[1])
    state_t = jnp.transpose(state_pool[0], (1, 0, 2))
    x3_all, h2_all, ri2_all, rg2_all, tc2_all, hs1, cnt1 = _mix1_sample(
        x2_all, state_t, nm[1], wp, sc, nf[1], *routers[1], cnt1, x3_all, h2_all, ri2_all, rg2_all, tc2_all)
    cplan1, ys1 = _moe(h2_all, ri2_all, cnt1, tc2_all, w_gate, w_up, w_down, 1)
    y_p, y_s = _final(cplan1, x3_all, rg2_all, ys1, norm_final.reshape(1, D_MODEL))

    def from_tmajor(a, width):
        return jnp.transpose(a.reshape(DEC_SEQ, DEC_BATCH, width), (1, 0, 2))

    y_prompt = y_p.reshape(BATCH, SEQ, D_MODEL)
    y_sample = from_tmajor(y_s, D_MODEL)
    win_k_p = k_last.reshape(1, BATCH, WINDOW, B_KV_HEADS, B_HEAD_DIM)
    win_v_p = v_last.reshape(1, BATCH, WINDOW, B_KV_HEADS, B_HEAD_DIM)
    win_k_s = jnp.transpose(k_new, (0, 3, 1, 2))[None]
    win_v_s = jnp.transpose(v_new, (0, 3, 1, 2))[None]
    chunk_v_p = va_last.reshape(1, BATCH, CHUNK, A_HEADS, A_HEAD_DIM)
    chunk_v_s = from_tmajor(va_s, A_WIDTH).reshape(1, DEC_BATCH, DEC_SEQ, A_HEADS, A_HEAD_DIM)
    pool_p = pool_tail[:, 1:][None]
    pool_s = jnp.concatenate([state_pool[0][:, DEC_SEQ:], from_tmajor(hs1, D_MODEL)], axis=1)[None]
    return (y_prompt, y_sample, win_k_p, win_v_p, win_k_s, win_v_s, chunk_v_p, chunk_v_s, pool_p, pool_s)
```

```python
import functools
import math

import numpy as np
import jax
import jax.numpy as jnp
from jax import lax
from jax.experimental import pallas as pl
from jax.experimental.pallas import tpu as pltpu

F32 = jnp.float32
BF16 = jnp.bfloat16

D_MODEL = 1024
BATCH = 2
SEQ = 8192
DEC_BATCH = 128
DEC_SEQ = 4
A_WIDTH = 512
A_HEADS = 8
A_HEAD_DIM = 64
CHUNK = 128
B_HEADS = 8
B_KV_HEADS = 2
B_HEAD_DIM = 64
B_GROUP = 4
WINDOW = 128
N_BUCKETS = 32
MAX_DISTANCE = WINDOW
Q_WIDTH = 512
KV_WIDTH = 128
IN_WIDTH = 2 * A_WIDTH + Q_WIDTH + 2 * KV_WIDTH
ATTN_SCALE = B_HEAD_DIM ** -0.5
NEG_INF = -1e30
POOL_SIZES = (2, 4, 8, 16)
POOL_GROUP_DIM = 256
POOL_MAX = 16
N_GROUPS = 4
EXPERTS_PER_GROUP = 8
N_EXPERTS = 32
TOP_K = 2
D_EXPERT = 512
EPS = 1e-6

LANES = 128
ROW_TILE = D_MODEL // LANES
T_PROMPT = BATCH * SEQ
T_SAMPLE = DEC_BATCH * DEC_SEQ
T_ALL = T_PROMPT + T_SAMPLE
TM = 512
N_PROMPT_BLOCKS = T_PROMPT // TM
N_ROW_BLOCKS = T_ALL // TM
STEPS_PER_BATCH = SEQ // TM
SUB = TM // WINDOW
N_SLOTS = T_ALL * TOP_K
MOE_BLK = 512
N_MOE_BLOCKS = N_SLOTS // MOE_BLK + N_EXPERTS
N_SORT_ROWS = N_MOE_BLOCKS * MOE_BLK
SAMPLE_GROUP = 8
N_SAMPLE_GROUPS = DEC_BATCH // SAMPLE_GROUP
VMEM_LIMIT = 56 * 1024 * 1024

STACK_HEADS = ((0, 2, 5, 7), (1, 3, 4, 6))


def _t5_bucket_np(dist):
    n = np.maximum(dist, 0)
    max_exact = N_BUCKETS // 2
    nf = np.maximum(n, 1).astype(np.float32)
    large = max_exact + (np.log(nf / np.float32(max_exact)) / np.float32(math.log(MAX_DISTANCE / max_exact))
                         * np.float32(N_BUCKETS - max_exact)).astype(np.int32)
    large = np.minimum(large, N_BUCKETS - 1)
    return np.where(n < max_exact, n, large).astype(np.int32)


def _bucket_tables():
    qi = np.arange(WINDOW)[:, None]
    ki = np.arange(2 * WINDOW)[None, :]
    dist = qi + WINDOW - ki
    valid = (dist >= 0) & (dist < WINDOW)
    bp = np.where(valid, _t5_bucket_np(dist), -1)
    bp_first = np.where(ki >= WINDOW, bp, -1)
    bkt_p = np.stack([bp_first, bp]).astype(np.int32)

    t = np.repeat(np.arange(DEC_SEQ), SAMPLE_GROUP)[:, None]
    b = np.tile(np.arange(SAMPLE_GROUP), DEC_SEQ)[:, None]
    cb = np.repeat(np.arange(SAMPLE_GROUP), WINDOW)[None, :]
    cj = np.tile(np.arange(WINDOW), SAMPLE_GROUP)[None, :]
    dist_c = t + WINDOW - cj
    valid_c = (cb == b) & (dist_c >= 0) & (dist_c < WINDOW)
    bkt_sc = np.where(valid_c, _t5_bucket_np(dist_c), -1).astype(np.int32)
    nt = np.repeat(np.arange(DEC_SEQ), SAMPLE_GROUP)[None, :]
    nb = np.tile(np.arange(SAMPLE_GROUP), DEC_SEQ)[None, :]
    dist_n = t - nt
    valid_n = (nb == b) & (dist_n >= 0)
    bkt_sn = np.where(valid_n, _t5_bucket_np(dist_n), -1).astype(np.int32)
    bkt_sn = np.concatenate([bkt_sn, np.full((32, LANES - 32), -1, np.int32)], axis=1)
    return bkt_p, bkt_sc, bkt_sn


_BKT_P, _BKT_SC, _BKT_SN = _bucket_tables()


def _cparams(semantics):
    return pltpu.CompilerParams(dimension_semantics=semantics, vmem_limit_bytes=VMEM_LIMIT)


def _rms(x, g):
    return x * lax.rsqrt(jnp.mean(x * x, axis=-1, keepdims=True) + EPS) * g


def _layernorm(x, g, b):
    xc = x - jnp.mean(x, axis=-1, keepdims=True)
    return xc * lax.rsqrt(jnp.mean(xc * xc, axis=-1, keepdims=True) + EPS) * g + b


def _dot(a, b):
    return jnp.dot(a, b, preferred_element_type=F32)


def _dot_nt(a, b):
    return lax.dot_general(a, b, (((1,), (1,)), ((), ())), preferred_element_type=F32)


def _project(x, nm, win, lng, lnb):
    h = _rms(x, nm)
    z = _dot(h.astype(BF16), win)
    u = jax.nn.gelu(z[:, :A_WIDTH])
    va = _layernorm(jax.nn.gelu(z[:, A_WIDTH:2 * A_WIDTH]), lng, lnb)
    q = z[:, 2 * A_WIDTH:2 * A_WIDTH + Q_WIDTH] * ATTN_SCALE
    k = z[:, 2 * A_WIDTH + Q_WIDTH:2 * A_WIDTH + Q_WIDTH + KV_WIDTH]
    v = z[:, 2 * A_WIDTH + Q_WIDTH + KV_WIDTH:]
    return u, va, q, k, v


def _route(x1, nf, wr, br):
    hf = _rms(x1, nf)
    h = hf.astype(BF16)
    h_lo = (hf - h.astype(F32)).astype(BF16)
    part = _dot(h, wr)
    logits = part[:, :LANES] + part[:, LANES:] + _dot(h_lo, wr[:, :LANES]) + br
    rows = logits.shape[0]
    lane = lax.broadcasted_iota(jnp.int32, (rows, LANES), 1)
    lanef = lane.astype(F32)
    big = jnp.float32(1e9)
    is_g = lane < N_GROUPS
    gl = jnp.where(is_g, logits, -jnp.inf)
    gmax = jnp.max(gl, axis=1, keepdims=True)
    gsel = jnp.min(jnp.where(gl == gmax, lanef, big), axis=1, keepdims=True)
    gsum = jnp.sum(jnp.where(is_g, jnp.exp(logits - gmax), 0.0), axis=1, keepdims=True)
    g1 = 1.0 / gsum
    lo = N_GROUPS + EXPERTS_PER_GROUP * gsel
    emask = (lanef >= lo) & (lanef < lo + EXPERTS_PER_GROUP)
    el = jnp.where(emask, logits, -jnp.inf)
    v1 = jnp.max(el, axis=1, keepdims=True)
    i1 = jnp.min(jnp.where(el == v1, lanef, big), axis=1, keepdims=True)
    el2 = jnp.where(lanef == i1, -jnp.inf, el)
    v2 = jnp.max(el2, axis=1, keepdims=True)
    i2 = jnp.min(jnp.where(el2 == v2, lanef, big), axis=1, keepdims=True)
    e2 = jnp.exp(v2 - v1)
    den = 1.0 + e2
    w1 = g1 / den
    w2 = g1 * e2 / den
    ids = jnp.where(lane == 0, i1 - N_GROUPS, jnp.where(lane == 1, i2 - N_GROUPS, 0.0)).astype(jnp.int32)
    gates = jnp.where(lane == 0, w1, jnp.where(lane == 1, w2, 0.0))
    return h, ids, gates


def _earlier_mask(rows):
    r = lax.broadcasted_iota(jnp.int32, (rows, rows), 0)
    c = lax.broadcasted_iota(jnp.int32, (rows, rows), 1)
    return (c < r).astype(BF16)


def _rank_pack(ids, cnt_ref, tcnt_ref, tri_ref=None):
    rows = ids.shape[0]
    lane = lax.broadcasted_iota(jnp.int32, (rows, LANES), 1)
    o0 = (lane == ids[:, 0:1]).astype(F32)
    o1 = (lane == ids[:, 1:2]).astype(F32)
    before = _earlier_mask(rows) if tri_ref is None else tri_ref[...]
    p01 = _dot(before, jnp.concatenate([o0, o1], axis=1).astype(BF16))
    p0 = p01[:, :LANES]
    p1 = p01[:, LANES:]
    c0 = jnp.sum(o0, axis=0, keepdims=True)
    c1 = jnp.sum(o1, axis=0, keepdims=True)
    ctile = c0 + c1
    cnt_ref[...] = cnt_ref[...] + ctile
    tcnt_ref[...] = ctile
    inc = jnp.broadcast_to(ctile, (8, LANES))
    lane8 = lax.broadcasted_iota(jnp.int32, (8, LANES), 1)
    for sh in (1, 2, 4, 8, 16, 32, 64):
        inc = inc + jnp.where(lane8 >= sh, pltpu.roll(inc, sh, 1), 0.0)
    start = inc[0:1] - ctile
    lpos0 = jnp.sum(o0 * (start + p0), axis=1, keepdims=True)
    lpos1 = jnp.sum(o1 * (start + c0 + p1), axis=1, keepdims=True)
    idf = ids.astype(F32)
    packed = jnp.where(lane < TOP_K, idf, 0.0)
    for ln, col in ((4, lpos0), (5, lpos1)):
        packed = jnp.where(lane == ln, col, packed)
    return jnp.transpose(packed)[:8].astype(jnp.int32)


def _prep_kernel(tab_ref, sink_ref, bp_ref, bsc_ref, bsn_ref, ws_ref, op_ref, osc_ref, osn_ref, ows_ref):
    def fill(bkt, write, sink_col0):
        col0 = lax.broadcasted_iota(jnp.int32, bkt.shape, 1) == 0
        for st, heads in enumerate(STACK_HEADS):
            for slot, h in enumerate(heads):
                acc = jnp.full(bkt.shape, NEG_INF, F32)
                for b in range(N_BUCKETS):
                    acc = jnp.where(bkt == b, tab_ref[b, h], acc)
                if sink_col0:
                    acc = jnp.where(col0, sink_ref[0, h], acc)
                write(st, slot, acc)

    for var in range(2):
        def wr_p(st, slot, acc, var=var):
            op_ref[var, st, slot * WINDOW:(slot + 1) * WINDOW, :] = acc
        fill(bp_ref[var], wr_p, True)

    rows_s = DEC_SEQ * SAMPLE_GROUP

    def wr_sc(st, slot, acc):
        osc_ref[st, slot * rows_s:(slot + 1) * rows_s, :] = acc
    fill(bsc_ref[...], wr_sc, True)

    def wr_sn(st, slot, acc):
        osn_ref[st, slot * rows_s:(slot + 1) * rows_s, :] = acc
    fill(bsn_ref[...], wr_sn, False)

    r = lax.broadcasted_iota(jnp.int32, (CHUNK, CHUNK), 0)
    c = lax.broadcasted_iota(jnp.int32, (CHUNK, CHUNK), 1)
    for h in range(A_HEADS):
        ows_ref[h // 2, :, (h % 2) * CHUNK:(h % 2 + 1) * CHUNK] = jnp.where(r >= c, ws_ref[h], 0.0).astype(BF16)


def _prep(rel_bias_table, sinks, w_s):
    vm = pl.BlockSpec(memory_space=pltpu.VMEM)
    sm = pl.BlockSpec(memory_space=pltpu.SMEM)
    rows_s = DEC_SEQ * SAMPLE_GROUP
    return pl.pallas_call(
        _prep_kernel,
        in_specs=[sm, sm, vm, vm, vm, vm],
        out_specs=[vm, vm, vm, vm],
        out_shape=[
            jax.ShapeDtypeStruct((2, 2, 4 * WINDOW, 2 * WINDOW), F32),
            jax.ShapeDtypeStruct((2, 4 * rows_s, SAMPLE_GROUP * WINDOW), F32),
            jax.ShapeDtypeStruct((2, 4 * rows_s, LANES), F32),
            jax.ShapeDtypeStruct((A_HEADS // 2, CHUNK, 2 * CHUNK), BF16),
        ],
        name="prep_tables",
    )(rel_bias_table, sinks.reshape(1, B_HEADS), jnp.asarray(_BKT_P), jnp.asarray(_BKT_SC), jnp.asarray(_BKT_SN), w_s)


def _gate_pairs(va_rows, wsp_ref, lane_lo):
    outs = []
    for p in range(A_HEADS // 2):
        vp = va_rows[:, p * LANES:(p + 1) * LANES]
        rhs = jnp.concatenate([jnp.where(lane_lo, vp, 0.0), jnp.where(lane_lo, 0.0, vp)], axis=0).astype(BF16)
        outs.append(_dot(wsp_ref[p], rhs))
    return jnp.concatenate(outs, axis=1)


def _prompt_steps(body, first_row_out):
    def kern(*refs):
        i = pl.program_id(0)

        @pl.when(i < N_PROMPT_BLOCKS)
        def _():
            body(*refs)

        @pl.when(i >= N_PROMPT_BLOCKS)
        def _():
            for r in refs[first_row_out:first_row_out + 5]:
                r[...] = jnp.zeros(r.shape, r.dtype)

    return kern


def _mix0_prompt_kernel(x_ref, nm_ref, win_ref, lng_ref, lnb_ref, wsp_ref, bs_ref, bias_ref,
                        wout_ref, nf_ref, wr_ref, br_ref,
                        x1_ref, h_ref, ri_ref, rg_ref, tc_ref, kl_ref, vl_ref, val_ref, cnt_ref,
                        kprev, vprev, mix_scr, tri):
    @pl.when(pl.program_id(0) == 0)
    def _():
        cnt_ref[...] = jnp.zeros_like(cnt_ref)
        tri[...] = _earlier_mask(TM)

    x = x_ref[...]
    u, va, q, k, v = _project(x, nm_ref[...], win_ref[...], lng_ref[...], lnb_ref[...])
    lane_lo = lax.broadcasted_iota(jnp.int32, (WINDOW, LANES), 1) < B_HEAD_DIM
    row0 = lax.broadcasted_iota(jnp.int32, (WINDOW, KV_WIDTH), 0) == 0
    first = pl.program_id(0) % STEPS_PER_BATCH == 0

    @pl.when(first)
    def _():
        kprev[...] = jnp.zeros_like(kprev)
        vprev[...] = jnp.zeros_like(vprev)

    for j in range(SUB):
        rows = slice(j * WINDOW, (j + 1) * WINDOW)
        s_gate = _gate_pairs(va[rows], wsp_ref, lane_lo)
        mix_scr[rows, :A_WIDTH] = u[rows] * (s_gate + bs_ref[...])

        if j == 0:
            kp, vp = kprev[...], vprev[...]
        else:
            prows = slice((j - 1) * WINDOW, j * WINDOW)
            kp, vp = k[prows], v[prows]
        kk = jnp.concatenate([jnp.where(row0, 0.0, kp), k[rows]], axis=0)
        vv = jnp.concatenate([jnp.where(row0, 0.0, vp), v[rows]], axis=0)
        kops = (kk.astype(BF16), pltpu.roll(kk, B_HEAD_DIM, 1).astype(BF16))
        vops = (vv.astype(BF16), pltpu.roll(vv, B_HEAD_DIM, 1).astype(BF16))
        qt = [q[rows, p * LANES:(p + 1) * LANES] for p in range(4)]
        q_even = [jnp.where(lane_lo, t, 0.0) for t in qt]
        q_odd = [jnp.where(lane_lo, 0.0, t) for t in qt]
        stacks = (jnp.concatenate([q_even[0], q_even[1], q_odd[2], q_odd[3]], axis=0),
                  jnp.concatenate([q_odd[0], q_odd[1], q_even[2], q_even[3]], axis=0))
        o = []
        for st in range(2):
            s = _dot_nt(stacks[st].astype(BF16), kops[st])
            if j == 0:
                bias = bias_ref[jnp.where(first, 0, 1), st]
            else:
                bias = bias_ref[1, st]
            s = s + bias
            m = jnp.max(s, axis=-1, keepdims=True)
            p = jnp.exp(s - m)
            den = jnp.sum(p, axis=-1, keepdims=True)
            o.append(_dot(p.astype(BF16), vops[st]) / den)
        oa, ob = o
        sl = [slice(i * WINDOW, (i + 1) * WINDOW) for i in range(4)]
        tiles = (jnp.where(lane_lo, oa[sl[0]], ob[sl[0]]), jnp.where(lane_lo, oa[sl[1]], ob[sl[1]]),
                 jnp.where(lane_lo, ob[sl[2]], oa[sl[2]]), jnp.where(lane_lo, ob[sl[3]], oa[sl[3]]))
        for p in range(4):
            mix_scr[rows, A_WIDTH + p * LANES:A_WIDTH + (p + 1) * LANES] = tiles[p]

    last = slice(TM - WINDOW, TM)
    kprev[...] = k[last]
    vprev[...] = v[last]
    kl_ref[...] = k[last]
    vl_ref[...] = v[last]
    val_ref[...] = va[last]

    x1 = x + _dot(mix_scr[...].astype(BF16), wout_ref[...])
    x1_ref[...] = x1
    h, ids, gates = _route(x1, nf_ref[...], wr_ref[...], br_ref[...])
    h_ref[...] = h.reshape(h_ref.shape)
    ri_ref[...] = _rank_pack(ids, cnt_ref, tc_ref, tri)
    rg_ref[...] = gates


def _const_spec(shape):
    nd = len(shape)
    return pl.BlockSpec(shape, lambda i, _n=nd: (0,) * _n)


def _mix0_prompt(x_all, nm, win, lng, lnb, wsp, bs_full, bias_p, wout, nf, wr, br):
    row_spec = pl.BlockSpec((TM, D_MODEL), lambda i: (i, 0))
    row3_spec = pl.BlockSpec((TM, ROW_TILE, LANES), lambda i: (i, 0, 0))
    lane_spec = pl.BlockSpec((TM, LANES), lambda i: (i, 0))
    last_kv = pl.BlockSpec((None, WINDOW, KV_WIDTH), lambda i: (jnp.minimum(i // STEPS_PER_BATCH, BATCH - 1), 0, 0))
    last_va = pl.BlockSpec((None, WINDOW, A_WIDTH), lambda i: (jnp.minimum(i // STEPS_PER_BATCH, BATCH - 1), 0, 0))
    return pl.pallas_call(
        _prompt_steps(_mix0_prompt_kernel, 12),
        grid=(N_ROW_BLOCKS,),
        in_specs=[pl.BlockSpec((TM, D_MODEL), lambda i: (jnp.minimum(i, N_PROMPT_BLOCKS - 1), 0)),
                  _const_spec((1, D_MODEL)), _const_spec((D_MODEL, IN_WIDTH)),
                  _const_spec((1, A_WIDTH)), _const_spec((1, A_WIDTH)),
                  _const_spec((A_HEADS // 2, CHUNK, 2 * CHUNK)), _const_spec((CHUNK, A_WIDTH)),
                  _const_spec((2, 2, 4 * WINDOW, 2 * WINDOW)),
                  _const_spec((A_WIDTH + Q_WIDTH, D_MODEL)), _const_spec((1, D_MODEL)),
                  _const_spec((D_MODEL, 2 * LANES)), _const_spec((1, LANES))],
        out_specs=[row_spec, row3_spec, pl.BlockSpec((8, TM), lambda i: (0, i)), lane_spec,
                   pl.BlockSpec((None, 1, LANES), lambda i: (i, 0, 0)),
                   last_kv, last_kv, last_va, _const_spec((1, LANES))],
        out_shape=[jax.ShapeDtypeStruct((T_ALL, D_MODEL), F32), jax.ShapeDtypeStruct((T_ALL, ROW_TILE, LANES), BF16),
                   jax.ShapeDtypeStruct((8, T_ALL), jnp.int32), jax.ShapeDtypeStruct((T_ALL, LANES), F32),
                   jax.ShapeDtypeStruct((N_ROW_BLOCKS, 1, LANES), F32),
                   jax.ShapeDtypeStruct((BATCH, WINDOW, KV_WIDTH), F32),
                   jax.ShapeDtypeStruct((BATCH, WINDOW, KV_WIDTH), F32),
                   jax.ShapeDtypeStruct((BATCH, WINDOW, A_WIDTH), F32),
                   jax.ShapeDtypeStruct((1, LANES), F32)],
        scratch_shapes=[pltpu.VMEM((WINDOW, KV_WIDTH), F32), pltpu.VMEM((WINDOW, KV_WIDTH), F32),
                        pltpu.VMEM((TM, D_MODEL), F32), pltpu.VMEM((TM, TM), BF16)],
        compiler_params=_cparams(("arbitrary",)),
        name="mix0_prompt",
    )(x_all, nm, win, lng, lnb, wsp, bs_full, bias_p, wout, nf, wr, br)


def _mix0_sample_kernel(x_ref, nm_ref, win_ref, lng_ref, lnb_ref, wcoef_ref, bcoef_ref,
                        ck_ref, cv_ref, bsc_ref, bsn_ref,
                        wout_ref, nf_ref, wr_ref, br_ref, cnt_in,
                        x1_in, h_in, ri_in, rg_in, tc_in,
                        x1_ref, h_ref, ri_ref, rg_ref, tc_ref, kn_ref, vn_ref, va_ref, cnt_ref,
                        q_scr, k_scr, v_scr, mix_scr):
    del x1_in, h_in, ri_in, rg_in, tc_in
    g = pl.program_id(0)

    @pl.when(g == 0)
    def _():
        u, va, q, k, v = _project(x_ref[...], nm_ref[...], win_ref[...], lng_ref[...], lnb_ref[...])
        q_scr[...] = q
        k_scr[...] = k
        v_scr[...] = v
        va_ref[...] = va
        for t in range(DEC_SEQ):
            acc = jnp.zeros((DEC_BATCH, A_WIDTH), F32) + bcoef_ref[t:t + 1, :]
            for s in range(t + 1):
                row = t * DEC_SEQ + s
                acc = acc + wcoef_ref[row:row + 1, :] * va[s * DEC_BATCH:(s + 1) * DEC_BATCH]
            mix_scr[t * DEC_BATCH:(t + 1) * DEC_BATCH, :A_WIDTH] = u[t * DEC_BATCH:(t + 1) * DEC_BATCH] * acc

    b0 = pl.multiple_of(g * SAMPLE_GROUP, SAMPLE_GROUP)
    lane_lo = lax.broadcasted_iota(jnp.int32, (DEC_SEQ * SAMPLE_GROUP, LANES), 1) < B_HEAD_DIM

    def grab(ref, width):
        return jnp.concatenate([ref[pl.ds(t * DEC_BATCH + b0, SAMPLE_GROUP), :] for t in range(DEC_SEQ)], axis=0)

    qg = grab(q_scr, Q_WIDTH)
    kn = grab(k_scr, KV_WIDTH)
    vn = grab(v_scr, KV_WIDTH)

    lane_w = lax.broadcasted_iota(jnp.int32, (KV_WIDTH, WINDOW), 1)
    n_new = DEC_SEQ * SAMPLE_GROUP

    def new_window(c_ref, new_rows, w_ref):
        nt = jnp.transpose(jnp.concatenate([new_rows, jnp.zeros((WINDOW - n_new, KV_WIDTH), F32)], axis=0))
        for b in range(SAMPLE_GROUP):
            w = pltpu.roll(c_ref[b].reshape(KV_WIDTH, WINDOW), WINDOW - DEC_SEQ, 1)
            for t in range(DEC_SEQ):
                src = t * SAMPLE_GROUP + b
                dst = WINDOW - DEC_SEQ + t
                w = jnp.where(lane_w == dst, pltpu.roll(nt, (dst - src) % WINDOW, 1), w)
            w_ref[b] = w.reshape(B_KV_HEADS, B_HEAD_DIM, WINDOW)

    new_window(ck_ref, kn, kn_ref)
    new_window(cv_ref, vn, vn_ref)
    ccol0 = lax.broadcasted_iota(jnp.int32, (KV_WIDTH, SAMPLE_GROUP * WINDOW), 1) == 0

    def cache_t(ref):
        t = jnp.concatenate([ref[b].reshape(KV_WIDTH, WINDOW) for b in range(SAMPLE_GROUP)], axis=1)
        return jnp.where(ccol0, 0.0, t)

    def head_swap(t):
        return jnp.concatenate([t[B_HEAD_DIM:], t[:B_HEAD_DIM]], axis=0)

    kct = cache_t(ck_ref)
    vct = cache_t(cv_ref)
    kc_ops = (kct.astype(BF16), head_swap(kct).astype(BF16))
    vc_ops = (vct.astype(BF16), head_swap(vct).astype(BF16))
    kn_ops = (kn.astype(BF16), pltpu.roll(kn, B_HEAD_DIM, 1).astype(BF16))
    vn_ops = (vn.astype(BF16), pltpu.roll(vn, B_HEAD_DIM, 1).astype(BF16))
    qt = [qg[:, p * LANES:(p + 1) * LANES] for p in range(4)]
    q_even = [jnp.where(lane_lo, t, 0.0) for t in qt]
    q_odd = [jnp.where(lane_lo, 0.0, t) for t in qt]
    stacks = (jnp.concatenate([q_even[0], q_even[1], q_odd[2], q_odd[3]], axis=0),
              jnp.concatenate([q_odd[0], q_odd[1], q_even[2], q_even[3]], axis=0))
    o = []
    for st in range(2):
        qs = stacks[st].astype(BF16)
        sc = _dot(qs, kc_ops[st]) + bsc_ref[st]
        sn = _dot_nt(qs, kn_ops[st]) + bsn_ref[st][:, :DEC_SEQ * SAMPLE_GROUP]
        m = jnp.maximum(jnp.max(sc, axis=-1, keepdims=True), jnp.max(sn, axis=-1, keepdims=True))
        pc = jnp.exp(sc - m)
        pn = jnp.exp(sn - m)
        den = jnp.sum(pc, axis=-1, keepdims=True) + jnp.sum(pn, axis=-1, keepdims=True)
        o.append((_dot_nt(pc.astype(BF16), vc_ops[st]) + _dot(pn.astype(BF16), vn_ops[st])) / den)
    oa, ob = o
    n = DEC_SEQ * SAMPLE_GROUP
    sl = [slice(i * n, (i + 1) * n) for i in range(4)]
    tiles = (jnp.where(lane_lo, oa[sl[0]], ob[sl[0]]), jnp.where(lane_lo, oa[sl[1]], ob[sl[1]]),
             jnp.where(lane_lo, ob[sl[2]], oa[sl[2]]), jnp.where(lane_lo, ob[sl[3]], oa[sl[3]]))
    for p in range(4):
        for t in range(DEC_SEQ):
            mix_scr[pl.ds(t * DEC_BATCH + b0, SAMPLE_GROUP), A_WIDTH + p * LANES:A_WIDTH + (p + 1) * LANES] = (
                tiles[p][t * SAMPLE_GROUP:(t + 1) * SAMPLE_GROUP])

    @pl.when(g == N_SAMPLE_GROUPS - 1)
    def _():
        x1 = x_ref[...] + _dot(mix_scr[...].astype(BF16), wout_ref[...])
        x1_ref[...] = x1
        h, ids, gates = _route(x1, nf_ref[...], wr_ref[...], br_ref[...])
        h_ref[...] = h.reshape(h_ref.shape)
        cnt_ref[...] = cnt_in[...]
        ri_ref[...] = _rank_pack(ids, cnt_ref, tc_ref)
        rg_ref[...] = gates


def _mix0_sample(x_all, nm, win, lng, lnb, wcoef, bcoef, ck, cv, bias_sc, bias_sn, wout, nf, wr, br, cnt,
                 x1_all, h_all, ri_all, rg_all, tc_all):
    sample_rows = pl.BlockSpec((TM, D_MODEL), lambda g: (N_PROMPT_BLOCKS, 0))
    sample_rows3 = pl.BlockSpec((TM, ROW_TILE, LANES), lambda g: (N_PROMPT_BLOCKS, 0, 0))
    sample_lanes = pl.BlockSpec((TM, LANES), lambda g: (N_PROMPT_BLOCKS, 0))
    cache_spec = pl.BlockSpec((SAMPLE_GROUP, B_KV_HEADS, B_HEAD_DIM, WINDOW), lambda g: (g, 0, 0, 0))
    anyspec = pl.BlockSpec(memory_space=pl.ANY)
    n_in = 16
    return pl.pallas_call(
        _mix0_sample_kernel,
        grid=(N_SAMPLE_GROUPS,),
        in_specs=[_const_spec((TM, D_MODEL)), _const_spec((1, D_MODEL)), _const_spec((D_MODEL, IN_WIDTH)),
                  _const_spec((1, A_WIDTH)), _const_spec((1, A_WIDTH)),
                  _const_spec((16, A_WIDTH)), _const_spec((8, A_WIDTH)),
                  cache_spec, cache_spec,
                  _const_spec((2, 4 * 32, SAMPLE_GROUP * WINDOW)), _const_spec((2, 4 * 32, LANES)),
                  _const_spec((A_WIDTH + Q_WIDTH, D_MODEL)), _const_spec((1, D_MODEL)),
                  _const_spec((D_MODEL, 2 * LANES)), _const_spec((1, LANES)), _const_spec((1, LANES)),
                  anyspec, anyspec, anyspec, anyspec, anyspec],
        out_specs=[sample_rows, sample_rows3, pl.BlockSpec((8, TM), lambda g: (0, N_PROMPT_BLOCKS)), sample_lanes,
                   pl.BlockSpec((None, 1, LANES), lambda g: (N_PROMPT_BLOCKS, 0, 0)),
                   cache_spec, cache_spec,
                   _const_spec((T_SAMPLE, A_WIDTH)), _const_spec((1, LANES))],
        out_shape=[jax.ShapeDtypeStruct((T_ALL, D_MODEL), F32), jax.ShapeDtypeStruct((T_ALL, ROW_TILE, LANES), BF16),
                   jax.ShapeDtypeStruct((8, T_ALL), jnp.int32), jax.ShapeDtypeStruct((T_ALL, LANES), F32),
                   jax.ShapeDtypeStruct((N_ROW_BLOCKS, 1, LANES), F32),
                   jax.ShapeDtypeStruct((DEC_BATCH, B_KV_HEADS, B_HEAD_DIM, WINDOW), F32),
                   jax.ShapeDtypeStruct((DEC_BATCH, B_KV_HEADS, B_HEAD_DIM, WINDOW), F32),
                   jax.ShapeDtypeStruct((T_SAMPLE, A_WIDTH), F32), jax.ShapeDtypeStruct((1, LANES), F32)],
        scratch_shapes=[pltpu.VMEM((T_SAMPLE, Q_WIDTH), F32), pltpu.VMEM((T_SAMPLE, KV_WIDTH), F32),
                        pltpu.VMEM((T_SAMPLE, KV_WIDTH), F32), pltpu.VMEM((T_SAMPLE, D_MODEL), F32)],
        input_output_aliases={n_in: 0, n_in + 1: 1, n_in + 2: 2, n_in + 3: 3, n_in + 4: 4},
        compiler_params=_cparams(("arbitrary",)),
        name="mix0_sample",
    )(x_all, nm, win, lng, lnb, wcoef, bcoef, ck, cv, bias_sc, bias_sn, wout, nf, wr, br, cnt,
      x1_all, h_all, ri_all, rg_all, tc_all)


def _moe_metadata(rt_all, cnt, tcnt):
    counts = cnt[0, :N_EXPERTS].astype(jnp.int32)
    padded = (counts + MOE_BLK - 1) // MOE_BLK * MOE_BLK
    pad_end = jnp.cumsum(padded)
    pad_start = pad_end - padded
    experts = jnp.arange(N_EXPERTS, dtype=jnp.int32)
    n_valid = (pad_end[-1] // MOE_BLK).astype(jnp.int32).reshape(1)
    blk_start = jnp.arange(N_MOE_BLOCKS, dtype=jnp.int32) * MOE_BLK
    block_e = jnp.minimum(jnp.sum((blk_start[:, None] >= pad_end[None, :]).astype(jnp.int32), axis=1),
                          N_EXPERTS - 1).astype(jnp.int32)
    zero_start = (pad_start + counts).astype(jnp.int32)
    zero_len = (padded - counts).astype(jnp.int32)
    first = (blk_start == pad_start[block_e]).astype(jnp.int32)
    used = counts > 0
    parity = ((jnp.cumsum(used.astype(jnp.int32)) - 1) % 2)[block_e].astype(jnp.int32)
    nearest = lax.cummin(jnp.where(used, experts, N_EXPERTS)[::-1])[::-1]
    next_used = jnp.concatenate([nearest[1:], jnp.full((1,), N_EXPERTS, jnp.int32)])
    nxt = jnp.where(next_used < N_EXPERTS, next_used, -1)[block_e].astype(jnp.int32)
    plan = (block_e, first, parity, nxt, n_valid)
    runs = tcnt[:, 0, :N_EXPERTS].astype(jnp.int32)
    gruns = runs.reshape(N_DISPATCH_STEPS, DISPATCH_TILES, N_EXPERTS)
    gtot = jnp.sum(gruns, axis=1)
    gstart = jnp.cumsum(gtot, axis=1) - gtot
    shift = ((gstart[:, None, :] + jnp.cumsum(gruns, axis=1) - gruns).reshape(N_ROW_BLOCKS, N_EXPERTS)
             - (jnp.cumsum(runs, axis=1) - runs))
    shift_rows = jnp.repeat(jnp.transpose(shift), TM, axis=1)
    hit = rt_all[:TOP_K, None, :] == experts[None, :, None]
    gpos = (rt_all[2 * TOP_K:3 * TOP_K] + jnp.sum(jnp.where(hit, shift_rows[None], 0), axis=1)).reshape(N_SLOTS)
    grun_dst = pad_start[None, :] + jnp.cumsum(gtot, axis=0) - gtot
    cplan = (gpos.astype(jnp.int32), gtot.reshape(-1), grun_dst.reshape(-1).astype(jnp.int32))
    dplan = cplan + (jnp.concatenate([zero_start, zero_len, n_valid]),)
    return plan, dplan, cplan


RUN_PIECE = 32
DISPATCH_TILES = 3
DISPATCH_ROWS = DISPATCH_TILES * TM
N_DISPATCH_STEPS = N_ROW_BLOCKS // DISPATCH_TILES


def _for_run_pieces(n, start_piece):
    whole = n // RUN_PIECE

    def body(j, carry):
        start_piece(j * RUN_PIECE, RUN_PIECE)
        return carry

    lax.fori_loop(0, whole, body, 0)
    o = whole * RUN_PIECE
    bit = RUN_PIECE // 2
    while bit >= 1:
        take = (n & bit) != 0

        @pl.when(take)
        def _(o=o, bit=bit):
            start_piece(o, bit)

        o = o + jnp.where(take, bit, 0)
        bit //= 2


def _dispatch_kernel(run_ref, rdst_ref, zs_ref, h_ref, gpos_hbm, xs_ref, zero_scr, stage, pos_s, sem, zsem, psem):
    i = pl.program_id(0)

    def pos_copy(step, do):
        for kk in range(TOP_K):
            do(pltpu.make_async_copy(gpos_hbm.at[pl.ds(kk * T_ALL + step * DISPATCH_ROWS, DISPATCH_ROWS)],
                                     pos_s.at[pl.ds(kk * DISPATCH_ROWS, DISPATCH_ROWS)], psem))

    @pl.when(i == 0)
    def _():
        pos_copy(0, lambda cp: cp.start())
        zero_scr[...] = jnp.zeros_like(zero_scr)

        def pieces(e, do):
            off = zs_ref[e]
            rem = zs_ref[N_EXPERTS + e]
            bit = MOE_BLK // 2
            while bit >= 1:
                take = (rem & bit) != 0

                @pl.when(take)
                def _(off=off, bit=bit):
                    do(pltpu.make_async_copy(zero_scr.at[pl.ds(0, bit)], xs_ref.at[pl.ds(off, bit)], zsem))

                off = off + jnp.where(take, bit, 0)
                bit //= 2

        def start_e(e, c):
            pieces(e, lambda cp: cp.start())
            return c

        def wait_e(e, c):
            pieces(e, lambda cp: cp.wait())
            return c

        def tail(do):
            def step(b, c):
                do(pltpu.make_async_copy(zero_scr, xs_ref.at[pl.ds(b * MOE_BLK, MOE_BLK)], zsem))
                return c
            return step

        n_valid = zs_ref[2 * N_EXPERTS]
        lax.fori_loop(0, N_EXPERTS, start_e, 0)
        lax.fori_loop(n_valid, N_MOE_BLOCKS, tail(lambda cp: cp.start()), 0)
        lax.fori_loop(0, N_EXPERTS, wait_e, 0)
        lax.fori_loop(n_valid, N_MOE_BLOCKS, tail(lambda cp: cp.wait()), 0)

    slot = i % 2
    pos_copy(i, lambda cp: cp.wait())

    for s in range(2):
        @pl.when(slot == s)
        def _(s=s):
            def place(r, carry):
                row = h_ref[r]
                for kk in range(TOP_K):
                    stage[s, pos_s[kk * DISPATCH_ROWS + r]] = row
                return carry

            lax.fori_loop(0, DISPATCH_ROWS, place, 0, unroll=32)

    @pl.when(i + 1 < N_DISPATCH_STEPS)
    def _():
        pos_copy(i + 1, lambda cp: cp.start())

    def send_run(e, off):
        n = run_ref[i * N_EXPERTS + e]
        dst = rdst_ref[i * N_EXPERTS + e]
        _for_run_pieces(n, lambda o, size: pltpu.make_async_copy(
            stage.at[slot, pl.ds(off + o, size)], xs_ref.at[pl.ds(dst + o, size)], sem.at[slot]).start(
                priority=size.bit_length() % 2))
        return off + n

    lax.fori_loop(0, N_EXPERTS, send_run, 0)

    def drain(s):
        pltpu.make_async_copy(stage.at[s], xs_ref.at[pl.ds(0, DISPATCH_ROWS * TOP_K)], sem.at[s]).wait()

    @pl.when(i >= 1)
    def _():
        drain(1 - slot)

    @pl.when(i == N_DISPATCH_STEPS - 1)
    def _():
        drain(slot)


def _dispatch(dplan, h_all):
    return pl.pallas_call(
        _dispatch_kernel,
        grid_spec=pltpu.PrefetchScalarGridSpec(
            num_scalar_prefetch=3,
            grid=(N_DISPATCH_STEPS,),
            in_specs=[pl.BlockSpec((DISPATCH_ROWS, ROW_TILE, LANES), lambda i, rn, rd, z: (i, 0, 0)),
                      pl.BlockSpec(memory_space=pl.ANY)],
            out_specs=pl.BlockSpec(memory_space=pl.ANY),
            scratch_shapes=[pltpu.VMEM((MOE_BLK, ROW_TILE, LANES), BF16),
                            pltpu.VMEM((2, DISPATCH_ROWS * TOP_K, ROW_TILE, LANES), BF16),
                            pltpu.SMEM((TOP_K * DISPATCH_ROWS,), jnp.int32),
                            pltpu.SemaphoreType.DMA((2,)), pltpu.SemaphoreType.DMA(()),
                            pltpu.SemaphoreType.DMA(())],
        ),
        out_shape=jax.ShapeDtypeStruct((N_SORT_ROWS, ROW_TILE, LANES), BF16),
        compiler_params=_cparams(("arbitrary",)),
        name="moe_dispatch",
    )(*dplan[1:], h_all, dplan[0])


def _experts_kernel(layer, be_ref, first_ref, par_ref, nxt_ref, nv_ref,
                    x_ref, wg_hbm, wu_hbm, wd_hbm, y_ref,
                    wg_s, wu_s, wd_s, wg_f, wu_f, wd_f, wsem):
    i = pl.program_id(0)

    def fetch(e, slot):
        return (pltpu.make_async_copy(wg_hbm.at[layer, e], wg_f.at[slot], wsem.at[slot]),
                pltpu.make_async_copy(wu_hbm.at[layer, e], wu_f.at[slot], wsem.at[slot]),
                pltpu.make_async_copy(wd_hbm.at[layer, e], wd_f.at[slot], wsem.at[slot]))

    @pl.when(i < nv_ref[0])
    def _():
        e = be_ref[i]
        slot = par_ref[i]

        @pl.when(i == 0)
        def _():
            for cp in fetch(e, slot):
                cp.start()

        @pl.when(first_ref[i] == 1)
        def _():
            for cp in fetch(e, slot):
                cp.wait()
            wg_s[...] = wg_f[slot].astype(BF16)
            wu_s[...] = wu_f[slot].astype(BF16)
            wd_s[...] = wd_f[slot].astype(BF16)
            nxt = nxt_ref[i]

            @pl.when(nxt >= 0)
            def _():
                for cp in fetch(nxt, 1 - slot):
                    cp.start()

        xb = x_ref[...].reshape(MOE_BLK, D_MODEL)
        a = jax.nn.silu(_dot(xb, wg_s[...])) * _dot(xb, wu_s[...])
        y_ref[...] = _dot(a.astype(BF16), wd_s[...]).reshape(y_ref.shape)

    @pl.when(i >= nv_ref[0])
    def _():
        y_ref[...] = jnp.zeros(y_ref.shape, y_ref.dtype)


def _experts(block_e, first, parity, nxt, n_valid, xs, w_gate, w_up, w_down, layer):
    def blk(i, be, fi, pa, nx, nv):
        return (jnp.maximum(jnp.minimum(i, nv[0] - 1), 0), 0, 0)

    anyspec = pl.BlockSpec(memory_space=pl.ANY)
    return pl.pallas_call(
        functools.partial(_experts_kernel, layer),
        grid_spec=pltpu.PrefetchScalarGridSpec(
            num_scalar_prefetch=5,
            grid=(N_MOE_BLOCKS,),
            in_specs=[pl.BlockSpec((MOE_BLK, ROW_TILE, LANES), blk), anyspec, anyspec, anyspec],
            out_specs=pl.BlockSpec((MOE_BLK, ROW_TILE, LANES), lambda i, be, fi, pa, nx, nv: (i, 0, 0)),
            scratch_shapes=[pltpu.VMEM((D_MODEL, D_EXPERT), BF16), pltpu.VMEM((D_MODEL, D_EXPERT), BF16),
                            pltpu.VMEM((D_EXPERT, D_MODEL), BF16),
                            pltpu.VMEM((2, D_MODEL, D_EXPERT), F32), pltpu.VMEM((2, D_MODEL, D_EXPERT), F32),
                            pltpu.VMEM((2, D_EXPERT, D_MODEL), F32), pltpu.SemaphoreType.DMA((2,))],
        ),
        out_shape=jax.ShapeDtypeStruct((N_SORT_ROWS, ROW_TILE, LANES), F32),
        compiler_params=_cparams(("arbitrary",)),
        name="moe_experts",
    )(block_e, first, parity, nxt, n_valid, xs, w_gate, w_up, w_down)


def _gather_rows(lpos_ref, run_ref, rdst_ref, ys_ref, ystage, ybufs, sem, i):
    def fetch(group, buf):
        def fetch_run(e, off):
            n = run_ref[group * N_EXPERTS + e]
            src = rdst_ref[group * N_EXPERTS + e]
            _for_run_pieces(n, lambda o, size: pltpu.make_async_copy(
                ys_ref.at[pl.ds(src + o, size)], ystage.at[buf, pl.ds(off + o, size)], sem.at[buf]).start(
                    priority=size.bit_length() % 2))
            return off + n

        lax.fori_loop(0, N_EXPERTS, fetch_run, 0)

    def wait(buf):
        pltpu.make_async_copy(ys_ref.at[pl.ds(0, DISPATCH_ROWS * TOP_K)], ystage.at[buf], sem.at[buf]).wait()

    cur = i % 2
    group = i // DISPATCH_TILES
    phase = i % DISPATCH_TILES
    last_of_group = phase == DISPATCH_TILES - 1

    @pl.when(i == 0)
    def _():
        fetch(0, 0)
        wait(0)

        def unplace(r, carry):
            for kk in range(TOP_K):
                ybufs[0][kk, r] = ystage[0, lpos_ref[kk * T_ALL + r]]
            return carry

        lax.fori_loop(0, TM, unplace, 0, unroll=8)
        fetch(1, 1)

    @pl.when(last_of_group & (group + 1 < N_DISPATCH_STEPS))
    def _():
        wait((group + 1) % 2)

    @pl.when(last_of_group & (group + 2 < N_DISPATCH_STEPS))
    def _():
        fetch(group + 2, group % 2)

    def pieces(compute, store):
        nxt = jnp.minimum(i + 1, N_ROW_BLOCKS - 1)
        nslot = (nxt // DISPATCH_TILES) % 2

        def variant(par):
            ycur, ynext = ybufs[par], ybufs[1 - par]

            def piece(j, carry):
                rows = pl.ds(pl.multiple_of(j * COMBINE_ROWS, COMBINE_ROWS), COMBINE_ROWS)
                out = compute(rows, ycur[0, rows].reshape(COMBINE_ROWS, D_MODEL),
                              ycur[1, rows].reshape(COMBINE_ROWS, D_MODEL))
                base = nxt * TM + j * COMBINE_ROWS
                for r in range(COMBINE_ROWS):
                    for kk in range(TOP_K):
                        ynext[kk, j * COMBINE_ROWS + r] = ystage[nslot, lpos_ref[kk * T_ALL + base + r]]
                store(rows, out)
                return carry

            lax.fori_loop(0, TM // COMBINE_ROWS, piece, 0)

        for par in range(2):
            @pl.when(cur == par)
            def _(par=par):
                variant(par)

    return pieces


COMBINE_ROWS = 64


def _combined(x_ref, rg_ref, rows, y0, y1):
    rg = rg_ref[rows, :]
    return x_ref[rows, :] + rg[:, 0:1] * y0 + rg[:, 1:2] * y1


_COMBINE_SCRATCH = [pltpu.VMEM((2, DISPATCH_ROWS * TOP_K, ROW_TILE, LANES), F32),
                    pltpu.VMEM((TOP_K, TM, ROW_TILE, LANES), F32), pltpu.VMEM((TOP_K, TM, ROW_TILE, LANES), F32),
                    pltpu.SemaphoreType.DMA((2,))]


def _combine_kernel(lpos_ref, run_ref, rdst_ref, x_ref, rg_ref, ys_ref, o_ref, ystage, ybuf0, ybuf1, sem):
    pieces = _gather_rows(lpos_ref, run_ref, rdst_ref, ys_ref, ystage, (ybuf0, ybuf1), sem, pl.program_id(0))

    def store(rows, out):
        o_ref[rows, :] = out

    pieces(functools.partial(_combined, x_ref, rg_ref), store)


def _combine(cplan, x_all, rg_all, ys):
    return pl.pallas_call(
        _combine_kernel,
        grid_spec=pltpu.PrefetchScalarGridSpec(
            num_scalar_prefetch=3,
            grid=(N_ROW_BLOCKS,),
            in_specs=[pl.BlockSpec((TM, D_MODEL), lambda i, a, b, c: (i, 0)),
                      pl.BlockSpec((TM, LANES), lambda i, a, b, c: (i, 0)),
                      pl.BlockSpec(memory_space=pl.ANY)],
            out_specs=pl.BlockSpec((TM, D_MODEL), lambda i, a, b, c: (i, 0)),
            scratch_shapes=_COMBINE_SCRATCH,
        ),
        out_shape=jax.ShapeDtypeStruct((T_ALL, D_MODEL), F32),
        compiler_params=_cparams(("arbitrary",)),
        name="moe_combine",
    )(*cplan, x_all, rg_all, ys)


def _final_kernel(lpos_ref, run_ref, rdst_ref, x_ref, rg_ref, ys_ref, nfin_ref, op_ref, os_ref,
                  ystage, ybuf0, ybuf1, sem):
    i = pl.program_id(0)
    pieces = _gather_rows(lpos_ref, run_ref, rdst_ref, ys_ref, ystage, (ybuf0, ybuf1), sem, i)

    def compute(rows, y0, y1):
        return _rms(_combined(x_ref, rg_ref, rows, y0, y1), nfin_ref[...])

    def store(rows, y):
        @pl.when(i < N_PROMPT_BLOCKS)
        def _():
            op_ref[rows, :] = y

        @pl.when(i >= N_PROMPT_BLOCKS)
        def _():
            os_ref[rows, :] = y

    pieces(compute, store)


def _final(cplan, x_all, rg_all, ys, nfin):
    return pl.pallas_call(
        _final_kernel,
        grid_spec=pltpu.PrefetchScalarGridSpec(
            num_scalar_prefetch=3,
            grid=(N_ROW_BLOCKS,),
            in_specs=[pl.BlockSpec((TM, D_MODEL), lambda i, a, b, c: (i, 0)),
                      pl.BlockSpec((TM, LANES), lambda i, a, b, c: (i, 0)),
                      pl.BlockSpec(memory_space=pl.ANY),
                      pl.BlockSpec((1, D_MODEL), lambda i, a, b, c: (0, 0))],
            out_specs=[pl.BlockSpec((TM, D_MODEL), lambda i, a, b, c: (jnp.minimum(i, N_PROMPT_BLOCKS - 1), 0)),
                       pl.BlockSpec((TM, D_MODEL), lambda i, a, b, c: (0, 0))],
            scratch_shapes=_COMBINE_SCRATCH,
        ),
        out_shape=[jax.ShapeDtypeStruct((T_PROMPT, D_MODEL), F32), jax.ShapeDtypeStruct((T_SAMPLE, D_MODEL), F32)],
        compiler_params=_cparams(("arbitrary",)),
        name="moe_combine_final",
    )(*cplan, x_all, rg_all, ys, nfin)


def _moe(h_all, rt_all, cnt, tcnt, w_gate, w_up, w_down, layer):
    plan, dplan, cplan = _moe_metadata(rt_all, cnt, tcnt)
    xs = _dispatch(dplan, h_all)
    ys = _experts(*plan, xs, w_gate, w_up, w_down, layer)
    return cplan, ys


def _pool_project(d_groups, wp_ref, scale):
    outs = [_dot(d_groups[g].astype(BF16), wp_ref[g]) for g in range(len(POOL_SIZES))]
    return jnp.concatenate(outs, axis=1) * scale


def _mix1_prompt_kernel(x_ref, nm_ref, wp_ref, sc_ref, nf_ref, wr_ref, br_ref,
                        x3_ref, h_ref, ri_ref, rg_ref, tc_ref, pl_ref, cnt_ref, ext, tri):
    i = pl.program_id(0)

    @pl.when(i == 0)
    def _():
        cnt_ref[...] = jnp.zeros_like(cnt_ref)
        tri[...] = _earlier_mask(TM)

    x = x_ref[...]
    hp = _rms(x, nm_ref[...])

    @pl.when(i % STEPS_PER_BATCH == 0)
    def _():
        ext[0:POOL_MAX, :] = jnp.zeros((POOL_MAX, D_MODEL), F32)

    ext[POOL_MAX:, :] = hp
    pos = (i % STEPS_PER_BATCH) * TM + lax.broadcasted_iota(jnp.int32, (TM, 1), 0)
    d_groups = []
    for g, w in enumerate(POOL_SIZES):
        cols = slice(g * POOL_GROUP_DIM, (g + 1) * POOL_GROUP_DIM)
        acc = ext[:, cols]
        span = 1
        while span < w:
            acc = acc + pltpu.roll(acc, span, 0)
            span *= 2
        cnt = jnp.minimum(pos + 1, w).astype(F32)
        d_groups.append(acc[POOL_MAX:] / cnt - hp[:, cols])
    tail = hp[TM - POOL_MAX:, :]
    ext[0:POOL_MAX, :] = tail
    pl_ref[...] = tail

    x3 = x + _pool_project(d_groups, wp_ref, sc_ref[...])
    x3_ref[...] = x3
    h, ids, gates = _route(x3, nf_ref[...], wr_ref[...], br_ref[...])
    h_ref[...] = h.reshape(h_ref.shape)
    ri_ref[...] = _rank_pack(ids, cnt_ref, tc_ref, tri)
    rg_ref[...] = gates


def _mix1_prompt(x_all, nm, wp, sc, nf, wr, br):
    row_spec = pl.BlockSpec((TM, D_MODEL), lambda i: (i, 0))
    row3_spec = pl.BlockSpec((TM, ROW_TILE, LANES), lambda i: (i, 0, 0))
    lane_spec = pl.BlockSpec((TM, LANES), lambda i: (i, 0))
    return pl.pallas_call(
        _prompt_steps(_mix1_prompt_kernel, 7),
        grid=(N_ROW_BLOCKS,),
        in_specs=[row_spec, _const_spec((1, D_MODEL)),
                  _const_spec((len(POOL_SIZES), POOL_GROUP_DIM, POOL_GROUP_DIM)), _const_spec((1, D_MODEL)),
                  _const_spec((1, D_MODEL)), _const_spec((D_MODEL, 2 * LANES)), _const_spec((1, LANES))],
        out_specs=[row_spec, row3_spec, pl.BlockSpec((8, TM), lambda i: (0, i)), lane_spec,
                   pl.BlockSpec((None, 1, LANES), lambda i: (i, 0, 0)),
                   pl.BlockSpec((None, POOL_MAX, D_MODEL),
                                lambda i: (jnp.minimum(i // STEPS_PER_BATCH, BATCH - 1), 0, 0)),
                   _const_spec((1, LANES))],
        out_shape=[jax.ShapeDtypeStruct((T_ALL, D_MODEL), F32), jax.ShapeDtypeStruct((T_ALL, ROW_TILE, LANES), BF16),
                   jax.ShapeDtypeStruct((8, T_ALL), jnp.int32), jax.ShapeDtypeStruct((T_ALL, LANES), F32),
                   jax.ShapeDtypeStruct((N_ROW_BLOCKS, 1, LANES), F32),
                   jax.ShapeDtypeStruct((BATCH, POOL_MAX, D_MODEL), F32), jax.ShapeDtypeStruct((1, LANES), F32)],
        scratch_shapes=[pltpu.VMEM((POOL_MAX + TM, D_MODEL), F32), pltpu.VMEM((TM, TM), BF16)],
        compiler_params=_cparams(("arbitrary",)),
        name="mix1_prompt",
    )(x_all, nm, wp, sc, nf, wr, br)


def _mix1_sample_kernel(x_ref, st_ref, nm_ref, wp_ref, sc_ref, nf_ref, wr_ref, br_ref, cnt_in,
                        x3_in, h_in, ri_in, rg_in, tc_in,
                        x3_ref, h_ref, ri_ref, rg_ref, tc_ref, hs_ref, cnt_ref):
    del x3_in, h_in, ri_in, rg_in, tc_in
    x = x_ref[...]
    hs = _rms(x, nm_ref[...])
    hs_ref[...] = hs
    n_ctx = POOL_MAX - 1
    d_groups = []
    for g, w in enumerate(POOL_SIZES):
        cols = slice(g * POOL_GROUP_DIM, (g + 1) * POOL_GROUP_DIM)
        parts = []
        for t in range(DEC_SEQ):
            acc = hs[t * DEC_BATCH:(t + 1) * DEC_BATCH, cols]
            for back in range(1, w):
                src = t - back
                if src >= 0:
                    acc = acc + hs[src * DEC_BATCH:(src + 1) * DEC_BATCH, cols]
                else:
                    acc = acc + st_ref[n_ctx + src, :, cols]
            parts.append(acc / float(w) - hs[t * DEC_BATCH:(t + 1) * DEC_BATCH, cols])
        d_groups.append(jnp.concatenate(parts, axis=0))
    x3 = x + _pool_project(d_groups, wp_ref, sc_ref[...])
    x3_ref[...] = x3
    h, ids, gates = _route(x3, nf_ref[...], wr_ref[...], br_ref[...])
    h_ref[...] = h.reshape(h_ref.shape)
    cnt_ref[...] = cnt_in[...]
    ri_ref[...] = _rank_pack(ids, cnt_ref, tc_ref)
    rg_ref[...] = gates


def _mix1_sample(x_all, state_t, nm, wp, sc, nf, wr, br, cnt, x3_all, h_all, ri_all, rg_all, tc_all):
    sample_rows = pl.BlockSpec((TM, D_MODEL), lambda g: (N_PROMPT_BLOCKS, 0))
    sample_rows3 = pl.BlockSpec((TM, ROW_TILE, LANES), lambda g: (N_PROMPT_BLOCKS, 0, 0))
    sample_lanes = pl.BlockSpec((TM, LANES), lambda g: (N_PROMPT_BLOCKS, 0))
    anyspec = pl.BlockSpec(memory_space=pl.ANY)
    n_in = 9
    return pl.pallas_call(
        _mix1_sample_kernel,
        grid=(1,),
        in_specs=[sample_rows, _const_spec((POOL_MAX - 1, DEC_BATCH, D_MODEL)), _const_spec((1, D_MODEL)),
                  _const_spec((len(POOL_SIZES), POOL_GROUP_DIM, POOL_GROUP_DIM)), _const_spec((1, D_MODEL)),
                  _const_spec((1, D_MODEL)), _const_spec((D_MODEL, 2 * LANES)), _const_spec((1, LANES)),
                  _const_spec((1, LANES)), anyspec, anyspec, anyspec, anyspec, anyspec],
        out_specs=[sample_rows, sample_rows3, pl.BlockSpec((8, TM), lambda g: (0, N_PROMPT_BLOCKS)), sample_lanes,
                   pl.BlockSpec((None, 1, LANES), lambda g: (N_PROMPT_BLOCKS, 0, 0)),
                   _const_spec((T_SAMPLE, D_MODEL)), _const_spec((1, LANES))],
        out_shape=[jax.ShapeDtypeStruct((T_ALL, D_MODEL), F32), jax.ShapeDtypeStruct((T_ALL, ROW_TILE, LANES), BF16),
                   jax.ShapeDtypeStruct((8, T_ALL), jnp.int32), jax.ShapeDtypeStruct((T_ALL, LANES), F32),
                   jax.ShapeDtypeStruct((N_ROW_BLOCKS, 1, LANES), F32),
                   jax.ShapeDtypeStruct((T_SAMPLE, D_MODEL), F32), jax.ShapeDtypeStruct((1, LANES), F32)],
        input_output_aliases={n_in: 0, n_in + 1: 1, n_in + 2: 2, n_in + 3: 3, n_in + 4: 4},
        compiler_params=_cparams(("arbitrary",)),
        name="mix1_sample",
    )(x_all, state_t, nm, wp, sc, nf, wr, br, cnt, x3_all, h_all, ri_all, rg_all, tc_all)


def _router_weights(wg, bg, we, be):
    w = jnp.concatenate([wg, jnp.transpose(we, (1, 0, 2)).reshape(D_MODEL, N_EXPERTS)], axis=1)
    b = jnp.concatenate([bg, be.reshape(N_EXPERTS)])
    pad = LANES - N_GROUPS - N_EXPERTS
    w = jnp.pad(w, ((0, 0), (0, pad)))
    w_hi = w.astype(BF16)
    w_lo = (w - w_hi.astype(F32)).astype(BF16)
    return jnp.concatenate([w_hi, w_lo], axis=1), jnp.pad(b, (0, pad)).reshape(1, LANES)


def kernel(x_prompt, x_sample, cache_k_win, cache_v_win, state_pool, norm_mix, norm_ffn, norm_final, w_in,
           a_ln_g, a_ln_b, a_w_s, a_b_s, b_sinks, rel_bias_table, w_out, c_w_pool, c_scale,
           router_group_w, router_group_b, router_expert_w, router_expert_b, w_gate, w_up, w_down):
    xs_t = jnp.transpose(x_sample, (1, 0, 2)).reshape(T_SAMPLE, D_MODEL)
    xp2 = x_prompt.reshape(T_PROMPT, D_MODEL)
    win =w_in[0].astype(BF16)
    wout = w_out[0].astype(BF16)
    lng = a_ln_g[0].reshape(1, A_WIDTH)
    lnb = a_ln_b[0].reshape(1, A_WIDTH)
    bias_p, bias_sc, bias_sn, wsp = _prep(rel_bias_table, b_sinks[0], a_w_s[0])
    bs_full = jnp.repeat(a_b_s[0].T, A_HEAD_DIM, axis=1)
    w4 = jnp.transpose(a_w_s[0][:, :DEC_SEQ, :DEC_SEQ], (1, 2, 0)).reshape(DEC_SEQ * DEC_SEQ, A_HEADS)
    wcoef = jnp.repeat(w4, A_HEAD_DIM, axis=1)
    bcoef = jnp.pad(jnp.repeat(a_b_s[0][:, :DEC_SEQ].T, A_HEAD_DIM, axis=1), ((0, 8 - DEC_SEQ), (0, 0)))
    ck = jnp.transpose(cache_k_win[0], (0, 2, 3, 1))
    cv = jnp.transpose(cache_v_win[0], (0, 2, 3, 1))
    routers = [_router_weights(router_group_w[l], router_group_b[l], router_expert_w[l], router_expert_b[l])
               for l in range(2)]
    nm = [norm_mix[l].reshape(1, D_MODEL) for l in range(2)]
    nf = [norm_ffn[l].reshape(1, D_MODEL) for l in range(2)]

    x1_all, h_all, ri_all, rg_all, tc_all, k_last, v_last, va_last, cnt0 = _mix0_prompt(
        xp2, nm[0], win, lng, lnb, wsp, bs_full, bias_p, wout, nf[0], *routers[0])
    x1_all, h_all, ri_all, rg_all, tc_all, k_new, v_new, va_s, cnt0 = _mix0_sample(
        xs_t, nm[0], win, lng, lnb, wcoef, bcoef, ck, cv, bias_sc, bias_sn, wout, nf[0], *routers[0], cnt0,
        x1_all, h_all, ri_all, rg_all, tc_all)
    cplan0, ys0 = _moe(h_all, ri_all, cnt0, tc_all, w_gate, w_up, w_down, 0)
    x2_all = _combine(cplan0, x1_all, rg_all, ys0)

    wp = c_w_pool[0].astype(BF16)
    sc = c_scale[0].reshape(1, D_MODEL)
    x3_all, h2_all, ri2_all, rg2_all, tc2_all, pool_tail, cnt1 = _mix1_prompt(
        x2_all, nm[1], wp, sc, nf[1], *routers[1])
    state_t = jnp.transpose(state_pool[0], (1, 0, 2))
    x3_all, h2_all, ri2_all, rg2_all, tc2_all, hs1, cnt1 = _mix1_sample(
        x2_all, state_t, nm[1], wp, sc, nf[1], *routers[1], cnt1, x3_all, h2_all, ri2_all, rg2_all, tc2_all)
    cplan1, ys1 = _moe(h2_all, ri2_all, cnt1, tc2_all, w_gate, w_up, w_down, 1)
    y_p, y_s = _final(cplan1, x3_all, rg2_all, ys1, norm_final.reshape(1, D_MODEL))

    def from_tmajor(a, width):
        return jnp.transpose(a.reshape(DEC_SEQ, DEC_BATCH, width), (1, 0, 2))

    y_prompt = y_p.reshape(BATCH, SEQ, D_MODEL)
    y_sample = from_tmajor(y_s, D_MODEL)
    win_k_p = k_last.reshape(1, BATCH, WINDOW, B_KV_HEADS, B_HEAD_DIM)
    win_v_p = v_last.reshape(1, BATCH, WINDOW, B_KV_HEADS, B_HEAD_DIM)
    win_k_s = jnp.transpose(k_new, (0, 3, 1, 2))[None]
    win_v_s = jnp.transpose(v_new, (0, 3, 1, 2))[None]
    chunk_v_p = va_last.reshape(1, BATCH, CHUNK, A_HEADS, A_HEAD_DIM)
    chunk_v_s = from_tmajor(va_s, A_WIDTH).reshape(1, DEC_BATCH, DEC_SEQ, A_HEADS, A_HEAD_DIM)
    pool_p = pool_tail[:, 1:][None]
    pool_s = jnp.concatenate([state_pool[0][:, DEC_SEQ:], from_tmajor(hs1, D_MODEL)], axis=1)[None]
    return (y_prompt, y_sample, win_k_p, win_v_p, win_k_s, win_v_s, chunk_v_p, chunk_v_s, pool_p, pool_s)
```
